```python
import math
import jax, jax.numpy as jnp
from jax import lax
import numpy as np

D_MODEL = 1024
BATCH = 16
SEQ = 4096
DEPTH = 1

N_META = 16
S5_WIDTH = D_MODEL // 2
S5_GROUP = 16
S5_GROUPS = S5_WIDTH // S5_GROUP
S5_STATE = 64
DT_MIN = 1e-3
DT_MAX = 1e-1
M_HEADS = 4
M_DK = D_MODEL // 8
M_DV = D_MODEL // 4
M_QK_WIDTH = M_HEADS * M_DK
M_V_WIDTH = M_HEADS * M_DV
M_CHUNK = 64
CONV_WIDTH = 4
D_FF = 4 * D_MODEL
ALPHA = (2.0 * DEPTH) ** 0.25
BETA = (8.0 * DEPTH) ** -0.25
LN_EPS = 1e-5
IN_SIZES = (S5_WIDTH, M_QK_WIDTH, M_QK_WIDTH, M_V_WIDTH, M_V_WIDTH, M_HEADS, M_HEADS, D_MODEL, D_MODEL)
IN_WIDTH = sum(IN_SIZES)
F_GATE_OFFSET = S5_WIDTH + 2 * M_QK_WIDTH + 2 * M_V_WIDTH + M_HEADS

kernel_name = "hybrid_s5_mlstm_gated_block"


def _layer_norm(x, g, b):
    xf = x.astype(jnp.float32)
    mu = jnp.mean(xf, axis=-1, keepdims=True)
    var = jnp.mean(jnp.square(xf - mu), axis=-1, keepdims=True)
    y = (xf - mu) * lax.rsqrt(var + LN_EPS)
    return (y * g.astype(jnp.float32) + b.astype(jnp.float32)).astype(x.dtype)


def _head_norm(h, g):
    hf = h.astype(jnp.float32)
    mu = jnp.mean(hf, axis=-1, keepdims=True)
    var = jnp.mean(jnp.square(hf - mu), axis=-1, keepdims=True)
    return (hf - mu) * lax.rsqrt(var + LN_EPS) * g.astype(jnp.float32).reshape(M_HEADS, M_DV)


def _split_columns(p):
    parts, start = [], 0
    for size in IN_SIZES:
        parts.append(p[..., start:start + size])
        start += size
    return parts


def _causal_depthwise_conv(x, w, b):
    c = x.shape[-1]
    y = lax.conv_general_dilated(
        x, w[:, None, :].astype(x.dtype), window_strides=(1,),
        padding=((CONV_WIDTH - 1, 0),), dimension_numbers=("NWC", "WIO", "NWC"),
        feature_group_count=c)
    return y + b


def _linear_recurrence_combine(left, right):
    a_l, b_l = left
    a_r, b_r = right
    return a_l * a_r, a_r * b_l + b_r


def _s5_mixer(u, lam_re, lam_im, log_dt, b_re, b_im, c_re, c_im, d_skip):
    f32 = jnp.float32
    bsz, length, _ = u.shape
    uf = u.astype(f32).reshape(bsz, length, S5_GROUPS, S5_GROUP)
    lam = lax.complex(lam_re.astype(f32), lam_im.astype(f32))
    dt = jnp.exp(log_dt.astype(f32))[:, None]
    lam_bar = jnp.exp(lam * dt)
    b_mat = lax.complex(b_re.astype(f32), b_im.astype(f32))
    b_bar = ((lam_bar - 1.0) / lam)[..., None] * b_mat
    bu = jnp.einsum("gph,blgh->blgp", b_bar, uf.astype(jnp.complex64))
    a = jnp.broadcast_to(lam_bar, (1, length, S5_GROUPS, S5_STATE))
    _, state = lax.associative_scan(_linear_recurrence_combine, (a, bu), axis=1)
    c_mat = lax.complex(c_re.astype(f32), c_im.astype(f32))
    y = jnp.real(jnp.einsum("ghp,blgp->blgh", c_mat, state))
    y = y + d_skip.astype(f32).reshape(S5_GROUPS, S5_GROUP) * uf
    return y.reshape(bsz, length, S5_WIDTH)


def _mlstm_mixer(q, k, v, i_pre, f_pre):
    f32 = jnp.float32
    bsz, length = q.shape[:2]
    n_pad = M_CHUNK - N_META
    n_chunks = (length + n_pad) // M_CHUNK

    def to_chunks(t, fill):
        t = t.astype(f32)
        t = jnp.pad(t, ((0, 0), (n_pad, 0)) + ((0, 0),) * (t.ndim - 2), constant_values=fill)
        t = t.reshape((bsz, n_chunks, M_CHUNK) + t.shape[2:])
        return jnp.moveaxis(t, (1, 3), (0, 2))

    qc = to_chunks(q, 0.0)
    kc = to_chunks(k * (M_DK ** -0.5), 0.0)
    vc = to_chunks(v, 0.0)
    log_i = to_chunks(i_pre, -jnp.inf)
    log_f = to_chunks(jax.nn.log_sigmoid(f_pre.astype(f32)), 0.0)
    causal = jnp.tril(jnp.ones((M_CHUNK, M_CHUNK), dtype=bool))

    def chunk_step(carry, inp):
        c_st, n_st, m_st = carry
        q_c, k_c, v_c, li_c, lf_c = inp
        b = jnp.cumsum(lf_c, axis=-1)
        d_mat = jnp.where(causal, b[..., :, None] - b[..., None, :] + li_c[..., None, :], -jnp.inf)
        m_inter = b + m_st[..., None]
        m_row = jnp.maximum(m_inter, jnp.max(d_mat, axis=-1))
        w_intra = jnp.exp(d_mat - m_row[..., None])
        w_inter = jnp.exp(m_inter - m_row)
        s = jnp.einsum("bhsd,bhjd->bhsj", q_c, k_c) * w_intra
        num = (jnp.einsum("bhsj,bhje->bhse", s, v_c)
               + w_inter[..., None] * jnp.einsum("bhsd,bhde->bhse", q_c, c_st))
        den = jnp.sum(s, axis=-1) + w_inter * jnp.einsum("bhsd,bhd->bhs", q_c, n_st)
        h_c = num / jnp.maximum(jnp.abs(den), jnp.exp(-m_row))[..., None]
        b_last = b[..., -1]
        g_log = b_last[..., None] - b + li_c
        m_new = jnp.maximum(b_last + m_st, jnp.max(g_log, axis=-1))
        w_k = jnp.exp(g_log - m_new[..., None])
        decay = jnp.exp(b_last + m_st - m_new)
        c_new = decay[..., None, None] * c_st + jnp.einsum("bhj,bhjd,bhje->bhde", w_k, k_c, v_c)
        n_new = decay[..., None] * n_st + jnp.einsum("bhj,bhjd->bhd", w_k, k_c)
        return (c_new, n_new, m_new), h_c

    init = (jnp.zeros((bsz, M_HEADS, M_DK, M_DV), f32),
            jnp.zeros((bsz, M_HEADS, M_DK), f32),
            jnp.zeros((bsz, M_HEADS), f32))
    _, h = lax.scan(chunk_step, init, (qc, kc, vc, log_i, log_f))
    h = jnp.moveaxis(h, (0, 2), (1, 3)).reshape(bsz, n_chunks * M_CHUNK, M_HEADS, M_DV)
    return h[:, n_pad:]


def _fwd_setup_inputs(seed: int = 0) -> dict:
    key = jax.random.key(seed)
    ks = jax.random.split(key, 27)
    f32 = jnp.float32

    def nrm(k, shape, scale):
        return scale * jax.random.normal(k, shape, f32)

    b_in = nrm(ks[5], (DEPTH, IN_WIDTH), 0.02)
    b_in = b_in.at[:, F_GATE_OFFSET:F_GATE_OFFSET + M_HEADS].add(jnp.linspace(3.0, 6.0, M_HEADS))
    s5_lambda_re = -0.5 + nrm(ks[8], (DEPTH, S5_GROUPS, S5_STATE), 0.01)
    s5_lambda_im = jnp.pi * jnp.arange(S5_STATE, dtype=f32) + nrm(ks[9], (DEPTH, S5_GROUPS, S5_STATE), 0.01)
    s5_log_dt = jax.random.uniform(ks[10], (DEPTH, S5_GROUPS), f32, math.log(DT_MIN), math.log(DT_MAX))
    return {
        "x": nrm(ks[0], (BATCH, SEQ, D_MODEL), 1.0),
        "meta_tokens": nrm(ks[1], (N_META, D_MODEL), 1.0),
        "ln0_g": 1.0 + nrm(ks[2], (D_MODEL,), 0.05),
        "ln0_b": nrm(ks[3], (D_MODEL,), 0.02),
        "w_in": nrm(ks[4], (DEPTH, D_MODEL, IN_WIDTH), D_MODEL ** -0.5),
        "b_in": b_in,
        "qk_conv_w": nrm(ks[6], (DEPTH, CONV_WIDTH, 2 * M_QK_WIDTH), CONV_WIDTH ** -0.5),
        "qk_conv_b": nrm(ks[7], (DEPTH, 2 * M_QK_WIDTH), 0.02),
        "s5_lambda_re": s5_lambda_re,
        "s5_lambda_im": s5_lambda_im,
        "s5_log_dt": s5_log_dt,
        "s5_b_re": nrm(ks[11], (DEPTH, S5_GROUPS, S5_STATE, S5_GROUP), (2.0 * S5_GROUP) ** -0.5),
        "s5_b_im": nrm(ks[12], (DEPTH, S5_GROUPS, S5_STATE, S5_GROUP), (2.0 * S5_GROUP) ** -0.5),
        "s5_c_re": nrm(ks[13], (DEPTH, S5_GROUPS, S5_GROUP, S5_STATE), S5_STATE ** -0.5),
        "s5_c_im": nrm(ks[14], (DEPTH, S5_GROUPS, S5_GROUP, S5_STATE), S5_STATE ** -0.5),
        "s5_d": nrm(ks[15], (DEPTH, S5_WIDTH), 1.0),
        "s5_w_glu": nrm(ks[16], (DEPTH, S5_WIDTH, 2 * D_MODEL), S5_WIDTH ** -0.5),
        "m_norm_g": 1.0 + nrm(ks[17], (DEPTH, M_V_WIDTH), 0.05),
        "m_w_out": nrm(ks[18], (DEPTH, M_V_WIDTH, D_MODEL), M_V_WIDTH ** -0.5),
        "w_o": nrm(ks[19], (DEPTH, D_MODEL, D_MODEL), BETA * D_MODEL ** -0.5),
        "ln1_g": 1.0 + nrm(ks[20], (DEPTH, D_MODEL), 0.05),
        "ln1_b": nrm(ks[21], (DEPTH, D_MODEL), 0.02),
        "w_up": nrm(ks[22], (DEPTH, D_MODEL, D_FF), D_MODEL ** -0.5),
        "b_up": nrm(ks[23], (DEPTH, D_FF), 0.02),
        "w_down": nrm(ks[24], (DEPTH, D_FF, D_MODEL), BETA * D_FF ** -0.5),
        "ln2_g": 1.0 + nrm(ks[25], (DEPTH, D_MODEL), 0.05),
        "ln2_b": nrm(ks[26], (DEPTH, D_MODEL), 0.02),
    }


def _fwd_reference(x, meta_tokens, ln0_g, ln0_b, w_in, b_in, qk_conv_w, qk_conv_b,
              s5_lambda_re, s5_lambda_im, s5_log_dt, s5_b_re, s5_b_im, s5_c_re, s5_c_im,
              s5_d, s5_w_glu, m_norm_g, m_w_out, w_o, ln1_g, ln1_b, w_up, b_up, w_down,
              ln2_g, ln2_b):
    bsz = x.shape[0]
    meta = jnp.broadcast_to(meta_tokens[None].astype(x.dtype), (bsz, N_META, D_MODEL))
    h = _layer_norm(jnp.concatenate([meta, x], axis=1), ln0_g, ln0_b)
    length = h.shape[1]
    for layer in range(DEPTH):
        p = h @ w_in[layer] + b_in[layer]
        u_s5, q, k, v, o_pre, i_pre, f_pre, g_s5, g_m = _split_columns(p)
        y_s5 = _s5_mixer(u_s5, s5_lambda_re[layer], s5_lambda_im[layer], s5_log_dt[layer],
                         s5_b_re[layer], s5_b_im[layer], s5_c_re[layer], s5_c_im[layer], s5_d[layer])
        z = jax.nn.gelu(y_s5).astype(h.dtype) @ s5_w_glu[layer]
        y_s5 = z[..., :D_MODEL] * jax.nn.sigmoid(z[..., D_MODEL:])
        qk = jax.nn.silu(_causal_depthwise_conv(jnp.concatenate([q, k], axis=-1),
                                                qk_conv_w[layer], qk_conv_b[layer]))
        q_h = qk[..., :M_QK_WIDTH].reshape(bsz, length, M_HEADS, M_DK)
        k_h = qk[..., M_QK_WIDTH:].reshape(bsz, length, M_HEADS, M_DK)
        v_h = v.reshape(bsz, length, M_HEADS, M_DV)
        hm = _mlstm_mixer(q_h, k_h, v_h, i_pre, f_pre)
        hm = _head_norm(hm, m_norm_g[layer]).reshape(bsz, length, M_V_WIDTH).astype(h.dtype)
        y_m = (jax.nn.sigmoid(o_pre) * hm) @ m_w_out[layer]
        mix = jax.nn.sigmoid(g_s5) * y_s5 + jax.nn.sigmoid(g_m) * y_m
        h = _layer_norm(ALPHA * h + mix @ w_o[layer], ln1_g[layer], ln1_b[layer])
        ff = jnp.square(jax.nn.relu(h @ w_up[layer] + b_up[layer])) @ w_down[layer]
        h = _layer_norm(ALPHA * h + ff, ln2_g[layer], ln2_b[layer])
    return h[:, N_META:]


import jax as _jax
import jax.numpy as _jnp

TWIN_FORMAT = 'train_step'
FWD_PARAMS = ['x', 'meta_tokens', 'ln0_g', 'ln0_b', 'w_in', 'b_in', 'qk_conv_w', 'qk_conv_b', 's5_lambda_re', 's5_lambda_im', 's5_log_dt', 's5_b_re', 's5_b_im', 's5_c_re', 's5_c_im', 's5_d', 's5_w_glu', 'm_norm_g', 'm_w_out', 'w_o', 'ln1_g', 'ln1_b', 'w_up', 'b_up', 'w_down', 'ln2_g', 'ln2_b']
TWIN_WEIGHTS = ['meta_tokens', 'ln0_g', 'ln0_b', 'w_in', 'b_in', 'qk_conv_w', 'qk_conv_b', 's5_lambda_re', 's5_lambda_im', 's5_log_dt', 's5_b_re', 's5_b_im', 's5_c_re', 's5_c_im', 's5_d', 's5_w_glu', 'm_norm_g', 'm_w_out', 'w_o', 'ln1_g', 'ln1_b', 'w_up', 'b_up', 'w_down', 'ln2_g', 'ln2_b']
TWIN_DIFF_INPUT = 'x'
TWIN_INPUTS = ['x', 'meta_tokens', 'ln0_g', 'ln0_b', 'w_in', 'b_in', 'qk_conv_w', 'qk_conv_b', 's5_lambda_re', 's5_lambda_im', 's5_log_dt', 's5_b_re', 's5_b_im', 's5_c_re', 's5_c_im', 's5_d', 's5_w_glu', 'm_norm_g', 'm_w_out', 'w_o', 'ln1_g', 'ln1_b', 'w_up', 'b_up', 'w_down', 'ln2_g', 'ln2_b', 'loss_target', 'm_meta_tokens', 'm_ln0_g', 'm_ln0_b', 'm_w_in', 'm_b_in', 'm_qk_conv_w', 'm_qk_conv_b', 'm_s5_lambda_re', 'm_s5_lambda_im', 'm_s5_log_dt', 'm_s5_b_re', 'm_s5_b_im', 'm_s5_c_re', 'm_s5_c_im', 'm_s5_d', 'm_s5_w_glu', 'm_m_norm_g', 'm_m_w_out', 'm_w_o', 'm_ln1_g', 'm_ln1_b', 'm_w_up', 'm_b_up', 'm_w_down', 'm_ln2_g', 'm_ln2_b', 'v_meta_tokens', 'v_ln0_g', 'v_ln0_b', 'v_w_in', 'v_b_in', 'v_qk_conv_w', 'v_qk_conv_b', 'v_s5_lambda_re', 'v_s5_lambda_im', 'v_s5_log_dt', 'v_s5_b_re', 'v_s5_b_im', 'v_s5_c_re', 'v_s5_c_im', 'v_s5_d', 'v_s5_w_glu', 'v_m_norm_g', 'v_m_w_out', 'v_w_o', 'v_ln1_g', 'v_ln1_b', 'v_w_up', 'v_b_up', 'v_w_down', 'v_ln2_g', 'v_ln2_b']
TWIN_OUTPUTS = ['loss', 'grad_x', 'grad_meta_tokens', 'grad_ln0_g', 'grad_ln0_b', 'grad_w_in', 'grad_b_in', 'grad_qk_conv_w', 'grad_qk_conv_b', 'grad_s5_lambda_re', 'grad_s5_lambda_im', 'grad_s5_log_dt', 'grad_s5_b_re', 'grad_s5_b_im', 'grad_s5_c_re', 'grad_s5_c_im', 'grad_s5_d', 'grad_s5_w_glu', 'grad_m_norm_g', 'grad_m_w_out', 'grad_w_o', 'grad_ln1_g', 'grad_ln1_b', 'grad_w_up', 'grad_b_up', 'grad_w_down', 'grad_ln2_g', 'grad_ln2_b', 'delta_meta_tokens', 'delta_ln0_g', 'delta_ln0_b', 'delta_w_in', 'delta_b_in', 'delta_qk_conv_w', 'delta_qk_conv_b', 'delta_s5_lambda_re', 'delta_s5_lambda_im', 'delta_s5_log_dt', 'delta_s5_b_re', 'delta_s5_b_im', 'delta_s5_c_re', 'delta_s5_c_im', 'delta_s5_d', 'delta_s5_w_glu', 'delta_m_norm_g', 'delta_m_w_out', 'delta_w_o', 'delta_ln1_g', 'delta_ln1_b', 'delta_w_up', 'delta_b_up', 'delta_w_down', 'delta_ln2_g', 'delta_ln2_b', 'new_m_meta_tokens', 'new_m_ln0_g', 'new_m_ln0_b', 'new_m_w_in', 'new_m_b_in', 'new_m_qk_conv_w', 'new_m_qk_conv_b', 'new_m_s5_lambda_re', 'new_m_s5_lambda_im', 'new_m_s5_log_dt', 'new_m_s5_b_re', 'new_m_s5_b_im', 'new_m_s5_c_re', 'new_m_s5_c_im', 'new_m_s5_d', 'new_m_s5_w_glu', 'new_m_m_norm_g', 'new_m_m_w_out', 'new_m_w_o', 'new_m_ln1_g', 'new_m_ln1_b', 'new_m_w_up', 'new_m_b_up', 'new_m_w_down', 'new_m_ln2_g', 'new_m_ln2_b', 'new_v_meta_tokens', 'new_v_ln0_g', 'new_v_ln0_b', 'new_v_w_in', 'new_v_b_in', 'new_v_qk_conv_w', 'new_v_qk_conv_b', 'new_v_s5_lambda_re', 'new_v_s5_lambda_im', 'new_v_s5_log_dt', 'new_v_s5_b_re', 'new_v_s5_b_im', 'new_v_s5_c_re', 'new_v_s5_c_im', 'new_v_s5_d', 'new_v_s5_w_glu', 'new_v_m_norm_g', 'new_v_m_w_out', 'new_v_w_o', 'new_v_ln1_g', 'new_v_ln1_b', 'new_v_w_up', 'new_v_b_up', 'new_v_w_down', 'new_v_ln2_g', 'new_v_ln2_b']
TWIN_LEAF_KINDS = {'loss': 'loss', 'grad_x': 'grad_x', 'grad_meta_tokens': 'grad_w', 'grad_ln0_g': 'grad_w', 'grad_ln0_b': 'grad_w', 'grad_w_in': 'grad_w', 'grad_b_in': 'grad_w', 'grad_qk_conv_w': 'grad_w', 'grad_qk_conv_b': 'grad_w', 'grad_s5_lambda_re': 'grad_w', 'grad_s5_lambda_im': 'grad_w', 'grad_s5_log_dt': 'grad_w', 'grad_s5_b_re': 'grad_w', 'grad_s5_b_im': 'grad_w', 'grad_s5_c_re': 'grad_w', 'grad_s5_c_im': 'grad_w', 'grad_s5_d': 'grad_w', 'grad_s5_w_glu': 'grad_w', 'grad_m_norm_g': 'grad_w', 'grad_m_w_out': 'grad_w', 'grad_w_o': 'grad_w', 'grad_ln1_g': 'grad_w', 'grad_ln1_b': 'grad_w', 'grad_w_up': 'grad_w', 'grad_b_up': 'grad_w', 'grad_w_down': 'grad_w', 'grad_ln2_g': 'grad_w', 'grad_ln2_b': 'grad_w', 'delta_meta_tokens': 'delta_w', 'delta_ln0_g': 'delta_w', 'delta_ln0_b': 'delta_w', 'delta_w_in': 'delta_w', 'delta_b_in': 'delta_w', 'delta_qk_conv_w': 'delta_w', 'delta_qk_conv_b': 'delta_w', 'delta_s5_lambda_re': 'delta_w', 'delta_s5_lambda_im': 'delta_w', 'delta_s5_log_dt': 'delta_w', 'delta_s5_b_re': 'delta_w', 'delta_s5_b_im': 'delta_w', 'delta_s5_c_re': 'delta_w', 'delta_s5_c_im': 'delta_w', 'delta_s5_d': 'delta_w', 'delta_s5_w_glu': 'delta_w', 'delta_m_norm_g': 'delta_w', 'delta_m_w_out': 'delta_w', 'delta_w_o': 'delta_w', 'delta_ln1_g': 'delta_w', 'delta_ln1_b': 'delta_w', 'delta_w_up': 'delta_w', 'delta_b_up': 'delta_w', 'delta_w_down': 'delta_w', 'delta_ln2_g': 'delta_w', 'delta_ln2_b': 'delta_w', 'new_m_meta_tokens': 'new_m', 'new_m_ln0_g': 'new_m', 'new_m_ln0_b': 'new_m', 'new_m_w_in': 'new_m', 'new_m_b_in': 'new_m', 'new_m_qk_conv_w': 'new_m', 'new_m_qk_conv_b': 'new_m', 'new_m_s5_lambda_re': 'new_m', 'new_m_s5_lambda_im': 'new_m', 'new_m_s5_log_dt': 'new_m', 'new_m_s5_b_re': 'new_m', 'new_m_s5_b_im': 'new_m', 'new_m_s5_c_re': 'new_m', 'new_m_s5_c_im': 'new_m', 'new_m_s5_d': 'new_m', 'new_m_s5_w_glu': 'new_m', 'new_m_m_norm_g': 'new_m', 'new_m_m_w_out': 'new_m', 'new_m_w_o': 'new_m', 'new_m_ln1_g': 'new_m', 'new_m_ln1_b': 'new_m', 'new_m_w_up': 'new_m', 'new_m_b_up': 'new_m', 'new_m_w_down': 'new_m', 'new_m_ln2_g': 'new_m', 'new_m_ln2_b': 'new_m', 'new_v_meta_tokens': 'new_v', 'new_v_ln0_g': 'new_v', 'new_v_ln0_b': 'new_v', 'new_v_w_in': 'new_v', 'new_v_b_in': 'new_v', 'new_v_qk_conv_w': 'new_v', 'new_v_qk_conv_b': 'new_v', 'new_v_s5_lambda_re': 'new_v', 'new_v_s5_lambda_im': 'new_v', 'new_v_s5_log_dt': 'new_v', 'new_v_s5_b_re': 'new_v', 'new_v_s5_b_im': 'new_v', 'new_v_s5_c_re': 'new_v', 'new_v_s5_c_im': 'new_v', 'new_v_s5_d': 'new_v', 'new_v_s5_w_glu': 'new_v', 'new_v_m_norm_g': 'new_v', 'new_v_m_w_out': 'new_v', 'new_v_w_o': 'new_v', 'new_v_ln1_g': 'new_v', 'new_v_ln1_b': 'new_v', 'new_v_w_up': 'new_v', 'new_v_b_up': 'new_v', 'new_v_w_down': 'new_v', 'new_v_ln2_g': 'new_v', 'new_v_ln2_b': 'new_v'}


def _forward(args):
    return _fwd_reference(*[args[k] for k in FWD_PARAMS])


def _output_shape():
    out = _jax.eval_shape(lambda: _forward(_fwd_setup_inputs(0)))
    return out.shape, out.dtype

N_MICROBATCH = 1
ADAM_LR = 0.001
ADAM_B1 = 0.9
ADAM_B2 = 0.999
ADAM_EPS = 1e-08
ADAM_WD = 0.01
ADAM_STEP = 10
PER_EXAMPLE_BATCH_AXIS = {'x': 0, 'loss_target': 0}
SHARED_INPUTS = []
_WEIGHT_DTYPES = {'meta_tokens': _jnp.float32, 'ln0_g': _jnp.float32, 'ln0_b': _jnp.float32, 'w_in': _jnp.float32, 'b_in': _jnp.float32, 'qk_conv_w': _jnp.float32, 'qk_conv_b': _jnp.float32, 's5_lambda_re': _jnp.float32, 's5_lambda_im': _jnp.float32, 's5_log_dt': _jnp.float32, 's5_b_re': _jnp.float32, 's5_b_im': _jnp.float32, 's5_c_re': _jnp.float32, 's5_c_im': _jnp.float32, 's5_d': _jnp.float32, 's5_w_glu': _jnp.float32, 'm_norm_g': _jnp.float32, 'm_w_out': _jnp.float32, 'w_o': _jnp.float32, 'ln1_g': _jnp.float32, 'ln1_b': _jnp.float32, 'w_up': _jnp.float32, 'b_up': _jnp.float32, 'w_down': _jnp.float32, 'ln2_g': _jnp.float32, 'ln2_b': _jnp.float32}
MOMENT_SCALE = {'meta_tokens': 3.799121e-03, 'ln0_g': 4.082517e+00, 'ln0_b': 1.624819e+00, 'w_in': 3.326355e-02, 'b_in': 5.165029e-01, 'qk_conv_w': 3.918657e-02, 'qk_conv_b': 3.921863e-02, 's5_lambda_re': 2.973472e-03, 's5_lambda_im': 2.821989e-03, 's5_log_dt': 2.011336e+00, 's5_b_re': 1.636552e-03, 's5_b_im': 1.625707e-03, 's5_c_re': 2.485235e-03, 's5_c_im': 2.438368e-03, 's5_d': 5.973862e-02, 's5_w_glu': 3.052815e-02, 'm_norm_g': 4.323031e-02, 'm_w_out': 4.533721e-02, 'w_o': 9.929637e-02, 'ln1_g': 4.334055e+00, 'ln1_b': 1.133536e+00, 'w_up': 7.763356e-02, 'b_up': 1.969439e-01, 'w_down': 3.980822e-01, 'ln2_g': 6.442747e+01, 'ln2_b': 1.365232e+01}


def _to_microbatches(a, axis):
    t = _jnp.moveaxis(a, axis, 0)
    t = t.reshape((N_MICROBATCH, t.shape[0] // N_MICROBATCH) + t.shape[1:])
    return _jnp.moveaxis(t, 1, axis + 1)


def setup_inputs(seed: int = 0) -> dict:
    inp = _fwd_setup_inputs(seed)
    key = _jax.random.fold_in(_jax.random.key(seed), 7919)
    shape, _ = _output_shape()
    out = dict(inp)
    out["loss_target"] = _jax.random.normal(_jax.random.fold_in(key, 0), shape, _jnp.float32)
    for i, name in enumerate(TWIN_WEIGHTS):
        w = inp[name].astype(_jnp.float32)
        if MOMENT_SCALE is None:
            s = _jnp.sqrt(_jnp.mean(_jnp.square(w)) + 1e-30)
        else:
            s = MOMENT_SCALE[name]
        km, kv = _jax.random.split(_jax.random.fold_in(key, i + 1))
        out[name] = w
        out["m_" + name] = s * _jax.random.normal(km, w.shape, _jnp.float32)
        out["v_" + name] = (s * s) * _jax.random.uniform(kv, w.shape, _jnp.float32, 0.5, 1.5)
    if N_MICROBATCH > 1:
        for name, axis in PER_EXAMPLE_BATCH_AXIS.items():
            out[name] = _to_microbatches(out[name], axis)
    return {'x': out['x'], 'meta_tokens': out['meta_tokens'], 'ln0_g': out['ln0_g'], 'ln0_b': out['ln0_b'], 'w_in': out['w_in'], 'b_in': out['b_in'], 'qk_conv_w': out['qk_conv_w'], 'qk_conv_b': out['qk_conv_b'], 's5_lambda_re': out['s5_lambda_re'], 's5_lambda_im': out['s5_lambda_im'], 's5_log_dt': out['s5_log_dt'], 's5_b_re': out['s5_b_re'], 's5_b_im': out['s5_b_im'], 's5_c_re': out['s5_c_re'], 's5_c_im': out['s5_c_im'], 's5_d': out['s5_d'], 's5_w_glu': out['s5_w_glu'], 'm_norm_g': out['m_norm_g'], 'm_w_out': out['m_w_out'], 'w_o': out['w_o'], 'ln1_g': out['ln1_g'], 'ln1_b': out['ln1_b'], 'w_up': out['w_up'], 'b_up': out['b_up'], 'w_down': out['w_down'], 'ln2_g': out['ln2_g'], 'ln2_b': out['ln2_b'], 'loss_target': out['loss_target'], 'm_meta_tokens': out['m_meta_tokens'], 'm_ln0_g': out['m_ln0_g'], 'm_ln0_b': out['m_ln0_b'], 'm_w_in': out['m_w_in'], 'm_b_in': out['m_b_in'], 'm_qk_conv_w': out['m_qk_conv_w'], 'm_qk_conv_b': out['m_qk_conv_b'], 'm_s5_lambda_re': out['m_s5_lambda_re'], 'm_s5_lambda_im': out['m_s5_lambda_im'], 'm_s5_log_dt': out['m_s5_log_dt'], 'm_s5_b_re': out['m_s5_b_re'], 'm_s5_b_im': out['m_s5_b_im'], 'm_s5_c_re': out['m_s5_c_re'], 'm_s5_c_im': out['m_s5_c_im'], 'm_s5_d': out['m_s5_d'], 'm_s5_w_glu': out['m_s5_w_glu'], 'm_m_norm_g': out['m_m_norm_g'], 'm_m_w_out': out['m_m_w_out'], 'm_w_o': out['m_w_o'], 'm_ln1_g': out['m_ln1_g'], 'm_ln1_b': out['m_ln1_b'], 'm_w_up': out['m_w_up'], 'm_b_up': out['m_b_up'], 'm_w_down': out['m_w_down'], 'm_ln2_g': out['m_ln2_g'], 'm_ln2_b': out['m_ln2_b'], 'v_meta_tokens': out['v_meta_tokens'], 'v_ln0_g': out['v_ln0_g'], 'v_ln0_b': out['v_ln0_b'], 'v_w_in': out['v_w_in'], 'v_b_in': out['v_b_in'], 'v_qk_conv_w': out['v_qk_conv_w'], 'v_qk_conv_b': out['v_qk_conv_b'], 'v_s5_lambda_re': out['v_s5_lambda_re'], 'v_s5_lambda_im': out['v_s5_lambda_im'], 'v_s5_log_dt': out['v_s5_log_dt'], 'v_s5_b_re': out['v_s5_b_re'], 'v_s5_b_im': out['v_s5_b_im'], 'v_s5_c_re': out['v_s5_c_re'], 'v_s5_c_im': out['v_s5_c_im'], 'v_s5_d': out['v_s5_d'], 'v_s5_w_glu': out['v_s5_w_glu'], 'v_m_norm_g': out['v_m_norm_g'], 'v_m_w_out': out['v_m_w_out'], 'v_w_o': out['v_w_o'], 'v_ln1_g': out['v_ln1_g'], 'v_ln1_b': out['v_ln1_b'], 'v_w_up': out['v_w_up'], 'v_b_up': out['v_b_up'], 'v_w_down': out['v_w_down'], 'v_ln2_g': out['v_ln2_g'], 'v_ln2_b': out['v_ln2_b']}


def _loss(weights, diff, rest, loss_target):
    with _jax.named_scope("forward"):
        args = {**rest, TWIN_DIFF_INPUT: diff, **{k: w.astype(_WEIGHT_DTYPES[k]) for k, w in weights.items()}}
        y = _forward(args)
    with _jax.named_scope("loss_head"):
        err = _jnp.square(y.astype(_jnp.float32) - loss_target)
        return 0.5 * _jnp.sum(_jnp.mean(err, axis=-1)) if err.ndim else 0.5 * err


def _adamw(w, g, m, v):
    m = ADAM_B1 * m + (1.0 - ADAM_B1) * g
    v = ADAM_B2 * v + (1.0 - ADAM_B2) * _jnp.square(g)
    m_hat = m / (1.0 - ADAM_B1 ** ADAM_STEP)
    v_hat = v / (1.0 - ADAM_B2 ** ADAM_STEP)
    delta = -ADAM_LR * (m_hat / (_jnp.sqrt(v_hat) + ADAM_EPS) + ADAM_WD * w)
    return delta, m, v


def reference(x, meta_tokens, ln0_g, ln0_b, w_in, b_in, qk_conv_w, qk_conv_b, s5_lambda_re, s5_lambda_im, s5_log_dt, s5_b_re, s5_b_im, s5_c_re, s5_c_im, s5_d, s5_w_glu, m_norm_g, m_w_out, w_o, ln1_g, ln1_b, w_up, b_up, w_down, ln2_g, ln2_b, loss_target, m_meta_tokens, m_ln0_g, m_ln0_b, m_w_in, m_b_in, m_qk_conv_w, m_qk_conv_b, m_s5_lambda_re, m_s5_lambda_im, m_s5_log_dt, m_s5_b_re, m_s5_b_im, m_s5_c_re, m_s5_c_im, m_s5_d, m_s5_w_glu, m_m_norm_g, m_m_w_out, m_w_o, m_ln1_g, m_ln1_b, m_w_up, m_b_up, m_w_down, m_ln2_g, m_ln2_b, v_meta_tokens, v_ln0_g, v_ln0_b, v_w_in, v_b_in, v_qk_conv_w, v_qk_conv_b, v_s5_lambda_re, v_s5_lambda_im, v_s5_log_dt, v_s5_b_re, v_s5_b_im, v_s5_c_re, v_s5_c_im, v_s5_d, v_s5_w_glu, v_m_norm_g, v_m_w_out, v_w_o, v_ln1_g, v_ln1_b, v_w_up, v_b_up, v_w_down, v_ln2_g, v_ln2_b):
    given = dict(x=x, meta_tokens=meta_tokens, ln0_g=ln0_g, ln0_b=ln0_b, w_in=w_in, b_in=b_in, qk_conv_w=qk_conv_w, qk_conv_b=qk_conv_b, s5_lambda_re=s5_lambda_re, s5_lambda_im=s5_lambda_im, s5_log_dt=s5_log_dt, s5_b_re=s5_b_re, s5_b_im=s5_b_im, s5_c_re=s5_c_re, s5_c_im=s5_c_im, s5_d=s5_d, s5_w_glu=s5_w_glu, m_norm_g=m_norm_g, m_w_out=m_w_out, w_o=w_o, ln1_g=ln1_g, ln1_b=ln1_b, w_up=w_up, b_up=b_up, w_down=w_down, ln2_g=ln2_g, ln2_b=ln2_b, loss_target=loss_target, m_meta_tokens=m_meta_tokens, m_ln0_g=m_ln0_g, m_ln0_b=m_ln0_b, m_w_in=m_w_in, m_b_in=m_b_in, m_qk_conv_w=m_qk_conv_w, m_qk_conv_b=m_qk_conv_b, m_s5_lambda_re=m_s5_lambda_re, m_s5_lambda_im=m_s5_lambda_im, m_s5_log_dt=m_s5_log_dt, m_s5_b_re=m_s5_b_re, m_s5_b_im=m_s5_b_im, m_s5_c_re=m_s5_c_re, m_s5_c_im=m_s5_c_im, m_s5_d=m_s5_d, m_s5_w_glu=m_s5_w_glu, m_m_norm_g=m_m_norm_g, m_m_w_out=m_m_w_out, m_w_o=m_w_o, m_ln1_g=m_ln1_g, m_ln1_b=m_ln1_b, m_w_up=m_w_up, m_b_up=m_b_up, m_w_down=m_w_down, m_ln2_g=m_ln2_g, m_ln2_b=m_ln2_b, v_meta_tokens=v_meta_tokens, v_ln0_g=v_ln0_g, v_ln0_b=v_ln0_b, v_w_in=v_w_in, v_b_in=v_b_in, v_qk_conv_w=v_qk_conv_w, v_qk_conv_b=v_qk_conv_b, v_s5_lambda_re=v_s5_lambda_re, v_s5_lambda_im=v_s5_lambda_im, v_s5_log_dt=v_s5_log_dt, v_s5_b_re=v_s5_b_re, v_s5_b_im=v_s5_b_im, v_s5_c_re=v_s5_c_re, v_s5_c_im=v_s5_c_im, v_s5_d=v_s5_d, v_s5_w_glu=v_s5_w_glu, v_m_norm_g=v_m_norm_g, v_m_w_out=v_m_w_out, v_w_o=v_w_o, v_ln1_g=v_ln1_g, v_ln1_b=v_ln1_b, v_w_up=v_w_up, v_b_up=v_b_up, v_w_down=v_w_down, v_ln2_g=v_ln2_g, v_ln2_b=v_ln2_b)
    weights = {n: given[n] for n in TWIN_WEIGHTS}
    shared = {n: given[n] for n in SHARED_INPUTS}
    per_example = {n: given[n] for n in ['x']}
    grad_fn = _jax.value_and_grad(_loss, argnums=(0, 1))

    def one_microbatch(ex, loss_target):
        ex = dict(ex)
        diff = ex.pop(TWIN_DIFF_INPUT)
        return grad_fn(weights, diff, {**shared, **ex}, loss_target)

    if N_MICROBATCH == 1:
        loss, (grad_w, grad_x) = one_microbatch(per_example, given["loss_target"])
    else:
        def body(carry, xs):
            loss_sum, grad_sum = carry
            l_k, (gw_k, gx_k) = one_microbatch(xs[0], xs[1])
            with _jax.named_scope("update"):
                return (loss_sum + l_k, _jax.tree.map(_jnp.add, grad_sum, gw_k)), gx_k

        init = (_jnp.zeros((), _jnp.float32), _jax.tree.map(_jnp.zeros_like, weights))
        (loss, grad_w), grad_x = _jax.lax.scan(body, init, (per_example, given["loss_target"]))
    with _jax.named_scope("update"):
        delta_w, new_m, new_v = {}, {}, {}
        for n in TWIN_WEIGHTS:
            delta_w[n], new_m[n], new_v[n] = _adamw(weights[n], grad_w[n], given["m_" + n], given["v_" + n])
    return (loss, grad_x, *[grad_w[n] for n in TWIN_WEIGHTS], *[delta_w[n] for n in TWIN_WEIGHTS],
            *[new_m[n] for n in TWIN_WEIGHTS], *[new_v[n] for n in TWIN_WEIGHTS])
```

```python
import functools
import math

import jax
import jax.numpy as jnp
from jax import lax
from jax.experimental import pallas as pl
from jax.experimental.pallas import tpu as pltpu

F32 = jnp.float32
BF16 = jnp.bfloat16

D_MODEL = 1024
N_META = 16
CHUNK = 64
PAD_ROWS = CHUNK - N_META
S5_WIDTH = 512
S5_GROUP = 16
S5_GROUPS = 32
S5_STATE = 64
S5_COLS = 2 * S5_GROUPS * S5_STATE
S5_BLK = 4
M_HEADS = 4
M_DK = 128
M_DV = 256
D_FF = 4096
N_DEV = 8
ALPHA = 2.0 ** 0.25
LN_EPS = 1e-5
IN_NAT = 5640
P_V, P_O, P_GS, P_GM, P_QK, P_U, P_IF, PW = 0, 1024, 2048, 3072, 4096, 5120, 5632, 5760
N_U, N_Q, N_K, N_V, N_O, N_I, N_GS, N_GM = 0, 512, 1024, 1536, 2560, 3584, 3592, 4616

ADAM_LR, ADAM_B1, ADAM_B2, ADAM_EPS, ADAM_WD, ADAM_STEP = 0.001, 0.9, 0.999, 1e-08, 0.01, 10

VMEM_LIMIT = 56 * 1024 * 1024
MESH = pl.DeviceIdType.MESH


def _pick(n, cands):
    for c in cands:
        if n % c == 0:
            return c
    raise ValueError(f"no tile for {n} among {cands}")


def _cparams(sem):
    return pltpu.CompilerParams(dimension_semantics=sem, vmem_limit_bytes=VMEM_LIMIT)


def _dot(a, b, ca, cb):
    return lax.dot_general(a, b, (((ca,), (cb,)), ((), ())), preferred_element_type=F32)


def _sigmoid(x):
    return 1.0 / (1.0 + jnp.exp(-x))


def _peer(k):
    x, y, c = lax.axis_index("x"), lax.axis_index("y"), lax.axis_index("c")
    px = 1 - x if k & 4 else x
    py = 1 - y if k & 2 else y
    pc = 1 - c if k & 1 else c
    return (px, py, pc), 4 * px + 2 * py + pc


def _my_id():
    return 4 * lax.axis_index("x") + 2 * lax.axis_index("y") + lax.axis_index("c")


def _exchange(arrs, name, gather):
    n = len(arrs)

    def body(*refs):
        ins, outs = refs[:n], refs[n:2 * n]
        send_sems, recv_sems, local_sems = refs[2 * n:]
        me = _my_id()
        local = []
        sends = []
        for a in range(n):
            src_me = ins[a] if gather else ins[a].at[me]
            cp = pltpu.make_async_copy(src_me, outs[a].at[me], local_sems.at[a])
            cp.start()
            local.append(cp)
            for k in range(1, N_DEV):
                peer, pid = _peer(k)
                src = ins[a] if gather else ins[a].at[pid]
                cp = pltpu.make_async_remote_copy(
                    src_ref=src, dst_ref=outs[a].at[me],
                    send_sem=send_sems.at[a, k - 1], recv_sem=recv_sems.at[a, k - 1],
                    device_id=peer, device_id_type=MESH)
                cp.start()
                sends.append(cp)
        for a in range(n):
            for k in range(1, N_DEV):
                peer, pid = _peer(k)
                src = ins[a] if gather else ins[a].at[pid]
                pltpu.make_async_remote_copy(
                    src_ref=src, dst_ref=outs[a].at[pid],
                    send_sem=send_sems.at[a, k - 1], recv_sem=recv_sems.at[a, k - 1],
                    device_id=peer, device_id_type=MESH).wait_recv()
        for cp in sends:
            cp.wait_send()
        for cp in local:
            cp.wait()

    out_shape = []
    for a in arrs:
        shp = (N_DEV,) + tuple(a.shape) if gather else tuple(a.shape)
        out_shape.append(jax.ShapeDtypeStruct(shp, a.dtype))
    return pl.pallas_call(
        body, name=name,
        out_shape=tuple(out_shape),
        in_specs=[pl.BlockSpec(memory_space=pl.ANY)] * n,
        out_specs=tuple([pl.BlockSpec(memory_space=pl.ANY)] * n),
        scratch_shapes=[pltpu.SemaphoreType.DMA((n, N_DEV - 1)),
                        pltpu.SemaphoreType.DMA((n, N_DEV - 1)),
                        pltpu.SemaphoreType.DMA((n,))],
    )(*arrs)


def _allreduce_small(pack):
    rows = pack.shape[0]

    def body(in_ref, out_ref, buf, send_sems, recv_sems):
        me = _my_id()
        sends = []
        for k in range(1, N_DEV):
            peer, pid = _peer(k)
            cp = pltpu.make_async_remote_copy(
                src_ref=in_ref, dst_ref=buf.at[me],
                send_sem=send_sems.at[k - 1], recv_sem=recv_sems.at[k - 1],
                device_id=peer, device_id_type=MESH)
            cp.start()
            sends.append(cp)
        for k in range(1, N_DEV):
            peer, pid = _peer(k)
            pltpu.make_async_remote_copy(
                src_ref=in_ref, dst_ref=buf.at[pid],
                send_sem=send_sems.at[k - 1], recv_sem=recv_sems.at[k - 1],
                device_id=peer, device_id_type=MESH).wait_recv()
        for cp in sends:
            cp.wait_send()
        buf[pl.ds(me, 1)] = in_ref[...][None]
        acc = buf[0]
        for s in range(1, N_DEV):
            acc = acc + buf[s]
        out_ref[...] = acc

    return pl.pallas_call(
        body, name="allreduce_small",
        out_shape=jax.ShapeDtypeStruct((rows, 128), F32),
        in_specs=[pl.BlockSpec(memory_space=pltpu.VMEM)],
        out_specs=pl.BlockSpec(memory_space=pltpu.VMEM),
        scratch_shapes=[pltpu.VMEM((N_DEV, rows, 128), F32),
                        pltpu.SemaphoreType.DMA((N_DEV - 1,)),
                        pltpu.SemaphoreType.DMA((N_DEV - 1,))],
        compiler_params=pltpu.CompilerParams(vmem_limit_bytes=VMEM_LIMIT),
    )(pack)


def _mm(a, b, mode, name, *, bias=None, add=None, add_scale=1.0, relu2_grad_of=None,
        act_out=False, colsum=False, out_dtype=F32):
    if mode == "nn":
        (M, K), (K2, N) = a.shape, b.shape
    elif mode == "nt":
        (M, K), (N, K2) = a.shape, b.shape
    else:
        (K, M), (K2, N) = a.shape, b.shape
    assert K == K2, (a.shape, b.shape, mode)
    tm = _pick(M, (1024, 640, 512, 256, 128)) if mode == "tn" else _pick(M, (640, 512, 256, 128))
    tn = _pick(N, (640, 512, 384, 256, 128))
    if mode == "tn":
        tk = _pick(K, (640, 512, 256, 128))
    else:
        tk = K if K <= 1024 else _pick(K, (1152, 1024, 640, 512))
    nk = K // tk
    has_bias, has_add, has_gate = bias is not None, add is not None, relu2_grad_of is not None

    def body(*refs):
        it = iter(refs)
        a_ref, b_ref = next(it), next(it)
        bias_ref = next(it) if has_bias else None
        add_ref = next(it) if has_add else None
        gate_ref = next(it) if has_gate else None
        o_ref = next(it)
        act_ref = next(it) if act_out else None
        cs_ref = next(it) if colsum else None
        acc_ref = next(it)
        i, k = pl.program_id(1), pl.program_id(2)

        @pl.when(k == 0)
        def _():
            acc_ref[...] = jnp.zeros_like(acc_ref)

        av = a_ref[...].astype(BF16)
        bv = b_ref[...].astype(BF16)
        if mode == "nn":
            acc_ref[...] += _dot(av, bv, 1, 0)
        elif mode == "nt":
            acc_ref[...] += _dot(av, bv, 1, 1)
        else:
            acc_ref[...] += _dot(av, bv, 0, 0)
        if colsum:
            @pl.when((i == 0) & (k == 0))
            def _():
                cs_ref[...] = jnp.zeros_like(cs_ref)

            @pl.when(i == 0)
            def _():
                cs_ref[0:1, :] += jnp.sum(b_ref[...].astype(F32), axis=0, keepdims=True)

        @pl.when(k == nk - 1)
        def _():
            r = acc_ref[...]
            if has_bias:
                r = r + bias_ref[...]
            if has_add:
                r = r + add_scale * add_ref[...]
            if has_gate:
                r = r * (2.0 * jnp.maximum(gate_ref[...], 0.0))
            o_ref[...] = r.astype(out_dtype)
            if act_out:
                act_ref[...] = jnp.square(jnp.maximum(r, 0.0)).astype(BF16)

    if mode == "nn":
        a_spec = pl.BlockSpec((tm, tk), lambda j, i, k: (i, k))
        b_spec = pl.BlockSpec((tk, tn), lambda j, i, k: (k, j))
    elif mode == "nt":
        a_spec = pl.BlockSpec((tm, tk), lambda j, i, k: (i, k))
        b_spec = pl.BlockSpec((tn, tk), lambda j, i, k: (j, k))
    else:
        a_spec = pl.BlockSpec((tk, tm), lambda j, i, k: (k, i))
        b_spec = pl.BlockSpec((tk, tn), lambda j, i, k: (k, j))
    in_specs, args = [a_spec, b_spec], [a, b]
    if has_bias:
        in_specs.append(pl.BlockSpec((1, tn), lambda j, i, k: (0, j)))
        args.append(bias)
    if has_add:
        in_specs.append(pl.BlockSpec((tm, tn), lambda j, i, k: (i, j)))
        args.append(add)
    if has_gate:
        in_specs.append(pl.BlockSpec((tm, tn), lambda j, i, k: (i, j)))
        args.append(relu2_grad_of)
    out_shape = [jax.ShapeDtypeStruct((M, N), out_dtype)]
    out_specs = [pl.BlockSpec((tm, tn), lambda j, i, k: (i, j))]
    if act_out:
        out_shape.append(jax.ShapeDtypeStruct((M, N), BF16))
        out_specs.append(pl.BlockSpec((tm, tn), lambda j, i, k: (i, j)))
    if colsum:
        out_shape.append(jax.ShapeDtypeStruct((8, N), F32))
        out_specs.append(pl.BlockSpec((8, tn), lambda j, i, k: (0, j)))
    res = pl.pallas_call(
        body, name=name,
        grid=(N // tn, M // tm, nk),
        in_specs=in_specs, out_specs=tuple(out_specs), out_shape=tuple(out_shape),
        scratch_shapes=[pltpu.VMEM((tm, tn), F32)],
        compiler_params=_cparams(("arbitrary", "arbitrary", "arbitrary")),
    )(*args)
    return res if len(res) > 1 else res[0]


def _ln_rows(v, g, b):
    mu = jnp.mean(v, axis=-1, keepdims=True)
    xc = v - mu
    var = jnp.mean(xc * xc, axis=-1, keepdims=True)
    return xc * lax.rsqrt(var + LN_EPS) * g + b


def _ln_bwd_rows(dy, v, g):
    mu = jnp.mean(v, axis=-1, keepdims=True)
    xc = v - mu
    var = jnp.mean(xc * xc, axis=-1, keepdims=True)
    rstd = lax.rsqrt(var + LN_EPS)
    xhat = xc * rstd
    dxh = dy * g
    dv = rstd * (dxh - jnp.mean(dxh, axis=-1, keepdims=True)
                 - xhat * jnp.mean(dxh * xhat, axis=-1, keepdims=True))
    return dv, xhat


def _ln0_fwd(x, meta, g, b):
    B, S, D = x.shape
    nch = S // CHUNK + 1

    def body(x_ref, meta_ref, g_ref, b_ref, h_ref, hb_ref):
        i = pl.program_id(1)

        @pl.when(i == 0)
        def _():
            m = _ln_rows(meta_ref[...], g_ref[...], b_ref[...])
            h_ref[0:PAD_ROWS, :] = jnp.zeros((PAD_ROWS, D), F32)
            h_ref[PAD_ROWS:CHUNK, :] = m
            hb_ref[0:PAD_ROWS, :] = jnp.zeros((PAD_ROWS, D), BF16)
            hb_ref[PAD_ROWS:CHUNK, :] = m.astype(BF16)

        @pl.when(i > 0)
        def _():
            y = _ln_rows(x_ref[0], g_ref[...], b_ref[...])
            h_ref[...] = y
            hb_ref[...] = y.astype(BF16)

    return pl.pallas_call(
        body, name="ln0_fwd", grid=(B, nch),
        in_specs=[pl.BlockSpec((1, CHUNK, D), lambda bb, i: (bb, jnp.maximum(i - 1, 0), 0)),
                  pl.BlockSpec((N_META, D), lambda bb, i: (0, 0)),
                  pl.BlockSpec((1, D), lambda bb, i: (0, 0)),
                  pl.BlockSpec((1, D), lambda bb, i: (0, 0))],
        out_specs=(pl.BlockSpec((CHUNK, D), lambda bb, i: (bb * nch + i, 0)),
                   pl.BlockSpec((CHUNK, D), lambda bb, i: (bb * nch + i, 0))),
        out_shape=(jax.ShapeDtypeStruct((B * nch * CHUNK, D), F32),
                   jax.ShapeDtypeStruct((B * nch * CHUNK, D), BF16)),
        compiler_params=_cparams(("arbitrary", "arbitrary")),
    )(x, meta, g, b)


def _ln0_bwd(dh0, x, meta, g):
    B, S, D = x.shape
    nch = S // CHUNK + 1

    def body(dh_ref, x_ref, meta_ref, g_ref, dx_ref, dmeta_ref, dg_ref, db_ref):
        bb, i = pl.program_id(0), pl.program_id(1)

        @pl.when((bb == 0) & (i == 0))
        def _():
            dmeta_ref[...] = jnp.zeros_like(dmeta_ref)
            dg_ref[...] = jnp.zeros_like(dg_ref)
            db_ref[...] = jnp.zeros_like(db_ref)

        @pl.when(i == 0)
        def _():
            dy = dh_ref[PAD_ROWS:CHUNK, :]
            dv, xhat = _ln_bwd_rows(dy, meta_ref[...], g_ref[...])
            dmeta_ref[...] += dv
            dg_ref[0:1, :] += jnp.sum(dy * xhat, axis=0, keepdims=True)
            db_ref[0:1, :] += jnp.sum(dy, axis=0, keepdims=True)
            dx_ref[0] = jnp.zeros((CHUNK, D), F32)

        @pl.when(i > 0)
        def _():
            dy = dh_ref[...]
            dv, xhat = _ln_bwd_rows(dy, x_ref[0], g_ref[...])
            dx_ref[0] = dv
            dg_ref[0:1, :] += jnp.sum(dy * xhat, axis=0, keepdims=True)
            db_ref[0:1, :] += jnp.sum(dy, axis=0, keepdims=True)

    xmap = lambda bb, i: (bb, jnp.maximum(i - 1, 0), 0)
    const = lambda bb, i: (0, 0)
    return pl.pallas_call(
        body, name="ln0_bwd", grid=(B, nch),
        in_specs=[pl.BlockSpec((CHUNK, D), lambda bb, i: (bb * nch + i, 0)),
                  pl.BlockSpec((1, CHUNK, D), xmap),
                  pl.BlockSpec((N_META, D), const),
                  pl.BlockSpec((1, D), const)],
        out_specs=(pl.BlockSpec((1, CHUNK, D), xmap),
                   pl.BlockSpec((N_META, D), const),
                   pl.BlockSpec((8, D), const),
                   pl.BlockSpec((8, D), const)),
        out_shape=(jax.ShapeDtypeStruct((B, S, D), F32),
                   jax.ShapeDtypeStruct((N_META, D), F32),
                   jax.ShapeDtypeStruct((8, D), F32),
                   jax.ShapeDtypeStruct((8, D), F32)),
        compiler_params=_cparams(("arbitrary", "arbitrary")),
    )(dh0, x, meta, g)


def _ln_res_fwd(h_prev, r, g, b, name):
    T, D = h_prev.shape
    tr = _pick(T, (320, 256, 128, 64))

    def body(hp_ref, r_ref, g_ref, b_ref, pre_ref, h_ref, hb_ref):
        pre = ALPHA * hp_ref[...] + r_ref[...]
        y = _ln_rows(pre, g_ref[...], b_ref[...])
        pre_ref[...] = pre
        h_ref[...] = y
        hb_ref[...] = y.astype(BF16)

    row = pl.BlockSpec((tr, D), lambda i: (i, 0))
    vec = pl.BlockSpec((1, D), lambda i: (0, 0))
    return pl.pallas_call(
        body, name=name, grid=(T // tr,),
        in_specs=[row, row, vec, vec], out_specs=(row, row, row),
        out_shape=(jax.ShapeDtypeStruct((T, D), F32), jax.ShapeDtypeStruct((T, D), F32),
                   jax.ShapeDtypeStruct((T, D), BF16)),
        compiler_params=_cparams(("arbitrary",)),
    )(h_prev, r, g, b)


def _ln_bwd(dh, pre, g, name):
    T, D = dh.shape
    tr = _pick(T, (320, 256, 128, 64))

    def body(dh_ref, pre_ref, g_ref, dp_ref, dpb_ref, dg_ref, db_ref):
        @pl.when(pl.program_id(0) == 0)
        def _():
            dg_ref[...] = jnp.zeros_like(dg_ref)
            db_ref[...] = jnp.zeros_like(db_ref)

        dy = dh_ref[...]
        dv, xhat = _ln_bwd_rows(dy, pre_ref[...], g_ref[...])
        dp_ref[...] = dv
        dpb_ref[...] = dv.astype(BF16)
        dg_ref[0:1, :] += jnp.sum(dy * xhat, axis=0, keepdims=True)
        db_ref[0:1, :] += jnp.sum(dy, axis=0, keepdims=True)

    row = pl.BlockSpec((tr, D), lambda i: (i, 0))
    vec = pl.BlockSpec((1, D), lambda i: (0, 0))
    acc = pl.BlockSpec((8, D), lambda i: (0, 0))
    return pl.pallas_call(
        body, name=name, grid=(T // tr,),
        in_specs=[row, row, vec], out_specs=(row, row, acc, acc),
        out_shape=(jax.ShapeDtypeStruct((T, D), F32), jax.ShapeDtypeStruct((T, D), BF16),
                   jax.ShapeDtypeStruct((8, D), F32), jax.ShapeDtypeStruct((8, D), F32)),
        compiler_params=_cparams(("arbitrary",)),
    )(dh, pre, g)


def _ln2_loss(h1, ff, g, b, target):
    T, D = h1.shape
    B, S, _ = target.shape
    nch = S // CHUNK + 1

    def body(h_ref, ff_ref, g_ref, b_ref, t_ref, loss_ref, dp_ref, dpb_ref, dg_ref, db_ref):
        bb, i = pl.program_id(0), pl.program_id(1)

        @pl.when((bb == 0) & (i == 0))
        def _():
            loss_ref[...] = jnp.zeros_like(loss_ref)
            dg_ref[...] = jnp.zeros_like(dg_ref)
            db_ref[...] = jnp.zeros_like(db_ref)

        @pl.when(i == 0)
        def _():
            dp_ref[...] = jnp.zeros((CHUNK, D), F32)
            dpb_ref[...] = jnp.zeros((CHUNK, D), BF16)

        @pl.when(i > 0)
        def _():
            pre = ALPHA * h_ref[...] + ff_ref[...]
            gg = g_ref[...]
            y = _ln_rows(pre, gg, b_ref[...])
            err = y - t_ref[0]
            loss_ref[0:1, 0:1] += 0.5 * jnp.sum(jnp.mean(err * err, axis=-1, keepdims=True),
                                                axis=0, keepdims=True)
            dy = err * (1.0 / D)
            dv, xhat = _ln_bwd_rows(dy, pre, gg)
            dp_ref[...] = dv
            dpb_ref[...] = dv.astype(BF16)
            dg_ref[0:1, :] += jnp.sum(dy * xhat, axis=0, keepdims=True)
            db_ref[0:1, :] += jnp.sum(dy, axis=0, keepdims=True)

    row = pl.BlockSpec((CHUNK, D), lambda bb, i: (bb * nch + i, 0))
    const = lambda bb, i: (0, 0)
    return pl.pallas_call(
        body, name="ln2_loss", grid=(B, nch),
        in_specs=[row, row, pl.BlockSpec((1, D), const), pl.BlockSpec((1, D), const),
                  pl.BlockSpec((1, CHUNK, D), lambda bb, i: (bb, jnp.maximum(i - 1, 0), 0))],
        out_specs=(pl.BlockSpec((8, 128), const), row, row,
                   pl.BlockSpec((8, D), const), pl.BlockSpec((8, D), const)),
        out_shape=(jax.ShapeDtypeStruct((8, 128), F32),
                   jax.ShapeDtypeStruct((T, D), F32), jax.ShapeDtypeStruct((T, D), BF16),
                   jax.ShapeDtypeStruct((8, D), F32), jax.ShapeDtypeStruct((8, D), F32)),
        compiler_params=_cparams(("arbitrary", "arbitrary")),
    )(h1, ff, g, b, target)


def _s5_prep(lam_re, lam_im, log_dt, b_re, b_im, c_re, c_im):
    dt = jnp.exp(log_dt)[:, None]
    mag = jnp.exp(lam_re * dt)
    ar = mag * jnp.cos(lam_im * dt)
    ai = mag * jnp.sin(lam_im * dt)
    nr, ni = ar - 1.0, ai
    den = lam_re * lam_re + lam_im * lam_im
    cr = (nr * lam_re + ni * lam_im) / den
    ci = (ni * lam_re - nr * lam_im) / den
    bbr = cr[..., None] * b_re - ci[..., None] * b_im
    bbi = cr[..., None] * b_im + ci[..., None] * b_re
    eye = jnp.eye(8, dtype=F32)
    bb = jnp.stack([bbr, bbi]).reshape(2, S5_BLK, 8, S5_STATE, S5_GROUP)
    bblk = jnp.einsum("rbgph,gj->bghrjp", bb, eye).reshape(S5_BLK, 128, 1024)
    cc = jnp.stack([c_re, -c_im]).reshape(2, S5_BLK, 8, S5_GROUP, S5_STATE)
    cblk = jnp.einsum("rbghp,gj->brjpgh", cc, eye).reshape(S5_BLK, 1024, 128)
    return ar.reshape(1, 2048), ai.reshape(1, 2048), bblk, cblk


def _scan_tables(ar, ai):
    pr, pi = [ar], [ai]
    for _ in range(7):
        pr, pi = pr + [pr[-1] * ar - pi[-1] * ai], pi + [pr[-1] * ai + pi[-1] * ar]
    pw_r = jnp.concatenate(pr, axis=0)
    pw_i = jnp.concatenate(pi, axis=0)
    row = jnp.arange(8)[:, None]

    def tables(sign, reverse):
        rows = []
        for n, sh in ((0, 1), (1, 2), (3, 4)):
            mask = (row < 8 - sh) if reverse else (row >= sh)
            rows.append(jnp.where(mask, pw_r[n][None, :], 0.0))
            rows.append(jnp.where(mask, sign * pw_i[n][None, :], 0.0))
        cr_ = pw_r[::-1] if reverse else pw_r
        ci_ = pw_i[::-1] if reverse else pw_i
        rows += [cr_, sign * ci_]
        return jnp.stack(rows)

    return tables(1.0, False), tables(-1.0, True)


def _scan_rows(s_ref, row0, tab_ref, carry_ref, ngroups, reverse, extra=None):
    take = 0 if reverse else 7
    rowid = lax.broadcasted_iota(jnp.int32, (8, 128), 0)

    def group(it, c):
        rb = (ngroups - 1 - it) if reverse else it
        r0 = pl.multiple_of(row0 + rb * 8, 8)
        for blk in range(S5_BLK):
            for j in range(4):
                cre = blk * 1024 + j * 128
                cim = cre + 512
                tc = blk * 512 + j * 128
                xr = s_ref[pl.ds(r0, 8), cre:cre + 128]
                xi = s_ref[pl.ds(r0, 8), cim:cim + 128]
                for lvl, sh in enumerate((1, 2, 4)):
                    lr = tab_ref[2 * lvl, :, tc:tc + 128]
                    li = tab_ref[2 * lvl + 1, :, tc:tc + 128]
                    shift = (8 - sh) if reverse else sh
                    sr = pltpu.roll(xr, shift, 0)
                    si = pltpu.roll(xi, shift, 0)
                    xr, xi = xr + lr * sr - li * si, xi + lr * si + li * sr
                pr = tab_ref[6, :, tc:tc + 128]
                pi = tab_ref[7, :, tc:tc + 128]
                c_r = carry_ref[:, cre:cre + 128]
                c_i = carry_ref[:, cim:cim + 128]
                xr, xi = xr + pr * c_r - pi * c_i, xi + pr * c_i + pi * c_r
                s_ref[pl.ds(r0, 8), cre:cre + 128] = xr
                s_ref[pl.ds(r0, 8), cim:cim + 128] = xi
                nr = jnp.sum(jnp.where(rowid == take, xr, 0.0), axis=0, keepdims=True)
                ni = jnp.sum(jnp.where(rowid == take, xi, 0.0), axis=0, keepdims=True)
                carry_ref[:, cre:cre + 128] = jnp.broadcast_to(nr, (8, 128))
                carry_ref[:, cim:cim + 128] = jnp.broadcast_to(ni, (8, 128))
                if extra is not None:
                    extra(r0, rb, cre, cim, tc, xr, xi)
        return c

    lax.fori_loop(0, ngroups, group, 0)


def _gelu(x):
    c = math.sqrt(2.0 / math.pi)
    t = jnp.tanh(c * (x + 0.044715 * x * x * x))
    return 0.5 * x * (1.0 + t)


def _gelu_grad(x):
    c = math.sqrt(2.0 / math.pi)
    t = jnp.tanh(c * (x + 0.044715 * x * x * x))
    return 0.5 * (1.0 + t) + 0.5 * x * (1.0 - t * t) * c * (1.0 + 3.0 * 0.044715 * x * x)


def _s5_tile(lp):
    return _pick(lp, (320, 256, 128, 64))


def _s5_fwd(p, bblk, cblk, tab_f, dskip, lp):
    T = p.shape[0]
    ts = _s5_tile(lp)
    nblk, per_seq = T // ts, lp // ts
    ucol = P_U // S5_WIDTH

    def body(u_ref, b_ref, c_ref, tab_ref, d_ref, y_ref, gy_ref, cin_ref, s_sc, carry_sc):
        r = pl.program_id(0)
        first = (r % per_seq) == 0

        @pl.when(first)
        def _():
            carry_sc[...] = jnp.zeros_like(carry_sc)

        cin_ref[0] = carry_sc[...]
        rowid = lax.broadcasted_iota(jnp.int32, (ts, 1), 0)
        u = jnp.where(first & (rowid < PAD_ROWS), 0.0, u_ref[...])
        ub = u.astype(BF16)
        for blk in range(S5_BLK):
            s_sc[:, blk * 1024:(blk + 1) * 1024] = _dot(ub[:, blk * 128:(blk + 1) * 128], b_ref[blk], 1, 0)
        _scan_rows(s_sc, 0, tab_ref, carry_sc, ts // 8, False)
        for blk in range(S5_BLK):
            sb = s_sc[:, blk * 1024:(blk + 1) * 1024].astype(BF16)
            yb = _dot(sb, c_ref[blk], 1, 0) + d_ref[:, blk * 128:(blk + 1) * 128] * u[:, blk * 128:(blk + 1) * 128]
            y_ref[:, blk * 128:(blk + 1) * 128] = yb
            gy_ref[:, blk * 128:(blk + 1) * 128] = _gelu(yb).astype(BF16)

    return pl.pallas_call(
        body, name="s5_fwd", grid=(nblk,),
        in_specs=[pl.BlockSpec((ts, S5_WIDTH), lambda r: (r, ucol)),
                  pl.BlockSpec((S5_BLK, 128, 1024), lambda r: (0, 0, 0)),
                  pl.BlockSpec((S5_BLK, 1024, 128), lambda r: (0, 0, 0)),
                  pl.BlockSpec((8, 8, 2048), lambda r: (0, 0, 0)),
                  pl.BlockSpec((1, S5_WIDTH), lambda r: (0, 0))],
        out_specs=(pl.BlockSpec((ts, S5_WIDTH), lambda r: (r, 0)),
                   pl.BlockSpec((ts, S5_WIDTH), lambda r: (r, 0)),
                   pl.BlockSpec((1, 8, S5_COLS), lambda r: (r, 0, 0))),
        out_shape=(jax.ShapeDtypeStruct((T, S5_WIDTH), F32),
                   jax.ShapeDtypeStruct((T, S5_WIDTH), BF16),
                   jax.ShapeDtypeStruct((nblk, 8, S5_COLS), F32)),
        scratch_shapes=[pltpu.VMEM((ts, S5_COLS), F32), pltpu.VMEM((8, S5_COLS), F32)],
        compiler_params=_cparams(("arbitrary",)),
    )(p, bblk.astype(BF16), cblk.astype(BF16), tab_f, dskip)


def _s5_bwd(p, y_pre, dgy, cin, bblk, cblk, tab_f, tab_b, dskip, lp):
    T = p.shape[0]
    ts = _s5_tile(lp)
    nblk, per_seq = T // ts, lp // ts
    ucol = P_U // S5_WIDTH
    ng = ts // 8

    def body(u_ref, y_ref, dgy_ref, cin_ref, b_ref, bt_ref, c_ref, ct_ref, tf_ref, tb_ref, d_ref,
             du_ref, dbb_ref, dcb_ref, dlam_ref, dd_ref, s_sc, a_sc, carry_sc, carry_b):
        t = pl.program_id(0)
        r = nblk - 1 - t
        first = (r % per_seq) == 0
        last = (r % per_seq) == per_seq - 1

        @pl.when(t == 0)
        def _():
            dbb_ref[...] = jnp.zeros_like(dbb_ref)
            dcb_ref[...] = jnp.zeros_like(dcb_ref)
            dlam_ref[...] = jnp.zeros_like(dlam_ref)
            dd_ref[...] = jnp.zeros_like(dd_ref)

        @pl.when(last)
        def _():
            carry_b[...] = jnp.zeros_like(carry_b)

        carry_sc[...] = cin_ref[0]
        s_sc[0:8, :] = cin_ref[0]
        rowid = lax.broadcasted_iota(jnp.int32, (ts, 1), 0)
        padrow = first & (rowid < PAD_ROWS)
        u = jnp.where(padrow, 0.0, u_ref[...])
        ub = u.astype(BF16)
        for blk in range(S5_BLK):
            s_sc[8:8 + ts, blk * 1024:(blk + 1) * 1024] = _dot(ub[:, blk * 128:(blk + 1) * 128], b_ref[blk], 1, 0)
        _scan_rows(s_sc, 8, tf_ref, carry_sc, ng, False)

        dy = dgy_ref[...] * _gelu_grad(y_ref[...])
        dyb = dy.astype(BF16)
        dd_ref[0:1, :] += jnp.sum(dy * u, axis=0, keepdims=True)
        for blk in range(S5_BLK):
            a_sc[:, blk * 1024:(blk + 1) * 1024] = _dot(dyb[:, blk * 128:(blk + 1) * 128], ct_ref[blk], 1, 0)
            sb = s_sc[8:8 + ts, blk * 1024:(blk + 1) * 1024].astype(BF16)
            dcb_ref[blk] += _dot(sb, dyb[:, blk * 128:(blk + 1) * 128], 0, 0)

        row8 = lax.broadcasted_iota(jnp.int32, (8, 128), 0)

        def lam_grad(r0, rb, cre, cim, tc, a_r, a_i):
            q0 = pl.multiple_of(r0, 8)
            q1 = pl.multiple_of(r0 + 8, 8)
            pr = jnp.where(row8 == 0, pltpu.roll(s_sc[pl.ds(q0, 8), cre:cre + 128], 1, 0),
                           pltpu.roll(s_sc[pl.ds(q1, 8), cre:cre + 128], 1, 0))
            pi = jnp.where(row8 == 0, pltpu.roll(s_sc[pl.ds(q0, 8), cim:cim + 128], 1, 0),
                           pltpu.roll(s_sc[pl.ds(q1, 8), cim:cim + 128], 1, 0))
            dlam_ref[0, :, tc:tc + 128] += a_r * pr + a_i * pi
            dlam_ref[1, :, tc:tc + 128] += a_i * pr - a_r * pi

        _scan_rows(a_sc, 0, tb_ref, carry_b, ng, True, extra=lam_grad)

        for blk in range(S5_BLK):
            ab = a_sc[:, blk * 1024:(blk + 1) * 1024].astype(BF16)
            dub = _dot(ab, bt_ref[blk], 1, 0) + d_ref[:, blk * 128:(blk + 1) * 128] * dy[:, blk * 128:(blk + 1) * 128]
            du_ref[:, blk * 128:(blk + 1) * 128] = jnp.where(padrow, 0.0, dub).astype(BF16)
            dbb_ref[blk] += _dot(ub[:, blk * 128:(blk + 1) * 128], ab, 0, 0)

    const3 = lambda t: (0, 0, 0)
    rev = lambda t: (nblk - 1 - t, 0)
    return pl.pallas_call(
        body, name="s5_bwd", grid=(nblk,),
        in_specs=[pl.BlockSpec((ts, S5_WIDTH), lambda t: (nblk - 1 - t, ucol)),
                  pl.BlockSpec((ts, S5_WIDTH), rev),
                  pl.BlockSpec((ts, S5_WIDTH), rev),
                  pl.BlockSpec((1, 8, S5_COLS), lambda t: (nblk - 1 - t, 0, 0)),
                  pl.BlockSpec((S5_BLK, 128, 1024), const3),
                  pl.BlockSpec((S5_BLK, 1024, 128), const3),
                  pl.BlockSpec((S5_BLK, 1024, 128), const3),
                  pl.BlockSpec((S5_BLK, 128, 1024), const3),
                  pl.BlockSpec((8, 8, 2048), const3),
                  pl.BlockSpec((8, 8, 2048), const3),
                  pl.BlockSpec((1, S5_WIDTH), lambda t: (0, 0))],
        out_specs=(pl.BlockSpec((ts, S5_WIDTH), rev),
                   pl.BlockSpec((S5_BLK, 128, 1024), const3),
                   pl.BlockSpec((S5_BLK, 1024, 128), const3),
                   pl.BlockSpec((2, 8, 2048), const3),
                   pl.BlockSpec((8, S5_WIDTH), lambda t: (0, 0))),
        out_shape=(jax.ShapeDtypeStruct((T, S5_WIDTH), BF16),
                   jax.ShapeDtypeStruct((S5_BLK, 128, 1024), F32),
                   jax.ShapeDtypeStruct((S5_BLK, 1024, 128), F32),
                   jax.ShapeDtypeStruct((2, 8, 2048), F32),
                   jax.ShapeDtypeStruct((8, S5_WIDTH), F32)),
        scratch_shapes=[pltpu.VMEM((ts + 8, S5_COLS), F32), pltpu.VMEM((ts, S5_COLS), F32),
                        pltpu.VMEM((8, S5_COLS), F32), pltpu.VMEM((8, S5_COLS), F32)],
        compiler_params=_cparams(("arbitrary",)),
    )(p, y_pre, dgy, cin, bblk.astype(BF16), jnp.swapaxes(bblk, 1, 2).astype(BF16),
      cblk.astype(BF16), jnp.swapaxes(cblk, 1, 2).astype(BF16), tab_f, tab_b, dskip)


def _row_in_seq(i, tr, lp):
    rowid = lax.broadcasted_iota(jnp.int32, (tr, 1), 0)
    return (i * tr + rowid) % lp


def _conv_fwd(p, w, b, lp):
    T = p.shape[0]
    tr = _pick(T, (320, 256, 128, 64))
    c = P_QK // 1024

    def body(x_ref, xp_ref, w_ref, b_ref, o_ref):
        i = pl.program_id(0)
        pos = _row_in_seq(i, tr, lp)
        x = jnp.where(pos < PAD_ROWS, 0.0, x_ref[...])
        pos_p = (i * tr - 8 + lax.broadcasted_iota(jnp.int32, (8, 1), 0)) % lp
        xp = jnp.where((pos_p < PAD_ROWS) | (i == 0), 0.0, xp_ref[...])
        xx = jnp.concatenate([xp, x], axis=0)
        acc = b_ref[...] + w_ref[3:4, :] * x
        for s in (1, 2, 3):
            acc = acc + w_ref[3 - s:4 - s, :] * pltpu.roll(xx, s, 0)[8:8 + tr]
        o_ref[...] = acc * _sigmoid(acc)

    return pl.pallas_call(
        body, name="conv_fwd", grid=(T // tr,),
        in_specs=[pl.BlockSpec((tr, 1024), lambda i: (i, c)),
                  pl.BlockSpec((8, 1024), lambda i: (jnp.maximum(i * (tr // 8) - 1, 0), c)),
                  pl.BlockSpec((4, 1024), lambda i: (0, 0)),
                  pl.BlockSpec((1, 1024), lambda i: (0, 0))],
        out_specs=pl.BlockSpec((tr, 1024), lambda i: (i, 0)),
        out_shape=jax.ShapeDtypeStruct((T, 1024), F32),
        compiler_params=_cparams(("arbitrary",)),
    )(p, p, w, b)


def _conv_bwd(p, dqk, w, b, lp):
    T = p.shape[0]
    tr = _pick(T, (320, 256, 128, 64))
    c = P_QK // 1024
    nb = T // tr

    def body(x_ref, xp_ref, xn_ref, g_ref, gn_ref, w_ref, b_ref, dx_ref, dw_ref, db_ref):
        i = pl.program_id(0)

        @pl.when(i == 0)
        def _():
            dw_ref[...] = jnp.zeros_like(dw_ref)
            db_ref[...] = jnp.zeros_like(db_ref)

        def seqpos(off, n):
            return (i * tr + off + lax.broadcasted_iota(jnp.int32, (n, 1), 0)) % lp

        x = jnp.where(seqpos(0, tr) < PAD_ROWS, 0.0, x_ref[...])
        xp = jnp.where((seqpos(-8, 8) < PAD_ROWS) | (i == 0), 0.0, xp_ref[...])
        xn = jnp.where((seqpos(tr, 8) < PAD_ROWS) | (i == nb - 1), 0.0, xn_ref[...])
        xx = jnp.concatenate([xp, x, xn], axis=0)
        gg = jnp.concatenate([g_ref[...], gn_ref[...]], axis=0)
        n2 = tr + 8
        acc = b_ref[...] + w_ref[3:4, :] * xx[8:8 + n2]
        for s in (1, 2, 3):
            acc = acc + w_ref[3 - s:4 - s, :] * pltpu.roll(xx, s, 0)[8:8 + n2]
        sg = _sigmoid(acc)
        dpre = gg * (sg * (1.0 + acc * (1.0 - sg)))
        valid = jnp.concatenate(
            [seqpos(0, tr) >= PAD_ROWS, (seqpos(tr, 8) >= PAD_ROWS) & (i < nb - 1)], axis=0)
        dpre = jnp.where(valid, dpre, 0.0)
        d0 = dpre[0:tr]
        dx = w_ref[3:4, :] * d0
        for s in (1, 2, 3):
            dx = dx + w_ref[3 - s:4 - s, :] * pltpu.roll(dpre, n2 - s, 0)[0:tr]
        dx_ref[...] = jnp.where(seqpos(0, tr) < PAD_ROWS, 0.0, dx).astype(BF16)
        db_ref[0:1, :] += jnp.sum(d0, axis=0, keepdims=True)
        dw_ref[3:4, :] += jnp.sum(d0 * x, axis=0, keepdims=True)
        for s in (1, 2, 3):
            xs = pltpu.roll(xx, s, 0)[8:8 + tr]
            dw_ref[3 - s:4 - s, :] += jnp.sum(d0 * xs, axis=0, keepdims=True)

    t8 = tr // 8
    return pl.pallas_call(
        body, name="conv_bwd", grid=(nb,),
        in_specs=[pl.BlockSpec((tr, 1024), lambda i: (i, c)),
                  pl.BlockSpec((8, 1024), lambda i: (jnp.maximum(i * t8 - 1, 0), c)),
                  pl.BlockSpec((8, 1024), lambda i: (jnp.minimum((i + 1) * t8, nb * t8 - 1), c)),
                  pl.BlockSpec((tr, 1024), lambda i: (i, 0)),
                  pl.BlockSpec((8, 1024), lambda i: (jnp.minimum((i + 1) * t8, nb * t8 - 1), 0)),
                  pl.BlockSpec((4, 1024), lambda i: (0, 0)),
                  pl.BlockSpec((1, 1024), lambda i: (0, 0))],
        out_specs=(pl.BlockSpec((tr, 1024), lambda i: (i, 0)),
                   pl.BlockSpec((8, 1024), lambda i: (0, 0)),
                   pl.BlockSpec((8, 1024), lambda i: (0, 0))),
        out_shape=(jax.ShapeDtypeStruct((T, 1024), BF16),
                   jax.ShapeDtypeStruct((8, 1024), F32),
                   jax.ShapeDtypeStruct((8, 1024), F32)),
        compiler_params=_cparams(("arbitrary",)),
    )(p, p, p, dqk, dqk, w, b)


def _split3(x):
    hi = x.astype(BF16)
    r1 = x - hi.astype(F32)
    mid = r1.astype(BF16)
    lo = (r1 - mid.astype(F32)).astype(BF16)
    return hi, mid, lo


def _tri_sum(x, upper):
    r = lax.broadcasted_iota(jnp.int32, (CHUNK, CHUNK), 0)
    c = lax.broadcasted_iota(jnp.int32, (CHUNK, CHUNK), 1)
    tri = jnp.where((r <= c) if upper else (r >= c), 1.0, 0.0).astype(BF16)
    hi, mid, lo = _split3(x)
    return _dot(tri, hi, 1, 0) + _dot(tri, mid, 1, 0) + _dot(tri, lo, 1, 0)


def _lane_col(x, lane):
    l = lax.broadcasted_iota(jnp.int32, x.shape, 1)
    return jnp.sum(jnp.where(l == lane, x, 0.0), axis=1, keepdims=True)


def _to_row(col):
    r = lax.broadcasted_iota(jnp.int32, (CHUNK, CHUNK), 0)
    c = lax.broadcasted_iota(jnp.int32, (CHUNK, CHUNK), 1)
    return jnp.sum(jnp.where(r == c, col, 0.0), axis=0, keepdims=True)


def _to_col(row):
    r = lax.broadcasted_iota(jnp.int32, (CHUNK, CHUNK), 0)
    c = lax.broadcasted_iota(jnp.int32, (CHUNK, CHUNK), 1)
    return jnp.sum(jnp.where(r == c, row, 0.0), axis=1, keepdims=True)


def _log_sigmoid(x):
    return jnp.minimum(x, 0.0) - jnp.log(1.0 + jnp.exp(-jnp.abs(x)))


def _mlstm_gates(ifv, padmask):
    lf = jnp.where(padmask, 0.0, _log_sigmoid(ifv))
    b_all = _tri_sum(lf, False)
    li = jnp.where(padmask, -jnp.inf, ifv)
    return li, b_all


def _mlstm_head_fwd(q, k, v, li_col, b_col, c_st, n_st, m_st):
    r = lax.broadcasted_iota(jnp.int32, (CHUNK, CHUNK), 0)
    c = lax.broadcasted_iota(jnp.int32, (CHUNK, CHUNK), 1)
    rowid = lax.broadcasted_iota(jnp.int32, (CHUNK, 1), 0)
    b_row = _to_row(b_col)
    li_row = _to_row(li_col)
    dmat = jnp.where(r >= c, b_col - b_row + li_row, -jnp.inf)
    m_inter = b_col + m_st
    m_row = jnp.maximum(m_inter, jnp.max(dmat, axis=1, keepdims=True))
    w_intra = jnp.exp(dmat - m_row)
    w_inter = jnp.exp(m_inter - m_row)
    qb, kb, vb = q.astype(BF16), k.astype(BF16), v.astype(BF16)
    qk = _dot(qb, kb, 1, 1)
    s = qk * w_intra
    cb = c_st.astype(BF16)
    qc = _dot(qb, cb, 1, 0)
    qn = jnp.sum(q * n_st, axis=1, keepdims=True)
    num = _dot(s.astype(BF16), vb, 1, 0) + w_inter * qc
    den = jnp.sum(s, axis=1, keepdims=True) + w_inter * qn
    floor = jnp.exp(-m_row)
    rinv = 1.0 / jnp.maximum(jnp.abs(den), floor)
    h = num * rinv
    b_last = jnp.sum(jnp.where(rowid == CHUNK - 1, b_col, 0.0), axis=0, keepdims=True)
    g_col = b_last - b_col + li_col
    m_new = jnp.maximum(b_last + m_st, jnp.max(g_col, axis=0, keepdims=True))
    w_k = jnp.exp(g_col - m_new)
    decay = jnp.exp(b_last + m_st - m_new)
    kw = w_k * k
    c_new = decay * c_st + _dot(kw.astype(BF16), vb, 0, 0)
    n_new = decay * n_st + jnp.sum(kw, axis=0, keepdims=True)
    return dict(h=h, c_new=c_new, n_new=n_new, m_new=m_new, w_intra=w_intra, w_inter=w_inter, s=s,
                qc=qc, qn=qn, den=den, floor=floor, rinv=rinv, w_k=w_k, decay=decay, kw=kw,
                qb=qb, kb=kb, vb=vb, cb=cb)


def _mlstm_fwd(p, qk, lp):
    T = p.shape[0]
    nch = lp // CHUNK
    B = T // lp
    scale = M_DK ** -0.5

    def body(q_ref, k_ref, v_ref, if_ref, h_ref, cst_ref, nm_ref, c_sc, nm_sc):
        ci = pl.program_id(1)

        @pl.when(ci == 0)
        def _():
            c_sc[...] = jnp.zeros_like(c_sc)
            nm_sc[...] = jnp.zeros_like(nm_sc)

        cst_ref[0] = c_sc[...]
        nm_ref[0] = nm_sc[...]
        rowid = lax.broadcasted_iota(jnp.int32, (CHUNK, 1), 0)
        padmask = (ci == 0) & (rowid < PAD_ROWS)
        li_all, b_all = _mlstm_gates(if_ref[...], padmask)
        for hd in range(M_HEADS):
            q = q_ref[:, hd * M_DK:(hd + 1) * M_DK]
            k = k_ref[:, hd * M_DK:(hd + 1) * M_DK] * scale
            v = v_ref[:, hd * M_DV:(hd + 1) * M_DV]
            o = _mlstm_head_fwd(q, k, v, _lane_col(li_all, hd), _lane_col(b_all, M_HEADS + hd),
                                c_sc[hd * M_DK:(hd + 1) * M_DK, :], nm_sc[hd:hd + 1, :],
                                nm_sc[M_HEADS + hd:M_HEADS + hd + 1, 0:1])
            h_ref[:, hd * M_DV:(hd + 1) * M_DV] = o["h"]
            c_sc[hd * M_DK:(hd + 1) * M_DK, :] = o["c_new"]
            nm_sc[hd:hd + 1, :] = o["n_new"]
            nm_sc[M_HEADS + hd:M_HEADS + hd + 1, :] = jnp.broadcast_to(o["m_new"], (1, 128))

    rowmap = lambda bb, ci: (bb * nch + ci, 0)
    return pl.pallas_call(
        body, name="mlstm_fwd", grid=(B, nch),
        in_specs=[pl.BlockSpec((CHUNK, 512), rowmap),
                  pl.BlockSpec((CHUNK, 512), lambda bb, ci: (bb * nch + ci, 1)),
                  pl.BlockSpec((CHUNK, 1024), lambda bb, ci: (bb * nch + ci, P_V // 1024)),
                  pl.BlockSpec((CHUNK, 128), lambda bb, ci: (bb * nch + ci, P_IF // 128))],
        out_specs=(pl.BlockSpec((CHUNK, 1024), rowmap),
                   pl.BlockSpec((1, M_HEADS * M_DK, M_DV), lambda bb, ci: (bb * nch + ci, 0, 0)),
                   pl.BlockSpec((1, 8, 128), lambda bb, ci: (bb * nch + ci, 0, 0))),
        out_shape=(jax.ShapeDtypeStruct((T, 1024), F32),
                   jax.ShapeDtypeStruct((B * nch, M_HEADS * M_DK, M_DV), F32),
                   jax.ShapeDtypeStruct((B * nch, 8, 128), F32)),
        scratch_shapes=[pltpu.VMEM((M_HEADS * M_DK, M_DV), F32), pltpu.VMEM((8, 128), F32)],
        compiler_params=_cparams(("arbitrary", "arbitrary")),
    )(qk, qk, p, p)


def _mlstm_bwd(p, qk, cst, nm, dh, lp):
    T = p.shape[0]
    nch = lp // CHUNK
    B = T // lp
    scale = M_DK ** -0.5

    def body(q_ref, k_ref, v_ref, if_ref, cst_ref, nm_ref, dh_ref, dqk_ref, dv_ref, dif_ref, dc_sc, dn_sc):
        t = pl.program_id(1)
        ci = nch - 1 - t

        @pl.when(t == 0)
        def _():
            dc_sc[...] = jnp.zeros_like(dc_sc)
            dn_sc[...] = jnp.zeros_like(dn_sc)

        rowid = lax.broadcasted_iota(jnp.int32, (CHUNK, 1), 0)
        lane = lax.broadcasted_iota(jnp.int32, (CHUNK, 128), 1)
        padmask = (ci == 0) & (rowid < PAD_ROWS)
        ifv = if_ref[...]
        li_all, b_all = _mlstm_gates(ifv, padmask)
        db_all = jnp.zeros((CHUNK, 128), F32)
        dli_all = jnp.zeros((CHUNK, 128), F32)
        for hd in range(M_HEADS):
            q = q_ref[:, hd * M_DK:(hd + 1) * M_DK]
            k = k_ref[:, hd * M_DK:(hd + 1) * M_DK] * scale
            v = v_ref[:, hd * M_DV:(hd + 1) * M_DV]
            c_st = cst_ref[0, hd * M_DK:(hd + 1) * M_DK, :]
            n_st = nm_ref[0, hd:hd + 1, :]
            m_st = nm_ref[0, M_HEADS + hd:M_HEADS + hd + 1, 0:1]
            o = _mlstm_head_fwd(q, k, v, _lane_col(li_all, hd), _lane_col(b_all, M_HEADS + hd), c_st, n_st, m_st)
            dc_new = dc_sc[hd * M_DK:(hd + 1) * M_DK, :]
            dn_new = dn_sc[hd:hd + 1, :]
            dcb = dc_new.astype(BF16)
            dhh = dh_ref[:, hd * M_DV:(hd + 1) * M_DV]
            dnum = dhh * o["rinv"]
            dhh_h = jnp.sum(dhh * o["h"], axis=1, keepdims=True)
            sgn = jnp.where(o["den"] >= 0.0, 1.0, -1.0)
            dden = jnp.where(jnp.abs(o["den"]) > o["floor"], -dhh_h * o["rinv"] * sgn, 0.0)
            dnb = dnum.astype(BF16)
            ds = _dot(dnb, o["vb"], 1, 1) + dden
            sb = o["s"].astype(BF16)
            kwb = o["kw"].astype(BF16)
            dv = _dot(sb, dnb, 0, 0) + _dot(kwb, dcb, 1, 0)
            dqk_m = (ds * o["w_intra"]).astype(BF16)
            wdn = o["w_inter"] * dnum
            wdd = o["w_inter"] * dden
            dq = _dot(dqk_m, o["kb"], 1, 0) + _dot(wdn.astype(BF16), o["cb"], 1, 1) + wdd * n_st
            vdc = _dot(o["vb"], dcb, 1, 1)
            dk = _dot(dqk_m, o["qb"], 0, 0) + o["w_k"] * (vdc + dn_new)
            dd = ds * o["s"]
            dd_col = _to_col(jnp.sum(dd, axis=0, keepdims=True))
            dmi = o["w_inter"] * (jnp.sum(dnum * o["qc"], axis=1, keepdims=True) + dden * o["qn"])
            dg = o["w_k"] * (jnp.sum(k * vdc, axis=1, keepdims=True) + jnp.sum(k * dn_new, axis=1, keepdims=True))
            d_blast = (o["decay"] * (jnp.sum(jnp.sum(dc_new * c_st, axis=1, keepdims=True), axis=0, keepdims=True)
                                     + jnp.sum(dn_new * n_st, axis=1, keepdims=True))
                       + jnp.sum(dg, axis=0, keepdims=True))
            db_col = jnp.sum(dd, axis=1, keepdims=True) - dd_col + dmi - dg
            db_col = db_col + jnp.where(rowid == CHUNK - 1, d_blast, 0.0)
            dli_col = dd_col + dg
            db_all = db_all + jnp.where(lane == M_HEADS + hd, db_col, 0.0)
            dli_all = dli_all + jnp.where(lane == hd, dli_col, 0.0)
            dc_sc[hd * M_DK:(hd + 1) * M_DK, :] = o["decay"] * dc_new + _dot(o["qb"], wdn.astype(BF16), 0, 0)
            dn_sc[hd:hd + 1, :] = o["decay"] * dn_new + jnp.sum(q * wdd, axis=0, keepdims=True)
            dqk_ref[:, hd * M_DK:(hd + 1) * M_DK] = dq
            dqk_ref[:, 512 + hd * M_DK:512 + (hd + 1) * M_DK] = dk * scale
            dv_ref[:, hd * M_DV:(hd + 1) * M_DV] = dv.astype(BF16)
        dlf = _tri_sum(db_all, True)
        dif = dli_all + dlf * _sigmoid(-ifv)
        dif_ref[...] = jnp.where(padmask | (lane >= 2 * M_HEADS), 0.0, dif).astype(BF16)

    rev = lambda bb, t: (bb * nch + nch - 1 - t, 0)
    rev3 = lambda bb, t: (bb * nch + nch - 1 - t, 0, 0)
    return pl.pallas_call(
        body, name="mlstm_bwd", grid=(B, nch),
        in_specs=[pl.BlockSpec((CHUNK, 512), rev),
                  pl.BlockSpec((CHUNK, 512), lambda bb, t: (bb * nch + nch - 1 - t, 1)),
                  pl.BlockSpec((CHUNK, 1024), lambda bb, t: (bb * nch + nch - 1 - t, P_V // 1024)),
                  pl.BlockSpec((CHUNK, 128), lambda bb, t: (bb * nch + nch - 1 - t, P_IF // 128)),
                  pl.BlockSpec((1, M_HEADS * M_DK, M_DV), rev3),
                  pl.BlockSpec((1, 8, 128), rev3),
                  pl.BlockSpec((CHUNK, 1024), rev)],
        out_specs=(pl.BlockSpec((CHUNK, 1024), rev),
                   pl.BlockSpec((CHUNK, 1024), rev),
                   pl.BlockSpec((CHUNK, 128), rev)),
        out_shape=(jax.ShapeDtypeStruct((T, 1024), F32),
                   jax.ShapeDtypeStruct((T, 1024), BF16),
                   jax.ShapeDtypeStruct((T, 128), BF16)),
        scratch_shapes=[pltpu.VMEM((M_HEADS * M_DK, M_DV), F32), pltpu.VMEM((8, 128), F32)],
        compiler_params=_cparams(("arbitrary", "arbitrary")),
    )(qk, qk, p, p, cst, nm, dh)


def _headnorm_fwd(hm, p, g):
    T = hm.shape[0]
    tr = _pick(T, (320, 256, 128, 64))

    def body(h_ref, o_ref, g_ref, a_ref):
        for hd in range(M_HEADS):
            sl = slice(hd * M_DV, (hd + 1) * M_DV)
            hn = _ln_rows(h_ref[:, sl], g_ref[:, sl], 0.0)
            a_ref[:, sl] = (_sigmoid(o_ref[:, sl]) * hn).astype(BF16)

    return pl.pallas_call(
        body, name="headnorm_fwd", grid=(T // tr,),
        in_specs=[pl.BlockSpec((tr, 1024), lambda i: (i, 0)),
                  pl.BlockSpec((tr, 1024), lambda i: (i, P_O // 1024)),
                  pl.BlockSpec((1, 1024), lambda i: (0, 0))],
        out_specs=pl.BlockSpec((tr, 1024), lambda i: (i, 0)),
        out_shape=jax.ShapeDtypeStruct((T, 1024), BF16),
        compiler_params=_cparams(("arbitrary",)),
    )(hm, p, g)


def _headnorm_bwd(da, hm, p, g):
    T = hm.shape[0]
    tr = _pick(T, (320, 256, 128, 64))

    def body(da_ref, h_ref, o_ref, g_ref, dh_ref, do_ref, dg_ref):
        @pl.when(pl.program_id(0) == 0)
        def _():
            dg_ref[...] = jnp.zeros_like(dg_ref)

        for hd in range(M_HEADS):
            sl = slice(hd * M_DV, (hd + 1) * M_DV)
            gg = g_ref[:, sl]
            so = _sigmoid(o_ref[:, sl])
            da = da_ref[:, sl]
            dhn = da * so
            dv, xhat = _ln_bwd_rows(dhn, h_ref[:, sl], gg)
            dh_ref[:, sl] = dv
            do_ref[:, sl] = (da * (xhat * gg) * so * (1.0 - so)).astype(BF16)
            dg_ref[0:1, sl] += jnp.sum(dhn * xhat, axis=0, keepdims=True)

    row = pl.BlockSpec((tr, 1024), lambda i: (i, 0))
    return pl.pallas_call(
        body, name="headnorm_bwd", grid=(T // tr,),
        in_specs=[row, row, pl.BlockSpec((tr, 1024), lambda i: (i, P_O // 1024)),
                  pl.BlockSpec((1, 1024), lambda i: (0, 0))],
        out_specs=(row, row, pl.BlockSpec((8, 1024), lambda i: (0, 0))),
        out_shape=(jax.ShapeDtypeStruct((T, 1024), F32), jax.ShapeDtypeStruct((T, 1024), BF16),
                   jax.ShapeDtypeStruct((8, 1024), F32)),
        compiler_params=_cparams(("arbitrary",)),
    )(da, hm, p, g)


def _mix_fwd(z, ym, p):
    T = ym.shape[0]
    tr = _pick(T, (320, 256, 128, 64))

    def body(z1_ref, z2_ref, ym_ref, gs_ref, gm_ref, o_ref):
        ys = z1_ref[...] * _sigmoid(z2_ref[...])
        o_ref[...] = (_sigmoid(gs_ref[...]) * ys + _sigmoid(gm_ref[...]) * ym_ref[...]).astype(BF16)

    def col(cb):
        return pl.BlockSpec((tr, 1024), lambda i: (i, cb))

    return pl.pallas_call(
        body, name="mix_fwd", grid=(T // tr,),
        in_specs=[col(0), col(1), col(0), col(P_GS // 1024), col(P_GM // 1024)],
        out_specs=col(0),
        out_shape=jax.ShapeDtypeStruct((T, 1024), BF16),
        compiler_params=_cparams(("arbitrary",)),
    )(z, z, ym, p, p)


def _mix_bwd(dmix, z, ym, p):
    T = ym.shape[0]
    tr = _pick(T, (320, 256, 128, 64))

    def body(d_ref, z1_ref, z2_ref, ym_ref, gs_ref, gm_ref, dz_ref, dym_ref, dgs_ref, dgm_ref):
        d = d_ref[...]
        z1 = z1_ref[...]
        s2 = _sigmoid(z2_ref[...])
        ss = _sigmoid(gs_ref[...])
        sm = _sigmoid(gm_ref[...])
        ys = z1 * s2
        dys = d * ss
        dgs_ref[...] = (d * ys * ss * (1.0 - ss)).astype(BF16)
        dgm_ref[...] = (d * ym_ref[...] * sm * (1.0 - sm)).astype(BF16)
        dym_ref[...] = (d * sm).astype(BF16)
        dz_ref[:, 0:1024] = (dys * s2).astype(BF16)
        dz_ref[:, 1024:2048] = (dys * z1 * s2 * (1.0 - s2)).astype(BF16)

    def col(cb):
        return pl.BlockSpec((tr, 1024), lambda i: (i, cb))

    o = jax.ShapeDtypeStruct((T, 1024), BF16)
    return pl.pallas_call(
        body, name="mix_bwd", grid=(T // tr,),
        in_specs=[col(0), col(0), col(1), col(0), col(P_GS // 1024), col(P_GM // 1024)],
        out_specs=(pl.BlockSpec((tr, 2048), lambda i: (i, 0)),) + (col(0),) * 3,
        out_shape=(jax.ShapeDtypeStruct((T, 2048), BF16),) + (o,) * 3,
        compiler_params=_cparams(("arbitrary",)),
    )(dmix, z, z, ym, p, p)


def _adamw_math(w, g, m, v):
    m2 = ADAM_B1 * m + (1.0 - ADAM_B1) * g
    v2 = ADAM_B2 * v + (1.0 - ADAM_B2) * jnp.square(g)
    m_hat = m2 / (1.0 - ADAM_B1 ** ADAM_STEP)
    v_hat = v2 / (1.0 - ADAM_B2 ** ADAM_STEP)
    delta = -ADAM_LR * (m_hat / (jnp.sqrt(v_hat) + ADAM_EPS) + ADAM_WD * w)
    return delta, m2, v2


def _adamw_big(recv, w, m, v, name):
    R, C = w.shape
    tr = _pick(R, (128, 64, 32, 16))

    def body(r_ref, w_ref, m_ref, v_ref, g_ref, d_ref, m2_ref, v2_ref):
        g = r_ref[0].astype(F32)
        for s in range(1, N_DEV):
            g = g + r_ref[s].astype(F32)
        d, m2, v2 = _adamw_math(w_ref[...], g, m_ref[...], v_ref[...])
        g_ref[...] = g
        d_ref[...] = d
        m2_ref[...] = m2
        v2_ref[...] = v2

    row = pl.BlockSpec((tr, C), lambda i: (i, 0))
    o = jax.ShapeDtypeStruct((R, C), F32)
    return pl.pallas_call(
        body, name=name, grid=(R // tr,),
        in_specs=[pl.BlockSpec((N_DEV, tr, C), lambda i: (0, i, 0)), row, row, row],
        out_specs=(row,) * 4, out_shape=(o,) * 4,
        compiler_params=_cparams(("arbitrary",)),
    )(recv, w, m, v)


def _adamw_small(g, w, m, v):
    R = g.shape[0]

    def body(g_ref, w_ref, m_ref, v_ref, d_ref, m2_ref, v2_ref):
        d, m2, v2 = _adamw_math(w_ref[...], g_ref[...], m_ref[...], v_ref[...])
        d_ref[...] = d
        m2_ref[...] = m2
        v2_ref[...] = v2

    full = pl.BlockSpec((R, 128), lambda i: (0, 0))
    o = jax.ShapeDtypeStruct((R, 128), F32)
    return pl.pallas_call(
        body, name="adamw_small", grid=(1,),
        in_specs=[full] * 4, out_specs=(full,) * 3, out_shape=(o,) * 3,
        compiler_params=_cparams(("arbitrary",)),
    )(g, w, m, v)


def _pack(arrs):
    parts = []
    for a in arrs:
        f = a.reshape(-1).astype(F32)
        n = -(-f.shape[0] // 1024) * 1024
        parts.append(jnp.pad(f, (0, n - f.shape[0])))
    return jnp.concatenate(parts).reshape(-1, 128)


def _unpack(pack, shapes):
    flat = pack.reshape(-1)
    out, off = [], 0
    for shp in shapes:
        n = math.prod(shp)
        out.append(flat[off:off + n].reshape(shp))
        off += -(-n // 1024) * 1024
    return out


def _cols_from_shards(g):
    return jnp.transpose(g, (1, 0, 2)).reshape(g.shape[1], -1)


def _cols_to_shards(w):
    R = w.shape[0]
    return jnp.transpose(w.reshape(R, N_DEV, -1), (1, 0, 2))


def _nat_to_aligned(w):
    pad = jnp.zeros(w.shape[:-1] + (PW - P_IF - 8,), w.dtype)
    return jnp.concatenate([
        w[..., N_V:N_V + 1024], w[..., N_O:N_O + 1024], w[..., N_GS:N_GS + 1024], w[..., N_GM:N_GM + 1024],
        w[..., N_Q:N_Q + 1024], w[..., N_U:N_U + 512], w[..., N_I:N_I + 8], pad], axis=-1)


def _aligned_to_nat(w):
    return jnp.concatenate([
        w[..., P_U:P_U + 512], w[..., P_QK:P_QK + 1024], w[..., P_V:P_V + 1024], w[..., P_O:P_O + 1024],
        w[..., P_IF:P_IF + 8], w[..., P_GS:P_GS + 1024], w[..., P_GM:P_GM + 1024]], axis=-1)


def kernel(x, meta_tokens, ln0_g, ln0_b, w_in, b_in, qk_conv_w, qk_conv_b, s5_lambda_re, s5_lambda_im, s5_log_dt, s5_b_re, s5_b_im, s5_c_re, s5_c_im, s5_d, s5_w_glu, m_norm_g, m_w_out, w_o, ln1_g, ln1_b, w_up, b_up, w_down, ln2_g, ln2_b, loss_target, m_meta_tokens, m_ln0_g, m_ln0_b, m_w_in, m_b_in, m_qk_conv_w, m_qk_conv_b, m_s5_lambda_re, m_s5_lambda_im, m_s5_log_dt, m_s5_b_re, m_s5_b_im, m_s5_c_re, m_s5_c_im, m_s5_d, m_s5_w_glu, m_m_norm_g, m_m_w_out, m_w_o, m_ln1_g, m_ln1_b, m_w_up, m_b_up, m_w_down, m_ln2_g, m_ln2_b, v_meta_tokens, v_ln0_g, v_ln0_b, v_w_in, v_b_in, v_qk_conv_w, v_qk_conv_b, v_s5_lambda_re, v_s5_lambda_im, v_s5_log_dt, v_s5_b_re, v_s5_b_im, v_s5_c_re, v_s5_c_im, v_s5_d, v_s5_w_glu, v_m_norm_g, v_m_w_out, v_w_o, v_ln1_g, v_ln1_b, v_w_up, v_b_up, v_w_down, v_ln2_g, v_ln2_b):
    B, S, D = x.shape
    lp = S + CHUNK
    me = _my_id()

    big = [w_in[0], s5_w_glu[0], m_w_out[0], w_o[0], w_up[0], w_down[0]]
    gathered = _exchange([a.astype(BF16) for a in big] + [meta_tokens, qk_conv_w[0]], "allgather_weights", True)
    win_f = _nat_to_aligned(_cols_from_shards(gathered[0]))
    wglu_f = _cols_from_shards(gathered[1])
    wmo_f = gathered[2].reshape(1024, 1024)
    wo_f = gathered[3].reshape(1024, 1024)
    wup_f = _cols_from_shards(gathered[4])
    wdown_f = gathered[5].reshape(D_FF, 1024)
    meta_f = _cols_from_shards(gathered[6])
    convw_f = _cols_from_shards(gathered[7])
    b_in_al = _nat_to_aligned(b_in)
    ln0g, ln0b = ln0_g.reshape(1, D), ln0_b.reshape(1, D)

    s5_args = (s5_lambda_re[0], s5_lambda_im[0], s5_log_dt[0], s5_b_re[0], s5_b_im[0], s5_c_re[0], s5_c_im[0])
    (ar, ai, bblk, cblk), s5_vjp = jax.vjp(_s5_prep, *s5_args)
    tab_f, tab_b = _scan_tables(ar, ai)

    h0, h0b = _ln0_fwd(x, meta_f, ln0g, ln0b)
    p = _mm(h0b, win_f, "nn", "mm_in", bias=b_in_al)
    y_pre, gy, cin = _s5_fwd(p, bblk, cblk, tab_f, s5_d, lp)
    z = _mm(gy, wglu_f, "nn", "mm_glu")
    qk = _conv_fwd(p, convw_f, qk_conv_b, lp)
    hm, cst, nm = _mlstm_fwd(p, qk, lp)
    a_m = _headnorm_fwd(hm, p, m_norm_g)
    ym = _mm(a_m, wmo_f, "nn", "mm_mout")
    mix = _mix_fwd(z, ym, p)
    r1 = _mm(mix, wo_f, "nn", "mm_o")
    pre1, h1, h1b = _ln_res_fwd(h0, r1, ln1_g, ln1_b, "ln1_fwd")
    up, act = _mm(h1b, wup_f, "nn", "mm_up", bias=b_up, act_out=True)
    ff = _mm(act, wdown_f, "nn", "mm_down")
    loss_acc, dpre2, dpre2b, dg2, db2 = _ln2_loss(h1, ff, ln2_g, ln2_b, loss_target)

    d_up = _mm(dpre2b, wdown_f, "nt", "mm_d_act", relu2_grad_of=up, out_dtype=BF16)
    g_wdown = _mm(act, dpre2b, "tn", "mm_g_wdown")
    g_wup, cs_up = _mm(h1b, d_up, "tn", "mm_g_wup", colsum=True)
    dh1 = _mm(d_up, wup_f, "nt", "mm_d_h1", add=dpre2, add_scale=ALPHA)
    dpre1, dpre1b, dg1, db1 = _ln_bwd(dh1, pre1, ln1_g, "ln1_bwd")
    g_wo = _mm(mix, dpre1b, "tn", "mm_g_wo")
    dmix = _mm(dpre1b, wo_f, "nt", "mm_d_mix")
    dz, dym, dgs, dgm = _mix_bwd(dmix, z, ym, p)
    g_wmo = _mm(a_m, dym, "tn", "mm_g_wmo")
    da = _mm(dym, wmo_f, "nt", "mm_d_a")
    dhm, do_pre, dg_norm = _headnorm_bwd(da, hm, p, m_norm_g)
    dqk, dv, dif = _mlstm_bwd(p, qk, cst, nm, dhm, lp)
    dqk_pre, dconv_w, dconv_b = _conv_bwd(p, dqk, convw_f, qk_conv_b, lp)
    g_wglu = _mm(gy, dz, "tn", "mm_g_wglu")
    dgy = _mm(dz, wglu_f, "nt", "mm_d_gy")
    du, dbblk, dcblk, dlam, dd = _s5_bwd(p, y_pre, dgy, cin, bblk, cblk, tab_f, tab_b, s5_d, lp)
    dp = jnp.concatenate([dv, do_pre, dgs, dgm, dqk_pre, du, dif], axis=1)
    g_win, cs_in = _mm(h0b, dp, "tn", "mm_g_win", colsum=True)
    dh0 = _mm(dp, win_f, "nt", "mm_d_h0", add=dpre1, add_scale=ALPHA)
    grad_x, dmeta, dg0, db0 = _ln0_bwd(dh0, x, meta_f, ln0g)

    dlam2 = jnp.sum(dlam, axis=1)
    s5_grads = s5_vjp((dlam2[0:1], dlam2[1:2], dbblk, dcblk))
    small_local = [
        loss_acc[0:1, 0:1], dg0[0:1], db0[0:1], _aligned_to_nat(cs_in[0:1]), dconv_b[0:1],
        s5_grads[0], s5_grads[1], s5_grads[2], s5_grads[3], s5_grads[4], s5_grads[5], s5_grads[6],
        dd[0:1], dg_norm[0:1], dg1[0:1], db1[0:1], cs_up[0:1], dg2[0:1], db2[0:1],
        dmeta, dconv_w[0:4]]
    small_shapes = [(), (D,), (D,), (1, IN_NAT), (1, 1024),
                    (1, 32, 64), (1, 32, 64), (1, 32), (1, 32, 64, 16), (1, 32, 64, 16), (1, 32, 16, 64), (1, 32, 16, 64),
                    (1, 512), (1, 1024), (1, 1024), (1, 1024), (1, D_FF), (1, 1024), (1, 1024),
                    (N_META, D), (4, 1024)]
    red = _unpack(_allreduce_small(_pack(small_local)), small_shapes)
    loss = red[0]
    g_meta = lax.dynamic_slice_in_dim(red[19], me * 128, 128, axis=1)
    g_convw = lax.dynamic_slice_in_dim(red[20], me * 128, 128, axis=1)[None]
    small_g = red[1:19] + [g_meta, g_convw]
    small_w = [ln0_g, ln0_b, b_in, qk_conv_b, s5_lambda_re, s5_lambda_im, s5_log_dt, s5_b_re, s5_b_im,
               s5_c_re, s5_c_im, s5_d, m_norm_g, ln1_g, ln1_b, b_up, ln2_g, ln2_b, meta_tokens, qk_conv_w]
    small_m = [m_ln0_g, m_ln0_b, m_b_in, m_qk_conv_b, m_s5_lambda_re, m_s5_lambda_im, m_s5_log_dt, m_s5_b_re,
               m_s5_b_im, m_s5_c_re, m_s5_c_im, m_s5_d, m_m_norm_g, m_ln1_g, m_ln1_b, m_b_up, m_ln2_g, m_ln2_b,
               m_meta_tokens, m_qk_conv_w]
    small_v = [v_ln0_g, v_ln0_b, v_b_in, v_qk_conv_b, v_s5_lambda_re, v_s5_lambda_im, v_s5_log_dt, v_s5_b_re,
               v_s5_b_im, v_s5_c_re, v_s5_c_im, v_s5_d, v_m_norm_g, v_ln1_g, v_ln1_b, v_b_up, v_ln2_g, v_ln2_b,
               v_meta_tokens, v_qk_conv_w]
    shapes_w = [tuple(w.shape) for w in small_w]
    small_g = [g.reshape(s) for g, s in zip(small_g, shapes_w)]
    sd, sm2, sv2 = _adamw_small(_pack(small_g), _pack(small_w), _pack(small_m), _pack(small_v))
    sd, sm2, sv2 = _unpack(sd, shapes_w), _unpack(sm2, shapes_w), _unpack(sv2, shapes_w)

    parts = [_cols_to_shards(_aligned_to_nat(g_win)), _cols_to_shards(g_wglu), g_wmo.reshape(N_DEV, 128, 1024),
             g_wo.reshape(N_DEV, 128, 1024), _cols_to_shards(g_wup), g_wdown.reshape(N_DEV, 512, 1024)]
    recv = _exchange([a.astype(BF16) for a in parts], "exchange_grads", False)
    big_m = [m_w_in[0], m_s5_w_glu[0], m_m_w_out[0], m_w_o[0], m_w_up[0], m_w_down[0]]
    big_v = [v_w_in[0], v_s5_w_glu[0], v_m_w_out[0], v_w_o[0], v_w_up[0], v_w_down[0]]
    names = ["w_in", "s5_w_glu", "m_w_out", "w_o", "w_up", "w_down"]
    big_out = [_adamw_big(r, w, m, v, "adamw_" + nm_) for r, w, m, v, nm_ in zip(recv, big, big_m, big_v, names)]

    order = ["meta_tokens", "ln0_g", "ln0_b", "w_in", "b_in", "qk_conv_w", "qk_conv_b", "s5_lambda_re", "s5_lambda_im",
             "s5_log_dt", "s5_b_re", "s5_b_im", "s5_c_re", "s5_c_im", "s5_d", "s5_w_glu", "m_norm_g", "m_w_out", "w_o",
             "ln1_g", "ln1_b", "w_up", "b_up", "w_down", "ln2_g", "ln2_b"]
    small_names = ["ln0_g", "ln0_b", "b_in", "qk_conv_b", "s5_lambda_re", "s5_lambda_im", "s5_log_dt", "s5_b_re",
                   "s5_b_im", "s5_c_re", "s5_c_im", "s5_d", "m_norm_g", "ln1_g", "ln1_b", "b_up", "ln2_g", "ln2_b",
                   "meta_tokens", "qk_conv_w"]
    res = {}
    for i, n in enumerate(small_names):
        res[n] = (small_g[i], sd[i], sm2[i], sv2[i])
    for i, n in enumerate(names):
        res[n] = tuple(o[None] for o in big_out[i])
    outs = [loss, grad_x]
    for kind in range(4):
        outs += [res[n][kind] for n in order]
    return tuple(outs)
```

```python
import functools
import math

import jax
import jax.numpy as jnp
from jax import lax
from jax.experimental import pallas as pl
from jax.experimental.pallas import tpu as pltpu

F32 = jnp.float32
BF16 = jnp.bfloat16

D_MODEL = 1024
N_META = 16
CHUNK = 64
PAD_ROWS = CHUNK - N_META
S5_WIDTH = 512
S5_GROUP = 16
S5_GROUPS = 32
S5_STATE = 64
S5_COLS = 2 * S5_GROUPS * S5_STATE
S5_BLK = 4
M_HEADS = 4
M_DK = 128
M_DV = 256
D_FF = 4096
N_DEV = 8
ALPHA = 2.0 ** 0.25
LN_EPS = 1e-5
IN_NAT = 5640
P_V, P_O, P_GS, P_GM, P_QK, P_U, P_IF, PW = 0, 1024, 2048, 3072, 4096, 5120, 5632, 5760
N_U, N_Q, N_K, N_V, N_O, N_I, N_GS, N_GM = 0, 512, 1024, 1536, 2560, 3584, 3592, 4616

ADAM_LR, ADAM_B1, ADAM_B2, ADAM_EPS, ADAM_WD, ADAM_STEP = 0.001, 0.9, 0.999, 1e-08, 0.01, 10

VMEM_LIMIT = 56 * 1024 * 1024
MESH = pl.DeviceIdType.MESH


def _pick(n, cands):
    for c in cands:
        if n % c == 0:
            return c
    raise ValueError(f"no tile for {n} among {cands}")


def _cparams(sem):
    return pltpu.CompilerParams(dimension_semantics=sem, vmem_limit_bytes=VMEM_LIMIT)


def _dot(a, b, ca, cb):
    return lax.dot_general(a, b, (((ca,), (cb,)), ((), ())), preferred_element_type=F32)


def _sigmoid(x):
    return 1.0 / (1.0 + jnp.exp(-x))


def _peer(k):
    x, y, c = lax.axis_index("x"), lax.axis_index("y"), lax.axis_index("c")
    px = 1 - x if k & 4 else x
    py = 1 - y if k & 2 else y
    pc = 1 - c if k & 1 else c
    return (px, py, pc), 4 * px + 2 * py + pc


def _my_id():
    return 4 * lax.axis_index("x") + 2 * lax.axis_index("y") + lax.axis_index("c")


def _hbm_call(body, name, arrs, out_shape, n_remote):
    n = len(arrs)
    return pl.pallas_call(
        body, name=name,
        out_shape=tuple(out_shape),
        in_specs=[pl.BlockSpec(memory_space=pl.ANY)] * n,
        out_specs=tuple([pl.BlockSpec(memory_space=pl.ANY)] * len(out_shape)),
        scratch_shapes=[pltpu.SemaphoreType.DMA((n, n_remote)),
                        pltpu.SemaphoreType.DMA((n, n_remote)),
                        pltpu.SemaphoreType.DMA((n,))],
    )(*arrs)


def _allgather(arrs):
    n = len(arrs)

    def body(*refs):
        ins, outs = refs[:n], refs[n:2 * n]
        send_sems, recv_sems, local_sems = refs[2 * n:]
        x, y, c = lax.axis_index("x"), lax.axis_index("y"), lax.axis_index("c")
        me, sibling = (x, y, c), (x, y, 1 - c)
        chips = [(1 - x, y), (x, 1 - y), (1 - x, 1 - y)]

        def slot(a, dev):
            return outs[a].at[4 * dev[0] + 2 * dev[1] + dev[2]]

        def copy(a, k, block, to, src=None):
            return pltpu.make_async_remote_copy(
                src_ref=slot(a, block) if src is None else src, dst_ref=slot(a, block),
                send_sem=send_sems.at[a, k], recv_sem=recv_sems.at[a, k],
                device_id=to, device_id_type=MESH)

        started = []
        for a in range(n):
            cp = pltpu.make_async_copy(ins[a], slot(a, me), local_sems.at[a])
            cp.start()
            started.append(cp)
        sends = []
        for a in range(n):
            sends.append(copy(a, 0, me, sibling, src=ins[a]))
            for j, chip in enumerate(chips):
                sends.append(copy(a, 1 + j, me, (*chip, c), src=ins[a]))
        for cp in sends:
            cp.start()
        for j, chip in enumerate(chips):
            for a in range(n):
                copy(a, 1 + j, (*chip, c), me).wait_recv()
                fwd = copy(a, 4 + j, (*chip, c), sibling)
                fwd.start()
                sends.append(fwd)
        for a in range(n):
            copy(a, 0, sibling, me).wait_recv()
            for j, chip in enumerate(chips):
                copy(a, 4 + j, (*chip, 1 - c), me).wait_recv()
        for cp in sends:
            cp.wait_send()
        for cp in started:
            cp.wait()

    out_shape = [jax.ShapeDtypeStruct((N_DEV,) + tuple(a.shape), a.dtype) for a in arrs]
    return _hbm_call(body, "allgather_weights", arrs, out_shape, N_DEV - 1)


def _pair_exchange(arrs):
    n = len(arrs)

    def body(*refs):
        ins, outs = refs[:n], refs[n:2 * n]
        send_sems, recv_sems, _ = refs[2 * n:]
        x, y, c = lax.axis_index("x"), lax.axis_index("y"), lax.axis_index("c")
        cps = []
        for a in range(n):
            cp = pltpu.make_async_remote_copy(
                src_ref=ins[a].at[pl.ds(0, 4), 1 - c], dst_ref=outs[a],
                send_sem=send_sems.at[a, 0], recv_sem=recv_sems.at[a, 0],
                device_id=(x, y, 1 - c), device_id_type=MESH)
            cp.start()
            cps.append(cp)
        for cp in cps:
            cp.wait()

    out_shape = [jax.ShapeDtypeStruct((4,) + tuple(a.shape[2:]), a.dtype) for a in arrs]
    return _hbm_call(body, "pair_exchange", arrs, out_shape, 1)


def _pair_sum(g4, sib, name):
    _, _, R, C = g4.shape
    tr, tc = _shard_tile(R, C)

    def body(c_ref, g_ref, s_ref, o_ref):
        o_ref[...] = (g_ref[:, 0].astype(F32) + s_ref[...].astype(F32)).astype(BF16)

    return pl.pallas_call(
        body, name=name,
        grid_spec=pltpu.PrefetchScalarGridSpec(
            num_scalar_prefetch=1, grid=(R // tr, C // tc),
            in_specs=[pl.BlockSpec((4, 1, tr, tc), lambda i, j, c_ref: (0, c_ref[0], i, j)),
                      pl.BlockSpec((4, tr, tc), lambda i, j, c_ref: (0, i, j))],
            out_specs=pl.BlockSpec((4, tr, tc), lambda i, j, c_ref: (0, i, j))),
        out_shape=jax.ShapeDtypeStruct((4, R, C), BF16),
        compiler_params=_cparams(("arbitrary", "arbitrary")),
    )(lax.axis_index("c").astype(jnp.int32).reshape(1), g4, sib)


def _chip_exchange(arrs):
    n = len(arrs)

    def body(*refs):
        ins, outs = refs[:n], refs[n:2 * n]
        send_sems, recv_sems, local_sems = refs[2 * n:]
        x, y, c = lax.axis_index("x"), lax.axis_index("y"), lax.axis_index("c")
        myq = 2 * x + y
        chips = [(1 - x, y), (x, 1 - y), (1 - x, 1 - y)]
        started, sends = [], []
        for a in range(n):
            cp = pltpu.make_async_copy(ins[a].at[myq], outs[a].at[myq], local_sems.at[a])
            cp.start()
            started.append(cp)
            for j, chip in enumerate(chips):
                cp = pltpu.make_async_remote_copy(
                    src_ref=ins[a].at[2 * chip[0] + chip[1]], dst_ref=outs[a].at[myq],
                    send_sem=send_sems.at[a, j], recv_sem=recv_sems.at[a, j],
                    device_id=(*chip, c), device_id_type=MESH)
                cp.start()
                sends.append(cp)
        for a in range(n):
            for j, chip in enumerate(chips):
                q = 2 * chip[0] + chip[1]
                pltpu.make_async_remote_copy(
                    src_ref=ins[a].at[q], dst_ref=outs[a].at[q],
                    send_sem=send_sems.at[a, j], recv_sem=recv_sems.at[a, j],
                    device_id=(*chip, c), device_id_type=MESH).wait_recv()
        for cp in sends:
            cp.wait_send()
        for cp in started:
            cp.wait()

    out_shape = [jax.ShapeDtypeStruct(tuple(a.shape), a.dtype) for a in arrs]
    return _hbm_call(body, "chip_exchange", arrs, out_shape, 3)


def _shard_tile(R, C):
    if R % 128 == 0:
        return 128, C
    return R, _pick(C, (256, 128))


def _allreduce_small(pack):
    rows = pack.shape[0]

    def body(in_ref, out_ref, buf, send_sems, recv_sems):
        me = _my_id()
        sends = []
        for k in range(1, N_DEV):
            peer, pid = _peer(k)
            cp = pltpu.make_async_remote_copy(
                src_ref=in_ref, dst_ref=buf.at[me],
                send_sem=send_sems.at[k - 1], recv_sem=recv_sems.at[k - 1],
                device_id=peer, device_id_type=MESH)
            cp.start()
            sends.append(cp)
        for k in range(1, N_DEV):
            peer, pid = _peer(k)
            pltpu.make_async_remote_copy(
                src_ref=in_ref, dst_ref=buf.at[pid],
                send_sem=send_sems.at[k - 1], recv_sem=recv_sems.at[k - 1],
                device_id=peer, device_id_type=MESH).wait_recv()
        for cp in sends:
            cp.wait_send()
        buf[pl.ds(me, 1)] = in_ref[...][None]
        acc = buf[0]
        for s in range(1, N_DEV):
            acc = acc + buf[s]
        out_ref[...] = acc

    return pl.pallas_call(
        body, name="allreduce_small",
        out_shape=jax.ShapeDtypeStruct((rows, 128), F32),
        in_specs=[pl.BlockSpec(memory_space=pltpu.VMEM)],
        out_specs=pl.BlockSpec(memory_space=pltpu.VMEM),
        scratch_shapes=[pltpu.VMEM((N_DEV, rows, 128), F32),
                        pltpu.SemaphoreType.DMA((N_DEV - 1,)),
                        pltpu.SemaphoreType.DMA((N_DEV - 1,))],
        compiler_params=pltpu.CompilerParams(vmem_limit_bytes=VMEM_LIMIT),
    )(pack)


def _mm_tiles(M, N, K, mode, out_bytes, n_extra_f32, a_bytes, b_bytes):
    budget = 40 * 1024 * 1024
    tms = [t for t in (1664, 1152, 1024, 640, 512, 256, 128) if M % t == 0]
    tns = [t for t in (1152, 1024, 640, 512, 384, 256, 128) if N % t == 0]
    if mode == "tn":
        tks = [t for t in (1664, 640, 512, 256, 128) if K % t == 0]
    else:
        tks = [K] if K <= 1152 else [t for t in (1152, 1024, 640, 512) if K % t == 0]
    best = None
    for tm in tms:
        for tn in tns:
            for tk in tks:
                nk = K // tk
                use = 2 * (tm * tk * a_bytes + tk * tn * b_bytes) + 2 * tm * tn * out_bytes
                use += 2 * n_extra_f32 * tm * tn * 4 + tm * tn * 4 * (2 if nk > 1 else 1)
                if use > budget:
                    continue
                score = (tm * tn * tk, tm * tn)
                if best is None or score > best[0]:
                    best = (score, (tm, tn, tk))
    assert best is not None, (M, N, K, mode)
    return best[1]


def _mm(a, b, mode, name, *, bias=None, add=None, add_scale=1.0, sqrt_gate_of=None,
        relu2_out=False, colsum=None, out_dtype=F32):
    if mode == "nn":
        (M, K), (K2, N) = a.shape, b.shape
    elif mode == "nt":
        (M, K), (N, K2) = a.shape, b.shape
    else:
        (K, M), (K2, N) = a.shape, b.shape
    assert K == K2, (a.shape, b.shape, mode)
    has_bias, has_add, has_gate = bias is not None, add is not None, sqrt_gate_of is not None
    tm, tn, tk = _mm_tiles(M, N, K, mode, jnp.dtype(out_dtype).itemsize, int(has_add) + int(has_gate),
                           a.dtype.itemsize, b.dtype.itemsize)
    nk = K // tk
    assert colsum is None or mode == "tn"
    assert colsum != "a" or N == tn

    def body(*refs):
        it = iter(refs)
        a_ref, b_ref = next(it), next(it)
        bias_ref = next(it) if has_bias else None
        add_ref = next(it) if has_add else None
        gate_ref = next(it) if has_gate else None
        o_ref = next(it)
        cs_ref = next(it) if colsum else None
        acc_ref = next(it) if nk > 1 else None
        i, k = pl.program_id(1), pl.program_id(2)

        av = a_ref[...].astype(BF16)
        bv = b_ref[...].astype(BF16)
        if mode == "nn":
            part = _dot(av, bv, 1, 0)
        elif mode == "nt":
            part = _dot(av, bv, 1, 1)
        else:
            part = _dot(av, bv, 0, 0)
        if colsum == "b":
            @pl.when((i == 0) & (k == 0))
            def _():
                cs_ref[...] = jnp.zeros_like(cs_ref)

            @pl.when(i == 0)
            def _():
                cs_ref[0:1, :] += jnp.sum(b_ref[...].astype(F32), axis=0, keepdims=True)
        if colsum == "a":
            @pl.when(k == 0)
            def _():
                cs_ref[...] = jnp.zeros_like(cs_ref)

            cs_ref[0:1, :] += jnp.sum(a_ref[...].astype(F32), axis=0, keepdims=True)

        def finish(r):
            if has_bias:
                r = r + bias_ref[...]
            if has_add:
                r = r + add_scale * add_ref[...]
            if has_gate:
                r = r * (2.0 * jnp.sqrt(gate_ref[...].astype(F32)))
            if relu2_out:
                r = jnp.square(jnp.maximum(r, 0.0))
            o_ref[...] = r.astype(out_dtype)

        if nk == 1:
            finish(part)
        else:
            @pl.when(k == 0)
            def _():
                acc_ref[...] = part

            @pl.when(k > 0)
            def _():
                acc_ref[...] += part

            @pl.when(k == nk - 1)
            def _():
                finish(acc_ref[...])

    if mode == "nn":
        a_spec = pl.BlockSpec((tm, tk), lambda j, i, k: (i, k))
        b_spec = pl.BlockSpec((tk, tn), lambda j, i, k: (k, j))
    elif mode == "nt":
        a_spec = pl.BlockSpec((tm, tk), lambda j, i, k: (i, k))
        b_spec = pl.BlockSpec((tn, tk), lambda j, i, k: (j, k))
    else:
        a_spec = pl.BlockSpec((tk, tm), lambda j, i, k: (k, i))
        b_spec = pl.BlockSpec((tk, tn), lambda j, i, k: (k, j))
    in_specs, args = [a_spec, b_spec], [a, b]
    if has_bias:
        in_specs.append(pl.BlockSpec((1, tn), lambda j, i, k: (0, j)))
        args.append(bias)
    if has_add:
        in_specs.append(pl.BlockSpec((tm, tn), lambda j, i, k: (i, j)))
        args.append(add)
    if has_gate:
        in_specs.append(pl.BlockSpec((tm, tn), lambda j, i, k: (i, j)))
        args.append(sqrt_gate_of)
    out_shape = [jax.ShapeDtypeStruct((M, N), out_dtype)]
    out_specs = [pl.BlockSpec((tm, tn), lambda j, i, k: (i, j))]
    if colsum == "b":
        out_shape.append(jax.ShapeDtypeStruct((8, N), F32))
        out_specs.append(pl.BlockSpec((8, tn), lambda j, i, k: (0, j)))
    if colsum == "a":
        out_shape.append(jax.ShapeDtypeStruct((8, M), F32))
        out_specs.append(pl.BlockSpec((8, tm), lambda j, i, k: (0, i)))
    res = pl.pallas_call(
        body, name=name,
        grid=(N // tn, M // tm, nk),
        in_specs=in_specs, out_specs=tuple(out_specs), out_shape=tuple(out_shape),
        scratch_shapes=[pltpu.VMEM((tm, tn), F32)] if nk > 1 else [],
        compiler_params=_cparams(("arbitrary", "arbitrary", "arbitrary")),
    )(*args)
    return res if len(res) > 1 else res[0]


def _ln_rows(v, g, b):
    mu = jnp.mean(v, axis=-1, keepdims=True)
    xc = v - mu
    var = jnp.mean(xc * xc, axis=-1, keepdims=True)
    return xc * lax.rsqrt(var + LN_EPS) * g + b


def _ln_bwd_rows(dy, v, g):
    mu = jnp.mean(v, axis=-1, keepdims=True)
    xc = v - mu
    var = jnp.mean(xc * xc, axis=-1, keepdims=True)
    rstd = lax.rsqrt(var + LN_EPS)
    xhat = xc * rstd
    dxh = dy * g
    dv = rstd * (dxh - jnp.mean(dxh, axis=-1, keepdims=True)
                 - xhat * jnp.mean(dxh * xhat, axis=-1, keepdims=True))
    return dv, xhat


def _ln0_fwd(x, meta, g, b):
    B, S, D = x.shape
    nch = S // CHUNK + 1

    def body(x_ref, meta_ref, g_ref, b_ref, h_ref, hb_ref):
        i = pl.program_id(1)

        @pl.when(i == 0)
        def _():
            m = _ln_rows(meta_ref[...], g_ref[...], b_ref[...])
            h_ref[0:PAD_ROWS, :] = jnp.zeros((PAD_ROWS, D), F32)
            h_ref[PAD_ROWS:CHUNK, :] = m
            hb_ref[0:PAD_ROWS, :] = jnp.zeros((PAD_ROWS, D), BF16)
            hb_ref[PAD_ROWS:CHUNK, :] = m.astype(BF16)

        @pl.when(i > 0)
        def _():
            y = _ln_rows(x_ref[0], g_ref[...], b_ref[...])
            h_ref[...] = y
            hb_ref[...] = y.astype(BF16)

    return pl.pallas_call(
        body, name="ln0_fwd", grid=(B, nch),
        in_specs=[pl.BlockSpec((1, CHUNK, D), lambda bb, i: (bb, jnp.maximum(i - 1, 0), 0)),
                  pl.BlockSpec((N_META, D), lambda bb, i: (0, 0)),
                  pl.BlockSpec((1, D), lambda bb, i: (0, 0)),
                  pl.BlockSpec((1, D), lambda bb, i: (0, 0))],
        out_specs=(pl.BlockSpec((CHUNK, D), lambda bb, i: (bb * nch + i, 0)),
                   pl.BlockSpec((CHUNK, D), lambda bb, i: (bb * nch + i, 0))),
        out_shape=(jax.ShapeDtypeStruct((B * nch * CHUNK, D), F32),
                   jax.ShapeDtypeStruct((B * nch * CHUNK, D), BF16)),
        compiler_params=_cparams(("arbitrary", "arbitrary")),
    )(x, meta, g, b)


def _ln0_bwd(dh0, x, meta, g):
    B, S, D = x.shape
    nch = S // CHUNK + 1

    def body(dh_ref, x_ref, meta_ref, g_ref, dx_ref, dmeta_ref, dg_ref, db_ref):
        bb, i = pl.program_id(0), pl.program_id(1)

        @pl.when((bb == 0) & (i == 0))
        def _():
            dmeta_ref[...] = jnp.zeros_like(dmeta_ref)
            dg_ref[...] = jnp.zeros_like(dg_ref)
            db_ref[...] = jnp.zeros_like(db_ref)

        @pl.when(i == 0)
        def _():
            dy = dh_ref[PAD_ROWS:CHUNK, :]
            dv, xhat = _ln_bwd_rows(dy, meta_ref[...], g_ref[...])
            dmeta_ref[...] += dv
            dg_ref[0:1, :] += jnp.sum(dy * xhat, axis=0, keepdims=True)
            db_ref[0:1, :] += jnp.sum(dy, axis=0, keepdims=True)
            dx_ref[0] = jnp.zeros((CHUNK, D), F32)

        @pl.when(i > 0)
        def _():
            dy = dh_ref[...]
            dv, xhat = _ln_bwd_rows(dy, x_ref[0], g_ref[...])
            dx_ref[0] = dv
            dg_ref[0:1, :] += jnp.sum(dy * xhat, axis=0, keepdims=True)
            db_ref[0:1, :] += jnp.sum(dy, axis=0, keepdims=True)

    xmap = lambda bb, i: (bb, jnp.maximum(i - 1, 0), 0)
    const = lambda bb, i: (0, 0)
    return pl.pallas_call(
        body, name="ln0_bwd", grid=(B, nch),
        in_specs=[pl.BlockSpec((CHUNK, D), lambda bb, i: (bb * nch + i, 0)),
                  pl.BlockSpec((1, CHUNK, D), xmap),
                  pl.BlockSpec((N_META, D), const),
                  pl.BlockSpec((1, D), const)],
        out_specs=(pl.BlockSpec((1, CHUNK, D), xmap),
                   pl.BlockSpec((N_META, D), const),
                   pl.BlockSpec((8, D), const),
                   pl.BlockSpec((8, D), const)),
        out_shape=(jax.ShapeDtypeStruct((B, S, D), F32),
                   jax.ShapeDtypeStruct((N_META, D), F32),
                   jax.ShapeDtypeStruct((8, D), F32),
                   jax.ShapeDtypeStruct((8, D), F32)),
        compiler_params=_cparams(("arbitrary", "arbitrary")),
    )(dh0, x, meta, g)


def _ln_res_fwd(h_prev, r, g, b, name):
    T, D = h_prev.shape
    tr = _pick(T, (320, 256, 128, 64))

    def body(hp_ref, r_ref, g_ref, b_ref, pre_ref, h_ref, hb_ref):
        pre = ALPHA * hp_ref[...] + r_ref[...]
        y = _ln_rows(pre, g_ref[...], b_ref[...])
        pre_ref[...] = pre
        h_ref[...] = y
        hb_ref[...] = y.astype(BF16)

    row = pl.BlockSpec((tr, D), lambda i: (i, 0))
    vec = pl.BlockSpec((1, D), lambda i: (0, 0))
    return pl.pallas_call(
        body, name=name, grid=(T // tr,),
        in_specs=[row, row, vec, vec], out_specs=(row, row, row),
        out_shape=(jax.ShapeDtypeStruct((T, D), F32), jax.ShapeDtypeStruct((T, D), F32),
                   jax.ShapeDtypeStruct((T, D), BF16)),
        compiler_params=_cparams(("arbitrary",)),
    )(h_prev, r, g, b)


def _ln_bwd(dh, pre, g, name):
    T, D = dh.shape
    tr = _pick(T, (320, 256, 128, 64))

    def body(dh_ref, pre_ref, g_ref, dp_ref, dpb_ref, dg_ref, db_ref):
        @pl.when(pl.program_id(0) == 0)
        def _():
            dg_ref[...] = jnp.zeros_like(dg_ref)
            db_ref[...] = jnp.zeros_like(db_ref)

        dy = dh_ref[...]
        dv, xhat = _ln_bwd_rows(dy, pre_ref[...], g_ref[...])
        dp_ref[...] = dv
        dpb_ref[...] = dv.astype(BF16)
        dg_ref[0:1, :] += jnp.sum(dy * xhat, axis=0, keepdims=True)
        db_ref[0:1, :] += jnp.sum(dy, axis=0, keepdims=True)

    row = pl.BlockSpec((tr, D), lambda i: (i, 0))
    vec = pl.BlockSpec((1, D), lambda i: (0, 0))
    acc = pl.BlockSpec((8, D), lambda i: (0, 0))
    return pl.pallas_call(
        body, name=name, grid=(T // tr,),
        in_specs=[row, row, vec], out_specs=(row, row, acc, acc),
        out_shape=(jax.ShapeDtypeStruct((T, D), F32), jax.ShapeDtypeStruct((T, D), BF16),
                   jax.ShapeDtypeStruct((8, D), F32), jax.ShapeDtypeStruct((8, D), F32)),
        compiler_params=_cparams(("arbitrary",)),
    )(dh, pre, g)


def _ln2_loss(h1, ff, g, b, target):
    T, D = h1.shape
    B, S, _ = target.shape
    nch = S // CHUNK + 1

    def body(h_ref, ff_ref, g_ref, b_ref, t_ref, loss_ref, dp_ref, dpb_ref, dg_ref, db_ref):
        bb, i = pl.program_id(0), pl.program_id(1)

        @pl.when((bb == 0) & (i == 0))
        def _():
            loss_ref[...] = jnp.zeros_like(loss_ref)
            dg_ref[...] = jnp.zeros_like(dg_ref)
            db_ref[...] = jnp.zeros_like(db_ref)

        @pl.when(i == 0)
        def _():
            dp_ref[...] = jnp.zeros((CHUNK, D), F32)
            dpb_ref[...] = jnp.zeros((CHUNK, D), BF16)

        @pl.when(i > 0)
        def _():
            pre = ALPHA * h_ref[...] + ff_ref[...]
            gg = g_ref[...]
            y = _ln_rows(pre, gg, b_ref[...])
            err = y - t_ref[0]
            loss_ref[0:1, 0:1] += 0.5 * jnp.sum(jnp.mean(err * err, axis=-1, keepdims=True),
                                                axis=0, keepdims=True)
            dy = err * (1.0 / D)
            dv, xhat = _ln_bwd_rows(dy, pre, gg)
            dp_ref[...] = dv
            dpb_ref[...] = dv.astype(BF16)
            dg_ref[0:1, :] += jnp.sum(dy * xhat, axis=0, keepdims=True)
            db_ref[0:1, :] += jnp.sum(dy, axis=0, keepdims=True)

    row = pl.BlockSpec((CHUNK, D), lambda bb, i: (bb * nch + i, 0))
    const = lambda bb, i: (0, 0)
    return pl.pallas_call(
        body, name="ln2_loss", grid=(B, nch),
        in_specs=[row, row, pl.BlockSpec((1, D), const), pl.BlockSpec((1, D), const),
                  pl.BlockSpec((1, CHUNK, D), lambda bb, i: (bb, jnp.maximum(i - 1, 0), 0))],
        out_specs=(pl.BlockSpec((8, 128), const), row, row,
                   pl.BlockSpec((8, D), const), pl.BlockSpec((8, D), const)),
        out_shape=(jax.ShapeDtypeStruct((8, 128), F32),
                   jax.ShapeDtypeStruct((T, D), F32), jax.ShapeDtypeStruct((T, D), BF16),
                   jax.ShapeDtypeStruct((8, D), F32), jax.ShapeDtypeStruct((8, D), F32)),
        compiler_params=_cparams(("arbitrary", "arbitrary")),
    )(h1, ff, g, b, target)


def _s5_prep(lam_re, lam_im, log_dt, b_re, b_im, c_re, c_im):
    dt = jnp.exp(log_dt)[:, None]
    mag = jnp.exp(lam_re * dt)
    ar = mag * jnp.cos(lam_im * dt)
    ai = mag * jnp.sin(lam_im * dt)
    nr, ni = ar - 1.0, ai
    den = lam_re * lam_re + lam_im * lam_im
    cr = (nr * lam_re + ni * lam_im) / den
    ci = (ni * lam_re - nr * lam_im) / den
    bbr = cr[..., None] * b_re - ci[..., None] * b_im
    bbi = cr[..., None] * b_im + ci[..., None] * b_re
    eye = jnp.eye(8, dtype=F32)
    bb = jnp.stack([bbr, bbi]).reshape(2, S5_BLK, 8, S5_STATE, S5_GROUP)
    bblk = jnp.einsum("rbgph,gj->bghrjp", bb, eye).reshape(S5_BLK, 128, 1024)
    cc = jnp.stack([c_re, -c_im]).reshape(2, S5_BLK, 8, S5_GROUP, S5_STATE)
    cblk = jnp.einsum("rbghp,gj->brjpgh", cc, eye).reshape(S5_BLK, 1024, 128)
    return ar.reshape(1, 2048), ai.reshape(1, 2048), bblk, cblk


def _scan_tables(ar, ai):
    pr, pi = [ar], [ai]
    for _ in range(7):
        pr, pi = pr + [pr[-1] * ar - pi[-1] * ai], pi + [pr[-1] * ai + pi[-1] * ar]
    pw_r = jnp.concatenate(pr, axis=0)
    pw_i = jnp.concatenate(pi, axis=0)
    row = jnp.arange(8)[:, None]

    def tables(sign, reverse):
        rows = []
        for n, sh in ((0, 1), (1, 2), (3, 4)):
            mask = (row < 8 - sh) if reverse else (row >= sh)
            rows.append(jnp.where(mask, pw_r[n][None, :], 0.0))
            rows.append(jnp.where(mask, sign * pw_i[n][None, :], 0.0))
        cr_ = pw_r[::-1] if reverse else pw_r
        ci_ = pw_i[::-1] if reverse else pw_i
        rows += [cr_, sign * ci_]
        return jnp.stack(rows)

    return tables(1.0, False), tables(-1.0, True)


def _scan_rows(s_ref, row0, tab_ref, carry_ref, ngroups, reverse, extra=None):
    take = 0 if reverse else 7
    rowid = lax.broadcasted_iota(jnp.int32, (8, 128), 0)

    def group(it, c):
        rb = (ngroups - 1 - it) if reverse else it
        r0 = pl.multiple_of(row0 + rb * 8, 8)
        for blk in range(S5_BLK):
            for j in range(4):
                cre = blk * 1024 + j * 128
                cim = cre + 512
                tc = blk * 512 + j * 128
                xr = s_ref[pl.ds(r0, 8), cre:cre + 128]
                xi = s_ref[pl.ds(r0, 8), cim:cim + 128]
                for lvl, sh in enumerate((1, 2, 4)):
                    lr = tab_ref[2 * lvl, :, tc:tc + 128]
                    li = tab_ref[2 * lvl + 1, :, tc:tc + 128]
                    shift = (8 - sh) if reverse else sh
                    sr = pltpu.roll(xr, shift, 0)
                    si = pltpu.roll(xi, shift, 0)
                    xr, xi = xr + lr * sr - li * si, xi + lr * si + li * sr
                pr = tab_ref[6, :, tc:tc + 128]
                pi = tab_ref[7, :, tc:tc + 128]
                c_r = carry_ref[:, cre:cre + 128]
                c_i = carry_ref[:, cim:cim + 128]
                xr, xi = xr + pr * c_r - pi * c_i, xi + pr * c_i + pi * c_r
                s_ref[pl.ds(r0, 8), cre:cre + 128] = xr
                s_ref[pl.ds(r0, 8), cim:cim + 128] = xi
                nr = jnp.sum(jnp.where(rowid == take, xr, 0.0), axis=0, keepdims=True)
                ni = jnp.sum(jnp.where(rowid == take, xi, 0.0), axis=0, keepdims=True)
                carry_ref[:, cre:cre + 128] = jnp.broadcast_to(nr, (8, 128))
                carry_ref[:, cim:cim + 128] = jnp.broadcast_to(ni, (8, 128))
                if extra is not None:
                    extra(r0, rb, cre, cim, tc, xr, xi)
        return c

    lax.fori_loop(0, ngroups, group, 0)


def _gelu(x):
    c = math.sqrt(2.0 / math.pi)
    t = jnp.tanh(c * (x + 0.044715 * x * x * x))
    return 0.5 * x * (1.0 + t)


def _gelu_grad(x):
    c = math.sqrt(2.0 / math.pi)
    t = jnp.tanh(c * (x + 0.044715 * x * x * x))
    return 0.5 * (1.0 + t) + 0.5 * x * (1.0 - t * t) * c * (1.0 + 3.0 * 0.044715 * x * x)


def _s5_tile(lp):
    return _pick(lp, (320, 256, 128, 64))


def _s5_fwd(p, bblk, cblk, tab_f, dskip, lp):
    T = p.shape[0]
    ts = _s5_tile(lp)
    nblk, per_seq = T // ts, lp // ts
    ucol = P_U // S5_WIDTH

    def body(u_ref, b_ref, c_ref, tab_ref, d_ref, y_ref, gy_ref, cin_ref, s_sc, carry_sc):
        r = pl.program_id(0)
        first = (r % per_seq) == 0

        @pl.when(first)
        def _():
            carry_sc[...] = jnp.zeros_like(carry_sc)

        cin_ref[0] = carry_sc[...]
        rowid = lax.broadcasted_iota(jnp.int32, (ts, 1), 0)
        u = jnp.where(first & (rowid < PAD_ROWS), 0.0, u_ref[...])
        ub = u.astype(BF16)
        for blk in range(S5_BLK):
            s_sc[:, blk * 1024:(blk + 1) * 1024] = _dot(ub[:, blk * 128:(blk + 1) * 128], b_ref[blk], 1, 0)
        _scan_rows(s_sc, 0, tab_ref, carry_sc, ts // 8, False)
        for blk in range(S5_BLK):
            sb = s_sc[:, blk * 1024:(blk + 1) * 1024].astype(BF16)
            yb = _dot(sb, c_ref[blk], 1, 0) + d_ref[:, blk * 128:(blk + 1) * 128] * u[:, blk * 128:(blk + 1) * 128]
            y_ref[:, blk * 128:(blk + 1) * 128] = yb
            gy_ref[:, blk * 128:(blk + 1) * 128] = _gelu(yb).astype(BF16)

    return pl.pallas_call(
        body, name="s5_fwd", grid=(nblk,),
        in_specs=[pl.BlockSpec((ts, S5_WIDTH), lambda r: (r, ucol)),
                  pl.BlockSpec((S5_BLK, 128, 1024), lambda r: (0, 0, 0)),
                  pl.BlockSpec((S5_BLK, 1024, 128), lambda r: (0, 0, 0)),
                  pl.BlockSpec((8, 8, 2048), lambda r: (0, 0, 0)),
                  pl.BlockSpec((1, S5_WIDTH), lambda r: (0, 0))],
        out_specs=(pl.BlockSpec((ts, S5_WIDTH), lambda r: (r, 0)),
                   pl.BlockSpec((ts, S5_WIDTH), lambda r: (r, 0)),
                   pl.BlockSpec((1, 8, S5_COLS), lambda r: (r, 0, 0))),
        out_shape=(jax.ShapeDtypeStruct((T, S5_WIDTH), F32),
                   jax.ShapeDtypeStruct((T, S5_WIDTH), BF16),
                   jax.ShapeDtypeStruct((nblk, 8, S5_COLS), F32)),
        scratch_shapes=[pltpu.VMEM((ts, S5_COLS), F32), pltpu.VMEM((8, S5_COLS), F32)],
        compiler_params=_cparams(("arbitrary",)),
    )(p, bblk.astype(BF16), cblk.astype(BF16), tab_f, dskip)


def _s5_bwd(p, y_pre, dgy, cin, bblk, cblk, tab_f, tab_b, dskip, lp):
    T = p.shape[0]
    ts = _s5_tile(lp)
    nblk, per_seq = T // ts, lp // ts
    ucol = P_U // S5_WIDTH
    ng = ts // 8

    def body(u_ref, y_ref, dgy_ref, cin_ref, b_ref, bt_ref, c_ref, ct_ref, tf_ref, tb_ref, d_ref,
             du_ref, dbb_ref, dcb_ref, dlam_ref, dd_ref, s_sc, a_sc, carry_sc, carry_b):
        t = pl.program_id(0)
        r = nblk - 1 - t
        first = (r % per_seq) == 0
        last = (r % per_seq) == per_seq - 1

        @pl.when(t == 0)
        def _():
            dbb_ref[...] = jnp.zeros_like(dbb_ref)
            dcb_ref[...] = jnp.zeros_like(dcb_ref)
            dlam_ref[...] = jnp.zeros_like(dlam_ref)
            dd_ref[...] = jnp.zeros_like(dd_ref)

        @pl.when(last)
        def _():
            carry_b[...] = jnp.zeros_like(carry_b)

        carry_sc[...] = cin_ref[0]
        s_sc[0:8, :] = cin_ref[0]
        rowid = lax.broadcasted_iota(jnp.int32, (ts, 1), 0)
        padrow = first & (rowid < PAD_ROWS)
        u = jnp.where(padrow, 0.0, u_ref[...])
        ub = u.astype(BF16)
        for blk in range(S5_BLK):
            s_sc[8:8 + ts, blk * 1024:(blk + 1) * 1024] = _dot(ub[:, blk * 128:(blk + 1) * 128], b_ref[blk], 1, 0)
        _scan_rows(s_sc, 8, tf_ref, carry_sc, ng, False)

        dy = dgy_ref[...] * _gelu_grad(y_ref[...])
        dyb = dy.astype(BF16)
        dd_ref[0:1, :] += jnp.sum(dy * u, axis=0, keepdims=True)
        for blk in range(S5_BLK):
            a_sc[:, blk * 1024:(blk + 1) * 1024] = _dot(dyb[:, blk * 128:(blk + 1) * 128], ct_ref[blk], 1, 0)
            sb = s_sc[8:8 + ts, blk * 1024:(blk + 1) * 1024].astype(BF16)
            dcb_ref[blk] += _dot(sb, dyb[:, blk * 128:(blk + 1) * 128], 0, 0)

        row8 = lax.broadcasted_iota(jnp.int32, (8, 128), 0)

        def lam_grad(r0, rb, cre, cim, tc, a_r, a_i):
            q0 = pl.multiple_of(r0, 8)
            q1 = pl.multiple_of(r0 + 8, 8)
            pr = jnp.where(row8 == 0, pltpu.roll(s_sc[pl.ds(q0, 8), cre:cre + 128], 1, 0),
                           pltpu.roll(s_sc[pl.ds(q1, 8), cre:cre + 128], 1, 0))
            pi = jnp.where(row8 == 0, pltpu.roll(s_sc[pl.ds(q0, 8), cim:cim + 128], 1, 0),
                           pltpu.roll(s_sc[pl.ds(q1, 8), cim:cim + 128], 1, 0))
            dlam_ref[0, :, tc:tc + 128] += a_r * pr + a_i * pi
            dlam_ref[1, :, tc:tc + 128] += a_i * pr - a_r * pi

        _scan_rows(a_sc, 0, tb_ref, carry_b, ng, True, extra=lam_grad)

        for blk in range(S5_BLK):
            ab = a_sc[:, blk * 1024:(blk + 1) * 1024].astype(BF16)
            dub = _dot(ab, bt_ref[blk], 1, 0) + d_ref[:, blk * 128:(blk + 1) * 128] * dy[:, blk * 128:(blk + 1) * 128]
            du_ref[:, blk * 128:(blk + 1) * 128] = jnp.where(padrow, 0.0, dub).astype(BF16)
            dbb_ref[blk] += _dot(ub[:, blk * 128:(blk + 1) * 128], ab, 0, 0)

    const3 = lambda t: (0, 0, 0)
    rev = lambda t: (nblk - 1 - t, 0)
    return pl.pallas_call(
        body, name="s5_bwd", grid=(nblk,),
        in_specs=[pl.BlockSpec((ts, S5_WIDTH), lambda t: (nblk - 1 - t, ucol)),
                  pl.BlockSpec((ts, S5_WIDTH), rev),
                  pl.BlockSpec((ts, S5_WIDTH), rev),
                  pl.BlockSpec((1, 8, S5_COLS), lambda t: (nblk - 1 - t, 0, 0)),
                  pl.BlockSpec((S5_BLK, 128, 1024), const3),
                  pl.BlockSpec((S5_BLK, 1024, 128), const3),
                  pl.BlockSpec((S5_BLK, 1024, 128), const3),
                  pl.BlockSpec((S5_BLK, 128, 1024), const3),
                  pl.BlockSpec((8, 8, 2048), const3),
                  pl.BlockSpec((8, 8, 2048), const3),
                  pl.BlockSpec((1, S5_WIDTH), lambda t: (0, 0))],
        out_specs=(pl.BlockSpec((ts, S5_WIDTH), rev),
                   pl.BlockSpec((S5_BLK, 128, 1024), const3),
                   pl.BlockSpec((S5_BLK, 1024, 128), const3),
                   pl.BlockSpec((2, 8, 2048), const3),
                   pl.BlockSpec((8, S5_WIDTH), lambda t: (0, 0))),
        out_shape=(jax.ShapeDtypeStruct((T, S5_WIDTH), BF16),
                   jax.ShapeDtypeStruct((S5_BLK, 128, 1024), F32),
                   jax.ShapeDtypeStruct((S5_BLK, 1024, 128), F32),
                   jax.ShapeDtypeStruct((2, 8, 2048), F32),
                   jax.ShapeDtypeStruct((8, S5_WIDTH), F32)),
        scratch_shapes=[pltpu.VMEM((ts + 8, S5_COLS), F32), pltpu.VMEM((ts, S5_COLS), F32),
                        pltpu.VMEM((8, S5_COLS), F32), pltpu.VMEM((8, S5_COLS), F32)],
        compiler_params=_cparams(("arbitrary",)),
    )(p, y_pre, dgy, cin, bblk.astype(BF16), jnp.swapaxes(bblk, 1, 2).astype(BF16),
      cblk.astype(BF16), jnp.swapaxes(cblk, 1, 2).astype(BF16), tab_f, tab_b, dskip)


def _row_in_seq(i, tr, lp):
    rowid = lax.broadcasted_iota(jnp.int32, (tr, 1), 0)
    return (i * tr + rowid) % lp


def _conv_fwd(p, w, b, lp):
    T = p.shape[0]
    tr = _pick(T, (320, 256, 128, 64))
    c = P_QK // 1024

    def body(x_ref, xp_ref, w_ref, b_ref, o_ref):
        i = pl.program_id(0)
        pos = _row_in_seq(i, tr, lp)
        x = jnp.where(pos < PAD_ROWS, 0.0, x_ref[...])
        pos_p = (i * tr - 8 + lax.broadcasted_iota(jnp.int32, (8, 1), 0)) % lp
        xp = jnp.where((pos_p < PAD_ROWS) | (i == 0), 0.0, xp_ref[...])
        xx = jnp.concatenate([xp, x], axis=0)
        acc = b_ref[...] + w_ref[3:4, :] * x
        for s in (1, 2, 3):
            acc = acc + w_ref[3 - s:4 - s, :] * pltpu.roll(xx, s, 0)[8:8 + tr]
        o_ref[...] = acc * _sigmoid(acc)

    return pl.pallas_call(
        body, name="conv_fwd", grid=(T // tr,),
        in_specs=[pl.BlockSpec((tr, 1024), lambda i: (i, c)),
                  pl.BlockSpec((8, 1024), lambda i: (jnp.maximum(i * (tr // 8) - 1, 0), c)),
                  pl.BlockSpec((4, 1024), lambda i: (0, 0)),
                  pl.BlockSpec((1, 1024), lambda i: (0, 0))],
        out_specs=pl.BlockSpec((tr, 1024), lambda i: (i, 0)),
        out_shape=jax.ShapeDtypeStruct((T, 1024), F32),
        compiler_params=_cparams(("arbitrary",)),
    )(p, p, w, b)


def _conv_bwd(p, dqk, w, b, lp):
    T = p.shape[0]
    tr = _pick(T, (320, 256, 128, 64))
    c = P_QK // 1024
    nb = T // tr

    def body(x_ref, xp_ref, xn_ref, g_ref, gn_ref, w_ref, b_ref, dx_ref, dw_ref, db_ref):
        i = pl.program_id(0)

        @pl.when(i == 0)
        def _():
            dw_ref[...] = jnp.zeros_like(dw_ref)
            db_ref[...] = jnp.zeros_like(db_ref)

        def seqpos(off, n):
            return (i * tr + off + lax.broadcasted_iota(jnp.int32, (n, 1), 0)) % lp

        x = jnp.where(seqpos(0, tr) < PAD_ROWS, 0.0, x_ref[...])
        xp = jnp.where((seqpos(-8, 8) < PAD_ROWS) | (i == 0), 0.0, xp_ref[...])
        xn = jnp.where((seqpos(tr, 8) < PAD_ROWS) | (i == nb - 1), 0.0, xn_ref[...])
        xx = jnp.concatenate([xp, x, xn], axis=0)
        gg = jnp.concatenate([g_ref[...], gn_ref[...]], axis=0)
        n2 = tr + 8
        acc = b_ref[...] + w_ref[3:4, :] * xx[8:8 + n2]
        for s in (1, 2, 3):
            acc = acc + w_ref[3 - s:4 - s, :] * pltpu.roll(xx, s, 0)[8:8 + n2]
        sg = _sigmoid(acc)
        dpre = gg * (sg * (1.0 + acc * (1.0 - sg)))
        valid = jnp.concatenate(
            [seqpos(0, tr) >= PAD_ROWS, (seqpos(tr, 8) >= PAD_ROWS) & (i < nb - 1)], axis=0)
        dpre = jnp.where(valid, dpre, 0.0)
        d0 = dpre[0:tr]
        dx = w_ref[3:4, :] * d0
        for s in (1, 2, 3):
            dx = dx + w_ref[3 - s:4 - s, :] * pltpu.roll(dpre, n2 - s, 0)[0:tr]
        dx_ref[...] = jnp.where(seqpos(0, tr) < PAD_ROWS, 0.0, dx).astype(BF16)
        db_ref[0:1, :] += jnp.sum(d0, axis=0, keepdims=True)
        dw_ref[3:4, :] += jnp.sum(d0 * x, axis=0, keepdims=True)
        for s in (1, 2, 3):
            xs = pltpu.roll(xx, s, 0)[8:8 + tr]
            dw_ref[3 - s:4 - s, :] += jnp.sum(d0 * xs, axis=0, keepdims=True)

    t8 = tr // 8
    return pl.pallas_call(
        body, name="conv_bwd", grid=(nb,),
        in_specs=[pl.BlockSpec((tr, 1024), lambda i: (i, c)),
                  pl.BlockSpec((8, 1024), lambda i: (jnp.maximum(i * t8 - 1, 0), c)),
                  pl.BlockSpec((8, 1024), lambda i: (jnp.minimum((i + 1) * t8, nb * t8 - 1), c)),
                  pl.BlockSpec((tr, 1024), lambda i: (i, 0)),
                  pl.BlockSpec((8, 1024), lambda i: (jnp.minimum((i + 1) * t8, nb * t8 - 1), 0)),
                  pl.BlockSpec((4, 1024), lambda i: (0, 0)),
                  pl.BlockSpec((1, 1024), lambda i: (0, 0))],
        out_specs=(pl.BlockSpec((tr, 1024), lambda i: (i, 0)),
                   pl.BlockSpec((8, 1024), lambda i: (0, 0)),
                   pl.BlockSpec((8, 1024), lambda i: (0, 0))),
        out_shape=(jax.ShapeDtypeStruct((T, 1024), BF16),
                   jax.ShapeDtypeStruct((8, 1024), F32),
                   jax.ShapeDtypeStruct((8, 1024), F32)),
        compiler_params=_cparams(("arbitrary",)),
    )(p, p, p, dqk, dqk, w, b)


def _split3(x):
    hi = x.astype(BF16)
    r1 = x - hi.astype(F32)
    mid = r1.astype(BF16)
    lo = (r1 - mid.astype(F32)).astype(BF16)
    return hi, mid, lo


def _tri_sum(x, upper):
    r = lax.broadcasted_iota(jnp.int32, (CHUNK, CHUNK), 0)
    c = lax.broadcasted_iota(jnp.int32, (CHUNK, CHUNK), 1)
    tri = jnp.where((r <= c) if upper else (r >= c), 1.0, 0.0).astype(BF16)
    hi, mid, lo = _split3(x)
    return _dot(tri, hi, 1, 0) + _dot(tri, mid, 1, 0) + _dot(tri, lo, 1, 0)


def _lane_col(x, lane):
    l = lax.broadcasted_iota(jnp.int32, x.shape, 1)
    return jnp.sum(jnp.where(l == lane, x, 0.0), axis=1, keepdims=True)


def _to_row(col):
    r = lax.broadcasted_iota(jnp.int32, (CHUNK, CHUNK), 0)
    c = lax.broadcasted_iota(jnp.int32, (CHUNK, CHUNK), 1)
    return jnp.sum(jnp.where(r == c, col, 0.0), axis=0, keepdims=True)


def _to_col(row):
    r = lax.broadcasted_iota(jnp.int32, (CHUNK, CHUNK), 0)
    c = lax.broadcasted_iota(jnp.int32, (CHUNK, CHUNK), 1)
    return jnp.sum(jnp.where(r == c, row, 0.0), axis=1, keepdims=True)


def _log_sigmoid(x):
    return jnp.minimum(x, 0.0) - jnp.log(1.0 + jnp.exp(-jnp.abs(x)))


def _mlstm_gates(ifv, padmask):
    lf = jnp.where(padmask, 0.0, _log_sigmoid(ifv))
    b_all = _tri_sum(lf, False)
    li = jnp.where(padmask, -jnp.inf, ifv)
    return li, b_all


def _mlstm_head_fwd(q, k, v, li_col, b_col, c_st, n_st, m_st):
    r = lax.broadcasted_iota(jnp.int32, (CHUNK, CHUNK), 0)
    c = lax.broadcasted_iota(jnp.int32, (CHUNK, CHUNK), 1)
    rowid = lax.broadcasted_iota(jnp.int32, (CHUNK, 1), 0)
    b_row = _to_row(b_col)
    li_row = _to_row(li_col)
    dmat = jnp.where(r >= c, b_col - b_row + li_row, -jnp.inf)
    m_inter = b_col + m_st
    m_row = jnp.maximum(m_inter, jnp.max(dmat, axis=1, keepdims=True))
    w_intra = jnp.exp(dmat - m_row)
    w_inter = jnp.exp(m_inter - m_row)
    qb, kb, vb = q.astype(BF16), k.astype(BF16), v.astype(BF16)
    qk = _dot(qb, kb, 1, 1)
    s = qk * w_intra
    cb = c_st.astype(BF16)
    qc = _dot(qb, cb, 1, 0)
    qn = jnp.sum(q * n_st, axis=1, keepdims=True)
    num = _dot(s.astype(BF16), vb, 1, 0) + w_inter * qc
    den = jnp.sum(s, axis=1, keepdims=True) + w_inter * qn
    floor = jnp.exp(-m_row)
    rinv = 1.0 / jnp.maximum(jnp.abs(den), floor)
    h = num * rinv
    b_last = jnp.sum(jnp.where(rowid == CHUNK - 1, b_col, 0.0), axis=0, keepdims=True)
    g_col = b_last - b_col + li_col
    m_new = jnp.maximum(b_last + m_st, jnp.max(g_col, axis=0, keepdims=True))
    w_k = jnp.exp(g_col - m_new)
    decay = jnp.exp(b_last + m_st - m_new)
    kw = w_k * k
    c_new = decay * c_st + _dot(kw.astype(BF16), vb, 0, 0)
    n_new = decay * n_st + jnp.sum(kw, axis=0, keepdims=True)
    return dict(h=h, c_new=c_new, n_new=n_new, m_new=m_new, w_intra=w_intra, w_inter=w_inter, s=s,
                qc=qc, qn=qn, den=den, floor=floor, rinv=rinv, w_k=w_k, decay=decay, kw=kw,
                qb=qb, kb=kb, vb=vb, cb=cb)


def _mlstm_fwd(p, qk, lp):
    T = p.shape[0]
    nch = lp // CHUNK
    B = T // lp
    scale = M_DK ** -0.5

    def body(q_ref, k_ref, v_ref, if_ref, h_ref, cst_ref, nm_ref, c_sc, nm_sc):
        ci = pl.program_id(1)

        @pl.when(ci == 0)
        def _():
            c_sc[...] = jnp.zeros_like(c_sc)
            nm_sc[...] = jnp.zeros_like(nm_sc)

        cst_ref[0] = c_sc[...]
        nm_ref[0] = nm_sc[...]
        rowid = lax.broadcasted_iota(jnp.int32, (CHUNK, 1), 0)
        padmask = (ci == 0) & (rowid < PAD_ROWS)
        li_all, b_all = _mlstm_gates(if_ref[...], padmask)
        for hd in range(M_HEADS):
            q = q_ref[:, hd * M_DK:(hd + 1) * M_DK]
            k = k_ref[:, hd * M_DK:(hd + 1) * M_DK] * scale
            v = v_ref[:, hd * M_DV:(hd + 1) * M_DV]
            o = _mlstm_head_fwd(q, k, v, _lane_col(li_all, hd), _lane_col(b_all, M_HEADS + hd),
                                c_sc[hd * M_DK:(hd + 1) * M_DK, :], nm_sc[hd:hd + 1, :],
                                nm_sc[M_HEADS + hd:M_HEADS + hd + 1, 0:1])
            h_ref[:, hd * M_DV:(hd + 1) * M_DV] = o["h"]
            c_sc[hd * M_DK:(hd + 1) * M_DK, :] = o["c_new"]
            nm_sc[hd:hd + 1, :] = o["n_new"]
            nm_sc[M_HEADS + hd:M_HEADS + hd + 1, :] = jnp.broadcast_to(o["m_new"], (1, 128))

    rowmap = lambda bb, ci: (bb * nch + ci, 0)
    return pl.pallas_call(
        body, name="mlstm_fwd", grid=(B, nch),
        in_specs=[pl.BlockSpec((CHUNK, 512), rowmap),
                  pl.BlockSpec((CHUNK, 512), lambda bb, ci: (bb * nch + ci, 1)),
                  pl.BlockSpec((CHUNK, 1024), lambda bb, ci: (bb * nch + ci, P_V // 1024)),
                  pl.BlockSpec((CHUNK, 128), lambda bb, ci: (bb * nch + ci, P_IF // 128))],
        out_specs=(pl.BlockSpec((CHUNK, 1024), rowmap),
                   pl.BlockSpec((1, M_HEADS * M_DK, M_DV), lambda bb, ci: (bb * nch + ci, 0, 0)),
                   pl.BlockSpec((1, 8, 128), lambda bb, ci: (bb * nch + ci, 0, 0))),
        out_shape=(jax.ShapeDtypeStruct((T, 1024), F32),
                   jax.ShapeDtypeStruct((B * nch, M_HEADS * M_DK, M_DV), F32),
                   jax.ShapeDtypeStruct((B * nch, 8, 128), F32)),
        scratch_shapes=[pltpu.VMEM((M_HEADS * M_DK, M_DV), F32), pltpu.VMEM((8, 128), F32)],
        compiler_params=_cparams(("arbitrary", "arbitrary")),
    )(qk, qk, p, p)


def _mlstm_bwd(p, qk, cst, nm, dh, lp):
    T = p.shape[0]
    nch = lp // CHUNK
    B = T // lp
    scale = M_DK ** -0.5

    def body(q_ref, k_ref, v_ref, if_ref, cst_ref, nm_ref, dh_ref, dqk_ref, dv_ref, dif_ref, dc_sc, dn_sc):
        t = pl.program_id(1)
        ci = nch - 1 - t

        @pl.when(t == 0)
        def _():
            dc_sc[...] = jnp.zeros_like(dc_sc)
            dn_sc[...] = jnp.zeros_like(dn_sc)

        rowid = lax.broadcasted_iota(jnp.int32, (CHUNK, 1), 0)
        lane = lax.broadcasted_iota(jnp.int32, (CHUNK, 128), 1)
        padmask = (ci == 0) & (rowid < PAD_ROWS)
        ifv = if_ref[...]
        li_all, b_all = _mlstm_gates(ifv, padmask)
        db_all = jnp.zeros((CHUNK, 128), F32)
        dli_all = jnp.zeros((CHUNK, 128), F32)
        for hd in range(M_HEADS):
            q = q_ref[:, hd * M_DK:(hd + 1) * M_DK]
            k = k_ref[:, hd * M_DK:(hd + 1) * M_DK] * scale
            v = v_ref[:, hd * M_DV:(hd + 1) * M_DV]
            c_st = cst_ref[0, hd * M_DK:(hd + 1) * M_DK, :]
            n_st = nm_ref[0, hd:hd + 1, :]
            m_st = nm_ref[0, M_HEADS + hd:M_HEADS + hd + 1, 0:1]
            o = _mlstm_head_fwd(q, k, v, _lane_col(li_all, hd), _lane_col(b_all, M_HEADS + hd), c_st, n_st, m_st)
            dc_new = dc_sc[hd * M_DK:(hd + 1) * M_DK, :]
            dn_new = dn_sc[hd:hd + 1, :]
            dcb = dc_new.astype(BF16)
            dhh = dh_ref[:, hd * M_DV:(hd + 1) * M_DV]
            dnum = dhh * o["rinv"]
            dhh_h = jnp.sum(dhh * o["h"], axis=1, keepdims=True)
            sgn = jnp.where(o["den"] >= 0.0, 1.0, -1.0)
            dden = jnp.where(jnp.abs(o["den"]) > o["floor"], -dhh_h * o["rinv"] * sgn, 0.0)
            dnb = dnum.astype(BF16)
            ds = _dot(dnb, o["vb"], 1, 1) + dden
            sb = o["s"].astype(BF16)
            kwb = o["kw"].astype(BF16)
            dv = _dot(sb, dnb, 0, 0) + _dot(kwb, dcb, 1, 0)
            dqk_m = (ds * o["w_intra"]).astype(BF16)
            wdn = o["w_inter"] * dnum
            wdd = o["w_inter"] * dden
            dq = _dot(dqk_m, o["kb"], 1, 0) + _dot(wdn.astype(BF16), o["cb"], 1, 1) + wdd * n_st
            vdc = _dot(o["vb"], dcb, 1, 1)
            dk = _dot(dqk_m, o["qb"], 0, 0) + o["w_k"] * (vdc + dn_new)
            dd = ds * o["s"]
            dd_col = _to_col(jnp.sum(dd, axis=0, keepdims=True))
            dmi = o["w_inter"] * (jnp.sum(dnum * o["qc"], axis=1, keepdims=True) + dden * o["qn"])
            dg = o["w_k"] * (jnp.sum(k * vdc, axis=1, keepdims=True) + jnp.sum(k * dn_new, axis=1, keepdims=True))
            d_blast = (o["decay"] * (jnp.sum(jnp.sum(dc_new * c_st, axis=1, keepdims=True), axis=0, keepdims=True)
                                     + jnp.sum(dn_new * n_st, axis=1, keepdims=True))
                       + jnp.sum(dg, axis=0, keepdims=True))
            db_col = jnp.sum(dd, axis=1, keepdims=True) - dd_col + dmi - dg
            db_col = db_col + jnp.where(rowid == CHUNK - 1, d_blast, 0.0)
            dli_col = dd_col + dg
            db_all = db_all + jnp.where(lane == M_HEADS + hd, db_col, 0.0)
            dli_all = dli_all + jnp.where(lane == hd, dli_col, 0.0)
            dc_sc[hd * M_DK:(hd + 1) * M_DK, :] = o["decay"] * dc_new + _dot(o["qb"], wdn.astype(BF16), 0, 0)
            dn_sc[hd:hd + 1, :] = o["decay"] * dn_new + jnp.sum(q * wdd, axis=0, keepdims=True)
            dqk_ref[:, hd * M_DK:(hd + 1) * M_DK] = dq
            dqk_ref[:, 512 + hd * M_DK:512 + (hd + 1) * M_DK] = dk * scale
            dv_ref[:, hd * M_DV:(hd + 1) * M_DV] = dv.astype(BF16)
        dlf = _tri_sum(db_all, True)
        dif = dli_all + dlf * _sigmoid(-ifv)
        dif_ref[...] = jnp.where(padmask | (lane >= 2 * M_HEADS), 0.0, dif).astype(BF16)

    rev = lambda bb, t: (bb * nch + nch - 1 - t, 0)
    rev3 = lambda bb, t: (bb * nch + nch - 1 - t, 0, 0)
    return pl.pallas_call(
        body, name="mlstm_bwd", grid=(B, nch),
        in_specs=[pl.BlockSpec((CHUNK, 512), rev),
                  pl.BlockSpec((CHUNK, 512), lambda bb, t: (bb * nch + nch - 1 - t, 1)),
                  pl.BlockSpec((CHUNK, 1024), lambda bb, t: (bb * nch + nch - 1 - t, P_V // 1024)),
                  pl.BlockSpec((CHUNK, 128), lambda bb, t: (bb * nch + nch - 1 - t, P_IF // 128)),
                  pl.BlockSpec((1, M_HEADS * M_DK, M_DV), rev3),
                  pl.BlockSpec((1, 8, 128), rev3),
                  pl.BlockSpec((CHUNK, 1024), rev)],
        out_specs=(pl.BlockSpec((CHUNK, 1024), rev),
                   pl.BlockSpec((CHUNK, 1024), rev),
                   pl.BlockSpec((CHUNK, 128), rev)),
        out_shape=(jax.ShapeDtypeStruct((T, 1024), F32),
                   jax.ShapeDtypeStruct((T, 1024), BF16),
                   jax.ShapeDtypeStruct((T, 128), BF16)),
        scratch_shapes=[pltpu.VMEM((M_HEADS * M_DK, M_DV), F32), pltpu.VMEM((8, 128), F32)],
        compiler_params=_cparams(("arbitrary", "arbitrary")),
    )(qk, qk, p, p, cst, nm, dh)


def _headnorm_fwd(hm, p, g):
    T = hm.shape[0]
    tr = _pick(T, (320, 256, 128, 64))

    def body(h_ref, o_ref, g_ref, a_ref):
        for hd in range(M_HEADS):
            sl = slice(hd * M_DV, (hd + 1) * M_DV)
            hn = _ln_rows(h_ref[:, sl], g_ref[:, sl], 0.0)
            a_ref[:, sl] = (_sigmoid(o_ref[:, sl]) * hn).astype(BF16)

    return pl.pallas_call(
        body, name="headnorm_fwd", grid=(T // tr,),
        in_specs=[pl.BlockSpec((tr, 1024), lambda i: (i, 0)),
                  pl.BlockSpec((tr, 1024), lambda i: (i, P_O // 1024)),
                  pl.BlockSpec((1, 1024), lambda i: (0, 0))],
        out_specs=pl.BlockSpec((tr, 1024), lambda i: (i, 0)),
        out_shape=jax.ShapeDtypeStruct((T, 1024), BF16),
        compiler_params=_cparams(("arbitrary",)),
    )(hm, p, g)


def _headnorm_bwd(da, hm, p, g):
    T = hm.shape[0]
    tr = _pick(T, (320, 256, 128, 64))

    def body(da_ref, h_ref, o_ref, g_ref, dh_ref, do_ref, dg_ref):
        @pl.when(pl.program_id(0) == 0)
        def _():
            dg_ref[...] = jnp.zeros_like(dg_ref)

        for hd in range(M_HEADS):
            sl = slice(hd * M_DV, (hd + 1) * M_DV)
            gg = g_ref[:, sl]
            so = _sigmoid(o_ref[:, sl])
            da = da_ref[:, sl]
            dhn = da * so
            dv, xhat = _ln_bwd_rows(dhn, h_ref[:, sl], gg)
            dh_ref[:, sl] = dv
            do_ref[:, sl] = (da * (xhat * gg) * so * (1.0 - so)).astype(BF16)
            dg_ref[0:1, sl] += jnp.sum(dhn * xhat, axis=0, keepdims=True)

    row = pl.BlockSpec((tr, 1024), lambda i: (i, 0))
    return pl.pallas_call(
        body, name="headnorm_bwd", grid=(T // tr,),
        in_specs=[row, row, pl.BlockSpec((tr, 1024), lambda i: (i, P_O // 1024)),
                  pl.BlockSpec((1, 1024), lambda i: (0, 0))],
        out_specs=(row, row, pl.BlockSpec((8, 1024), lambda i: (0, 0))),
        out_shape=(jax.ShapeDtypeStruct((T, 1024), F32), jax.ShapeDtypeStruct((T, 1024), BF16),
                   jax.ShapeDtypeStruct((8, 1024), F32)),
        compiler_params=_cparams(("arbitrary",)),
    )(da, hm, p, g)


def _mix_fwd(z, ym, p):
    T = ym.shape[0]
    tr = _pick(T, (320, 256, 128, 64))

    def body(z1_ref, z2_ref, ym_ref, gs_ref, gm_ref, o_ref):
        ys = z1_ref[...] * _sigmoid(z2_ref[...])
        o_ref[...] = (_sigmoid(gs_ref[...]) * ys + _sigmoid(gm_ref[...]) * ym_ref[...]).astype(BF16)

    def col(cb):
        return pl.BlockSpec((tr, 1024), lambda i: (i, cb))

    return pl.pallas_call(
        body, name="mix_fwd", grid=(T // tr,),
        in_specs=[col(0), col(1), col(0), col(P_GS // 1024), col(P_GM // 1024)],
        out_specs=col(0),
        out_shape=jax.ShapeDtypeStruct((T, 1024), BF16),
        compiler_params=_cparams(("arbitrary",)),
    )(z, z, ym, p, p)


def _mix_bwd(dmix, z, ym, p):
    T = ym.shape[0]
    tr = _pick(T, (320, 256, 128, 64))

    def body(d_ref, z1_ref, z2_ref, ym_ref, gs_ref, gm_ref, dz_ref, dym_ref, dgs_ref, dgm_ref):
        d = d_ref[...]
        z1 = z1_ref[...]
        s2 = _sigmoid(z2_ref[...])
        ss = _sigmoid(gs_ref[...])
        sm = _sigmoid(gm_ref[...])
        ys = z1 * s2
        dys = d * ss
        dgs_ref[...] = (d * ys * ss * (1.0 - ss)).astype(BF16)
        dgm_ref[...] = (d * ym_ref[...] * sm * (1.0 - sm)).astype(BF16)
        dym_ref[...] = (d * sm).astype(BF16)
        dz_ref[:, 0:1024] = (dys * s2).astype(BF16)
        dz_ref[:, 1024:2048] = (dys * z1 * s2 * (1.0 - s2)).astype(BF16)

    def col(cb):
        return pl.BlockSpec((tr, 1024), lambda i: (i, cb))

    o = jax.ShapeDtypeStruct((T, 1024), BF16)
    return pl.pallas_call(
        body, name="mix_bwd", grid=(T // tr,),
        in_specs=[col(0), col(0), col(1), col(0), col(P_GS // 1024), col(P_GM // 1024)],
        out_specs=(pl.BlockSpec((tr, 2048), lambda i: (i, 0)),) + (col(0),) * 3,
        out_shape=(jax.ShapeDtypeStruct((T, 2048), BF16),) + (o,) * 3,
        compiler_params=_cparams(("arbitrary",)),
    )(dmix, z, z, ym, p, p)


def _adamw_math(w, g, m, v):
    m2 = ADAM_B1 * m + (1.0 - ADAM_B1) * g
    v2 = ADAM_B2 * v + (1.0 - ADAM_B2) * jnp.square(g)
    m_hat = m2 / (1.0 - ADAM_B1 ** ADAM_STEP)
    v_hat = v2 / (1.0 - ADAM_B2 ** ADAM_STEP)
    delta = -ADAM_LR * (m_hat / (jnp.sqrt(v_hat) + ADAM_EPS) + ADAM_WD * w)
    return delta, m2, v2


def _adamw_big(recv, w, m, v, name):
    R, C = w.shape
    tr, tc = _shard_tile(R, C)
    ns = recv.shape[0]

    def body(r_ref, w_ref, m_ref, v_ref, g_ref, d_ref, m2_ref, v2_ref):
        g = r_ref[0].astype(F32)
        for s in range(1, ns):
            g = g + r_ref[s].astype(F32)
        d, m2, v2 = _adamw_math(w_ref[...], g, m_ref[...], v_ref[...])
        g_ref[...] = g
        d_ref[...] = d
        m2_ref[...] = m2
        v2_ref[...] = v2

    blk = pl.BlockSpec((tr, tc), lambda i, j: (i, j))
    o = jax.ShapeDtypeStruct((R, C), F32)
    return pl.pallas_call(
        body, name=name, grid=(R // tr, C // tc),
        in_specs=[pl.BlockSpec((ns, tr, tc), lambda i, j: (0, i, j)), blk, blk, blk],
        out_specs=(blk,) * 4, out_shape=(o,) * 4,
        compiler_params=_cparams(("arbitrary", "arbitrary")),
    )(recv, w, m, v)


def _adamw_small(g, w, m, v):
    R = g.shape[0]

    def body(g_ref, w_ref, m_ref, v_ref, d_ref, m2_ref, v2_ref):
        d, m2, v2 = _adamw_math(w_ref[...], g_ref[...], m_ref[...], v_ref[...])
        d_ref[...] = d
        m2_ref[...] = m2
        v2_ref[...] = v2

    full = pl.BlockSpec((R, 128), lambda i: (0, 0))
    o = jax.ShapeDtypeStruct((R, 128), F32)
    return pl.pallas_call(
        body, name="adamw_small", grid=(1,),
        in_specs=[full] * 4, out_specs=(full,) * 3, out_shape=(o,) * 3,
        compiler_params=_cparams(("arbitrary",)),
    )(g, w, m, v)


def _pack(arrs):
    parts = []
    for a in arrs:
        f = a.reshape(-1).astype(F32)
        n = -(-f.shape[0] // 1024) * 1024
        parts.append(jnp.pad(f, (0, n - f.shape[0])))
    return jnp.concatenate(parts).reshape(-1, 128)


def _unpack(pack, shapes):
    flat = pack.reshape(-1)
    out, off = [], 0
    for shp in shapes:
        n = math.prod(shp)
        out.append(flat[off:off + n].reshape(shp))
        off += -(-n // 1024) * 1024
    return out


def _cols_from_shards(g):
    return jnp.transpose(g, (1, 0, 2)).reshape(g.shape[1], -1)


def _cols_to_shards(w):
    R = w.shape[0]
    return jnp.transpose(w.reshape(R, N_DEV, -1), (1, 0, 2))


def _nat_to_aligned(w, axis):
    def sl(start, size):
        return lax.slice_in_dim(w, start, start + size, axis=axis)

    pads = [(0, 0)] * w.ndim
    pads[axis] = (0, PW - P_IF - 8)
    return jnp.concatenate([sl(N_V, 1024), sl(N_O, 1024), sl(N_GS, 1024), sl(N_GM, 1024), sl(N_Q, 1024),
                            sl(N_U, 512), jnp.pad(sl(N_I, 8), pads)], axis=axis)


def _aligned_to_nat(w, axis):
    def sl(start, size):
        return lax.slice_in_dim(w, start, start + size, axis=axis)

    return jnp.concatenate([sl(P_U, 512), sl(P_QK, 1024), sl(P_V, 1024), sl(P_O, 1024), sl(P_IF, 8),
                            sl(P_GS, 1024), sl(P_GM, 1024)], axis=axis)


def kernel(x, meta_tokens, ln0_g, ln0_b, w_in, b_in, qk_conv_w, qk_conv_b, s5_lambda_re, s5_lambda_im, s5_log_dt, s5_b_re, s5_b_im, s5_c_re, s5_c_im, s5_d, s5_w_glu, m_norm_g, m_w_out, w_o, ln1_g, ln1_b, w_up, b_up, w_down, ln2_g, ln2_b, loss_target, m_meta_tokens, m_ln0_g, m_ln0_b, m_w_in, m_b_in, m_qk_conv_w, m_qk_conv_b, m_s5_lambda_re, m_s5_lambda_im, m_s5_log_dt, m_s5_b_re, m_s5_b_im, m_s5_c_re, m_s5_c_im, m_s5_d, m_s5_w_glu, m_m_norm_g, m_m_w_out, m_w_o, m_ln1_g, m_ln1_b, m_w_up, m_b_up, m_w_down, m_ln2_g, m_ln2_b, v_meta_tokens, v_ln0_g, v_ln0_b, v_w_in, v_b_in, v_qk_conv_w, v_qk_conv_b, v_s5_lambda_re, v_s5_lambda_im, v_s5_log_dt, v_s5_b_re, v_s5_b_im, v_s5_c_re, v_s5_c_im, v_s5_d, v_s5_w_glu, v_m_norm_g, v_m_w_out, v_w_o, v_ln1_g, v_ln1_b, v_w_up, v_b_up, v_w_down, v_ln2_g, v_ln2_b):
    B, S, D = x.shape
    lp = S + CHUNK
    me = _my_id()

    big = [w_in[0].T, s5_w_glu[0], m_w_out[0], w_o[0], w_up[0], w_down[0]]
    gathered = _allgather([a.astype(BF16) for a in big] + [meta_tokens, qk_conv_w[0]])
    win_t = _nat_to_aligned(gathered[0].reshape(IN_NAT, D), 0)
    wglu_f = _cols_from_shards(gathered[1])
    wmo_f = gathered[2].reshape(1024, 1024)
    wo_f = gathered[3].reshape(1024, 1024)
    wup_f = _cols_from_shards(gathered[4])
    wdown_f = gathered[5].reshape(D_FF, 1024)
    meta_f = _cols_from_shards(gathered[6])
    convw_f = _cols_from_shards(gathered[7])
    b_in_al = _nat_to_aligned(b_in, 1)
    ln0g, ln0b = ln0_g.reshape(1, D), ln0_b.reshape(1, D)

    s5_args = (s5_lambda_re[0], s5_lambda_im[0], s5_log_dt[0], s5_b_re[0], s5_b_im[0], s5_c_re[0], s5_c_im[0])
    (ar, ai, bblk, cblk), s5_vjp = jax.vjp(_s5_prep, *s5_args)
    tab_f, tab_b = _scan_tables(ar, ai)

    h0, h0b = _ln0_fwd(x, meta_f, ln0g, ln0b)
    p = _mm(h0b, win_t, "nt", "mm_in", bias=b_in_al)
    y_pre, gy, cin = _s5_fwd(p, bblk, cblk, tab_f, s5_d, lp)
    z = _mm(gy, wglu_f, "nn", "mm_glu")
    qk = _conv_fwd(p, convw_f, qk_conv_b, lp)
    hm, cst, nm = _mlstm_fwd(p, qk, lp)
    a_m = _headnorm_fwd(hm, p, m_norm_g)
    ym = _mm(a_m, wmo_f, "nn", "mm_mout")
    mix = _mix_fwd(z, ym, p)
    r1 = _mm(mix, wo_f, "nn", "mm_o")
    pre1, h1, h1b = _ln_res_fwd(h0, r1, ln1_g, ln1_b, "ln1_fwd")
    act = _mm(h1b, wup_f, "nn", "mm_up", bias=b_up, relu2_out=True, out_dtype=BF16)
    ff = _mm(act, wdown_f, "nn", "mm_down")
    loss_acc, dpre2, dpre2b, dg2, db2 = _ln2_loss(h1, ff, ln2_g, ln2_b, loss_target)

    d_up = _mm(dpre2b, wdown_f, "nt", "mm_d_act", sqrt_gate_of=act, out_dtype=BF16)
    g_wdown = _mm(act, dpre2b, "tn", "mm_g_wdown")
    g_wup, cs_up = _mm(h1b, d_up, "tn", "mm_g_wup", colsum="b")
    dh1 = _mm(d_up, wup_f, "nt", "mm_d_h1", add=dpre2, add_scale=ALPHA)
    dpre1, dpre1b, dg1, db1 = _ln_bwd(dh1, pre1, ln1_g, "ln1_bwd")
    g_wo = _mm(mix, dpre1b, "tn", "mm_g_wo")
    dmix = _mm(dpre1b, wo_f, "nt", "mm_d_mix")
    dz, dym, dgs, dgm = _mix_bwd(dmix, z, ym, p)
    g_wmo = _mm(a_m, dym, "tn", "mm_g_wmo")
    da = _mm(dym, wmo_f, "nt", "mm_d_a")
    dhm, do_pre, dg_norm = _headnorm_bwd(da, hm, p, m_norm_g)
    dqk, dv, dif = _mlstm_bwd(p, qk, cst, nm, dhm, lp)
    dqk_pre, dconv_w, dconv_b = _conv_bwd(p, dqk, convw_f, qk_conv_b, lp)
    g_wglu = _mm(gy, dz, "tn", "mm_g_wglu")
    dgy = _mm(dz, wglu_f, "nt", "mm_d_gy")
    du, dbblk, dcblk, dlam, dd = _s5_bwd(p, y_pre, dgy, cin, bblk, cblk, tab_f, tab_b, s5_d, lp)
    dp = jnp.concatenate([dv, do_pre, dgs, dgm, dqk_pre, du, dif], axis=1)
    g_win_t, cs_in = _mm(dp, h0b, "tn", "mm_g_win", colsum="a")
    dh0 = _mm(dp, win_t, "nn", "mm_d_h0", add=dpre1, add_scale=ALPHA)
    grad_x, dmeta, dg0, db0 = _ln0_bwd(dh0, x, meta_f, ln0g)

    dlam2 = jnp.sum(dlam, axis=1)
    s5_grads = s5_vjp((dlam2[0:1], dlam2[1:2], dbblk, dcblk))
    small_local = [
        loss_acc[0:1, 0:1], dg0[0:1], db0[0:1], _aligned_to_nat(cs_in[0:1], 1), dconv_b[0:1],
        s5_grads[0], s5_grads[1], s5_grads[2], s5_grads[3], s5_grads[4], s5_grads[5], s5_grads[6],
        dd[0:1], dg_norm[0:1], dg1[0:1], db1[0:1], cs_up[0:1], dg2[0:1], db2[0:1],
        dmeta, dconv_w[0:4]]
    small_shapes = [(), (D,), (D,), (1, IN_NAT), (1, 1024),
                    (1, 32, 64), (1, 32, 64), (1, 32), (1, 32, 64, 16), (1, 32, 64, 16), (1, 32, 16, 64), (1, 32, 16, 64),
                    (1, 512), (1, 1024), (1, 1024), (1, 1024), (1, D_FF), (1, 1024), (1, 1024),
                    (N_META, D), (4, 1024)]
    red = _unpack(_allreduce_small(_pack(small_local)), small_shapes)
    loss = red[0]
    g_meta = lax.dynamic_slice_in_dim(red[19], me * 128, 128, axis=1)
    g_convw = lax.dynamic_slice_in_dim(red[20], me * 128, 128, axis=1)[None]
    small_g = red[1:19] + [g_meta, g_convw]
    small_w = [ln0_g, ln0_b, b_in, qk_conv_b, s5_lambda_re, s5_lambda_im, s5_log_dt, s5_b_re, s5_b_im,
               s5_c_re, s5_c_im, s5_d, m_norm_g, ln1_g, ln1_b, b_up, ln2_g, ln2_b, meta_tokens, qk_conv_w]
    small_m = [m_ln0_g, m_ln0_b, m_b_in, m_qk_conv_b, m_s5_lambda_re, m_s5_lambda_im, m_s5_log_dt, m_s5_b_re,
               m_s5_b_im, m_s5_c_re, m_s5_c_im, m_s5_d, m_m_norm_g, m_ln1_g, m_ln1_b, m_b_up, m_ln2_g, m_ln2_b,
               m_meta_tokens, m_qk_conv_w]
    small_v = [v_ln0_g, v_ln0_b, v_b_in, v_qk_conv_b, v_s5_lambda_re, v_s5_lambda_im, v_s5_log_dt, v_s5_b_re,
               v_s5_b_im, v_s5_c_re, v_s5_c_im, v_s5_d, v_m_norm_g, v_ln1_g, v_ln1_b, v_b_up, v_ln2_g, v_ln2_b,
               v_meta_tokens, v_qk_conv_w]
    shapes_w = [tuple(w.shape) for w in small_w]
    small_g = [g.reshape(s) for g, s in zip(small_g, shapes_w)]
    sd, sm2, sv2 = _adamw_small(_pack(small_g), _pack(small_w), _pack(small_m), _pack(small_v))
    sd, sm2, sv2 = _unpack(sd, shapes_w), _unpack(sm2, shapes_w), _unpack(sv2, shapes_w)

    parts = [_aligned_to_nat(g_win_t, 0).reshape(N_DEV, IN_NAT // N_DEV, D), _cols_to_shards(g_wglu),
             g_wmo.reshape(N_DEV, 128, 1024), g_wo.reshape(N_DEV, 128, 1024), _cols_to_shards(g_wup),
             g_wdown.reshape(N_DEV, 512, 1024)]
    parts = [a.astype(BF16).reshape((4, 2) + a.shape[1:]) for a in parts]
    sib = _pair_exchange(parts)
    names = ["w_in", "s5_w_glu", "m_w_out", "w_o", "w_up", "w_down"]
    pair = [_pair_sum(g4, s, "pair_sum_" + nm_) for g4, s, nm_ in zip(parts, sib, names)]
    recv = _chip_exchange(pair)
    big_m = [m_w_in[0].T, m_s5_w_glu[0], m_m_w_out[0], m_w_o[0], m_w_up[0], m_w_down[0]]
    big_v = [v_w_in[0].T, v_s5_w_glu[0], v_m_w_out[0], v_w_o[0], v_w_up[0], v_w_down[0]]
    big_w = [w_in[0].T, s5_w_glu[0], m_w_out[0], w_o[0], w_up[0], w_down[0]]
    big_out = [_adamw_big(r, w, m, v, "adamw_" + nm_) for r, w, m, v, nm_ in zip(recv, big_w, big_m, big_v, names)]
    big_out[0] = [o.T for o in big_out[0]]

    order = ["meta_tokens", "ln0_g", "ln0_b", "w_in", "b_in", "qk_conv_w", "qk_conv_b", "s5_lambda_re", "s5_lambda_im",
             "s5_log_dt", "s5_b_re", "s5_b_im", "s5_c_re", "s5_c_im", "s5_d", "s5_w_glu", "m_norm_g", "m_w_out", "w_o",
             "ln1_g", "ln1_b", "w_up", "b_up", "w_down", "ln2_g", "ln2_b"]
    small_names = ["ln0_g", "ln0_b", "b_in", "qk_conv_b", "s5_lambda_re", "s5_lambda_im", "s5_log_dt", "s5_b_re",
                   "s5_b_im", "s5_c_re", "s5_c_im", "s5_d", "m_norm_g", "ln1_g", "ln1_b", "b_up", "ln2_g", "ln2_b",
                   "meta_tokens", "qk_conv_w"]
    res = {}
    for i, n in enumerate(small_names):
        res[n] = (small_g[i], sd[i], sm2[i], sv2[i])
    for i, n in enumerate(names):
        res[n] = tuple(o[None] for o in big_out[i])
    outs = [loss, grad_x]
    for kind in range(4):
        outs += [res[n][kind] for n in order]
    return tuple(outs)
```

```python
import functools
import math

import jax
import jax.numpy as jnp
from jax import lax
from jax.experimental import pallas as pl
from jax.experimental.pallas import tpu as pltpu

F32 = jnp.float32
BF16 = jnp.bfloat16

D_MODEL = 1024
N_META = 16
CHUNK = 64
PAD_ROWS = CHUNK - N_META
S5_WIDTH = 512
S5_GROUP = 16
S5_GROUPS = 32
S5_STATE = 64
S5_COLS = 2 * S5_GROUPS * S5_STATE
S5_BLK = 4
M_HEADS = 4
M_DK = 128
M_DV = 256
D_FF = 4096
N_DEV = 8
ALPHA = 2.0 ** 0.25
LN_EPS = 1e-5
IN_NAT = 5640
P_V, P_O, P_GS, P_GM, P_QK, P_U, P_IF, PW = 0, 1024, 2048, 3072, 4096, 5120, 5632, 5760
N_U, N_Q, N_K, N_V, N_O, N_I, N_GS, N_GM = 0, 512, 1024, 1536, 2560, 3584, 3592, 4616

ADAM_LR, ADAM_B1, ADAM_B2, ADAM_EPS, ADAM_WD, ADAM_STEP = 0.001, 0.9, 0.999, 1e-08, 0.01, 10

VMEM_LIMIT = 56 * 1024 * 1024
MESH = pl.DeviceIdType.MESH


def _pick(n, cands):
    for c in cands:
        if n % c == 0:
            return c
    raise ValueError(f"no tile for {n} among {cands}")


def _cparams(sem):
    return pltpu.CompilerParams(dimension_semantics=sem, vmem_limit_bytes=VMEM_LIMIT)


def _dot(a, b, ca, cb):
    return lax.dot_general(a, b, (((ca,), (cb,)), ((), ())), preferred_element_type=F32)


def _sigmoid(x):
    return 1.0 / (1.0 + jnp.exp(-x))


def _peer(k):
    x, y, c = lax.axis_index("x"), lax.axis_index("y"), lax.axis_index("c")
    px = 1 - x if k & 4 else x
    py = 1 - y if k & 2 else y
    pc = 1 - c if k & 1 else c
    return (px, py, pc), 4 * px + 2 * py + pc


def _my_id():
    return 4 * lax.axis_index("x") + 2 * lax.axis_index("y") + lax.axis_index("c")


def _hbm_call(body, name, arrs, out_shape, n_remote):
    n = len(arrs)
    return pl.pallas_call(
        body, name=name,
        out_shape=tuple(out_shape),
        in_specs=[pl.BlockSpec(memory_space=pl.ANY)] * n,
        out_specs=tuple([pl.BlockSpec(memory_space=pl.ANY)] * len(out_shape)),
        scratch_shapes=[pltpu.SemaphoreType.DMA((n, n_remote)),
                        pltpu.SemaphoreType.DMA((n, n_remote)),
                        pltpu.SemaphoreType.DMA((n,))],
    )(*arrs)


def _allgather(arrs):
    n = len(arrs)

    def body(*refs):
        ins, outs = refs[:n], refs[n:2 * n]
        send_sems, recv_sems, local_sems = refs[2 * n:]
        x, y, c = lax.axis_index("x"), lax.axis_index("y"), lax.axis_index("c")
        me, sibling = (x, y, c), (x, y, 1 - c)
        chips = [(1 - x, y), (x, 1 - y), (1 - x, 1 - y)]

        def slot(a, dev):
            return outs[a].at[4 * dev[0] + 2 * dev[1] + dev[2]]

        def copy(a, k, block, to, src=None):
            return pltpu.make_async_remote_copy(
                src_ref=slot(a, block) if src is None else src, dst_ref=slot(a, block),
                send_sem=send_sems.at[a, k], recv_sem=recv_sems.at[a, k],
                device_id=to, device_id_type=MESH)

        started = []
        for a in range(n):
            cp = pltpu.make_async_copy(ins[a], slot(a, me), local_sems.at[a])
            cp.start()
            started.append(cp)
        sends = []
        for a in range(n):
            sends.append(copy(a, 0, me, sibling, src=ins[a]))
            for j, chip in enumerate(chips):
                sends.append(copy(a, 1 + j, me, (*chip, c), src=ins[a]))
        for cp in sends:
            cp.start()
        for j, chip in enumerate(chips):
            for a in range(n):
                copy(a, 1 + j, (*chip, c), me).wait_recv()
                fwd = copy(a, 4 + j, (*chip, c), sibling)
                fwd.start()
                sends.append(fwd)
        for a in range(n):
            copy(a, 0, sibling, me).wait_recv()
            for j, chip in enumerate(chips):
                copy(a, 4 + j, (*chip, 1 - c), me).wait_recv()
        for cp in sends:
            cp.wait_send()
        for cp in started:
            cp.wait()

    out_shape = [jax.ShapeDtypeStruct((N_DEV,) + tuple(a.shape), a.dtype) for a in arrs]
    return _hbm_call(body, "allgather_weights", arrs, out_shape, N_DEV - 1)


def _pair_exchange(arrs):
    n = len(arrs)

    def body(*refs):
        ins, outs = refs[:n], refs[n:2 * n]
        send_sems, recv_sems, _ = refs[2 * n:]
        x, y, c = lax.axis_index("x"), lax.axis_index("y"), lax.axis_index("c")
        cps = []
        for a in range(n):
            cp = pltpu.make_async_remote_copy(
                src_ref=ins[a].at[pl.ds(0, 4), 1 - c], dst_ref=outs[a],
                send_sem=send_sems.at[a, 0], recv_sem=recv_sems.at[a, 0],
                device_id=(x, y, 1 - c), device_id_type=MESH)
            cp.start()
            cps.append(cp)
        for cp in cps:
            cp.wait()

    out_shape = [jax.ShapeDtypeStruct((4,) + tuple(a.shape[2:]), a.dtype) for a in arrs]
    return _hbm_call(body, "pair_exchange", arrs, out_shape, 1)


def _pair_sum(g4, sib, name):
    _, _, R, C = g4.shape
    tr, tc = _shard_tile(R, C)

    def body(c_ref, g_ref, s_ref, o_ref):
        o_ref[...] = (g_ref[:, 0].astype(F32) + s_ref[...].astype(F32)).astype(BF16)

    return pl.pallas_call(
        body, name=name,
        grid_spec=pltpu.PrefetchScalarGridSpec(
            num_scalar_prefetch=1, grid=(R // tr, C // tc),
            in_specs=[pl.BlockSpec((4, 1, tr, tc), lambda i, j, c_ref: (0, c_ref[0], i, j)),
                      pl.BlockSpec((4, tr, tc), lambda i, j, c_ref: (0, i, j))],
            out_specs=pl.BlockSpec((4, tr, tc), lambda i, j, c_ref: (0, i, j))),
        out_shape=jax.ShapeDtypeStruct((4, R, C), BF16),
        compiler_params=_cparams(("arbitrary", "arbitrary")),
    )(lax.axis_index("c").astype(jnp.int32).reshape(1), g4, sib)


def _chip_exchange(arrs):
    n = len(arrs)

    def body(*refs):
        ins, outs = refs[:n], refs[n:2 * n]
        send_sems, recv_sems, local_sems = refs[2 * n:]
        x, y, c = lax.axis_index("x"), lax.axis_index("y"), lax.axis_index("c")
        myq = 2 * x + y
        chips = [(1 - x, y), (x, 1 - y), (1 - x, 1 - y)]
        started, sends = [], []
        for a in range(n):
            cp = pltpu.make_async_copy(ins[a].at[myq], outs[a].at[myq], local_sems.at[a])
            cp.start()
            started.append(cp)
            for j, chip in enumerate(chips):
                cp = pltpu.make_async_remote_copy(
                    src_ref=ins[a].at[2 * chip[0] + chip[1]], dst_ref=outs[a].at[myq],
                    send_sem=send_sems.at[a, j], recv_sem=recv_sems.at[a, j],
                    device_id=(*chip, c), device_id_type=MESH)
                cp.start()
                sends.append(cp)
        for a in range(n):
            for j, chip in enumerate(chips):
                q = 2 * chip[0] + chip[1]
                pltpu.make_async_remote_copy(
                    src_ref=ins[a].at[q], dst_ref=outs[a].at[q],
                    send_sem=send_sems.at[a, j], recv_sem=recv_sems.at[a, j],
                    device_id=(*chip, c), device_id_type=MESH).wait_recv()
        for cp in sends:
            cp.wait_send()
        for cp in started:
            cp.wait()

    out_shape = [jax.ShapeDtypeStruct(tuple(a.shape), a.dtype) for a in arrs]
    return _hbm_call(body, "chip_exchange", arrs, out_shape, 3)


def _shard_tile(R, C):
    if R % 128 == 0:
        return 128, C
    return R, _pick(C, (256, 128))


def _allreduce_small(pack):
    rows = pack.shape[0]

    def body(in_ref, out_ref, buf, send_sems, recv_sems):
        me = _my_id()
        sends = []
        for k in range(1, N_DEV):
            peer, pid = _peer(k)
            cp = pltpu.make_async_remote_copy(
                src_ref=in_ref, dst_ref=buf.at[me],
                send_sem=send_sems.at[k - 1], recv_sem=recv_sems.at[k - 1],
                device_id=peer, device_id_type=MESH)
            cp.start()
            sends.append(cp)
        for k in range(1, N_DEV):
            peer, pid = _peer(k)
            pltpu.make_async_remote_copy(
                src_ref=in_ref, dst_ref=buf.at[pid],
                send_sem=send_sems.at[k - 1], recv_sem=recv_sems.at[k - 1],
                device_id=peer, device_id_type=MESH).wait_recv()
        for cp in sends:
            cp.wait_send()
        buf[pl.ds(me, 1)] = in_ref[...][None]
        acc = buf[0]
        for s in range(1, N_DEV):
            acc = acc + buf[s]
        out_ref[...] = acc

    return pl.pallas_call(
        body, name="allreduce_small",
        out_shape=jax.ShapeDtypeStruct((rows, 128), F32),
        in_specs=[pl.BlockSpec(memory_space=pltpu.VMEM)],
        out_specs=pl.BlockSpec(memory_space=pltpu.VMEM),
        scratch_shapes=[pltpu.VMEM((N_DEV, rows, 128), F32),
                        pltpu.SemaphoreType.DMA((N_DEV - 1,)),
                        pltpu.SemaphoreType.DMA((N_DEV - 1,))],
        compiler_params=pltpu.CompilerParams(vmem_limit_bytes=VMEM_LIMIT),
    )(pack)


def _mm_tiles(M, N, K, mode, out_bytes, n_extra_f32, a_bytes, b_bytes):
    budget = 40 * 1024 * 1024
    tms = [t for t in (1664, 1152, 1024, 640, 512, 256, 128) if M % t == 0]
    tns = [t for t in (1152, 1024, 640, 512, 384, 256, 128) if N % t == 0]
    if mode == "tn":
        tks = [t for t in (1664, 640, 512, 256, 128) if K % t == 0]
    else:
        tks = [K] if K <= 1152 else [t for t in (1152, 1024, 640, 512) if K % t == 0]
    best = None
    for tm in tms:
        for tn in tns:
            for tk in tks:
                nk = K // tk
                use = 2 * (tm * tk * a_bytes + tk * tn * b_bytes) + 2 * tm * tn * out_bytes
                use += 2 * n_extra_f32 * tm * tn * 4 + tm * tn * 4 * (2 if nk > 1 else 1)
                if use > budget:
                    continue
                score = (tm * tn * tk, tm * tn)
                if best is None or score > best[0]:
                    best = (score, (tm, tn, tk))
    assert best is not None, (M, N, K, mode)
    return best[1]


def _mm(a, b, mode, name, *, bias=None, add=None, add_scale=1.0, sqrt_gate_of=None,
        relu2_out=False, colsum=None, out_dtype=F32):
    if mode == "nn":
        (M, K), (K2, N) = a.shape, b.shape
    elif mode == "nt":
        (M, K), (N, K2) = a.shape, b.shape
    else:
        (K, M), (K2, N) = a.shape, b.shape
    assert K == K2, (a.shape, b.shape, mode)
    has_bias, has_add, has_gate = bias is not None, add is not None, sqrt_gate_of is not None
    tm, tn, tk = _mm_tiles(M, N, K, mode, jnp.dtype(out_dtype).itemsize, int(has_add) + int(has_gate),
                           a.dtype.itemsize, b.dtype.itemsize)
    nk = K // tk
    assert colsum is None or mode == "tn"
    assert colsum != "a" or N == tn

    def body(*refs):
        it = iter(refs)
        a_ref, b_ref = next(it), next(it)
        bias_ref = next(it) if has_bias else None
        add_ref = next(it) if has_add else None
        gate_ref = next(it) if has_gate else None
        o_ref = next(it)
        cs_ref = next(it) if colsum else None
        acc_ref = next(it) if nk > 1 else None
        i, k = pl.program_id(1), pl.program_id(2)

        av = a_ref[...].astype(BF16)
        bv = b_ref[...].astype(BF16)
        if mode == "nn":
            part = _dot(av, bv, 1, 0)
        elif mode == "nt":
            part = _dot(av, bv, 1, 1)
        else:
            part = _dot(av, bv, 0, 0)
        if colsum == "b":
            @pl.when((i == 0) & (k == 0))
            def _():
                cs_ref[...] = jnp.zeros_like(cs_ref)

            @pl.when(i == 0)
            def _():
                cs_ref[0:1, :] += jnp.sum(b_ref[...].astype(F32), axis=0, keepdims=True)
        if colsum == "a":
            @pl.when(k == 0)
            def _():
                cs_ref[...] = jnp.zeros_like(cs_ref)

            cs_ref[0:1, :] += jnp.sum(a_ref[...].astype(F32), axis=0, keepdims=True)

        def finish(r):
            if has_bias:
                r = r + bias_ref[...]
            if has_add:
                r = r + add_scale * add_ref[...]
            if has_gate:
                r = r * (2.0 * jnp.sqrt(gate_ref[...].astype(F32)))
            if relu2_out:
                r = jnp.square(jnp.maximum(r, 0.0))
            o_ref[...] = r.astype(out_dtype)

        if nk == 1:
            finish(part)
        else:
            @pl.when(k == 0)
            def _():
                acc_ref[...] = part

            @pl.when(k > 0)
            def _():
                acc_ref[...] += part

            @pl.when(k == nk - 1)
            def _():
                finish(acc_ref[...])

    if mode == "nn":
        a_spec = pl.BlockSpec((tm, tk), lambda j, i, k: (i, k))
        b_spec = pl.BlockSpec((tk, tn), lambda j, i, k: (k, j))
    elif mode == "nt":
        a_spec = pl.BlockSpec((tm, tk), lambda j, i, k: (i, k))
        b_spec = pl.BlockSpec((tn, tk), lambda j, i, k: (j, k))
    else:
        a_spec = pl.BlockSpec((tk, tm), lambda j, i, k: (k, i))
        b_spec = pl.BlockSpec((tk, tn), lambda j, i, k: (k, j))
    in_specs, args = [a_spec, b_spec], [a, b]
    if has_bias:
        in_specs.append(pl.BlockSpec((1, tn), lambda j, i, k: (0, j)))
        args.append(bias)
    if has_add:
        in_specs.append(pl.BlockSpec((tm, tn), lambda j, i, k: (i, j)))
        args.append(add)
    if has_gate:
        in_specs.append(pl.BlockSpec((tm, tn), lambda j, i, k: (i, j)))
        args.append(sqrt_gate_of)
    out_shape = [jax.ShapeDtypeStruct((M, N), out_dtype)]
    out_specs = [pl.BlockSpec((tm, tn), lambda j, i, k: (i, j))]
    if colsum == "b":
        out_shape.append(jax.ShapeDtypeStruct((8, N), F32))
        out_specs.append(pl.BlockSpec((8, tn), lambda j, i, k: (0, j)))
    if colsum == "a":
        out_shape.append(jax.ShapeDtypeStruct((8, M), F32))
        out_specs.append(pl.BlockSpec((8, tm), lambda j, i, k: (0, i)))
    res = pl.pallas_call(
        body, name=name,
        grid=(N // tn, M // tm, nk),
        in_specs=in_specs, out_specs=tuple(out_specs), out_shape=tuple(out_shape),
        scratch_shapes=[pltpu.VMEM((tm, tn), F32)] if nk > 1 else [],
        compiler_params=_cparams(("arbitrary", "arbitrary", "arbitrary")),
    )(*args)
    return res if len(res) > 1 else res[0]


def _ln_rows(v, g, b):
    mu = jnp.mean(v, axis=-1, keepdims=True)
    xc = v - mu
    var = jnp.mean(xc * xc, axis=-1, keepdims=True)
    return xc * lax.rsqrt(var + LN_EPS) * g + b


def _ln_bwd_rows(dy, v, g):
    mu = jnp.mean(v, axis=-1, keepdims=True)
    xc = v - mu
    var = jnp.mean(xc * xc, axis=-1, keepdims=True)
    rstd = lax.rsqrt(var + LN_EPS)
    xhat = xc * rstd
    dxh = dy * g
    dv = rstd * (dxh - jnp.mean(dxh, axis=-1, keepdims=True)
                 - xhat * jnp.mean(dxh * xhat, axis=-1, keepdims=True))
    return dv, xhat


def _real_tile(S):
    return _pick(S, (512, 256, 128, 64))


def _real_rows(rb, ncols, lp):
    return pl.BlockSpec((pl.Element(rb), pl.Element(ncols)),
                        lambda bb, j: (pl.multiple_of(bb * lp + CHUNK + j * rb, CHUNK), 0))


def _head_rows(ncols, lp):
    return pl.BlockSpec((CHUNK, ncols), lambda bb: (bb * (lp // CHUNK), 0))


def _ln0_fwd(x, meta, g, b):
    B, S, D = x.shape
    lp = S + CHUNK
    rb = _real_tile(S)

    def body(x_ref, g_ref, b_ref, h_ref, hb_ref):
        y = _ln_rows(x_ref[0], g_ref[...], b_ref[...])
        h_ref[...] = y
        hb_ref[...] = y.astype(BF16)

    vec = pl.BlockSpec((1, D), lambda bb, j: (0, 0))
    h, hb = pl.pallas_call(
        body, name="ln0_fwd", grid=(B, S // rb),
        in_specs=[pl.BlockSpec((1, rb, D), lambda bb, j: (bb, j, 0)), vec, vec],
        out_specs=(_real_rows(rb, D, lp), _real_rows(rb, D, lp)),
        out_shape=(jax.ShapeDtypeStruct((B * lp, D), F32), jax.ShapeDtypeStruct((B * lp, D), BF16)),
        compiler_params=_cparams(("arbitrary", "arbitrary")),
    )(x, g, b)

    def head(meta_ref, g_ref, b_ref, hin_ref, hbin_ref, h_ref, hb_ref):
        m = _ln_rows(meta_ref[...], g_ref[...], b_ref[...])
        h_ref[0:PAD_ROWS, :] = jnp.zeros((PAD_ROWS, D), F32)
        h_ref[PAD_ROWS:CHUNK, :] = m
        hb_ref[0:PAD_ROWS, :] = jnp.zeros((PAD_ROWS, D), BF16)
        hb_ref[PAD_ROWS:CHUNK, :] = m.astype(BF16)

    vec1 = pl.BlockSpec((1, D), lambda bb: (0, 0))
    anyspec = pl.BlockSpec(memory_space=pl.ANY)
    return pl.pallas_call(
        head, name="ln0_head", grid=(B,),
        in_specs=[pl.BlockSpec((N_META, D), lambda bb: (0, 0)), vec1, vec1, anyspec, anyspec],
        out_specs=(_head_rows(D, lp), _head_rows(D, lp)),
        out_shape=(jax.ShapeDtypeStruct((B * lp, D), F32), jax.ShapeDtypeStruct((B * lp, D), BF16)),
        input_output_aliases={3: 0, 4: 1},
        compiler_params=_cparams(("arbitrary",)),
    )(meta, g, b, h, hb)


def _ln0_bwd(dh0, x, meta, g):
    B, S, D = x.shape
    lp = S + CHUNK
    rb = _real_tile(S)

    def body(dh_ref, x_ref, g_ref, dx_ref, dg_ref, db_ref):
        @pl.when((pl.program_id(0) == 0) & (pl.program_id(1) == 0))
        def _():
            dg_ref[...] = jnp.zeros_like(dg_ref)
            db_ref[...] = jnp.zeros_like(db_ref)

        dy = dh_ref[...]
        dv, xhat = _ln_bwd_rows(dy, x_ref[0], g_ref[...])
        dx_ref[0] = dv
        dg_ref[0:1, :] += jnp.sum(dy * xhat, axis=0, keepdims=True)
        db_ref[0:1, :] += jnp.sum(dy, axis=0, keepdims=True)

    const = lambda bb, j: (0, 0)
    xblk = pl.BlockSpec((1, rb, D), lambda bb, j: (bb, j, 0))
    acc_shape = jax.ShapeDtypeStruct((8, D), F32)
    dx, dg, db = pl.pallas_call(
        body, name="ln0_bwd", grid=(B, S // rb),
        in_specs=[_real_rows(rb, D, lp), xblk, pl.BlockSpec((1, D), const)],
        out_specs=(xblk, pl.BlockSpec((8, D), const), pl.BlockSpec((8, D), const)),
        out_shape=(jax.ShapeDtypeStruct((B, S, D), F32), acc_shape, acc_shape),
        compiler_params=_cparams(("arbitrary", "arbitrary")),
    )(dh0, x, g)

    def head(dh_ref, meta_ref, g_ref, dmeta_ref, dg_ref, db_ref):
        @pl.when(pl.program_id(0) == 0)
        def _():
            dmeta_ref[...] = jnp.zeros_like(dmeta_ref)
            dg_ref[...] = jnp.zeros_like(dg_ref)
            db_ref[...] = jnp.zeros_like(db_ref)

        dy = dh_ref[PAD_ROWS:CHUNK, :]
        dv, xhat = _ln_bwd_rows(dy, meta_ref[...], g_ref[...])
        dmeta_ref[...] += dv
        dg_ref[0:1, :] += jnp.sum(dy * xhat, axis=0, keepdims=True)
        db_ref[0:1, :] += jnp.sum(dy, axis=0, keepdims=True)

    c1 = lambda bb: (0, 0)
    dmeta, dgm, dbm = pl.pallas_call(
        head, name="ln0_bwd_head", grid=(B,),
        in_specs=[_head_rows(D, lp), pl.BlockSpec((N_META, D), c1), pl.BlockSpec((1, D), c1)],
        out_specs=(pl.BlockSpec((N_META, D), c1), pl.BlockSpec((8, D), c1), pl.BlockSpec((8, D), c1)),
        out_shape=(jax.ShapeDtypeStruct((N_META, D), F32), acc_shape, acc_shape),
        compiler_params=_cparams(("arbitrary",)),
    )(dh0, meta, g)
    return dx, dmeta, dg + dgm, db + dbm


def _ln_res_fwd(h_prev, r, g, b, name):
    T, D = h_prev.shape
    tr = _pick(T, (320, 256, 128, 64))

    def body(hp_ref, r_ref, g_ref, b_ref, pre_ref, h_ref, hb_ref):
        pre = ALPHA * hp_ref[...] + r_ref[...]
        y = _ln_rows(pre, g_ref[...], b_ref[...])
        pre_ref[...] = pre
        h_ref[...] = y
        hb_ref[...] = y.astype(BF16)

    row = pl.BlockSpec((tr, D), lambda i: (i, 0))
    vec = pl.BlockSpec((1, D), lambda i: (0, 0))
    return pl.pallas_call(
        body, name=name, grid=(T // tr,),
        in_specs=[row, row, vec, vec], out_specs=(row, row, row),
        out_shape=(jax.ShapeDtypeStruct((T, D), F32), jax.ShapeDtypeStruct((T, D), F32),
                   jax.ShapeDtypeStruct((T, D), BF16)),
        compiler_params=_cparams(("arbitrary",)),
    )(h_prev, r, g, b)


def _ln_bwd(dh, pre, g, name):
    T, D = dh.shape
    tr = _pick(T, (320, 256, 128, 64))

    def body(dh_ref, pre_ref, g_ref, dp_ref, dpb_ref, dg_ref, db_ref):
        @pl.when(pl.program_id(0) == 0)
        def _():
            dg_ref[...] = jnp.zeros_like(dg_ref)
            db_ref[...] = jnp.zeros_like(db_ref)

        dy = dh_ref[...]
        dv, xhat = _ln_bwd_rows(dy, pre_ref[...], g_ref[...])
        dp_ref[...] = dv
        dpb_ref[...] = dv.astype(BF16)
        dg_ref[0:1, :] += jnp.sum(dy * xhat, axis=0, keepdims=True)
        db_ref[0:1, :] += jnp.sum(dy, axis=0, keepdims=True)

    row = pl.BlockSpec((tr, D), lambda i: (i, 0))
    vec = pl.BlockSpec((1, D), lambda i: (0, 0))
    acc = pl.BlockSpec((8, D), lambda i: (0, 0))
    return pl.pallas_call(
        body, name=name, grid=(T // tr,),
        in_specs=[row, row, vec], out_specs=(row, row, acc, acc),
        out_shape=(jax.ShapeDtypeStruct((T, D), F32), jax.ShapeDtypeStruct((T, D), BF16),
                   jax.ShapeDtypeStruct((8, D), F32), jax.ShapeDtypeStruct((8, D), F32)),
        compiler_params=_cparams(("arbitrary",)),
    )(dh, pre, g)


def _ln2_loss(h1, ff, g, b, target):
    T, D = h1.shape
    B, S, _ = target.shape
    lp = S + CHUNK
    rb = _real_tile(S)

    def body(h_ref, ff_ref, g_ref, b_ref, t_ref, loss_ref, dp_ref, dpb_ref, dg_ref, db_ref):
        @pl.when((pl.program_id(0) == 0) & (pl.program_id(1) == 0))
        def _():
            loss_ref[...] = jnp.zeros_like(loss_ref)
            dg_ref[...] = jnp.zeros_like(dg_ref)
            db_ref[...] = jnp.zeros_like(db_ref)

        pre = ALPHA * h_ref[...] + ff_ref[...]
        gg = g_ref[...]
        y = _ln_rows(pre, gg, b_ref[...])
        err = y - t_ref[0]
        loss_ref[0:1, 0:1] += 0.5 * jnp.sum(jnp.mean(err * err, axis=-1, keepdims=True), axis=0, keepdims=True)
        dy = err * (1.0 / D)
        dv, xhat = _ln_bwd_rows(dy, pre, gg)
        dp_ref[...] = dv
        dpb_ref[...] = dv.astype(BF16)
        dg_ref[0:1, :] += jnp.sum(dy * xhat, axis=0, keepdims=True)
        db_ref[0:1, :] += jnp.sum(dy, axis=0, keepdims=True)

    row = _real_rows(rb, D, lp)
    const = lambda bb, j: (0, 0)
    loss, dp, dpb, dg, db = pl.pallas_call(
        body, name="ln2_loss", grid=(B, S // rb),
        in_specs=[row, row, pl.BlockSpec((1, D), const), pl.BlockSpec((1, D), const),
                  pl.BlockSpec((1, rb, D), lambda bb, j: (bb, j, 0))],
        out_specs=(pl.BlockSpec((8, 128), const), row, row,
                   pl.BlockSpec((8, D), const), pl.BlockSpec((8, D), const)),
        out_shape=(jax.ShapeDtypeStruct((8, 128), F32),
                   jax.ShapeDtypeStruct((T, D), F32), jax.ShapeDtypeStruct((T, D), BF16),
                   jax.ShapeDtypeStruct((8, D), F32), jax.ShapeDtypeStruct((8, D), F32)),
        compiler_params=_cparams(("arbitrary", "arbitrary")),
    )(h1, ff, g, b, target)

    def head(dpin_ref, dpbin_ref, dp_ref, dpb_ref):
        dp_ref[...] = jnp.zeros((CHUNK, D), F32)
        dpb_ref[...] = jnp.zeros((CHUNK, D), BF16)

    anyspec = pl.BlockSpec(memory_space=pl.ANY)
    dp, dpb = pl.pallas_call(
        head, name="ln2_head", grid=(B,),
        in_specs=[anyspec, anyspec],
        out_specs=(_head_rows(D, lp), _head_rows(D, lp)),
        out_shape=(jax.ShapeDtypeStruct((T, D), F32), jax.ShapeDtypeStruct((T, D), BF16)),
        input_output_aliases={0: 0, 1: 1},
        compiler_params=_cparams(("arbitrary",)),
    )(dp, dpb)
    return loss, dp, dpb, dg, db


def _s5_prep(lam_re, lam_im, log_dt, b_re, b_im, c_re, c_im):
    dt = jnp.exp(log_dt)[:, None]
    mag = jnp.exp(lam_re * dt)
    ar = mag * jnp.cos(lam_im * dt)
    ai = mag * jnp.sin(lam_im * dt)
    nr, ni = ar - 1.0, ai
    den = lam_re * lam_re + lam_im * lam_im
    cr = (nr * lam_re + ni * lam_im) / den
    ci = (ni * lam_re - nr * lam_im) / den
    bbr = cr[..., None] * b_re - ci[..., None] * b_im
    bbi = cr[..., None] * b_im + ci[..., None] * b_re
    eye = jnp.eye(8, dtype=F32)
    bb = jnp.stack([bbr, bbi]).reshape(2, S5_BLK, 8, S5_STATE, S5_GROUP)
    bblk = jnp.einsum("rbgph,gj->bghrjp", bb, eye).reshape(S5_BLK, 128, 1024)
    cc = jnp.stack([c_re, -c_im]).reshape(2, S5_BLK, 8, S5_GROUP, S5_STATE)
    cblk = jnp.einsum("rbghp,gj->brjpgh", cc, eye).reshape(S5_BLK, 1024, 128)
    return ar.reshape(1, 2048), ai.reshape(1, 2048), bblk, cblk


def _scan_tables(ar, ai):
    pr, pi = [ar], [ai]
    for _ in range(7):
        pr, pi = pr + [pr[-1] * ar - pi[-1] * ai], pi + [pr[-1] * ai + pi[-1] * ar]
    pw_r = jnp.concatenate(pr, axis=0)
    pw_i = jnp.concatenate(pi, axis=0)
    row = jnp.arange(8)[:, None]

    def tables(sign, reverse):
        rows = []
        for n, sh in ((0, 1), (1, 2), (3, 4)):
            mask = (row < 8 - sh) if reverse else (row >= sh)
            rows.append(jnp.where(mask, pw_r[n][None, :], 0.0))
            rows.append(jnp.where(mask, sign * pw_i[n][None, :], 0.0))
        cr_ = pw_r[::-1] if reverse else pw_r
        ci_ = pw_i[::-1] if reverse else pw_i
        rows += [cr_, sign * ci_]
        return jnp.stack(rows)

    return tables(1.0, False), tables(-1.0, True)


def _seg_scan(s_ref, row0, seg, lam_ref, tab_ref, carry_ref, reverse, cseg_ref=None, extra=None):
    sgn = -1.0 if reverse else 1.0
    take, edge = (0, 7) if reverse else (7, 0)
    rowid = lax.broadcasted_iota(jnp.int32, (8, 128), 0)
    all_pairs = [(blk * 1024 + j * 128, blk * 1024 + j * 128 + 512, blk * 512 + j * 128)
                 for blk in range(S5_BLK) for j in range(4)]

    def rows(it):
        i = (seg - 1 - it) if reverse else it
        return i, pl.multiple_of(row0 + i * 8, 8)

    for half in range(2):
        pairs = all_pairs[8 * half:8 * half + 8]

        def pass1(it, carry):
            _, r0 = rows(it)
            out = []
            for n, (cre, cim, tc) in enumerate(pairs):
                lr = lam_ref[0, :, tc:tc + 128]
                li = sgn * lam_ref[1, :, tc:tc + 128]
                pr, pi = carry[2 * n], carry[2 * n + 1]
                xr = lr * pr - li * pi + s_ref[pl.ds(r0, 8), cre:cre + 128]
                xi = lr * pi + li * pr + s_ref[pl.ds(r0, 8), cim:cim + 128]
                s_ref[pl.ds(r0, 8), cre:cre + 128] = xr
                s_ref[pl.ds(r0, 8), cim:cim + 128] = xi
                out += [xr, xi]
            return tuple(out)

        ends = lax.fori_loop(0, seg, pass1, tuple(jnp.zeros((8, 128), F32) for _ in range(16)))

        start = []
        for n, (cre, cim, tc) in enumerate(pairs):
            xr, xi = ends[2 * n], ends[2 * n + 1]
            for lvl, sh in enumerate((1, 2, 4)):
                lr = tab_ref[2 * lvl, :, tc:tc + 128]
                li = tab_ref[2 * lvl + 1, :, tc:tc + 128]
                shift = (8 - sh) if reverse else sh
                sr = pltpu.roll(xr, shift, 0)
                si = pltpu.roll(xi, shift, 0)
                xr, xi = xr + lr * sr - li * si, xi + lr * si + li * sr
            pr = tab_ref[6, :, tc:tc + 128]
            pi = tab_ref[7, :, tc:tc + 128]
            c_r = carry_ref[:, cre:cre + 128]
            c_i = carry_ref[:, cim:cim + 128]
            er, ei = xr + pr * c_r - pi * c_i, xi + pr * c_i + pi * c_r
            back = 7 if reverse else 1
            in_r = jnp.where(rowid == edge, c_r, pltpu.roll(er, back, 0))
            in_i = jnp.where(rowid == edge, c_i, pltpu.roll(ei, back, 0))
            nr = jnp.sum(jnp.where(rowid == take, er, 0.0), axis=0, keepdims=True)
            ni = jnp.sum(jnp.where(rowid == take, ei, 0.0), axis=0, keepdims=True)
            carry_ref[:, cre:cre + 128] = jnp.broadcast_to(nr, (8, 128))
            carry_ref[:, cim:cim + 128] = jnp.broadcast_to(ni, (8, 128))
            if cseg_ref is not None:
                cseg_ref[0:8, cre:cre + 128] = in_r
                cseg_ref[0:8, cim:cim + 128] = in_i
            start += [in_r, in_i]

        def pass2(it, carry):
            i, r0 = rows(it)
            out = []
            for n, (cre, cim, tc) in enumerate(pairs):
                lr = lam_ref[0, :, tc:tc + 128]
                li = sgn * lam_ref[1, :, tc:tc + 128]
                dr, di = carry[2 * n], carry[2 * n + 1]
                dr, di = lr * dr - li * di, lr * di + li * dr
                xr = s_ref[pl.ds(r0, 8), cre:cre + 128] + dr
                xi = s_ref[pl.ds(r0, 8), cim:cim + 128] + di
                s_ref[pl.ds(r0, 8), cre:cre + 128] = xr
                s_ref[pl.ds(r0, 8), cim:cim + 128] = xi
                if extra is not None:
                    extra(i, cre, cim, tc, xr, xi)
                out += [dr, di]
            return tuple(out)

        lax.fori_loop(0, seg, pass2, tuple(start))


def _to_segments(src_refs, dst_ref, seg, first=None):
    sub = lax.broadcasted_iota(jnp.int32, (8, 1), 0)
    for i in range(seg):
        for c, src in enumerate(src_refs):
            v = src[pl.ds(i, 8, stride=seg), :]
            if first is not None:
                v = jnp.where(first & (sub * seg + i < PAD_ROWS), 0.0, v)
            dst_ref[8 * i:8 * i + 8, c * 128:(c + 1) * 128] = v


def _from_segments(src_ref, dst_ref, seg, fn=None, zero_head=None):
    m = seg // 8
    for j in range(seg):
        for c in range(src_ref.shape[0]):
            v = src_ref[c, pl.ds(64 * (j % m) + j // m, 8, stride=8), :]
            if zero_head is not None and j < PAD_ROWS // 8:
                v = jnp.where(zero_head, 0.0, v)
            dst_ref[8 * j:8 * j + 8, c * 128:(c + 1) * 128] = v if fn is None else fn(v)


def _gelu(x):
    c = math.sqrt(2.0 / math.pi)
    t = jnp.tanh(c * (x + 0.044715 * x * x * x))
    return 0.5 * x * (1.0 + t)


def _gelu_grad(x):
    c = math.sqrt(2.0 / math.pi)
    t = jnp.tanh(c * (x + 0.044715 * x * x * x))
    return 0.5 * (1.0 + t) + 0.5 * x * (1.0 - t * t) * c * (1.0 + 3.0 * 0.044715 * x * x)


def _s5_tile(lp):
    return _pick(lp, (320, 256, 128, 64))


def _s5_fwd(p, bblk, cblk, lam8, tab_f, dskip, lp):
    T = p.shape[0]
    ts = _s5_tile(lp)
    seg = ts // 8
    nblk, per_seq = T // ts, lp // ts
    ucol = P_U // S5_WIDTH

    def body(u0, u1, u2, u3, b_ref, c_ref, lam_ref, tab_ref, d_ref, y_ref, gy_ref, cin_ref, s_sc, u_sc, y_sc, carry_sc):
        r = pl.program_id(0)
        first = (r % per_seq) == 0

        @pl.when(first)
        def _():
            carry_sc[...] = jnp.zeros_like(carry_sc)

        cin_ref[0] = carry_sc[...]
        _to_segments((u0, u1, u2, u3), u_sc, seg, first)
        u = u_sc[...]
        ub = u.astype(BF16)
        for blk in range(S5_BLK):
            s_sc[:, blk * 1024:(blk + 1) * 1024] = _dot(ub[:, blk * 128:(blk + 1) * 128], b_ref[blk], 1, 0)
        _seg_scan(s_sc, 0, seg, lam_ref, tab_ref, carry_sc, False)
        for blk in range(S5_BLK):
            sb = s_sc[:, blk * 1024:(blk + 1) * 1024].astype(BF16)
            y_sc[blk] = _dot(sb, c_ref[blk], 1, 0) + d_ref[:, blk * 128:(blk + 1) * 128] * u[:, blk * 128:(blk + 1) * 128]
        _from_segments(y_sc, y_ref, seg)
        gy_ref[...] = _gelu(y_ref[...]).astype(BF16)

    def ublock(c):
        return pl.BlockSpec((ts, 128), lambda r: (r, 4 * ucol + c))

    return pl.pallas_call(
        body, name="s5_fwd", grid=(nblk,),
        in_specs=[ublock(0), ublock(1), ublock(2), ublock(3),
                  pl.BlockSpec((S5_BLK, 128, 1024), lambda r: (0, 0, 0)),
                  pl.BlockSpec((S5_BLK, 1024, 128), lambda r: (0, 0, 0)),
                  pl.BlockSpec((2, 8, 2048), lambda r: (0, 0, 0)),
                  pl.BlockSpec((8, 8, 2048), lambda r: (0, 0, 0)),
                  pl.BlockSpec((1, S5_WIDTH), lambda r: (0, 0))],
        out_specs=(pl.BlockSpec((ts, S5_WIDTH), lambda r: (r, 0)),
                   pl.BlockSpec((ts, S5_WIDTH), lambda r: (r, 0)),
                   pl.BlockSpec((1, 8, S5_COLS), lambda r: (r, 0, 0))),
        out_shape=(jax.ShapeDtypeStruct((T, S5_WIDTH), F32),
                   jax.ShapeDtypeStruct((T, S5_WIDTH), BF16),
                   jax.ShapeDtypeStruct((nblk, 8, S5_COLS), F32)),
        scratch_shapes=[pltpu.VMEM((ts, S5_COLS), F32), pltpu.VMEM((ts, S5_WIDTH), F32),
                        pltpu.VMEM((S5_BLK, ts, 128), F32), pltpu.VMEM((8, S5_COLS), F32)],
        compiler_params=_cparams(("arbitrary",)),
    )(p, p, p, p, bblk.astype(BF16), cblk.astype(BF16), lam8, tab_f, dskip)


def _s5_bwd(p, y_pre, dgy, cin, bblk, cblk, lam8, tab_f, tab_b, dskip, lp):
    T = p.shape[0]
    ts = _s5_tile(lp)
    seg = ts // 8
    nblk, per_seq = T // ts, lp // ts
    ucol = P_U // S5_WIDTH

    def body(u0, u1, u2, u3, y0, y1, y2, y3, g0, g1, g2, g3, cin_ref, b_ref, bt_ref, c_ref, ct_ref, lam_ref, tf_ref,
             tb_ref, d_ref, du_ref, dbb_ref, dcb_ref, dlam_ref, dd_ref, s_sc, a_sc, u_sc, dy_sc, y_sc, w_sc,
             carry_sc, carry_b):
        t = pl.program_id(0)
        r = nblk - 1 - t
        first = (r % per_seq) == 0
        last = (r % per_seq) == per_seq - 1

        @pl.when(t == 0)
        def _():
            dbb_ref[...] = jnp.zeros_like(dbb_ref)
            dcb_ref[...] = jnp.zeros_like(dcb_ref)
            dlam_ref[...] = jnp.zeros_like(dlam_ref)
            dd_ref[...] = jnp.zeros_like(dd_ref)

        @pl.when(last)
        def _():
            carry_b[...] = jnp.zeros_like(carry_b)

        carry_sc[...] = cin_ref[0]
        _to_segments((u0, u1, u2, u3), u_sc, seg, first)
        u = u_sc[...]
        ub = u.astype(BF16)
        for blk in range(S5_BLK):
            s_sc[8:8 + ts, blk * 1024:(blk + 1) * 1024] = _dot(ub[:, blk * 128:(blk + 1) * 128], b_ref[blk], 1, 0)
        _seg_scan(s_sc, 8, seg, lam_ref, tf_ref, carry_sc, False, cseg_ref=s_sc)

        _to_segments((g0, g1, g2, g3), dy_sc, seg)
        _to_segments((y0, y1, y2, y3), y_sc, seg)
        dy = dy_sc[...] * _gelu_grad(y_sc[...])
        dy_sc[...] = dy
        dyb = dy.astype(BF16)
        dd_ref[0:1, :] += jnp.sum(dy * u, axis=0, keepdims=True)
        for blk in range(S5_BLK):
            a_sc[:, blk * 1024:(blk + 1) * 1024] = _dot(dyb[:, blk * 128:(blk + 1) * 128], ct_ref[blk], 1, 0)
            sb = s_sc[8:8 + ts, blk * 1024:(blk + 1) * 1024].astype(BF16)
            dcb_ref[blk] += _dot(sb, dyb[:, blk * 128:(blk + 1) * 128], 0, 0)

        def lam_grad(i, cre, cim, tc, a_r, a_i):
            r0 = pl.multiple_of(i * 8, 8)
            pr = s_sc[pl.ds(r0, 8), cre:cre + 128]
            pi = s_sc[pl.ds(r0, 8), cim:cim + 128]
            dlam_ref[0, :, tc:tc + 128] += a_r * pr + a_i * pi
            dlam_ref[1, :, tc:tc + 128] += a_i * pr - a_r * pi

        _seg_scan(a_sc, 0, seg, lam_ref, tb_ref, carry_b, True, extra=lam_grad)

        for blk in range(S5_BLK):
            ab = a_sc[:, blk * 1024:(blk + 1) * 1024].astype(BF16)
            w_sc[blk] = _dot(ab, bt_ref[blk], 1, 0) + d_ref[:, blk * 128:(blk + 1) * 128] * dy_sc[:, blk * 128:(blk + 1) * 128]
            dbb_ref[blk] += _dot(u_sc[:, blk * 128:(blk + 1) * 128].astype(BF16), ab, 0, 0)
        _from_segments(w_sc, y_sc, seg, zero_head=first)
        du_ref[...] = y_sc[...].astype(BF16)

    const3 = lambda t: (0, 0, 0)
    rev = lambda t: (nblk - 1 - t, 0)

    def lanes(c0):
        return [pl.BlockSpec((ts, 128), lambda t, cc=c0 + c: (nblk - 1 - t, cc)) for c in range(4)]

    return pl.pallas_call(
        body, name="s5_bwd", grid=(nblk,),
        in_specs=lanes(4 * ucol) + lanes(0) + lanes(0) + [
                  pl.BlockSpec((1, 8, S5_COLS), lambda t: (nblk - 1 - t, 0, 0)),
                  pl.BlockSpec((S5_BLK, 128, 1024), const3),
                  pl.BlockSpec((S5_BLK, 1024, 128), const3),
                  pl.BlockSpec((S5_BLK, 1024, 128), const3),
                  pl.BlockSpec((S5_BLK, 128, 1024), const3),
                  pl.BlockSpec((2, 8, 2048), const3),
                  pl.BlockSpec((8, 8, 2048), const3),
                  pl.BlockSpec((8, 8, 2048), const3),
                  pl.BlockSpec((1, S5_WIDTH), lambda t: (0, 0))],
        out_specs=(pl.BlockSpec((ts, S5_WIDTH), rev),
                   pl.BlockSpec((S5_BLK, 128, 1024), const3),
                   pl.BlockSpec((S5_BLK, 1024, 128), const3),
                   pl.BlockSpec((2, 8, 2048), const3),
                   pl.BlockSpec((8, S5_WIDTH), lambda t: (0, 0))),
        out_shape=(jax.ShapeDtypeStruct((T, S5_WIDTH), BF16),
                   jax.ShapeDtypeStruct((S5_BLK, 128, 1024), F32),
                   jax.ShapeDtypeStruct((S5_BLK, 1024, 128), F32),
                   jax.ShapeDtypeStruct((2, 8, 2048), F32),
                   jax.ShapeDtypeStruct((8, S5_WIDTH), F32)),
        scratch_shapes=[pltpu.VMEM((ts + 8, S5_COLS), F32), pltpu.VMEM((ts, S5_COLS), F32),
                        pltpu.VMEM((ts, S5_WIDTH), F32), pltpu.VMEM((ts, S5_WIDTH), F32),
                        pltpu.VMEM((ts, S5_WIDTH), F32), pltpu.VMEM((S5_BLK, ts, 128), F32),
                        pltpu.VMEM((8, S5_COLS), F32), pltpu.VMEM((8, S5_COLS), F32)],
        compiler_params=_cparams(("arbitrary",)),
    )(p, p, p, p, y_pre, y_pre, y_pre, y_pre, dgy, dgy, dgy, dgy, cin,
      bblk.astype(BF16), jnp.swapaxes(bblk, 1, 2).astype(BF16),
      cblk.astype(BF16), jnp.swapaxes(cblk, 1, 2).astype(BF16), lam8, tab_f, tab_b, dskip)


def _row_in_seq(i, tr, lp):
    rowid = lax.broadcasted_iota(jnp.int32, (tr, 1), 0)
    return (i * tr + rowid) % lp


def _conv_fwd(p, w, b, lp):
    T = p.shape[0]
    tr = _pick(T, (320, 256, 128, 64))
    c = P_QK // 1024

    def body(x_ref, xp_ref, w_ref, b_ref, o_ref):
        i = pl.program_id(0)
        pos = _row_in_seq(i, tr, lp)
        x = jnp.where(pos < PAD_ROWS, 0.0, x_ref[...])
        pos_p = (i * tr - 8 + lax.broadcasted_iota(jnp.int32, (8, 1), 0)) % lp
        xp = jnp.where((pos_p < PAD_ROWS) | (i == 0), 0.0, xp_ref[...])
        xx = jnp.concatenate([xp, x], axis=0)
        acc = b_ref[...] + w_ref[3:4, :] * x
        for s in (1, 2, 3):
            acc = acc + w_ref[3 - s:4 - s, :] * pltpu.roll(xx, s, 0)[8:8 + tr]
        o_ref[...] = acc * _sigmoid(acc)

    return pl.pallas_call(
        body, name="conv_fwd", grid=(T // tr,),
        in_specs=[pl.BlockSpec((tr, 1024), lambda i: (i, c)),
                  pl.BlockSpec((8, 1024), lambda i: (jnp.maximum(i * (tr // 8) - 1, 0), c)),
                  pl.BlockSpec((4, 1024), lambda i: (0, 0)),
                  pl.BlockSpec((1, 1024), lambda i: (0, 0))],
        out_specs=pl.BlockSpec((tr, 1024), lambda i: (i, 0)),
        out_shape=jax.ShapeDtypeStruct((T, 1024), F32),
        compiler_params=_cparams(("arbitrary",)),
    )(p, p, w, b)


def _conv_bwd(p, dqk, w, b, lp):
    T = p.shape[0]
    tr = _pick(T, (320, 256, 128, 64))
    c = P_QK // 1024
    nb = T // tr

    def body(x_ref, xp_ref, xn_ref, g_ref, gn_ref, w_ref, b_ref, dx_ref, dw_ref, db_ref):
        i = pl.program_id(0)

        @pl.when(i == 0)
        def _():
            dw_ref[...] = jnp.zeros_like(dw_ref)
            db_ref[...] = jnp.zeros_like(db_ref)

        def seqpos(off, n):
            return (i * tr + off + lax.broadcasted_iota(jnp.int32, (n, 1), 0)) % lp

        x = jnp.where(seqpos(0, tr) < PAD_ROWS, 0.0, x_ref[...])
        xp = jnp.where((seqpos(-8, 8) < PAD_ROWS) | (i == 0), 0.0, xp_ref[...])
        xn = jnp.where((seqpos(tr, 8) < PAD_ROWS) | (i == nb - 1), 0.0, xn_ref[...])
        xx = jnp.concatenate([xp, x, xn], axis=0)
        gg = jnp.concatenate([g_ref[...], gn_ref[...]], axis=0)
        n2 = tr + 8
        acc = b_ref[...] + w_ref[3:4, :] * xx[8:8 + n2]
        for s in (1, 2, 3):
            acc = acc + w_ref[3 - s:4 - s, :] * pltpu.roll(xx, s, 0)[8:8 + n2]
        sg = _sigmoid(acc)
        dpre = gg * (sg * (1.0 + acc * (1.0 - sg)))
        valid = jnp.concatenate(
            [seqpos(0, tr) >= PAD_ROWS, (seqpos(tr, 8) >= PAD_ROWS) & (i < nb - 1)], axis=0)
        dpre = jnp.where(valid, dpre, 0.0)
        d0 = dpre[0:tr]
        dx = w_ref[3:4, :] * d0
        for s in (1, 2, 3):
            dx = dx + w_ref[3 - s:4 - s, :] * pltpu.roll(dpre, n2 - s, 0)[0:tr]
        dx_ref[...] = jnp.where(seqpos(0, tr) < PAD_ROWS, 0.0, dx).astype(BF16)
        db_ref[0:1, :] += jnp.sum(d0, axis=0, keepdims=True)
        dw_ref[3:4, :] += jnp.sum(d0 * x, axis=0, keepdims=True)
        for s in (1, 2, 3):
            xs = pltpu.roll(xx, s, 0)[8:8 + tr]
            dw_ref[3 - s:4 - s, :] += jnp.sum(d0 * xs, axis=0, keepdims=True)

    t8 = tr // 8
    return pl.pallas_call(
        body, name="conv_bwd", grid=(nb,),
        in_specs=[pl.BlockSpec((tr, 1024), lambda i: (i, c)),
                  pl.BlockSpec((8, 1024), lambda i: (jnp.maximum(i * t8 - 1, 0), c)),
                  pl.BlockSpec((8, 1024), lambda i: (jnp.minimum((i + 1) * t8, nb * t8 - 1), c)),
                  pl.BlockSpec((tr, 1024), lambda i: (i, 0)),
                  pl.BlockSpec((8, 1024), lambda i: (jnp.minimum((i + 1) * t8, nb * t8 - 1), 0)),
                  pl.BlockSpec((4, 1024), lambda i: (0, 0)),
                  pl.BlockSpec((1, 1024), lambda i: (0, 0))],
        out_specs=(pl.BlockSpec((tr, 1024), lambda i: (i, 0)),
                   pl.BlockSpec((8, 1024), lambda i: (0, 0)),
                   pl.BlockSpec((8, 1024), lambda i: (0, 0))),
        out_shape=(jax.ShapeDtypeStruct((T, 1024), BF16),
                   jax.ShapeDtypeStruct((8, 1024), F32),
                   jax.ShapeDtypeStruct((8, 1024), F32)),
        compiler_params=_cparams(("arbitrary",)),
    )(p, p, p, dqk, dqk, w, b)


def _split3(x):
    hi = x.astype(BF16)
    r1 = x - hi.astype(F32)
    mid = r1.astype(BF16)
    lo = (r1 - mid.astype(F32)).astype(BF16)
    return hi, mid, lo


def _tri_sum(x, upper):
    r = lax.broadcasted_iota(jnp.int32, (CHUNK, CHUNK), 0)
    c = lax.broadcasted_iota(jnp.int32, (CHUNK, CHUNK), 1)
    tri = jnp.where((r <= c) if upper else (r >= c), 1.0, 0.0).astype(BF16)
    hi, mid, lo = _split3(x)
    return _dot(tri, hi, 1, 0) + _dot(tri, mid, 1, 0) + _dot(tri, lo, 1, 0)


def _lane_col(x, lane):
    l = lax.broadcasted_iota(jnp.int32, x.shape, 1)
    return jnp.sum(jnp.where(l == lane, x, 0.0), axis=1, keepdims=True)


def _to_row(col):
    r = lax.broadcasted_iota(jnp.int32, (CHUNK, CHUNK), 0)
    c = lax.broadcasted_iota(jnp.int32, (CHUNK, CHUNK), 1)
    return jnp.sum(jnp.where(r == c, col, 0.0), axis=0, keepdims=True)


def _to_col(row):
    r = lax.broadcasted_iota(jnp.int32, (CHUNK, CHUNK), 0)
    c = lax.broadcasted_iota(jnp.int32, (CHUNK, CHUNK), 1)
    return jnp.sum(jnp.where(r == c, row, 0.0), axis=1, keepdims=True)


def _log_sigmoid(x):
    return jnp.minimum(x, 0.0) - jnp.log(1.0 + jnp.exp(-jnp.abs(x)))


def _mlstm_gates(ifv, padmask):
    lf = jnp.where(padmask, 0.0, _log_sigmoid(ifv))
    b_all = _tri_sum(lf, False)
    li = jnp.where(padmask, -jnp.inf, ifv)
    return li, b_all


def _mlstm_head_fwd(q, k, v, li_col, b_col, c_st, n_st, m_st):
    r = lax.broadcasted_iota(jnp.int32, (CHUNK, CHUNK), 0)
    c = lax.broadcasted_iota(jnp.int32, (CHUNK, CHUNK), 1)
    rowid = lax.broadcasted_iota(jnp.int32, (CHUNK, 1), 0)
    b_row = _to_row(b_col)
    li_row = _to_row(li_col)
    dmat = jnp.where(r >= c, b_col - b_row + li_row, -jnp.inf)
    m_inter = b_col + m_st
    m_row = jnp.maximum(m_inter, jnp.max(dmat, axis=1, keepdims=True))
    w_intra = jnp.exp(dmat - m_row)
    w_inter = jnp.exp(m_inter - m_row)
    qb, kb, vb = q.astype(BF16), k.astype(BF16), v.astype(BF16)
    qk = _dot(qb, kb, 1, 1)
    s = qk * w_intra
    cb = c_st.astype(BF16)
    qc = _dot(qb, cb, 1, 0)
    qn = jnp.sum(q * n_st, axis=1, keepdims=True)
    num = _dot(s.astype(BF16), vb, 1, 0) + w_inter * qc
    den = jnp.sum(s, axis=1, keepdims=True) + w_inter * qn
    floor = jnp.exp(-m_row)
    rinv = 1.0 / jnp.maximum(jnp.abs(den), floor)
    h = num * rinv
    b_last = jnp.sum(jnp.where(rowid == CHUNK - 1, b_col, 0.0), axis=0, keepdims=True)
    g_col = b_last - b_col + li_col
    m_new = jnp.maximum(b_last + m_st, jnp.max(g_col, axis=0, keepdims=True))
    w_k = jnp.exp(g_col - m_new)
    decay = jnp.exp(b_last + m_st - m_new)
    kw = w_k * k
    c_new = decay * c_st + _dot(kw.astype(BF16), vb, 0, 0)
    n_new = decay * n_st + jnp.sum(kw, axis=0, keepdims=True)
    return dict(h=h, c_new=c_new, n_new=n_new, m_new=m_new, w_intra=w_intra, w_inter=w_inter, s=s,
                qc=qc, qn=qn, den=den, floor=floor, rinv=rinv, w_k=w_k, decay=decay, kw=kw,
                qb=qb, kb=kb, vb=vb, cb=cb)


def _mlstm_fwd(p, qk, lp):
    T = p.shape[0]
    nch = lp // CHUNK
    B = T // lp
    scale = M_DK ** -0.5

    def body(q_ref, k_ref, v_ref, if_ref, h_ref, cst_ref, nm_ref, c_sc, nm_sc):
        ci = pl.program_id(0)

        @pl.when(ci == 0)
        def _():
            c_sc[...] = jnp.zeros_like(c_sc)
            nm_sc[...] = jnp.zeros_like(nm_sc)

        rowid = lax.broadcasted_iota(jnp.int32, (CHUNK, 1), 0)
        padmask = (ci == 0) & (rowid < PAD_ROWS)
        for bb in range(B):
            cst_ref[bb, 0] = c_sc[bb]
            nm_ref[bb, 0] = nm_sc[bb]
            li_all, b_all = _mlstm_gates(if_ref[bb], padmask)
            for hd in range(M_HEADS):
                q = q_ref[bb, :, hd * M_DK:(hd + 1) * M_DK]
                k = k_ref[bb, :, hd * M_DK:(hd + 1) * M_DK] * scale
                v = v_ref[bb, :, hd * M_DV:(hd + 1) * M_DV]
                o = _mlstm_head_fwd(q, k, v, _lane_col(li_all, hd), _lane_col(b_all, M_HEADS + hd),
                                    c_sc[bb, hd * M_DK:(hd + 1) * M_DK, :], nm_sc[bb, hd:hd + 1, :],
                                    nm_sc[bb, M_HEADS + hd:M_HEADS + hd + 1, 0:1])
                h_ref[bb, :, hd * M_DV:(hd + 1) * M_DV] = o["h"]
                c_sc[bb, hd * M_DK:(hd + 1) * M_DK, :] = o["c_new"]
                nm_sc[bb, hd:hd + 1, :] = o["n_new"]
                nm_sc[bb, M_HEADS + hd:M_HEADS + hd + 1, :] = jnp.broadcast_to(o["m_new"], (1, 128))

    qk3, p3 = qk.reshape(B, lp, 1024), p.reshape(B, lp, PW)
    h, cst, nm = pl.pallas_call(
        body, name="mlstm_fwd", grid=(nch,),
        in_specs=[pl.BlockSpec((B, CHUNK, 512), lambda ci: (0, ci, 0)),
                  pl.BlockSpec((B, CHUNK, 512), lambda ci: (0, ci, 1)),
                  pl.BlockSpec((B, CHUNK, 1024), lambda ci: (0, ci, P_V // 1024)),
                  pl.BlockSpec((B, CHUNK, 128), lambda ci: (0, ci, P_IF // 128))],
        out_specs=(pl.BlockSpec((B, CHUNK, 1024), lambda ci: (0, ci, 0)),
                   pl.BlockSpec((B, 1, M_HEADS * M_DK, M_DV), lambda ci: (0, ci, 0, 0)),
                   pl.BlockSpec((B, 1, 8, 128), lambda ci: (0, ci, 0, 0))),
        out_shape=(jax.ShapeDtypeStruct((B, lp, 1024), F32),
                   jax.ShapeDtypeStruct((B, nch, M_HEADS * M_DK, M_DV), F32),
                   jax.ShapeDtypeStruct((B, nch, 8, 128), F32)),
        scratch_shapes=[pltpu.VMEM((B, M_HEADS * M_DK, M_DV), F32), pltpu.VMEM((B, 8, 128), F32)],
        compiler_params=_cparams(("arbitrary",)),
    )(qk3, qk3, p3, p3)
    return h.reshape(T, 1024), cst, nm


def _mlstm_bwd(p, qk, cst, nm, dh, lp):
    T = p.shape[0]
    nch = lp // CHUNK
    B = T // lp
    scale = M_DK ** -0.5

    def body(q_ref, k_ref, v_ref, if_ref, cst_ref, nm_ref, dh_ref, dqk_ref, dv_ref, dif_ref, dc_sc, dn_sc):
        t = pl.program_id(0)
        ci = nch - 1 - t

        @pl.when(t == 0)
        def _():
            dc_sc[...] = jnp.zeros_like(dc_sc)
            dn_sc[...] = jnp.zeros_like(dn_sc)

        for bb in range(B):
            one_sequence(bb, ci, q_ref, k_ref, v_ref, if_ref, cst_ref, nm_ref, dh_ref, dqk_ref, dv_ref, dif_ref,
                         dc_sc, dn_sc)

    def one_sequence(bb, ci, q_ref, k_ref, v_ref, if_ref, cst_ref, nm_ref, dh_ref, dqk_ref, dv_ref, dif_ref,
                     dc_sc, dn_sc):
        rowid = lax.broadcasted_iota(jnp.int32, (CHUNK, 1), 0)
        lane = lax.broadcasted_iota(jnp.int32, (CHUNK, 128), 1)
        padmask = (ci == 0) & (rowid < PAD_ROWS)
        ifv = if_ref[bb]
        li_all, b_all = _mlstm_gates(ifv, padmask)
        db_all = jnp.zeros((CHUNK, 128), F32)
        dli_all = jnp.zeros((CHUNK, 128), F32)
        for hd in range(M_HEADS):
            q = q_ref[bb, :, hd * M_DK:(hd + 1) * M_DK]
            k = k_ref[bb, :, hd * M_DK:(hd + 1) * M_DK] * scale
            v = v_ref[bb, :, hd * M_DV:(hd + 1) * M_DV]
            c_st = cst_ref[bb, 0, hd * M_DK:(hd + 1) * M_DK, :]
            n_st = nm_ref[bb, 0, hd:hd + 1, :]
            m_st = nm_ref[bb, 0, M_HEADS + hd:M_HEADS + hd + 1, 0:1]
            o = _mlstm_head_fwd(q, k, v, _lane_col(li_all, hd), _lane_col(b_all, M_HEADS + hd), c_st, n_st, m_st)
            dc_new = dc_sc[bb, hd * M_DK:(hd + 1) * M_DK, :]
            dn_new = dn_sc[bb, hd:hd + 1, :]
            dcb = dc_new.astype(BF16)
            dhh = dh_ref[bb, :, hd * M_DV:(hd + 1) * M_DV]
            dnum = dhh * o["rinv"]
            dhh_h = jnp.sum(dhh * o["h"], axis=1, keepdims=True)
            sgn = jnp.where(o["den"] >= 0.0, 1.0, -1.0)
            dden = jnp.where(jnp.abs(o["den"]) > o["floor"], -dhh_h * o["rinv"] * sgn, 0.0)
            dnb = dnum.astype(BF16)
            ds = _dot(dnb, o["vb"], 1, 1) + dden
            sb = o["s"].astype(BF16)
            kwb = o["kw"].astype(BF16)
            dv = _dot(sb, dnb, 0, 0) + _dot(kwb, dcb, 1, 0)
            dqk_m = (ds * o["w_intra"]).astype(BF16)
            wdn = o["w_inter"] * dnum
            wdd = o["w_inter"] * dden
            dq = _dot(dqk_m, o["kb"], 1, 0) + _dot(wdn.astype(BF16), o["cb"], 1, 1) + wdd * n_st
            vdc = _dot(o["vb"], dcb, 1, 1)
            dk = _dot(dqk_m, o["qb"], 0, 0) + o["w_k"] * (vdc + dn_new)
            dd = ds * o["s"]
            dd_col = _to_col(jnp.sum(dd, axis=0, keepdims=True))
            dmi = o["w_inter"] * (jnp.sum(dnum * o["qc"], axis=1, keepdims=True) + dden * o["qn"])
            dg = o["w_k"] * (jnp.sum(k * vdc, axis=1, keepdims=True) + jnp.sum(k * dn_new, axis=1, keepdims=True))
            d_blast = (o["decay"] * (jnp.sum(jnp.sum(dc_new * c_st, axis=1, keepdims=True), axis=0, keepdims=True)
                                     + jnp.sum(dn_new * n_st, axis=1, keepdims=True))
                       + jnp.sum(dg, axis=0, keepdims=True))
            db_col = jnp.sum(dd, axis=1, keepdims=True) - dd_col + dmi - dg
            db_col = db_col + jnp.where(rowid == CHUNK - 1, d_blast, 0.0)
            dli_col = dd_col + dg
            db_all = db_all + jnp.where(lane == M_HEADS + hd, db_col, 0.0)
            dli_all = dli_all + jnp.where(lane == hd, dli_col, 0.0)
            dc_sc[bb, hd * M_DK:(hd + 1) * M_DK, :] = o["decay"] * dc_new + _dot(o["qb"], wdn.astype(BF16), 0, 0)
            dn_sc[bb, hd:hd + 1, :] = o["decay"] * dn_new + jnp.sum(q * wdd, axis=0, keepdims=True)
            dqk_ref[bb, :, hd * M_DK:(hd + 1) * M_DK] = dq
            dqk_ref[bb, :, 512 + hd * M_DK:512 + (hd + 1) * M_DK] = dk * scale
            dv_ref[bb, :, hd * M_DV:(hd + 1) * M_DV] = dv.astype(BF16)
        dlf = _tri_sum(db_all, True)
        dif = dli_all + dlf * _sigmoid(-ifv)
        dif_ref[bb] = jnp.where(padmask | (lane >= 2 * M_HEADS), 0.0, dif).astype(BF16)

    def rev(cb):
        return lambda t: (0, nch - 1 - t, cb)

    rev4 = lambda t: (0, nch - 1 - t, 0, 0)
    qk3, p3, dh3 = qk.reshape(B, lp, 1024), p.reshape(B, lp, PW), dh.reshape(B, lp, 1024)
    dqk, dv, dif = pl.pallas_call(
        body, name="mlstm_bwd", grid=(nch,),
        in_specs=[pl.BlockSpec((B, CHUNK, 512), rev(0)),
                  pl.BlockSpec((B, CHUNK, 512), rev(1)),
                  pl.BlockSpec((B, CHUNK, 1024), rev(P_V // 1024)),
                  pl.BlockSpec((B, CHUNK, 128), rev(P_IF // 128)),
                  pl.BlockSpec((B, 1, M_HEADS * M_DK, M_DV), rev4),
                  pl.BlockSpec((B, 1, 8, 128), rev4),
                  pl.BlockSpec((B, CHUNK, 1024), rev(0))],
        out_specs=(pl.BlockSpec((B, CHUNK, 1024), rev(0)),
                   pl.BlockSpec((B, CHUNK, 1024), rev(0)),
                   pl.BlockSpec((B, CHUNK, 128), rev(0))),
        out_shape=(jax.ShapeDtypeStruct((B, lp, 1024), F32),
                   jax.ShapeDtypeStruct((B, lp, 1024), BF16),
                   jax.ShapeDtypeStruct((B, lp, 128), BF16)),
        scratch_shapes=[pltpu.VMEM((B, M_HEADS * M_DK, M_DV), F32), pltpu.VMEM((B, 8, 128), F32)],
        compiler_params=_cparams(("arbitrary",)),
    )(qk3, qk3, p3, p3, cst, nm, dh3)
    return dqk.reshape(T, 1024), dv.reshape(T, 1024), dif.reshape(T, 128)


def _headnorm_fwd(hm, p, g):
    T = hm.shape[0]
    tr = _pick(T, (320, 256, 128, 64))

    def body(h_ref, o_ref, g_ref, a_ref):
        for hd in range(M_HEADS):
            sl = slice(hd * M_DV, (hd + 1) * M_DV)
            hn = _ln_rows(h_ref[:, sl], g_ref[:, sl], 0.0)
            a_ref[:, sl] = (_sigmoid(o_ref[:, sl]) * hn).astype(BF16)

    return pl.pallas_call(
        body, name="headnorm_fwd", grid=(T // tr,),
        in_specs=[pl.BlockSpec((tr, 1024), lambda i: (i, 0)),
                  pl.BlockSpec((tr, 1024), lambda i: (i, P_O // 1024)),
                  pl.BlockSpec((1, 1024), lambda i: (0, 0))],
        out_specs=pl.BlockSpec((tr, 1024), lambda i: (i, 0)),
        out_shape=jax.ShapeDtypeStruct((T, 1024), BF16),
        compiler_params=_cparams(("arbitrary",)),
    )(hm, p, g)


def _headnorm_bwd(da, hm, p, g):
    T = hm.shape[0]
    tr = _pick(T, (320, 256, 128, 64))

    def body(da_ref, h_ref, o_ref, g_ref, dh_ref, do_ref, dg_ref):
        @pl.when(pl.program_id(0) == 0)
        def _():
            dg_ref[...] = jnp.zeros_like(dg_ref)

        for hd in range(M_HEADS):
            sl = slice(hd * M_DV, (hd + 1) * M_DV)
            gg = g_ref[:, sl]
            so = _sigmoid(o_ref[:, sl])
            da = da_ref[:, sl]
            dhn = da * so
            dv, xhat = _ln_bwd_rows(dhn, h_ref[:, sl], gg)
            dh_ref[:, sl] = dv
            do_ref[:, sl] = (da * (xhat * gg) * so * (1.0 - so)).astype(BF16)
            dg_ref[0:1, sl] += jnp.sum(dhn * xhat, axis=0, keepdims=True)

    row = pl.BlockSpec((tr, 1024), lambda i: (i, 0))
    return pl.pallas_call(
        body, name="headnorm_bwd", grid=(T // tr,),
        in_specs=[row, row, pl.BlockSpec((tr, 1024), lambda i: (i, P_O // 1024)),
                  pl.BlockSpec((1, 1024), lambda i: (0, 0))],
        out_specs=(row, row, pl.BlockSpec((8, 1024), lambda i: (0, 0))),
        out_shape=(jax.ShapeDtypeStruct((T, 1024), F32), jax.ShapeDtypeStruct((T, 1024), BF16),
                   jax.ShapeDtypeStruct((8, 1024), F32)),
        compiler_params=_cparams(("arbitrary",)),
    )(da, hm, p, g)


def _mix_fwd(z, ym, p):
    T = ym.shape[0]
    tr = _pick(T, (320, 256, 128, 64))

    def body(z1_ref, z2_ref, ym_ref, gs_ref, gm_ref, o_ref):
        ys = z1_ref[...] * _sigmoid(z2_ref[...])
        o_ref[...] = (_sigmoid(gs_ref[...]) * ys + _sigmoid(gm_ref[...]) * ym_ref[...]).astype(BF16)

    def col(cb):
        return pl.BlockSpec((tr, 1024), lambda i: (i, cb))

    return pl.pallas_call(
        body, name="mix_fwd", grid=(T // tr,),
        in_specs=[col(0), col(1), col(0), col(P_GS // 1024), col(P_GM // 1024)],
        out_specs=col(0),
        out_shape=jax.ShapeDtypeStruct((T, 1024), BF16),
        compiler_params=_cparams(("arbitrary",)),
    )(z, z, ym, p, p)


def _mix_bwd(dmix, z, ym, p):
    T = ym.shape[0]
    tr = _pick(T, (320, 256, 128, 64))

    def body(d_ref, z1_ref, z2_ref, ym_ref, gs_ref, gm_ref, dz_ref, dym_ref, dgs_ref, dgm_ref):
        d = d_ref[...]
        z1 = z1_ref[...]
        s2 = _sigmoid(z2_ref[...])
        ss = _sigmoid(gs_ref[...])
        sm = _sigmoid(gm_ref[...])
        ys = z1 * s2
        dys = d * ss
        dgs_ref[...] = (d * ys * ss * (1.0 - ss)).astype(BF16)
        dgm_ref[...] = (d * ym_ref[...] * sm * (1.0 - sm)).astype(BF16)
        dym_ref[...] = (d * sm).astype(BF16)
        dz_ref[:, 0:1024] = (dys * s2).astype(BF16)
        dz_ref[:, 1024:2048] = (dys * z1 * s2 * (1.0 - s2)).astype(BF16)

    def col(cb):
        return pl.BlockSpec((tr, 1024), lambda i: (i, cb))

    o = jax.ShapeDtypeStruct((T, 1024), BF16)
    return pl.pallas_call(
        body, name="mix_bwd", grid=(T // tr,),
        in_specs=[col(0), col(0), col(1), col(0), col(P_GS // 1024), col(P_GM // 1024)],
        out_specs=(pl.BlockSpec((tr, 2048), lambda i: (i, 0)),) + (col(0),) * 3,
        out_shape=(jax.ShapeDtypeStruct((T, 2048), BF16),) + (o,) * 3,
        compiler_params=_cparams(("arbitrary",)),
    )(dmix, z, z, ym, p, p)


def _adamw_math(w, g, m, v):
    m2 = ADAM_B1 * m + (1.0 - ADAM_B1) * g
    v2 = ADAM_B2 * v + (1.0 - ADAM_B2) * jnp.square(g)
    m_hat = m2 / (1.0 - ADAM_B1 ** ADAM_STEP)
    v_hat = v2 / (1.0 - ADAM_B2 ** ADAM_STEP)
    delta = -ADAM_LR * (m_hat / (jnp.sqrt(v_hat) + ADAM_EPS) + ADAM_WD * w)
    return delta, m2, v2


def _adamw_big(recv, w, m, v, name):
    R, C = w.shape
    tr, tc = _shard_tile(R, C)
    ns = recv.shape[0]

    def body(r_ref, w_ref, m_ref, v_ref, g_ref, d_ref, m2_ref, v2_ref):
        g = r_ref[0].astype(F32)
        for s in range(1, ns):
            g = g + r_ref[s].astype(F32)
        d, m2, v2 = _adamw_math(w_ref[...], g, m_ref[...], v_ref[...])
        g_ref[...] = g
        d_ref[...] = d
        m2_ref[...] = m2
        v2_ref[...] = v2

    blk = pl.BlockSpec((tr, tc), lambda i, j: (i, j))
    o = jax.ShapeDtypeStruct((R, C), F32)
    return pl.pallas_call(
        body, name=name, grid=(R // tr, C // tc),
        in_specs=[pl.BlockSpec((ns, tr, tc), lambda i, j: (0, i, j)), blk, blk, blk],
        out_specs=(blk,) * 4, out_shape=(o,) * 4,
        compiler_params=_cparams(("arbitrary", "arbitrary")),
    )(recv, w, m, v)


def _adamw_small(g, w, m, v):
    R = g.shape[0]

    def body(g_ref, w_ref, m_ref, v_ref, d_ref, m2_ref, v2_ref):
        d, m2, v2 = _adamw_math(w_ref[...], g_ref[...], m_ref[...], v_ref[...])
        d_ref[...] = d
        m2_ref[...] = m2
        v2_ref[...] = v2

    full = pl.BlockSpec((R, 128), lambda i: (0, 0))
    o = jax.ShapeDtypeStruct((R, 128), F32)
    return pl.pallas_call(
        body, name="adamw_small", grid=(1,),
        in_specs=[full] * 4, out_specs=(full,) * 3, out_shape=(o,) * 3,
        compiler_params=_cparams(("arbitrary",)),
    )(g, w, m, v)


def _pack(arrs):
    parts = []
    for a in arrs:
        f = a.reshape(-1).astype(F32)
        n = -(-f.shape[0] // 1024) * 1024
        parts.append(jnp.pad(f, (0, n - f.shape[0])))
    return jnp.concatenate(parts).reshape(-1, 128)


def _unpack(pack, shapes):
    flat = pack.reshape(-1)
    out, off = [], 0
    for shp in shapes:
        n = math.prod(shp)
        out.append(flat[off:off + n].reshape(shp))
        off += -(-n // 1024) * 1024
    return out


def _cols_from_shards(g):
    return jnp.transpose(g, (1, 0, 2)).reshape(g.shape[1], -1)


def _cols_to_shards(w):
    R = w.shape[0]
    return jnp.transpose(w.reshape(R, N_DEV, -1), (1, 0, 2))


def _nat_to_aligned(w, axis):
    def sl(start, size):
        return lax.slice_in_dim(w, start, start + size, axis=axis)

    pads = [(0, 0)] * w.ndim
    pads[axis] = (0, PW - P_IF - 8)
    return jnp.concatenate([sl(N_V, 1024), sl(N_O, 1024), sl(N_GS, 1024), sl(N_GM, 1024), sl(N_Q, 1024),
                            sl(N_U, 512), jnp.pad(sl(N_I, 8), pads)], axis=axis)


def _aligned_to_nat(w, axis):
    def sl(start, size):
        return lax.slice_in_dim(w, start, start + size, axis=axis)

    return jnp.concatenate([sl(P_U, 512), sl(P_QK, 1024), sl(P_V, 1024), sl(P_O, 1024), sl(P_IF, 8),
                            sl(P_GS, 1024), sl(P_GM, 1024)], axis=axis)


def kernel(x, meta_tokens, ln0_g, ln0_b, w_in, b_in, qk_conv_w, qk_conv_b, s5_lambda_re, s5_lambda_im, s5_log_dt, s5_b_re, s5_b_im, s5_c_re, s5_c_im, s5_d, s5_w_glu, m_norm_g, m_w_out, w_o, ln1_g, ln1_b, w_up, b_up, w_down, ln2_g, ln2_b, loss_target, m_meta_tokens, m_ln0_g, m_ln0_b, m_w_in, m_b_in, m_qk_conv_w, m_qk_conv_b, m_s5_lambda_re, m_s5_lambda_im, m_s5_log_dt, m_s5_b_re, m_s5_b_im, m_s5_c_re, m_s5_c_im, m_s5_d, m_s5_w_glu, m_m_norm_g, m_m_w_out, m_w_o, m_ln1_g, m_ln1_b, m_w_up, m_b_up, m_w_down, m_ln2_g, m_ln2_b, v_meta_tokens, v_ln0_g, v_ln0_b, v_w_in, v_b_in, v_qk_conv_w, v_qk_conv_b, v_s5_lambda_re, v_s5_lambda_im, v_s5_log_dt, v_s5_b_re, v_s5_b_im, v_s5_c_re, v_s5_c_im, v_s5_d, v_s5_w_glu, v_m_norm_g, v_m_w_out, v_w_o, v_ln1_g, v_ln1_b, v_w_up, v_b_up, v_w_down, v_ln2_g, v_ln2_b):
    B, S, D = x.shape
    lp = S + CHUNK
    me = _my_id()

    big = [w_in[0].T, s5_w_glu[0], m_w_out[0], w_o[0], w_up[0], w_down[0]]
    gathered = _allgather([a.astype(BF16) for a in big] + [meta_tokens, qk_conv_w[0]])
    win_t = _nat_to_aligned(gathered[0].reshape(IN_NAT, D), 0)
    wglu_f = _cols_from_shards(gathered[1])
    wmo_f = gathered[2].reshape(1024, 1024)
    wo_f = gathered[3].reshape(1024, 1024)
    wup_f = _cols_from_shards(gathered[4])
    wdown_f = gathered[5].reshape(D_FF, 1024)
    meta_f = _cols_from_shards(gathered[6])
    convw_f = _cols_from_shards(gathered[7])
    b_in_al = _nat_to_aligned(b_in, 1)
    ln0g, ln0b = ln0_g.reshape(1, D), ln0_b.reshape(1, D)

    s5_args = (s5_lambda_re[0], s5_lambda_im[0], s5_log_dt[0], s5_b_re[0], s5_b_im[0], s5_c_re[0], s5_c_im[0])
    (ar, ai, bblk, cblk), s5_vjp = jax.vjp(_s5_prep, *s5_args)
    seg = _s5_tile(lp) // 8
    mu_r, mu_i = ar, ai
    for _ in range(seg - 1):
        mu_r, mu_i = mu_r * ar - mu_i * ai, mu_r * ai + mu_i * ar
    tab_f, tab_b = _scan_tables(mu_r, mu_i)
    lam8 = jnp.stack([jnp.broadcast_to(ar, (8, 2048)), jnp.broadcast_to(ai, (8, 2048))])

    h0, h0b = _ln0_fwd(x, meta_f, ln0g, ln0b)
    p = _mm(h0b, win_t, "nt", "mm_in", bias=b_in_al)
    y_pre, gy, cin = _s5_fwd(p, bblk, cblk, lam8, tab_f, s5_d, lp)
    z = _mm(gy, wglu_f, "nn", "mm_glu")
    qk = _conv_fwd(p, convw_f, qk_conv_b, lp)
    hm, cst, nm = _mlstm_fwd(p, qk, lp)
    a_m = _headnorm_fwd(hm, p, m_norm_g)
    ym = _mm(a_m, wmo_f, "nn", "mm_mout")
    mix = _mix_fwd(z, ym, p)
    r1 = _mm(mix, wo_f, "nn", "mm_o")
    pre1, h1, h1b = _ln_res_fwd(h0, r1, ln1_g, ln1_b, "ln1_fwd")
    act = _mm(h1b, wup_f, "nn", "mm_up", bias=b_up, relu2_out=True, out_dtype=BF16)
    ff = _mm(act, wdown_f, "nn", "mm_down")
    loss_acc, dpre2, dpre2b, dg2, db2 = _ln2_loss(h1, ff, ln2_g, ln2_b, loss_target)

    d_up = _mm(dpre2b, wdown_f, "nt", "mm_d_act", sqrt_gate_of=act, out_dtype=BF16)
    g_wdown = _mm(act, dpre2b, "tn", "mm_g_wdown")
    g_wup, cs_up = _mm(h1b, d_up, "tn", "mm_g_wup", colsum="b")
    dh1 = _mm(d_up, wup_f, "nt", "mm_d_h1", add=dpre2, add_scale=ALPHA)
    dpre1, dpre1b, dg1, db1 = _ln_bwd(dh1, pre1, ln1_g, "ln1_bwd")
    g_wo = _mm(mix, dpre1b, "tn", "mm_g_wo")
    dmix = _mm(dpre1b, wo_f, "nt", "mm_d_mix")
    dz, dym, dgs, dgm = _mix_bwd(dmix, z, ym, p)
    g_wmo = _mm(a_m, dym, "tn", "mm_g_wmo")
    da = _mm(dym, wmo_f, "nt", "mm_d_a")
    dhm, do_pre, dg_norm = _headnorm_bwd(da, hm, p, m_norm_g)
    dqk, dv, dif = _mlstm_bwd(p, qk, cst, nm, dhm, lp)
    dqk_pre, dconv_w, dconv_b = _conv_bwd(p, dqk, convw_f, qk_conv_b, lp)
    g_wglu = _mm(gy, dz, "tn", "mm_g_wglu")
    dgy = _mm(dz, wglu_f, "nt", "mm_d_gy")
    du, dbblk, dcblk, dlam, dd = _s5_bwd(p, y_pre, dgy, cin, bblk, cblk, lam8, tab_f, tab_b, s5_d, lp)
    dp = jnp.concatenate([dv, do_pre, dgs, dgm, dqk_pre, du, dif], axis=1)
    g_win_t, cs_in = _mm(dp, h0b, "tn", "mm_g_win", colsum="a")
    dh0 = _mm(dp, win_t, "nn", "mm_d_h0", add=dpre1, add_scale=ALPHA)
    grad_x, dmeta, dg0, db0 = _ln0_bwd(dh0, x, meta_f, ln0g)

    dlam2 = jnp.sum(dlam, axis=1)
    s5_grads = s5_vjp((dlam2[0:1], dlam2[1:2], dbblk, dcblk))
    small_local = [
        loss_acc[0:1, 0:1], dg0[0:1], db0[0:1], _aligned_to_nat(cs_in[0:1], 1), dconv_b[0:1],
        s5_grads[0], s5_grads[1], s5_grads[2], s5_grads[3], s5_grads[4], s5_grads[5], s5_grads[6],
        dd[0:1], dg_norm[0:1], dg1[0:1], db1[0:1], cs_up[0:1], dg2[0:1], db2[0:1],
        dmeta, dconv_w[0:4]]
    small_shapes = [(), (D,), (D,), (1, IN_NAT), (1, 1024),
                    (1, 32, 64), (1, 32, 64), (1, 32), (1, 32, 64, 16), (1, 32, 64, 16), (1, 32, 16, 64), (1, 32, 16, 64),
                    (1, 512), (1, 1024), (1, 1024), (1, 1024), (1, D_FF), (1, 1024), (1, 1024),
                    (N_META, D), (4, 1024)]
    red = _unpack(_allreduce_small(_pack(small_local)), small_shapes)
    loss = red[0]
    g_meta = lax.dynamic_slice_in_dim(red[19], me * 128, 128, axis=1)
    g_convw = lax.dynamic_slice_in_dim(red[20], me * 128, 128, axis=1)[None]
    small_g = red[1:19] + [g_meta, g_convw]
    small_w = [ln0_g, ln0_b, b_in, qk_conv_b, s5_lambda_re, s5_lambda_im, s5_log_dt, s5_b_re, s5_b_im,
               s5_c_re, s5_c_im, s5_d, m_norm_g, ln1_g, ln1_b, b_up, ln2_g, ln2_b, meta_tokens, qk_conv_w]
    small_m = [m_ln0_g, m_ln0_b, m_b_in, m_qk_conv_b, m_s5_lambda_re, m_s5_lambda_im, m_s5_log_dt, m_s5_b_re,
               m_s5_b_im, m_s5_c_re, m_s5_c_im, m_s5_d, m_m_norm_g, m_ln1_g, m_ln1_b, m_b_up, m_ln2_g, m_ln2_b,
               m_meta_tokens, m_qk_conv_w]
    small_v = [v_ln0_g, v_ln0_b, v_b_in, v_qk_conv_b, v_s5_lambda_re, v_s5_lambda_im, v_s5_log_dt, v_s5_b_re,
               v_s5_b_im, v_s5_c_re, v_s5_c_im, v_s5_d, v_m_norm_g, v_ln1_g, v_ln1_b, v_b_up, v_ln2_g, v_ln2_b,
               v_meta_tokens, v_qk_conv_w]
    shapes_w = [tuple(w.shape) for w in small_w]
    small_g = [g.reshape(s) for g, s in zip(small_g, shapes_w)]
    sd, sm2, sv2 = _adamw_small(_pack(small_g), _pack(small_w), _pack(small_m), _pack(small_v))
    sd, sm2, sv2 = _unpack(sd, shapes_w), _unpack(sm2, shapes_w), _unpack(sv2, shapes_w)

    parts = [_aligned_to_nat(g_win_t, 0).reshape(N_DEV, IN_NAT // N_DEV, D), _cols_to_shards(g_wglu),
             g_wmo.reshape(N_DEV, 128, 1024), g_wo.reshape(N_DEV, 128, 1024), _cols_to_shards(g_wup),
             g_wdown.reshape(N_DEV, 512, 1024)]
    parts = [a.astype(BF16).reshape((4, 2) + a.shape[1:]) for a in parts]
    sib = _pair_exchange(parts)
    names = ["w_in", "s5_w_glu", "m_w_out", "w_o", "w_up", "w_down"]
    pair = [_pair_sum(g4, s, "pair_sum_" + nm_) for g4, s, nm_ in zip(parts, sib, names)]
    recv = _chip_exchange(pair)
    big_m = [m_w_in[0].T, m_s5_w_glu[0], m_m_w_out[0], m_w_o[0], m_w_up[0], m_w_down[0]]
    big_v = [v_w_in[0].T, v_s5_w_glu[0], v_m_w_out[0], v_w_o[0], v_w_up[0], v_w_down[0]]
    big_w = [w_in[0].T, s5_w_glu[0], m_w_out[0], w_o[0], w_up[0], w_down[0]]
    big_out = [_adamw_big(r, w, m, v, "adamw_" + nm_) for r, w, m, v, nm_ in zip(recv, big_w, big_m, big_v, names)]
    big_out[0] = [o.T for o in big_out[0]]

    order = ["meta_tokens", "ln0_g", "ln0_b", "w_in", "b_in", "qk_conv_w", "qk_conv_b", "s5_lambda_re", "s5_lambda_im",
             "s5_log_dt", "s5_b_re", "s5_b_im", "s5_c_re", "s5_c_im", "s5_d", "s5_w_glu", "m_norm_g", "m_w_out", "w_o",
             "ln1_g", "ln1_b", "w_up", "b_up", "w_down", "ln2_g", "ln2_b"]
    small_names = ["ln0_g", "ln0_b", "b_in", "qk_conv_b", "s5_lambda_re", "s5_lambda_im", "s5_log_dt", "s5_b_re",
                   "s5_b_im", "s5_c_re", "s5_c_im", "s5_d", "m_norm_g", "ln1_g", "ln1_b", "b_up", "ln2_g", "ln2_b",
                   "meta_tokens", "qk_conv_w"]
    res = {}
    for i, n in enumerate(small_names):
        res[n] = (small_g[i], sd[i], sm2[i], sv2[i])
    for i, n in enumerate(names):
        res[n] = tuple(o[None] for o in big_out[i])
    outs = [loss, grad_x]
    for kind in range(4):
        outs += [res[n][kind] for n in order]
    return tuple(outs)
```

```python
import functools
import math

import jax
import jax.numpy as jnp
from jax import lax
from jax.experimental import pallas as pl
from jax.experimental.pallas import tpu as pltpu

F32 = jnp.float32
BF16 = jnp.bfloat16

D_MODEL = 1024
N_META = 16
CHUNK = 64
PAD_ROWS = CHUNK - N_META
S5_WIDTH = 512
S5_GROUP = 16
S5_GROUPS = 32
S5_STATE = 64
S5_COLS = 2 * S5_GROUPS * S5_STATE
S5_BLK = 4
M_HEADS = 4
M_DK = 128
M_DV = 256
D_FF = 4096
N_DEV = 8
ALPHA = 2.0 ** 0.25
LN_EPS = 1e-5
IN_NAT = 5640
P_V, P_O, P_GS, P_GM, P_QK, P_U, P_IF, PW = 0, 1024, 2048, 3072, 4096, 5120, 5632, 5760
N_U, N_Q, N_K, N_V, N_O, N_I, N_GS, N_GM = 0, 512, 1024, 1536, 2560, 3584, 3592, 4616

ADAM_LR, ADAM_B1, ADAM_B2, ADAM_EPS, ADAM_WD, ADAM_STEP = 0.001, 0.9, 0.999, 1e-08, 0.01, 10

VMEM_LIMIT = 56 * 1024 * 1024
MESH = pl.DeviceIdType.MESH


def _pick(n, cands):
    for c in cands:
        if n % c == 0:
            return c
    raise ValueError(f"no tile for {n} among {cands}")


def _cparams(sem):
    return pltpu.CompilerParams(dimension_semantics=sem, vmem_limit_bytes=VMEM_LIMIT)


def _dot(a, b, ca, cb):
    return lax.dot_general(a, b, (((ca,), (cb,)), ((), ())), preferred_element_type=F32)


def _sigmoid(x):
    return 1.0 / (1.0 + jnp.exp(-x))


def _peer(k):
    x, y, c = lax.axis_index("x"), lax.axis_index("y"), lax.axis_index("c")
    px = 1 - x if k & 4 else x
    py = 1 - y if k & 2 else y
    pc = 1 - c if k & 1 else c
    return (px, py, pc), 4 * px + 2 * py + pc


def _my_id():
    return 4 * lax.axis_index("x") + 2 * lax.axis_index("y") + lax.axis_index("c")


def _hbm_call(body, name, arrs, out_shape, n_remote):
    n = len(arrs)
    return pl.pallas_call(
        body, name=name,
        out_shape=tuple(out_shape),
        in_specs=[pl.BlockSpec(memory_space=pl.ANY)] * n,
        out_specs=tuple([pl.BlockSpec(memory_space=pl.ANY)] * len(out_shape)),
        scratch_shapes=[pltpu.SemaphoreType.DMA((n, n_remote)),
                        pltpu.SemaphoreType.DMA((n, n_remote)),
                        pltpu.SemaphoreType.DMA((n,))],
    )(*arrs)


class _Gather:
    def __init__(self, ins, outs, send_sems, recv_sems, local_sems):
        self.ins, self.outs, self.n = ins, outs, len(ins)
        self.send_sems, self.recv_sems, self.local_sems = send_sems, recv_sems, local_sems
        x, y, c = lax.axis_index("x"), lax.axis_index("y"), lax.axis_index("c")
        self.c = c
        self.me, self.sibling = (x, y, c), (x, y, 1 - c)
        self.chips = [(1 - x, y), (x, 1 - y), (1 - x, 1 - y)]

    def slot(self, a, dev):
        return self.outs[a].at[4 * dev[0] + 2 * dev[1] + dev[2]]

    def copy(self, a, k, block, to, own=False):
        return pltpu.make_async_remote_copy(
            src_ref=self.ins[a] if own else self.slot(a, block), dst_ref=self.slot(a, block),
            send_sem=self.send_sems.at[a, k], recv_sem=self.recv_sems.at[a, k],
            device_id=to, device_id_type=MESH)

    def first_sends(self, a):
        return [self.copy(a, 0, self.me, self.sibling, own=True)] + [
            self.copy(a, 1 + j, self.me, (*chip, self.c), own=True) for j, chip in enumerate(self.chips)]

    def start(self):
        for a in range(self.n):
            pltpu.make_async_copy(self.ins[a], self.slot(a, self.me), self.local_sems.at[a]).start()
            for cp in self.first_sends(a):
                cp.start()

    def finish(self):
        forwards = []
        for j, chip in enumerate(self.chips):
            for a in range(self.n):
                self.copy(a, 1 + j, (*chip, self.c), self.me).wait_recv()
                fwd = self.copy(a, 4 + j, (*chip, self.c), self.sibling)
                fwd.start()
                forwards.append(fwd)
        for a in range(self.n):
            self.copy(a, 0, self.sibling, self.me).wait_recv()
            for j, chip in enumerate(self.chips):
                self.copy(a, 4 + j, (*chip, 1 - self.c), self.me).wait_recv()
        for a in range(self.n):
            for cp in self.first_sends(a):
                cp.wait_send()
            pltpu.make_async_copy(self.ins[a], self.slot(a, self.me), self.local_sems.at[a]).wait()
        for cp in forwards:
            cp.wait_send()

    @staticmethod
    def out_shapes(arrs):
        return [jax.ShapeDtypeStruct((N_DEV,) + tuple(a.shape), a.dtype) for a in arrs]


class _AllToAll:
    def __init__(self, ins, outs, send_sems, recv_sems, local_sems):
        self.ins, self.outs, self.n = ins, outs, len(ins)
        self.send_sems, self.recv_sems, self.local_sems = send_sems, recv_sems, local_sems
        self.me = _my_id()

    def copy(self, a, k, landing):
        peer, pid = _peer(k)
        return pltpu.make_async_remote_copy(
            src_ref=self.ins[a].at[pid], dst_ref=self.outs[a].at[pid if landing else self.me],
            send_sem=self.send_sems.at[a, k - 1], recv_sem=self.recv_sems.at[a, k - 1],
            device_id=peer, device_id_type=MESH)

    def local(self, a):
        return pltpu.make_async_copy(self.ins[a].at[self.me], self.outs[a].at[self.me], self.local_sems.at[a])

    def start(self):
        for a in range(self.n):
            self.local(a).start()
            for k in range(1, N_DEV):
                self.copy(a, k, False).start()

    def finish(self):
        for a in range(self.n):
            for k in range(1, N_DEV):
                self.copy(a, k, True).wait_recv()
        for a in range(self.n):
            for k in range(1, N_DEV):
                self.copy(a, k, False).wait_send()
            self.local(a).wait()

    @staticmethod
    def out_shapes(arrs):
        return [jax.ShapeDtypeStruct(tuple(a.shape), a.dtype) for a in arrs]


def _comm_scratch(n):
    return [pltpu.SemaphoreType.DMA((n, N_DEV - 1)), pltpu.SemaphoreType.DMA((n, N_DEV - 1)),
            pltpu.SemaphoreType.DMA((n,))]


def _allgather(arrs):
    n = len(arrs)

    def body(*refs):
        g = _Gather(refs[:n], refs[n:2 * n], *refs[2 * n:])
        g.start()
        g.finish()

    return _hbm_call(body, "allgather_weights", arrs, _Gather.out_shapes(arrs), N_DEV - 1)


def _pair_exchange(arrs):
    n = len(arrs)

    def body(*refs):
        ins, outs = refs[:n], refs[n:2 * n]
        send_sems, recv_sems, _ = refs[2 * n:]
        x, y, c = lax.axis_index("x"), lax.axis_index("y"), lax.axis_index("c")
        cps = []
        for a in range(n):
            cp = pltpu.make_async_remote_copy(
                src_ref=ins[a].at[pl.ds(0, 4), 1 - c], dst_ref=outs[a],
                send_sem=send_sems.at[a, 0], recv_sem=recv_sems.at[a, 0],
                device_id=(x, y, 1 - c), device_id_type=MESH)
            cp.start()
            cps.append(cp)
        for cp in cps:
            cp.wait()

    out_shape = [jax.ShapeDtypeStruct((4,) + tuple(a.shape[2:]), a.dtype) for a in arrs]
    return _hbm_call(body, "pair_exchange", arrs, out_shape, 1)


def _pair_sum(g4, sib, name):
    _, _, R, C = g4.shape
    tr, tc = _shard_tile(R, C)

    def body(c_ref, g_ref, s_ref, o_ref):
        o_ref[...] = (g_ref[:, 0].astype(F32) + s_ref[...].astype(F32)).astype(BF16)

    return pl.pallas_call(
        body, name=name,
        grid_spec=pltpu.PrefetchScalarGridSpec(
            num_scalar_prefetch=1, grid=(R // tr, C // tc),
            in_specs=[pl.BlockSpec((4, 1, tr, tc), lambda i, j, c_ref: (0, c_ref[0], i, j)),
                      pl.BlockSpec((4, tr, tc), lambda i, j, c_ref: (0, i, j))],
            out_specs=pl.BlockSpec((4, tr, tc), lambda i, j, c_ref: (0, i, j))),
        out_shape=jax.ShapeDtypeStruct((4, R, C), BF16),
        compiler_params=_cparams(("arbitrary", "arbitrary")),
    )(lax.axis_index("c").astype(jnp.int32).reshape(1), g4, sib)


def _chip_exchange(arrs):
    n = len(arrs)

    def body(*refs):
        ins, outs = refs[:n], refs[n:2 * n]
        send_sems, recv_sems, local_sems = refs[2 * n:]
        x, y, c = lax.axis_index("x"), lax.axis_index("y"), lax.axis_index("c")
        myq = 2 * x + y
        chips = [(1 - x, y), (x, 1 - y), (1 - x, 1 - y)]
        started, sends = [], []
        for a in range(n):
            cp = pltpu.make_async_copy(ins[a].at[myq], outs[a].at[myq], local_sems.at[a])
            cp.start()
            started.append(cp)
            for j, chip in enumerate(chips):
                cp = pltpu.make_async_remote_copy(
                    src_ref=ins[a].at[2 * chip[0] + chip[1]], dst_ref=outs[a].at[myq],
                    send_sem=send_sems.at[a, j], recv_sem=recv_sems.at[a, j],
                    device_id=(*chip, c), device_id_type=MESH)
                cp.start()
                sends.append(cp)
        for a in range(n):
            for j, chip in enumerate(chips):
                q = 2 * chip[0] + chip[1]
                pltpu.make_async_remote_copy(
                    src_ref=ins[a].at[q], dst_ref=outs[a].at[q],
                    send_sem=send_sems.at[a, j], recv_sem=recv_sems.at[a, j],
                    device_id=(*chip, c), device_id_type=MESH).wait_recv()
        for cp in sends:
            cp.wait_send()
        for cp in started:
            cp.wait()

    out_shape = [jax.ShapeDtypeStruct(tuple(a.shape), a.dtype) for a in arrs]
    return _hbm_call(body, "chip_exchange", arrs, out_shape, 3)


def _shard_tile(R, C):
    if R % 128 == 0:
        return 128, C
    return R, _pick(C, (256, 128))


def _allreduce_small(pack):
    rows = pack.shape[0]

    def body(in_ref, out_ref, buf, send_sems, recv_sems):
        me = _my_id()
        sends = []
        for k in range(1, N_DEV):
            peer, pid = _peer(k)
            cp = pltpu.make_async_remote_copy(
                src_ref=in_ref, dst_ref=buf.at[me],
                send_sem=send_sems.at[k - 1], recv_sem=recv_sems.at[k - 1],
                device_id=peer, device_id_type=MESH)
            cp.start()
            sends.append(cp)
        for k in range(1, N_DEV):
            peer, pid = _peer(k)
            pltpu.make_async_remote_copy(
                src_ref=in_ref, dst_ref=buf.at[pid],
                send_sem=send_sems.at[k - 1], recv_sem=recv_sems.at[k - 1],
                device_id=peer, device_id_type=MESH).wait_recv()
        for cp in sends:
            cp.wait_send()
        buf[pl.ds(me, 1)] = in_ref[...][None]
        acc = buf[0]
        for s in range(1, N_DEV):
            acc = acc + buf[s]
        out_ref[...] = acc

    return pl.pallas_call(
        body, name="allreduce_small",
        out_shape=jax.ShapeDtypeStruct((rows, 128), F32),
        in_specs=[pl.BlockSpec(memory_space=pltpu.VMEM)],
        out_specs=pl.BlockSpec(memory_space=pltpu.VMEM),
        scratch_shapes=[pltpu.VMEM((N_DEV, rows, 128), F32),
                        pltpu.SemaphoreType.DMA((N_DEV - 1,)),
                        pltpu.SemaphoreType.DMA((N_DEV - 1,))],
        compiler_params=pltpu.CompilerParams(vmem_limit_bytes=VMEM_LIMIT),
    )(pack)


def _mm_tiles(M, N, K, mode, out_bytes, n_extra_f32, a_bytes, b_bytes):
    budget = 40 * 1024 * 1024
    tms = [t for t in (1664, 1152, 1024, 640, 512, 256, 128) if M % t == 0]
    tns = [t for t in (1152, 1024, 640, 512, 384, 256, 128) if N % t == 0]
    if mode == "tn":
        tks = [t for t in (1664, 640, 512, 256, 128) if K % t == 0]
    else:
        tks = [K] if K <= 1152 else [t for t in (1152, 1024, 640, 512) if K % t == 0]
    best = None
    for tm in tms:
        for tn in tns:
            for tk in tks:
                nk = K // tk
                use = 2 * (tm * tk * a_bytes + tk * tn * b_bytes) + 2 * tm * tn * out_bytes
                use += 2 * n_extra_f32 * tm * tn * 4 + tm * tn * 4 * (2 if nk > 1 else 1)
                if use > budget:
                    continue
                score = (tm * tn * tk, tm * tn)
                if best is None or score > best[0]:
                    best = (score, (tm, tn, tk))
    assert best is not None, (M, N, K, mode)
    return best[1]


def _mm(a, b, mode, name, *, bias=None, add=None, add_scale=1.0, sqrt_gate_of=None,
        relu2_out=False, colsum=None, out_dtype=F32, comm=None):
    if mode == "nn":
        (M, K), (K2, N) = a.shape, b.shape
    elif mode == "nt":
        (M, K), (N, K2) = a.shape, b.shape
    else:
        (K, M), (K2, N) = a.shape, b.shape
    assert K == K2, (a.shape, b.shape, mode)
    has_bias, has_add, has_gate = bias is not None, add is not None, sqrt_gate_of is not None
    tm, tn, tk = _mm_tiles(M, N, K, mode, jnp.dtype(out_dtype).itemsize, int(has_add) + int(has_gate),
                           a.dtype.itemsize, b.dtype.itemsize)
    nk = K // tk
    assert colsum is None or mode == "tn"
    assert colsum != "a" or N == tn
    comm_cls, comm_arrs = comm if comm is not None else (None, [])
    nc = len(comm_arrs)
    grid = (N // tn, M // tm, nk)

    def body(*refs):
        it = iter(refs)
        a_ref, b_ref = next(it), next(it)
        bias_ref = next(it) if has_bias else None
        add_ref = next(it) if has_add else None
        gate_ref = next(it) if has_gate else None
        comm_ins = [next(it) for _ in range(nc)]
        o_ref = next(it)
        cs_ref = next(it) if colsum else None
        comm_outs = [next(it) for _ in range(nc)]
        acc_ref = next(it) if nk > 1 else None
        j, i, k = pl.program_id(0), pl.program_id(1), pl.program_id(2)
        if nc:
            exchange = comm_cls(comm_ins, comm_outs, next(it), next(it), next(it))

            @pl.when((j == 0) & (i == 0) & (k == 0))
            def _():
                exchange.start()

        av = a_ref[...].astype(BF16)
        bv = b_ref[...].astype(BF16)
        if mode == "nn":
            part = _dot(av, bv, 1, 0)
        elif mode == "nt":
            part = _dot(av, bv, 1, 1)
        else:
            part = _dot(av, bv, 0, 0)
        if colsum == "b":
            @pl.when((i == 0) & (k == 0))
            def _():
                cs_ref[...] = jnp.zeros_like(cs_ref)

            @pl.when(i == 0)
            def _():
                cs_ref[0:1, :] += jnp.sum(b_ref[...].astype(F32), axis=0, keepdims=True)
        if colsum == "a":
            @pl.when(k == 0)
            def _():
                cs_ref[...] = jnp.zeros_like(cs_ref)

            cs_ref[0:1, :] += jnp.sum(a_ref[...].astype(F32), axis=0, keepdims=True)

        def finish(r):
            if has_bias:
                r = r + bias_ref[...]
            if has_add:
                r = r + add_scale * add_ref[...]
            if has_gate:
                r = r * (2.0 * jnp.sqrt(gate_ref[...].astype(F32)))
            if relu2_out:
                r = jnp.square(jnp.maximum(r, 0.0))
            o_ref[...] = r.astype(out_dtype)

        if nk == 1:
            finish(part)
        else:
            @pl.when(k == 0)
            def _():
                acc_ref[...] = part

            @pl.when(k > 0)
            def _():
                acc_ref[...] += part

            @pl.when(k == nk - 1)
            def _():
                finish(acc_ref[...])

        if nc:
            @pl.when((j == grid[0] - 1) & (i == grid[1] - 1) & (k == nk - 1))
            def _():
                exchange.finish()

    if mode == "nn":
        a_spec = pl.BlockSpec((tm, tk), lambda j, i, k: (i, k))
        b_spec = pl.BlockSpec((tk, tn), lambda j, i, k: (k, j))
    elif mode == "nt":
        a_spec = pl.BlockSpec((tm, tk), lambda j, i, k: (i, k))
        b_spec = pl.BlockSpec((tn, tk), lambda j, i, k: (j, k))
    else:
        a_spec = pl.BlockSpec((tk, tm), lambda j, i, k: (k, i))
        b_spec = pl.BlockSpec((tk, tn), lambda j, i, k: (k, j))
    in_specs, args = [a_spec, b_spec], [a, b]
    if has_bias:
        in_specs.append(pl.BlockSpec((1, tn), lambda j, i, k: (0, j)))
        args.append(bias)
    if has_add:
        in_specs.append(pl.BlockSpec((tm, tn), lambda j, i, k: (i, j)))
        args.append(add)
    if has_gate:
        in_specs.append(pl.BlockSpec((tm, tn), lambda j, i, k: (i, j)))
        args.append(sqrt_gate_of)
    out_shape = [jax.ShapeDtypeStruct((M, N), out_dtype)]
    out_specs = [pl.BlockSpec((tm, tn), lambda j, i, k: (i, j))]
    if colsum == "b":
        out_shape.append(jax.ShapeDtypeStruct((8, N), F32))
        out_specs.append(pl.BlockSpec((8, tn), lambda j, i, k: (0, j)))
    if colsum == "a":
        out_shape.append(jax.ShapeDtypeStruct((8, M), F32))
        out_specs.append(pl.BlockSpec((8, tm), lambda j, i, k: (0, i)))
    scratch = [pltpu.VMEM((tm, tn), F32)] if nk > 1 else []
    if nc:
        in_specs += [pl.BlockSpec(memory_space=pl.ANY)] * nc
        args += list(comm_arrs)
        out_specs += [pl.BlockSpec(memory_space=pl.ANY)] * nc
        out_shape += comm_cls.out_shapes(comm_arrs)
        scratch += _comm_scratch(nc)
    res = pl.pallas_call(
        body, name=name, grid=grid,
        in_specs=in_specs, out_specs=tuple(out_specs), out_shape=tuple(out_shape),
        scratch_shapes=scratch,
        compiler_params=_cparams(("arbitrary", "arbitrary", "arbitrary")),
    )(*args)
    return res if len(res) > 1 else res[0]


def _ln_rows(v, g, b):
    mu = jnp.mean(v, axis=-1, keepdims=True)
    xc = v - mu
    var = jnp.mean(xc * xc, axis=-1, keepdims=True)
    return xc * lax.rsqrt(var + LN_EPS) * g + b


def _ln_bwd_rows(dy, v, g):
    mu = jnp.mean(v, axis=-1, keepdims=True)
    xc = v - mu
    var = jnp.mean(xc * xc, axis=-1, keepdims=True)
    rstd = lax.rsqrt(var + LN_EPS)
    xhat = xc * rstd
    dxh = dy * g
    dv = rstd * (dxh - jnp.mean(dxh, axis=-1, keepdims=True)
                 - xhat * jnp.mean(dxh * xhat, axis=-1, keepdims=True))
    return dv, xhat


def _real_tile(S):
    return _pick(S, (512, 256, 128, 64))


def _real_rows(rb, ncols, lp):
    return pl.BlockSpec((pl.Element(rb), pl.Element(ncols)),
                        lambda bb, j: (pl.multiple_of(bb * lp + CHUNK + j * rb, CHUNK), 0))


def _head_rows(ncols, lp):
    return pl.BlockSpec((CHUNK, ncols), lambda bb: (bb * (lp // CHUNK), 0))


def _ln0_fwd(x, meta, g, b):
    B, S, D = x.shape
    lp = S + CHUNK
    rb = _real_tile(S)

    def body(x_ref, g_ref, b_ref, h_ref, hb_ref):
        y = _ln_rows(x_ref[0], g_ref[...], b_ref[...])
        h_ref[...] = y
        hb_ref[...] = y.astype(BF16)

    vec = pl.BlockSpec((1, D), lambda bb, j: (0, 0))
    h, hb = pl.pallas_call(
        body, name="ln0_fwd", grid=(B, S // rb),
        in_specs=[pl.BlockSpec((1, rb, D), lambda bb, j: (bb, j, 0)), vec, vec],
        out_specs=(_real_rows(rb, D, lp), _real_rows(rb, D, lp)),
        out_shape=(jax.ShapeDtypeStruct((B * lp, D), F32), jax.ShapeDtypeStruct((B * lp, D), BF16)),
        compiler_params=_cparams(("arbitrary", "arbitrary")),
    )(x, g, b)

    def head(meta_ref, g_ref, b_ref, hin_ref, hbin_ref, h_ref, hb_ref):
        m = _ln_rows(meta_ref[...], g_ref[...], b_ref[...])
        h_ref[0:PAD_ROWS, :] = jnp.zeros((PAD_ROWS, D), F32)
        h_ref[PAD_ROWS:CHUNK, :] = m
        hb_ref[0:PAD_ROWS, :] = jnp.zeros((PAD_ROWS, D), BF16)
        hb_ref[PAD_ROWS:CHUNK, :] = m.astype(BF16)

    vec1 = pl.BlockSpec((1, D), lambda bb: (0, 0))
    anyspec = pl.BlockSpec(memory_space=pl.ANY)
    return pl.pallas_call(
        head, name="ln0_head", grid=(B,),
        in_specs=[pl.BlockSpec((N_META, D), lambda bb: (0, 0)), vec1, vec1, anyspec, anyspec],
        out_specs=(_head_rows(D, lp), _head_rows(D, lp)),
        out_shape=(jax.ShapeDtypeStruct((B * lp, D), F32), jax.ShapeDtypeStruct((B * lp, D), BF16)),
        input_output_aliases={3: 0, 4: 1},
        compiler_params=_cparams(("arbitrary",)),
    )(meta, g, b, h, hb)


def _ln0_bwd(dh0, x, meta, g):
    B, S, D = x.shape
    lp = S + CHUNK
    rb = _real_tile(S)

    def body(dh_ref, x_ref, g_ref, dx_ref, dg_ref, db_ref):
        @pl.when((pl.program_id(0) == 0) & (pl.program_id(1) == 0))
        def _():
            dg_ref[...] = jnp.zeros_like(dg_ref)
            db_ref[...] = jnp.zeros_like(db_ref)

        dy = dh_ref[...]
        dv, xhat = _ln_bwd_rows(dy, x_ref[0], g_ref[...])
        dx_ref[0] = dv
        dg_ref[0:1, :] += jnp.sum(dy * xhat, axis=0, keepdims=True)
        db_ref[0:1, :] += jnp.sum(dy, axis=0, keepdims=True)

    const = lambda bb, j: (0, 0)
    xblk = pl.BlockSpec((1, rb, D), lambda bb, j: (bb, j, 0))
    acc_shape = jax.ShapeDtypeStruct((8, D), F32)
    dx, dg, db = pl.pallas_call(
        body, name="ln0_bwd", grid=(B, S // rb),
        in_specs=[_real_rows(rb, D, lp), xblk, pl.BlockSpec((1, D), const)],
        out_specs=(xblk, pl.BlockSpec((8, D), const), pl.BlockSpec((8, D), const)),
        out_shape=(jax.ShapeDtypeStruct((B, S, D), F32), acc_shape, acc_shape),
        compiler_params=_cparams(("arbitrary", "arbitrary")),
    )(dh0, x, g)

    def head(dh_ref, meta_ref, g_ref, dmeta_ref, dg_ref, db_ref):
        @pl.when(pl.program_id(0) == 0)
        def _():
            dmeta_ref[...] = jnp.zeros_like(dmeta_ref)
            dg_ref[...] = jnp.zeros_like(dg_ref)
            db_ref[...] = jnp.zeros_like(db_ref)

        dy = dh_ref[PAD_ROWS:CHUNK, :]
        dv, xhat = _ln_bwd_rows(dy, meta_ref[...], g_ref[...])
        dmeta_ref[...] += dv
        dg_ref[0:1, :] += jnp.sum(dy * xhat, axis=0, keepdims=True)
        db_ref[0:1, :] += jnp.sum(dy, axis=0, keepdims=True)

    c1 = lambda bb: (0, 0)
    dmeta, dgm, dbm = pl.pallas_call(
        head, name="ln0_bwd_head", grid=(B,),
        in_specs=[_head_rows(D, lp), pl.BlockSpec((N_META, D), c1), pl.BlockSpec((1, D), c1)],
        out_specs=(pl.BlockSpec((N_META, D), c1), pl.BlockSpec((8, D), c1), pl.BlockSpec((8, D), c1)),
        out_shape=(jax.ShapeDtypeStruct((N_META, D), F32), acc_shape, acc_shape),
        compiler_params=_cparams(("arbitrary",)),
    )(dh0, meta, g)
    return dx, dmeta, dg + dgm, db + dbm


def _ln_res_fwd(h_prev, r, g, b, name):
    T, D = h_prev.shape
    tr = _pick(T, (320, 256, 128, 64))

    def body(hp_ref, r_ref, g_ref, b_ref, pre_ref, h_ref, hb_ref):
        pre = ALPHA * hp_ref[...] + r_ref[...]
        y = _ln_rows(pre, g_ref[...], b_ref[...])
        pre_ref[...] = pre
        h_ref[...] = y
        hb_ref[...] = y.astype(BF16)

    row = pl.BlockSpec((tr, D), lambda i: (i, 0))
    vec = pl.BlockSpec((1, D), lambda i: (0, 0))
    return pl.pallas_call(
        body, name=name, grid=(T // tr,),
        in_specs=[row, row, vec, vec], out_specs=(row, row, row),
        out_shape=(jax.ShapeDtypeStruct((T, D), F32), jax.ShapeDtypeStruct((T, D), F32),
                   jax.ShapeDtypeStruct((T, D), BF16)),
        compiler_params=_cparams(("arbitrary",)),
    )(h_prev, r, g, b)


def _ln_bwd(dh, pre, g, name):
    T, D = dh.shape
    tr = _pick(T, (320, 256, 128, 64))

    def body(dh_ref, pre_ref, g_ref, dp_ref, dpb_ref, dg_ref, db_ref):
        @pl.when(pl.program_id(0) == 0)
        def _():
            dg_ref[...] = jnp.zeros_like(dg_ref)
            db_ref[...] = jnp.zeros_like(db_ref)

        dy = dh_ref[...]
        dv, xhat = _ln_bwd_rows(dy, pre_ref[...], g_ref[...])
        dp_ref[...] = dv
        dpb_ref[...] = dv.astype(BF16)
        dg_ref[0:1, :] += jnp.sum(dy * xhat, axis=0, keepdims=True)
        db_ref[0:1, :] += jnp.sum(dy, axis=0, keepdims=True)

    row = pl.BlockSpec((tr, D), lambda i: (i, 0))
    vec = pl.BlockSpec((1, D), lambda i: (0, 0))
    acc = pl.BlockSpec((8, D), lambda i: (0, 0))
    return pl.pallas_call(
        body, name=name, grid=(T // tr,),
        in_specs=[row, row, vec], out_specs=(row, row, acc, acc),
        out_shape=(jax.ShapeDtypeStruct((T, D), F32), jax.ShapeDtypeStruct((T, D), BF16),
                   jax.ShapeDtypeStruct((8, D), F32), jax.ShapeDtypeStruct((8, D), F32)),
        compiler_params=_cparams(("arbitrary",)),
    )(dh, pre, g)


def _ln2_loss(h1, ff, g, b, target):
    T, D = h1.shape
    B, S, _ = target.shape
    lp = S + CHUNK
    rb = _real_tile(S)

    def body(h_ref, ff_ref, g_ref, b_ref, t_ref, loss_ref, dp_ref, dpb_ref, dg_ref, db_ref):
        @pl.when((pl.program_id(0) == 0) & (pl.program_id(1) == 0))
        def _():
            loss_ref[...] = jnp.zeros_like(loss_ref)
            dg_ref[...] = jnp.zeros_like(dg_ref)
            db_ref[...] = jnp.zeros_like(db_ref)

        pre = ALPHA * h_ref[...] + ff_ref[...]
        gg = g_ref[...]
        y = _ln_rows(pre, gg, b_ref[...])
        err = y - t_ref[0]
        loss_ref[0:1, 0:1] += 0.5 * jnp.sum(jnp.mean(err * err, axis=-1, keepdims=True), axis=0, keepdims=True)
        dy = err * (1.0 / D)
        dv, xhat = _ln_bwd_rows(dy, pre, gg)
        dp_ref[...] = dv
        dpb_ref[...] = dv.astype(BF16)
        dg_ref[0:1, :] += jnp.sum(dy * xhat, axis=0, keepdims=True)
        db_ref[0:1, :] += jnp.sum(dy, axis=0, keepdims=True)

    row = _real_rows(rb, D, lp)
    const = lambda bb, j: (0, 0)
    loss, dp, dpb, dg, db = pl.pallas_call(
        body, name="ln2_loss", grid=(B, S // rb),
        in_specs=[row, row, pl.BlockSpec((1, D), const), pl.BlockSpec((1, D), const),
                  pl.BlockSpec((1, rb, D), lambda bb, j: (bb, j, 0))],
        out_specs=(pl.BlockSpec((8, 128), const), row, row,
                   pl.BlockSpec((8, D), const), pl.BlockSpec((8, D), const)),
        out_shape=(jax.ShapeDtypeStruct((8, 128), F32),
                   jax.ShapeDtypeStruct((T, D), F32), jax.ShapeDtypeStruct((T, D), BF16),
                   jax.ShapeDtypeStruct((8, D), F32), jax.ShapeDtypeStruct((8, D), F32)),
        compiler_params=_cparams(("arbitrary", "arbitrary")),
    )(h1, ff, g, b, target)

    def head(dpin_ref, dpbin_ref, dp_ref, dpb_ref):
        dp_ref[...] = jnp.zeros((CHUNK, D), F32)
        dpb_ref[...] = jnp.zeros((CHUNK, D), BF16)

    anyspec = pl.BlockSpec(memory_space=pl.ANY)
    dp, dpb = pl.pallas_call(
        head, name="ln2_head", grid=(B,),
        in_specs=[anyspec, anyspec],
        out_specs=(_head_rows(D, lp), _head_rows(D, lp)),
        out_shape=(jax.ShapeDtypeStruct((T, D), F32), jax.ShapeDtypeStruct((T, D), BF16)),
        input_output_aliases={0: 0, 1: 1},
        compiler_params=_cparams(("arbitrary",)),
    )(dp, dpb)
    return loss, dp, dpb, dg, db


def _s5_prep(lam_re, lam_im, log_dt, b_re, b_im, c_re, c_im):
    dt = jnp.exp(log_dt)[:, None]
    mag = jnp.exp(lam_re * dt)
    ar = mag * jnp.cos(lam_im * dt)
    ai = mag * jnp.sin(lam_im * dt)
    nr, ni = ar - 1.0, ai
    den = lam_re * lam_re + lam_im * lam_im
    cr = (nr * lam_re + ni * lam_im) / den
    ci = (ni * lam_re - nr * lam_im) / den
    bbr = cr[..., None] * b_re - ci[..., None] * b_im
    bbi = cr[..., None] * b_im + ci[..., None] * b_re
    eye = jnp.eye(8, dtype=F32)
    bb = jnp.stack([bbr, bbi]).reshape(2, S5_BLK, 8, S5_STATE, S5_GROUP)
    bblk = jnp.einsum("rbgph,gj->bghrjp", bb, eye).reshape(S5_BLK, 128, 1024)
    cc = jnp.stack([c_re, -c_im]).reshape(2, S5_BLK, 8, S5_GROUP, S5_STATE)
    cblk = jnp.einsum("rbghp,gj->brjpgh", cc, eye).reshape(S5_BLK, 1024, 128)
    return ar.reshape(1, 2048), ai.reshape(1, 2048), bblk, cblk


def _scan_tables(ar, ai):
    pr, pi = [ar], [ai]
    for _ in range(7):
        pr, pi = pr + [pr[-1] * ar - pi[-1] * ai], pi + [pr[-1] * ai + pi[-1] * ar]
    pw_r = jnp.concatenate(pr, axis=0)
    pw_i = jnp.concatenate(pi, axis=0)
    rev_r = jnp.concatenate(pr[::-1], axis=0)
    rev_i = jnp.concatenate(pi[::-1], axis=0)
    row = jnp.arange(8)[:, None]

    def tables(sign, reverse):
        rows = []
        for n, sh in ((0, 1), (1, 2), (3, 4)):
            mask = (row < 8 - sh) if reverse else (row >= sh)
            rows.append(jnp.where(mask, pw_r[n][None, :], 0.0))
            rows.append(jnp.where(mask, sign * pw_i[n][None, :], 0.0))
        cr_ = rev_r if reverse else pw_r
        ci_ = rev_i if reverse else pw_i
        rows += [cr_, sign * ci_]
        return jnp.stack(rows)

    return tables(1.0, False), tables(-1.0, True)


def _seg_scan(s_ref, row0, seg, lam_ref, tab_ref, carry_ref, reverse, cseg_ref=None, extra=None):
    sgn = -1.0 if reverse else 1.0
    take, edge = (0, 7) if reverse else (7, 0)
    rowid = lax.broadcasted_iota(jnp.int32, (8, 128), 0)
    all_pairs = [(blk * 1024 + j * 128, blk * 1024 + j * 128 + 512, blk * 512 + j * 128)
                 for blk in range(S5_BLK) for j in range(4)]

    def rows(it):
        i = (seg - 1 - it) if reverse else it
        return i, pl.multiple_of(row0 + i * 8, 8)

    for half in range(2):
        pairs = all_pairs[8 * half:8 * half + 8]

        def pass1(it, carry):
            _, r0 = rows(it)
            out = []
            for n, (cre, cim, tc) in enumerate(pairs):
                lr = lam_ref[0, :, tc:tc + 128]
                li = sgn * lam_ref[1, :, tc:tc + 128]
                pr, pi = carry[2 * n], carry[2 * n + 1]
                xr = lr * pr - li * pi + s_ref[pl.ds(r0, 8), cre:cre + 128]
                xi = lr * pi + li * pr + s_ref[pl.ds(r0, 8), cim:cim + 128]
                s_ref[pl.ds(r0, 8), cre:cre + 128] = xr
                s_ref[pl.ds(r0, 8), cim:cim + 128] = xi
                out += [xr, xi]
            return tuple(out)

        ends = lax.fori_loop(0, seg, pass1, tuple(jnp.zeros((8, 128), F32) for _ in range(16)))

        start = []
        for n, (cre, cim, tc) in enumerate(pairs):
            xr, xi = ends[2 * n], ends[2 * n + 1]
            for lvl, sh in enumerate((1, 2, 4)):
                lr = tab_ref[2 * lvl, :, tc:tc + 128]
                li = tab_ref[2 * lvl + 1, :, tc:tc + 128]
                shift = (8 - sh) if reverse else sh
                sr = pltpu.roll(xr, shift, 0)
                si = pltpu.roll(xi, shift, 0)
                xr, xi = xr + lr * sr - li * si, xi + lr * si + li * sr
            pr = tab_ref[6, :, tc:tc + 128]
            pi = tab_ref[7, :, tc:tc + 128]
            c_r = carry_ref[:, cre:cre + 128]
            c_i = carry_ref[:, cim:cim + 128]
            er, ei = xr + pr * c_r - pi * c_i, xi + pr * c_i + pi * c_r
            back = 7 if reverse else 1
            in_r = jnp.where(rowid == edge, c_r, pltpu.roll(er, back, 0))
            in_i = jnp.where(rowid == edge, c_i, pltpu.roll(ei, back, 0))
            nr = jnp.sum(jnp.where(rowid == take, er, 0.0), axis=0, keepdims=True)
            ni = jnp.sum(jnp.where(rowid == take, ei, 0.0), axis=0, keepdims=True)
            carry_ref[:, cre:cre + 128] = jnp.broadcast_to(nr, (8, 128))
            carry_ref[:, cim:cim + 128] = jnp.broadcast_to(ni, (8, 128))
            if cseg_ref is not None:
                cseg_ref[0:8, cre:cre + 128] = in_r
                cseg_ref[0:8, cim:cim + 128] = in_i
            start += [in_r, in_i]

        def pass2(it, carry):
            i, r0 = rows(it)
            out = []
            for n, (cre, cim, tc) in enumerate(pairs):
                lr = lam_ref[0, :, tc:tc + 128]
                li = sgn * lam_ref[1, :, tc:tc + 128]
                dr, di = carry[2 * n], carry[2 * n + 1]
                dr, di = lr * dr - li * di, lr * di + li * dr
                xr = s_ref[pl.ds(r0, 8), cre:cre + 128] + dr
                xi = s_ref[pl.ds(r0, 8), cim:cim + 128] + di
                s_ref[pl.ds(r0, 8), cre:cre + 128] = xr
                s_ref[pl.ds(r0, 8), cim:cim + 128] = xi
                if extra is not None:
                    extra(i, cre, cim, tc, xr, xi)
                out += [dr, di]
            return tuple(out)

        lax.fori_loop(0, seg, pass2, tuple(start))


def _to_segments(src_refs, dst_ref, seg, first=None):
    sub = lax.broadcasted_iota(jnp.int32, (8, 1), 0)
    for i in range(seg):
        for c, src in enumerate(src_refs):
            v = src[pl.ds(i, 8, stride=seg), :]
            if first is not None:
                v = jnp.where(first & (sub * seg + i < PAD_ROWS), 0.0, v)
            dst_ref[8 * i:8 * i + 8, c * 128:(c + 1) * 128] = v


def _from_segments(src_ref, dst_ref, seg, fn=None, zero_head=None):
    m = seg // 8
    for j in range(seg):
        for c in range(src_ref.shape[0]):
            v = src_ref[c, pl.ds(64 * (j % m) + j // m, 8, stride=8), :]
            if zero_head is not None and j < PAD_ROWS // 8:
                v = jnp.where(zero_head, 0.0, v)
            dst_ref[8 * j:8 * j + 8, c * 128:(c + 1) * 128] = v if fn is None else fn(v)


def _gelu(x):
    c = math.sqrt(2.0 / math.pi)
    t = jnp.tanh(c * (x + 0.044715 * x * x * x))
    return 0.5 * x * (1.0 + t)


def _gelu_grad(x):
    c = math.sqrt(2.0 / math.pi)
    t = jnp.tanh(c * (x + 0.044715 * x * x * x))
    return 0.5 * (1.0 + t) + 0.5 * x * (1.0 - t * t) * c * (1.0 + 3.0 * 0.044715 * x * x)


def _s5_tile(lp):
    return _pick(lp, (320, 256, 128, 64))


def _s5_fwd(p, bblk, cblk, lam8, tab_f, dskip, lp):
    T = p.shape[0]
    ts = _s5_tile(lp)
    seg = ts // 8
    nblk, per_seq = T // ts, lp // ts
    ucol = P_U // S5_WIDTH

    def body(u0, u1, u2, u3, b_ref, c_ref, lam_ref, tab_ref, d_ref, y_ref, gy_ref, cin_ref, s_sc, u_sc, y_sc, carry_sc):
        r = pl.program_id(0)
        first = (r % per_seq) == 0

        @pl.when(first)
        def _():
            carry_sc[...] = jnp.zeros_like(carry_sc)

        cin_ref[0] = carry_sc[...]
        _to_segments((u0, u1, u2, u3), u_sc, seg, first)
        u = u_sc[...]
        ub = u.astype(BF16)
        for blk in range(S5_BLK):
            s_sc[:, blk * 1024:(blk + 1) * 1024] = _dot(ub[:, blk * 128:(blk + 1) * 128], b_ref[blk], 1, 0)
        _seg_scan(s_sc, 0, seg, lam_ref, tab_ref, carry_sc, False)
        for blk in range(S5_BLK):
            sb = s_sc[:, blk * 1024:(blk + 1) * 1024].astype(BF16)
            y_sc[blk] = _dot(sb, c_ref[blk], 1, 0) + d_ref[:, blk * 128:(blk + 1) * 128] * u[:, blk * 128:(blk + 1) * 128]
        _from_segments(y_sc, y_ref, seg)
        gy_ref[...] = _gelu(y_ref[...]).astype(BF16)

    def ublock(c):
        return pl.BlockSpec((ts, 128), lambda r: (r, 4 * ucol + c))

    return pl.pallas_call(
        body, name="s5_fwd", grid=(nblk,),
        in_specs=[ublock(0), ublock(1), ublock(2), ublock(3),
                  pl.BlockSpec((S5_BLK, 128, 1024), lambda r: (0, 0, 0)),
                  pl.BlockSpec((S5_BLK, 1024, 128), lambda r: (0, 0, 0)),
                  pl.BlockSpec((2, 8, 2048), lambda r: (0, 0, 0)),
                  pl.BlockSpec((8, 8, 2048), lambda r: (0, 0, 0)),
                  pl.BlockSpec((1, S5_WIDTH), lambda r: (0, 0))],
        out_specs=(pl.BlockSpec((ts, S5_WIDTH), lambda r: (r, 0)),
                   pl.BlockSpec((ts, S5_WIDTH), lambda r: (r, 0)),
                   pl.BlockSpec((1, 8, S5_COLS), lambda r: (r, 0, 0))),
        out_shape=(jax.ShapeDtypeStruct((T, S5_WIDTH), F32),
                   jax.ShapeDtypeStruct((T, S5_WIDTH), BF16),
                   jax.ShapeDtypeStruct((nblk, 8, S5_COLS), F32)),
        scratch_shapes=[pltpu.VMEM((ts, S5_COLS), F32), pltpu.VMEM((ts, S5_WIDTH), F32),
                        pltpu.VMEM((S5_BLK, ts, 128), F32), pltpu.VMEM((8, S5_COLS), F32)],
        compiler_params=_cparams(("arbitrary",)),
    )(p, p, p, p, bblk.astype(BF16), cblk.astype(BF16), lam8, tab_f, dskip)


def _s5_bwd(p, y_pre, dgy, cin, bblk, cblk, lam8, tab_f, tab_b, dskip, dif, dp, lp):
    T = p.shape[0]
    ts = _s5_tile(lp)
    seg = ts // 8
    nblk, per_seq = T // ts, lp // ts
    ucol = P_U // S5_WIDTH
    assert P_IF == P_U + S5_WIDTH and P_U % (S5_WIDTH + 128) == 0

    def body(u0, u1, u2, u3, y0, y1, y2, y3, g0, g1, g2, g3, cin_ref, b_ref, bt_ref, c_ref, ct_ref, lam_ref, tf_ref,
             tb_ref, d_ref, dif_ref, dpin_ref, du_ref, dbb_ref, dcb_ref, dlam_ref, dd_ref, s_sc, a_sc, u_sc, dy_sc,
             y_sc, w_sc, carry_sc, carry_b):
        t = pl.program_id(0)
        r = nblk - 1 - t
        first = (r % per_seq) == 0
        last = (r % per_seq) == per_seq - 1

        @pl.when(t == 0)
        def _():
            dbb_ref[...] = jnp.zeros_like(dbb_ref)
            dcb_ref[...] = jnp.zeros_like(dcb_ref)
            dlam_ref[...] = jnp.zeros_like(dlam_ref)
            dd_ref[...] = jnp.zeros_like(dd_ref)

        @pl.when(last)
        def _():
            carry_b[...] = jnp.zeros_like(carry_b)

        carry_sc[...] = cin_ref[0]
        _to_segments((u0, u1, u2, u3), u_sc, seg, first)
        u = u_sc[...]
        ub = u.astype(BF16)
        for blk in range(S5_BLK):
            s_sc[8:8 + ts, blk * 1024:(blk + 1) * 1024] = _dot(ub[:, blk * 128:(blk + 1) * 128], b_ref[blk], 1, 0)
        _seg_scan(s_sc, 8, seg, lam_ref, tf_ref, carry_sc, False, cseg_ref=s_sc)

        _to_segments((g0, g1, g2, g3), dy_sc, seg)
        _to_segments((y0, y1, y2, y3), y_sc, seg)
        dy = dy_sc[...] * _gelu_grad(y_sc[...])
        dy_sc[...] = dy
        dyb = dy.astype(BF16)
        dd_ref[0:1, :] += jnp.sum(dy * u, axis=0, keepdims=True)
        for blk in range(S5_BLK):
            a_sc[:, blk * 1024:(blk + 1) * 1024] = _dot(dyb[:, blk * 128:(blk + 1) * 128], ct_ref[blk], 1, 0)
            sb = s_sc[8:8 + ts, blk * 1024:(blk + 1) * 1024].astype(BF16)
            dcb_ref[blk] += _dot(sb, dyb[:, blk * 128:(blk + 1) * 128], 0, 0)

        def lam_grad(i, cre, cim, tc, a_r, a_i):
            r0 = pl.multiple_of(i * 8, 8)
            pr = s_sc[pl.ds(r0, 8), cre:cre + 128]
            pi = s_sc[pl.ds(r0, 8), cim:cim + 128]
            dlam_ref[0, :, tc:tc + 128] += a_r * pr + a_i * pi
            dlam_ref[1, :, tc:tc + 128] += a_i * pr - a_r * pi

        _seg_scan(a_sc, 0, seg, lam_ref, tb_ref, carry_b, True, extra=lam_grad)

        for blk in range(S5_BLK):
            ab = a_sc[:, blk * 1024:(blk + 1) * 1024].astype(BF16)
            w_sc[blk] = _dot(ab, bt_ref[blk], 1, 0) + d_ref[:, blk * 128:(blk + 1) * 128] * dy_sc[:, blk * 128:(blk + 1) * 128]
            dbb_ref[blk] += _dot(u_sc[:, blk * 128:(blk + 1) * 128].astype(BF16), ab, 0, 0)
        _from_segments(w_sc, y_sc, seg, zero_head=first)
        du_ref[:, 0:S5_WIDTH] = y_sc[...].astype(BF16)
        du_ref[:, S5_WIDTH:S5_WIDTH + 128] = dif_ref[...]

    const3 = lambda t: (0, 0, 0)
    rev = lambda t: (nblk - 1 - t, 0)

    def lanes(c0):
        return [pl.BlockSpec((ts, 128), lambda t, cc=c0 + c: (nblk - 1 - t, cc)) for c in range(4)]

    return pl.pallas_call(
        body, name="s5_bwd", grid=(nblk,),
        in_specs=lanes(4 * ucol) + lanes(0) + lanes(0) + [
                  pl.BlockSpec((1, 8, S5_COLS), lambda t: (nblk - 1 - t, 0, 0)),
                  pl.BlockSpec((S5_BLK, 128, 1024), const3),
                  pl.BlockSpec((S5_BLK, 1024, 128), const3),
                  pl.BlockSpec((S5_BLK, 1024, 128), const3),
                  pl.BlockSpec((S5_BLK, 128, 1024), const3),
                  pl.BlockSpec((2, 8, 2048), const3),
                  pl.BlockSpec((8, 8, 2048), const3),
                  pl.BlockSpec((8, 8, 2048), const3),
                  pl.BlockSpec((1, S5_WIDTH), lambda t: (0, 0)),
                  pl.BlockSpec((ts, 128), rev),
                  pl.BlockSpec(memory_space=pl.ANY)],
        out_specs=(pl.BlockSpec((ts, S5_WIDTH + 128), lambda t: (nblk - 1 - t, P_U // (S5_WIDTH + 128))),
                   pl.BlockSpec((S5_BLK, 128, 1024), const3),
                   pl.BlockSpec((S5_BLK, 1024, 128), const3),
                   pl.BlockSpec((2, 8, 2048), const3),
                   pl.BlockSpec((8, S5_WIDTH), lambda t: (0, 0))),
        input_output_aliases={22: 0},
        out_shape=(jax.ShapeDtypeStruct((T, PW), BF16),
                   jax.ShapeDtypeStruct((S5_BLK, 128, 1024), F32),
                   jax.ShapeDtypeStruct((S5_BLK, 1024, 128), F32),
                   jax.ShapeDtypeStruct((2, 8, 2048), F32),
                   jax.ShapeDtypeStruct((8, S5_WIDTH), F32)),
        scratch_shapes=[pltpu.VMEM((ts + 8, S5_COLS), F32), pltpu.VMEM((ts, S5_COLS), F32),
                        pltpu.VMEM((ts, S5_WIDTH), F32), pltpu.VMEM((ts, S5_WIDTH), F32),
                        pltpu.VMEM((ts, S5_WIDTH), F32), pltpu.VMEM((S5_BLK, ts, 128), F32),
                        pltpu.VMEM((8, S5_COLS), F32), pltpu.VMEM((8, S5_COLS), F32)],
        compiler_params=_cparams(("arbitrary",)),
    )(p, p, p, p, y_pre, y_pre, y_pre, y_pre, dgy, dgy, dgy, dgy, cin,
      bblk.astype(BF16), jnp.swapaxes(bblk, 1, 2).astype(BF16),
      cblk.astype(BF16), jnp.swapaxes(cblk, 1, 2).astype(BF16), lam8, tab_f, tab_b, dskip, dif, dp)


def _row_in_seq(i, tr, lp):
    rowid = lax.broadcasted_iota(jnp.int32, (tr, 1), 0)
    return (i * tr + rowid) % lp


def _conv_fwd(p, w, b, lp):
    T = p.shape[0]
    tr = _pick(T, (320, 256, 128, 64))
    c = P_QK // 1024

    def body(x_ref, xp_ref, w_ref, b_ref, o_ref):
        i = pl.program_id(0)
        pos = _row_in_seq(i, tr, lp)
        x = jnp.where(pos < PAD_ROWS, 0.0, x_ref[...])
        pos_p = (i * tr - 8 + lax.broadcasted_iota(jnp.int32, (8, 1), 0)) % lp
        xp = jnp.where((pos_p < PAD_ROWS) | (i == 0), 0.0, xp_ref[...])
        xx = jnp.concatenate([xp, x], axis=0)
        acc = b_ref[...] + w_ref[3:4, :] * x
        for s in (1, 2, 3):
            acc = acc + w_ref[3 - s:4 - s, :] * pltpu.roll(xx, s, 0)[8:8 + tr]
        o_ref[...] = acc * _sigmoid(acc)

    return pl.pallas_call(
        body, name="conv_fwd", grid=(T // tr,),
        in_specs=[pl.BlockSpec((tr, 1024), lambda i: (i, c)),
                  pl.BlockSpec((8, 1024), lambda i: (jnp.maximum(i * (tr // 8) - 1, 0), c)),
                  pl.BlockSpec((4, 1024), lambda i: (0, 0)),
                  pl.BlockSpec((1, 1024), lambda i: (0, 0))],
        out_specs=pl.BlockSpec((tr, 1024), lambda i: (i, 0)),
        out_shape=jax.ShapeDtypeStruct((T, 1024), F32),
        compiler_params=_cparams(("arbitrary",)),
    )(p, p, w, b)


def _conv_bwd(p, dqk, w, b, dp, lp):
    T = p.shape[0]
    tr = _pick(T, (320, 256, 128, 64))
    c = P_QK // 1024
    nb = T // tr

    def body(x_ref, xp_ref, xn_ref, g_ref, gn_ref, w_ref, b_ref, dpin_ref, dx_ref, dw_ref, db_ref):
        i = pl.program_id(0)

        @pl.when(i == 0)
        def _():
            dw_ref[...] = jnp.zeros_like(dw_ref)
            db_ref[...] = jnp.zeros_like(db_ref)

        def seqpos(off, n):
            return (i * tr + off + lax.broadcasted_iota(jnp.int32, (n, 1), 0)) % lp

        x = jnp.where(seqpos(0, tr) < PAD_ROWS, 0.0, x_ref[...])
        xp = jnp.where((seqpos(-8, 8) < PAD_ROWS) | (i == 0), 0.0, xp_ref[...])
        xn = jnp.where((seqpos(tr, 8) < PAD_ROWS) | (i == nb - 1), 0.0, xn_ref[...])
        xx = jnp.concatenate([xp, x, xn], axis=0)
        gg = jnp.concatenate([g_ref[...], gn_ref[...]], axis=0)
        n2 = tr + 8
        acc = b_ref[...] + w_ref[3:4, :] * xx[8:8 + n2]
        for s in (1, 2, 3):
            acc = acc + w_ref[3 - s:4 - s, :] * pltpu.roll(xx, s, 0)[8:8 + n2]
        sg = _sigmoid(acc)
        dpre = gg * (sg * (1.0 + acc * (1.0 - sg)))
        valid = jnp.concatenate(
            [seqpos(0, tr) >= PAD_ROWS, (seqpos(tr, 8) >= PAD_ROWS) & (i < nb - 1)], axis=0)
        dpre = jnp.where(valid, dpre, 0.0)
        d0 = dpre[0:tr]
        dx = w_ref[3:4, :] * d0
        for s in (1, 2, 3):
            dx = dx + w_ref[3 - s:4 - s, :] * pltpu.roll(dpre, n2 - s, 0)[0:tr]
        dx_ref[...] = jnp.where(seqpos(0, tr) < PAD_ROWS, 0.0, dx).astype(BF16)
        db_ref[0:1, :] += jnp.sum(d0, axis=0, keepdims=True)
        dw_ref[3:4, :] += jnp.sum(d0 * x, axis=0, keepdims=True)
        for s in (1, 2, 3):
            xs = pltpu.roll(xx, s, 0)[8:8 + tr]
            dw_ref[3 - s:4 - s, :] += jnp.sum(d0 * xs, axis=0, keepdims=True)

    t8 = tr // 8
    return pl.pallas_call(
        body, name="conv_bwd", grid=(nb,),
        in_specs=[pl.BlockSpec((tr, 1024), lambda i: (i, c)),
                  pl.BlockSpec((8, 1024), lambda i: (jnp.maximum(i * t8 - 1, 0), c)),
                  pl.BlockSpec((8, 1024), lambda i: (jnp.minimum((i + 1) * t8, nb * t8 - 1), c)),
                  pl.BlockSpec((tr, 1024), lambda i: (i, 0)),
                  pl.BlockSpec((8, 1024), lambda i: (jnp.minimum((i + 1) * t8, nb * t8 - 1), 0)),
                  pl.BlockSpec((4, 1024), lambda i: (0, 0)),
                  pl.BlockSpec((1, 1024), lambda i: (0, 0)),
                  pl.BlockSpec(memory_space=pl.ANY)],
        out_specs=(pl.BlockSpec((tr, 1024), lambda i: (i, c)),
                   pl.BlockSpec((8, 1024), lambda i: (0, 0)),
                   pl.BlockSpec((8, 1024), lambda i: (0, 0))),
        out_shape=(jax.ShapeDtypeStruct((T, PW), BF16),
                   jax.ShapeDtypeStruct((8, 1024), F32),
                   jax.ShapeDtypeStruct((8, 1024), F32)),
        input_output_aliases={7: 0},
        compiler_params=_cparams(("arbitrary",)),
    )(p, p, p, dqk, dqk, w, b, dp)


def _split3(x):
    hi = x.astype(BF16)
    r1 = x - hi.astype(F32)
    mid = r1.astype(BF16)
    lo = (r1 - mid.astype(F32)).astype(BF16)
    return hi, mid, lo


def _tri_sum(x, upper):
    r = lax.broadcasted_iota(jnp.int32, (CHUNK, CHUNK), 0)
    c = lax.broadcasted_iota(jnp.int32, (CHUNK, CHUNK), 1)
    tri = jnp.where((r <= c) if upper else (r >= c), 1.0, 0.0).astype(BF16)
    hi, mid, lo = _split3(x)
    return _dot(tri, hi, 1, 0) + _dot(tri, mid, 1, 0) + _dot(tri, lo, 1, 0)


def _lane_col(x, lane):
    l = lax.broadcasted_iota(jnp.int32, x.shape, 1)
    return jnp.sum(jnp.where(l == lane, x, 0.0), axis=1, keepdims=True)


def _to_row(col):
    r = lax.broadcasted_iota(jnp.int32, (CHUNK, CHUNK), 0)
    c = lax.broadcasted_iota(jnp.int32, (CHUNK, CHUNK), 1)
    return jnp.sum(jnp.where(r == c, col, 0.0), axis=0, keepdims=True)


def _to_col(row):
    r = lax.broadcasted_iota(jnp.int32, (CHUNK, CHUNK), 0)
    c = lax.broadcasted_iota(jnp.int32, (CHUNK, CHUNK), 1)
    return jnp.sum(jnp.where(r == c, row, 0.0), axis=1, keepdims=True)


def _log_sigmoid(x):
    return jnp.minimum(x, 0.0) - jnp.log(1.0 + jnp.exp(-jnp.abs(x)))


def _mlstm_gates(ifv, padmask):
    lf = jnp.where(padmask, 0.0, _log_sigmoid(ifv))
    b_all = _tri_sum(lf, False)
    li = jnp.where(padmask, -jnp.inf, ifv)
    return li, b_all


def _mlstm_head_fwd(q, k, v, li_col, b_col, c_st, n_st, m_st):
    r = lax.broadcasted_iota(jnp.int32, (CHUNK, CHUNK), 0)
    c = lax.broadcasted_iota(jnp.int32, (CHUNK, CHUNK), 1)
    rowid = lax.broadcasted_iota(jnp.int32, (CHUNK, 1), 0)
    b_row = _to_row(b_col)
    li_row = _to_row(li_col)
    dmat = jnp.where(r >= c, b_col - b_row + li_row, -jnp.inf)
    m_inter = b_col + m_st
    m_row = jnp.maximum(m_inter, jnp.max(dmat, axis=1, keepdims=True))
    w_intra = jnp.exp(dmat - m_row)
    w_inter = jnp.exp(m_inter - m_row)
    qb, kb, vb = q.astype(BF16), k.astype(BF16), v.astype(BF16)
    qk = _dot(qb, kb, 1, 1)
    s = qk * w_intra
    cb = c_st.astype(BF16)
    qc = _dot(qb, cb, 1, 0)
    qn = jnp.sum(q * n_st, axis=1, keepdims=True)
    num = _dot(s.astype(BF16), vb, 1, 0) + w_inter * qc
    den = jnp.sum(s, axis=1, keepdims=True) + w_inter * qn
    floor = jnp.exp(-m_row)
    rinv = 1.0 / jnp.maximum(jnp.abs(den), floor)
    h = num * rinv
    b_last = jnp.sum(jnp.where(rowid == CHUNK - 1, b_col, 0.0), axis=0, keepdims=True)
    g_col = b_last - b_col + li_col
    m_new = jnp.maximum(b_last + m_st, jnp.max(g_col, axis=0, keepdims=True))
    w_k = jnp.exp(g_col - m_new)
    decay = jnp.exp(b_last + m_st - m_new)
    kw = w_k * k
    c_new = decay * c_st + _dot(kw.astype(BF16), vb, 0, 0)
    n_new = decay * n_st + jnp.sum(kw, axis=0, keepdims=True)
    return dict(h=h, c_new=c_new, n_new=n_new, m_new=m_new, w_intra=w_intra, w_inter=w_inter, s=s,
                qc=qc, qn=qn, den=den, floor=floor, rinv=rinv, w_k=w_k, decay=decay, kw=kw,
                qb=qb, kb=kb, vb=vb, cb=cb)


def _mlstm_fwd(p, qk, lp):
    T = p.shape[0]
    nch = lp // CHUNK
    B = T // lp
    scale = M_DK ** -0.5

    def body(q_ref, k_ref, v_ref, if_ref, h_ref, cst_ref, nm_ref, c_sc, nm_sc):
        ci = pl.program_id(0)

        @pl.when(ci == 0)
        def _():
            c_sc[...] = jnp.zeros_like(c_sc)
            nm_sc[...] = jnp.zeros_like(nm_sc)

        rowid = lax.broadcasted_iota(jnp.int32, (CHUNK, 1), 0)
        padmask = (ci == 0) & (rowid < PAD_ROWS)
        for bb in range(B):
            cst_ref[bb, 0] = c_sc[bb]
            nm_ref[bb, 0] = nm_sc[bb]
            li_all, b_all = _mlstm_gates(if_ref[bb], padmask)
            for hd in range(M_HEADS):
                q = q_ref[bb, :, hd * M_DK:(hd + 1) * M_DK]
                k = k_ref[bb, :, hd * M_DK:(hd + 1) * M_DK] * scale
                v = v_ref[bb, :, hd * M_DV:(hd + 1) * M_DV]
                o = _mlstm_head_fwd(q, k, v, _lane_col(li_all, hd), _lane_col(b_all, M_HEADS + hd),
                                    c_sc[bb, hd * M_DK:(hd + 1) * M_DK, :], nm_sc[bb, hd:hd + 1, :],
                                    nm_sc[bb, M_HEADS + hd:M_HEADS + hd + 1, 0:1])
                h_ref[bb, :, hd * M_DV:(hd + 1) * M_DV] = o["h"]
                c_sc[bb, hd * M_DK:(hd + 1) * M_DK, :] = o["c_new"]
                nm_sc[bb, hd:hd + 1, :] = o["n_new"]
                nm_sc[bb, M_HEADS + hd:M_HEADS + hd + 1, :] = jnp.broadcast_to(o["m_new"], (1, 128))

    qk3, p3 = qk.reshape(B, lp, 1024), p.reshape(B, lp, PW)
    h, cst, nm = pl.pallas_call(
        body, name="mlstm_fwd", grid=(nch,),
        in_specs=[pl.BlockSpec((B, CHUNK, 512), lambda ci: (0, ci, 0)),
                  pl.BlockSpec((B, CHUNK, 512), lambda ci: (0, ci, 1)),
                  pl.BlockSpec((B, CHUNK, 1024), lambda ci: (0, ci, P_V // 1024)),
                  pl.BlockSpec((B, CHUNK, 128), lambda ci: (0, ci, P_IF // 128))],
        out_specs=(pl.BlockSpec((B, CHUNK, 1024), lambda ci: (0, ci, 0)),
                   pl.BlockSpec((B, 1, M_HEADS * M_DK, M_DV), lambda ci: (0, ci, 0, 0)),
                   pl.BlockSpec((B, 1, 8, 128), lambda ci: (0, ci, 0, 0))),
        out_shape=(jax.ShapeDtypeStruct((B, lp, 1024), F32),
                   jax.ShapeDtypeStruct((B, nch, M_HEADS * M_DK, M_DV), F32),
                   jax.ShapeDtypeStruct((B, nch, 8, 128), F32)),
        scratch_shapes=[pltpu.VMEM((B, M_HEADS * M_DK, M_DV), F32), pltpu.VMEM((B, 8, 128), F32)],
        compiler_params=_cparams(("arbitrary",)),
    )(qk3, qk3, p3, p3)
    return h.reshape(T, 1024), cst, nm


def _mlstm_bwd(p, qk, cst, nm, dh, dp, lp):
    T = p.shape[0]
    nch = lp // CHUNK
    B = T // lp
    scale = M_DK ** -0.5

    def body(q_ref, k_ref, v_ref, if_ref, cst_ref, nm_ref, dh_ref, dpin_ref, dqk_ref, dv_ref, dif_ref, dc_sc, dn_sc):
        t = pl.program_id(0)
        ci = nch - 1 - t

        @pl.when(t == 0)
        def _():
            dc_sc[...] = jnp.zeros_like(dc_sc)
            dn_sc[...] = jnp.zeros_like(dn_sc)

        for bb in range(B):
            one_sequence(bb, ci, q_ref, k_ref, v_ref, if_ref, cst_ref, nm_ref, dh_ref, dqk_ref, dv_ref, dif_ref,
                         dc_sc, dn_sc)

    def one_sequence(bb, ci, q_ref, k_ref, v_ref, if_ref, cst_ref, nm_ref, dh_ref, dqk_ref, dv_ref, dif_ref,
                     dc_sc, dn_sc):
        rowid = lax.broadcasted_iota(jnp.int32, (CHUNK, 1), 0)
        lane = lax.broadcasted_iota(jnp.int32, (CHUNK, 128), 1)
        padmask = (ci == 0) & (rowid < PAD_ROWS)
        ifv = if_ref[bb]
        li_all, b_all = _mlstm_gates(ifv, padmask)
        db_all = jnp.zeros((CHUNK, 128), F32)
        dli_all = jnp.zeros((CHUNK, 128), F32)
        for hd in range(M_HEADS):
            q = q_ref[bb, :, hd * M_DK:(hd + 1) * M_DK]
            k = k_ref[bb, :, hd * M_DK:(hd + 1) * M_DK] * scale
            v = v_ref[bb, :, hd * M_DV:(hd + 1) * M_DV]
            c_st = cst_ref[bb, 0, hd * M_DK:(hd + 1) * M_DK, :]
            n_st = nm_ref[bb, 0, hd:hd + 1, :]
            m_st = nm_ref[bb, 0, M_HEADS + hd:M_HEADS + hd + 1, 0:1]
            o = _mlstm_head_fwd(q, k, v, _lane_col(li_all, hd), _lane_col(b_all, M_HEADS + hd), c_st, n_st, m_st)
            dc_new = dc_sc[bb, hd * M_DK:(hd + 1) * M_DK, :]
            dn_new = dn_sc[bb, hd:hd + 1, :]
            dcb = dc_new.astype(BF16)
            dhh = dh_ref[bb, :, hd * M_DV:(hd + 1) * M_DV]
            dnum = dhh * o["rinv"]
            dhh_h = jnp.sum(dhh * o["h"], axis=1, keepdims=True)
            sgn = jnp.where(o["den"] >= 0.0, 1.0, -1.0)
            dden = jnp.where(jnp.abs(o["den"]) > o["floor"], -dhh_h * o["rinv"] * sgn, 0.0)
            dnb = dnum.astype(BF16)
            ds = _dot(dnb, o["vb"], 1, 1) + dden
            sb = o["s"].astype(BF16)
            kwb = o["kw"].astype(BF16)
            dv = _dot(sb, dnb, 0, 0) + _dot(kwb, dcb, 1, 0)
            dqk_m = (ds * o["w_intra"]).astype(BF16)
            wdn = o["w_inter"] * dnum
            wdd = o["w_inter"] * dden
            dq = _dot(dqk_m, o["kb"], 1, 0) + _dot(wdn.astype(BF16), o["cb"], 1, 1) + wdd * n_st
            vdc = _dot(o["vb"], dcb, 1, 1)
            dk = _dot(dqk_m, o["qb"], 0, 0) + o["w_k"] * (vdc + dn_new)
            dd = ds * o["s"]
            dd_col = _to_col(jnp.sum(dd, axis=0, keepdims=True))
            dmi = o["w_inter"] * (jnp.sum(dnum * o["qc"], axis=1, keepdims=True) + dden * o["qn"])
            dg = o["w_k"] * (jnp.sum(k * vdc, axis=1, keepdims=True) + jnp.sum(k * dn_new, axis=1, keepdims=True))
            d_blast = (o["decay"] * (jnp.sum(jnp.sum(dc_new * c_st, axis=1, keepdims=True), axis=0, keepdims=True)
                                     + jnp.sum(dn_new * n_st, axis=1, keepdims=True))
                       + jnp.sum(dg, axis=0, keepdims=True))
            db_col = jnp.sum(dd, axis=1, keepdims=True) - dd_col + dmi - dg
            db_col = db_col + jnp.where(rowid == CHUNK - 1, d_blast, 0.0)
            dli_col = dd_col + dg
            db_all = db_all + jnp.where(lane == M_HEADS + hd, db_col, 0.0)
            dli_all = dli_all + jnp.where(lane == hd, dli_col, 0.0)
            dc_sc[bb, hd * M_DK:(hd + 1) * M_DK, :] = o["decay"] * dc_new + _dot(o["qb"], wdn.astype(BF16), 0, 0)
            dn_sc[bb, hd:hd + 1, :] = o["decay"] * dn_new + jnp.sum(q * wdd, axis=0, keepdims=True)
            dqk_ref[bb, :, hd * M_DK:(hd + 1) * M_DK] = dq
            dqk_ref[bb, :, 512 + hd * M_DK:512 + (hd + 1) * M_DK] = dk * scale
            dv_ref[bb, :, hd * M_DV:(hd + 1) * M_DV] = dv.astype(BF16)
        dlf = _tri_sum(db_all, True)
        dif = dli_all + dlf * _sigmoid(-ifv)
        dif_ref[bb] = jnp.where(padmask | (lane >= 2 * M_HEADS), 0.0, dif).astype(BF16)

    def rev(cb):
        return lambda t: (0, nch - 1 - t, cb)

    rev4 = lambda t: (0, nch - 1 - t, 0, 0)
    qk3, p3, dh3 = qk.reshape(B, lp, 1024), p.reshape(B, lp, PW), dh.reshape(B, lp, 1024)
    dqk, dv, dif = pl.pallas_call(
        body, name="mlstm_bwd", grid=(nch,),
        in_specs=[pl.BlockSpec((B, CHUNK, 512), rev(0)),
                  pl.BlockSpec((B, CHUNK, 512), rev(1)),
                  pl.BlockSpec((B, CHUNK, 1024), rev(P_V // 1024)),
                  pl.BlockSpec((B, CHUNK, 128), rev(P_IF // 128)),
                  pl.BlockSpec((B, 1, M_HEADS * M_DK, M_DV), rev4),
                  pl.BlockSpec((B, 1, 8, 128), rev4),
                  pl.BlockSpec((B, CHUNK, 1024), rev(0)),
                  pl.BlockSpec(memory_space=pl.ANY)],
        out_specs=(pl.BlockSpec((B, CHUNK, 1024), rev(0)),
                   pl.BlockSpec((B, CHUNK, 1024), rev(P_V // 1024)),
                   pl.BlockSpec((B, CHUNK, 128), rev(0))),
        out_shape=(jax.ShapeDtypeStruct((B, lp, 1024), F32),
                   jax.ShapeDtypeStruct((B, lp, PW), BF16),
                   jax.ShapeDtypeStruct((B, lp, 128), BF16)),
        input_output_aliases={7: 1},
        scratch_shapes=[pltpu.VMEM((B, M_HEADS * M_DK, M_DV), F32), pltpu.VMEM((B, 8, 128), F32)],
        compiler_params=_cparams(("arbitrary",)),
    )(qk3, qk3, p3, p3, cst, nm, dh3, dp.reshape(B, lp, PW))
    return dqk.reshape(T, 1024), dv.reshape(T, PW), dif.reshape(T, 128)


def _headnorm_fwd(hm, p, g):
    T = hm.shape[0]
    tr = _pick(T, (320, 256, 128, 64))

    def body(h_ref, o_ref, g_ref, a_ref):
        for hd in range(M_HEADS):
            sl = slice(hd * M_DV, (hd + 1) * M_DV)
            hn = _ln_rows(h_ref[:, sl], g_ref[:, sl], 0.0)
            a_ref[:, sl] = (_sigmoid(o_ref[:, sl]) * hn).astype(BF16)

    return pl.pallas_call(
        body, name="headnorm_fwd", grid=(T // tr,),
        in_specs=[pl.BlockSpec((tr, 1024), lambda i: (i, 0)),
                  pl.BlockSpec((tr, 1024), lambda i: (i, P_O // 1024)),
                  pl.BlockSpec((1, 1024), lambda i: (0, 0))],
        out_specs=pl.BlockSpec((tr, 1024), lambda i: (i, 0)),
        out_shape=jax.ShapeDtypeStruct((T, 1024), BF16),
        compiler_params=_cparams(("arbitrary",)),
    )(hm, p, g)


def _headnorm_bwd(da, hm, p, g, dp):
    T = hm.shape[0]
    tr = _pick(T, (320, 256, 128, 64))

    def body(da_ref, h_ref, o_ref, g_ref, dpin_ref, dh_ref, do_ref, dg_ref):
        @pl.when(pl.program_id(0) == 0)
        def _():
            dg_ref[...] = jnp.zeros_like(dg_ref)

        for hd in range(M_HEADS):
            sl = slice(hd * M_DV, (hd + 1) * M_DV)
            gg = g_ref[:, sl]
            so = _sigmoid(o_ref[:, sl])
            da = da_ref[:, sl]
            dhn = da * so
            dv, xhat = _ln_bwd_rows(dhn, h_ref[:, sl], gg)
            dh_ref[:, sl] = dv
            do_ref[:, sl] = (da * (xhat * gg) * so * (1.0 - so)).astype(BF16)
            dg_ref[0:1, sl] += jnp.sum(dhn * xhat, axis=0, keepdims=True)

    row = pl.BlockSpec((tr, 1024), lambda i: (i, 0))
    return pl.pallas_call(
        body, name="headnorm_bwd", grid=(T // tr,),
        in_specs=[row, row, pl.BlockSpec((tr, 1024), lambda i: (i, P_O // 1024)),
                  pl.BlockSpec((1, 1024), lambda i: (0, 0)), pl.BlockSpec(memory_space=pl.ANY)],
        out_specs=(row, pl.BlockSpec((tr, 1024), lambda i: (i, P_O // 1024)),
                   pl.BlockSpec((8, 1024), lambda i: (0, 0))),
        out_shape=(jax.ShapeDtypeStruct((T, 1024), F32), jax.ShapeDtypeStruct((T, PW), BF16),
                   jax.ShapeDtypeStruct((8, 1024), F32)),
        input_output_aliases={4: 1},
        compiler_params=_cparams(("arbitrary",)),
    )(da, hm, p, g, dp)


def _mix_fwd(z, ym, p):
    T = ym.shape[0]
    tr = _pick(T, (320, 256, 128, 64))

    def body(z1_ref, z2_ref, ym_ref, gs_ref, gm_ref, o_ref):
        ys = z1_ref[...] * _sigmoid(z2_ref[...])
        o_ref[...] = (_sigmoid(gs_ref[...]) * ys + _sigmoid(gm_ref[...]) * ym_ref[...]).astype(BF16)

    def col(cb):
        return pl.BlockSpec((tr, 1024), lambda i: (i, cb))

    return pl.pallas_call(
        body, name="mix_fwd", grid=(T // tr,),
        in_specs=[col(0), col(1), col(0), col(P_GS // 1024), col(P_GM // 1024)],
        out_specs=col(0),
        out_shape=jax.ShapeDtypeStruct((T, 1024), BF16),
        compiler_params=_cparams(("arbitrary",)),
    )(z, z, ym, p, p)


def _mix_bwd(dmix, z, ym, p):
    T = ym.shape[0]
    tr = _pick(T, (320, 256, 128, 64))

    def body(d_ref, z1_ref, z2_ref, ym_ref, gs_ref, gm_ref, dz_ref, dym_ref, dp_ref):
        d = d_ref[...]
        z1 = z1_ref[...]
        s2 = _sigmoid(z2_ref[...])
        ss = _sigmoid(gs_ref[...])
        sm = _sigmoid(gm_ref[...])
        ys = z1 * s2
        dys = d * ss
        dp_ref[:, 0:1024] = (d * ys * ss * (1.0 - ss)).astype(BF16)
        dp_ref[:, 1024:2048] = (d * ym_ref[...] * sm * (1.0 - sm)).astype(BF16)
        dym_ref[...] = (d * sm).astype(BF16)
        dz_ref[:, 0:1024] = (dys * s2).astype(BF16)
        dz_ref[:, 1024:2048] = (dys * z1 * s2 * (1.0 - s2)).astype(BF16)

    def col(cb):
        return pl.BlockSpec((tr, 1024), lambda i: (i, cb))

    o = jax.ShapeDtypeStruct((T, 1024), BF16)
    return pl.pallas_call(
        body, name="mix_bwd", grid=(T // tr,),
        in_specs=[col(0), col(0), col(1), col(0), col(P_GS // 1024), col(P_GM // 1024)],
        out_specs=(pl.BlockSpec((tr, 2048), lambda i: (i, 0)), col(0),
                   pl.BlockSpec((tr, 2048), lambda i: (i, P_GS // 2048))),
        out_shape=(jax.ShapeDtypeStruct((T, 2048), BF16), o, jax.ShapeDtypeStruct((T, PW), BF16)),
        compiler_params=_cparams(("arbitrary",)),
    )(dmix, z, z, ym, p, p)


def _adamw_math(w, g, m, v):
    m2 = ADAM_B1 * m + (1.0 - ADAM_B1) * g
    v2 = ADAM_B2 * v + (1.0 - ADAM_B2) * jnp.square(g)
    m_hat = m2 / (1.0 - ADAM_B1 ** ADAM_STEP)
    v_hat = v2 / (1.0 - ADAM_B2 ** ADAM_STEP)
    delta = -ADAM_LR * (m_hat / (jnp.sqrt(v_hat) + ADAM_EPS) + ADAM_WD * w)
    return delta, m2, v2


def _adamw_big(recv, w, m, v, name):
    R, C = w.shape
    tr, tc = _shard_tile(R, C)
    ns = recv.shape[0]

    def body(r_ref, w_ref, m_ref, v_ref, g_ref, d_ref, m2_ref, v2_ref):
        g = r_ref[0].astype(F32)
        for s in range(1, ns):
            g = g + r_ref[s].astype(F32)
        d, m2, v2 = _adamw_math(w_ref[...], g, m_ref[...], v_ref[...])
        g_ref[...] = g
        d_ref[...] = d
        m2_ref[...] = m2
        v2_ref[...] = v2

    blk = pl.BlockSpec((tr, tc), lambda i, j: (i, j))
    o = jax.ShapeDtypeStruct((R, C), F32)
    return pl.pallas_call(
        body, name=name, grid=(R // tr, C // tc),
        in_specs=[pl.BlockSpec((ns, tr, tc), lambda i, j: (0, i, j)), blk, blk, blk],
        out_specs=(blk,) * 4, out_shape=(o,) * 4,
        compiler_params=_cparams(("arbitrary", "arbitrary")),
    )(recv, w, m, v)


def _adamw_small(g, w, m, v):
    R = g.shape[0]

    def body(g_ref, w_ref, m_ref, v_ref, d_ref, m2_ref, v2_ref):
        d, m2, v2 = _adamw_math(w_ref[...], g_ref[...], m_ref[...], v_ref[...])
        d_ref[...] = d
        m2_ref[...] = m2
        v2_ref[...] = v2

    full = pl.BlockSpec((R, 128), lambda i: (0, 0))
    o = jax.ShapeDtypeStruct((R, 128), F32)
    return pl.pallas_call(
        body, name="adamw_small", grid=(1,),
        in_specs=[full] * 4, out_specs=(full,) * 3, out_shape=(o,) * 3,
        compiler_params=_cparams(("arbitrary",)),
    )(g, w, m, v)


def _pack(arrs):
    parts = [a.reshape(-1).astype(F32) for a in arrs]
    total = sum(f.shape[0] for f in parts)
    tail = -(-total // 1024) * 1024 - total
    return jnp.concatenate(parts + [jnp.zeros((tail,), F32)]).reshape(-1, 128)


def _unpack(pack, shapes):
    flat = pack.reshape(-1)
    out, off = [], 0
    for shp in shapes:
        n = math.prod(shp)
        out.append(flat[off:off + n].reshape(shp))
        off += n
    return out


def _cols_from_shards(g):
    return jnp.transpose(g, (1, 0, 2)).reshape(g.shape[1], -1)


def _cols_to_shards(w):
    R = w.shape[0]
    return jnp.transpose(w.reshape(R, N_DEV, -1), (1, 0, 2))


def _nat_to_aligned(w, axis):
    def sl(start, size):
        return lax.slice_in_dim(w, start, start + size, axis=axis)

    pads = [(0, 0)] * w.ndim
    pads[axis] = (0, PW - P_IF - 8)
    return jnp.concatenate([sl(N_V, 1024), sl(N_O, 1024), sl(N_GS, 1024), sl(N_GM, 1024), sl(N_Q, 1024),
                            sl(N_U, 512), jnp.pad(sl(N_I, 8), pads)], axis=axis)


def _aligned_to_nat(w, axis):
    def sl(start, size):
        return lax.slice_in_dim(w, start, start + size, axis=axis)

    return jnp.concatenate([sl(P_U, 512), sl(P_QK, 1024), sl(P_V, 1024), sl(P_O, 1024), sl(P_IF, 8),
                            sl(P_GS, 1024), sl(P_GM, 1024)], axis=axis)


def kernel(x, meta_tokens, ln0_g, ln0_b, w_in, b_in, qk_conv_w, qk_conv_b, s5_lambda_re, s5_lambda_im, s5_log_dt, s5_b_re, s5_b_im, s5_c_re, s5_c_im, s5_d, s5_w_glu, m_norm_g, m_w_out, w_o, ln1_g, ln1_b, w_up, b_up, w_down, ln2_g, ln2_b, loss_target, m_meta_tokens, m_ln0_g, m_ln0_b, m_w_in, m_b_in, m_qk_conv_w, m_qk_conv_b, m_s5_lambda_re, m_s5_lambda_im, m_s5_log_dt, m_s5_b_re, m_s5_b_im, m_s5_c_re, m_s5_c_im, m_s5_d, m_s5_w_glu, m_m_norm_g, m_m_w_out, m_w_o, m_ln1_g, m_ln1_b, m_w_up, m_b_up, m_w_down, m_ln2_g, m_ln2_b, v_meta_tokens, v_ln0_g, v_ln0_b, v_w_in, v_b_in, v_qk_conv_w, v_qk_conv_b, v_s5_lambda_re, v_s5_lambda_im, v_s5_log_dt, v_s5_b_re, v_s5_b_im, v_s5_c_re, v_s5_c_im, v_s5_d, v_s5_w_glu, v_m_norm_g, v_m_w_out, v_w_o, v_ln1_g, v_ln1_b, v_w_up, v_b_up, v_w_down, v_ln2_g, v_ln2_b):
    B, S, D = x.shape
    lp = S + CHUNK
    me = _my_id()

    first = _allgather([w_in[0].T.astype(BF16), meta_tokens, qk_conv_w[0]])
    win_t = _nat_to_aligned(first[0].reshape(IN_NAT, D), 0)
    meta_f = _cols_from_shards(first[1])
    convw_f = _cols_from_shards(first[2])
    later = [a.astype(BF16) for a in (s5_w_glu[0], m_w_out[0], w_o[0], w_up[0], w_down[0])]
    b_in_al = _nat_to_aligned(b_in, 1)
    ln0g, ln0b = ln0_g.reshape(1, D), ln0_b.reshape(1, D)

    s5_args = (s5_lambda_re[0], s5_lambda_im[0], s5_log_dt[0], s5_b_re[0], s5_b_im[0], s5_c_re[0], s5_c_im[0])
    (ar, ai, bblk, cblk), s5_vjp = jax.vjp(_s5_prep, *s5_args)
    seg = _s5_tile(lp) // 8
    mu_r, mu_i = ar, ai
    for _ in range(seg - 1):
        mu_r, mu_i = mu_r * ar - mu_i * ai, mu_r * ai + mu_i * ar
    tab_f, tab_b = _scan_tables(mu_r, mu_i)
    lam8 = jnp.stack([jnp.broadcast_to(ar, (8, 2048)), jnp.broadcast_to(ai, (8, 2048))])

    h0, h0b = _ln0_fwd(x, meta_f, ln0g, ln0b)
    p, *gathered = _mm(h0b, win_t, "nt", "mm_in", bias=b_in_al, comm=(_Gather, later))
    wglu_f = _cols_from_shards(gathered[0])
    wmo_f = gathered[1].reshape(1024, 1024)
    wo_f = gathered[2].reshape(1024, 1024)
    wup_f = _cols_from_shards(gathered[3])
    wdown_f = gathered[4].reshape(D_FF, 1024)
    y_pre, gy, cin = _s5_fwd(p, bblk, cblk, lam8, tab_f, s5_d, lp)
    z = _mm(gy, wglu_f, "nn", "mm_glu")
    qk = _conv_fwd(p, convw_f, qk_conv_b, lp)
    hm, cst, nm = _mlstm_fwd(p, qk, lp)
    a_m = _headnorm_fwd(hm, p, m_norm_g)
    ym = _mm(a_m, wmo_f, "nn", "mm_mout")
    mix = _mix_fwd(z, ym, p)
    r1 = _mm(mix, wo_f, "nn", "mm_o")
    pre1, h1, h1b = _ln_res_fwd(h0, r1, ln1_g, ln1_b, "ln1_fwd")
    act = _mm(h1b, wup_f, "nn", "mm_up", bias=b_up, relu2_out=True, out_dtype=BF16)
    ff = _mm(act, wdown_f, "nn", "mm_down")
    loss_acc, dpre2, dpre2b, dg2, db2 = _ln2_loss(h1, ff, ln2_g, ln2_b, loss_target)

    d_up = _mm(dpre2b, wdown_f, "nt", "mm_d_act", sqrt_gate_of=act, out_dtype=BF16)
    g_wdown = _mm(act, dpre2b, "tn", "mm_g_wdown", out_dtype=BF16)
    g_wup, cs_up = _mm(h1b, d_up, "tn", "mm_g_wup", colsum="b", out_dtype=BF16)
    dh1 = _mm(d_up, wup_f, "nt", "mm_d_h1", add=dpre2, add_scale=ALPHA)
    dpre1, dpre1b, dg1, db1 = _ln_bwd(dh1, pre1, ln1_g, "ln1_bwd")
    g_wo = _mm(mix, dpre1b, "tn", "mm_g_wo", out_dtype=BF16)
    dmix = _mm(dpre1b, wo_f, "nt", "mm_d_mix")
    dz, dym, dp = _mix_bwd(dmix, z, ym, p)
    g_wmo = _mm(a_m, dym, "tn", "mm_g_wmo", out_dtype=BF16)
    da = _mm(dym, wmo_f, "nt", "mm_d_a")
    dhm, dp, dg_norm = _headnorm_bwd(da, hm, p, m_norm_g, dp)
    dqk, dp, dif = _mlstm_bwd(p, qk, cst, nm, dhm, dp, lp)
    dp, dconv_w, dconv_b = _conv_bwd(p, dqk, convw_f, qk_conv_b, dp, lp)
    g_wglu = _mm(gy, dz, "tn", "mm_g_wglu", out_dtype=BF16)
    dgy = _mm(dz, wglu_f, "nt", "mm_d_gy")
    dp, dbblk, dcblk, dlam, dd = _s5_bwd(p, y_pre, dgy, cin, bblk, cblk, lam8, tab_f, tab_b, s5_d, dif, dp, lp)
    early = [g_wdown.reshape(N_DEV, 512, 1024), _cols_to_shards(g_wup)]
    g_win_t, cs_in, *recv_early = _mm(dp, h0b, "tn", "mm_g_win", colsum="a", comm=(_AllToAll, early))
    mid = [g_wo.reshape(N_DEV, 128, 1024), g_wmo.reshape(N_DEV, 128, 1024), _cols_to_shards(g_wglu)]
    dh0, *recv_mid = _mm(dp, win_t, "nn", "mm_d_h0", add=dpre1, add_scale=ALPHA, comm=(_AllToAll, mid))
    grad_x, dmeta, dg0, db0 = _ln0_bwd(dh0, x, meta_f, ln0g)

    dlam2 = jnp.sum(dlam, axis=1)
    s5_grads = s5_vjp((dlam2[0:1], dlam2[1:2], dbblk, dcblk))
    small_local = [
        loss_acc[0:1, 0:1], dg0[0:1], db0[0:1], _aligned_to_nat(cs_in[0:1], 1), dconv_b[0:1],
        s5_grads[0], s5_grads[1], s5_grads[2], s5_grads[3], s5_grads[4], s5_grads[5], s5_grads[6],
        dd[0:1], dg_norm[0:1], dg1[0:1], db1[0:1], cs_up[0:1], dg2[0:1], db2[0:1],
        dmeta, dconv_w[0:4]]
    small_shapes = [(), (D,), (D,), (1, IN_NAT), (1, 1024),
                    (1, 32, 64), (1, 32, 64), (1, 32), (1, 32, 64, 16), (1, 32, 64, 16), (1, 32, 16, 64), (1, 32, 16, 64),
                    (1, 512), (1, 1024), (1, 1024), (1, 1024), (1, D_FF), (1, 1024), (1, 1024),
                    (N_META, D), (4, 1024)]
    red = _unpack(_allreduce_small(_pack(small_local)), small_shapes)
    loss = red[0]
    g_meta = lax.dynamic_slice_in_dim(red[19], me * 128, 128, axis=1)
    g_convw = lax.dynamic_slice_in_dim(red[20], me * 128, 128, axis=1)[None]
    small_g = red[1:19] + [g_meta, g_convw]
    small_w = [ln0_g, ln0_b, b_in, qk_conv_b, s5_lambda_re, s5_lambda_im, s5_log_dt, s5_b_re, s5_b_im,
               s5_c_re, s5_c_im, s5_d, m_norm_g, ln1_g, ln1_b, b_up, ln2_g, ln2_b, meta_tokens, qk_conv_w]
    small_m = [m_ln0_g, m_ln0_b, m_b_in, m_qk_conv_b, m_s5_lambda_re, m_s5_lambda_im, m_s5_log_dt, m_s5_b_re,
               m_s5_b_im, m_s5_c_re, m_s5_c_im, m_s5_d, m_m_norm_g, m_ln1_g, m_ln1_b, m_b_up, m_ln2_g, m_ln2_b,
               m_meta_tokens, m_qk_conv_w]
    small_v = [v_ln0_g, v_ln0_b, v_b_in, v_qk_conv_b, v_s5_lambda_re, v_s5_lambda_im, v_s5_log_dt, v_s5_b_re,
               v_s5_b_im, v_s5_c_re, v_s5_c_im, v_s5_d, v_m_norm_g, v_ln1_g, v_ln1_b, v_b_up, v_ln2_g, v_ln2_b,
               v_meta_tokens, v_qk_conv_w]
    shapes_w = [tuple(w.shape) for w in small_w]
    small_g = [g.reshape(s) for g, s in zip(small_g, shapes_w)]
    sd, sm2, sv2 = _adamw_small(_pack(small_g), _pack(small_w), _pack(small_m), _pack(small_v))
    sd, sm2, sv2 = _unpack(sd, shapes_w), _unpack(sm2, shapes_w), _unpack(sv2, shapes_w)

    g_win4 = _aligned_to_nat(g_win_t, 0).astype(BF16).reshape(4, 2, IN_NAT // N_DEV, D)
    sib = _pair_exchange([g_win4])
    recv_win = _chip_exchange([_pair_sum(g_win4, sib[0], "pair_sum_w_in")])
    names = ["w_in", "s5_w_glu", "m_w_out", "w_o", "w_up", "w_down"]
    recv = [recv_win[0], recv_mid[2], recv_mid[1], recv_mid[0], recv_early[1], recv_early[0]]
    big_m = [m_w_in[0].T, m_s5_w_glu[0], m_m_w_out[0], m_w_o[0], m_w_up[0], m_w_down[0]]
    big_v = [v_w_in[0].T, v_s5_w_glu[0], v_m_w_out[0], v_w_o[0], v_w_up[0], v_w_down[0]]
    big_w = [w_in[0].T, s5_w_glu[0], m_w_out[0], w_o[0], w_up[0], w_down[0]]
    big_out = [_adamw_big(r, w, m, v, "adamw_" + nm_) for r, w, m, v, nm_ in zip(recv, big_w, big_m, big_v, names)]
    big_out[0] = [o.T for o in big_out[0]]

    order = ["meta_tokens", "ln0_g", "ln0_b", "w_in", "b_in", "qk_conv_w", "qk_conv_b", "s5_lambda_re", "s5_lambda_im",
             "s5_log_dt", "s5_b_re", "s5_b_im", "s5_c_re", "s5_c_im", "s5_d", "s5_w_glu", "m_norm_g", "m_w_out", "w_o",
             "ln1_g", "ln1_b", "w_up", "b_up", "w_down", "ln2_g", "ln2_b"]
    small_names = ["ln0_g", "ln0_b", "b_in", "qk_conv_b", "s5_lambda_re", "s5_lambda_im", "s5_log_dt", "s5_b_re",
                   "s5_b_im", "s5_c_re", "s5_c_im", "s5_d", "m_norm_g", "ln1_g", "ln1_b", "b_up", "ln2_g", "ln2_b",
                   "meta_tokens", "qk_conv_w"]
    res = {}
    for i, n in enumerate(small_names):
        res[n] = (small_g[i], sd[i], sm2[i], sv2[i])
    for i, n in enumerate(names):
        res[n] = tuple(o[None] for o in big_out[i])
    outs = [loss, grad_x]
    for kind in range(4):
        outs += [res[n][kind] for n in order]
    return tuple(outs)
```

```python
import functools
import math

import jax
import jax.numpy as jnp
from jax import lax
from jax.experimental import pallas as pl
from jax.experimental.pallas import tpu as pltpu

F32 = jnp.float32
BF16 = jnp.bfloat16

D_MODEL = 1024
N_META = 16
CHUNK = 64
PAD_ROWS = CHUNK - N_META
S5_WIDTH = 512
S5_GROUP = 16
S5_GROUPS = 32
S5_STATE = 64
S5_COLS = 2 * S5_GROUPS * S5_STATE
S5_BLK = 4
M_HEADS = 4
M_DK = 128
M_DV = 256
D_FF = 4096
N_DEV = 8
ALPHA = 2.0 ** 0.25
LN_EPS = 1e-5
IN_NAT = 5640
P_V, P_O, P_GS, P_GM, P_QK, P_U, P_IF, PW = 0, 1024, 2048, 3072, 4096, 5120, 5632, 5760
N_U, N_Q, N_K, N_V, N_O, N_I, N_GS, N_GM = 0, 512, 1024, 1536, 2560, 3584, 3592, 4616

ADAM_LR, ADAM_B1, ADAM_B2, ADAM_EPS, ADAM_WD, ADAM_STEP = 0.001, 0.9, 0.999, 1e-08, 0.01, 10

VMEM_LIMIT = 56 * 1024 * 1024
MESH = pl.DeviceIdType.MESH


def _pick(n, cands):
    for c in cands:
        if n % c == 0:
            return c
    raise ValueError(f"no tile for {n} among {cands}")


def _cparams(sem):
    return pltpu.CompilerParams(dimension_semantics=sem, vmem_limit_bytes=VMEM_LIMIT)


def _dot(a, b, ca, cb):
    return lax.dot_general(a, b, (((ca,), (cb,)), ((), ())), preferred_element_type=F32)


def _sigmoid(x):
    return 1.0 / (1.0 + jnp.exp(-x))


def _peer(k):
    x, y, c = lax.axis_index("x"), lax.axis_index("y"), lax.axis_index("c")
    px = 1 - x if k & 4 else x
    py = 1 - y if k & 2 else y
    pc = 1 - c if k & 1 else c
    return (px, py, pc), 4 * px + 2 * py + pc


def _my_id():
    return 4 * lax.axis_index("x") + 2 * lax.axis_index("y") + lax.axis_index("c")


def _hbm_call(body, name, arrs, out_shape, n_remote):
    n = len(arrs)
    return pl.pallas_call(
        body, name=name,
        out_shape=tuple(out_shape),
        in_specs=[pl.BlockSpec(memory_space=pl.ANY)] * n,
        out_specs=tuple([pl.BlockSpec(memory_space=pl.ANY)] * len(out_shape)),
        scratch_shapes=[pltpu.SemaphoreType.DMA((n, n_remote)),
                        pltpu.SemaphoreType.DMA((n, n_remote)),
                        pltpu.SemaphoreType.DMA((n,))],
    )(*arrs)


class _Gather:
    def __init__(self, ins, outs, send_sems, recv_sems, local_sems):
        self.ins, self.outs, self.n = ins, outs, len(ins)
        self.send_sems, self.recv_sems, self.local_sems = send_sems, recv_sems, local_sems
        x, y, c = lax.axis_index("x"), lax.axis_index("y"), lax.axis_index("c")
        self.c = c
        self.me, self.sibling = (x, y, c), (x, y, 1 - c)
        self.chips = [(1 - x, y), (x, 1 - y), (1 - x, 1 - y)]

    def slot(self, a, dev):
        return self.outs[a].at[4 * dev[0] + 2 * dev[1] + dev[2]]

    def copy(self, a, k, block, to, own=False):
        return pltpu.make_async_remote_copy(
            src_ref=self.ins[a] if own else self.slot(a, block), dst_ref=self.slot(a, block),
            send_sem=self.send_sems.at[a, k], recv_sem=self.recv_sems.at[a, k],
            device_id=to, device_id_type=MESH)

    def first_sends(self, a):
        return [self.copy(a, 0, self.me, self.sibling, own=True)] + [
            self.copy(a, 1 + j, self.me, (*chip, self.c), own=True) for j, chip in enumerate(self.chips)]

    def start(self):
        for a in range(self.n):
            pltpu.make_async_copy(self.ins[a], self.slot(a, self.me), self.local_sems.at[a]).start()
            for cp in self.first_sends(a):
                cp.start()

    def finish(self):
        forwards = []
        for j, chip in enumerate(self.chips):
            for a in range(self.n):
                self.copy(a, 1 + j, (*chip, self.c), self.me).wait_recv()
                fwd = self.copy(a, 4 + j, (*chip, self.c), self.sibling)
                fwd.start()
                forwards.append(fwd)
        for a in range(self.n):
            self.copy(a, 0, self.sibling, self.me).wait_recv()
            for j, chip in enumerate(self.chips):
                self.copy(a, 4 + j, (*chip, 1 - self.c), self.me).wait_recv()
        for a in range(self.n):
            for cp in self.first_sends(a):
                cp.wait_send()
            pltpu.make_async_copy(self.ins[a], self.slot(a, self.me), self.local_sems.at[a]).wait()
        for cp in forwards:
            cp.wait_send()

    @staticmethod
    def out_shapes(arrs):
        return [jax.ShapeDtypeStruct((N_DEV,) + tuple(a.shape), a.dtype) for a in arrs]


class _AllToAll:
    def __init__(self, ins, outs, send_sems, recv_sems, local_sems):
        self.ins, self.outs, self.n = ins, outs, len(ins)
        self.send_sems, self.recv_sems, self.local_sems = send_sems, recv_sems, local_sems
        self.me = _my_id()

    def copy(self, a, k, landing):
        peer, pid = _peer(k)
        return pltpu.make_async_remote_copy(
            src_ref=self.ins[a].at[pid], dst_ref=self.outs[a].at[pid if landing else self.me],
            send_sem=self.send_sems.at[a, k - 1], recv_sem=self.recv_sems.at[a, k - 1],
            device_id=peer, device_id_type=MESH)

    def local(self, a):
        return pltpu.make_async_copy(self.ins[a].at[self.me], self.outs[a].at[self.me], self.local_sems.at[a])

    def start(self):
        for a in range(self.n):
            self.local(a).start()
            for k in range(1, N_DEV):
                self.copy(a, k, False).start()

    def finish(self):
        for a in range(self.n):
            for k in range(1, N_DEV):
                self.copy(a, k, True).wait_recv()
        for a in range(self.n):
            for k in range(1, N_DEV):
                self.copy(a, k, False).wait_send()
            self.local(a).wait()

    @staticmethod
    def out_shapes(arrs):
        return [jax.ShapeDtypeStruct(tuple(a.shape), a.dtype) for a in arrs]


def _comm_scratch(n):
    return [pltpu.SemaphoreType.DMA((n, N_DEV - 1)), pltpu.SemaphoreType.DMA((n, N_DEV - 1)),
            pltpu.SemaphoreType.DMA((n,))]


def _ride_call(body, name, grid, in_specs, out_specs, out_shape, scratch, args, comm=None, aliases=None):
    n_in, n_out = len(in_specs), len(out_specs)
    in_specs, out_specs, out_shape = list(in_specs), list(out_specs), list(out_shape)
    scratch, args = list(scratch), list(args)
    kernel_fn = body
    if comm is not None:
        cls, arrs = comm
        n = len(arrs)

        def kernel_fn(*refs):
            ins, cin = refs[:n_in], refs[n_in:n_in + n]
            outs = refs[n_in + n:n_in + n + n_out]
            cout = refs[n_in + n + n_out:n_in + 2 * n + n_out]
            own_scratch, sems = refs[n_in + 2 * n + n_out:-3], refs[-3:]
            exchange = cls(cin, cout, *sems)
            ids = [pl.program_id(d) for d in range(len(grid))]
            first = functools.reduce(lambda a, b: a & b, [i == 0 for i in ids])
            last = functools.reduce(lambda a, b: a & b, [i == g - 1 for i, g in zip(ids, grid)])

            @pl.when(first)
            def _():
                exchange.start()

            body(*ins, *outs, *own_scratch)

            @pl.when(last)
            def _():
                exchange.finish()

        in_specs += [pl.BlockSpec(memory_space=pl.ANY)] * n
        args += list(arrs)
        out_specs += [pl.BlockSpec(memory_space=pl.ANY)] * n
        out_shape += cls.out_shapes(arrs)
        scratch += _comm_scratch(n)
    res = pl.pallas_call(
        kernel_fn, name=name, grid=grid,
        in_specs=in_specs, out_specs=tuple(out_specs), out_shape=tuple(out_shape),
        scratch_shapes=scratch, input_output_aliases=aliases or {},
        compiler_params=_cparams(("arbitrary",) * len(grid)),
    )(*args)
    return list(res[:n_out]), list(res[n_out:])


def _allgather(arrs):
    n = len(arrs)

    def body(*refs):
        g = _Gather(refs[:n], refs[n:2 * n], *refs[2 * n:])
        g.start()
        g.finish()

    return _hbm_call(body, "allgather_weights", arrs, _Gather.out_shapes(arrs), N_DEV - 1)


def _shard_tile(R, C):
    if R % 128 == 0:
        return 128, C
    return R, _pick(C, (256, 128))


def _allreduce_small(pack):
    rows = pack.shape[0]

    def body(in_ref, out_ref, buf, send_sems, recv_sems):
        me = _my_id()
        sends = []
        for k in range(1, N_DEV):
            peer, pid = _peer(k)
            cp = pltpu.make_async_remote_copy(
                src_ref=in_ref, dst_ref=buf.at[me],
                send_sem=send_sems.at[k - 1], recv_sem=recv_sems.at[k - 1],
                device_id=peer, device_id_type=MESH)
            cp.start()
            sends.append(cp)
        for k in range(1, N_DEV):
            peer, pid = _peer(k)
            pltpu.make_async_remote_copy(
                src_ref=in_ref, dst_ref=buf.at[pid],
                send_sem=send_sems.at[k - 1], recv_sem=recv_sems.at[k - 1],
                device_id=peer, device_id_type=MESH).wait_recv()
        for cp in sends:
            cp.wait_send()
        buf[pl.ds(me, 1)] = in_ref[...][None]
        acc = buf[0]
        for s in range(1, N_DEV):
            acc = acc + buf[s]
        out_ref[...] = acc

    return pl.pallas_call(
        body, name="allreduce_small",
        out_shape=jax.ShapeDtypeStruct((rows, 128), F32),
        in_specs=[pl.BlockSpec(memory_space=pltpu.VMEM)],
        out_specs=pl.BlockSpec(memory_space=pltpu.VMEM),
        scratch_shapes=[pltpu.VMEM((N_DEV, rows, 128), F32),
                        pltpu.SemaphoreType.DMA((N_DEV - 1,)),
                        pltpu.SemaphoreType.DMA((N_DEV - 1,))],
        compiler_params=pltpu.CompilerParams(vmem_limit_bytes=VMEM_LIMIT),
    )(pack)


def _mm_tiles(M, N, K, mode, out_bytes, n_extra_f32, a_bytes, b_bytes):
    budget = 40 * 1024 * 1024
    tms = [t for t in (1664, 1152, 1024, 640, 512, 256, 128) if M % t == 0]
    tns = [t for t in (1152, 1024, 640, 512, 384, 256, 128) if N % t == 0]
    if mode == "tn":
        tks = [t for t in (1664, 640, 512, 256, 128) if K % t == 0]
    else:
        tks = [K] if K <= 1152 else [t for t in (1152, 1024, 640, 512) if K % t == 0]
    best = None
    for tm in tms:
        for tn in tns:
            for tk in tks:
                nk = K // tk
                use = 2 * (tm * tk * a_bytes + tk * tn * b_bytes) + 2 * tm * tn * out_bytes
                use += 2 * n_extra_f32 * tm * tn * 4 + tm * tn * 4 * (2 if nk > 1 else 1)
                if use > budget:
                    continue
                score = (tm * tn * tk, tm * tn)
                if best is None or score > best[0]:
                    best = (score, (tm, tn, tk))
    assert best is not None, (M, N, K, mode)
    return best[1]


def _mm(a, b, mode, name, *, bias=None, add=None, add_scale=1.0, sqrt_gate_of=None,
        relu2_out=False, colsum=None, out_dtype=F32, comm=None):
    if mode == "nn":
        (M, K), (K2, N) = a.shape, b.shape
    elif mode == "nt":
        (M, K), (N, K2) = a.shape, b.shape
    else:
        (K, M), (K2, N) = a.shape, b.shape
    assert K == K2, (a.shape, b.shape, mode)
    has_bias, has_add, has_gate = bias is not None, add is not None, sqrt_gate_of is not None
    tm, tn, tk = _mm_tiles(M, N, K, mode, jnp.dtype(out_dtype).itemsize, int(has_add) + int(has_gate),
                           a.dtype.itemsize, b.dtype.itemsize)
    nk = K // tk
    assert colsum is None or mode == "tn"
    assert colsum != "a" or N == tn
    comm_cls, comm_arrs = comm if comm is not None else (None, [])
    nc = len(comm_arrs)
    grid = (N // tn, M // tm, nk)

    def body(*refs):
        it = iter(refs)
        a_ref, b_ref = next(it), next(it)
        bias_ref = next(it) if has_bias else None
        add_ref = next(it) if has_add else None
        gate_ref = next(it) if has_gate else None
        comm_ins = [next(it) for _ in range(nc)]
        o_ref = next(it)
        cs_ref = next(it) if colsum else None
        comm_outs = [next(it) for _ in range(nc)]
        acc_ref = next(it) if nk > 1 else None
        j, i, k = pl.program_id(0), pl.program_id(1), pl.program_id(2)
        if nc:
            exchange = comm_cls(comm_ins, comm_outs, next(it), next(it), next(it))

            @pl.when((j == 0) & (i == 0) & (k == 0))
            def _():
                exchange.start()

        av = a_ref[...].astype(BF16)
        bv = b_ref[...].astype(BF16)
        if mode == "nn":
            part = _dot(av, bv, 1, 0)
        elif mode == "nt":
            part = _dot(av, bv, 1, 1)
        else:
            part = _dot(av, bv, 0, 0)
        if colsum == "b":
            @pl.when((i == 0) & (k == 0))
            def _():
                cs_ref[...] = jnp.zeros_like(cs_ref)

            @pl.when(i == 0)
            def _():
                cs_ref[0:1, :] += jnp.sum(b_ref[...].astype(F32), axis=0, keepdims=True)
        if colsum == "a":
            @pl.when(k == 0)
            def _():
                cs_ref[...] = jnp.zeros_like(cs_ref)

            cs_ref[0:1, :] += jnp.sum(a_ref[...].astype(F32), axis=0, keepdims=True)

        def finish(r):
            if has_bias:
                r = r + bias_ref[...]
            if has_add:
                r = r + add_scale * add_ref[...]
            if has_gate:
                r = r * (2.0 * jnp.sqrt(gate_ref[...].astype(F32)))
            if relu2_out:
                r = jnp.square(jnp.maximum(r, 0.0))
            o_ref[...] = r.astype(out_dtype)

        if nk == 1:
            finish(part)
        else:
            @pl.when(k == 0)
            def _():
                acc_ref[...] = part

            @pl.when(k > 0)
            def _():
                acc_ref[...] += part

            @pl.when(k == nk - 1)
            def _():
                finish(acc_ref[...])

        if nc:
            @pl.when((j == grid[0] - 1) & (i == grid[1] - 1) & (k == nk - 1))
            def _():
                exchange.finish()

    if mode == "nn":
        a_spec = pl.BlockSpec((tm, tk), lambda j, i, k: (i, k))
        b_spec = pl.BlockSpec((tk, tn), lambda j, i, k: (k, j))
    elif mode == "nt":
        a_spec = pl.BlockSpec((tm, tk), lambda j, i, k: (i, k))
        b_spec = pl.BlockSpec((tn, tk), lambda j, i, k: (j, k))
    else:
        a_spec = pl.BlockSpec((tk, tm), lambda j, i, k: (k, i))
        b_spec = pl.BlockSpec((tk, tn), lambda j, i, k: (k, j))
    in_specs, args = [a_spec, b_spec], [a, b]
    if has_bias:
        in_specs.append(pl.BlockSpec((1, tn), lambda j, i, k: (0, j)))
        args.append(bias)
    if has_add:
        in_specs.append(pl.BlockSpec((tm, tn), lambda j, i, k: (i, j)))
        args.append(add)
    if has_gate:
        in_specs.append(pl.BlockSpec((tm, tn), lambda j, i, k: (i, j)))
        args.append(sqrt_gate_of)
    out_shape = [jax.ShapeDtypeStruct((M, N), out_dtype)]
    out_specs = [pl.BlockSpec((tm, tn), lambda j, i, k: (i, j))]
    if colsum == "b":
        out_shape.append(jax.ShapeDtypeStruct((8, N), F32))
        out_specs.append(pl.BlockSpec((8, tn), lambda j, i, k: (0, j)))
    if colsum == "a":
        out_shape.append(jax.ShapeDtypeStruct((8, M), F32))
        out_specs.append(pl.BlockSpec((8, tm), lambda j, i, k: (0, i)))
    scratch = [pltpu.VMEM((tm, tn), F32)] if nk > 1 else []
    if nc:
        in_specs += [pl.BlockSpec(memory_space=pl.ANY)] * nc
        args += list(comm_arrs)
        out_specs += [pl.BlockSpec(memory_space=pl.ANY)] * nc
        out_shape += comm_cls.out_shapes(comm_arrs)
        scratch += _comm_scratch(nc)
    res = pl.pallas_call(
        body, name=name, grid=grid,
        in_specs=in_specs, out_specs=tuple(out_specs), out_shape=tuple(out_shape),
        scratch_shapes=scratch,
        compiler_params=_cparams(("arbitrary", "arbitrary", "arbitrary")),
    )(*args)
    return res if len(res) > 1 else res[0]


def _ln_rows(v, g, b):
    mu = jnp.mean(v, axis=-1, keepdims=True)
    xc = v - mu
    var = jnp.mean(xc * xc, axis=-1, keepdims=True)
    return xc * lax.rsqrt(var + LN_EPS) * g + b


def _ln_bwd_rows(dy, v, g):
    mu = jnp.mean(v, axis=-1, keepdims=True)
    xc = v - mu
    var = jnp.mean(xc * xc, axis=-1, keepdims=True)
    rstd = lax.rsqrt(var + LN_EPS)
    xhat = xc * rstd
    dxh = dy * g
    dv = rstd * (dxh - jnp.mean(dxh, axis=-1, keepdims=True)
                 - xhat * jnp.mean(dxh * xhat, axis=-1, keepdims=True))
    return dv, xhat


def _real_tile(S):
    return _pick(S, (512, 256, 128, 64))


def _real_rows(rb, ncols, lp):
    return pl.BlockSpec((pl.Element(rb), pl.Element(ncols)),
                        lambda bb, j: (pl.multiple_of(bb * lp + CHUNK + j * rb, CHUNK), 0))


def _head_rows(ncols, lp):
    return pl.BlockSpec((CHUNK, ncols), lambda bb: (bb * (lp // CHUNK), 0))


def _ln0_fwd(x, meta, g, b):
    B, S, D = x.shape
    lp = S + CHUNK
    rb = _real_tile(S)

    def body(x_ref, g_ref, b_ref, h_ref, hb_ref):
        y = _ln_rows(x_ref[0], g_ref[...], b_ref[...])
        h_ref[...] = y
        hb_ref[...] = y.astype(BF16)

    vec = pl.BlockSpec((1, D), lambda bb, j: (0, 0))
    h, hb = pl.pallas_call(
        body, name="ln0_fwd", grid=(B, S // rb),
        in_specs=[pl.BlockSpec((1, rb, D), lambda bb, j: (bb, j, 0)), vec, vec],
        out_specs=(_real_rows(rb, D, lp), _real_rows(rb, D, lp)),
        out_shape=(jax.ShapeDtypeStruct((B * lp, D), F32), jax.ShapeDtypeStruct((B * lp, D), BF16)),
        compiler_params=_cparams(("arbitrary", "arbitrary")),
    )(x, g, b)

    def head(meta_ref, g_ref, b_ref, hin_ref, hbin_ref, h_ref, hb_ref):
        m = _ln_rows(meta_ref[...], g_ref[...], b_ref[...])
        h_ref[0:PAD_ROWS, :] = jnp.zeros((PAD_ROWS, D), F32)
        h_ref[PAD_ROWS:CHUNK, :] = m
        hb_ref[0:PAD_ROWS, :] = jnp.zeros((PAD_ROWS, D), BF16)
        hb_ref[PAD_ROWS:CHUNK, :] = m.astype(BF16)

    vec1 = pl.BlockSpec((1, D), lambda bb: (0, 0))
    anyspec = pl.BlockSpec(memory_space=pl.ANY)
    return pl.pallas_call(
        head, name="ln0_head", grid=(B,),
        in_specs=[pl.BlockSpec((N_META, D), lambda bb: (0, 0)), vec1, vec1, anyspec, anyspec],
        out_specs=(_head_rows(D, lp), _head_rows(D, lp)),
        out_shape=(jax.ShapeDtypeStruct((B * lp, D), F32), jax.ShapeDtypeStruct((B * lp, D), BF16)),
        input_output_aliases={3: 0, 4: 1},
        compiler_params=_cparams(("arbitrary",)),
    )(meta, g, b, h, hb)


def _ln0_bwd(dh0, x, meta, g):
    B, S, D = x.shape
    lp = S + CHUNK
    rb = _real_tile(S)

    def body(dh_ref, x_ref, g_ref, dx_ref, dg_ref, db_ref):
        @pl.when((pl.program_id(0) == 0) & (pl.program_id(1) == 0))
        def _():
            dg_ref[...] = jnp.zeros_like(dg_ref)
            db_ref[...] = jnp.zeros_like(db_ref)

        dy = dh_ref[...]
        dv, xhat = _ln_bwd_rows(dy, x_ref[0], g_ref[...])
        dx_ref[0] = dv
        dg_ref[0:1, :] += jnp.sum(dy * xhat, axis=0, keepdims=True)
        db_ref[0:1, :] += jnp.sum(dy, axis=0, keepdims=True)

    const = lambda bb, j: (0, 0)
    xblk = pl.BlockSpec((1, rb, D), lambda bb, j: (bb, j, 0))
    acc_shape = jax.ShapeDtypeStruct((8, D), F32)
    dx, dg, db = pl.pallas_call(
        body, name="ln0_bwd", grid=(B, S // rb),
        in_specs=[_real_rows(rb, D, lp), xblk, pl.BlockSpec((1, D), const)],
        out_specs=(xblk, pl.BlockSpec((8, D), const), pl.BlockSpec((8, D), const)),
        out_shape=(jax.ShapeDtypeStruct((B, S, D), F32), acc_shape, acc_shape),
        compiler_params=_cparams(("arbitrary", "arbitrary")),
    )(dh0, x, g)

    def head(dh_ref, meta_ref, g_ref, dmeta_ref, dg_ref, db_ref):
        @pl.when(pl.program_id(0) == 0)
        def _():
            dmeta_ref[...] = jnp.zeros_like(dmeta_ref)
            dg_ref[...] = jnp.zeros_like(dg_ref)
            db_ref[...] = jnp.zeros_like(db_ref)

        dy = dh_ref[PAD_ROWS:CHUNK, :]
        dv, xhat = _ln_bwd_rows(dy, meta_ref[...], g_ref[...])
        dmeta_ref[...] += dv
        dg_ref[0:1, :] += jnp.sum(dy * xhat, axis=0, keepdims=True)
        db_ref[0:1, :] += jnp.sum(dy, axis=0, keepdims=True)

    c1 = lambda bb: (0, 0)
    dmeta, dgm, dbm = pl.pallas_call(
        head, name="ln0_bwd_head", grid=(B,),
        in_specs=[_head_rows(D, lp), pl.BlockSpec((N_META, D), c1), pl.BlockSpec((1, D), c1)],
        out_specs=(pl.BlockSpec((N_META, D), c1), pl.BlockSpec((8, D), c1), pl.BlockSpec((8, D), c1)),
        out_shape=(jax.ShapeDtypeStruct((N_META, D), F32), acc_shape, acc_shape),
        compiler_params=_cparams(("arbitrary",)),
    )(dh0, meta, g)
    return dx, dmeta, dg + dgm, db + dbm


def _ln_res_fwd(h_prev, r, g, b, name):
    T, D = h_prev.shape
    tr = _pick(T, (320, 256, 128, 64))

    def body(hp_ref, r_ref, g_ref, b_ref, pre_ref, h_ref, hb_ref):
        pre = ALPHA * hp_ref[...] + r_ref[...]
        y = _ln_rows(pre, g_ref[...], b_ref[...])
        pre_ref[...] = pre
        h_ref[...] = y
        hb_ref[...] = y.astype(BF16)

    row = pl.BlockSpec((tr, D), lambda i: (i, 0))
    vec = pl.BlockSpec((1, D), lambda i: (0, 0))
    return pl.pallas_call(
        body, name=name, grid=(T // tr,),
        in_specs=[row, row, vec, vec], out_specs=(row, row, row),
        out_shape=(jax.ShapeDtypeStruct((T, D), F32), jax.ShapeDtypeStruct((T, D), F32),
                   jax.ShapeDtypeStruct((T, D), BF16)),
        compiler_params=_cparams(("arbitrary",)),
    )(h_prev, r, g, b)


def _ln_bwd(dh, pre, g, name):
    T, D = dh.shape
    tr = _pick(T, (320, 256, 128, 64))

    def body(dh_ref, pre_ref, g_ref, dp_ref, dpb_ref, dg_ref, db_ref):
        @pl.when(pl.program_id(0) == 0)
        def _():
            dg_ref[...] = jnp.zeros_like(dg_ref)
            db_ref[...] = jnp.zeros_like(db_ref)

        dy = dh_ref[...]
        dv, xhat = _ln_bwd_rows(dy, pre_ref[...], g_ref[...])
        dp_ref[...] = dv
        dpb_ref[...] = dv.astype(BF16)
        dg_ref[0:1, :] += jnp.sum(dy * xhat, axis=0, keepdims=True)
        db_ref[0:1, :] += jnp.sum(dy, axis=0, keepdims=True)

    row = pl.BlockSpec((tr, D), lambda i: (i, 0))
    vec = pl.BlockSpec((1, D), lambda i: (0, 0))
    acc = pl.BlockSpec((8, D), lambda i: (0, 0))
    return pl.pallas_call(
        body, name=name, grid=(T // tr,),
        in_specs=[row, row, vec], out_specs=(row, row, acc, acc),
        out_shape=(jax.ShapeDtypeStruct((T, D), F32), jax.ShapeDtypeStruct((T, D), BF16),
                   jax.ShapeDtypeStruct((8, D), F32), jax.ShapeDtypeStruct((8, D), F32)),
        compiler_params=_cparams(("arbitrary",)),
    )(dh, pre, g)


def _ln2_loss(h1, ff, g, b, target):
    T, D = h1.shape
    B, S, _ = target.shape
    lp = S + CHUNK
    rb = _real_tile(S)

    def body(h_ref, ff_ref, g_ref, b_ref, t_ref, loss_ref, dp_ref, dpb_ref, dg_ref, db_ref):
        @pl.when((pl.program_id(0) == 0) & (pl.program_id(1) == 0))
        def _():
            loss_ref[...] = jnp.zeros_like(loss_ref)
            dg_ref[...] = jnp.zeros_like(dg_ref)
            db_ref[...] = jnp.zeros_like(db_ref)

        pre = ALPHA * h_ref[...] + ff_ref[...]
        gg = g_ref[...]
        y = _ln_rows(pre, gg, b_ref[...])
        err = y - t_ref[0]
        loss_ref[0:1, 0:1] += 0.5 * jnp.sum(jnp.mean(err * err, axis=-1, keepdims=True), axis=0, keepdims=True)
        dy = err * (1.0 / D)
        dv, xhat = _ln_bwd_rows(dy, pre, gg)
        dp_ref[...] = dv
        dpb_ref[...] = dv.astype(BF16)
        dg_ref[0:1, :] += jnp.sum(dy * xhat, axis=0, keepdims=True)
        db_ref[0:1, :] += jnp.sum(dy, axis=0, keepdims=True)

    row = _real_rows(rb, D, lp)
    const = lambda bb, j: (0, 0)
    loss, dp, dpb, dg, db = pl.pallas_call(
        body, name="ln2_loss", grid=(B, S // rb),
        in_specs=[row, row, pl.BlockSpec((1, D), const), pl.BlockSpec((1, D), const),
                  pl.BlockSpec((1, rb, D), lambda bb, j: (bb, j, 0))],
        out_specs=(pl.BlockSpec((8, 128), const), row, row,
                   pl.BlockSpec((8, D), const), pl.BlockSpec((8, D), const)),
        out_shape=(jax.ShapeDtypeStruct((8, 128), F32),
                   jax.ShapeDtypeStruct((T, D), F32), jax.ShapeDtypeStruct((T, D), BF16),
                   jax.ShapeDtypeStruct((8, D), F32), jax.ShapeDtypeStruct((8, D), F32)),
        compiler_params=_cparams(("arbitrary", "arbitrary")),
    )(h1, ff, g, b, target)

    def head(dpin_ref, dpbin_ref, dp_ref, dpb_ref):
        dp_ref[...] = jnp.zeros((CHUNK, D), F32)
        dpb_ref[...] = jnp.zeros((CHUNK, D), BF16)

    anyspec = pl.BlockSpec(memory_space=pl.ANY)
    dp, dpb = pl.pallas_call(
        head, name="ln2_head", grid=(B,),
        in_specs=[anyspec, anyspec],
        out_specs=(_head_rows(D, lp), _head_rows(D, lp)),
        out_shape=(jax.ShapeDtypeStruct((T, D), F32), jax.ShapeDtypeStruct((T, D), BF16)),
        input_output_aliases={0: 0, 1: 1},
        compiler_params=_cparams(("arbitrary",)),
    )(dp, dpb)
    return loss, dp, dpb, dg, db


def _s5_prep(lam_re, lam_im, log_dt, b_re, b_im, c_re, c_im):
    dt = jnp.exp(log_dt)[:, None]
    mag = jnp.exp(lam_re * dt)
    ar = mag * jnp.cos(lam_im * dt)
    ai = mag * jnp.sin(lam_im * dt)
    nr, ni = ar - 1.0, ai
    den = lam_re * lam_re + lam_im * lam_im
    cr = (nr * lam_re + ni * lam_im) / den
    ci = (ni * lam_re - nr * lam_im) / den
    bbr = cr[..., None] * b_re - ci[..., None] * b_im
    bbi = cr[..., None] * b_im + ci[..., None] * b_re
    eye = jnp.eye(8, dtype=F32)
    bb = jnp.stack([bbr, bbi]).reshape(2, S5_BLK, 8, S5_STATE, S5_GROUP)
    bblk = jnp.einsum("rbgph,gj->bghrjp", bb, eye).reshape(S5_BLK, 128, 1024)
    cc = jnp.stack([c_re, -c_im]).reshape(2, S5_BLK, 8, S5_GROUP, S5_STATE)
    cblk = jnp.einsum("rbghp,gj->brjpgh", cc, eye).reshape(S5_BLK, 1024, 128)
    return ar.reshape(1, 2048), ai.reshape(1, 2048), bblk, cblk


def _scan_tables(ar, ai):
    pr, pi = [ar], [ai]
    for _ in range(7):
        pr, pi = pr + [pr[-1] * ar - pi[-1] * ai], pi + [pr[-1] * ai + pi[-1] * ar]
    pw_r = jnp.concatenate(pr, axis=0)
    pw_i = jnp.concatenate(pi, axis=0)
    rev_r = jnp.concatenate(pr[::-1], axis=0)
    rev_i = jnp.concatenate(pi[::-1], axis=0)
    row = jnp.arange(8)[:, None]

    def tables(sign, reverse):
        rows = []
        for n, sh in ((0, 1), (1, 2), (3, 4)):
            mask = (row < 8 - sh) if reverse else (row >= sh)
            rows.append(jnp.where(mask, pw_r[n][None, :], 0.0))
            rows.append(jnp.where(mask, sign * pw_i[n][None, :], 0.0))
        cr_ = rev_r if reverse else pw_r
        ci_ = rev_i if reverse else pw_i
        rows += [cr_, sign * ci_]
        return jnp.stack(rows)

    return tables(1.0, False), tables(-1.0, True)


def _seg_scan(s_ref, row0, seg, lam_ref, tab_ref, carry_ref, reverse, cseg_ref=None, extra=None):
    sgn = -1.0 if reverse else 1.0
    take, edge = (0, 7) if reverse else (7, 0)
    rowid = lax.broadcasted_iota(jnp.int32, (8, 128), 0)
    all_pairs = [(blk * 1024 + j * 128, blk * 1024 + j * 128 + 512, blk * 512 + j * 128)
                 for blk in range(S5_BLK) for j in range(4)]

    def rows(it):
        i = (seg - 1 - it) if reverse else it
        return i, pl.multiple_of(row0 + i * 8, 8)

    for half in range(2):
        pairs = all_pairs[8 * half:8 * half + 8]

        def pass1(it, carry):
            _, r0 = rows(it)
            out = []
            for n, (cre, cim, tc) in enumerate(pairs):
                lr = lam_ref[0, :, tc:tc + 128]
                li = sgn * lam_ref[1, :, tc:tc + 128]
                pr, pi = carry[2 * n], carry[2 * n + 1]
                xr = lr * pr - li * pi + s_ref[pl.ds(r0, 8), cre:cre + 128]
                xi = lr * pi + li * pr + s_ref[pl.ds(r0, 8), cim:cim + 128]
                s_ref[pl.ds(r0, 8), cre:cre + 128] = xr
                s_ref[pl.ds(r0, 8), cim:cim + 128] = xi
                out += [xr, xi]
            return tuple(out)

        ends = lax.fori_loop(0, seg, pass1, tuple(jnp.zeros((8, 128), F32) for _ in range(16)))

        start = []
        for n, (cre, cim, tc) in enumerate(pairs):
            xr, xi = ends[2 * n], ends[2 * n + 1]
            for lvl, sh in enumerate((1, 2, 4)):
                lr = tab_ref[2 * lvl, :, tc:tc + 128]
                li = tab_ref[2 * lvl + 1, :, tc:tc + 128]
                shift = (8 - sh) if reverse else sh
                sr = pltpu.roll(xr, shift, 0)
                si = pltpu.roll(xi, shift, 0)
                xr, xi = xr + lr * sr - li * si, xi + lr * si + li * sr
            pr = tab_ref[6, :, tc:tc + 128]
            pi = tab_ref[7, :, tc:tc + 128]
            c_r = carry_ref[:, cre:cre + 128]
            c_i = carry_ref[:, cim:cim + 128]
            er, ei = xr + pr * c_r - pi * c_i, xi + pr * c_i + pi * c_r
            back = 7 if reverse else 1
            in_r = jnp.where(rowid == edge, c_r, pltpu.roll(er, back, 0))
            in_i = jnp.where(rowid == edge, c_i, pltpu.roll(ei, back, 0))
            nr = jnp.sum(jnp.where(rowid == take, er, 0.0), axis=0, keepdims=True)
            ni = jnp.sum(jnp.where(rowid == take, ei, 0.0), axis=0, keepdims=True)
            carry_ref[:, cre:cre + 128] = jnp.broadcast_to(nr, (8, 128))
            carry_ref[:, cim:cim + 128] = jnp.broadcast_to(ni, (8, 128))
            if cseg_ref is not None:
                cseg_ref[0:8, cre:cre + 128] = in_r
                cseg_ref[0:8, cim:cim + 128] = in_i
            start += [in_r, in_i]

        def pass2(it, carry):
            i, r0 = rows(it)
            out = []
            for n, (cre, cim, tc) in enumerate(pairs):
                lr = lam_ref[0, :, tc:tc + 128]
                li = sgn * lam_ref[1, :, tc:tc + 128]
                dr, di = carry[2 * n], carry[2 * n + 1]
                dr, di = lr * dr - li * di, lr * di + li * dr
                xr = s_ref[pl.ds(r0, 8), cre:cre + 128] + dr
                xi = s_ref[pl.ds(r0, 8), cim:cim + 128] + di
                s_ref[pl.ds(r0, 8), cre:cre + 128] = xr
                s_ref[pl.ds(r0, 8), cim:cim + 128] = xi
                if extra is not None:
                    extra(i, cre, cim, tc, xr, xi)
                out += [dr, di]
            return tuple(out)

        lax.fori_loop(0, seg, pass2, tuple(start))


def _to_segments(src_refs, dst_ref, seg, first=None):
    sub = lax.broadcasted_iota(jnp.int32, (8, 1), 0)
    for i in range(seg):
        for c, src in enumerate(src_refs):
            v = src[pl.ds(i, 8, stride=seg), :]
            if first is not None:
                v = jnp.where(first & (sub * seg + i < PAD_ROWS), 0.0, v)
            dst_ref[8 * i:8 * i + 8, c * 128:(c + 1) * 128] = v


def _from_segments(src_ref, dst_ref, seg, fn=None, zero_head=None):
    m = seg // 8
    for j in range(seg):
        for c in range(src_ref.shape[0]):
            v = src_ref[c, pl.ds(64 * (j % m) + j // m, 8, stride=8), :]
            if zero_head is not None and j < PAD_ROWS // 8:
                v = jnp.where(zero_head, 0.0, v)
            dst_ref[8 * j:8 * j + 8, c * 128:(c + 1) * 128] = v if fn is None else fn(v)


def _gelu(x):
    c = math.sqrt(2.0 / math.pi)
    t = jnp.tanh(c * (x + 0.044715 * x * x * x))
    return 0.5 * x * (1.0 + t)


def _gelu_grad(x):
    c = math.sqrt(2.0 / math.pi)
    t = jnp.tanh(c * (x + 0.044715 * x * x * x))
    return 0.5 * (1.0 + t) + 0.5 * x * (1.0 - t * t) * c * (1.0 + 3.0 * 0.044715 * x * x)


def _s5_tile(lp):
    return _pick(lp, (320, 256, 128, 64))


def _s5_fwd(p, bblk, cblk, lam8, tab_f, dskip, lp, comm=None):
    T = p.shape[0]
    ts = _s5_tile(lp)
    seg = ts // 8
    nblk, per_seq = T // ts, lp // ts
    ucol = P_U // S5_WIDTH

    def body(u0, u1, u2, u3, b_ref, c_ref, lam_ref, tab_ref, d_ref, y_ref, gy_ref, cin_ref, s_sc, u_sc, y_sc, carry_sc):
        r = pl.program_id(0)
        first = (r % per_seq) == 0

        @pl.when(first)
        def _():
            carry_sc[...] = jnp.zeros_like(carry_sc)

        cin_ref[0] = carry_sc[...]
        _to_segments((u0, u1, u2, u3), u_sc, seg, first)
        u = u_sc[...]
        ub = u.astype(BF16)
        for blk in range(S5_BLK):
            s_sc[:, blk * 1024:(blk + 1) * 1024] = _dot(ub[:, blk * 128:(blk + 1) * 128], b_ref[blk], 1, 0)
        _seg_scan(s_sc, 0, seg, lam_ref, tab_ref, carry_sc, False)
        for blk in range(S5_BLK):
            sb = s_sc[:, blk * 1024:(blk + 1) * 1024].astype(BF16)
            y_sc[blk] = _dot(sb, c_ref[blk], 1, 0) + d_ref[:, blk * 128:(blk + 1) * 128] * u[:, blk * 128:(blk + 1) * 128]
        _from_segments(y_sc, y_ref, seg)
        gy_ref[...] = _gelu(y_ref[...]).astype(BF16)

    def ublock(c):
        return pl.BlockSpec((ts, 128), lambda r: (r, 4 * ucol + c))

    return _ride_call(
        body, "s5_fwd", (nblk,),
        [ublock(0), ublock(1), ublock(2), ublock(3),
         pl.BlockSpec((S5_BLK, 128, 1024), lambda r: (0, 0, 0)),
         pl.BlockSpec((S5_BLK, 1024, 128), lambda r: (0, 0, 0)),
         pl.BlockSpec((2, 8, 2048), lambda r: (0, 0, 0)),
         pl.BlockSpec((8, 8, 2048), lambda r: (0, 0, 0)),
         pl.BlockSpec((1, S5_WIDTH), lambda r: (0, 0))],
        [pl.BlockSpec((ts, S5_WIDTH), lambda r: (r, 0)),
         pl.BlockSpec((ts, S5_WIDTH), lambda r: (r, 0)),
         pl.BlockSpec((1, 8, S5_COLS), lambda r: (r, 0, 0))],
        [jax.ShapeDtypeStruct((T, S5_WIDTH), F32),
         jax.ShapeDtypeStruct((T, S5_WIDTH), BF16),
         jax.ShapeDtypeStruct((nblk, 8, S5_COLS), F32)],
        [pltpu.VMEM((ts, S5_COLS), F32), pltpu.VMEM((ts, S5_WIDTH), F32),
         pltpu.VMEM((S5_BLK, ts, 128), F32), pltpu.VMEM((8, S5_COLS), F32)],
        (p, p, p, p, bblk.astype(BF16), cblk.astype(BF16), lam8, tab_f, dskip), comm=comm)


def _s5_bwd(p, y_pre, dgy, cin, bblk, cblk, lam8, tab_f, tab_b, dskip, dif, dp, lp, comm=None):
    T = p.shape[0]
    ts = _s5_tile(lp)
    seg = ts // 8
    nblk, per_seq = T // ts, lp // ts
    ucol = P_U // S5_WIDTH
    assert P_IF == P_U + S5_WIDTH and P_U % (S5_WIDTH + 128) == 0

    def body(u0, u1, u2, u3, y0, y1, y2, y3, g0, g1, g2, g3, cin_ref, b_ref, bt_ref, c_ref, ct_ref, lam_ref, tf_ref,
             tb_ref, d_ref, dif_ref, dpin_ref, du_ref, dbb_ref, dcb_ref, dlam_ref, dd_ref, s_sc, a_sc, u_sc, dy_sc,
             y_sc, w_sc, carry_sc, carry_b):
        t = pl.program_id(0)
        r = nblk - 1 - t
        first = (r % per_seq) == 0
        last = (r % per_seq) == per_seq - 1

        @pl.when(t == 0)
        def _():
            dbb_ref[...] = jnp.zeros_like(dbb_ref)
            dcb_ref[...] = jnp.zeros_like(dcb_ref)
            dlam_ref[...] = jnp.zeros_like(dlam_ref)
            dd_ref[...] = jnp.zeros_like(dd_ref)

        @pl.when(last)
        def _():
            carry_b[...] = jnp.zeros_like(carry_b)

        carry_sc[...] = cin_ref[0]
        _to_segments((u0, u1, u2, u3), u_sc, seg, first)
        u = u_sc[...]
        ub = u.astype(BF16)
        for blk in range(S5_BLK):
            s_sc[8:8 + ts, blk * 1024:(blk + 1) * 1024] = _dot(ub[:, blk * 128:(blk + 1) * 128], b_ref[blk], 1, 0)
        _seg_scan(s_sc, 8, seg, lam_ref, tf_ref, carry_sc, False, cseg_ref=s_sc)

        _to_segments((g0, g1, g2, g3), dy_sc, seg)
        _to_segments((y0, y1, y2, y3), y_sc, seg)
        dy = dy_sc[...] * _gelu_grad(y_sc[...])
        dy_sc[...] = dy
        dyb = dy.astype(BF16)
        dd_ref[0:1, :] += jnp.sum(dy * u, axis=0, keepdims=True)
        for blk in range(S5_BLK):
            a_sc[:, blk * 1024:(blk + 1) * 1024] = _dot(dyb[:, blk * 128:(blk + 1) * 128], ct_ref[blk], 1, 0)
            sb = s_sc[8:8 + ts, blk * 1024:(blk + 1) * 1024].astype(BF16)
            dcb_ref[blk] += _dot(sb, dyb[:, blk * 128:(blk + 1) * 128], 0, 0)

        def lam_grad(i, cre, cim, tc, a_r, a_i):
            r0 = pl.multiple_of(i * 8, 8)
            pr = s_sc[pl.ds(r0, 8), cre:cre + 128]
            pi = s_sc[pl.ds(r0, 8), cim:cim + 128]
            dlam_ref[0, :, tc:tc + 128] += a_r * pr + a_i * pi
            dlam_ref[1, :, tc:tc + 128] += a_i * pr - a_r * pi

        _seg_scan(a_sc, 0, seg, lam_ref, tb_ref, carry_b, True, extra=lam_grad)

        for blk in range(S5_BLK):
            ab = a_sc[:, blk * 1024:(blk + 1) * 1024].astype(BF16)
            w_sc[blk] = _dot(ab, bt_ref[blk], 1, 0) + d_ref[:, blk * 128:(blk + 1) * 128] * dy_sc[:, blk * 128:(blk + 1) * 128]
            dbb_ref[blk] += _dot(u_sc[:, blk * 128:(blk + 1) * 128].astype(BF16), ab, 0, 0)
        _from_segments(w_sc, y_sc, seg, zero_head=first)
        du_ref[:, 0:S5_WIDTH] = y_sc[...].astype(BF16)
        du_ref[:, S5_WIDTH:S5_WIDTH + 128] = dif_ref[...]

    const3 = lambda t: (0, 0, 0)
    rev = lambda t: (nblk - 1 - t, 0)

    def lanes(c0):
        return [pl.BlockSpec((ts, 128), lambda t, cc=c0 + c: (nblk - 1 - t, cc)) for c in range(4)]

    return _ride_call(
        body, "s5_bwd", (nblk,),
        lanes(4 * ucol) + lanes(0) + lanes(0) + [
            pl.BlockSpec((1, 8, S5_COLS), lambda t: (nblk - 1 - t, 0, 0)),
            pl.BlockSpec((S5_BLK, 128, 1024), const3),
            pl.BlockSpec((S5_BLK, 1024, 128), const3),
            pl.BlockSpec((S5_BLK, 1024, 128), const3),
            pl.BlockSpec((S5_BLK, 128, 1024), const3),
            pl.BlockSpec((2, 8, 2048), const3),
            pl.BlockSpec((8, 8, 2048), const3),
            pl.BlockSpec((8, 8, 2048), const3),
            pl.BlockSpec((1, S5_WIDTH), lambda t: (0, 0)),
            pl.BlockSpec((ts, 128), rev),
            pl.BlockSpec(memory_space=pl.ANY)],
        [pl.BlockSpec((ts, S5_WIDTH + 128), lambda t: (nblk - 1 - t, P_U // (S5_WIDTH + 128))),
         pl.BlockSpec((S5_BLK, 128, 1024), const3),
         pl.BlockSpec((S5_BLK, 1024, 128), const3),
         pl.BlockSpec((2, 8, 2048), const3),
         pl.BlockSpec((8, S5_WIDTH), lambda t: (0, 0))],
        [jax.ShapeDtypeStruct((T, PW), BF16),
         jax.ShapeDtypeStruct((S5_BLK, 128, 1024), F32),
         jax.ShapeDtypeStruct((S5_BLK, 1024, 128), F32),
         jax.ShapeDtypeStruct((2, 8, 2048), F32),
         jax.ShapeDtypeStruct((8, S5_WIDTH), F32)],
        [pltpu.VMEM((ts + 8, S5_COLS), F32), pltpu.VMEM((ts, S5_COLS), F32),
         pltpu.VMEM((ts, S5_WIDTH), F32), pltpu.VMEM((ts, S5_WIDTH), F32),
         pltpu.VMEM((ts, S5_WIDTH), F32), pltpu.VMEM((S5_BLK, ts, 128), F32),
         pltpu.VMEM((8, S5_COLS), F32), pltpu.VMEM((8, S5_COLS), F32)],
        (p, p, p, p, y_pre, y_pre, y_pre, y_pre, dgy, dgy, dgy, dgy, cin,
         bblk.astype(BF16), jnp.swapaxes(bblk, 1, 2).astype(BF16),
         cblk.astype(BF16), jnp.swapaxes(cblk, 1, 2).astype(BF16), lam8, tab_f, tab_b, dskip, dif, dp),
        comm=comm, aliases={22: 0})


def _row_in_seq(i, tr, lp):
    rowid = lax.broadcasted_iota(jnp.int32, (tr, 1), 0)
    return (i * tr + rowid) % lp


def _conv_fwd(p, w, b, lp):
    T = p.shape[0]
    tr = _pick(T, (320, 256, 128, 64))
    c = P_QK // 1024

    def body(x_ref, xp_ref, w_ref, b_ref, o_ref):
        i = pl.program_id(0)
        pos = _row_in_seq(i, tr, lp)
        x = jnp.where(pos < PAD_ROWS, 0.0, x_ref[...])
        pos_p = (i * tr - 8 + lax.broadcasted_iota(jnp.int32, (8, 1), 0)) % lp
        xp = jnp.where((pos_p < PAD_ROWS) | (i == 0), 0.0, xp_ref[...])
        xx = jnp.concatenate([xp, x], axis=0)
        acc = b_ref[...] + w_ref[3:4, :] * x
        for s in (1, 2, 3):
            acc = acc + w_ref[3 - s:4 - s, :] * pltpu.roll(xx, s, 0)[8:8 + tr]
        o_ref[...] = acc * _sigmoid(acc)

    return pl.pallas_call(
        body, name="conv_fwd", grid=(T // tr,),
        in_specs=[pl.BlockSpec((tr, 1024), lambda i: (i, c)),
                  pl.BlockSpec((8, 1024), lambda i: (jnp.maximum(i * (tr // 8) - 1, 0), c)),
                  pl.BlockSpec((4, 1024), lambda i: (0, 0)),
                  pl.BlockSpec((1, 1024), lambda i: (0, 0))],
        out_specs=pl.BlockSpec((tr, 1024), lambda i: (i, 0)),
        out_shape=jax.ShapeDtypeStruct((T, 1024), F32),
        compiler_params=_cparams(("arbitrary",)),
    )(p, p, w, b)


def _conv_bwd(p, dqk, w, b, dp, lp):
    T = p.shape[0]
    tr = _pick(T, (320, 256, 128, 64))
    c = P_QK // 1024
    nb = T // tr

    def body(x_ref, xp_ref, xn_ref, g_ref, gn_ref, w_ref, b_ref, dpin_ref, dx_ref, dw_ref, db_ref):
        i = pl.program_id(0)

        @pl.when(i == 0)
        def _():
            dw_ref[...] = jnp.zeros_like(dw_ref)
            db_ref[...] = jnp.zeros_like(db_ref)

        def seqpos(off, n):
            return (i * tr + off + lax.broadcasted_iota(jnp.int32, (n, 1), 0)) % lp

        x = jnp.where(seqpos(0, tr) < PAD_ROWS, 0.0, x_ref[...])
        xp = jnp.where((seqpos(-8, 8) < PAD_ROWS) | (i == 0), 0.0, xp_ref[...])
        xn = jnp.where((seqpos(tr, 8) < PAD_ROWS) | (i == nb - 1), 0.0, xn_ref[...])
        xx = jnp.concatenate([xp, x, xn], axis=0)
        gg = jnp.concatenate([g_ref[...], gn_ref[...]], axis=0)
        n2 = tr + 8
        acc = b_ref[...] + w_ref[3:4, :] * xx[8:8 + n2]
        for s in (1, 2, 3):
            acc = acc + w_ref[3 - s:4 - s, :] * pltpu.roll(xx, s, 0)[8:8 + n2]
        sg = _sigmoid(acc)
        dpre = gg * (sg * (1.0 + acc * (1.0 - sg)))
        valid = jnp.concatenate(
            [seqpos(0, tr) >= PAD_ROWS, (seqpos(tr, 8) >= PAD_ROWS) & (i < nb - 1)], axis=0)
        dpre = jnp.where(valid, dpre, 0.0)
        d0 = dpre[0:tr]
        dx = w_ref[3:4, :] * d0
        for s in (1, 2, 3):
            dx = dx + w_ref[3 - s:4 - s, :] * pltpu.roll(dpre, n2 - s, 0)[0:tr]
        dx_ref[...] = jnp.where(seqpos(0, tr) < PAD_ROWS, 0.0, dx).astype(BF16)
        db_ref[0:1, :] += jnp.sum(d0, axis=0, keepdims=True)
        dw_ref[3:4, :] += jnp.sum(d0 * x, axis=0, keepdims=True)
        for s in (1, 2, 3):
            xs = pltpu.roll(xx, s, 0)[8:8 + tr]
            dw_ref[3 - s:4 - s, :] += jnp.sum(d0 * xs, axis=0, keepdims=True)

    t8 = tr // 8
    return pl.pallas_call(
        body, name="conv_bwd", grid=(nb,),
        in_specs=[pl.BlockSpec((tr, 1024), lambda i: (i, c)),
                  pl.BlockSpec((8, 1024), lambda i: (jnp.maximum(i * t8 - 1, 0), c)),
                  pl.BlockSpec((8, 1024), lambda i: (jnp.minimum((i + 1) * t8, nb * t8 - 1), c)),
                  pl.BlockSpec((tr, 1024), lambda i: (i, 0)),
                  pl.BlockSpec((8, 1024), lambda i: (jnp.minimum((i + 1) * t8, nb * t8 - 1), 0)),
                  pl.BlockSpec((4, 1024), lambda i: (0, 0)),
                  pl.BlockSpec((1, 1024), lambda i: (0, 0)),
                  pl.BlockSpec(memory_space=pl.ANY)],
        out_specs=(pl.BlockSpec((tr, 1024), lambda i: (i, c)),
                   pl.BlockSpec((8, 1024), lambda i: (0, 0)),
                   pl.BlockSpec((8, 1024), lambda i: (0, 0))),
        out_shape=(jax.ShapeDtypeStruct((T, PW), BF16),
                   jax.ShapeDtypeStruct((8, 1024), F32),
                   jax.ShapeDtypeStruct((8, 1024), F32)),
        input_output_aliases={7: 0},
        compiler_params=_cparams(("arbitrary",)),
    )(p, p, p, dqk, dqk, w, b, dp)


def _split3(x):
    hi = x.astype(BF16)
    r1 = x - hi.astype(F32)
    mid = r1.astype(BF16)
    lo = (r1 - mid.astype(F32)).astype(BF16)
    return hi, mid, lo


def _tri_sum(x, upper):
    r = lax.broadcasted_iota(jnp.int32, (CHUNK, CHUNK), 0)
    c = lax.broadcasted_iota(jnp.int32, (CHUNK, CHUNK), 1)
    tri = jnp.where((r <= c) if upper else (r >= c), 1.0, 0.0).astype(BF16)
    hi, mid, lo = _split3(x)
    return _dot(tri, hi, 1, 0) + _dot(tri, mid, 1, 0) + _dot(tri, lo, 1, 0)


def _lane_col(x, lane):
    l = lax.broadcasted_iota(jnp.int32, x.shape, 1)
    return jnp.sum(jnp.where(l == lane, x, 0.0), axis=1, keepdims=True)


def _to_row(col):
    r = lax.broadcasted_iota(jnp.int32, (CHUNK, CHUNK), 0)
    c = lax.broadcasted_iota(jnp.int32, (CHUNK, CHUNK), 1)
    return jnp.sum(jnp.where(r == c, col, 0.0), axis=0, keepdims=True)


def _to_col(row):
    r = lax.broadcasted_iota(jnp.int32, (CHUNK, CHUNK), 0)
    c = lax.broadcasted_iota(jnp.int32, (CHUNK, CHUNK), 1)
    return jnp.sum(jnp.where(r == c, row, 0.0), axis=1, keepdims=True)


def _log_sigmoid(x):
    return jnp.minimum(x, 0.0) - jnp.log(1.0 + jnp.exp(-jnp.abs(x)))


def _mlstm_gates(ifv, padmask):
    lf = jnp.where(padmask, 0.0, _log_sigmoid(ifv))
    b_all = _tri_sum(lf, False)
    li = jnp.where(padmask, -jnp.inf, ifv)
    return li, b_all


def _mlstm_head_fwd(q, k, v, li_col, b_col, c_st, n_st, m_st):
    r = lax.broadcasted_iota(jnp.int32, (CHUNK, CHUNK), 0)
    c = lax.broadcasted_iota(jnp.int32, (CHUNK, CHUNK), 1)
    rowid = lax.broadcasted_iota(jnp.int32, (CHUNK, 1), 0)
    b_row = _to_row(b_col)
    li_row = _to_row(li_col)
    dmat = jnp.where(r >= c, b_col - b_row + li_row, -jnp.inf)
    m_inter = b_col + m_st
    m_row = jnp.maximum(m_inter, jnp.max(dmat, axis=1, keepdims=True))
    w_intra = jnp.exp(dmat - m_row)
    w_inter = jnp.exp(m_inter - m_row)
    qb, kb, vb = q.astype(BF16), k.astype(BF16), v.astype(BF16)
    qk = _dot(qb, kb, 1, 1)
    s = qk * w_intra
    cb = c_st.astype(BF16)
    qc = _dot(qb, cb, 1, 0)
    qn = jnp.sum(q * n_st, axis=1, keepdims=True)
    num = _dot(s.astype(BF16), vb, 1, 0) + w_inter * qc
    den = jnp.sum(s, axis=1, keepdims=True) + w_inter * qn
    floor = jnp.exp(-m_row)
    rinv = 1.0 / jnp.maximum(jnp.abs(den), floor)
    h = num * rinv
    b_last = jnp.sum(jnp.where(rowid == CHUNK - 1, b_col, 0.0), axis=0, keepdims=True)
    g_col = b_last - b_col + li_col
    m_new = jnp.maximum(b_last + m_st, jnp.max(g_col, axis=0, keepdims=True))
    w_k = jnp.exp(g_col - m_new)
    decay = jnp.exp(b_last + m_st - m_new)
    kw = w_k * k
    c_new = decay * c_st + _dot(kw.astype(BF16), vb, 0, 0)
    n_new = decay * n_st + jnp.sum(kw, axis=0, keepdims=True)
    return dict(h=h, c_new=c_new, n_new=n_new, m_new=m_new, w_intra=w_intra, w_inter=w_inter, s=s,
                qc=qc, qn=qn, den=den, floor=floor, rinv=rinv, w_k=w_k, decay=decay, kw=kw,
                qb=qb, kb=kb, vb=vb, cb=cb)


def _mlstm_fwd(p, qk, lp):
    T = p.shape[0]
    nch = lp // CHUNK
    B = T // lp
    scale = M_DK ** -0.5

    def body(q_ref, k_ref, v_ref, if_ref, h_ref, cst_ref, nm_ref, c_sc, nm_sc):
        ci = pl.program_id(0)

        @pl.when(ci == 0)
        def _():
            c_sc[...] = jnp.zeros_like(c_sc)
            nm_sc[...] = jnp.zeros_like(nm_sc)

        rowid = lax.broadcasted_iota(jnp.int32, (CHUNK, 1), 0)
        padmask = (ci == 0) & (rowid < PAD_ROWS)
        for bb in range(B):
            cst_ref[bb, 0] = c_sc[bb]
            nm_ref[bb, 0] = nm_sc[bb]
            li_all, b_all = _mlstm_gates(if_ref[bb], padmask)
            for hd in range(M_HEADS):
                q = q_ref[bb, :, hd * M_DK:(hd + 1) * M_DK]
                k = k_ref[bb, :, hd * M_DK:(hd + 1) * M_DK] * scale
                v = v_ref[bb, :, hd * M_DV:(hd + 1) * M_DV]
                o = _mlstm_head_fwd(q, k, v, _lane_col(li_all, hd), _lane_col(b_all, M_HEADS + hd),
                                    c_sc[bb, hd * M_DK:(hd + 1) * M_DK, :], nm_sc[bb, hd:hd + 1, :],
                                    nm_sc[bb, M_HEADS + hd:M_HEADS + hd + 1, 0:1])
                h_ref[bb, :, hd * M_DV:(hd + 1) * M_DV] = o["h"]
                c_sc[bb, hd * M_DK:(hd + 1) * M_DK, :] = o["c_new"]
                nm_sc[bb, hd:hd + 1, :] = o["n_new"]
                nm_sc[bb, M_HEADS + hd:M_HEADS + hd + 1, :] = jnp.broadcast_to(o["m_new"], (1, 128))

    qk3, p3 = qk.reshape(B, lp, 1024), p.reshape(B, lp, PW)
    h, cst, nm = pl.pallas_call(
        body, name="mlstm_fwd", grid=(nch,),
        in_specs=[pl.BlockSpec((B, CHUNK, 512), lambda ci: (0, ci, 0)),
                  pl.BlockSpec((B, CHUNK, 512), lambda ci: (0, ci, 1)),
                  pl.BlockSpec((B, CHUNK, 1024), lambda ci: (0, ci, P_V // 1024)),
                  pl.BlockSpec((B, CHUNK, 128), lambda ci: (0, ci, P_IF // 128))],
        out_specs=(pl.BlockSpec((B, CHUNK, 1024), lambda ci: (0, ci, 0)),
                   pl.BlockSpec((B, 1, M_HEADS * M_DK, M_DV), lambda ci: (0, ci, 0, 0)),
                   pl.BlockSpec((B, 1, 8, 128), lambda ci: (0, ci, 0, 0))),
        out_shape=(jax.ShapeDtypeStruct((B, lp, 1024), F32),
                   jax.ShapeDtypeStruct((B, nch, M_HEADS * M_DK, M_DV), F32),
                   jax.ShapeDtypeStruct((B, nch, 8, 128), F32)),
        scratch_shapes=[pltpu.VMEM((B, M_HEADS * M_DK, M_DV), F32), pltpu.VMEM((B, 8, 128), F32)],
        compiler_params=_cparams(("arbitrary",)),
    )(qk3, qk3, p3, p3)
    return h.reshape(T, 1024), cst, nm


def _mlstm_bwd(p, qk, cst, nm, dh, dp, lp, comm=None):
    T = p.shape[0]
    nch = lp // CHUNK
    B = T // lp
    scale = M_DK ** -0.5

    def body(q_ref, k_ref, v_ref, if_ref, cst_ref, nm_ref, dh_ref, dpin_ref, dqk_ref, dv_ref, dif_ref, dc_sc, dn_sc):
        t = pl.program_id(0)
        ci = nch - 1 - t

        @pl.when(t == 0)
        def _():
            dc_sc[...] = jnp.zeros_like(dc_sc)
            dn_sc[...] = jnp.zeros_like(dn_sc)

        for bb in range(B):
            one_sequence(bb, ci, q_ref, k_ref, v_ref, if_ref, cst_ref, nm_ref, dh_ref, dqk_ref, dv_ref, dif_ref,
                         dc_sc, dn_sc)

    def one_sequence(bb, ci, q_ref, k_ref, v_ref, if_ref, cst_ref, nm_ref, dh_ref, dqk_ref, dv_ref, dif_ref,
                     dc_sc, dn_sc):
        rowid = lax.broadcasted_iota(jnp.int32, (CHUNK, 1), 0)
        lane = lax.broadcasted_iota(jnp.int32, (CHUNK, 128), 1)
        padmask = (ci == 0) & (rowid < PAD_ROWS)
        ifv = if_ref[bb]
        li_all, b_all = _mlstm_gates(ifv, padmask)
        db_all = jnp.zeros((CHUNK, 128), F32)
        dli_all = jnp.zeros((CHUNK, 128), F32)
        for hd in range(M_HEADS):
            q = q_ref[bb, :, hd * M_DK:(hd + 1) * M_DK]
            k = k_ref[bb, :, hd * M_DK:(hd + 1) * M_DK] * scale
            v = v_ref[bb, :, hd * M_DV:(hd + 1) * M_DV]
            c_st = cst_ref[bb, 0, hd * M_DK:(hd + 1) * M_DK, :]
            n_st = nm_ref[bb, 0, hd:hd + 1, :]
            m_st = nm_ref[bb, 0, M_HEADS + hd:M_HEADS + hd + 1, 0:1]
            o = _mlstm_head_fwd(q, k, v, _lane_col(li_all, hd), _lane_col(b_all, M_HEADS + hd), c_st, n_st, m_st)
            dc_new = dc_sc[bb, hd * M_DK:(hd + 1) * M_DK, :]
            dn_new = dn_sc[bb, hd:hd + 1, :]
            dcb = dc_new.astype(BF16)
            dhh = dh_ref[bb, :, hd * M_DV:(hd + 1) * M_DV]
            dnum = dhh * o["rinv"]
            dhh_h = jnp.sum(dhh * o["h"], axis=1, keepdims=True)
            sgn = jnp.where(o["den"] >= 0.0, 1.0, -1.0)
            dden = jnp.where(jnp.abs(o["den"]) > o["floor"], -dhh_h * o["rinv"] * sgn, 0.0)
            dnb = dnum.astype(BF16)
            ds = _dot(dnb, o["vb"], 1, 1) + dden
            sb = o["s"].astype(BF16)
            kwb = o["kw"].astype(BF16)
            dv = _dot(sb, dnb, 0, 0) + _dot(kwb, dcb, 1, 0)
            dqk_m = (ds * o["w_intra"]).astype(BF16)
            wdn = o["w_inter"] * dnum
            wdd = o["w_inter"] * dden
            dq = _dot(dqk_m, o["kb"], 1, 0) + _dot(wdn.astype(BF16), o["cb"], 1, 1) + wdd * n_st
            vdc = _dot(o["vb"], dcb, 1, 1)
            dk = _dot(dqk_m, o["qb"], 0, 0) + o["w_k"] * (vdc + dn_new)
            dd = ds * o["s"]
            dd_col = _to_col(jnp.sum(dd, axis=0, keepdims=True))
            dmi = o["w_inter"] * (jnp.sum(dnum * o["qc"], axis=1, keepdims=True) + dden * o["qn"])
            dg = o["w_k"] * (jnp.sum(k * vdc, axis=1, keepdims=True) + jnp.sum(k * dn_new, axis=1, keepdims=True))
            d_blast = (o["decay"] * (jnp.sum(jnp.sum(dc_new * c_st, axis=1, keepdims=True), axis=0, keepdims=True)
                                     + jnp.sum(dn_new * n_st, axis=1, keepdims=True))
                       + jnp.sum(dg, axis=0, keepdims=True))
            db_col = jnp.sum(dd, axis=1, keepdims=True) - dd_col + dmi - dg
            db_col = db_col + jnp.where(rowid == CHUNK - 1, d_blast, 0.0)
            dli_col = dd_col + dg
            db_all = db_all + jnp.where(lane == M_HEADS + hd, db_col, 0.0)
            dli_all = dli_all + jnp.where(lane == hd, dli_col, 0.0)
            dc_sc[bb, hd * M_DK:(hd + 1) * M_DK, :] = o["decay"] * dc_new + _dot(o["qb"], wdn.astype(BF16), 0, 0)
            dn_sc[bb, hd:hd + 1, :] = o["decay"] * dn_new + jnp.sum(q * wdd, axis=0, keepdims=True)
            dqk_ref[bb, :, hd * M_DK:(hd + 1) * M_DK] = dq
            dqk_ref[bb, :, 512 + hd * M_DK:512 + (hd + 1) * M_DK] = dk * scale
            dv_ref[bb, :, hd * M_DV:(hd + 1) * M_DV] = dv.astype(BF16)
        dlf = _tri_sum(db_all, True)
        dif = dli_all + dlf * _sigmoid(-ifv)
        dif_ref[bb] = jnp.where(padmask | (lane >= 2 * M_HEADS), 0.0, dif).astype(BF16)

    def rev(cb):
        return lambda t: (0, nch - 1 - t, cb)

    rev4 = lambda t: (0, nch - 1 - t, 0, 0)
    qk3, p3, dh3 = qk.reshape(B, lp, 1024), p.reshape(B, lp, PW), dh.reshape(B, lp, 1024)
    (dqk, dv, dif), comm_res = _ride_call(
        body, "mlstm_bwd", (nch,),
        [pl.BlockSpec((B, CHUNK, 512), rev(0)),
         pl.BlockSpec((B, CHUNK, 512), rev(1)),
         pl.BlockSpec((B, CHUNK, 1024), rev(P_V // 1024)),
         pl.BlockSpec((B, CHUNK, 128), rev(P_IF // 128)),
         pl.BlockSpec((B, 1, M_HEADS * M_DK, M_DV), rev4),
         pl.BlockSpec((B, 1, 8, 128), rev4),
         pl.BlockSpec((B, CHUNK, 1024), rev(0)),
         pl.BlockSpec(memory_space=pl.ANY)],
        [pl.BlockSpec((B, CHUNK, 1024), rev(0)),
         pl.BlockSpec((B, CHUNK, 1024), rev(P_V // 1024)),
         pl.BlockSpec((B, CHUNK, 128), rev(0))],
        [jax.ShapeDtypeStruct((B, lp, 1024), F32),
         jax.ShapeDtypeStruct((B, lp, PW), BF16),
         jax.ShapeDtypeStruct((B, lp, 128), BF16)],
        [pltpu.VMEM((B, M_HEADS * M_DK, M_DV), F32), pltpu.VMEM((B, 8, 128), F32)],
        (qk3, qk3, p3, p3, cst, nm, dh3, dp.reshape(B, lp, PW)), comm=comm, aliases={7: 1})
    return dqk.reshape(T, 1024), dv.reshape(T, PW), dif.reshape(T, 128), comm_res


def _headnorm_fwd(hm, p, g):
    T = hm.shape[0]
    tr = _pick(T, (320, 256, 128, 64))

    def body(h_ref, o_ref, g_ref, a_ref):
        for hd in range(M_HEADS):
            sl = slice(hd * M_DV, (hd + 1) * M_DV)
            hn = _ln_rows(h_ref[:, sl], g_ref[:, sl], 0.0)
            a_ref[:, sl] = (_sigmoid(o_ref[:, sl]) * hn).astype(BF16)

    return pl.pallas_call(
        body, name="headnorm_fwd", grid=(T // tr,),
        in_specs=[pl.BlockSpec((tr, 1024), lambda i: (i, 0)),
                  pl.BlockSpec((tr, 1024), lambda i: (i, P_O // 1024)),
                  pl.BlockSpec((1, 1024), lambda i: (0, 0))],
        out_specs=pl.BlockSpec((tr, 1024), lambda i: (i, 0)),
        out_shape=jax.ShapeDtypeStruct((T, 1024), BF16),
        compiler_params=_cparams(("arbitrary",)),
    )(hm, p, g)


def _headnorm_bwd(da, hm, p, g, dp):
    T = hm.shape[0]
    tr = _pick(T, (320, 256, 128, 64))

    def body(da_ref, h_ref, o_ref, g_ref, dpin_ref, dh_ref, do_ref, dg_ref):
        @pl.when(pl.program_id(0) == 0)
        def _():
            dg_ref[...] = jnp.zeros_like(dg_ref)

        for hd in range(M_HEADS):
            sl = slice(hd * M_DV, (hd + 1) * M_DV)
            gg = g_ref[:, sl]
            so = _sigmoid(o_ref[:, sl])
            da = da_ref[:, sl]
            dhn = da * so
            dv, xhat = _ln_bwd_rows(dhn, h_ref[:, sl], gg)
            dh_ref[:, sl] = dv
            do_ref[:, sl] = (da * (xhat * gg) * so * (1.0 - so)).astype(BF16)
            dg_ref[0:1, sl] += jnp.sum(dhn * xhat, axis=0, keepdims=True)

    row = pl.BlockSpec((tr, 1024), lambda i: (i, 0))
    return pl.pallas_call(
        body, name="headnorm_bwd", grid=(T // tr,),
        in_specs=[row, row, pl.BlockSpec((tr, 1024), lambda i: (i, P_O // 1024)),
                  pl.BlockSpec((1, 1024), lambda i: (0, 0)), pl.BlockSpec(memory_space=pl.ANY)],
        out_specs=(row, pl.BlockSpec((tr, 1024), lambda i: (i, P_O // 1024)),
                   pl.BlockSpec((8, 1024), lambda i: (0, 0))),
        out_shape=(jax.ShapeDtypeStruct((T, 1024), F32), jax.ShapeDtypeStruct((T, PW), BF16),
                   jax.ShapeDtypeStruct((8, 1024), F32)),
        input_output_aliases={4: 1},
        compiler_params=_cparams(("arbitrary",)),
    )(da, hm, p, g, dp)


def _mix_fwd(z, ym, p):
    T = ym.shape[0]
    tr = _pick(T, (320, 256, 128, 64))

    def body(z1_ref, z2_ref, ym_ref, gs_ref, gm_ref, o_ref):
        ys = z1_ref[...] * _sigmoid(z2_ref[...])
        o_ref[...] = (_sigmoid(gs_ref[...]) * ys + _sigmoid(gm_ref[...]) * ym_ref[...]).astype(BF16)

    def col(cb):
        return pl.BlockSpec((tr, 1024), lambda i: (i, cb))

    return pl.pallas_call(
        body, name="mix_fwd", grid=(T // tr,),
        in_specs=[col(0), col(1), col(0), col(P_GS // 1024), col(P_GM // 1024)],
        out_specs=col(0),
        out_shape=jax.ShapeDtypeStruct((T, 1024), BF16),
        compiler_params=_cparams(("arbitrary",)),
    )(z, z, ym, p, p)


def _mix_bwd(dmix, z, ym, p):
    T = ym.shape[0]
    tr = _pick(T, (320, 256, 128, 64))

    def body(d_ref, z1_ref, z2_ref, ym_ref, gs_ref, gm_ref, dz_ref, dym_ref, dp_ref):
        d = d_ref[...]
        z1 = z1_ref[...]
        s2 = _sigmoid(z2_ref[...])
        ss = _sigmoid(gs_ref[...])
        sm = _sigmoid(gm_ref[...])
        ys = z1 * s2
        dys = d * ss
        dp_ref[:, 0:1024] = (d * ys * ss * (1.0 - ss)).astype(BF16)
        dp_ref[:, 1024:2048] = (d * ym_ref[...] * sm * (1.0 - sm)).astype(BF16)
        dym_ref[...] = (d * sm).astype(BF16)
        dz_ref[:, 0:1024] = (dys * s2).astype(BF16)
        dz_ref[:, 1024:2048] = (dys * z1 * s2 * (1.0 - s2)).astype(BF16)

    def col(cb):
        return pl.BlockSpec((tr, 1024), lambda i: (i, cb))

    o = jax.ShapeDtypeStruct((T, 1024), BF16)
    return pl.pallas_call(
        body, name="mix_bwd", grid=(T // tr,),
        in_specs=[col(0), col(0), col(1), col(0), col(P_GS // 1024), col(P_GM // 1024)],
        out_specs=(pl.BlockSpec((tr, 2048), lambda i: (i, 0)), col(0),
                   pl.BlockSpec((tr, 2048), lambda i: (i, P_GS // 2048))),
        out_shape=(jax.ShapeDtypeStruct((T, 2048), BF16), o, jax.ShapeDtypeStruct((T, PW), BF16)),
        compiler_params=_cparams(("arbitrary",)),
    )(dmix, z, z, ym, p, p)


def _adamw_math(w, g, m, v):
    m2 = ADAM_B1 * m + (1.0 - ADAM_B1) * g
    v2 = ADAM_B2 * v + (1.0 - ADAM_B2) * jnp.square(g)
    m_hat = m2 / (1.0 - ADAM_B1 ** ADAM_STEP)
    v_hat = v2 / (1.0 - ADAM_B2 ** ADAM_STEP)
    delta = -ADAM_LR * (m_hat / (jnp.sqrt(v_hat) + ADAM_EPS) + ADAM_WD * w)
    return delta, m2, v2


def _adamw_big(recv, w, m, v, name):
    R, C = w.shape
    tr, tc = _shard_tile(R, C)
    ns = recv.shape[0]

    def body(r_ref, w_ref, m_ref, v_ref, g_ref, d_ref, m2_ref, v2_ref):
        g = r_ref[0].astype(F32)
        for s in range(1, ns):
            g = g + r_ref[s].astype(F32)
        d, m2, v2 = _adamw_math(w_ref[...], g, m_ref[...], v_ref[...])
        g_ref[...] = g
        d_ref[...] = d
        m2_ref[...] = m2
        v2_ref[...] = v2

    blk = pl.BlockSpec((tr, tc), lambda i, j: (i, j))
    o = jax.ShapeDtypeStruct((R, C), F32)
    return pl.pallas_call(
        body, name=name, grid=(R // tr, C // tc),
        in_specs=[pl.BlockSpec((ns, tr, tc), lambda i, j: (0, i, j)), blk, blk, blk],
        out_specs=(blk,) * 4, out_shape=(o,) * 4,
        compiler_params=_cparams(("arbitrary", "arbitrary")),
    )(recv, w, m, v)


def _adamw_small(gs, ws, ms, vs):
    n = len(ws)
    shapes = [tuple(w.shape) for w in ws]

    def flat2(a):
        return a.reshape(1, -1) if a.ndim == 1 else a.reshape(-1, a.shape[-1])

    def body(*refs):
        g, w, m, v = refs[:n], refs[n:2 * n], refs[2 * n:3 * n], refs[3 * n:4 * n]
        d, m2, v2 = refs[4 * n:5 * n], refs[5 * n:6 * n], refs[6 * n:]
        for i in range(n):
            di, mi, vi = _adamw_math(w[i][...], g[i][...], m[i][...], v[i][...])
            d[i][...] = di
            m2[i][...] = mi
            v2[i][...] = vi

    args = [flat2(a) for a in list(gs) + list(ws) + list(ms) + list(vs)]
    vm = pl.BlockSpec(memory_space=pltpu.VMEM)
    outs = pl.pallas_call(
        body, name="adamw_small",
        in_specs=[vm] * (4 * n), out_specs=tuple([vm] * (3 * n)),
        out_shape=tuple(jax.ShapeDtypeStruct(a.shape, F32) for a in args[:n] * 3),
        compiler_params=pltpu.CompilerParams(vmem_limit_bytes=VMEM_LIMIT),
    )(*args)
    outs = [o.reshape(s) for o, s in zip(outs, shapes * 3)]
    return outs[:n], outs[n:2 * n], outs[2 * n:]


def _pack(arrs):
    parts = []
    for a in arrs:
        f = a.reshape(-1).astype(F32)
        n = -(-f.shape[0] // 1024) * 1024
        parts.append(jnp.pad(f, (0, n - f.shape[0])))
    return jnp.concatenate(parts).reshape(-1, 128)


def _unpack(pack, shapes):
    flat = pack.reshape(-1)
    out, off = [], 0
    for shp in shapes:
        n = math.prod(shp)
        out.append(flat[off:off + n].reshape(shp))
        off += -(-n // 1024) * 1024
    return out


def _cols_from_shards(g):
    return jnp.transpose(g, (1, 0, 2)).reshape(g.shape[1], -1)


def _cols_to_shards(w):
    R = w.shape[0]
    return jnp.transpose(w.reshape(R, N_DEV, -1), (1, 0, 2))


def _nat_to_aligned(w, axis):
    def sl(start, size):
        return lax.slice_in_dim(w, start, start + size, axis=axis)

    pads = [(0, 0)] * w.ndim
    pads[axis] = (0, PW - P_IF - 8)
    return jnp.concatenate([sl(N_V, 1024), sl(N_O, 1024), sl(N_GS, 1024), sl(N_GM, 1024), sl(N_Q, 1024),
                            sl(N_U, 512), jnp.pad(sl(N_I, 8), pads)], axis=axis)


def _aligned_to_nat(w, axis):
    def sl(start, size):
        return lax.slice_in_dim(w, start, start + size, axis=axis)

    return jnp.concatenate([sl(P_U, 512), sl(P_QK, 1024), sl(P_V, 1024), sl(P_O, 1024), sl(P_IF, 8),
                            sl(P_GS, 1024), sl(P_GM, 1024)], axis=axis)


def kernel(x, meta_tokens, ln0_g, ln0_b, w_in, b_in, qk_conv_w, qk_conv_b, s5_lambda_re, s5_lambda_im, s5_log_dt, s5_b_re, s5_b_im, s5_c_re, s5_c_im, s5_d, s5_w_glu, m_norm_g, m_w_out, w_o, ln1_g, ln1_b, w_up, b_up, w_down, ln2_g, ln2_b, loss_target, m_meta_tokens, m_ln0_g, m_ln0_b, m_w_in, m_b_in, m_qk_conv_w, m_qk_conv_b, m_s5_lambda_re, m_s5_lambda_im, m_s5_log_dt, m_s5_b_re, m_s5_b_im, m_s5_c_re, m_s5_c_im, m_s5_d, m_s5_w_glu, m_m_norm_g, m_m_w_out, m_w_o, m_ln1_g, m_ln1_b, m_w_up, m_b_up, m_w_down, m_ln2_g, m_ln2_b, v_meta_tokens, v_ln0_g, v_ln0_b, v_w_in, v_b_in, v_qk_conv_w, v_qk_conv_b, v_s5_lambda_re, v_s5_lambda_im, v_s5_log_dt, v_s5_b_re, v_s5_b_im, v_s5_c_re, v_s5_c_im, v_s5_d, v_s5_w_glu, v_m_norm_g, v_m_w_out, v_w_o, v_ln1_g, v_ln1_b, v_w_up, v_b_up, v_w_down, v_ln2_g, v_ln2_b):
    B, S, D = x.shape
    lp = S + CHUNK
    me = _my_id()

    first = _allgather([w_in[0].T.astype(BF16), meta_tokens, qk_conv_w[0]])
    win_t = _nat_to_aligned(first[0].reshape(IN_NAT, D), 0)
    meta_f = _cols_from_shards(first[1])
    convw_f = _cols_from_shards(first[2])
    later = [a.astype(BF16) for a in (s5_w_glu[0], m_w_out[0], w_o[0])]
    latest = [a.astype(BF16) for a in (w_up[0], w_down[0])]
    b_in_al = _nat_to_aligned(b_in, 1)
    ln0g, ln0b = ln0_g.reshape(1, D), ln0_b.reshape(1, D)

    s5_args = (s5_lambda_re[0], s5_lambda_im[0], s5_log_dt[0], s5_b_re[0], s5_b_im[0], s5_c_re[0], s5_c_im[0])
    (ar, ai, bblk, cblk), s5_vjp = jax.vjp(_s5_prep, *s5_args)
    seg = _s5_tile(lp) // 8
    mu_r, mu_i = ar, ai
    for _ in range(seg - 1):
        mu_r, mu_i = mu_r * ar - mu_i * ai, mu_r * ai + mu_i * ar
    tab_f, tab_b = _scan_tables(mu_r, mu_i)
    lam8 = jnp.stack([jnp.broadcast_to(ar, (8, 2048)), jnp.broadcast_to(ai, (8, 2048))])

    h0, h0b = _ln0_fwd(x, meta_f, ln0g, ln0b)
    p, *gathered = _mm(h0b, win_t, "nt", "mm_in", bias=b_in_al, comm=(_Gather, later))
    wglu_f = _cols_from_shards(gathered[0])
    wmo_f = gathered[1].reshape(1024, 1024)
    wo_f = gathered[2].reshape(1024, 1024)
    (y_pre, gy, cin), gathered2 = _s5_fwd(p, bblk, cblk, lam8, tab_f, s5_d, lp, comm=(_Gather, latest))
    wup_f = _cols_from_shards(gathered2[0])
    wdown_f = gathered2[1].reshape(D_FF, 1024)
    z = _mm(gy, wglu_f, "nn", "mm_glu")
    qk = _conv_fwd(p, convw_f, qk_conv_b, lp)
    hm, cst, nm = _mlstm_fwd(p, qk, lp)
    a_m = _headnorm_fwd(hm, p, m_norm_g)
    ym = _mm(a_m, wmo_f, "nn", "mm_mout")
    mix = _mix_fwd(z, ym, p)
    r1 = _mm(mix, wo_f, "nn", "mm_o")
    pre1, h1, h1b = _ln_res_fwd(h0, r1, ln1_g, ln1_b, "ln1_fwd")
    act = _mm(h1b, wup_f, "nn", "mm_up", bias=b_up, relu2_out=True, out_dtype=BF16)
    ff = _mm(act, wdown_f, "nn", "mm_down")
    loss_acc, dpre2, dpre2b, dg2, db2 = _ln2_loss(h1, ff, ln2_g, ln2_b, loss_target)

    d_up = _mm(dpre2b, wdown_f, "nt", "mm_d_act", sqrt_gate_of=act, out_dtype=BF16)
    g_wdown = _mm(act, dpre2b, "tn", "mm_g_wdown", out_dtype=BF16)
    g_wup, cs_up = _mm(h1b, d_up, "tn", "mm_g_wup", colsum="b", out_dtype=BF16)
    dh1 = _mm(d_up, wup_f, "nt", "mm_d_h1", add=dpre2, add_scale=ALPHA)
    dpre1, dpre1b, dg1, db1 = _ln_bwd(dh1, pre1, ln1_g, "ln1_bwd")
    g_wo = _mm(mix, dpre1b, "tn", "mm_g_wo", out_dtype=BF16)
    dmix = _mm(dpre1b, wo_f, "nt", "mm_d_mix")
    dz, dym, dp = _mix_bwd(dmix, z, ym, p)
    g_wmo = _mm(a_m, dym, "tn", "mm_g_wmo", out_dtype=BF16)
    da = _mm(dym, wmo_f, "nt", "mm_d_a")
    dhm, dp, dg_norm = _headnorm_bwd(da, hm, p, m_norm_g, dp)
    early = [g_wdown.reshape(N_DEV, 512, 1024), _cols_to_shards(g_wup), g_wo.reshape(N_DEV, 128, 1024),
             g_wmo.reshape(N_DEV, 128, 1024)]
    dqk, dp, dif, recv_early = _mlstm_bwd(p, qk, cst, nm, dhm, dp, lp, comm=(_AllToAll, early))
    dp, dconv_w, dconv_b = _conv_bwd(p, dqk, convw_f, qk_conv_b, dp, lp)
    g_wglu = _mm(gy, dz, "tn", "mm_g_wglu", out_dtype=BF16)
    dgy = _mm(dz, wglu_f, "nt", "mm_d_gy")
    (dp, dbblk, dcblk, dlam, dd), recv_glu = _s5_bwd(p, y_pre, dgy, cin, bblk, cblk, lam8, tab_f, tab_b, s5_d, dif, dp,
                                                     lp, comm=(_AllToAll, [_cols_to_shards(g_wglu)]))
    g_win_t, cs_in = _mm(dp, h0b, "tn", "mm_g_win", colsum="a")
    g_win8 = _aligned_to_nat(g_win_t, 0).astype(BF16).reshape(N_DEV, IN_NAT // N_DEV, D)
    dh0, recv_win = _mm(dp, win_t, "nn", "mm_d_h0", add=dpre1, add_scale=ALPHA, comm=(_AllToAll, [g_win8]))
    grad_x, dmeta, dg0, db0 = _ln0_bwd(dh0, x, meta_f, ln0g)

    dlam2 = jnp.sum(dlam, axis=1)
    s5_grads = s5_vjp((dlam2[0:1], dlam2[1:2], dbblk, dcblk))
    small_local = [
        loss_acc[0:1, 0:1], dg0[0:1], db0[0:1], _aligned_to_nat(cs_in[0:1], 1), dconv_b[0:1],
        s5_grads[0], s5_grads[1], s5_grads[2], s5_grads[3], s5_grads[4], s5_grads[5], s5_grads[6],
        dd[0:1], dg_norm[0:1], dg1[0:1], db1[0:1], cs_up[0:1], dg2[0:1], db2[0:1],
        dmeta, dconv_w[0:4]]
    small_shapes = [(), (D,), (D,), (1, IN_NAT), (1, 1024),
                    (1, 32, 64), (1, 32, 64), (1, 32), (1, 32, 64, 16), (1, 32, 64, 16), (1, 32, 16, 64), (1, 32, 16, 64),
                    (1, 512), (1, 1024), (1, 1024), (1, 1024), (1, D_FF), (1, 1024), (1, 1024),
                    (N_META, D), (4, 1024)]
    red = _unpack(_allreduce_small(_pack(small_local)), small_shapes)
    loss = red[0]
    g_meta = lax.dynamic_slice_in_dim(red[19], me * 128, 128, axis=1)
    g_convw = lax.dynamic_slice_in_dim(red[20], me * 128, 128, axis=1)[None]
    small_g = red[1:19] + [g_meta, g_convw]
    small_w = [ln0_g, ln0_b, b_in, qk_conv_b, s5_lambda_re, s5_lambda_im, s5_log_dt, s5_b_re, s5_b_im,
               s5_c_re, s5_c_im, s5_d, m_norm_g, ln1_g, ln1_b, b_up, ln2_g, ln2_b, meta_tokens, qk_conv_w]
    small_m = [m_ln0_g, m_ln0_b, m_b_in, m_qk_conv_b, m_s5_lambda_re, m_s5_lambda_im, m_s5_log_dt, m_s5_b_re,
               m_s5_b_im, m_s5_c_re, m_s5_c_im, m_s5_d, m_m_norm_g, m_ln1_g, m_ln1_b, m_b_up, m_ln2_g, m_ln2_b,
               m_meta_tokens, m_qk_conv_w]
    small_v = [v_ln0_g, v_ln0_b, v_b_in, v_qk_conv_b, v_s5_lambda_re, v_s5_lambda_im, v_s5_log_dt, v_s5_b_re,
               v_s5_b_im, v_s5_c_re, v_s5_c_im, v_s5_d, v_m_norm_g, v_ln1_g, v_ln1_b, v_b_up, v_ln2_g, v_ln2_b,
               v_meta_tokens, v_qk_conv_w]
    shapes_w = [tuple(w.shape) for w in small_w]
    small_g = [g.reshape(s) for g, s in zip(small_g, shapes_w)]
    sd, sm2, sv2 = _adamw_small(small_g, small_w, small_m, small_v)

    names = ["w_in", "s5_w_glu", "m_w_out", "w_o", "w_up", "w_down"]
    recv = [recv_win, recv_glu[0], recv_early[3], recv_early[2], recv_early[1], recv_early[0]]
    big_m = [m_w_in[0].T, m_s5_w_glu[0], m_m_w_out[0], m_w_o[0], m_w_up[0], m_w_down[0]]
    big_v = [v_w_in[0].T, v_s5_w_glu[0], v_m_w_out[0], v_w_o[0], v_w_up[0], v_w_down[0]]
    big_w = [w_in[0].T, s5_w_glu[0], m_w_out[0], w_o[0], w_up[0], w_down[0]]
    big_out = [_adamw_big(r, w, m, v, "adamw_" + nm_) for r, w, m, v, nm_ in zip(recv, big_w, big_m, big_v, names)]
    big_out[0] = [o.T for o in big_out[0]]

    order = ["meta_tokens", "ln0_g", "ln0_b", "w_in", "b_in", "qk_conv_w", "qk_conv_b", "s5_lambda_re", "s5_lambda_im",
             "s5_log_dt", "s5_b_re", "s5_b_im", "s5_c_re", "s5_c_im", "s5_d", "s5_w_glu", "m_norm_g", "m_w_out", "w_o",
             "ln1_g", "ln1_b", "w_up", "b_up", "w_down", "ln2_g", "ln2_b"]
    small_names = ["ln0_g", "ln0_b", "b_in", "qk_conv_b", "s5_lambda_re", "s5_lambda_im", "s5_log_dt", "s5_b_re",
                   "s5_b_im", "s5_c_re", "s5_c_im", "s5_d", "m_norm_g", "ln1_g", "ln1_b", "b_up", "ln2_g", "ln2_b",
                   "meta_tokens", "qk_conv_w"]
    res = {}
    for i, n in enumerate(small_names):
        res[n] = (small_g[i], sd[i], sm2[i], sv2[i])
    for i, n in enumerate(names):
        res[n] = tuple(o[None] for o in big_out[i])
    outs = [loss, grad_x]
    for kind in range(4):
        outs += [res[n][kind] for n in order]
    return tuple(outs)
```

```python
import functools
import math

import jax
import jax.numpy as jnp
from jax import lax
from jax.experimental import pallas as pl
from jax.experimental.pallas import tpu as pltpu

F32 = jnp.float32
BF16 = jnp.bfloat16

D_MODEL = 1024
N_META = 16
CHUNK = 128
PAD_ROWS = CHUNK - N_META
S5_WIDTH = 512
S5_GROUP = 16
S5_GROUPS = 32
S5_STATE = 64
S5_COLS = 2 * S5_GROUPS * S5_STATE
S5_BLK = 4
M_HEADS = 4
M_DK = 128
M_DV = 256
D_FF = 4096
N_DEV = 8
ALPHA = 2.0 ** 0.25
LN_EPS = 1e-5
IN_NAT = 5640
P_V, P_O, P_GS, P_GM, P_QK, P_U, P_IF, PW = 0, 1024, 2048, 3072, 4096, 5120, 5632, 5760
N_U, N_Q, N_K, N_V, N_O, N_I, N_GS, N_GM = 0, 512, 1024, 1536, 2560, 3584, 3592, 4616

ADAM_LR, ADAM_B1, ADAM_B2, ADAM_EPS, ADAM_WD, ADAM_STEP = 0.001, 0.9, 0.999, 1e-08, 0.01, 10

VMEM_LIMIT = 56 * 1024 * 1024
MESH = pl.DeviceIdType.MESH


def _pick(n, cands):
    for c in cands:
        if n % c == 0:
            return c
    raise ValueError(f"no tile for {n} among {cands}")


def _cparams(sem):
    return pltpu.CompilerParams(dimension_semantics=sem, vmem_limit_bytes=VMEM_LIMIT)


def _dot(a, b, ca, cb):
    return lax.dot_general(a, b, (((ca,), (cb,)), ((), ())), preferred_element_type=F32)


def _sigmoid(x):
    return 1.0 / (1.0 + jnp.exp(-x))


def _peer(k):
    x, y, c = lax.axis_index("x"), lax.axis_index("y"), lax.axis_index("c")
    px = 1 - x if k & 4 else x
    py = 1 - y if k & 2 else y
    pc = 1 - c if k & 1 else c
    return (px, py, pc), 4 * px + 2 * py + pc


def _my_id():
    return 4 * lax.axis_index("x") + 2 * lax.axis_index("y") + lax.axis_index("c")


def _hbm_call(body, name, arrs, out_shape, n_remote):
    n = len(arrs)
    return pl.pallas_call(
        body, name=name,
        out_shape=tuple(out_shape),
        in_specs=[pl.BlockSpec(memory_space=pl.ANY)] * n,
        out_specs=tuple([pl.BlockSpec(memory_space=pl.ANY)] * len(out_shape)),
        scratch_shapes=[pltpu.SemaphoreType.DMA((n, n_remote)),
                        pltpu.SemaphoreType.DMA((n, n_remote)),
                        pltpu.SemaphoreType.DMA((n,))],
    )(*arrs)


class _Gather:
    def __init__(self, ins, outs, send_sems, recv_sems, local_sems):
        self.ins, self.outs, self.n = ins, outs, len(ins)
        self.send_sems, self.recv_sems, self.local_sems = send_sems, recv_sems, local_sems
        x, y, c = lax.axis_index("x"), lax.axis_index("y"), lax.axis_index("c")
        self.c = c
        self.me, self.sibling = (x, y, c), (x, y, 1 - c)
        self.chips = [(1 - x, y), (x, 1 - y), (1 - x, 1 - y)]

    def slot(self, a, dev):
        return self.outs[a].at[4 * dev[0] + 2 * dev[1] + dev[2]]

    def copy(self, a, k, block, to, own=False):
        return pltpu.make_async_remote_copy(
            src_ref=self.ins[a] if own else self.slot(a, block), dst_ref=self.slot(a, block),
            send_sem=self.send_sems.at[a, k], recv_sem=self.recv_sems.at[a, k],
            device_id=to, device_id_type=MESH)

    def first_sends(self, a):
        return [self.copy(a, 0, self.me, self.sibling, own=True)] + [
            self.copy(a, 1 + j, self.me, (*chip, self.c), own=True) for j, chip in enumerate(self.chips)]

    def start(self):
        for a in range(self.n):
            pltpu.make_async_copy(self.ins[a], self.slot(a, self.me), self.local_sems.at[a]).start()
            for cp in self.first_sends(a):
                cp.start()

    def finish(self):
        forwards = []
        for j, chip in enumerate(self.chips):
            for a in range(self.n):
                self.copy(a, 1 + j, (*chip, self.c), self.me).wait_recv()
                fwd = self.copy(a, 4 + j, (*chip, self.c), self.sibling)
                fwd.start()
                forwards.append(fwd)
        for a in range(self.n):
            self.copy(a, 0, self.sibling, self.me).wait_recv()
            for j, chip in enumerate(self.chips):
                self.copy(a, 4 + j, (*chip, 1 - self.c), self.me).wait_recv()
        for a in range(self.n):
            for cp in self.first_sends(a):
                cp.wait_send()
            pltpu.make_async_copy(self.ins[a], self.slot(a, self.me), self.local_sems.at[a]).wait()
        for cp in forwards:
            cp.wait_send()

    @staticmethod
    def out_shapes(arrs):
        return [jax.ShapeDtypeStruct((N_DEV,) + tuple(a.shape), a.dtype) for a in arrs]


class _AllToAll:
    def __init__(self, ins, outs, send_sems, recv_sems, local_sems):
        self.ins, self.outs, self.n = ins, outs, len(ins)
        self.send_sems, self.recv_sems, self.local_sems = send_sems, recv_sems, local_sems
        self.me = _my_id()

    def copy(self, a, k, landing):
        peer, pid = _peer(k)
        return pltpu.make_async_remote_copy(
            src_ref=self.ins[a].at[pid], dst_ref=self.outs[a].at[pid if landing else self.me],
            send_sem=self.send_sems.at[a, k - 1], recv_sem=self.recv_sems.at[a, k - 1],
            device_id=peer, device_id_type=MESH)

    def local(self, a):
        return pltpu.make_async_copy(self.ins[a].at[self.me], self.outs[a].at[self.me], self.local_sems.at[a])

    def start(self):
        for a in range(self.n):
            self.local(a).start()
            for k in range(1, N_DEV):
                self.copy(a, k, False).start()

    def finish(self):
        for a in range(self.n):
            for k in range(1, N_DEV):
                self.copy(a, k, True).wait_recv()
        for a in range(self.n):
            for k in range(1, N_DEV):
                self.copy(a, k, False).wait_send()
            self.local(a).wait()

    @staticmethod
    def out_shapes(arrs):
        return [jax.ShapeDtypeStruct(tuple(a.shape), a.dtype) for a in arrs]


def _comm_scratch(n):
    return [pltpu.SemaphoreType.DMA((n, N_DEV - 1)), pltpu.SemaphoreType.DMA((n, N_DEV - 1)),
            pltpu.SemaphoreType.DMA((n,))]


def _ride_call(body, name, grid, in_specs, out_specs, out_shape, scratch, args, comm=None, aliases=None):
    n_in, n_out = len(in_specs), len(out_specs)
    in_specs, out_specs, out_shape = list(in_specs), list(out_specs), list(out_shape)
    scratch, args = list(scratch), list(args)
    kernel_fn = body
    if comm is not None:
        cls, arrs = comm
        n = len(arrs)

        def kernel_fn(*refs):
            ins, cin = refs[:n_in], refs[n_in:n_in + n]
            outs = refs[n_in + n:n_in + n + n_out]
            cout = refs[n_in + n + n_out:n_in + 2 * n + n_out]
            own_scratch, sems = refs[n_in + 2 * n + n_out:-3], refs[-3:]
            exchange = cls(cin, cout, *sems)
            ids = [pl.program_id(d) for d in range(len(grid))]
            first = functools.reduce(lambda a, b: a & b, [i == 0 for i in ids])
            last = functools.reduce(lambda a, b: a & b, [i == g - 1 for i, g in zip(ids, grid)])

            @pl.when(first)
            def _():
                exchange.start()

            body(*ins, *outs, *own_scratch)

            @pl.when(last)
            def _():
                exchange.finish()

        in_specs += [pl.BlockSpec(memory_space=pl.ANY)] * n
        args += list(arrs)
        out_specs += [pl.BlockSpec(memory_space=pl.ANY)] * n
        out_shape += cls.out_shapes(arrs)
        scratch += _comm_scratch(n)
    res = pl.pallas_call(
        kernel_fn, name=name, grid=grid,
        in_specs=in_specs, out_specs=tuple(out_specs), out_shape=tuple(out_shape),
        scratch_shapes=scratch, input_output_aliases=aliases or {},
        compiler_params=_cparams(("arbitrary",) * len(grid)),
    )(*args)
    return list(res[:n_out]), list(res[n_out:])


def _allgather(arrs):
    n = len(arrs)

    def body(*refs):
        g = _Gather(refs[:n], refs[n:2 * n], *refs[2 * n:])
        g.start()
        g.finish()

    return _hbm_call(body, "allgather_weights", arrs, _Gather.out_shapes(arrs), N_DEV - 1)


def _shard_tile(R, C):
    if R % 128 == 0:
        return 128, C
    return R, _pick(C, (256, 128))


def _allreduce_small(pack):
    rows = pack.shape[0]

    def body(in_ref, out_ref, buf, send_sems, recv_sems):
        me = _my_id()
        sends = []
        for k in range(1, N_DEV):
            peer, pid = _peer(k)
            cp = pltpu.make_async_remote_copy(
                src_ref=in_ref, dst_ref=buf.at[me],
                send_sem=send_sems.at[k - 1], recv_sem=recv_sems.at[k - 1],
                device_id=peer, device_id_type=MESH)
            cp.start()
            sends.append(cp)
        for k in range(1, N_DEV):
            peer, pid = _peer(k)
            pltpu.make_async_remote_copy(
                src_ref=in_ref, dst_ref=buf.at[pid],
                send_sem=send_sems.at[k - 1], recv_sem=recv_sems.at[k - 1],
                device_id=peer, device_id_type=MESH).wait_recv()
        for cp in sends:
            cp.wait_send()
        buf[pl.ds(me, 1)] = in_ref[...][None]
        acc = buf[0]
        for s in range(1, N_DEV):
            acc = acc + buf[s]
        out_ref[...] = acc

    return pl.pallas_call(
        body, name="allreduce_small",
        out_shape=jax.ShapeDtypeStruct((rows, 128), F32),
        in_specs=[pl.BlockSpec(memory_space=pltpu.VMEM)],
        out_specs=pl.BlockSpec(memory_space=pltpu.VMEM),
        scratch_shapes=[pltpu.VMEM((N_DEV, rows, 128), F32),
                        pltpu.SemaphoreType.DMA((N_DEV - 1,)),
                        pltpu.SemaphoreType.DMA((N_DEV - 1,))],
        compiler_params=pltpu.CompilerParams(vmem_limit_bytes=VMEM_LIMIT),
    )(pack)


def _mm_tiles(M, N, K, mode, out_bytes, n_extra_f32, a_bytes, b_bytes):
    budget = 40 * 1024 * 1024
    tms = [t for t in (1664, 1408, 1152, 1024, 640, 512, 256, 128) if M % t == 0]
    tns = [t for t in (1152, 1024, 640, 512, 384, 256, 128) if N % t == 0]
    if mode == "tn":
        tks = [t for t in (1664, 1408, 640, 512, 256, 128) if K % t == 0]
    else:
        tks = [K] if K <= 1152 else [t for t in (1152, 1024, 640, 512) if K % t == 0]
    best = None
    for tm in tms:
        for tn in tns:
            for tk in tks:
                nk = K // tk
                use = 2 * (tm * tk * a_bytes + tk * tn * b_bytes) + 2 * tm * tn * out_bytes
                use += 2 * n_extra_f32 * tm * tn * 4 + tm * tn * 4 * (2 if nk > 1 else 1)
                if use > budget:
                    continue
                score = (tm * tn * tk, tm * tn)
                if best is None or score > best[0]:
                    best = (score, (tm, tn, tk))
    assert best is not None, (M, N, K, mode)
    return best[1]


def _mm(a, b, mode, name, *, bias=None, add=None, add_scale=1.0, sqrt_gate_of=None,
        relu2_out=False, colsum=None, out_dtype=F32, comm=None):
    if mode == "nn":
        (M, K), (K2, N) = a.shape, b.shape
    elif mode == "nt":
        (M, K), (N, K2) = a.shape, b.shape
    else:
        (K, M), (K2, N) = a.shape, b.shape
    assert K == K2, (a.shape, b.shape, mode)
    has_bias, has_add, has_gate = bias is not None, add is not None, sqrt_gate_of is not None
    tm, tn, tk = _mm_tiles(M, N, K, mode, jnp.dtype(out_dtype).itemsize, int(has_add) + int(has_gate),
                           a.dtype.itemsize, b.dtype.itemsize)
    nk = K // tk
    assert colsum is None or mode == "tn"
    assert colsum != "a" or N == tn
    comm_cls, comm_arrs = comm if comm is not None else (None, [])
    nc = len(comm_arrs)
    grid = (N // tn, M // tm, nk)

    def body(*refs):
        it = iter(refs)
        a_ref, b_ref = next(it), next(it)
        bias_ref = next(it) if has_bias else None
        add_ref = next(it) if has_add else None
        gate_ref = next(it) if has_gate else None
        comm_ins = [next(it) for _ in range(nc)]
        o_ref = next(it)
        cs_ref = next(it) if colsum else None
        comm_outs = [next(it) for _ in range(nc)]
        acc_ref = next(it) if nk > 1 else None
        j, i, k = pl.program_id(0), pl.program_id(1), pl.program_id(2)
        if nc:
            exchange = comm_cls(comm_ins, comm_outs, next(it), next(it), next(it))

            @pl.when((j == 0) & (i == 0) & (k == 0))
            def _():
                exchange.start()

        av = a_ref[...].astype(BF16)
        bv = b_ref[...].astype(BF16)
        if mode == "nn":
            part = _dot(av, bv, 1, 0)
        elif mode == "nt":
            part = _dot(av, bv, 1, 1)
        else:
            part = _dot(av, bv, 0, 0)
        if colsum == "b":
            @pl.when((i == 0) & (k == 0))
            def _():
                cs_ref[...] = jnp.zeros_like(cs_ref)

            @pl.when(i == 0)
            def _():
                cs_ref[0:1, :] += jnp.sum(b_ref[...].astype(F32), axis=0, keepdims=True)
        if colsum == "a":
            @pl.when(k == 0)
            def _():
                cs_ref[...] = jnp.zeros_like(cs_ref)

            cs_ref[0:1, :] += jnp.sum(a_ref[...].astype(F32), axis=0, keepdims=True)

        def finish(r):
            if has_bias:
                r = r + bias_ref[...]
            if has_add:
                r = r + add_scale * add_ref[...]
            if has_gate:
                r = r * (2.0 * jnp.sqrt(gate_ref[...].astype(F32)))
            if relu2_out:
                r = jnp.square(jnp.maximum(r, 0.0))
            o_ref[...] = r.astype(out_dtype)

        if nk == 1:
            finish(part)
        else:
            @pl.when(k == 0)
            def _():
                acc_ref[...] = part

            @pl.when(k > 0)
            def _():
                acc_ref[...] += part

            @pl.when(k == nk - 1)
            def _():
                finish(acc_ref[...])

        if nc:
            @pl.when((j == grid[0] - 1) & (i == grid[1] - 1) & (k == nk - 1))
            def _():
                exchange.finish()

    if mode == "nn":
        a_spec = pl.BlockSpec((tm, tk), lambda j, i, k: (i, k))
        b_spec = pl.BlockSpec((tk, tn), lambda j, i, k: (k, j))
    elif mode == "nt":
        a_spec = pl.BlockSpec((tm, tk), lambda j, i, k: (i, k))
        b_spec = pl.BlockSpec((tn, tk), lambda j, i, k: (j, k))
    else:
        a_spec = pl.BlockSpec((tk, tm), lambda j, i, k: (k, i))
        b_spec = pl.BlockSpec((tk, tn), lambda j, i, k: (k, j))
    in_specs, args = [a_spec, b_spec], [a, b]
    if has_bias:
        in_specs.append(pl.BlockSpec((1, tn), lambda j, i, k: (0, j)))
        args.append(bias)
    if has_add:
        in_specs.append(pl.BlockSpec((tm, tn), lambda j, i, k: (i, j)))
        args.append(add)
    if has_gate:
        in_specs.append(pl.BlockSpec((tm, tn), lambda j, i, k: (i, j)))
        args.append(sqrt_gate_of)
    out_shape = [jax.ShapeDtypeStruct((M, N), out_dtype)]
    out_specs = [pl.BlockSpec((tm, tn), lambda j, i, k: (i, j))]
    if colsum == "b":
        out_shape.append(jax.ShapeDtypeStruct((8, N), F32))
        out_specs.append(pl.BlockSpec((8, tn), lambda j, i, k: (0, j)))
    if colsum == "a":
        out_shape.append(jax.ShapeDtypeStruct((8, M), F32))
        out_specs.append(pl.BlockSpec((8, tm), lambda j, i, k: (0, i)))
    scratch = [pltpu.VMEM((tm, tn), F32)] if nk > 1 else []
    if nc:
        in_specs += [pl.BlockSpec(memory_space=pl.ANY)] * nc
        args += list(comm_arrs)
        out_specs += [pl.BlockSpec(memory_space=pl.ANY)] * nc
        out_shape += comm_cls.out_shapes(comm_arrs)
        scratch += _comm_scratch(nc)
    res = pl.pallas_call(
        body, name=name, grid=grid,
        in_specs=in_specs, out_specs=tuple(out_specs), out_shape=tuple(out_shape),
        scratch_shapes=scratch,
        compiler_params=_cparams(("arbitrary", "arbitrary", "arbitrary")),
    )(*args)
    return res if len(res) > 1 else res[0]


def _ln_rows(v, g, b):
    mu = jnp.mean(v, axis=-1, keepdims=True)
    xc = v - mu
    var = jnp.mean(xc * xc, axis=-1, keepdims=True)
    return xc * lax.rsqrt(var + LN_EPS) * g + b


def _ln_bwd_rows(dy, v, g):
    mu = jnp.mean(v, axis=-1, keepdims=True)
    xc = v - mu
    var = jnp.mean(xc * xc, axis=-1, keepdims=True)
    rstd = lax.rsqrt(var + LN_EPS)
    xhat = xc * rstd
    dxh = dy * g
    dv = rstd * (dxh - jnp.mean(dxh, axis=-1, keepdims=True)
                 - xhat * jnp.mean(dxh * xhat, axis=-1, keepdims=True))
    return dv, xhat


def _real_tile(S):
    return _pick(S, (512, 256, 128, 64))


def _real_rows(rb, ncols, lp):
    return pl.BlockSpec((pl.Element(rb), pl.Element(ncols)),
                        lambda bb, j: (pl.multiple_of(bb * lp + CHUNK + j * rb, CHUNK), 0))


def _head_rows(ncols, lp):
    return pl.BlockSpec((CHUNK, ncols), lambda bb: (bb * (lp // CHUNK), 0))


def _ln0_fwd(x, meta, g, b):
    B, S, D = x.shape
    lp = S + CHUNK
    rb = _real_tile(S)

    def body(x_ref, g_ref, b_ref, h_ref, hb_ref):
        y = _ln_rows(x_ref[0], g_ref[...], b_ref[...])
        h_ref[...] = y
        hb_ref[...] = y.astype(BF16)

    vec = pl.BlockSpec((1, D), lambda bb, j: (0, 0))
    h, hb = pl.pallas_call(
        body, name="ln0_fwd", grid=(B, S // rb),
        in_specs=[pl.BlockSpec((1, rb, D), lambda bb, j: (bb, j, 0)), vec, vec],
        out_specs=(_real_rows(rb, D, lp), _real_rows(rb, D, lp)),
        out_shape=(jax.ShapeDtypeStruct((B * lp, D), F32), jax.ShapeDtypeStruct((B * lp, D), BF16)),
        compiler_params=_cparams(("arbitrary", "arbitrary")),
    )(x, g, b)

    def head(meta_ref, g_ref, b_ref, hin_ref, hbin_ref, h_ref, hb_ref):
        m = _ln_rows(meta_ref[...], g_ref[...], b_ref[...])
        h_ref[0:PAD_ROWS, :] = jnp.zeros((PAD_ROWS, D), F32)
        h_ref[PAD_ROWS:CHUNK, :] = m
        hb_ref[0:PAD_ROWS, :] = jnp.zeros((PAD_ROWS, D), BF16)
        hb_ref[PAD_ROWS:CHUNK, :] = m.astype(BF16)

    vec1 = pl.BlockSpec((1, D), lambda bb: (0, 0))
    anyspec = pl.BlockSpec(memory_space=pl.ANY)
    return pl.pallas_call(
        head, name="ln0_head", grid=(B,),
        in_specs=[pl.BlockSpec((N_META, D), lambda bb: (0, 0)), vec1, vec1, anyspec, anyspec],
        out_specs=(_head_rows(D, lp), _head_rows(D, lp)),
        out_shape=(jax.ShapeDtypeStruct((B * lp, D), F32), jax.ShapeDtypeStruct((B * lp, D), BF16)),
        input_output_aliases={3: 0, 4: 1},
        compiler_params=_cparams(("arbitrary",)),
    )(meta, g, b, h, hb)


def _ln0_bwd(dh0, x, meta, g):
    B, S, D = x.shape
    lp = S + CHUNK
    rb = _real_tile(S)

    def body(dh_ref, x_ref, g_ref, dx_ref, dg_ref, db_ref):
        @pl.when((pl.program_id(0) == 0) & (pl.program_id(1) == 0))
        def _():
            dg_ref[...] = jnp.zeros_like(dg_ref)
            db_ref[...] = jnp.zeros_like(db_ref)

        dy = dh_ref[...]
        dv, xhat = _ln_bwd_rows(dy, x_ref[0], g_ref[...])
        dx_ref[0] = dv
        dg_ref[0:1, :] += jnp.sum(dy * xhat, axis=0, keepdims=True)
        db_ref[0:1, :] += jnp.sum(dy, axis=0, keepdims=True)

    const = lambda bb, j: (0, 0)
    xblk = pl.BlockSpec((1, rb, D), lambda bb, j: (bb, j, 0))
    acc_shape = jax.ShapeDtypeStruct((8, D), F32)
    dx, dg, db = pl.pallas_call(
        body, name="ln0_bwd", grid=(B, S // rb),
        in_specs=[_real_rows(rb, D, lp), xblk, pl.BlockSpec((1, D), const)],
        out_specs=(xblk, pl.BlockSpec((8, D), const), pl.BlockSpec((8, D), const)),
        out_shape=(jax.ShapeDtypeStruct((B, S, D), F32), acc_shape, acc_shape),
        compiler_params=_cparams(("arbitrary", "arbitrary")),
    )(dh0, x, g)

    def head(dh_ref, meta_ref, g_ref, dmeta_ref, dg_ref, db_ref):
        @pl.when(pl.program_id(0) == 0)
        def _():
            dmeta_ref[...] = jnp.zeros_like(dmeta_ref)
            dg_ref[...] = jnp.zeros_like(dg_ref)
            db_ref[...] = jnp.zeros_like(db_ref)

        dy = dh_ref[PAD_ROWS:CHUNK, :]
        dv, xhat = _ln_bwd_rows(dy, meta_ref[...], g_ref[...])
        dmeta_ref[...] += dv
        dg_ref[0:1, :] += jnp.sum(dy * xhat, axis=0, keepdims=True)
        db_ref[0:1, :] += jnp.sum(dy, axis=0, keepdims=True)

    c1 = lambda bb: (0, 0)
    dmeta, dgm, dbm = pl.pallas_call(
        head, name="ln0_bwd_head", grid=(B,),
        in_specs=[_head_rows(D, lp), pl.BlockSpec((N_META, D), c1), pl.BlockSpec((1, D), c1)],
        out_specs=(pl.BlockSpec((N_META, D), c1), pl.BlockSpec((8, D), c1), pl.BlockSpec((8, D), c1)),
        out_shape=(jax.ShapeDtypeStruct((N_META, D), F32), acc_shape, acc_shape),
        compiler_params=_cparams(("arbitrary",)),
    )(dh0, meta, g)
    return dx, dmeta, dg + dgm, db + dbm


def _ln_res_fwd(h_prev, r, g, b, name):
    T, D = h_prev.shape
    tr = _pick(T, (384, 320, 256, 128, 64))

    def body(hp_ref, r_ref, g_ref, b_ref, pre_ref, h_ref, hb_ref):
        pre = ALPHA * hp_ref[...] + r_ref[...]
        y = _ln_rows(pre, g_ref[...], b_ref[...])
        pre_ref[...] = pre
        h_ref[...] = y
        hb_ref[...] = y.astype(BF16)

    row = pl.BlockSpec((tr, D), lambda i: (i, 0))
    vec = pl.BlockSpec((1, D), lambda i: (0, 0))
    return pl.pallas_call(
        body, name=name, grid=(T // tr,),
        in_specs=[row, row, vec, vec], out_specs=(row, row, row),
        out_shape=(jax.ShapeDtypeStruct((T, D), F32), jax.ShapeDtypeStruct((T, D), F32),
                   jax.ShapeDtypeStruct((T, D), BF16)),
        compiler_params=_cparams(("arbitrary",)),
    )(h_prev, r, g, b)


def _ln_bwd(dh, pre, g, name):
    T, D = dh.shape
    tr = _pick(T, (384, 320, 256, 128, 64))

    def body(dh_ref, pre_ref, g_ref, dp_ref, dpb_ref, dg_ref, db_ref):
        @pl.when(pl.program_id(0) == 0)
        def _():
            dg_ref[...] = jnp.zeros_like(dg_ref)
            db_ref[...] = jnp.zeros_like(db_ref)

        dy = dh_ref[...]
        dv, xhat = _ln_bwd_rows(dy, pre_ref[...], g_ref[...])
        dp_ref[...] = dv
        dpb_ref[...] = dv.astype(BF16)
        dg_ref[0:1, :] += jnp.sum(dy * xhat, axis=0, keepdims=True)
        db_ref[0:1, :] += jnp.sum(dy, axis=0, keepdims=True)

    row = pl.BlockSpec((tr, D), lambda i: (i, 0))
    vec = pl.BlockSpec((1, D), lambda i: (0, 0))
    acc = pl.BlockSpec((8, D), lambda i: (0, 0))
    return pl.pallas_call(
        body, name=name, grid=(T // tr,),
        in_specs=[row, row, vec], out_specs=(row, row, acc, acc),
        out_shape=(jax.ShapeDtypeStruct((T, D), F32), jax.ShapeDtypeStruct((T, D), BF16),
                   jax.ShapeDtypeStruct((8, D), F32), jax.ShapeDtypeStruct((8, D), F32)),
        compiler_params=_cparams(("arbitrary",)),
    )(dh, pre, g)


def _ln2_loss(h1, ff, g, b, target):
    T, D = h1.shape
    B, S, _ = target.shape
    lp = S + CHUNK
    rb = _real_tile(S)

    def body(h_ref, ff_ref, g_ref, b_ref, t_ref, loss_ref, dp_ref, dpb_ref, dg_ref, db_ref):
        @pl.when((pl.program_id(0) == 0) & (pl.program_id(1) == 0))
        def _():
            loss_ref[...] = jnp.zeros_like(loss_ref)
            dg_ref[...] = jnp.zeros_like(dg_ref)
            db_ref[...] = jnp.zeros_like(db_ref)

        pre = ALPHA * h_ref[...] + ff_ref[...]
        gg = g_ref[...]
        y = _ln_rows(pre, gg, b_ref[...])
        err = y - t_ref[0]
        loss_ref[0:1, 0:1] += 0.5 * jnp.sum(jnp.mean(err * err, axis=-1, keepdims=True), axis=0, keepdims=True)
        dy = err * (1.0 / D)
        dv, xhat = _ln_bwd_rows(dy, pre, gg)
        dp_ref[...] = dv
        dpb_ref[...] = dv.astype(BF16)
        dg_ref[0:1, :] += jnp.sum(dy * xhat, axis=0, keepdims=True)
        db_ref[0:1, :] += jnp.sum(dy, axis=0, keepdims=True)

    row = _real_rows(rb, D, lp)
    const = lambda bb, j: (0, 0)
    loss, dp, dpb, dg, db = pl.pallas_call(
        body, name="ln2_loss", grid=(B, S // rb),
        in_specs=[row, row, pl.BlockSpec((1, D), const), pl.BlockSpec((1, D), const),
                  pl.BlockSpec((1, rb, D), lambda bb, j: (bb, j, 0))],
        out_specs=(pl.BlockSpec((8, 128), const), row, row,
                   pl.BlockSpec((8, D), const), pl.BlockSpec((8, D), const)),
        out_shape=(jax.ShapeDtypeStruct((8, 128), F32),
                   jax.ShapeDtypeStruct((T, D), F32), jax.ShapeDtypeStruct((T, D), BF16),
                   jax.ShapeDtypeStruct((8, D), F32), jax.ShapeDtypeStruct((8, D), F32)),
        compiler_params=_cparams(("arbitrary", "arbitrary")),
    )(h1, ff, g, b, target)

    def head(dpin_ref, dpbin_ref, dp_ref, dpb_ref):
        dp_ref[...] = jnp.zeros((CHUNK, D), F32)
        dpb_ref[...] = jnp.zeros((CHUNK, D), BF16)

    anyspec = pl.BlockSpec(memory_space=pl.ANY)
    dp, dpb = pl.pallas_call(
        head, name="ln2_head", grid=(B,),
        in_specs=[anyspec, anyspec],
        out_specs=(_head_rows(D, lp), _head_rows(D, lp)),
        out_shape=(jax.ShapeDtypeStruct((T, D), F32), jax.ShapeDtypeStruct((T, D), BF16)),
        input_output_aliases={0: 0, 1: 1},
        compiler_params=_cparams(("arbitrary",)),
    )(dp, dpb)
    return loss, dp, dpb, dg, db


def _s5_prep(lam_re, lam_im, log_dt, b_re, b_im, c_re, c_im):
    dt = jnp.exp(log_dt)[:, None]
    mag = jnp.exp(lam_re * dt)
    ar = mag * jnp.cos(lam_im * dt)
    ai = mag * jnp.sin(lam_im * dt)
    nr, ni = ar - 1.0, ai
    den = lam_re * lam_re + lam_im * lam_im
    cr = (nr * lam_re + ni * lam_im) / den
    ci = (ni * lam_re - nr * lam_im) / den
    bbr = cr[..., None] * b_re - ci[..., None] * b_im
    bbi = cr[..., None] * b_im + ci[..., None] * b_re
    eye = jnp.eye(8, dtype=F32)
    bb = jnp.stack([bbr, bbi]).reshape(2, S5_BLK, 8, S5_STATE, S5_GROUP)
    bblk = jnp.einsum("rbgph,gj->bghrjp", bb, eye).reshape(S5_BLK, 128, 1024)
    cc = jnp.stack([c_re, -c_im]).reshape(2, S5_BLK, 8, S5_GROUP, S5_STATE)
    cblk = jnp.einsum("rbghp,gj->brjpgh", cc, eye).reshape(S5_BLK, 1024, 128)
    return ar.reshape(1, 2048), ai.reshape(1, 2048), bblk, cblk


def _scan_tables(ar, ai):
    pr, pi = [ar], [ai]
    for _ in range(7):
        pr, pi = pr + [pr[-1] * ar - pi[-1] * ai], pi + [pr[-1] * ai + pi[-1] * ar]
    pw_r = jnp.concatenate(pr, axis=0)
    pw_i = jnp.concatenate(pi, axis=0)
    rev_r = jnp.concatenate(pr[::-1], axis=0)
    rev_i = jnp.concatenate(pi[::-1], axis=0)
    row = jnp.arange(8)[:, None]

    def tables(sign, reverse):
        rows = []
        for n, sh in ((0, 1), (1, 2), (3, 4)):
            mask = (row < 8 - sh) if reverse else (row >= sh)
            rows.append(jnp.where(mask, pw_r[n][None, :], 0.0))
            rows.append(jnp.where(mask, sign * pw_i[n][None, :], 0.0))
        cr_ = rev_r if reverse else pw_r
        ci_ = rev_i if reverse else pw_i
        rows += [cr_, sign * ci_]
        return jnp.stack(rows)

    return tables(1.0, False), tables(-1.0, True)


def _seg_scan(s_ref, row0, seg, lam_ref, tab_ref, carry_ref, reverse, cseg_ref=None, extra=None):
    sgn = -1.0 if reverse else 1.0
    take, edge = (0, 7) if reverse else (7, 0)
    rowid = lax.broadcasted_iota(jnp.int32, (8, 128), 0)
    all_pairs = [(blk * 1024 + j * 128, blk * 1024 + j * 128 + 512, blk * 512 + j * 128)
                 for blk in range(S5_BLK) for j in range(4)]

    def rows(it):
        i = (seg - 1 - it) if reverse else it
        return i, pl.multiple_of(row0 + i * 8, 8)

    for half in range(2):
        pairs = all_pairs[8 * half:8 * half + 8]

        def pass1(it, carry):
            _, r0 = rows(it)
            out = []
            for n, (cre, cim, tc) in enumerate(pairs):
                lr = lam_ref[0, :, tc:tc + 128]
                li = sgn * lam_ref[1, :, tc:tc + 128]
                pr, pi = carry[2 * n], carry[2 * n + 1]
                xr = lr * pr - li * pi + s_ref[pl.ds(r0, 8), cre:cre + 128]
                xi = lr * pi + li * pr + s_ref[pl.ds(r0, 8), cim:cim + 128]
                s_ref[pl.ds(r0, 8), cre:cre + 128] = xr
                s_ref[pl.ds(r0, 8), cim:cim + 128] = xi
                out += [xr, xi]
            return tuple(out)

        ends = lax.fori_loop(0, seg, pass1, tuple(jnp.zeros((8, 128), F32) for _ in range(16)))

        start = []
        for n, (cre, cim, tc) in enumerate(pairs):
            xr, xi = ends[2 * n], ends[2 * n + 1]
            for lvl, sh in enumerate((1, 2, 4)):
                lr = tab_ref[2 * lvl, :, tc:tc + 128]
                li = tab_ref[2 * lvl + 1, :, tc:tc + 128]
                shift = (8 - sh) if reverse else sh
                sr = pltpu.roll(xr, shift, 0)
                si = pltpu.roll(xi, shift, 0)
                xr, xi = xr + lr * sr - li * si, xi + lr * si + li * sr
            pr = tab_ref[6, :, tc:tc + 128]
            pi = tab_ref[7, :, tc:tc + 128]
            c_r = carry_ref[:, cre:cre + 128]
            c_i = carry_ref[:, cim:cim + 128]
            er, ei = xr + pr * c_r - pi * c_i, xi + pr * c_i + pi * c_r
            back = 7 if reverse else 1
            in_r = jnp.where(rowid == edge, c_r, pltpu.roll(er, back, 0))
            in_i = jnp.where(rowid == edge, c_i, pltpu.roll(ei, back, 0))
            nr = jnp.sum(jnp.where(rowid == take, er, 0.0), axis=0, keepdims=True)
            ni = jnp.sum(jnp.where(rowid == take, ei, 0.0), axis=0, keepdims=True)
            carry_ref[:, cre:cre + 128] = jnp.broadcast_to(nr, (8, 128))
            carry_ref[:, cim:cim + 128] = jnp.broadcast_to(ni, (8, 128))
            if cseg_ref is not None:
                cseg_ref[0:8, cre:cre + 128] = in_r
                cseg_ref[0:8, cim:cim + 128] = in_i
            start += [in_r, in_i]

        def pass2(it, carry):
            i, r0 = rows(it)
            out = []
            for n, (cre, cim, tc) in enumerate(pairs):
                lr = lam_ref[0, :, tc:tc + 128]
                li = sgn * lam_ref[1, :, tc:tc + 128]
                dr, di = carry[2 * n], carry[2 * n + 1]
                dr, di = lr * dr - li * di, lr * di + li * dr
                xr = s_ref[pl.ds(r0, 8), cre:cre + 128] + dr
                xi = s_ref[pl.ds(r0, 8), cim:cim + 128] + di
                s_ref[pl.ds(r0, 8), cre:cre + 128] = xr
                s_ref[pl.ds(r0, 8), cim:cim + 128] = xi
                if extra is not None:
                    extra(i, cre, cim, tc, xr, xi)
                out += [dr, di]
            return tuple(out)

        lax.fori_loop(0, seg, pass2, tuple(start))


def _to_segments(src_refs, dst_ref, seg, first=None):
    sub = lax.broadcasted_iota(jnp.int32, (8, 1), 0)
    for i in range(seg):
        for c, src in enumerate(src_refs):
            v = src[pl.ds(i, 8, stride=seg), :]
            if first is not None:
                v = jnp.where(first & (sub * seg + i < PAD_ROWS), 0.0, v)
            dst_ref[8 * i:8 * i + 8, c * 128:(c + 1) * 128] = v


def _from_segments(src_ref, dst_ref, seg, fn=None, zero_head=None):
    m = seg // 8
    for j in range(seg):
        for c in range(src_ref.shape[0]):
            v = src_ref[c, pl.ds(64 * (j % m) + j // m, 8, stride=8), :]
            if zero_head is not None and j < PAD_ROWS // 8:
                v = jnp.where(zero_head, 0.0, v)
            dst_ref[8 * j:8 * j + 8, c * 128:(c + 1) * 128] = v if fn is None else fn(v)


def _gelu(x):
    c = math.sqrt(2.0 / math.pi)
    t = jnp.tanh(c * (x + 0.044715 * x * x * x))
    return 0.5 * x * (1.0 + t)


def _gelu_grad(x):
    c = math.sqrt(2.0 / math.pi)
    t = jnp.tanh(c * (x + 0.044715 * x * x * x))
    return 0.5 * (1.0 + t) + 0.5 * x * (1.0 - t * t) * c * (1.0 + 3.0 * 0.044715 * x * x)


def _s5_tile(lp):
    return _pick(lp, (384, 320, 256, 128, 64))


def _s5_fwd(p, bblk, cblk, lam8, tab_f, dskip, lp, comm=None):
    T = p.shape[0]
    ts = _s5_tile(lp)
    seg = ts // 8
    nblk, per_seq = T // ts, lp // ts
    ucol = P_U // S5_WIDTH

    def body(u0, u1, u2, u3, b_ref, c_ref, lam_ref, tab_ref, d_ref, y_ref, gy_ref, cin_ref, s_sc, u_sc, y_sc, carry_sc):
        r = pl.program_id(0)
        first = (r % per_seq) == 0

        @pl.when(first)
        def _():
            carry_sc[...] = jnp.zeros_like(carry_sc)

        cin_ref[0] = carry_sc[...]
        _to_segments((u0, u1, u2, u3), u_sc, seg, first)
        u = u_sc[...]
        ub = u.astype(BF16)
        for blk in range(S5_BLK):
            s_sc[:, blk * 1024:(blk + 1) * 1024] = _dot(ub[:, blk * 128:(blk + 1) * 128], b_ref[blk], 1, 0)
        _seg_scan(s_sc, 0, seg, lam_ref, tab_ref, carry_sc, False)
        for blk in range(S5_BLK):
            sb = s_sc[:, blk * 1024:(blk + 1) * 1024].astype(BF16)
            y_sc[blk] = _dot(sb, c_ref[blk], 1, 0) + d_ref[:, blk * 128:(blk + 1) * 128] * u[:, blk * 128:(blk + 1) * 128]
        _from_segments(y_sc, y_ref, seg)
        gy_ref[...] = _gelu(y_ref[...]).astype(BF16)

    def ublock(c):
        return pl.BlockSpec((ts, 128), lambda r: (r, 4 * ucol + c))

    return _ride_call(
        body, "s5_fwd", (nblk,),
        [ublock(0), ublock(1), ublock(2), ublock(3),
         pl.BlockSpec((S5_BLK, 128, 1024), lambda r: (0, 0, 0)),
         pl.BlockSpec((S5_BLK, 1024, 128), lambda r: (0, 0, 0)),
         pl.BlockSpec((2, 8, 2048), lambda r: (0, 0, 0)),
         pl.BlockSpec((8, 8, 2048), lambda r: (0, 0, 0)),
         pl.BlockSpec((1, S5_WIDTH), lambda r: (0, 0))],
        [pl.BlockSpec((ts, S5_WIDTH), lambda r: (r, 0)),
         pl.BlockSpec((ts, S5_WIDTH), lambda r: (r, 0)),
         pl.BlockSpec((1, 8, S5_COLS), lambda r: (r, 0, 0))],
        [jax.ShapeDtypeStruct((T, S5_WIDTH), F32),
         jax.ShapeDtypeStruct((T, S5_WIDTH), BF16),
         jax.ShapeDtypeStruct((nblk, 8, S5_COLS), F32)],
        [pltpu.VMEM((ts, S5_COLS), F32), pltpu.VMEM((ts, S5_WIDTH), F32),
         pltpu.VMEM((S5_BLK, ts, 128), F32), pltpu.VMEM((8, S5_COLS), F32)],
        (p, p, p, p, bblk.astype(BF16), cblk.astype(BF16), lam8, tab_f, dskip), comm=comm)


def _s5_bwd(p, y_pre, dgy, cin, bblk, cblk, lam8, tab_f, tab_b, dskip, dif, dp, lp, comm=None):
    T = p.shape[0]
    ts = _s5_tile(lp)
    seg = ts // 8
    nblk, per_seq = T // ts, lp // ts
    ucol = P_U // S5_WIDTH
    assert P_IF == P_U + S5_WIDTH and P_U % (S5_WIDTH + 128) == 0

    def body(u0, u1, u2, u3, y0, y1, y2, y3, g0, g1, g2, g3, cin_ref, b_ref, bt_ref, c_ref, ct_ref, lam_ref, tf_ref,
             tb_ref, d_ref, dif_ref, dpin_ref, du_ref, dbb_ref, dcb_ref, dlam_ref, dd_ref, s_sc, a_sc, u_sc, dy_sc,
             y_sc, w_sc, carry_sc, carry_b):
        t = pl.program_id(0)
        r = nblk - 1 - t
        first = (r % per_seq) == 0
        last = (r % per_seq) == per_seq - 1

        @pl.when(t == 0)
        def _():
            dbb_ref[...] = jnp.zeros_like(dbb_ref)
            dcb_ref[...] = jnp.zeros_like(dcb_ref)
            dlam_ref[...] = jnp.zeros_like(dlam_ref)
            dd_ref[...] = jnp.zeros_like(dd_ref)

        @pl.when(last)
        def _():
            carry_b[...] = jnp.zeros_like(carry_b)

        carry_sc[...] = cin_ref[0]
        _to_segments((u0, u1, u2, u3), u_sc, seg, first)
        u = u_sc[...]
        ub = u.astype(BF16)
        for blk in range(S5_BLK):
            s_sc[8:8 + ts, blk * 1024:(blk + 1) * 1024] = _dot(ub[:, blk * 128:(blk + 1) * 128], b_ref[blk], 1, 0)
        _seg_scan(s_sc, 8, seg, lam_ref, tf_ref, carry_sc, False, cseg_ref=s_sc)

        _to_segments((g0, g1, g2, g3), dy_sc, seg)
        _to_segments((y0, y1, y2, y3), y_sc, seg)
        dy = dy_sc[...] * _gelu_grad(y_sc[...])
        dy_sc[...] = dy
        dyb = dy.astype(BF16)
        dd_ref[0:1, :] += jnp.sum(dy * u, axis=0, keepdims=True)
        for blk in range(S5_BLK):
            a_sc[:, blk * 1024:(blk + 1) * 1024] = _dot(dyb[:, blk * 128:(blk + 1) * 128], ct_ref[blk], 1, 0)
            sb = s_sc[8:8 + ts, blk * 1024:(blk + 1) * 1024].astype(BF16)
            dcb_ref[blk] += _dot(sb, dyb[:, blk * 128:(blk + 1) * 128], 0, 0)

        def lam_grad(i, cre, cim, tc, a_r, a_i):
            r0 = pl.multiple_of(i * 8, 8)
            pr = s_sc[pl.ds(r0, 8), cre:cre + 128]
            pi = s_sc[pl.ds(r0, 8), cim:cim + 128]
            dlam_ref[0, :, tc:tc + 128] += a_r * pr + a_i * pi
            dlam_ref[1, :, tc:tc + 128] += a_i * pr - a_r * pi

        _seg_scan(a_sc, 0, seg, lam_ref, tb_ref, carry_b, True, extra=lam_grad)

        for blk in range(S5_BLK):
            ab = a_sc[:, blk * 1024:(blk + 1) * 1024].astype(BF16)
            w_sc[blk] = _dot(ab, bt_ref[blk], 1, 0) + d_ref[:, blk * 128:(blk + 1) * 128] * dy_sc[:, blk * 128:(blk + 1) * 128]
            dbb_ref[blk] += _dot(u_sc[:, blk * 128:(blk + 1) * 128].astype(BF16), ab, 0, 0)
        _from_segments(w_sc, y_sc, seg, zero_head=first)
        du_ref[:, 0:S5_WIDTH] = y_sc[...].astype(BF16)
        du_ref[:, S5_WIDTH:S5_WIDTH + 128] = dif_ref[...]

    const3 = lambda t: (0, 0, 0)
    rev = lambda t: (nblk - 1 - t, 0)

    def lanes(c0):
        return [pl.BlockSpec((ts, 128), lambda t, cc=c0 + c: (nblk - 1 - t, cc)) for c in range(4)]

    return _ride_call(
        body, "s5_bwd", (nblk,),
        lanes(4 * ucol) + lanes(0) + lanes(0) + [
            pl.BlockSpec((1, 8, S5_COLS), lambda t: (nblk - 1 - t, 0, 0)),
            pl.BlockSpec((S5_BLK, 128, 1024), const3),
            pl.BlockSpec((S5_BLK, 1024, 128), const3),
            pl.BlockSpec((S5_BLK, 1024, 128), const3),
            pl.BlockSpec((S5_BLK, 128, 1024), const3),
            pl.BlockSpec((2, 8, 2048), const3),
            pl.BlockSpec((8, 8, 2048), const3),
            pl.BlockSpec((8, 8, 2048), const3),
            pl.BlockSpec((1, S5_WIDTH), lambda t: (0, 0)),
            pl.BlockSpec((ts, 128), rev),
            pl.BlockSpec(memory_space=pl.ANY)],
        [pl.BlockSpec((ts, S5_WIDTH + 128), lambda t: (nblk - 1 - t, P_U // (S5_WIDTH + 128))),
         pl.BlockSpec((S5_BLK, 128, 1024), const3),
         pl.BlockSpec((S5_BLK, 1024, 128), const3),
         pl.BlockSpec((2, 8, 2048), const3),
         pl.BlockSpec((8, S5_WIDTH), lambda t: (0, 0))],
        [jax.ShapeDtypeStruct((T, PW), BF16),
         jax.ShapeDtypeStruct((S5_BLK, 128, 1024), F32),
         jax.ShapeDtypeStruct((S5_BLK, 1024, 128), F32),
         jax.ShapeDtypeStruct((2, 8, 2048), F32),
         jax.ShapeDtypeStruct((8, S5_WIDTH), F32)],
        [pltpu.VMEM((ts + 8, S5_COLS), F32), pltpu.VMEM((ts, S5_COLS), F32),
         pltpu.VMEM((ts, S5_WIDTH), F32), pltpu.VMEM((ts, S5_WIDTH), F32),
         pltpu.VMEM((ts, S5_WIDTH), F32), pltpu.VMEM((S5_BLK, ts, 128), F32),
         pltpu.VMEM((8, S5_COLS), F32), pltpu.VMEM((8, S5_COLS), F32)],
        (p, p, p, p, y_pre, y_pre, y_pre, y_pre, dgy, dgy, dgy, dgy, cin,
         bblk.astype(BF16), jnp.swapaxes(bblk, 1, 2).astype(BF16),
         cblk.astype(BF16), jnp.swapaxes(cblk, 1, 2).astype(BF16), lam8, tab_f, tab_b, dskip, dif, dp),
        comm=comm, aliases={22: 0})


def _row_in_seq(i, tr, lp):
    rowid = lax.broadcasted_iota(jnp.int32, (tr, 1), 0)
    return (i * tr + rowid) % lp


def _conv_fwd(p, w, b, lp):
    T = p.shape[0]
    tr = _pick(T, (384, 320, 256, 128, 64))
    c = P_QK // 1024

    def body(x_ref, xp_ref, w_ref, b_ref, o_ref):
        i = pl.program_id(0)
        pos = _row_in_seq(i, tr, lp)
        x = jnp.where(pos < PAD_ROWS, 0.0, x_ref[...])
        pos_p = (i * tr - 8 + lax.broadcasted_iota(jnp.int32, (8, 1), 0)) % lp
        xp = jnp.where((pos_p < PAD_ROWS) | (i == 0), 0.0, xp_ref[...])
        xx = jnp.concatenate([xp, x], axis=0)
        acc = b_ref[...] + w_ref[3:4, :] * x
        for s in (1, 2, 3):
            acc = acc + w_ref[3 - s:4 - s, :] * pltpu.roll(xx, s, 0)[8:8 + tr]
        o_ref[...] = acc * _sigmoid(acc)

    return pl.pallas_call(
        body, name="conv_fwd", grid=(T // tr,),
        in_specs=[pl.BlockSpec((tr, 1024), lambda i: (i, c)),
                  pl.BlockSpec((8, 1024), lambda i: (jnp.maximum(i * (tr // 8) - 1, 0), c)),
                  pl.BlockSpec((4, 1024), lambda i: (0, 0)),
                  pl.BlockSpec((1, 1024), lambda i: (0, 0))],
        out_specs=pl.BlockSpec((tr, 1024), lambda i: (i, 0)),
        out_shape=jax.ShapeDtypeStruct((T, 1024), F32),
        compiler_params=_cparams(("arbitrary",)),
    )(p, p, w, b)


def _conv_bwd(p, dqk, w, b, dp, lp):
    T = p.shape[0]
    tr = _pick(T, (384, 320, 256, 128, 64))
    c = P_QK // 1024
    nb = T // tr

    def body(x_ref, xp_ref, xn_ref, g_ref, gn_ref, w_ref, b_ref, dpin_ref, dx_ref, dw_ref, db_ref):
        i = pl.program_id(0)

        @pl.when(i == 0)
        def _():
            dw_ref[...] = jnp.zeros_like(dw_ref)
            db_ref[...] = jnp.zeros_like(db_ref)

        def seqpos(off, n):
            return (i * tr + off + lax.broadcasted_iota(jnp.int32, (n, 1), 0)) % lp

        x = jnp.where(seqpos(0, tr) < PAD_ROWS, 0.0, x_ref[...])
        xp = jnp.where((seqpos(-8, 8) < PAD_ROWS) | (i == 0), 0.0, xp_ref[...])
        xn = jnp.where((seqpos(tr, 8) < PAD_ROWS) | (i == nb - 1), 0.0, xn_ref[...])
        xx = jnp.concatenate([xp, x, xn], axis=0)
        gg = jnp.concatenate([g_ref[...], gn_ref[...]], axis=0)
        n2 = tr + 8
        acc = b_ref[...] + w_ref[3:4, :] * xx[8:8 + n2]
        for s in (1, 2, 3):
            acc = acc + w_ref[3 - s:4 - s, :] * pltpu.roll(xx, s, 0)[8:8 + n2]
        sg = _sigmoid(acc)
        dpre = gg * (sg * (1.0 + acc * (1.0 - sg)))
        valid = jnp.concatenate(
            [seqpos(0, tr) >= PAD_ROWS, (seqpos(tr, 8) >= PAD_ROWS) & (i < nb - 1)], axis=0)
        dpre = jnp.where(valid, dpre, 0.0)
        d0 = dpre[0:tr]
        dx = w_ref[3:4, :] * d0
        for s in (1, 2, 3):
            dx = dx + w_ref[3 - s:4 - s, :] * pltpu.roll(dpre, n2 - s, 0)[0:tr]
        dx_ref[...] = jnp.where(seqpos(0, tr) < PAD_ROWS, 0.0, dx).astype(BF16)
        db_ref[0:1, :] += jnp.sum(d0, axis=0, keepdims=True)
        dw_ref[3:4, :] += jnp.sum(d0 * x, axis=0, keepdims=True)
        for s in (1, 2, 3):
            xs = pltpu.roll(xx, s, 0)[8:8 + tr]
            dw_ref[3 - s:4 - s, :] += jnp.sum(d0 * xs, axis=0, keepdims=True)

    t8 = tr // 8
    return pl.pallas_call(
        body, name="conv_bwd", grid=(nb,),
        in_specs=[pl.BlockSpec((tr, 1024), lambda i: (i, c)),
                  pl.BlockSpec((8, 1024), lambda i: (jnp.maximum(i * t8 - 1, 0), c)),
                  pl.BlockSpec((8, 1024), lambda i: (jnp.minimum((i + 1) * t8, nb * t8 - 1), c)),
                  pl.BlockSpec((tr, 1024), lambda i: (i, 0)),
                  pl.BlockSpec((8, 1024), lambda i: (jnp.minimum((i + 1) * t8, nb * t8 - 1), 0)),
                  pl.BlockSpec((4, 1024), lambda i: (0, 0)),
                  pl.BlockSpec((1, 1024), lambda i: (0, 0)),
                  pl.BlockSpec(memory_space=pl.ANY)],
        out_specs=(pl.BlockSpec((tr, 1024), lambda i: (i, c)),
                   pl.BlockSpec((8, 1024), lambda i: (0, 0)),
                   pl.BlockSpec((8, 1024), lambda i: (0, 0))),
        out_shape=(jax.ShapeDtypeStruct((T, PW), BF16),
                   jax.ShapeDtypeStruct((8, 1024), F32),
                   jax.ShapeDtypeStruct((8, 1024), F32)),
        input_output_aliases={7: 0},
        compiler_params=_cparams(("arbitrary",)),
    )(p, p, p, dqk, dqk, w, b, dp)


def _split3(x):
    hi = x.astype(BF16)
    r1 = x - hi.astype(F32)
    mid = r1.astype(BF16)
    lo = (r1 - mid.astype(F32)).astype(BF16)
    return hi, mid, lo


def _tri_sum(x, upper):
    r = lax.broadcasted_iota(jnp.int32, (CHUNK, CHUNK), 0)
    c = lax.broadcasted_iota(jnp.int32, (CHUNK, CHUNK), 1)
    tri = jnp.where((r <= c) if upper else (r >= c), 1.0, 0.0).astype(BF16)
    hi, mid, lo = _split3(x)
    return _dot(tri, hi, 1, 0) + _dot(tri, mid, 1, 0) + _dot(tri, lo, 1, 0)


def _lane_col(x, lane):
    l = lax.broadcasted_iota(jnp.int32, x.shape, 1)
    return jnp.sum(jnp.where(l == lane, x, 0.0), axis=1, keepdims=True)


def _to_row(col):
    r = lax.broadcasted_iota(jnp.int32, (CHUNK, CHUNK), 0)
    c = lax.broadcasted_iota(jnp.int32, (CHUNK, CHUNK), 1)
    return jnp.sum(jnp.where(r == c, col, 0.0), axis=0, keepdims=True)


def _to_col(row):
    r = lax.broadcasted_iota(jnp.int32, (CHUNK, CHUNK), 0)
    c = lax.broadcasted_iota(jnp.int32, (CHUNK, CHUNK), 1)
    return jnp.sum(jnp.where(r == c, row, 0.0), axis=1, keepdims=True)


def _log_sigmoid(x):
    return jnp.minimum(x, 0.0) - jnp.log(1.0 + jnp.exp(-jnp.abs(x)))


def _mlstm_gates(ifv, padmask):
    lf = jnp.where(padmask, 0.0, _log_sigmoid(ifv))
    b_all = _tri_sum(lf, False)
    li = jnp.where(padmask, -jnp.inf, ifv)
    return li, b_all


def _mlstm_head_fwd(q, k, v, li_col, b_col, c_st, n_st, m_st):
    r = lax.broadcasted_iota(jnp.int32, (CHUNK, CHUNK), 0)
    c = lax.broadcasted_iota(jnp.int32, (CHUNK, CHUNK), 1)
    rowid = lax.broadcasted_iota(jnp.int32, (CHUNK, 1), 0)
    b_row = _to_row(b_col)
    li_row = _to_row(li_col)
    dmat = jnp.where(r >= c, b_col - b_row + li_row, -jnp.inf)
    m_inter = b_col + m_st
    m_row = jnp.maximum(m_inter, jnp.max(dmat, axis=1, keepdims=True))
    w_intra = jnp.exp(dmat - m_row)
    w_inter = jnp.exp(m_inter - m_row)
    qb, kb, vb = q.astype(BF16), k.astype(BF16), v.astype(BF16)
    qk = _dot(qb, kb, 1, 1)
    s = qk * w_intra
    cb = c_st.astype(BF16)
    qc = _dot(qb, cb, 1, 0)
    qn = jnp.sum(q * n_st, axis=1, keepdims=True)
    num = _dot(s.astype(BF16), vb, 1, 0) + w_inter * qc
    den = jnp.sum(s, axis=1, keepdims=True) + w_inter * qn
    floor = jnp.exp(-m_row)
    rinv = 1.0 / jnp.maximum(jnp.abs(den), floor)
    h = num * rinv
    b_last = jnp.sum(jnp.where(rowid == CHUNK - 1, b_col, 0.0), axis=0, keepdims=True)
    g_col = b_last - b_col + li_col
    m_new = jnp.maximum(b_last + m_st, jnp.max(g_col, axis=0, keepdims=True))
    w_k = jnp.exp(g_col - m_new)
    decay = jnp.exp(b_last + m_st - m_new)
    kw = w_k * k
    c_new = decay * c_st + _dot(kw.astype(BF16), vb, 0, 0)
    n_new = decay * n_st + jnp.sum(kw, axis=0, keepdims=True)
    return dict(h=h, c_new=c_new, n_new=n_new, m_new=m_new, w_intra=w_intra, w_inter=w_inter, s=s,
                qc=qc, qn=qn, den=den, floor=floor, rinv=rinv, w_k=w_k, decay=decay, kw=kw,
                qb=qb, kb=kb, vb=vb, cb=cb)


def _mlstm_fwd(p, qk, lp):
    T = p.shape[0]
    nch = lp // CHUNK
    B = T // lp
    scale = M_DK ** -0.5

    def body(q_ref, k_ref, v_ref, if_ref, h_ref, cst_ref, nm_ref, *state):
        c_scs, nm_scs = state[:B * M_HEADS], state[B * M_HEADS:]
        ci = pl.program_id(0)

        @pl.when(ci == 0)
        def _():
            for ref in state:
                ref[...] = jnp.zeros_like(ref)

        rowid = lax.broadcasted_iota(jnp.int32, (CHUNK, 1), 0)
        padmask = (ci == 0) & (rowid < PAD_ROWS)
        for bb in range(B):
            li_all, b_all = _mlstm_gates(if_ref[bb], padmask)
            for hd in range(M_HEADS):
                c_sc, nm_sc = c_scs[bb * M_HEADS + hd], nm_scs[bb * M_HEADS + hd]
                c_st, n_st, m_row = c_sc[...], nm_sc[0:1, :], nm_sc[1:2, :]
                cst_ref[bb, 0, hd * M_DK:(hd + 1) * M_DK, :] = c_st
                nm_ref[bb, 0, hd:hd + 1, :] = n_st
                nm_ref[bb, 0, M_HEADS + hd:M_HEADS + hd + 1, :] = m_row
                q = q_ref[bb, :, hd * M_DK:(hd + 1) * M_DK]
                k = k_ref[bb, :, hd * M_DK:(hd + 1) * M_DK] * scale
                v = v_ref[bb, :, hd * M_DV:(hd + 1) * M_DV]
                o = _mlstm_head_fwd(q, k, v, _lane_col(li_all, hd), _lane_col(b_all, M_HEADS + hd),
                                    c_st, n_st, nm_sc[1:2, 0:1])
                h_ref[bb, :, hd * M_DV:(hd + 1) * M_DV] = o["h"]
                c_sc[...] = o["c_new"]
                nm_sc[0:1, :] = o["n_new"]
                nm_sc[1:2, :] = jnp.broadcast_to(o["m_new"], (1, 128))

    qk3, p3 = qk.reshape(B, lp, 1024), p.reshape(B, lp, PW)
    h, cst, nm = pl.pallas_call(
        body, name="mlstm_fwd", grid=(nch,),
        in_specs=[pl.BlockSpec((B, CHUNK, 512), lambda ci: (0, ci, 0)),
                  pl.BlockSpec((B, CHUNK, 512), lambda ci: (0, ci, 1)),
                  pl.BlockSpec((B, CHUNK, 1024), lambda ci: (0, ci, P_V // 1024)),
                  pl.BlockSpec((B, CHUNK, 128), lambda ci: (0, ci, P_IF // 128))],
        out_specs=(pl.BlockSpec((B, CHUNK, 1024), lambda ci: (0, ci, 0)),
                   pl.BlockSpec((B, 1, M_HEADS * M_DK, M_DV), lambda ci: (0, ci, 0, 0)),
                   pl.BlockSpec((B, 1, 8, 128), lambda ci: (0, ci, 0, 0))),
        out_shape=(jax.ShapeDtypeStruct((B, lp, 1024), F32),
                   jax.ShapeDtypeStruct((B, nch, M_HEADS * M_DK, M_DV), F32),
                   jax.ShapeDtypeStruct((B, nch, 8, 128), F32)),
        scratch_shapes=([pltpu.VMEM((M_DK, M_DV), F32)] * (B * M_HEADS)
                        + [pltpu.VMEM((8, 128), F32)] * (B * M_HEADS)),
        compiler_params=_cparams(("arbitrary",)),
    )(qk3, qk3, p3, p3)
    return h.reshape(T, 1024), cst, nm


def _mlstm_bwd(p, qk, cst, nm, dh, dp, lp, comm=None):
    T = p.shape[0]
    nch = lp // CHUNK
    B = T // lp
    scale = M_DK ** -0.5

    def body(q_ref, k_ref, v_ref, if_ref, cst_ref, nm_ref, dh_ref, dpin_ref, dqk_ref, dv_ref, dif_ref, *state):
        t = pl.program_id(0)
        ci = nch - 1 - t

        @pl.when(t == 0)
        def _():
            for ref in state:
                ref[...] = jnp.zeros_like(ref)

        for bb in range(B):
            one_sequence(bb, ci, q_ref, k_ref, v_ref, if_ref, cst_ref, nm_ref, dh_ref, dqk_ref, dv_ref, dif_ref,
                         state[bb * M_HEADS:(bb + 1) * M_HEADS],
                         state[(B + bb) * M_HEADS:(B + bb + 1) * M_HEADS])

    def one_sequence(bb, ci, q_ref, k_ref, v_ref, if_ref, cst_ref, nm_ref, dh_ref, dqk_ref, dv_ref, dif_ref,
                     dc_scs, dn_scs):
        rowid = lax.broadcasted_iota(jnp.int32, (CHUNK, 1), 0)
        lane = lax.broadcasted_iota(jnp.int32, (CHUNK, 128), 1)
        padmask = (ci == 0) & (rowid < PAD_ROWS)
        ifv = if_ref[bb]
        li_all, b_all = _mlstm_gates(ifv, padmask)
        db_all = jnp.zeros((CHUNK, 128), F32)
        dli_all = jnp.zeros((CHUNK, 128), F32)
        for hd in range(M_HEADS):
            q = q_ref[bb, :, hd * M_DK:(hd + 1) * M_DK]
            k = k_ref[bb, :, hd * M_DK:(hd + 1) * M_DK] * scale
            v = v_ref[bb, :, hd * M_DV:(hd + 1) * M_DV]
            c_st = cst_ref[bb, 0, hd * M_DK:(hd + 1) * M_DK, :]
            n_st = nm_ref[bb, 0, hd:hd + 1, :]
            m_st = nm_ref[bb, 0, M_HEADS + hd:M_HEADS + hd + 1, 0:1]
            o = _mlstm_head_fwd(q, k, v, _lane_col(li_all, hd), _lane_col(b_all, M_HEADS + hd), c_st, n_st, m_st)
            dc_sc, dn_sc = dc_scs[hd], dn_scs[hd]
            dc_new = dc_sc[...]
            dn_new = dn_sc[0:1, :]
            dcb = dc_new.astype(BF16)
            dhh = dh_ref[bb, :, hd * M_DV:(hd + 1) * M_DV]
            dnum = dhh * o["rinv"]
            dhh_h = jnp.sum(dhh * o["h"], axis=1, keepdims=True)
            sgn = jnp.where(o["den"] >= 0.0, 1.0, -1.0)
            dden = jnp.where(jnp.abs(o["den"]) > o["floor"], -dhh_h * o["rinv"] * sgn, 0.0)
            dnb = dnum.astype(BF16)
            ds = _dot(dnb, o["vb"], 1, 1) + dden
            sb = o["s"].astype(BF16)
            kwb = o["kw"].astype(BF16)
            dv = _dot(sb, dnb, 0, 0) + _dot(kwb, dcb, 1, 0)
            dqk_m = (ds * o["w_intra"]).astype(BF16)
            wdn = o["w_inter"] * dnum
            wdd = o["w_inter"] * dden
            dq = _dot(dqk_m, o["kb"], 1, 0) + _dot(wdn.astype(BF16), o["cb"], 1, 1) + wdd * n_st
            vdc = _dot(o["vb"], dcb, 1, 1)
            dk = _dot(dqk_m, o["qb"], 0, 0) + o["w_k"] * (vdc + dn_new)
            dd = ds * o["s"]
            dd_col = _to_col(jnp.sum(dd, axis=0, keepdims=True))
            dmi = o["w_inter"] * (jnp.sum(dnum * o["qc"], axis=1, keepdims=True) + dden * o["qn"])
            dg = o["w_k"] * (jnp.sum(k * vdc, axis=1, keepdims=True) + jnp.sum(k * dn_new, axis=1, keepdims=True))
            d_blast = (o["decay"] * (jnp.sum(jnp.sum(dc_new * c_st, axis=1, keepdims=True), axis=0, keepdims=True)
                                     + jnp.sum(dn_new * n_st, axis=1, keepdims=True))
                       + jnp.sum(dg, axis=0, keepdims=True))
            db_col = jnp.sum(dd, axis=1, keepdims=True) - dd_col + dmi - dg
            db_col = db_col + jnp.where(rowid == CHUNK - 1, d_blast, 0.0)
            dli_col = dd_col + dg
            db_all = db_all + jnp.where(lane == M_HEADS + hd, db_col, 0.0)
            dli_all = dli_all + jnp.where(lane == hd, dli_col, 0.0)
            dc_sc[...] = o["decay"] * dc_new + _dot(o["qb"], wdn.astype(BF16), 0, 0)
            dn_sc[0:1, :] = o["decay"] * dn_new + jnp.sum(q * wdd, axis=0, keepdims=True)
            dqk_ref[bb, :, hd * M_DK:(hd + 1) * M_DK] = dq
            dqk_ref[bb, :, 512 + hd * M_DK:512 + (hd + 1) * M_DK] = dk * scale
            dv_ref[bb, :, hd * M_DV:(hd + 1) * M_DV] = dv.astype(BF16)
        dlf = _tri_sum(db_all, True)
        dif = dli_all + dlf * _sigmoid(-ifv)
        dif_ref[bb] = jnp.where(padmask | (lane >= 2 * M_HEADS), 0.0, dif).astype(BF16)

    def rev(cb):
        return lambda t: (0, nch - 1 - t, cb)

    rev4 = lambda t: (0, nch - 1 - t, 0, 0)
    qk3, p3, dh3 = qk.reshape(B, lp, 1024), p.reshape(B, lp, PW), dh.reshape(B, lp, 1024)
    (dqk, dv, dif), comm_res = _ride_call(
        body, "mlstm_bwd", (nch,),
        [pl.BlockSpec((B, CHUNK, 512), rev(0)),
         pl.BlockSpec((B, CHUNK, 512), rev(1)),
         pl.BlockSpec((B, CHUNK, 1024), rev(P_V // 1024)),
         pl.BlockSpec((B, CHUNK, 128), rev(P_IF // 128)),
         pl.BlockSpec((B, 1, M_HEADS * M_DK, M_DV), rev4),
         pl.BlockSpec((B, 1, 8, 128), rev4),
         pl.BlockSpec((B, CHUNK, 1024), rev(0)),
         pl.BlockSpec(memory_space=pl.ANY)],
        [pl.BlockSpec((B, CHUNK, 1024), rev(0)),
         pl.BlockSpec((B, CHUNK, 1024), rev(P_V // 1024)),
         pl.BlockSpec((B, CHUNK, 128), rev(0))],
        [jax.ShapeDtypeStruct((B, lp, 1024), F32),
         jax.ShapeDtypeStruct((B, lp, PW), BF16),
         jax.ShapeDtypeStruct((B, lp, 128), BF16)],
        [pltpu.VMEM((M_DK, M_DV), F32)] * (B * M_HEADS) + [pltpu.VMEM((8, 128), F32)] * (B * M_HEADS),
        (qk3, qk3, p3, p3, cst, nm, dh3, dp.reshape(B, lp, PW)), comm=comm, aliases={7: 1})
    return dqk.reshape(T, 1024), dv.reshape(T, PW), dif.reshape(T, 128), comm_res


def _headnorm_fwd(hm, p, g):
    T = hm.shape[0]
    tr = _pick(T, (384, 320, 256, 128, 64))

    def body(h_ref, o_ref, g_ref, a_ref):
        for hd in range(M_HEADS):
            sl = slice(hd * M_DV, (hd + 1) * M_DV)
            hn = _ln_rows(h_ref[:, sl], g_ref[:, sl], 0.0)
            a_ref[:, sl] = (_sigmoid(o_ref[:, sl]) * hn).astype(BF16)

    return pl.pallas_call(
        body, name="headnorm_fwd", grid=(T // tr,),
        in_specs=[pl.BlockSpec((tr, 1024), lambda i: (i, 0)),
                  pl.BlockSpec((tr, 1024), lambda i: (i, P_O // 1024)),
                  pl.BlockSpec((1, 1024), lambda i: (0, 0))],
        out_specs=pl.BlockSpec((tr, 1024), lambda i: (i, 0)),
        out_shape=jax.ShapeDtypeStruct((T, 1024), BF16),
        compiler_params=_cparams(("arbitrary",)),
    )(hm, p, g)


def _headnorm_bwd(da, hm, p, g, dp):
    T = hm.shape[0]
    tr = _pick(T, (384, 320, 256, 128, 64))

    def body(da_ref, h_ref, o_ref, g_ref, dpin_ref, dh_ref, do_ref, dg_ref):
        @pl.when(pl.program_id(0) == 0)
        def _():
            dg_ref[...] = jnp.zeros_like(dg_ref)

        for hd in range(M_HEADS):
            sl = slice(hd * M_DV, (hd + 1) * M_DV)
            gg = g_ref[:, sl]
            so = _sigmoid(o_ref[:, sl])
            da = da_ref[:, sl]
            dhn = da * so
            dv, xhat = _ln_bwd_rows(dhn, h_ref[:, sl], gg)
            dh_ref[:, sl] = dv
            do_ref[:, sl] = (da * (xhat * gg) * so * (1.0 - so)).astype(BF16)
            dg_ref[0:1, sl] += jnp.sum(dhn * xhat, axis=0, keepdims=True)

    row = pl.BlockSpec((tr, 1024), lambda i: (i, 0))
    return pl.pallas_call(
        body, name="headnorm_bwd", grid=(T // tr,),
        in_specs=[row, row, pl.BlockSpec((tr, 1024), lambda i: (i, P_O // 1024)),
                  pl.BlockSpec((1, 1024), lambda i: (0, 0)), pl.BlockSpec(memory_space=pl.ANY)],
        out_specs=(row, pl.BlockSpec((tr, 1024), lambda i: (i, P_O // 1024)),
                   pl.BlockSpec((8, 1024), lambda i: (0, 0))),
        out_shape=(jax.ShapeDtypeStruct((T, 1024), F32), jax.ShapeDtypeStruct((T, PW), BF16),
                   jax.ShapeDtypeStruct((8, 1024), F32)),
        input_output_aliases={4: 1},
        compiler_params=_cparams(("arbitrary",)),
    )(da, hm, p, g, dp)


def _mix_fwd(z, ym, p):
    T = ym.shape[0]
    tr = _pick(T, (384, 320, 256, 128, 64))

    def body(z1_ref, z2_ref, ym_ref, gs_ref, gm_ref, o_ref):
        ys = z1_ref[...] * _sigmoid(z2_ref[...])
        o_ref[...] = (_sigmoid(gs_ref[...]) * ys + _sigmoid(gm_ref[...]) * ym_ref[...]).astype(BF16)

    def col(cb):
        return pl.BlockSpec((tr, 1024), lambda i: (i, cb))

    return pl.pallas_call(
        body, name="mix_fwd", grid=(T // tr,),
        in_specs=[col(0), col(1), col(0), col(P_GS // 1024), col(P_GM // 1024)],
        out_specs=col(0),
        out_shape=jax.ShapeDtypeStruct((T, 1024), BF16),
        compiler_params=_cparams(("arbitrary",)),
    )(z, z, ym, p, p)


def _mix_bwd(dmix, z, ym, p):
    T = ym.shape[0]
    tr = _pick(T, (384, 320, 256, 128, 64))

    def body(d_ref, z1_ref, z2_ref, ym_ref, gs_ref, gm_ref, dz_ref, dym_ref, dp_ref):
        d = d_ref[...]
        z1 = z1_ref[...]
        s2 = _sigmoid(z2_ref[...])
        ss = _sigmoid(gs_ref[...])
        sm = _sigmoid(gm_ref[...])
        ys = z1 * s2
        dys = d * ss
        dp_ref[:, 0:1024] = (d * ys * ss * (1.0 - ss)).astype(BF16)
        dp_ref[:, 1024:2048] = (d * ym_ref[...] * sm * (1.0 - sm)).astype(BF16)
        dym_ref[...] = (d * sm).astype(BF16)
        dz_ref[:, 0:1024] = (dys * s2).astype(BF16)
        dz_ref[:, 1024:2048] = (dys * z1 * s2 * (1.0 - s2)).astype(BF16)

    def col(cb):
        return pl.BlockSpec((tr, 1024), lambda i: (i, cb))

    o = jax.ShapeDtypeStruct((T, 1024), BF16)
    return pl.pallas_call(
        body, name="mix_bwd", grid=(T // tr,),
        in_specs=[col(0), col(0), col(1), col(0), col(P_GS // 1024), col(P_GM // 1024)],
        out_specs=(pl.BlockSpec((tr, 2048), lambda i: (i, 0)), col(0),
                   pl.BlockSpec((tr, 2048), lambda i: (i, P_GS // 2048))),
        out_shape=(jax.ShapeDtypeStruct((T, 2048), BF16), o, jax.ShapeDtypeStruct((T, PW), BF16)),
        compiler_params=_cparams(("arbitrary",)),
    )(dmix, z, z, ym, p, p)


def _adamw_math(w, g, m, v):
    m2 = ADAM_B1 * m + (1.0 - ADAM_B1) * g
    v2 = ADAM_B2 * v + (1.0 - ADAM_B2) * jnp.square(g)
    m_hat = m2 / (1.0 - ADAM_B1 ** ADAM_STEP)
    v_hat = v2 / (1.0 - ADAM_B2 ** ADAM_STEP)
    delta = -ADAM_LR * (m_hat / (jnp.sqrt(v_hat) + ADAM_EPS) + ADAM_WD * w)
    return delta, m2, v2


def _adamw_big(recv, w, m, v, name):
    R, C = w.shape
    tr, tc = _shard_tile(R, C)
    ns = recv.shape[0]

    def body(r_ref, w_ref, m_ref, v_ref, g_ref, d_ref, m2_ref, v2_ref):
        g = r_ref[0].astype(F32)
        for s in range(1, ns):
            g = g + r_ref[s].astype(F32)
        d, m2, v2 = _adamw_math(w_ref[...], g, m_ref[...], v_ref[...])
        g_ref[...] = g
        d_ref[...] = d
        m2_ref[...] = m2
        v2_ref[...] = v2

    blk = pl.BlockSpec((tr, tc), lambda i, j: (i, j))
    o = jax.ShapeDtypeStruct((R, C), F32)
    return pl.pallas_call(
        body, name=name, grid=(R // tr, C // tc),
        in_specs=[pl.BlockSpec((ns, tr, tc), lambda i, j: (0, i, j)), blk, blk, blk],
        out_specs=(blk,) * 4, out_shape=(o,) * 4,
        compiler_params=_cparams(("arbitrary", "arbitrary")),
    )(recv, w, m, v)


def _adamw_small(gs, ws, ms, vs):
    n = len(ws)
    shapes = [tuple(w.shape) for w in ws]

    def flat2(a):
        return a.reshape(1, -1) if a.ndim == 1 else a.reshape(-1, a.shape[-1])

    def body(*refs):
        g, w, m, v = refs[:n], refs[n:2 * n], refs[2 * n:3 * n], refs[3 * n:4 * n]
        d, m2, v2 = refs[4 * n:5 * n], refs[5 * n:6 * n], refs[6 * n:]
        for i in range(n):
            di, mi, vi = _adamw_math(w[i][...], g[i][...], m[i][...], v[i][...])
            d[i][...] = di
            m2[i][...] = mi
            v2[i][...] = vi

    args = [flat2(a) for a in list(gs) + list(ws) + list(ms) + list(vs)]
    vm = pl.BlockSpec(memory_space=pltpu.VMEM)
    outs = pl.pallas_call(
        body, name="adamw_small",
        in_specs=[vm] * (4 * n), out_specs=tuple([vm] * (3 * n)),
        out_shape=tuple(jax.ShapeDtypeStruct(a.shape, F32) for a in args[:n] * 3),
        compiler_params=pltpu.CompilerParams(vmem_limit_bytes=VMEM_LIMIT),
    )(*args)
    outs = [o.reshape(s) for o, s in zip(outs, shapes * 3)]
    return outs[:n], outs[n:2 * n], outs[2 * n:]


def _pack(arrs):
    parts = []
    for a in arrs:
        f = a.reshape(-1).astype(F32)
        n = -(-f.shape[0] // 1024) * 1024
        parts.append(jnp.pad(f, (0, n - f.shape[0])))
    return jnp.concatenate(parts).reshape(-1, 128)


def _unpack(pack, shapes):
    flat = pack.reshape(-1)
    out, off = [], 0
    for shp in shapes:
        n = math.prod(shp)
        out.append(flat[off:off + n].reshape(shp))
        off += -(-n // 1024) * 1024
    return out


def _cols_from_shards(g):
    return jnp.transpose(g, (1, 0, 2)).reshape(g.shape[1], -1)


def _cols_to_shards(w):
    R = w.shape[0]
    return jnp.transpose(w.reshape(R, N_DEV, -1), (1, 0, 2))


def _nat_to_aligned(w, axis):
    def sl(start, size):
        return lax.slice_in_dim(w, start, start + size, axis=axis)

    pads = [(0, 0)] * w.ndim
    pads[axis] = (0, PW - P_IF - 8)
    return jnp.concatenate([sl(N_V, 1024), sl(N_O, 1024), sl(N_GS, 1024), sl(N_GM, 1024), sl(N_Q, 1024),
                            sl(N_U, 512), jnp.pad(sl(N_I, 8), pads)], axis=axis)


def _aligned_to_nat(w, axis):
    def sl(start, size):
        return lax.slice_in_dim(w, start, start + size, axis=axis)

    return jnp.concatenate([sl(P_U, 512), sl(P_QK, 1024), sl(P_V, 1024), sl(P_O, 1024), sl(P_IF, 8),
                            sl(P_GS, 1024), sl(P_GM, 1024)], axis=axis)


def kernel(x, meta_tokens, ln0_g, ln0_b, w_in, b_in, qk_conv_w, qk_conv_b, s5_lambda_re, s5_lambda_im, s5_log_dt, s5_b_re, s5_b_im, s5_c_re, s5_c_im, s5_d, s5_w_glu, m_norm_g, m_w_out, w_o, ln1_g, ln1_b, w_up, b_up, w_down, ln2_g, ln2_b, loss_target, m_meta_tokens, m_ln0_g, m_ln0_b, m_w_in, m_b_in, m_qk_conv_w, m_qk_conv_b, m_s5_lambda_re, m_s5_lambda_im, m_s5_log_dt, m_s5_b_re, m_s5_b_im, m_s5_c_re, m_s5_c_im, m_s5_d, m_s5_w_glu, m_m_norm_g, m_m_w_out, m_w_o, m_ln1_g, m_ln1_b, m_w_up, m_b_up, m_w_down, m_ln2_g, m_ln2_b, v_meta_tokens, v_ln0_g, v_ln0_b, v_w_in, v_b_in, v_qk_conv_w, v_qk_conv_b, v_s5_lambda_re, v_s5_lambda_im, v_s5_log_dt, v_s5_b_re, v_s5_b_im, v_s5_c_re, v_s5_c_im, v_s5_d, v_s5_w_glu, v_m_norm_g, v_m_w_out, v_w_o, v_ln1_g, v_ln1_b, v_w_up, v_b_up, v_w_down, v_ln2_g, v_ln2_b):
    B, S, D = x.shape
    lp = S + CHUNK
    me = _my_id()

    first = _allgather([w_in[0].T.astype(BF16), meta_tokens, qk_conv_w[0]])
    win_t = _nat_to_aligned(first[0].reshape(IN_NAT, D), 0)
    meta_f = _cols_from_shards(first[1])
    convw_f = _cols_from_shards(first[2])
    later = [a.astype(BF16) for a in (s5_w_glu[0], m_w_out[0], w_o[0])]
    latest = [a.astype(BF16) for a in (w_up[0], w_down[0])]
    b_in_al = _nat_to_aligned(b_in, 1)
    ln0g, ln0b = ln0_g.reshape(1, D), ln0_b.reshape(1, D)

    s5_args = (s5_lambda_re[0], s5_lambda_im[0], s5_log_dt[0], s5_b_re[0], s5_b_im[0], s5_c_re[0], s5_c_im[0])
    (ar, ai, bblk, cblk), s5_vjp = jax.vjp(_s5_prep, *s5_args)
    seg = _s5_tile(lp) // 8
    mu_r, mu_i = ar, ai
    for _ in range(seg - 1):
        mu_r, mu_i = mu_r * ar - mu_i * ai, mu_r * ai + mu_i * ar
    tab_f, tab_b = _scan_tables(mu_r, mu_i)
    lam8 = jnp.stack([jnp.broadcast_to(ar, (8, 2048)), jnp.broadcast_to(ai, (8, 2048))])

    h0, h0b = _ln0_fwd(x, meta_f, ln0g, ln0b)
    p, *gathered = _mm(h0b, win_t, "nt", "mm_in", bias=b_in_al, comm=(_Gather, later))
    wglu_f = _cols_from_shards(gathered[0])
    wmo_f = gathered[1].reshape(1024, 1024)
    wo_f = gathered[2].reshape(1024, 1024)
    (y_pre, gy, cin), gathered2 = _s5_fwd(p, bblk, cblk, lam8, tab_f, s5_d, lp, comm=(_Gather, latest))
    wup_f = _cols_from_shards(gathered2[0])
    wdown_f = gathered2[1].reshape(D_FF, 1024)
    z = _mm(gy, wglu_f, "nn", "mm_glu")
    qk = _conv_fwd(p, convw_f, qk_conv_b, lp)
    hm, cst, nm = _mlstm_fwd(p, qk, lp)
    a_m = _headnorm_fwd(hm, p, m_norm_g)
    ym = _mm(a_m, wmo_f, "nn", "mm_mout")
    mix = _mix_fwd(z, ym, p)
    r1 = _mm(mix, wo_f, "nn", "mm_o")
    pre1, h1, h1b = _ln_res_fwd(h0, r1, ln1_g, ln1_b, "ln1_fwd")
    act = _mm(h1b, wup_f, "nn", "mm_up", bias=b_up, relu2_out=True, out_dtype=BF16)
    ff = _mm(act, wdown_f, "nn", "mm_down")
    loss_acc, dpre2, dpre2b, dg2, db2 = _ln2_loss(h1, ff, ln2_g, ln2_b, loss_target)

    wdown_t, wup_t = wdown_f.T, jnp.transpose(gathered2[0], (0, 2, 1)).reshape(D_FF, 1024)
    d_up = _mm(dpre2b, wdown_t, "nn", "mm_d_act", sqrt_gate_of=act, out_dtype=BF16)
    g_wdown = _mm(act, dpre2b, "tn", "mm_g_wdown", out_dtype=BF16)
    g_wup, cs_up = _mm(h1b, d_up, "tn", "mm_g_wup", colsum="b", out_dtype=BF16)
    dh1 = _mm(d_up, wup_t, "nn", "mm_d_h1", add=dpre2, add_scale=ALPHA)
    dpre1, dpre1b, dg1, db1 = _ln_bwd(dh1, pre1, ln1_g, "ln1_bwd")
    g_wo = _mm(mix, dpre1b, "tn", "mm_g_wo", out_dtype=BF16)
    dmix = _mm(dpre1b, wo_f, "nt", "mm_d_mix")
    dz, dym, dp = _mix_bwd(dmix, z, ym, p)
    g_wmo = _mm(a_m, dym, "tn", "mm_g_wmo", out_dtype=BF16)
    da = _mm(dym, wmo_f, "nt", "mm_d_a")
    dhm, dp, dg_norm = _headnorm_bwd(da, hm, p, m_norm_g, dp)
    early = [g_wdown.reshape(N_DEV, 512, 1024), _cols_to_shards(g_wup), g_wo.reshape(N_DEV, 128, 1024),
             g_wmo.reshape(N_DEV, 128, 1024)]
    dqk, dp, dif, recv_early = _mlstm_bwd(p, qk, cst, nm, dhm, dp, lp, comm=(_AllToAll, early))
    dp, dconv_w, dconv_b = _conv_bwd(p, dqk, convw_f, qk_conv_b, dp, lp)
    g_wglu = _mm(gy, dz, "tn", "mm_g_wglu", out_dtype=BF16)
    dgy = _mm(dz, wglu_f, "nt", "mm_d_gy")
    (dp, dbblk, dcblk, dlam, dd), recv_glu = _s5_bwd(p, y_pre, dgy, cin, bblk, cblk, lam8, tab_f, tab_b, s5_d, dif, dp,
                                                     lp, comm=(_AllToAll, [_cols_to_shards(g_wglu)]))
    g_win_t, cs_in = _mm(dp, h0b, "tn", "mm_g_win", colsum="a")
    g_win8 = _aligned_to_nat(g_win_t, 0).astype(BF16).reshape(N_DEV, IN_NAT // N_DEV, D)
    dh0, recv_win = _mm(dp, win_t, "nn", "mm_d_h0", add=dpre1, add_scale=ALPHA, comm=(_AllToAll, [g_win8]))
    grad_x, dmeta, dg0, db0 = _ln0_bwd(dh0, x, meta_f, ln0g)

    dlam2 = jnp.sum(dlam, axis=1)
    s5_grads = s5_vjp((dlam2[0:1], dlam2[1:2], dbblk, dcblk))
    small_local = [
        loss_acc[0:1, 0:1], dg0[0:1], db0[0:1], _aligned_to_nat(cs_in[0:1], 1), dconv_b[0:1],
        s5_grads[0], s5_grads[1], s5_grads[2], s5_grads[3], s5_grads[4], s5_grads[5], s5_grads[6],
        dd[0:1], dg_norm[0:1], dg1[0:1], db1[0:1], cs_up[0:1], dg2[0:1], db2[0:1],
        dmeta, dconv_w[0:4]]
    small_shapes = [(), (D,), (D,), (1, IN_NAT), (1, 1024),
                    (1, 32, 64), (1, 32, 64), (1, 32), (1, 32, 64, 16), (1, 32, 64, 16), (1, 32, 16, 64), (1, 32, 16, 64),
                    (1, 512), (1, 1024), (1, 1024), (1, 1024), (1, D_FF), (1, 1024), (1, 1024),
                    (N_META, D), (4, 1024)]
    red = _unpack(_allreduce_small(_pack(small_local)), small_shapes)
    loss = red[0]
    g_meta = lax.dynamic_slice_in_dim(red[19], me * 128, 128, axis=1)
    g_convw = lax.dynamic_slice_in_dim(red[20], me * 128, 128, axis=1)[None]
    small_g = red[1:19] + [g_meta, g_convw]
    small_w = [ln0_g, ln0_b, b_in, qk_conv_b, s5_lambda_re, s5_lambda_im, s5_log_dt, s5_b_re, s5_b_im,
               s5_c_re, s5_c_im, s5_d, m_norm_g, ln1_g, ln1_b, b_up, ln2_g, ln2_b, meta_tokens, qk_conv_w]
    small_m = [m_ln0_g, m_ln0_b, m_b_in, m_qk_conv_b, m_s5_lambda_re, m_s5_lambda_im, m_s5_log_dt, m_s5_b_re,
               m_s5_b_im, m_s5_c_re, m_s5_c_im, m_s5_d, m_m_norm_g, m_ln1_g, m_ln1_b, m_b_up, m_ln2_g, m_ln2_b,
               m_meta_tokens, m_qk_conv_w]
    small_v = [v_ln0_g, v_ln0_b, v_b_in, v_qk_conv_b, v_s5_lambda_re, v_s5_lambda_im, v_s5_log_dt, v_s5_b_re,
               v_s5_b_im, v_s5_c_re, v_s5_c_im, v_s5_d, v_m_norm_g, v_ln1_g, v_ln1_b, v_b_up, v_ln2_g, v_ln2_b,
               v_meta_tokens, v_qk_conv_w]
    shapes_w = [tuple(w.shape) for w in small_w]
    small_g = [g.reshape(s) for g, s in zip(small_g, shapes_w)]
    sd, sm2, sv2 = _adamw_small(small_g, small_w, small_m, small_v)

    names = ["w_in", "s5_w_glu", "m_w_out", "w_o", "w_up", "w_down"]
    recv = [recv_win, recv_glu[0], recv_early[3], recv_early[2], recv_early[1], recv_early[0]]
    big_m = [m_w_in[0].T, m_s5_w_glu[0], m_m_w_out[0], m_w_o[0], m_w_up[0], m_w_down[0]]
    big_v = [v_w_in[0].T, v_s5_w_glu[0], v_m_w_out[0], v_w_o[0], v_w_up[0], v_w_down[0]]
    big_w = [w_in[0].T, s5_w_glu[0], m_w_out[0], w_o[0], w_up[0], w_down[0]]
    big_out = [_adamw_big(r, w, m, v, "adamw_" + nm_) for r, w, m, v, nm_ in zip(recv, big_w, big_m, big_v, names)]
    big_out[0] = [o.T for o in big_out[0]]

    order = ["meta_tokens", "ln0_g", "ln0_b", "w_in", "b_in", "qk_conv_w", "qk_conv_b", "s5_lambda_re", "s5_lambda_im",
             "s5_log_dt", "s5_b_re", "s5_b_im", "s5_c_re", "s5_c_im", "s5_d", "s5_w_glu", "m_norm_g", "m_w_out", "w_o",
             "ln1_g", "ln1_b", "w_up", "b_up", "w_down", "ln2_g", "ln2_b"]
    small_names = ["ln0_g", "ln0_b", "b_in", "qk_conv_b", "s5_lambda_re", "s5_lambda_im", "s5_log_dt", "s5_b_re",
                   "s5_b_im", "s5_c_re", "s5_c_im", "s5_d", "m_norm_g", "ln1_g", "ln1_b", "b_up", "ln2_g", "ln2_b",
                   "meta_tokens", "qk_conv_w"]
    res = {}
    for i, n in enumerate(small_names):
        res[n] = (small_g[i], sd[i], sm2[i], sv2[i])
    for i, n in enumerate(names):
        res[n] = tuple(o[None] for o in big_out[i])
    outs = [loss, grad_x]
    for kind in range(4):
        outs += [res[n][kind] for n in order]
    return tuple(outs)
```

```python
import functools
import math

import jax
import jax.numpy as jnp
from jax import lax
from jax.experimental import pallas as pl
from jax.experimental.pallas import tpu as pltpu

F32 = jnp.float32
BF16 = jnp.bfloat16

D_MODEL = 1024
N_META = 16
CHUNK = 128
PAD_ROWS = CHUNK - N_META
S5_WIDTH = 512
S5_GROUP = 16
S5_GROUPS = 32
S5_STATE = 64
S5_COLS = 2 * S5_GROUPS * S5_STATE
S5_BLK = 4
M_HEADS = 4
M_DK = 128
M_DV = 256
D_FF = 4096
N_DEV = 8
ALPHA = 2.0 ** 0.25
LN_EPS = 1e-5
IN_NAT = 5640
P_V, P_O, P_GS, P_GM, P_QK, P_U, P_IF, PW = 0, 1024, 2048, 3072, 4096, 5120, 5632, 5760
N_U, N_Q, N_K, N_V, N_O, N_I, N_GS, N_GM = 0, 512, 1024, 1536, 2560, 3584, 3592, 4616

ADAM_LR, ADAM_B1, ADAM_B2, ADAM_EPS, ADAM_WD, ADAM_STEP = 0.001, 0.9, 0.999, 1e-08, 0.01, 10

VMEM_LIMIT = 56 * 1024 * 1024
MXU_WIDTH = 256
MESH = pl.DeviceIdType.MESH


def _pick(n, cands):
    for c in cands:
        if n % c == 0:
            return c
    raise ValueError(f"no tile for {n} among {cands}")


def _cparams(sem):
    return pltpu.CompilerParams(dimension_semantics=sem, vmem_limit_bytes=VMEM_LIMIT)


def _dot(a, b, ca, cb):
    return lax.dot_general(a, b, (((ca,), (cb,)), ((), ())), preferred_element_type=F32)


def _sigmoid(x):
    return 1.0 / (1.0 + jnp.exp(-x))


def _peer(k):
    x, y, c = lax.axis_index("x"), lax.axis_index("y"), lax.axis_index("c")
    px = 1 - x if k & 4 else x
    py = 1 - y if k & 2 else y
    pc = 1 - c if k & 1 else c
    return (px, py, pc), 4 * px + 2 * py + pc


def _my_id():
    return 4 * lax.axis_index("x") + 2 * lax.axis_index("y") + lax.axis_index("c")


def _hbm_call(body, name, arrs, out_shape, n_remote):
    n = len(arrs)
    return pl.pallas_call(
        body, name=name,
        out_shape=tuple(out_shape),
        in_specs=[pl.BlockSpec(memory_space=pl.ANY)] * n,
        out_specs=tuple([pl.BlockSpec(memory_space=pl.ANY)] * len(out_shape)),
        scratch_shapes=[pltpu.SemaphoreType.DMA((n, n_remote)),
                        pltpu.SemaphoreType.DMA((n, n_remote)),
                        pltpu.SemaphoreType.DMA((n,))],
    )(*arrs)


class _Gather:
    def __init__(self, ins, outs, send_sems, recv_sems, local_sems):
        self.ins, self.outs, self.n = ins, outs, len(ins)
        self.send_sems, self.recv_sems, self.local_sems = send_sems, recv_sems, local_sems
        x, y, c = lax.axis_index("x"), lax.axis_index("y"), lax.axis_index("c")
        self.c = c
        self.me, self.sibling = (x, y, c), (x, y, 1 - c)
        self.chips = [(1 - x, y), (x, 1 - y), (1 - x, 1 - y)]

    def slot(self, a, dev):
        return self.outs[a].at[4 * dev[0] + 2 * dev[1] + dev[2]]

    def copy(self, a, k, block, to, own=False):
        return pltpu.make_async_remote_copy(
            src_ref=self.ins[a] if own else self.slot(a, block), dst_ref=self.slot(a, block),
            send_sem=self.send_sems.at[a, k], recv_sem=self.recv_sems.at[a, k],
            device_id=to, device_id_type=MESH)

    def first_sends(self, a):
        return [self.copy(a, 0, self.me, self.sibling, own=True)] + [
            self.copy(a, 1 + j, self.me, (*chip, self.c), own=True) for j, chip in enumerate(self.chips)]

    def start(self):
        for a in range(self.n):
            pltpu.make_async_copy(self.ins[a], self.slot(a, self.me), self.local_sems.at[a]).start()
            for cp in self.first_sends(a):
                cp.start()

    def finish(self):
        forwards = []
        for j, chip in enumerate(self.chips):
            for a in range(self.n):
                self.copy(a, 1 + j, (*chip, self.c), self.me).wait_recv()
                fwd = self.copy(a, 4 + j, (*chip, self.c), self.sibling)
                fwd.start()
                forwards.append(fwd)
        for a in range(self.n):
            self.copy(a, 0, self.sibling, self.me).wait_recv()
            for j, chip in enumerate(self.chips):
                self.copy(a, 4 + j, (*chip, 1 - self.c), self.me).wait_recv()
        for a in range(self.n):
            for cp in self.first_sends(a):
                cp.wait_send()
            pltpu.make_async_copy(self.ins[a], self.slot(a, self.me), self.local_sems.at[a]).wait()
        for cp in forwards:
            cp.wait_send()

    @staticmethod
    def out_shapes(arrs):
        return [jax.ShapeDtypeStruct((N_DEV,) + tuple(a.shape), a.dtype) for a in arrs]


class _AllToAll:
    def __init__(self, ins, outs, send_sems, recv_sems, local_sems):
        self.ins, self.outs, self.n = ins, outs, len(ins)
        self.send_sems, self.recv_sems, self.local_sems = send_sems, recv_sems, local_sems
        self.me = _my_id()

    def copy(self, a, k, landing):
        peer, pid = _peer(k)
        return pltpu.make_async_remote_copy(
            src_ref=self.ins[a].at[pid], dst_ref=self.outs[a].at[pid if landing else self.me],
            send_sem=self.send_sems.at[a, k - 1], recv_sem=self.recv_sems.at[a, k - 1],
            device_id=peer, device_id_type=MESH)

    def local(self, a):
        return pltpu.make_async_copy(self.ins[a].at[self.me], self.outs[a].at[self.me], self.local_sems.at[a])

    def start(self):
        for a in range(self.n):
            self.local(a).start()
            for k in range(1, N_DEV):
                self.copy(a, k, False).start()

    def finish(self):
        for a in range(self.n):
            for k in range(1, N_DEV):
                self.copy(a, k, True).wait_recv()
        for a in range(self.n):
            for k in range(1, N_DEV):
                self.copy(a, k, False).wait_send()
            self.local(a).wait()

    @staticmethod
    def out_shapes(arrs):
        return [jax.ShapeDtypeStruct(tuple(a.shape), a.dtype) for a in arrs]


def _comm_scratch(n):
    return [pltpu.SemaphoreType.DMA((n, N_DEV - 1)), pltpu.SemaphoreType.DMA((n, N_DEV - 1)),
            pltpu.SemaphoreType.DMA((n,))]


def _ride_call(body, name, grid, in_specs, out_specs, out_shape, scratch, args, comm=None, aliases=None):
    n_in, n_out = len(in_specs), len(out_specs)
    in_specs, out_specs, out_shape = list(in_specs), list(out_specs), list(out_shape)
    scratch, args = list(scratch), list(args)
    kernel_fn = body
    if comm is not None:
        cls, arrs = comm
        n = len(arrs)

        def kernel_fn(*refs):
            ins, cin = refs[:n_in], refs[n_in:n_in + n]
            outs = refs[n_in + n:n_in + n + n_out]
            cout = refs[n_in + n + n_out:n_in + 2 * n + n_out]
            own_scratch, sems = refs[n_in + 2 * n + n_out:-3], refs[-3:]
            exchange = cls(cin, cout, *sems)
            ids = [pl.program_id(d) for d in range(len(grid))]
            first = functools.reduce(lambda a, b: a & b, [i == 0 for i in ids])
            last = functools.reduce(lambda a, b: a & b, [i == g - 1 for i, g in zip(ids, grid)])

            @pl.when(first)
            def _():
                exchange.start()

            body(*ins, *outs, *own_scratch)

            @pl.when(last)
            def _():
                exchange.finish()

        in_specs += [pl.BlockSpec(memory_space=pl.ANY)] * n
        args += list(arrs)
        out_specs += [pl.BlockSpec(memory_space=pl.ANY)] * n
        out_shape += cls.out_shapes(arrs)
        scratch += _comm_scratch(n)
    res = pl.pallas_call(
        kernel_fn, name=name, grid=grid,
        in_specs=in_specs, out_specs=tuple(out_specs), out_shape=tuple(out_shape),
        scratch_shapes=scratch, input_output_aliases=aliases or {},
        compiler_params=_cparams(("arbitrary",) * len(grid)),
    )(*args)
    return list(res[:n_out]), list(res[n_out:])


def _shard_tile(R, C):
    if R % 128 == 0:
        return 128, C
    return R, _pick(C, (256, 128))


def _allreduce_small(pack):
    rows = pack.shape[0]

    def gather(*refs):
        g = _Gather(refs[:1], refs[1:2], *refs[2:])
        g.start()
        g.finish()

    slots = _hbm_call(gather, "allreduce_gather", [pack], _Gather.out_shapes([pack]), N_DEV - 1)[0]

    def body(s_ref, o_ref):
        acc = s_ref[0]
        for s in range(1, N_DEV):
            acc = acc + s_ref[s]
        o_ref[...] = acc

    return pl.pallas_call(
        body, name="allreduce_sum", grid=(1,),
        in_specs=[pl.BlockSpec((N_DEV, rows, 128), lambda i: (0, 0, 0))],
        out_specs=pl.BlockSpec((rows, 128), lambda i: (0, 0)),
        out_shape=jax.ShapeDtypeStruct((rows, 128), F32),
        compiler_params=_cparams(("arbitrary",)),
    )(slots)


def _mm_tiles(M, N, K, mode, out_bytes, n_extra_f32, a_bytes, b_bytes):
    budget = 40 * 1024 * 1024
    tms = [t for t in (1664, 1408, 1152, 1024, 640, 512, 256, 128) if M % t == 0]
    tns = [t for t in (1152, 1024, 640, 512, 384, 256, 128) if N % t == 0]
    if mode == "tn":
        tks = [t for t in (1664, 1408, 640, 512, 256, 128) if K % t == 0]
    else:
        tks = [K] if K <= 1152 else [t for t in (1152, 1024, 640, 512) if K % t == 0]
    best = None
    for tm in tms:
        for tn in tns:
            for tk in tks:
                nk = K // tk
                use = 2 * (tm * tk * a_bytes + tk * tn * b_bytes) + 2 * tm * tn * out_bytes
                use += 2 * n_extra_f32 * tm * tn * 4 + tm * tn * 4 * (2 if nk > 1 else 1)
                if use > budget:
                    continue
                score = (tm * tn * tk, tm * tn)
                if best is None or score > best[0]:
                    best = (score, (tm, tn, tk))
    assert best is not None, (M, N, K, mode)
    return best[1]


def _mm(a, b, mode, name, *, bias=None, add=None, add_scale=1.0, sqrt_gate_of=None,
        relu2_out=False, colsum=None, out_dtype=F32, comm=None):
    if mode == "nn":
        (M, K), (K2, N) = a.shape, b.shape
    elif mode == "nt":
        (M, K), (N, K2) = a.shape, b.shape
    else:
        (K, M), (K2, N) = a.shape, b.shape
    assert K == K2, (a.shape, b.shape, mode)
    has_bias, has_add, has_gate = bias is not None, add is not None, sqrt_gate_of is not None
    tm, tn, tk = _mm_tiles(M, N, K, mode, jnp.dtype(out_dtype).itemsize, int(has_add) + int(has_gate),
                           a.dtype.itemsize, b.dtype.itemsize)
    nk = K // tk
    assert colsum is None or mode == "tn"
    assert colsum != "a" or N == tn
    comm_cls, comm_arrs = comm if comm is not None else (None, [])
    nc = len(comm_arrs)
    grid = (N // tn, M // tm, nk)

    def body(*refs):
        it = iter(refs)
        a_ref, b_ref = next(it), next(it)
        bias_ref = next(it) if has_bias else None
        add_ref = next(it) if has_add else None
        gate_ref = next(it) if has_gate else None
        comm_ins = [next(it) for _ in range(nc)]
        o_ref = next(it)
        cs_ref = next(it) if colsum else None
        comm_outs = [next(it) for _ in range(nc)]
        acc_ref = next(it) if nk > 1 else None
        j, i, k = pl.program_id(0), pl.program_id(1), pl.program_id(2)
        if nc:
            exchange = comm_cls(comm_ins, comm_outs, next(it), next(it), next(it))

            @pl.when((j == 0) & (i == 0) & (k == 0))
            def _():
                exchange.start()

        blocks = [slice(n0, min(n0 + MXU_WIDTH, tn)) for n0 in range(0, tn, MXU_WIDTH)]
        av = a_ref[...].astype(BF16)

        def part(cols):
            if mode == "nn":
                return _dot(av, b_ref[:, cols].astype(BF16), 1, 0)
            if mode == "nt":
                return _dot(av, b_ref[cols, :].astype(BF16), 1, 1)
            return _dot(av, b_ref[:, cols].astype(BF16), 0, 0)

        if colsum == "b":
            @pl.when((i == 0) & (k == 0))
            def _():
                cs_ref[...] = jnp.zeros_like(cs_ref)

            @pl.when(i == 0)
            def _():
                cs_ref[0:1, :] += jnp.sum(b_ref[...].astype(F32), axis=0, keepdims=True)
        if colsum == "a":
            @pl.when(k == 0)
            def _():
                cs_ref[...] = jnp.zeros_like(cs_ref)

            cs_ref[0:1, :] += jnp.sum(a_ref[...].astype(F32), axis=0, keepdims=True)

        def finish(cols, r):
            if has_bias:
                r = r + bias_ref[:, cols]
            if has_add:
                r = r + add_scale * add_ref[:, cols]
            if has_gate:
                r = r * (2.0 * jnp.sqrt(gate_ref[:, cols].astype(F32)))
            if relu2_out:
                r = jnp.square(jnp.maximum(r, 0.0))
            o_ref[:, cols] = r.astype(out_dtype)

        if nk == 1:
            for cols in blocks:
                finish(cols, part(cols))
        else:
            @pl.when(k == 0)
            def _():
                for cols in blocks:
                    acc_ref[:, cols] = part(cols)

            @pl.when((k > 0) & (k < nk - 1))
            def _():
                for cols in blocks:
                    acc_ref[:, cols] += part(cols)

            @pl.when(k == nk - 1)
            def _():
                for cols in blocks:
                    finish(cols, acc_ref[:, cols] + part(cols))

        if nc:
            @pl.when((j == grid[0] - 1) & (i == grid[1] - 1) & (k == nk - 1))
            def _():
                exchange.finish()

    if mode == "nn":
        a_spec = pl.BlockSpec((tm, tk), lambda j, i, k: (i, k))
        b_spec = pl.BlockSpec((tk, tn), lambda j, i, k: (k, j))
    elif mode == "nt":
        a_spec = pl.BlockSpec((tm, tk), lambda j, i, k: (i, k))
        b_spec = pl.BlockSpec((tn, tk), lambda j, i, k: (j, k))
    else:
        a_spec = pl.BlockSpec((tk, tm), lambda j, i, k: (k, i))
        b_spec = pl.BlockSpec((tk, tn), lambda j, i, k: (k, j))
    in_specs, args = [a_spec, b_spec], [a, b]
    if has_bias:
        in_specs.append(pl.BlockSpec((1, tn), lambda j, i, k: (0, j)))
        args.append(bias)
    if has_add:
        in_specs.append(pl.BlockSpec((tm, tn), lambda j, i, k: (i, j)))
        args.append(add)
    if has_gate:
        in_specs.append(pl.BlockSpec((tm, tn), lambda j, i, k: (i, j)))
        args.append(sqrt_gate_of)
    out_shape = [jax.ShapeDtypeStruct((M, N), out_dtype)]
    out_specs = [pl.BlockSpec((tm, tn), lambda j, i, k: (i, j))]
    if colsum == "b":
        out_shape.append(jax.ShapeDtypeStruct((8, N), F32))
        out_specs.append(pl.BlockSpec((8, tn), lambda j, i, k: (0, j)))
    if colsum == "a":
        out_shape.append(jax.ShapeDtypeStruct((8, M), F32))
        out_specs.append(pl.BlockSpec((8, tm), lambda j, i, k: (0, i)))
    scratch = [pltpu.VMEM((tm, tn), F32)] if nk > 1 else []
    if nc:
        in_specs += [pl.BlockSpec(memory_space=pl.ANY)] * nc
        args += list(comm_arrs)
        out_specs += [pl.BlockSpec(memory_space=pl.ANY)] * nc
        out_shape += comm_cls.out_shapes(comm_arrs)
        scratch += _comm_scratch(nc)
    res = pl.pallas_call(
        body, name=name, grid=grid,
        in_specs=in_specs, out_specs=tuple(out_specs), out_shape=tuple(out_shape),
        scratch_shapes=scratch,
        compiler_params=_cparams(("arbitrary", "arbitrary", "arbitrary")),
    )(*args)
    return res if len(res) > 1 else res[0]


def _ln_rows(v, g, b):
    mu = jnp.mean(v, axis=-1, keepdims=True)
    xc = v - mu
    var = jnp.mean(xc * xc, axis=-1, keepdims=True)
    return xc * lax.rsqrt(var + LN_EPS) * g + b


def _ln_bwd_rows(dy, v, g):
    mu = jnp.mean(v, axis=-1, keepdims=True)
    xc = v - mu
    var = jnp.mean(xc * xc, axis=-1, keepdims=True)
    rstd = lax.rsqrt(var + LN_EPS)
    xhat = xc * rstd
    dxh = dy * g
    dv = rstd * (dxh - jnp.mean(dxh, axis=-1, keepdims=True)
                 - xhat * jnp.mean(dxh * xhat, axis=-1, keepdims=True))
    return dv, xhat


def _real_tile(S):
    return _pick(S, (512, 256, 128, 64))


def _real_rows(rb, ncols, lp):
    return pl.BlockSpec((pl.Element(rb), pl.Element(ncols)),
                        lambda bb, j: (pl.multiple_of(bb * lp + CHUNK + j * rb, CHUNK), 0))


def _head_rows(ncols, lp):
    return pl.BlockSpec((CHUNK, ncols), lambda bb: (bb * (lp // CHUNK), 0))


def _ln0_fwd(x, g, b, comm=None):
    B, S, D = x.shape
    lp = S + CHUNK
    rb = _real_tile(S)

    def body(x_ref, g_ref, b_ref, h_ref, hb_ref):
        y = _ln_rows(x_ref[0], g_ref[...], b_ref[...])
        h_ref[...] = y
        hb_ref[...] = y.astype(BF16)

    vec = pl.BlockSpec((1, D), lambda bb, j: (0, 0))
    return _ride_call(
        body, "ln0_fwd", (B, S // rb),
        [pl.BlockSpec((1, rb, D), lambda bb, j: (bb, j, 0)), vec, vec],
        [_real_rows(rb, D, lp), _real_rows(rb, D, lp)],
        [jax.ShapeDtypeStruct((B * lp, D), F32), jax.ShapeDtypeStruct((B * lp, D), BF16)],
        [], (x, g, b), comm=comm)


def _ln0_head(meta, g, b, h, hb, lp):
    D = meta.shape[1]
    B = h.shape[0] // lp

    def head(meta_ref, g_ref, b_ref, hin_ref, hbin_ref, h_ref, hb_ref):
        m = _ln_rows(meta_ref[...], g_ref[...], b_ref[...])
        h_ref[0:PAD_ROWS, :] = jnp.zeros((PAD_ROWS, D), F32)
        h_ref[PAD_ROWS:CHUNK, :] = m
        hb_ref[0:PAD_ROWS, :] = jnp.zeros((PAD_ROWS, D), BF16)
        hb_ref[PAD_ROWS:CHUNK, :] = m.astype(BF16)

    vec1 = pl.BlockSpec((1, D), lambda bb: (0, 0))
    anyspec = pl.BlockSpec(memory_space=pl.ANY)
    return pl.pallas_call(
        head, name="ln0_head", grid=(B,),
        in_specs=[pl.BlockSpec((N_META, D), lambda bb: (0, 0)), vec1, vec1, anyspec, anyspec],
        out_specs=(_head_rows(D, lp), _head_rows(D, lp)),
        out_shape=(jax.ShapeDtypeStruct((B * lp, D), F32), jax.ShapeDtypeStruct((B * lp, D), BF16)),
        input_output_aliases={3: 0, 4: 1},
        compiler_params=_cparams(("arbitrary",)),
    )(meta, g, b, h, hb)


def _ln0_bwd(dh0, x, meta, g):
    B, S, D = x.shape
    lp = S + CHUNK
    rb = _real_tile(S)

    def body(dh_ref, x_ref, g_ref, dx_ref, dg_ref, db_ref):
        @pl.when((pl.program_id(0) == 0) & (pl.program_id(1) == 0))
        def _():
            dg_ref[...] = jnp.zeros_like(dg_ref)
            db_ref[...] = jnp.zeros_like(db_ref)

        dy = dh_ref[...]
        dv, xhat = _ln_bwd_rows(dy, x_ref[0], g_ref[...])
        dx_ref[0] = dv
        dg_ref[0:1, :] += jnp.sum(dy * xhat, axis=0, keepdims=True)
        db_ref[0:1, :] += jnp.sum(dy, axis=0, keepdims=True)

    const = lambda bb, j: (0, 0)
    xblk = pl.BlockSpec((1, rb, D), lambda bb, j: (bb, j, 0))
    acc_shape = jax.ShapeDtypeStruct((8, D), F32)
    dx, dg, db = pl.pallas_call(
        body, name="ln0_bwd", grid=(B, S // rb),
        in_specs=[_real_rows(rb, D, lp), xblk, pl.BlockSpec((1, D), const)],
        out_specs=(xblk, pl.BlockSpec((8, D), const), pl.BlockSpec((8, D), const)),
        out_shape=(jax.ShapeDtypeStruct((B, S, D), F32), acc_shape, acc_shape),
        compiler_params=_cparams(("arbitrary", "arbitrary")),
    )(dh0, x, g)

    def head(dh_ref, meta_ref, g_ref, dmeta_ref, dg_ref, db_ref):
        @pl.when(pl.program_id(0) == 0)
        def _():
            dmeta_ref[...] = jnp.zeros_like(dmeta_ref)
            dg_ref[...] = jnp.zeros_like(dg_ref)
            db_ref[...] = jnp.zeros_like(db_ref)

        dy = dh_ref[PAD_ROWS:CHUNK, :]
        dv, xhat = _ln_bwd_rows(dy, meta_ref[...], g_ref[...])
        dmeta_ref[...] += dv
        dg_ref[0:1, :] += jnp.sum(dy * xhat, axis=0, keepdims=True)
        db_ref[0:1, :] += jnp.sum(dy, axis=0, keepdims=True)

    c1 = lambda bb: (0, 0)
    dmeta, dgm, dbm = pl.pallas_call(
        head, name="ln0_bwd_head", grid=(B,),
        in_specs=[_head_rows(D, lp), pl.BlockSpec((N_META, D), c1), pl.BlockSpec((1, D), c1)],
        out_specs=(pl.BlockSpec((N_META, D), c1), pl.BlockSpec((8, D), c1), pl.BlockSpec((8, D), c1)),
        out_shape=(jax.ShapeDtypeStruct((N_META, D), F32), acc_shape, acc_shape),
        compiler_params=_cparams(("arbitrary",)),
    )(dh0, meta, g)
    return dx, dmeta, dg + dgm, db + dbm


def _ln_res_fwd(h_prev, r, g, b, name):
    T, D = h_prev.shape
    tr = _pick(T, (384, 320, 256, 128, 64))

    def body(hp_ref, r_ref, g_ref, b_ref, pre_ref, h_ref, hb_ref):
        pre = ALPHA * hp_ref[...] + r_ref[...]
        y = _ln_rows(pre, g_ref[...], b_ref[...])
        pre_ref[...] = pre
        h_ref[...] = y
        hb_ref[...] = y.astype(BF16)

    row = pl.BlockSpec((tr, D), lambda i: (i, 0))
    vec = pl.BlockSpec((1, D), lambda i: (0, 0))
    return pl.pallas_call(
        body, name=name, grid=(T // tr,),
        in_specs=[row, row, vec, vec], out_specs=(row, row, row),
        out_shape=(jax.ShapeDtypeStruct((T, D), F32), jax.ShapeDtypeStruct((T, D), F32),
                   jax.ShapeDtypeStruct((T, D), BF16)),
        compiler_params=_cparams(("arbitrary",)),
    )(h_prev, r, g, b)


def _ln_bwd(dh, pre, g, name):
    T, D = dh.shape
    tr = _pick(T, (384, 320, 256, 128, 64))

    def body(dh_ref, pre_ref, g_ref, dp_ref, dpb_ref, dg_ref, db_ref):
        @pl.when(pl.program_id(0) == 0)
        def _():
            dg_ref[...] = jnp.zeros_like(dg_ref)
            db_ref[...] = jnp.zeros_like(db_ref)

        dy = dh_ref[...]
        dv, xhat = _ln_bwd_rows(dy, pre_ref[...], g_ref[...])
        dp_ref[...] = dv
        dpb_ref[...] = dv.astype(BF16)
        dg_ref[0:1, :] += jnp.sum(dy * xhat, axis=0, keepdims=True)
        db_ref[0:1, :] += jnp.sum(dy, axis=0, keepdims=True)

    row = pl.BlockSpec((tr, D), lambda i: (i, 0))
    vec = pl.BlockSpec((1, D), lambda i: (0, 0))
    acc = pl.BlockSpec((8, D), lambda i: (0, 0))
    return pl.pallas_call(
        body, name=name, grid=(T // tr,),
        in_specs=[row, row, vec], out_specs=(row, row, acc, acc),
        out_shape=(jax.ShapeDtypeStruct((T, D), F32), jax.ShapeDtypeStruct((T, D), BF16),
                   jax.ShapeDtypeStruct((8, D), F32), jax.ShapeDtypeStruct((8, D), F32)),
        compiler_params=_cparams(("arbitrary",)),
    )(dh, pre, g)


def _ln2_loss(h1, ff, g, b, target):
    T, D = h1.shape
    B, S, _ = target.shape
    lp = S + CHUNK
    rb = _real_tile(S)

    def body(h_ref, ff_ref, g_ref, b_ref, t_ref, loss_ref, dp_ref, dpb_ref, dg_ref, db_ref):
        @pl.when((pl.program_id(0) == 0) & (pl.program_id(1) == 0))
        def _():
            loss_ref[...] = jnp.zeros_like(loss_ref)
            dg_ref[...] = jnp.zeros_like(dg_ref)
            db_ref[...] = jnp.zeros_like(db_ref)

        pre = ALPHA * h_ref[...] + ff_ref[...]
        gg = g_ref[...]
        y = _ln_rows(pre, gg, b_ref[...])
        err = y - t_ref[0]
        loss_ref[0:1, 0:1] += 0.5 * jnp.sum(jnp.mean(err * err, axis=-1, keepdims=True), axis=0, keepdims=True)
        dy = err * (1.0 / D)
        dv, xhat = _ln_bwd_rows(dy, pre, gg)
        dp_ref[...] = dv
        dpb_ref[...] = dv.astype(BF16)
        dg_ref[0:1, :] += jnp.sum(dy * xhat, axis=0, keepdims=True)
        db_ref[0:1, :] += jnp.sum(dy, axis=0, keepdims=True)

    row = _real_rows(rb, D, lp)
    const = lambda bb, j: (0, 0)
    loss, dp, dpb, dg, db = pl.pallas_call(
        body, name="ln2_loss", grid=(B, S // rb),
        in_specs=[row, row, pl.BlockSpec((1, D), const), pl.BlockSpec((1, D), const),
                  pl.BlockSpec((1, rb, D), lambda bb, j: (bb, j, 0))],
        out_specs=(pl.BlockSpec((8, 128), const), row, row,
                   pl.BlockSpec((8, D), const), pl.BlockSpec((8, D), const)),
        out_shape=(jax.ShapeDtypeStruct((8, 128), F32),
                   jax.ShapeDtypeStruct((T, D), F32), jax.ShapeDtypeStruct((T, D), BF16),
                   jax.ShapeDtypeStruct((8, D), F32), jax.ShapeDtypeStruct((8, D), F32)),
        compiler_params=_cparams(("arbitrary", "arbitrary")),
    )(h1, ff, g, b, target)

    def head(dpin_ref, dpbin_ref, dp_ref, dpb_ref):
        dp_ref[...] = jnp.zeros((CHUNK, D), F32)
        dpb_ref[...] = jnp.zeros((CHUNK, D), BF16)

    anyspec = pl.BlockSpec(memory_space=pl.ANY)
    dp, dpb = pl.pallas_call(
        head, name="ln2_head", grid=(B,),
        in_specs=[anyspec, anyspec],
        out_specs=(_head_rows(D, lp), _head_rows(D, lp)),
        out_shape=(jax.ShapeDtypeStruct((T, D), F32), jax.ShapeDtypeStruct((T, D), BF16)),
        input_output_aliases={0: 0, 1: 1},
        compiler_params=_cparams(("arbitrary",)),
    )(dp, dpb)
    return loss, dp, dpb, dg, db


def _s5_prep(lam_re, lam_im, log_dt, b_re, b_im, c_re, c_im):
    dt = jnp.exp(log_dt)[:, None]
    mag = jnp.exp(lam_re * dt)
    ar = mag * jnp.cos(lam_im * dt)
    ai = mag * jnp.sin(lam_im * dt)
    nr, ni = ar - 1.0, ai
    den = lam_re * lam_re + lam_im * lam_im
    cr = (nr * lam_re + ni * lam_im) / den
    ci = (ni * lam_re - nr * lam_im) / den
    bbr = cr[..., None] * b_re - ci[..., None] * b_im
    bbi = cr[..., None] * b_im + ci[..., None] * b_re
    eye = jnp.eye(8, dtype=F32)
    bb = jnp.stack([bbr, bbi]).reshape(2, S5_BLK, 8, S5_STATE, S5_GROUP)
    bblk = jnp.einsum("rbgph,gj->bghrjp", bb, eye).reshape(S5_BLK, 128, 1024)
    cc = jnp.stack([c_re, -c_im]).reshape(2, S5_BLK, 8, S5_GROUP, S5_STATE)
    cblk = jnp.einsum("rbghp,gj->brjpgh", cc, eye).reshape(S5_BLK, 1024, 128)
    return ar.reshape(1, 2048), ai.reshape(1, 2048), bblk, cblk


def _scan_tables(ar, ai):
    pr, pi = [ar], [ai]
    for _ in range(7):
        pr, pi = pr + [pr[-1] * ar - pi[-1] * ai], pi + [pr[-1] * ai + pi[-1] * ar]
    pw_r = jnp.concatenate(pr, axis=0)
    pw_i = jnp.concatenate(pi, axis=0)
    rev_r = jnp.concatenate(pr[::-1], axis=0)
    rev_i = jnp.concatenate(pi[::-1], axis=0)
    row = jnp.arange(8)[:, None]

    def tables(sign, reverse):
        rows = []
        for n, sh in ((0, 1), (1, 2), (3, 4)):
            mask = (row < 8 - sh) if reverse else (row >= sh)
            rows.append(jnp.where(mask, pw_r[n][None, :], 0.0))
            rows.append(jnp.where(mask, sign * pw_i[n][None, :], 0.0))
        cr_ = rev_r if reverse else pw_r
        ci_ = rev_i if reverse else pw_i
        rows += [cr_, sign * ci_]
        return jnp.stack(rows)

    return tables(1.0, False), tables(-1.0, True)


def _seg_scan(s_ref, row0, seg, lam_ref, tab_ref, carry_ref, reverse, cseg_ref=None, extra=None):
    sgn = -1.0 if reverse else 1.0
    take, edge = (0, 7) if reverse else (7, 0)
    rowid = lax.broadcasted_iota(jnp.int32, (8, 128), 0)
    all_pairs = [(blk * 1024 + j * 128, blk * 1024 + j * 128 + 512, blk * 512 + j * 128)
                 for blk in range(S5_BLK) for j in range(4)]

    def rows(it):
        i = (seg - 1 - it) if reverse else it
        return i, pl.multiple_of(row0 + i * 8, 8)

    for half in range(2):
        pairs = all_pairs[8 * half:8 * half + 8]

        def pass1(it, carry):
            _, r0 = rows(it)
            out = []
            for n, (cre, cim, tc) in enumerate(pairs):
                lr = lam_ref[0, :, tc:tc + 128]
                li = sgn * lam_ref[1, :, tc:tc + 128]
                pr, pi = carry[2 * n], carry[2 * n + 1]
                xr = lr * pr - li * pi + s_ref[pl.ds(r0, 8), cre:cre + 128]
                xi = lr * pi + li * pr + s_ref[pl.ds(r0, 8), cim:cim + 128]
                s_ref[pl.ds(r0, 8), cre:cre + 128] = xr
                s_ref[pl.ds(r0, 8), cim:cim + 128] = xi
                out += [xr, xi]
            return tuple(out)

        ends = lax.fori_loop(0, seg, pass1, tuple(jnp.zeros((8, 128), F32) for _ in range(16)))

        start = []
        for n, (cre, cim, tc) in enumerate(pairs):
            xr, xi = ends[2 * n], ends[2 * n + 1]
            for lvl, sh in enumerate((1, 2, 4)):
                lr = tab_ref[2 * lvl, :, tc:tc + 128]
                li = tab_ref[2 * lvl + 1, :, tc:tc + 128]
                shift = (8 - sh) if reverse else sh
                sr = pltpu.roll(xr, shift, 0)
                si = pltpu.roll(xi, shift, 0)
                xr, xi = xr + lr * sr - li * si, xi + lr * si + li * sr
            pr = tab_ref[6, :, tc:tc + 128]
            pi = tab_ref[7, :, tc:tc + 128]
            c_r = carry_ref[:, cre:cre + 128]
            c_i = carry_ref[:, cim:cim + 128]
            er, ei = xr + pr * c_r - pi * c_i, xi + pr * c_i + pi * c_r
            back = 7 if reverse else 1
            in_r = jnp.where(rowid == edge, c_r, pltpu.roll(er, back, 0))
            in_i = jnp.where(rowid == edge, c_i, pltpu.roll(ei, back, 0))
            nr = jnp.sum(jnp.where(rowid == take, er, 0.0), axis=0, keepdims=True)
            ni = jnp.sum(jnp.where(rowid == take, ei, 0.0), axis=0, keepdims=True)
            carry_ref[:, cre:cre + 128] = jnp.broadcast_to(nr, (8, 128))
            carry_ref[:, cim:cim + 128] = jnp.broadcast_to(ni, (8, 128))
            if cseg_ref is not None:
                cseg_ref[0:8, cre:cre + 128] = in_r
                cseg_ref[0:8, cim:cim + 128] = in_i
            start += [in_r, in_i]

        def pass2(it, carry):
            i, r0 = rows(it)
            out = []
            for n, (cre, cim, tc) in enumerate(pairs):
                lr = lam_ref[0, :, tc:tc + 128]
                li = sgn * lam_ref[1, :, tc:tc + 128]
                dr, di = carry[2 * n], carry[2 * n + 1]
                dr, di = lr * dr - li * di, lr * di + li * dr
                xr = s_ref[pl.ds(r0, 8), cre:cre + 128] + dr
                xi = s_ref[pl.ds(r0, 8), cim:cim + 128] + di
                s_ref[pl.ds(r0, 8), cre:cre + 128] = xr
                s_ref[pl.ds(r0, 8), cim:cim + 128] = xi
                if extra is not None:
                    extra(i, cre, cim, tc, xr, xi)
                out += [dr, di]
            return tuple(out)

        lax.fori_loop(0, seg, pass2, tuple(start))


def _to_segments(src_refs, dst_ref, seg, first=None):
    sub = lax.broadcasted_iota(jnp.int32, (8, 1), 0)
    for i in range(seg):
        for c, src in enumerate(src_refs):
            v = src[pl.ds(i, 8, stride=seg), :]
            if first is not None:
                v = jnp.where(first & (sub * seg + i < PAD_ROWS), 0.0, v)
            dst_ref[8 * i:8 * i + 8, c * 128:(c + 1) * 128] = v


def _from_segments(src_ref, dst_ref, seg, fn=None, zero_head=None):
    m = seg // 8
    for j in range(seg):
        for c in range(src_ref.shape[0]):
            v = src_ref[c, pl.ds(64 * (j % m) + j // m, 8, stride=8), :]
            if zero_head is not None and j < PAD_ROWS // 8:
                v = jnp.where(zero_head, 0.0, v)
            dst_ref[8 * j:8 * j + 8, c * 128:(c + 1) * 128] = v if fn is None else fn(v)


def _gelu(x):
    c = math.sqrt(2.0 / math.pi)
    t = jnp.tanh(c * (x + 0.044715 * x * x * x))
    return 0.5 * x * (1.0 + t)


def _gelu_grad(x):
    c = math.sqrt(2.0 / math.pi)
    t = jnp.tanh(c * (x + 0.044715 * x * x * x))
    return 0.5 * (1.0 + t) + 0.5 * x * (1.0 - t * t) * c * (1.0 + 3.0 * 0.044715 * x * x)


def _s5_tile(lp):
    return _pick(lp, (384, 320, 256, 128, 64))


def _s5_fwd(p, bblk, cblk, lam8, tab_f, dskip, lp, comm=None):
    T = p.shape[0]
    ts = _s5_tile(lp)
    seg = ts // 8
    nblk, per_seq = T // ts, lp // ts
    ucol = P_U // S5_WIDTH

    def body(u0, u1, u2, u3, b_ref, c_ref, lam_ref, tab_ref, d_ref, y_ref, gy_ref, cin_ref, s_sc, u_sc, y_sc, carry_sc):
        r = pl.program_id(0)
        first = (r % per_seq) == 0

        @pl.when(first)
        def _():
            carry_sc[...] = jnp.zeros_like(carry_sc)

        cin_ref[0] = carry_sc[...]
        _to_segments((u0, u1, u2, u3), u_sc, seg, first)
        u = u_sc[...]
        ub = u.astype(BF16)
        for blk in range(S5_BLK):
            s_sc[:, blk * 1024:(blk + 1) * 1024] = _dot(ub[:, blk * 128:(blk + 1) * 128], b_ref[blk], 1, 0)
        _seg_scan(s_sc, 0, seg, lam_ref, tab_ref, carry_sc, False)
        for blk in range(S5_BLK):
            sb = s_sc[:, blk * 1024:(blk + 1) * 1024].astype(BF16)
            y_sc[blk] = _dot(sb, c_ref[blk], 1, 0) + d_ref[:, blk * 128:(blk + 1) * 128] * u[:, blk * 128:(blk + 1) * 128]
        _from_segments(y_sc, y_ref, seg)
        gy_ref[...] = _gelu(y_ref[...]).astype(BF16)

    def ublock(c):
        return pl.BlockSpec((ts, 128), lambda r: (r, 4 * ucol + c))

    return _ride_call(
        body, "s5_fwd", (nblk,),
        [ublock(0), ublock(1), ublock(2), ublock(3),
         pl.BlockSpec((S5_BLK, 128, 1024), lambda r: (0, 0, 0)),
         pl.BlockSpec((S5_BLK, 1024, 128), lambda r: (0, 0, 0)),
         pl.BlockSpec((2, 8, 2048), lambda r: (0, 0, 0)),
         pl.BlockSpec((8, 8, 2048), lambda r: (0, 0, 0)),
         pl.BlockSpec((1, S5_WIDTH), lambda r: (0, 0))],
        [pl.BlockSpec((ts, S5_WIDTH), lambda r: (r, 0)),
         pl.BlockSpec((ts, S5_WIDTH), lambda r: (r, 0)),
         pl.BlockSpec((1, 8, S5_COLS), lambda r: (r, 0, 0))],
        [jax.ShapeDtypeStruct((T, S5_WIDTH), F32),
         jax.ShapeDtypeStruct((T, S5_WIDTH), BF16),
         jax.ShapeDtypeStruct((nblk, 8, S5_COLS), F32)],
        [pltpu.VMEM((ts, S5_COLS), F32), pltpu.VMEM((ts, S5_WIDTH), F32),
         pltpu.VMEM((S5_BLK, ts, 128), F32), pltpu.VMEM((8, S5_COLS), F32)],
        (p, p, p, p, bblk.astype(BF16), cblk.astype(BF16), lam8, tab_f, dskip), comm=comm)


def _s5_bwd(p, y_pre, dgy, cin, bblk, cblk, lam8, tab_f, tab_b, dskip, dif, dp, lp, comm=None):
    T = p.shape[0]
    ts = _s5_tile(lp)
    seg = ts // 8
    nblk, per_seq = T // ts, lp // ts
    ucol = P_U // S5_WIDTH
    assert P_IF == P_U + S5_WIDTH and P_U % (S5_WIDTH + 128) == 0

    def body(u0, u1, u2, u3, y0, y1, y2, y3, g0, g1, g2, g3, cin_ref, b_ref, bt_ref, c_ref, ct_ref, lam_ref, tf_ref,
             tb_ref, d_ref, dif_ref, dpin_ref, du_ref, dbb_ref, dcb_ref, dlam_ref, dd_ref, s_sc, a_sc, u_sc, dy_sc,
             y_sc, w_sc, carry_sc, carry_b):
        t = pl.program_id(0)
        r = nblk - 1 - t
        first = (r % per_seq) == 0
        last = (r % per_seq) == per_seq - 1

        @pl.when(t == 0)
        def _():
            dbb_ref[...] = jnp.zeros_like(dbb_ref)
            dcb_ref[...] = jnp.zeros_like(dcb_ref)
            dlam_ref[...] = jnp.zeros_like(dlam_ref)
            dd_ref[...] = jnp.zeros_like(dd_ref)

        @pl.when(last)
        def _():
            carry_b[...] = jnp.zeros_like(carry_b)

        carry_sc[...] = cin_ref[0]
        _to_segments((u0, u1, u2, u3), u_sc, seg, first)
        u = u_sc[...]
        ub = u.astype(BF16)
        for blk in range(S5_BLK):
            s_sc[8:8 + ts, blk * 1024:(blk + 1) * 1024] = _dot(ub[:, blk * 128:(blk + 1) * 128], b_ref[blk], 1, 0)
        _seg_scan(s_sc, 8, seg, lam_ref, tf_ref, carry_sc, False, cseg_ref=s_sc)

        _to_segments((g0, g1, g2, g3), dy_sc, seg)
        _to_segments((y0, y1, y2, y3), y_sc, seg)
        dy = dy_sc[...] * _gelu_grad(y_sc[...])
        dy_sc[...] = dy
        dyb = dy.astype(BF16)
        dd_ref[0:1, :] += jnp.sum(dy * u, axis=0, keepdims=True)
        for blk in range(S5_BLK):
            a_sc[:, blk * 1024:(blk + 1) * 1024] = _dot(dyb[:, blk * 128:(blk + 1) * 128], ct_ref[blk], 1, 0)
            sb = s_sc[8:8 + ts, blk * 1024:(blk + 1) * 1024].astype(BF16)
            dcb_ref[blk] += _dot(sb, dyb[:, blk * 128:(blk + 1) * 128], 0, 0)

        def lam_grad(i, cre, cim, tc, a_r, a_i):
            r0 = pl.multiple_of(i * 8, 8)
            pr = s_sc[pl.ds(r0, 8), cre:cre + 128]
            pi = s_sc[pl.ds(r0, 8), cim:cim + 128]
            dlam_ref[0, :, tc:tc + 128] += a_r * pr + a_i * pi
            dlam_ref[1, :, tc:tc + 128] += a_i * pr - a_r * pi

        _seg_scan(a_sc, 0, seg, lam_ref, tb_ref, carry_b, True, extra=lam_grad)

        for blk in range(S5_BLK):
            ab = a_sc[:, blk * 1024:(blk + 1) * 1024].astype(BF16)
            w_sc[blk] = _dot(ab, bt_ref[blk], 1, 0) + d_ref[:, blk * 128:(blk + 1) * 128] * dy_sc[:, blk * 128:(blk + 1) * 128]
            dbb_ref[blk] += _dot(u_sc[:, blk * 128:(blk + 1) * 128].astype(BF16), ab, 0, 0)
        _from_segments(w_sc, y_sc, seg, zero_head=first)
        du_ref[:, 0:S5_WIDTH] = y_sc[...].astype(BF16)
        du_ref[:, S5_WIDTH:S5_WIDTH + 128] = dif_ref[...]

    const3 = lambda t: (0, 0, 0)
    rev = lambda t: (nblk - 1 - t, 0)

    def lanes(c0):
        return [pl.BlockSpec((ts, 128), lambda t, cc=c0 + c: (nblk - 1 - t, cc)) for c in range(4)]

    return _ride_call(
        body, "s5_bwd", (nblk,),
        lanes(4 * ucol) + lanes(0) + lanes(0) + [
            pl.BlockSpec((1, 8, S5_COLS), lambda t: (nblk - 1 - t, 0, 0)),
            pl.BlockSpec((S5_BLK, 128, 1024), const3),
            pl.BlockSpec((S5_BLK, 1024, 128), const3),
            pl.BlockSpec((S5_BLK, 1024, 128), const3),
            pl.BlockSpec((S5_BLK, 128, 1024), const3),
            pl.BlockSpec((2, 8, 2048), const3),
            pl.BlockSpec((8, 8, 2048), const3),
            pl.BlockSpec((8, 8, 2048), const3),
            pl.BlockSpec((1, S5_WIDTH), lambda t: (0, 0)),
            pl.BlockSpec((ts, 128), rev),
            pl.BlockSpec(memory_space=pl.ANY)],
        [pl.BlockSpec((ts, S5_WIDTH + 128), lambda t: (nblk - 1 - t, P_U // (S5_WIDTH + 128))),
         pl.BlockSpec((S5_BLK, 128, 1024), const3),
         pl.BlockSpec((S5_BLK, 1024, 128), const3),
         pl.BlockSpec((2, 8, 2048), const3),
         pl.BlockSpec((8, S5_WIDTH), lambda t: (0, 0))],
        [jax.ShapeDtypeStruct((T, PW), BF16),
         jax.ShapeDtypeStruct((S5_BLK, 128, 1024), F32),
         jax.ShapeDtypeStruct((S5_BLK, 1024, 128), F32),
         jax.ShapeDtypeStruct((2, 8, 2048), F32),
         jax.ShapeDtypeStruct((8, S5_WIDTH), F32)],
        [pltpu.VMEM((ts + 8, S5_COLS), F32), pltpu.VMEM((ts, S5_COLS), F32),
         pltpu.VMEM((ts, S5_WIDTH), F32), pltpu.VMEM((ts, S5_WIDTH), F32),
         pltpu.VMEM((ts, S5_WIDTH), F32), pltpu.VMEM((S5_BLK, ts, 128), F32),
         pltpu.VMEM((8, S5_COLS), F32), pltpu.VMEM((8, S5_COLS), F32)],
        (p, p, p, p, y_pre, y_pre, y_pre, y_pre, dgy, dgy, dgy, dgy, cin,
         bblk.astype(BF16), jnp.swapaxes(bblk, 1, 2).astype(BF16),
         cblk.astype(BF16), jnp.swapaxes(cblk, 1, 2).astype(BF16), lam8, tab_f, tab_b, dskip, dif, dp),
        comm=comm, aliases={22: 0})


def _row_in_seq(i, tr, lp):
    rowid = lax.broadcasted_iota(jnp.int32, (tr, 1), 0)
    return (i * tr + rowid) % lp


def _conv_fwd(p, w, b, lp):
    T = p.shape[0]
    tr = _pick(T, (384, 320, 256, 128, 64))
    c = P_QK // 1024

    def body(x_ref, xp_ref, w_ref, b_ref, o_ref):
        i = pl.program_id(0)
        pos = _row_in_seq(i, tr, lp)
        x = jnp.where(pos < PAD_ROWS, 0.0, x_ref[...])
        pos_p = (i * tr - 8 + lax.broadcasted_iota(jnp.int32, (8, 1), 0)) % lp
        xp = jnp.where((pos_p < PAD_ROWS) | (i == 0), 0.0, xp_ref[...])
        xx = jnp.concatenate([xp, x], axis=0)
        acc = b_ref[...] + w_ref[3:4, :] * x
        for s in (1, 2, 3):
            acc = acc + w_ref[3 - s:4 - s, :] * pltpu.roll(xx, s, 0)[8:8 + tr]
        o_ref[...] = acc * _sigmoid(acc)

    return pl.pallas_call(
        body, name="conv_fwd", grid=(T // tr,),
        in_specs=[pl.BlockSpec((tr, 1024), lambda i: (i, c)),
                  pl.BlockSpec((8, 1024), lambda i: (jnp.maximum(i * (tr // 8) - 1, 0), c)),
                  pl.BlockSpec((4, 1024), lambda i: (0, 0)),
                  pl.BlockSpec((1, 1024), lambda i: (0, 0))],
        out_specs=pl.BlockSpec((tr, 1024), lambda i: (i, 0)),
        out_shape=jax.ShapeDtypeStruct((T, 1024), F32),
        compiler_params=_cparams(("arbitrary",)),
    )(p, p, w, b)


def _conv_bwd(p, dqk, w, b, dp, lp):
    T = p.shape[0]
    tr = _pick(T, (384, 320, 256, 128, 64))
    c = P_QK // 1024
    nb = T // tr

    def body(x_ref, xp_ref, xn_ref, g_ref, gn_ref, w_ref, b_ref, dpin_ref, dx_ref, dw_ref, db_ref):
        i = pl.program_id(0)

        @pl.when(i == 0)
        def _():
            dw_ref[...] = jnp.zeros_like(dw_ref)
            db_ref[...] = jnp.zeros_like(db_ref)

        def seqpos(off, n):
            return (i * tr + off + lax.broadcasted_iota(jnp.int32, (n, 1), 0)) % lp

        x = jnp.where(seqpos(0, tr) < PAD_ROWS, 0.0, x_ref[...])
        xp = jnp.where((seqpos(-8, 8) < PAD_ROWS) | (i == 0), 0.0, xp_ref[...])
        xn = jnp.where((seqpos(tr, 8) < PAD_ROWS) | (i == nb - 1), 0.0, xn_ref[...])
        xx = jnp.concatenate([xp, x, xn], axis=0)
        gg = jnp.concatenate([g_ref[...], gn_ref[...]], axis=0)
        n2 = tr + 8
        acc = b_ref[...] + w_ref[3:4, :] * xx[8:8 + n2]
        for s in (1, 2, 3):
            acc = acc + w_ref[3 - s:4 - s, :] * pltpu.roll(xx, s, 0)[8:8 + n2]
        sg = _sigmoid(acc)
        dpre = gg * (sg * (1.0 + acc * (1.0 - sg)))
        valid = jnp.concatenate(
            [seqpos(0, tr) >= PAD_ROWS, (seqpos(tr, 8) >= PAD_ROWS) & (i < nb - 1)], axis=0)
        dpre = jnp.where(valid, dpre, 0.0)
        d0 = dpre[0:tr]
        dx = w_ref[3:4, :] * d0
        for s in (1, 2, 3):
            dx = dx + w_ref[3 - s:4 - s, :] * pltpu.roll(dpre, n2 - s, 0)[0:tr]
        dx_ref[...] = jnp.where(seqpos(0, tr) < PAD_ROWS, 0.0, dx).astype(BF16)
        db_ref[0:1, :] += jnp.sum(d0, axis=0, keepdims=True)
        dw_ref[3:4, :] += jnp.sum(d0 * x, axis=0, keepdims=True)
        for s in (1, 2, 3):
            xs = pltpu.roll(xx, s, 0)[8:8 + tr]
            dw_ref[3 - s:4 - s, :] += jnp.sum(d0 * xs, axis=0, keepdims=True)

    t8 = tr // 8
    return pl.pallas_call(
        body, name="conv_bwd", grid=(nb,),
        in_specs=[pl.BlockSpec((tr, 1024), lambda i: (i, c)),
                  pl.BlockSpec((8, 1024), lambda i: (jnp.maximum(i * t8 - 1, 0), c)),
                  pl.BlockSpec((8, 1024), lambda i: (jnp.minimum((i + 1) * t8, nb * t8 - 1), c)),
                  pl.BlockSpec((tr, 1024), lambda i: (i, 0)),
                  pl.BlockSpec((8, 1024), lambda i: (jnp.minimum((i + 1) * t8, nb * t8 - 1), 0)),
                  pl.BlockSpec((4, 1024), lambda i: (0, 0)),
                  pl.BlockSpec((1, 1024), lambda i: (0, 0)),
                  pl.BlockSpec(memory_space=pl.ANY)],
        out_specs=(pl.BlockSpec((tr, 1024), lambda i: (i, c)),
                   pl.BlockSpec((8, 1024), lambda i: (0, 0)),
                   pl.BlockSpec((8, 1024), lambda i: (0, 0))),
        out_shape=(jax.ShapeDtypeStruct((T, PW), BF16),
                   jax.ShapeDtypeStruct((8, 1024), F32),
                   jax.ShapeDtypeStruct((8, 1024), F32)),
        input_output_aliases={7: 0},
        compiler_params=_cparams(("arbitrary",)),
    )(p, p, p, dqk, dqk, w, b, dp)


def _split3(x):
    hi = x.astype(BF16)
    r1 = x - hi.astype(F32)
    mid = r1.astype(BF16)
    lo = (r1 - mid.astype(F32)).astype(BF16)
    return hi, mid, lo


def _tri_sum(x, upper):
    r = lax.broadcasted_iota(jnp.int32, (CHUNK, CHUNK), 0)
    c = lax.broadcasted_iota(jnp.int32, (CHUNK, CHUNK), 1)
    tri = jnp.where((r <= c) if upper else (r >= c), 1.0, 0.0).astype(BF16)
    hi, mid, lo = _split3(x)
    return _dot(tri, hi, 1, 0) + _dot(tri, mid, 1, 0) + _dot(tri, lo, 1, 0)


def _lane_col(x, lane):
    l = lax.broadcasted_iota(jnp.int32, x.shape, 1)
    return jnp.sum(jnp.where(l == lane, x, 0.0), axis=1, keepdims=True)


def _to_row(col):
    r = lax.broadcasted_iota(jnp.int32, (CHUNK, CHUNK), 0)
    c = lax.broadcasted_iota(jnp.int32, (CHUNK, CHUNK), 1)
    return jnp.sum(jnp.where(r == c, col, 0.0), axis=0, keepdims=True)


def _to_col(row):
    r = lax.broadcasted_iota(jnp.int32, (CHUNK, CHUNK), 0)
    c = lax.broadcasted_iota(jnp.int32, (CHUNK, CHUNK), 1)
    return jnp.sum(jnp.where(r == c, row, 0.0), axis=1, keepdims=True)


def _log_sigmoid(x):
    return jnp.minimum(x, 0.0) - jnp.log(1.0 + jnp.exp(-jnp.abs(x)))


def _mlstm_gates(ifv, padmask):
    lf = jnp.where(padmask, 0.0, _log_sigmoid(ifv))
    b_all = _tri_sum(lf, False)
    li = jnp.where(padmask, -jnp.inf, ifv)
    return li, b_all


def _mlstm_head_fwd(q, k, v, li_col, b_col, c_st, n_st, m_st):
    r = lax.broadcasted_iota(jnp.int32, (CHUNK, CHUNK), 0)
    c = lax.broadcasted_iota(jnp.int32, (CHUNK, CHUNK), 1)
    rowid = lax.broadcasted_iota(jnp.int32, (CHUNK, 1), 0)
    b_row = _to_row(b_col)
    li_row = _to_row(li_col)
    dmat = jnp.where(r >= c, b_col - b_row + li_row, -jnp.inf)
    m_inter = b_col + m_st
    m_row = jnp.maximum(m_inter, jnp.max(dmat, axis=1, keepdims=True))
    w_intra = jnp.exp(dmat - m_row)
    w_inter = jnp.exp(m_inter - m_row)
    qb, kb, vb = q.astype(BF16), k.astype(BF16), v.astype(BF16)
    qk = _dot(qb, kb, 1, 1)
    s = qk * w_intra
    cb = c_st.astype(BF16)
    qc = _dot(qb, cb, 1, 0)
    qn = jnp.sum(q * n_st, axis=1, keepdims=True)
    num = _dot(s.astype(BF16), vb, 1, 0) + w_inter * qc
    den = jnp.sum(s, axis=1, keepdims=True) + w_inter * qn
    floor = jnp.exp(-m_row)
    rinv = 1.0 / jnp.maximum(jnp.abs(den), floor)
    h = num * rinv
    b_last = jnp.sum(jnp.where(rowid == CHUNK - 1, b_col, 0.0), axis=0, keepdims=True)
    g_col = b_last - b_col + li_col
    m_new = jnp.maximum(b_last + m_st, jnp.max(g_col, axis=0, keepdims=True))
    w_k = jnp.exp(g_col - m_new)
    decay = jnp.exp(b_last + m_st - m_new)
    kw = w_k * k
    c_new = decay * c_st + _dot(kw.astype(BF16), vb, 0, 0)
    n_new = decay * n_st + jnp.sum(kw, axis=0, keepdims=True)
    return dict(h=h, c_new=c_new, n_new=n_new, m_new=m_new, w_intra=w_intra, w_inter=w_inter, s=s,
                qc=qc, qn=qn, den=den, floor=floor, rinv=rinv, w_k=w_k, decay=decay, kw=kw,
                qb=qb, kb=kb, vb=vb, cb=cb)


def _mlstm_fwd(p, qk, lp):
    T = p.shape[0]
    nch = lp // CHUNK
    B = T // lp
    scale = M_DK ** -0.5

    def body(q_ref, k_ref, v_ref, if_ref, h_ref, cst_ref, nm_ref, *state):
        c_scs, nm_scs = state[:B * M_HEADS], state[B * M_HEADS:]
        ci = pl.program_id(0)

        @pl.when(ci == 0)
        def _():
            for ref in state:
                ref[...] = jnp.zeros_like(ref)

        rowid = lax.broadcasted_iota(jnp.int32, (CHUNK, 1), 0)
        padmask = (ci == 0) & (rowid < PAD_ROWS)
        for bb in range(B):
            li_all, b_all = _mlstm_gates(if_ref[bb], padmask)
            for hd in range(M_HEADS):
                c_sc, nm_sc = c_scs[bb * M_HEADS + hd], nm_scs[bb * M_HEADS + hd]
                c_st, n_st, m_row = c_sc[...], nm_sc[0:1, :], nm_sc[1:2, :]
                cst_ref[bb, 0, hd * M_DK:(hd + 1) * M_DK, :] = c_st
                nm_ref[bb, 0, hd:hd + 1, :] = n_st
                nm_ref[bb, 0, M_HEADS + hd:M_HEADS + hd + 1, :] = m_row
                q = q_ref[bb, :, hd * M_DK:(hd + 1) * M_DK]
                k = k_ref[bb, :, hd * M_DK:(hd + 1) * M_DK] * scale
                v = v_ref[bb, :, hd * M_DV:(hd + 1) * M_DV]
                o = _mlstm_head_fwd(q, k, v, _lane_col(li_all, hd), _lane_col(b_all, M_HEADS + hd),
                                    c_st, n_st, nm_sc[1:2, 0:1])
                h_ref[bb, :, hd * M_DV:(hd + 1) * M_DV] = o["h"]
                c_sc[...] = o["c_new"]
                nm_sc[0:1, :] = o["n_new"]
                nm_sc[1:2, :] = jnp.broadcast_to(o["m_new"], (1, 128))

    qk3, p3 = qk.reshape(B, lp, 1024), p.reshape(B, lp, PW)
    h, cst, nm = pl.pallas_call(
        body, name="mlstm_fwd", grid=(nch,),
        in_specs=[pl.BlockSpec((B, CHUNK, 512), lambda ci: (0, ci, 0)),
                  pl.BlockSpec((B, CHUNK, 512), lambda ci: (0, ci, 1)),
                  pl.BlockSpec((B, CHUNK, 1024), lambda ci: (0, ci, P_V // 1024)),
                  pl.BlockSpec((B, CHUNK, 128), lambda ci: (0, ci, P_IF // 128))],
        out_specs=(pl.BlockSpec((B, CHUNK, 1024), lambda ci: (0, ci, 0)),
                   pl.BlockSpec((B, 1, M_HEADS * M_DK, M_DV), lambda ci: (0, ci, 0, 0)),
                   pl.BlockSpec((B, 1, 8, 128), lambda ci: (0, ci, 0, 0))),
        out_shape=(jax.ShapeDtypeStruct((B, lp, 1024), F32),
                   jax.ShapeDtypeStruct((B, nch, M_HEADS * M_DK, M_DV), F32),
                   jax.ShapeDtypeStruct((B, nch, 8, 128), F32)),
        scratch_shapes=([pltpu.VMEM((M_DK, M_DV), F32)] * (B * M_HEADS)
                        + [pltpu.VMEM((8, 128), F32)] * (B * M_HEADS)),
        compiler_params=_cparams(("arbitrary",)),
    )(qk3, qk3, p3, p3)
    return h.reshape(T, 1024), cst, nm


def _mlstm_bwd(p, qk, cst, nm, dh, dp, lp, comm=None):
    T = p.shape[0]
    nch = lp // CHUNK
    B = T // lp
    scale = M_DK ** -0.5

    def body(q_ref, k_ref, v_ref, if_ref, cst_ref, nm_ref, dh_ref, dpin_ref, dqk_ref, dv_ref, dif_ref, *state):
        t = pl.program_id(0)
        ci = nch - 1 - t

        @pl.when(t == 0)
        def _():
            for ref in state:
                ref[...] = jnp.zeros_like(ref)

        for bb in range(B):
            one_sequence(bb, ci, q_ref, k_ref, v_ref, if_ref, cst_ref, nm_ref, dh_ref, dqk_ref, dv_ref, dif_ref,
                         state[bb * M_HEADS:(bb + 1) * M_HEADS],
                         state[(B + bb) * M_HEADS:(B + bb + 1) * M_HEADS])

    def one_sequence(bb, ci, q_ref, k_ref, v_ref, if_ref, cst_ref, nm_ref, dh_ref, dqk_ref, dv_ref, dif_ref,
                     dc_scs, dn_scs):
        rowid = lax.broadcasted_iota(jnp.int32, (CHUNK, 1), 0)
        lane = lax.broadcasted_iota(jnp.int32, (CHUNK, 128), 1)
        padmask = (ci == 0) & (rowid < PAD_ROWS)
        ifv = if_ref[bb]
        li_all, b_all = _mlstm_gates(ifv, padmask)
        db_all = jnp.zeros((CHUNK, 128), F32)
        dli_all = jnp.zeros((CHUNK, 128), F32)
        for hd in range(M_HEADS):
            q = q_ref[bb, :, hd * M_DK:(hd + 1) * M_DK]
            k = k_ref[bb, :, hd * M_DK:(hd + 1) * M_DK] * scale
            v = v_ref[bb, :, hd * M_DV:(hd + 1) * M_DV]
            c_st = cst_ref[bb, 0, hd * M_DK:(hd + 1) * M_DK, :]
            n_st = nm_ref[bb, 0, hd:hd + 1, :]
            m_st = nm_ref[bb, 0, M_HEADS + hd:M_HEADS + hd + 1, 0:1]
            o = _mlstm_head_fwd(q, k, v, _lane_col(li_all, hd), _lane_col(b_all, M_HEADS + hd), c_st, n_st, m_st)
            dc_sc, dn_sc = dc_scs[hd], dn_scs[hd]
            dc_new = dc_sc[...]
            dn_new = dn_sc[0:1, :]
            dcb = dc_new.astype(BF16)
            dhh = dh_ref[bb, :, hd * M_DV:(hd + 1) * M_DV]
            dnum = dhh * o["rinv"]
            dhh_h = jnp.sum(dhh * o["h"], axis=1, keepdims=True)
            sgn = jnp.where(o["den"] >= 0.0, 1.0, -1.0)
            dden = jnp.where(jnp.abs(o["den"]) > o["floor"], -dhh_h * o["rinv"] * sgn, 0.0)
            dnb = dnum.astype(BF16)
            ds = _dot(dnb, o["vb"], 1, 1) + dden
            sb = o["s"].astype(BF16)
            kwb = o["kw"].astype(BF16)
            dv = _dot(sb, dnb, 0, 0) + _dot(kwb, dcb, 1, 0)
            dqk_m = (ds * o["w_intra"]).astype(BF16)
            wdn = o["w_inter"] * dnum
            wdd = o["w_inter"] * dden
            dq = _dot(dqk_m, o["kb"], 1, 0) + _dot(wdn.astype(BF16), o["cb"], 1, 1) + wdd * n_st
            vdc = _dot(o["vb"], dcb, 1, 1)
            dk = _dot(dqk_m, o["qb"], 0, 0) + o["w_k"] * (vdc + dn_new)
            dd = ds * o["s"]
            dd_col = _to_col(jnp.sum(dd, axis=0, keepdims=True))
            dmi = o["w_inter"] * (jnp.sum(dnum * o["qc"], axis=1, keepdims=True) + dden * o["qn"])
            dg = o["w_k"] * (jnp.sum(k * vdc, axis=1, keepdims=True) + jnp.sum(k * dn_new, axis=1, keepdims=True))
            d_blast = (o["decay"] * (jnp.sum(jnp.sum(dc_new * c_st, axis=1, keepdims=True), axis=0, keepdims=True)
                                     + jnp.sum(dn_new * n_st, axis=1, keepdims=True))
                       + jnp.sum(dg, axis=0, keepdims=True))
            db_col = jnp.sum(dd, axis=1, keepdims=True) - dd_col + dmi - dg
            db_col = db_col + jnp.where(rowid == CHUNK - 1, d_blast, 0.0)
            dli_col = dd_col + dg
            db_all = db_all + jnp.where(lane == M_HEADS + hd, db_col, 0.0)
            dli_all = dli_all + jnp.where(lane == hd, dli_col, 0.0)
            dc_sc[...] = o["decay"] * dc_new + _dot(o["qb"], wdn.astype(BF16), 0, 0)
            dn_sc[0:1, :] = o["decay"] * dn_new + jnp.sum(q * wdd, axis=0, keepdims=True)
            dqk_ref[bb, :, hd * M_DK:(hd + 1) * M_DK] = dq
            dqk_ref[bb, :, 512 + hd * M_DK:512 + (hd + 1) * M_DK] = dk * scale
            dv_ref[bb, :, hd * M_DV:(hd + 1) * M_DV] = dv.astype(BF16)
        dlf = _tri_sum(db_all, True)
        dif = dli_all + dlf * _sigmoid(-ifv)
        dif_ref[bb] = jnp.where(padmask | (lane >= 2 * M_HEADS), 0.0, dif).astype(BF16)

    def rev(cb):
        return lambda t: (0, nch - 1 - t, cb)

    rev4 = lambda t: (0, nch - 1 - t, 0, 0)
    qk3, p3, dh3 = qk.reshape(B, lp, 1024), p.reshape(B, lp, PW), dh.reshape(B, lp, 1024)
    (dqk, dv, dif), comm_res = _ride_call(
        body, "mlstm_bwd", (nch,),
        [pl.BlockSpec((B, CHUNK, 512), rev(0)),
         pl.BlockSpec((B, CHUNK, 512), rev(1)),
         pl.BlockSpec((B, CHUNK, 1024), rev(P_V // 1024)),
         pl.BlockSpec((B, CHUNK, 128), rev(P_IF // 128)),
         pl.BlockSpec((B, 1, M_HEADS * M_DK, M_DV), rev4),
         pl.BlockSpec((B, 1, 8, 128), rev4),
         pl.BlockSpec((B, CHUNK, 1024), rev(0)),
         pl.BlockSpec(memory_space=pl.ANY)],
        [pl.BlockSpec((B, CHUNK, 1024), rev(0)),
         pl.BlockSpec((B, CHUNK, 1024), rev(P_V // 1024)),
         pl.BlockSpec((B, CHUNK, 128), rev(0))],
        [jax.ShapeDtypeStruct((B, lp, 1024), F32),
         jax.ShapeDtypeStruct((B, lp, PW), BF16),
         jax.ShapeDtypeStruct((B, lp, 128), BF16)],
        [pltpu.VMEM((M_DK, M_DV), F32)] * (B * M_HEADS) + [pltpu.VMEM((8, 128), F32)] * (B * M_HEADS),
        (qk3, qk3, p3, p3, cst, nm, dh3, dp.reshape(B, lp, PW)), comm=comm, aliases={7: 1})
    return dqk.reshape(T, 1024), dv.reshape(T, PW), dif.reshape(T, 128), comm_res


def _headnorm_fwd(hm, p, g):
    T = hm.shape[0]
    tr = _pick(T, (384, 320, 256, 128, 64))

    def body(h_ref, o_ref, g_ref, a_ref):
        for hd in range(M_HEADS):
            sl = slice(hd * M_DV, (hd + 1) * M_DV)
            hn = _ln_rows(h_ref[:, sl], g_ref[:, sl], 0.0)
            a_ref[:, sl] = (_sigmoid(o_ref[:, sl]) * hn).astype(BF16)

    return pl.pallas_call(
        body, name="headnorm_fwd", grid=(T // tr,),
        in_specs=[pl.BlockSpec((tr, 1024), lambda i: (i, 0)),
                  pl.BlockSpec((tr, 1024), lambda i: (i, P_O // 1024)),
                  pl.BlockSpec((1, 1024), lambda i: (0, 0))],
        out_specs=pl.BlockSpec((tr, 1024), lambda i: (i, 0)),
        out_shape=jax.ShapeDtypeStruct((T, 1024), BF16),
        compiler_params=_cparams(("arbitrary",)),
    )(hm, p, g)


def _headnorm_bwd(da, hm, p, g, dp):
    T = hm.shape[0]
    tr = _pick(T, (384, 320, 256, 128, 64))

    def body(da_ref, h_ref, o_ref, g_ref, dpin_ref, dh_ref, do_ref, dg_ref):
        @pl.when(pl.program_id(0) == 0)
        def _():
            dg_ref[...] = jnp.zeros_like(dg_ref)

        for hd in range(M_HEADS):
            sl = slice(hd * M_DV, (hd + 1) * M_DV)
            gg = g_ref[:, sl]
            so = _sigmoid(o_ref[:, sl])
            da = da_ref[:, sl]
            dhn = da * so
            dv, xhat = _ln_bwd_rows(dhn, h_ref[:, sl], gg)
            dh_ref[:, sl] = dv
            do_ref[:, sl] = (da * (xhat * gg) * so * (1.0 - so)).astype(BF16)
            dg_ref[0:1, sl] += jnp.sum(dhn * xhat, axis=0, keepdims=True)

    row = pl.BlockSpec((tr, 1024), lambda i: (i, 0))
    return pl.pallas_call(
        body, name="headnorm_bwd", grid=(T // tr,),
        in_specs=[row, row, pl.BlockSpec((tr, 1024), lambda i: (i, P_O // 1024)),
                  pl.BlockSpec((1, 1024), lambda i: (0, 0)), pl.BlockSpec(memory_space=pl.ANY)],
        out_specs=(row, pl.BlockSpec((tr, 1024), lambda i: (i, P_O // 1024)),
                   pl.BlockSpec((8, 1024), lambda i: (0, 0))),
        out_shape=(jax.ShapeDtypeStruct((T, 1024), F32), jax.ShapeDtypeStruct((T, PW), BF16),
                   jax.ShapeDtypeStruct((8, 1024), F32)),
        input_output_aliases={4: 1},
        compiler_params=_cparams(("arbitrary",)),
    )(da, hm, p, g, dp)


def _mix_fwd(z, ym, p):
    T = ym.shape[0]
    tr = _pick(T, (384, 320, 256, 128, 64))

    def body(z1_ref, z2_ref, ym_ref, gs_ref, gm_ref, o_ref):
        ys = z1_ref[...] * _sigmoid(z2_ref[...])
        o_ref[...] = (_sigmoid(gs_ref[...]) * ys + _sigmoid(gm_ref[...]) * ym_ref[...]).astype(BF16)

    def col(cb):
        return pl.BlockSpec((tr, 1024), lambda i: (i, cb))

    return pl.pallas_call(
        body, name="mix_fwd", grid=(T // tr,),
        in_specs=[col(0), col(1), col(0), col(P_GS // 1024), col(P_GM // 1024)],
        out_specs=col(0),
        out_shape=jax.ShapeDtypeStruct((T, 1024), BF16),
        compiler_params=_cparams(("arbitrary",)),
    )(z, z, ym, p, p)


def _mix_bwd(dmix, z, ym, p):
    T = ym.shape[0]
    tr = _pick(T, (384, 320, 256, 128, 64))

    def body(d_ref, z1_ref, z2_ref, ym_ref, gs_ref, gm_ref, dz_ref, dym_ref, dp_ref):
        d = d_ref[...]
        z1 = z1_ref[...]
        s2 = _sigmoid(z2_ref[...])
        ss = _sigmoid(gs_ref[...])
        sm = _sigmoid(gm_ref[...])
        ys = z1 * s2
        dys = d * ss
        dp_ref[:, 0:1024] = (d * ys * ss * (1.0 - ss)).astype(BF16)
        dp_ref[:, 1024:2048] = (d * ym_ref[...] * sm * (1.0 - sm)).astype(BF16)
        dym_ref[...] = (d * sm).astype(BF16)
        dz_ref[:, 0:1024] = (dys * s2).astype(BF16)
        dz_ref[:, 1024:2048] = (dys * z1 * s2 * (1.0 - s2)).astype(BF16)

    def col(cb):
        return pl.BlockSpec((tr, 1024), lambda i: (i, cb))

    o = jax.ShapeDtypeStruct((T, 1024), BF16)
    return pl.pallas_call(
        body, name="mix_bwd", grid=(T // tr,),
        in_specs=[col(0), col(0), col(1), col(0), col(P_GS // 1024), col(P_GM // 1024)],
        out_specs=(pl.BlockSpec((tr, 2048), lambda i: (i, 0)), col(0),
                   pl.BlockSpec((tr, 2048), lambda i: (i, P_GS // 2048))),
        out_shape=(jax.ShapeDtypeStruct((T, 2048), BF16), o, jax.ShapeDtypeStruct((T, PW), BF16)),
        compiler_params=_cparams(("arbitrary",)),
    )(dmix, z, z, ym, p, p)


def _adamw_math(w, g, m, v):
    m2 = ADAM_B1 * m + (1.0 - ADAM_B1) * g
    v2 = ADAM_B2 * v + (1.0 - ADAM_B2) * jnp.square(g)
    m_hat = m2 / (1.0 - ADAM_B1 ** ADAM_STEP)
    v_hat = v2 / (1.0 - ADAM_B2 ** ADAM_STEP)
    delta = -ADAM_LR * (m_hat / (jnp.sqrt(v_hat) + ADAM_EPS) + ADAM_WD * w)
    return delta, m2, v2


def _adamw_big(recv, w, m, v, name):
    R, C = w.shape
    tr, tc = _shard_tile(R, C)
    ns = recv.shape[0]

    def body(r_ref, w_ref, m_ref, v_ref, g_ref, d_ref, m2_ref, v2_ref):
        g = r_ref[0].astype(F32)
        for s in range(1, ns):
            g = g + r_ref[s].astype(F32)
        d, m2, v2 = _adamw_math(w_ref[...], g, m_ref[...], v_ref[...])
        g_ref[...] = g
        d_ref[...] = d
        m2_ref[...] = m2
        v2_ref[...] = v2

    blk = pl.BlockSpec((tr, tc), lambda i, j: (i, j))
    o = jax.ShapeDtypeStruct((R, C), F32)
    return pl.pallas_call(
        body, name=name, grid=(R // tr, C // tc),
        in_specs=[pl.BlockSpec((ns, tr, tc), lambda i, j: (0, i, j)), blk, blk, blk],
        out_specs=(blk,) * 4, out_shape=(o,) * 4,
        compiler_params=_cparams(("arbitrary", "arbitrary")),
    )(recv, w, m, v)


def _adamw_small(gs, ws, ms, vs):
    n = len(ws)
    shapes = [tuple(w.shape) for w in ws]

    def flat2(a):
        return a.reshape(1, -1) if a.ndim == 1 else a.reshape(-1, a.shape[-1])

    def body(*refs):
        g, w, m, v = refs[:n], refs[n:2 * n], refs[2 * n:3 * n], refs[3 * n:4 * n]
        d, m2, v2 = refs[4 * n:5 * n], refs[5 * n:6 * n], refs[6 * n:]
        for i in range(n):
            di, mi, vi = _adamw_math(w[i][...], g[i][...], m[i][...], v[i][...])
            d[i][...] = di
            m2[i][...] = mi
            v2[i][...] = vi

    args = [flat2(a) for a in list(gs) + list(ws) + list(ms) + list(vs)]
    vm = pl.BlockSpec(memory_space=pltpu.VMEM)
    outs = pl.pallas_call(
        body, name="adamw_small",
        in_specs=[vm] * (4 * n), out_specs=tuple([vm] * (3 * n)),
        out_shape=tuple(jax.ShapeDtypeStruct(a.shape, F32) for a in args[:n] * 3),
        compiler_params=pltpu.CompilerParams(vmem_limit_bytes=VMEM_LIMIT),
    )(*args)
    outs = [o.reshape(s) for o, s in zip(outs, shapes * 3)]
    return outs[:n], outs[n:2 * n], outs[2 * n:]


def _pack(arrs):
    parts = []
    for a in arrs:
        f = a.reshape(-1).astype(F32)
        n = -(-f.shape[0] // 1024) * 1024
        parts.append(jnp.pad(f, (0, n - f.shape[0])))
    return jnp.concatenate(parts).reshape(-1, 128)


def _unpack(pack, shapes):
    flat = pack.reshape(-1)
    out, off = [], 0
    for shp in shapes:
        n = math.prod(shp)
        out.append(flat[off:off + n].reshape(shp))
        off += -(-n // 1024) * 1024
    return out


def _cols_from_shards(g):
    return jnp.transpose(g, (1, 0, 2)).reshape(g.shape[1], -1)


def _cols_to_shards(w):
    R = w.shape[0]
    return jnp.transpose(w.reshape(R, N_DEV, -1), (1, 0, 2))


def _nat_to_aligned(w, axis):
    def sl(start, size):
        return lax.slice_in_dim(w, start, start + size, axis=axis)

    pads = [(0, 0)] * w.ndim
    pads[axis] = (0, PW - P_IF - 8)
    return jnp.concatenate([sl(N_V, 1024), sl(N_O, 1024), sl(N_GS, 1024), sl(N_GM, 1024), sl(N_Q, 1024),
                            sl(N_U, 512), jnp.pad(sl(N_I, 8), pads)], axis=axis)


def _aligned_to_nat(w, axis):
    def sl(start, size):
        return lax.slice_in_dim(w, start, start + size, axis=axis)

    return jnp.concatenate([sl(P_U, 512), sl(P_QK, 1024), sl(P_V, 1024), sl(P_O, 1024), sl(P_IF, 8),
                            sl(P_GS, 1024), sl(P_GM, 1024)], axis=axis)


def kernel(x, meta_tokens, ln0_g, ln0_b, w_in, b_in, qk_conv_w, qk_conv_b, s5_lambda_re, s5_lambda_im, s5_log_dt, s5_b_re, s5_b_im, s5_c_re, s5_c_im, s5_d, s5_w_glu, m_norm_g, m_w_out, w_o, ln1_g, ln1_b, w_up, b_up, w_down, ln2_g, ln2_b, loss_target, m_meta_tokens, m_ln0_g, m_ln0_b, m_w_in, m_b_in, m_qk_conv_w, m_qk_conv_b, m_s5_lambda_re, m_s5_lambda_im, m_s5_log_dt, m_s5_b_re, m_s5_b_im, m_s5_c_re, m_s5_c_im, m_s5_d, m_s5_w_glu, m_m_norm_g, m_m_w_out, m_w_o, m_ln1_g, m_ln1_b, m_w_up, m_b_up, m_w_down, m_ln2_g, m_ln2_b, v_meta_tokens, v_ln0_g, v_ln0_b, v_w_in, v_b_in, v_qk_conv_w, v_qk_conv_b, v_s5_lambda_re, v_s5_lambda_im, v_s5_log_dt, v_s5_b_re, v_s5_b_im, v_s5_c_re, v_s5_c_im, v_s5_d, v_s5_w_glu, v_m_norm_g, v_m_w_out, v_w_o, v_ln1_g, v_ln1_b, v_w_up, v_b_up, v_w_down, v_ln2_g, v_ln2_b):
    B, S, D = x.shape
    lp = S + CHUNK
    me = _my_id()

    ln0g, ln0b = ln0_g.reshape(1, D), ln0_b.reshape(1, D)
    (h0, h0b), first = _ln0_fwd(x, ln0g, ln0b, comm=(_Gather, [w_in[0].T.astype(BF16), meta_tokens, qk_conv_w[0]]))
    win_t = _nat_to_aligned(first[0].reshape(IN_NAT, D), 0)
    meta_f = _cols_from_shards(first[1])
    convw_f = _cols_from_shards(first[2])
    later = [a.astype(BF16) for a in (s5_w_glu[0], m_w_out[0], w_o[0])]
    latest = [a.astype(BF16) for a in (w_up[0], w_down[0])]
    b_in_al = _nat_to_aligned(b_in, 1)

    s5_args = (s5_lambda_re[0], s5_lambda_im[0], s5_log_dt[0], s5_b_re[0], s5_b_im[0], s5_c_re[0], s5_c_im[0])
    (ar, ai, bblk, cblk), s5_vjp = jax.vjp(_s5_prep, *s5_args)
    seg = _s5_tile(lp) // 8
    mu_r, mu_i = ar, ai
    for _ in range(seg - 1):
        mu_r, mu_i = mu_r * ar - mu_i * ai, mu_r * ai + mu_i * ar
    tab_f, tab_b = _scan_tables(mu_r, mu_i)
    lam8 = jnp.stack([jnp.broadcast_to(ar, (8, 2048)), jnp.broadcast_to(ai, (8, 2048))])

    h0, h0b = _ln0_head(meta_f, ln0g, ln0b, h0, h0b, lp)
    p, *gathered = _mm(h0b, win_t, "nt", "mm_in", bias=b_in_al, comm=(_Gather, later))
    wglu_f = _cols_from_shards(gathered[0])
    wmo_f = gathered[1].reshape(1024, 1024)
    wo_f = gathered[2].reshape(1024, 1024)
    (y_pre, gy, cin), gathered2 = _s5_fwd(p, bblk, cblk, lam8, tab_f, s5_d, lp, comm=(_Gather, latest))
    wup_f = _cols_from_shards(gathered2[0])
    wdown_f = gathered2[1].reshape(D_FF, 1024)
    z = _mm(gy, wglu_f, "nn", "mm_glu")
    qk = _conv_fwd(p, convw_f, qk_conv_b, lp)
    hm, cst, nm = _mlstm_fwd(p, qk, lp)
    a_m = _headnorm_fwd(hm, p, m_norm_g)
    ym = _mm(a_m, wmo_f, "nn", "mm_mout")
    mix = _mix_fwd(z, ym, p)
    r1 = _mm(mix, wo_f, "nn", "mm_o")
    pre1, h1, h1b = _ln_res_fwd(h0, r1, ln1_g, ln1_b, "ln1_fwd")
    act = _mm(h1b, wup_f, "nn", "mm_up", bias=b_up, relu2_out=True, out_dtype=BF16)
    ff = _mm(act, wdown_f, "nn", "mm_down")
    loss_acc, dpre2, dpre2b, dg2, db2 = _ln2_loss(h1, ff, ln2_g, ln2_b, loss_target)

    d_up = _mm(dpre2b, wdown_f, "nt", "mm_d_act", sqrt_gate_of=act, out_dtype=BF16)
    g_wdown = _mm(act, dpre2b, "tn", "mm_g_wdown", out_dtype=BF16)
    g_wup, cs_up = _mm(h1b, d_up, "tn", "mm_g_wup", colsum="b", out_dtype=BF16)
    dh1 = _mm(d_up, wup_f, "nt", "mm_d_h1", add=dpre2, add_scale=ALPHA)
    dpre1, dpre1b, dg1, db1 = _ln_bwd(dh1, pre1, ln1_g, "ln1_bwd")
    g_wo = _mm(mix, dpre1b, "tn", "mm_g_wo", out_dtype=BF16)
    dmix = _mm(dpre1b, wo_f, "nt", "mm_d_mix")
    dz, dym, dp = _mix_bwd(dmix, z, ym, p)
    g_wmo = _mm(a_m, dym, "tn", "mm_g_wmo", out_dtype=BF16)
    da = _mm(dym, wmo_f, "nt", "mm_d_a")
    dhm, dp, dg_norm = _headnorm_bwd(da, hm, p, m_norm_g, dp)
    early = [g_wdown.reshape(N_DEV, 512, 1024), _cols_to_shards(g_wup), g_wo.reshape(N_DEV, 128, 1024),
             g_wmo.reshape(N_DEV, 128, 1024)]
    dqk, dp, dif, recv_early = _mlstm_bwd(p, qk, cst, nm, dhm, dp, lp, comm=(_AllToAll, early))
    dp, dconv_w, dconv_b = _conv_bwd(p, dqk, convw_f, qk_conv_b, dp, lp)
    g_wglu = _mm(gy, dz, "tn", "mm_g_wglu", out_dtype=BF16)
    dgy = _mm(dz, wglu_f, "nt", "mm_d_gy")
    (dp, dbblk, dcblk, dlam, dd), recv_glu = _s5_bwd(p, y_pre, dgy, cin, bblk, cblk, lam8, tab_f, tab_b, s5_d, dif, dp,
                                                     lp, comm=(_AllToAll, [_cols_to_shards(g_wglu)]))
    g_win_t, cs_in = _mm(dp, h0b, "tn", "mm_g_win", colsum="a")
    g_win8 = _aligned_to_nat(g_win_t, 0).astype(BF16).reshape(N_DEV, IN_NAT // N_DEV, D)
    dh0, recv_win = _mm(dp, win_t, "nn", "mm_d_h0", add=dpre1, add_scale=ALPHA, comm=(_AllToAll, [g_win8]))
    grad_x, dmeta, dg0, db0 = _ln0_bwd(dh0, x, meta_f, ln0g)

    dlam2 = jnp.sum(dlam, axis=1)
    s5_grads = s5_vjp((dlam2[0:1], dlam2[1:2], dbblk, dcblk))
    small_local = [
        loss_acc[0:1, 0:1], dg0[0:1], db0[0:1], _aligned_to_nat(cs_in[0:1], 1), dconv_b[0:1],
        s5_grads[0], s5_grads[1], s5_grads[2], s5_grads[3], s5_grads[4], s5_grads[5], s5_grads[6],
        dd[0:1], dg_norm[0:1], dg1[0:1], db1[0:1], cs_up[0:1], dg2[0:1], db2[0:1],
        dmeta, dconv_w[0:4]]
    small_shapes = [(), (D,), (D,), (1, IN_NAT), (1, 1024),
                    (1, 32, 64), (1, 32, 64), (1, 32), (1, 32, 64, 16), (1, 32, 64, 16), (1, 32, 16, 64), (1, 32, 16, 64),
                    (1, 512), (1, 1024), (1, 1024), (1, 1024), (1, D_FF), (1, 1024), (1, 1024),
                    (N_META, D), (4, 1024)]
    red = _unpack(_allreduce_small(_pack(small_local)), small_shapes)
    loss = red[0]
    g_meta = lax.dynamic_slice_in_dim(red[19], me * 128, 128, axis=1)
    g_convw = lax.dynamic_slice_in_dim(red[20], me * 128, 128, axis=1)[None]
    small_g = red[1:19] + [g_meta, g_convw]
    small_w = [ln0_g, ln0_b, b_in, qk_conv_b, s5_lambda_re, s5_lambda_im, s5_log_dt, s5_b_re, s5_b_im,
               s5_c_re, s5_c_im, s5_d, m_norm_g, ln1_g, ln1_b, b_up, ln2_g, ln2_b, meta_tokens, qk_conv_w]
    small_m = [m_ln0_g, m_ln0_b, m_b_in, m_qk_conv_b, m_s5_lambda_re, m_s5_lambda_im, m_s5_log_dt, m_s5_b_re,
               m_s5_b_im, m_s5_c_re, m_s5_c_im, m_s5_d, m_m_norm_g, m_ln1_g, m_ln1_b, m_b_up, m_ln2_g, m_ln2_b,
               m_meta_tokens, m_qk_conv_w]
    small_v = [v_ln0_g, v_ln0_b, v_b_in, v_qk_conv_b, v_s5_lambda_re, v_s5_lambda_im, v_s5_log_dt, v_s5_b_re,
               v_s5_b_im, v_s5_c_re, v_s5_c_im, v_s5_d, v_m_norm_g, v_ln1_g, v_ln1_b, v_b_up, v_ln2_g, v_ln2_b,
               v_meta_tokens, v_qk_conv_w]
    shapes_w = [tuple(w.shape) for w in small_w]
    small_g = [g.reshape(s) for g, s in zip(small_g, shapes_w)]
    sd, sm2, sv2 = _adamw_small(small_g, small_w, small_m, small_v)

    names = ["w_in", "s5_w_glu", "m_w_out", "w_o", "w_up", "w_down"]
    recv = [recv_win, recv_glu[0], recv_early[3], recv_early[2], recv_early[1], recv_early[0]]
    big_m = [m_w_in[0].T, m_s5_w_glu[0], m_m_w_out[0], m_w_o[0], m_w_up[0], m_w_down[0]]
    big_v = [v_w_in[0].T, v_s5_w_glu[0], v_m_w_out[0], v_w_o[0], v_w_up[0], v_w_down[0]]
    big_w = [w_in[0].T, s5_w_glu[0], m_w_out[0], w_o[0], w_up[0], w_down[0]]
    big_out = [_adamw_big(r, w, m, v, "adamw_" + nm_) for r, w, m, v, nm_ in zip(recv, big_w, big_m, big_v, names)]
    big_out[0] = [o.T for o in big_out[0]]

    order = ["meta_tokens", "ln0_g", "ln0_b", "w_in", "b_in", "qk_conv_w", "qk_conv_b", "s5_lambda_re", "s5_lambda_im",
             "s5_log_dt", "s5_b_re", "s5_b_im", "s5_c_re", "s5_c_im", "s5_d", "s5_w_glu", "m_norm_g", "m_w_out", "w_o",
             "ln1_g", "ln1_b", "w_up", "b_up", "w_down", "ln2_g", "ln2_b"]
    small_names = ["ln0_g", "ln0_b", "b_in", "qk_conv_b", "s5_lambda_re", "s5_lambda_im", "s5_log_dt", "s5_b_re",
                   "s5_b_im", "s5_c_re", "s5_c_im", "s5_d", "m_norm_g", "ln1_g", "ln1_b", "b_up", "ln2_g", "ln2_b",
                   "meta_tokens", "qk_conv_w"]
    res = {}
    for i, n in enumerate(small_names):
        res[n] = (small_g[i], sd[i], sm2[i], sv2[i])
    for i, n in enumerate(names):
        res[n] = tuple(o[None] for o in big_out[i])
    outs = [loss, grad_x]
    for kind in range(4):
        outs += [res[n][kind] for n in order]
    return tuple(outs)
```

```python
import functools
import math

import jax
import jax.numpy as jnp
from jax import lax
from jax.experimental import pallas as pl
from jax.experimental.pallas import tpu as pltpu

F32 = jnp.float32
BF16 = jnp.bfloat16

D_MODEL = 1024
N_META = 16
CHUNK = 128
PAD_ROWS = CHUNK - N_META
S5_WIDTH = 512
S5_GROUP = 16
S5_GROUPS = 32
S5_STATE = 64
S5_COLS = 2 * S5_GROUPS * S5_STATE
S5_BLK = 4
M_HEADS = 4
M_DK = 128
M_DV = 256
D_FF = 4096
N_DEV = 8
ALPHA = 2.0 ** 0.25
LN_EPS = 1e-5
IN_NAT = 5640
P_V, P_O, P_GS, P_GM, P_QK, P_U, P_IF, PW = 0, 1024, 2048, 3072, 4096, 5120, 5632, 5760
N_U, N_Q, N_K, N_V, N_O, N_I, N_GS, N_GM = 0, 512, 1024, 1536, 2560, 3584, 3592, 4616

ADAM_LR, ADAM_B1, ADAM_B2, ADAM_EPS, ADAM_WD, ADAM_STEP = 0.001, 0.9, 0.999, 1e-08, 0.01, 10

VMEM_LIMIT = 56 * 1024 * 1024
MXU_WIDTH = 256
MESH = pl.DeviceIdType.MESH


def _pick(n, cands):
    for c in cands:
        if n % c == 0:
            return c
    raise ValueError(f"no tile for {n} among {cands}")


def _cparams(sem):
    return pltpu.CompilerParams(dimension_semantics=sem, vmem_limit_bytes=VMEM_LIMIT)


def _dot(a, b, ca, cb):
    return lax.dot_general(a, b, (((ca,), (cb,)), ((), ())), preferred_element_type=F32)


def _sigmoid(x):
    return 1.0 / (1.0 + jnp.exp(-x))


def _peer(k):
    x, y, c = lax.axis_index("x"), lax.axis_index("y"), lax.axis_index("c")
    px = 1 - x if k & 4 else x
    py = 1 - y if k & 2 else y
    pc = 1 - c if k & 1 else c
    return (px, py, pc), 4 * px + 2 * py + pc


def _my_id():
    return 4 * lax.axis_index("x") + 2 * lax.axis_index("y") + lax.axis_index("c")


def _hbm_call(body, name, arrs, out_shape, n_remote):
    n = len(arrs)
    return pl.pallas_call(
        body, name=name,
        out_shape=tuple(out_shape),
        in_specs=[pl.BlockSpec(memory_space=pl.ANY)] * n,
        out_specs=tuple([pl.BlockSpec(memory_space=pl.ANY)] * len(out_shape)),
        scratch_shapes=[pltpu.SemaphoreType.DMA((n, n_remote)),
                        pltpu.SemaphoreType.DMA((n, n_remote)),
                        pltpu.SemaphoreType.DMA((n,))],
    )(*arrs)


class _Gather:
    def __init__(self, ins, outs, send_sems, recv_sems, local_sems):
        self.ins, self.outs, self.n = ins, outs, len(ins)
        self.send_sems, self.recv_sems, self.local_sems = send_sems, recv_sems, local_sems
        x, y, c = lax.axis_index("x"), lax.axis_index("y"), lax.axis_index("c")
        self.c = c
        self.me, self.sibling = (x, y, c), (x, y, 1 - c)
        self.chips = [(1 - x, y), (x, 1 - y), (1 - x, 1 - y)]

    def slot(self, a, dev):
        return self.outs[a].at[4 * dev[0] + 2 * dev[1] + dev[2]]

    def copy(self, a, k, block, to, own=False):
        return pltpu.make_async_remote_copy(
            src_ref=self.ins[a] if own else self.slot(a, block), dst_ref=self.slot(a, block),
            send_sem=self.send_sems.at[a, k], recv_sem=self.recv_sems.at[a, k],
            device_id=to, device_id_type=MESH)

    def first_sends(self, a):
        return [self.copy(a, 0, self.me, self.sibling, own=True)] + [
            self.copy(a, 1 + j, self.me, (*chip, self.c), own=True) for j, chip in enumerate(self.chips)]

    def start(self):
        for a in range(self.n):
            pltpu.make_async_copy(self.ins[a], self.slot(a, self.me), self.local_sems.at[a]).start()
            for cp in self.first_sends(a):
                cp.start()

    def finish(self):
        forwards = []
        for j, chip in enumerate(self.chips):
            for a in range(self.n):
                self.copy(a, 1 + j, (*chip, self.c), self.me).wait_recv()
                fwd = self.copy(a, 4 + j, (*chip, self.c), self.sibling)
                fwd.start()
                forwards.append(fwd)
        for a in range(self.n):
            self.copy(a, 0, self.sibling, self.me).wait_recv()
            for j, chip in enumerate(self.chips):
                self.copy(a, 4 + j, (*chip, 1 - self.c), self.me).wait_recv()
        for a in range(self.n):
            for cp in self.first_sends(a):
                cp.wait_send()
            pltpu.make_async_copy(self.ins[a], self.slot(a, self.me), self.local_sems.at[a]).wait()
        for cp in forwards:
            cp.wait_send()

    @staticmethod
    def out_shapes(arrs):
        return [jax.ShapeDtypeStruct((N_DEV,) + tuple(a.shape), a.dtype) for a in arrs]


class _AllToAll:
    def __init__(self, ins, outs, send_sems, recv_sems, local_sems):
        self.ins, self.outs, self.n = ins, outs, len(ins)
        self.send_sems, self.recv_sems, self.local_sems = send_sems, recv_sems, local_sems
        self.me = _my_id()

    def copy(self, a, k, landing):
        peer, pid = _peer(k)
        return pltpu.make_async_remote_copy(
            src_ref=self.ins[a].at[pid], dst_ref=self.outs[a].at[pid if landing else self.me],
            send_sem=self.send_sems.at[a, k - 1], recv_sem=self.recv_sems.at[a, k - 1],
            device_id=peer, device_id_type=MESH)

    def local(self, a):
        return pltpu.make_async_copy(self.ins[a].at[self.me], self.outs[a].at[self.me], self.local_sems.at[a])

    def start(self):
        for a in range(self.n):
            self.local(a).start()
            for k in range(1, N_DEV):
                self.copy(a, k, False).start()

    def finish(self):
        for a in range(self.n):
            for k in range(1, N_DEV):
                self.copy(a, k, True).wait_recv()
        for a in range(self.n):
            for k in range(1, N_DEV):
                self.copy(a, k, False).wait_send()
            self.local(a).wait()

    @staticmethod
    def out_shapes(arrs):
        return [jax.ShapeDtypeStruct(tuple(a.shape), a.dtype) for a in arrs]


def _comm_scratch(n):
    return [pltpu.SemaphoreType.DMA((n, N_DEV - 1)), pltpu.SemaphoreType.DMA((n, N_DEV - 1)),
            pltpu.SemaphoreType.DMA((n,))]


def _ride_call(body, name, grid, in_specs, out_specs, out_shape, scratch, args, comm=None, aliases=None):
    n_in, n_out = len(in_specs), len(out_specs)
    in_specs, out_specs, out_shape = list(in_specs), list(out_specs), list(out_shape)
    scratch, args = list(scratch), list(args)
    kernel_fn = body
    if comm is not None:
        cls, arrs = comm
        n = len(arrs)

        def kernel_fn(*refs):
            ins, cin = refs[:n_in], refs[n_in:n_in + n]
            outs = refs[n_in + n:n_in + n + n_out]
            cout = refs[n_in + n + n_out:n_in + 2 * n + n_out]
            own_scratch, sems = refs[n_in + 2 * n + n_out:-3], refs[-3:]
            exchange = cls(cin, cout, *sems)
            ids = [pl.program_id(d) for d in range(len(grid))]
            first = functools.reduce(lambda a, b: a & b, [i == 0 for i in ids])
            last = functools.reduce(lambda a, b: a & b, [i == g - 1 for i, g in zip(ids, grid)])

            @pl.when(first)
            def _():
                exchange.start()

            body(*ins, *outs, *own_scratch)

            @pl.when(last)
            def _():
                exchange.finish()

        in_specs += [pl.BlockSpec(memory_space=pl.ANY)] * n
        args += list(arrs)
        out_specs += [pl.BlockSpec(memory_space=pl.ANY)] * n
        out_shape += cls.out_shapes(arrs)
        scratch += _comm_scratch(n)
    res = pl.pallas_call(
        kernel_fn, name=name, grid=grid,
        in_specs=in_specs, out_specs=tuple(out_specs), out_shape=tuple(out_shape),
        scratch_shapes=scratch, input_output_aliases=aliases or {},
        compiler_params=_cparams(("arbitrary",) * len(grid)),
    )(*args)
    return list(res[:n_out]), list(res[n_out:])


def _shard_tile(R, C):
    if R % 128 == 0:
        return 128, C
    return R, _pick(C, (256, 128))


def _allreduce_small(pack):
    rows = pack.shape[0]

    def gather(*refs):
        g = _Gather(refs[:1], refs[1:2], *refs[2:])
        g.start()
        g.finish()

    slots = _hbm_call(gather, "allreduce_gather", [pack], _Gather.out_shapes([pack]), N_DEV - 1)[0]

    def body(s_ref, o_ref):
        acc = s_ref[0]
        for s in range(1, N_DEV):
            acc = acc + s_ref[s]
        o_ref[...] = acc

    return pl.pallas_call(
        body, name="allreduce_sum", grid=(1,),
        in_specs=[pl.BlockSpec((N_DEV, rows, 128), lambda i: (0, 0, 0))],
        out_specs=pl.BlockSpec((rows, 128), lambda i: (0, 0)),
        out_shape=jax.ShapeDtypeStruct((rows, 128), F32),
        compiler_params=_cparams(("arbitrary",)),
    )(slots)


def _mm_tiles(M, N, K, mode, out_bytes, n_extra_f32, a_bytes, b_bytes):
    budget = 40 * 1024 * 1024
    tms = [t for t in (1664, 1408, 1152, 1024, 640, 512, 256, 128) if M % t == 0]
    tns = [t for t in (1152, 1024, 640, 512, 384, 256, 128) if N % t == 0]
    if mode == "tn":
        tks = [t for t in (1664, 1408, 640, 512, 256, 128) if K % t == 0]
    else:
        tks = [K] if K <= 1152 else [t for t in (1152, 1024, 640, 512) if K % t == 0]
    best = None
    for tm in tms:
        for tn in tns:
            for tk in tks:
                nk = K // tk
                use = 2 * (tm * tk * a_bytes + tk * tn * b_bytes) + 2 * tm * tn * out_bytes
                use += 2 * n_extra_f32 * tm * tn * 4 + tm * tn * 4 * (2 if nk > 1 else 1)
                if use > budget:
                    continue
                score = (tm * tn * tk, tm * tn)
                if best is None or score > best[0]:
                    best = (score, (tm, tn, tk))
    assert best is not None, (M, N, K, mode)
    return best[1]


def _mm(a, b, mode, name, *, bias=None, add=None, add_scale=1.0, sqrt_gate_of=None,
        relu2_out=False, colsum=None, out_dtype=F32, comm=None):
    if mode == "nn":
        (M, K), (K2, N) = a.shape, b.shape
    elif mode == "nt":
        (M, K), (N, K2) = a.shape, b.shape
    else:
        (K, M), (K2, N) = a.shape, b.shape
    assert K == K2, (a.shape, b.shape, mode)
    has_bias, has_add, has_gate = bias is not None, add is not None, sqrt_gate_of is not None
    tm, tn, tk = _mm_tiles(M, N, K, mode, jnp.dtype(out_dtype).itemsize, int(has_add) + int(has_gate),
                           a.dtype.itemsize, b.dtype.itemsize)
    nk = K // tk
    assert colsum is None or mode == "tn"
    assert colsum != "a" or N == tn
    comm_cls, comm_arrs = comm if comm is not None else (None, [])
    nc = len(comm_arrs)
    grid = (N // tn, M // tm, nk)

    def body(*refs):
        it = iter(refs)
        a_ref, b_ref = next(it), next(it)
        bias_ref = next(it) if has_bias else None
        add_ref = next(it) if has_add else None
        gate_ref = next(it) if has_gate else None
        comm_ins = [next(it) for _ in range(nc)]
        o_ref = next(it)
        cs_ref = next(it) if colsum else None
        comm_outs = [next(it) for _ in range(nc)]
        acc_ref = next(it) if nk > 1 else None
        j, i, k = pl.program_id(0), pl.program_id(1), pl.program_id(2)
        if nc:
            exchange = comm_cls(comm_ins, comm_outs, next(it), next(it), next(it))

            @pl.when((j == 0) & (i == 0) & (k == 0))
            def _():
                exchange.start()

        blocks = [slice(n0, min(n0 + MXU_WIDTH, tn)) for n0 in range(0, tn, MXU_WIDTH)]
        av = a_ref[...].astype(BF16)

        def part(cols):
            if mode == "nn":
                return _dot(av, b_ref[:, cols].astype(BF16), 1, 0)
            if mode == "nt":
                return _dot(av, b_ref[cols, :].astype(BF16), 1, 1)
            return _dot(av, b_ref[:, cols].astype(BF16), 0, 0)

        if colsum == "b":
            @pl.when((i == 0) & (k == 0))
            def _():
                cs_ref[...] = jnp.zeros_like(cs_ref)

            @pl.when(i == 0)
            def _():
                cs_ref[0:1, :] += jnp.sum(b_ref[...].astype(F32), axis=0, keepdims=True)
        if colsum == "a":
            @pl.when(k == 0)
            def _():
                cs_ref[...] = jnp.zeros_like(cs_ref)

            cs_ref[0:1, :] += jnp.sum(a_ref[...].astype(F32), axis=0, keepdims=True)

        def finish(cols, r):
            if has_bias:
                r = r + bias_ref[:, cols]
            if has_add:
                r = r + add_scale * add_ref[:, cols]
            if has_gate:
                r = r * (2.0 * jnp.sqrt(gate_ref[:, cols].astype(F32)))
            if relu2_out:
                r = jnp.square(jnp.maximum(r, 0.0))
            o_ref[:, cols] = r.astype(out_dtype)

        if nk == 1:
            for cols in blocks:
                finish(cols, part(cols))
        else:
            @pl.when(k == 0)
            def _():
                for cols in blocks:
                    acc_ref[:, cols] = part(cols)

            @pl.when((k > 0) & (k < nk - 1))
            def _():
                for cols in blocks:
                    acc_ref[:, cols] += part(cols)

            @pl.when(k == nk - 1)
            def _():
                for cols in blocks:
                    finish(cols, acc_ref[:, cols] + part(cols))

        if nc:
            @pl.when((j == grid[0] - 1) & (i == grid[1] - 1) & (k == nk - 1))
            def _():
                exchange.finish()

    if mode == "nn":
        a_spec = pl.BlockSpec((tm, tk), lambda j, i, k: (i, k))
        b_spec = pl.BlockSpec((tk, tn), lambda j, i, k: (k, j))
    elif mode == "nt":
        a_spec = pl.BlockSpec((tm, tk), lambda j, i, k: (i, k))
        b_spec = pl.BlockSpec((tn, tk), lambda j, i, k: (j, k))
    else:
        a_spec = pl.BlockSpec((tk, tm), lambda j, i, k: (k, i))
        b_spec = pl.BlockSpec((tk, tn), lambda j, i, k: (k, j))
    in_specs, args = [a_spec, b_spec], [a, b]
    if has_bias:
        in_specs.append(pl.BlockSpec((1, tn), lambda j, i, k: (0, j)))
        args.append(bias)
    if has_add:
        in_specs.append(pl.BlockSpec((tm, tn), lambda j, i, k: (i, j)))
        args.append(add)
    if has_gate:
        in_specs.append(pl.BlockSpec((tm, tn), lambda j, i, k: (i, j)))
        args.append(sqrt_gate_of)
    out_shape = [jax.ShapeDtypeStruct((M, N), out_dtype)]
    out_specs = [pl.BlockSpec((tm, tn), lambda j, i, k: (i, j))]
    if colsum == "b":
        out_shape.append(jax.ShapeDtypeStruct((8, N), F32))
        out_specs.append(pl.BlockSpec((8, tn), lambda j, i, k: (0, j)))
    if colsum == "a":
        out_shape.append(jax.ShapeDtypeStruct((8, M), F32))
        out_specs.append(pl.BlockSpec((8, tm), lambda j, i, k: (0, i)))
    scratch = [pltpu.VMEM((tm, tn), F32)] if nk > 1 else []
    if nc:
        in_specs += [pl.BlockSpec(memory_space=pl.ANY)] * nc
        args += list(comm_arrs)
        out_specs += [pl.BlockSpec(memory_space=pl.ANY)] * nc
        out_shape += comm_cls.out_shapes(comm_arrs)
        scratch += _comm_scratch(nc)
    res = pl.pallas_call(
        body, name=name, grid=grid,
        in_specs=in_specs, out_specs=tuple(out_specs), out_shape=tuple(out_shape),
        scratch_shapes=scratch,
        compiler_params=_cparams(("arbitrary", "arbitrary", "arbitrary")),
    )(*args)
    return res if len(res) > 1 else res[0]


def _ln_rows(v, g, b):
    mu = jnp.mean(v, axis=-1, keepdims=True)
    xc = v - mu
    var = jnp.mean(xc * xc, axis=-1, keepdims=True)
    return xc * lax.rsqrt(var + LN_EPS) * g + b


def _ln_bwd_rows(dy, v, g):
    mu = jnp.mean(v, axis=-1, keepdims=True)
    xc = v - mu
    var = jnp.mean(xc * xc, axis=-1, keepdims=True)
    rstd = lax.rsqrt(var + LN_EPS)
    xhat = xc * rstd
    dxh = dy * g
    dv = rstd * (dxh - jnp.mean(dxh, axis=-1, keepdims=True)
                 - xhat * jnp.mean(dxh * xhat, axis=-1, keepdims=True))
    return dv, xhat


def _real_tile(S):
    return _pick(S, (512, 256, 128, 64))


def _real_rows(rb, ncols, lp):
    return pl.BlockSpec((pl.Element(rb), pl.Element(ncols)),
                        lambda bb, j: (pl.multiple_of(bb * lp + CHUNK + j * rb, CHUNK), 0))


def _head_rows(ncols, lp):
    return pl.BlockSpec((CHUNK, ncols), lambda bb: (bb * (lp // CHUNK), 0))


def _ln0_fwd(x, g, b, comm=None):
    B, S, D = x.shape
    lp = S + CHUNK
    rb = _real_tile(S)

    def body(x_ref, g_ref, b_ref, h_ref, hb_ref):
        y = _ln_rows(x_ref[0], g_ref[...], b_ref[...])
        h_ref[...] = y
        hb_ref[...] = y.astype(BF16)

    vec = pl.BlockSpec((1, D), lambda bb, j: (0, 0))
    return _ride_call(
        body, "ln0_fwd", (B, S // rb),
        [pl.BlockSpec((1, rb, D), lambda bb, j: (bb, j, 0)), vec, vec],
        [_real_rows(rb, D, lp), _real_rows(rb, D, lp)],
        [jax.ShapeDtypeStruct((B * lp, D), F32), jax.ShapeDtypeStruct((B * lp, D), BF16)],
        [], (x, g, b), comm=comm)


def _ln0_head(meta, g, b, h, hb, lp):
    D = meta.shape[1]
    B = h.shape[0] // lp

    def head(meta_ref, g_ref, b_ref, hin_ref, hbin_ref, h_ref, hb_ref):
        m = _ln_rows(meta_ref[...], g_ref[...], b_ref[...])
        h_ref[0:PAD_ROWS, :] = jnp.zeros((PAD_ROWS, D), F32)
        h_ref[PAD_ROWS:CHUNK, :] = m
        hb_ref[0:PAD_ROWS, :] = jnp.zeros((PAD_ROWS, D), BF16)
        hb_ref[PAD_ROWS:CHUNK, :] = m.astype(BF16)

    vec1 = pl.BlockSpec((1, D), lambda bb: (0, 0))
    anyspec = pl.BlockSpec(memory_space=pl.ANY)
    return pl.pallas_call(
        head, name="ln0_head", grid=(B,),
        in_specs=[pl.BlockSpec((N_META, D), lambda bb: (0, 0)), vec1, vec1, anyspec, anyspec],
        out_specs=(_head_rows(D, lp), _head_rows(D, lp)),
        out_shape=(jax.ShapeDtypeStruct((B * lp, D), F32), jax.ShapeDtypeStruct((B * lp, D), BF16)),
        input_output_aliases={3: 0, 4: 1},
        compiler_params=_cparams(("arbitrary",)),
    )(meta, g, b, h, hb)


def _ln0_bwd(dh0, x, meta, g):
    B, S, D = x.shape
    lp = S + CHUNK
    rb = _real_tile(S)

    def body(dh_ref, x_ref, g_ref, dx_ref, dg_ref, db_ref):
        @pl.when((pl.program_id(0) == 0) & (pl.program_id(1) == 0))
        def _():
            dg_ref[...] = jnp.zeros_like(dg_ref)
            db_ref[...] = jnp.zeros_like(db_ref)

        dy = dh_ref[...]
        dv, xhat = _ln_bwd_rows(dy, x_ref[0], g_ref[...])
        dx_ref[0] = dv
        dg_ref[0:1, :] += jnp.sum(dy * xhat, axis=0, keepdims=True)
        db_ref[0:1, :] += jnp.sum(dy, axis=0, keepdims=True)

    const = lambda bb, j: (0, 0)
    xblk = pl.BlockSpec((1, rb, D), lambda bb, j: (bb, j, 0))
    acc_shape = jax.ShapeDtypeStruct((8, D), F32)
    dx, dg, db = pl.pallas_call(
        body, name="ln0_bwd", grid=(B, S // rb),
        in_specs=[_real_rows(rb, D, lp), xblk, pl.BlockSpec((1, D), const)],
        out_specs=(xblk, pl.BlockSpec((8, D), const), pl.BlockSpec((8, D), const)),
        out_shape=(jax.ShapeDtypeStruct((B, S, D), F32), acc_shape, acc_shape),
        compiler_params=_cparams(("arbitrary", "arbitrary")),
    )(dh0, x, g)

    def head(dh_ref, meta_ref, g_ref, dmeta_ref, dg_ref, db_ref):
        @pl.when(pl.program_id(0) == 0)
        def _():
            dmeta_ref[...] = jnp.zeros_like(dmeta_ref)
            dg_ref[...] = jnp.zeros_like(dg_ref)
            db_ref[...] = jnp.zeros_like(db_ref)

        dy = dh_ref[PAD_ROWS:CHUNK, :]
        dv, xhat = _ln_bwd_rows(dy, meta_ref[...], g_ref[...])
        dmeta_ref[...] += dv
        dg_ref[0:1, :] += jnp.sum(dy * xhat, axis=0, keepdims=True)
        db_ref[0:1, :] += jnp.sum(dy, axis=0, keepdims=True)

    c1 = lambda bb: (0, 0)
    dmeta, dgm, dbm = pl.pallas_call(
        head, name="ln0_bwd_head", grid=(B,),
        in_specs=[_head_rows(D, lp), pl.BlockSpec((N_META, D), c1), pl.BlockSpec((1, D), c1)],
        out_specs=(pl.BlockSpec((N_META, D), c1), pl.BlockSpec((8, D), c1), pl.BlockSpec((8, D), c1)),
        out_shape=(jax.ShapeDtypeStruct((N_META, D), F32), acc_shape, acc_shape),
        compiler_params=_cparams(("arbitrary",)),
    )(dh0, meta, g)
    return dx, dmeta, dg + dgm, db + dbm


def _ln_res_fwd(h_prev, r, g, b, name):
    T, D = h_prev.shape
    tr = _pick(T, (384, 320, 256, 128, 64))

    def body(hp_ref, r_ref, g_ref, b_ref, pre_ref, h_ref, hb_ref):
        pre = ALPHA * hp_ref[...] + r_ref[...]
        y = _ln_rows(pre, g_ref[...], b_ref[...])
        pre_ref[...] = pre
        h_ref[...] = y
        hb_ref[...] = y.astype(BF16)

    row = pl.BlockSpec((tr, D), lambda i: (i, 0))
    vec = pl.BlockSpec((1, D), lambda i: (0, 0))
    return pl.pallas_call(
        body, name=name, grid=(T // tr,),
        in_specs=[row, row, vec, vec], out_specs=(row, row, row),
        out_shape=(jax.ShapeDtypeStruct((T, D), F32), jax.ShapeDtypeStruct((T, D), F32),
                   jax.ShapeDtypeStruct((T, D), BF16)),
        compiler_params=_cparams(("arbitrary",)),
    )(h_prev, r, g, b)


def _ln_bwd(dh, pre, g, name):
    T, D = dh.shape
    tr = _pick(T, (384, 320, 256, 128, 64))

    def body(dh_ref, pre_ref, g_ref, dp_ref, dpb_ref, dg_ref, db_ref):
        @pl.when(pl.program_id(0) == 0)
        def _():
            dg_ref[...] = jnp.zeros_like(dg_ref)
            db_ref[...] = jnp.zeros_like(db_ref)

        dy = dh_ref[...]
        dv, xhat = _ln_bwd_rows(dy, pre_ref[...], g_ref[...])
        dp_ref[...] = dv
        dpb_ref[...] = dv.astype(BF16)
        dg_ref[0:1, :] += jnp.sum(dy * xhat, axis=0, keepdims=True)
        db_ref[0:1, :] += jnp.sum(dy, axis=0, keepdims=True)

    row = pl.BlockSpec((tr, D), lambda i: (i, 0))
    vec = pl.BlockSpec((1, D), lambda i: (0, 0))
    acc = pl.BlockSpec((8, D), lambda i: (0, 0))
    return pl.pallas_call(
        body, name=name, grid=(T // tr,),
        in_specs=[row, row, vec], out_specs=(row, row, acc, acc),
        out_shape=(jax.ShapeDtypeStruct((T, D), F32), jax.ShapeDtypeStruct((T, D), BF16),
                   jax.ShapeDtypeStruct((8, D), F32), jax.ShapeDtypeStruct((8, D), F32)),
        compiler_params=_cparams(("arbitrary",)),
    )(dh, pre, g)


def _ln2_loss(h1, ff, g, b, target):
    T, D = h1.shape
    B, S, _ = target.shape
    lp = S + CHUNK
    rb = _real_tile(S)

    def body(h_ref, ff_ref, g_ref, b_ref, t_ref, loss_ref, dp_ref, dpb_ref, dg_ref, db_ref):
        @pl.when((pl.program_id(0) == 0) & (pl.program_id(1) == 0))
        def _():
            loss_ref[...] = jnp.zeros_like(loss_ref)
            dg_ref[...] = jnp.zeros_like(dg_ref)
            db_ref[...] = jnp.zeros_like(db_ref)

        pre = ALPHA * h_ref[...] + ff_ref[...]
        gg = g_ref[...]
        y = _ln_rows(pre, gg, b_ref[...])
        err = y - t_ref[0]
        loss_ref[0:1, 0:1] += 0.5 * jnp.sum(jnp.mean(err * err, axis=-1, keepdims=True), axis=0, keepdims=True)
        dy = err * (1.0 / D)
        dv, xhat = _ln_bwd_rows(dy, pre, gg)
        dp_ref[...] = dv
        dpb_ref[...] = dv.astype(BF16)
        dg_ref[0:1, :] += jnp.sum(dy * xhat, axis=0, keepdims=True)
        db_ref[0:1, :] += jnp.sum(dy, axis=0, keepdims=True)

    row = _real_rows(rb, D, lp)
    const = lambda bb, j: (0, 0)
    loss, dp, dpb, dg, db = pl.pallas_call(
        body, name="ln2_loss", grid=(B, S // rb),
        in_specs=[row, row, pl.BlockSpec((1, D), const), pl.BlockSpec((1, D), const),
                  pl.BlockSpec((1, rb, D), lambda bb, j: (bb, j, 0))],
        out_specs=(pl.BlockSpec((8, 128), const), row, row,
                   pl.BlockSpec((8, D), const), pl.BlockSpec((8, D), const)),
        out_shape=(jax.ShapeDtypeStruct((8, 128), F32),
                   jax.ShapeDtypeStruct((T, D), F32), jax.ShapeDtypeStruct((T, D), BF16),
                   jax.ShapeDtypeStruct((8, D), F32), jax.ShapeDtypeStruct((8, D), F32)),
        compiler_params=_cparams(("arbitrary", "arbitrary")),
    )(h1, ff, g, b, target)

    def head(dpin_ref, dpbin_ref, dp_ref, dpb_ref):
        dp_ref[...] = jnp.zeros((CHUNK, D), F32)
        dpb_ref[...] = jnp.zeros((CHUNK, D), BF16)

    anyspec = pl.BlockSpec(memory_space=pl.ANY)
    dp, dpb = pl.pallas_call(
        head, name="ln2_head", grid=(B,),
        in_specs=[anyspec, anyspec],
        out_specs=(_head_rows(D, lp), _head_rows(D, lp)),
        out_shape=(jax.ShapeDtypeStruct((T, D), F32), jax.ShapeDtypeStruct((T, D), BF16)),
        input_output_aliases={0: 0, 1: 1},
        compiler_params=_cparams(("arbitrary",)),
    )(dp, dpb)
    return loss, dp, dpb, dg, db


def _s5_prep(lam_re, lam_im, log_dt, b_re, b_im, c_re, c_im):
    dt = jnp.exp(log_dt)[:, None]
    mag = jnp.exp(lam_re * dt)
    ar = mag * jnp.cos(lam_im * dt)
    ai = mag * jnp.sin(lam_im * dt)
    nr, ni = ar - 1.0, ai
    den = lam_re * lam_re + lam_im * lam_im
    cr = (nr * lam_re + ni * lam_im) / den
    ci = (ni * lam_re - nr * lam_im) / den
    bbr = cr[..., None] * b_re - ci[..., None] * b_im
    bbi = cr[..., None] * b_im + ci[..., None] * b_re
    eye = jnp.eye(8, dtype=F32)
    bb = jnp.stack([bbr, bbi]).reshape(2, S5_BLK, 8, S5_STATE, S5_GROUP)
    bblk = jnp.einsum("rbgph,gj->bghrjp", bb, eye).reshape(S5_BLK, 128, 1024)
    cc = jnp.stack([c_re, -c_im]).reshape(2, S5_BLK, 8, S5_GROUP, S5_STATE)
    cblk = jnp.einsum("rbghp,gj->brjpgh", cc, eye).reshape(S5_BLK, 1024, 128)
    return ar.reshape(1, 2048), ai.reshape(1, 2048), bblk, cblk


def _scan_tables(ar, ai):
    pr, pi = [ar], [ai]
    for _ in range(7):
        pr, pi = pr + [pr[-1] * ar - pi[-1] * ai], pi + [pr[-1] * ai + pi[-1] * ar]
    pw_r = jnp.concatenate(pr, axis=0)
    pw_i = jnp.concatenate(pi, axis=0)
    rev_r = jnp.concatenate(pr[::-1], axis=0)
    rev_i = jnp.concatenate(pi[::-1], axis=0)
    row = jnp.arange(8)[:, None]

    def tables(sign, reverse):
        rows = []
        for n, sh in ((0, 1), (1, 2), (3, 4)):
            mask = (row < 8 - sh) if reverse else (row >= sh)
            rows.append(jnp.where(mask, pw_r[n][None, :], 0.0))
            rows.append(jnp.where(mask, sign * pw_i[n][None, :], 0.0))
        cr_ = rev_r if reverse else pw_r
        ci_ = rev_i if reverse else pw_i
        rows += [cr_, sign * ci_]
        return jnp.stack(rows)

    return tables(1.0, False), tables(-1.0, True)


def _seg_scan(s_ref, row0, seg, lam_ref, tab_ref, carry_ref, reverse, cseg_ref=None, extra=None):
    sgn = -1.0 if reverse else 1.0
    take, edge = (0, 7) if reverse else (7, 0)
    rowid = lax.broadcasted_iota(jnp.int32, (8, 128), 0)
    all_pairs = [(blk * 1024 + j * 128, blk * 1024 + j * 128 + 512, blk * 512 + j * 128)
                 for blk in range(S5_BLK) for j in range(4)]

    def rows(it):
        i = (seg - 1 - it) if reverse else it
        return i, pl.multiple_of(row0 + i * 8, 8)

    for half in range(2):
        pairs = all_pairs[8 * half:8 * half + 8]

        def pass1(it, carry):
            _, r0 = rows(it)
            out = []
            for n, (cre, cim, tc) in enumerate(pairs):
                lr = lam_ref[0, :, tc:tc + 128]
                li = sgn * lam_ref[1, :, tc:tc + 128]
                pr, pi = carry[2 * n], carry[2 * n + 1]
                xr = lr * pr - li * pi + s_ref[pl.ds(r0, 8), cre:cre + 128]
                xi = lr * pi + li * pr + s_ref[pl.ds(r0, 8), cim:cim + 128]
                s_ref[pl.ds(r0, 8), cre:cre + 128] = xr
                s_ref[pl.ds(r0, 8), cim:cim + 128] = xi
                out += [xr, xi]
            return tuple(out)

        ends = lax.fori_loop(0, seg, pass1, tuple(jnp.zeros((8, 128), F32) for _ in range(16)))

        start = []
        for n, (cre, cim, tc) in enumerate(pairs):
            xr, xi = ends[2 * n], ends[2 * n + 1]
            for lvl, sh in enumerate((1, 2, 4)):
                lr = tab_ref[2 * lvl, :, tc:tc + 128]
                li = tab_ref[2 * lvl + 1, :, tc:tc + 128]
                shift = (8 - sh) if reverse else sh
                sr = pltpu.roll(xr, shift, 0)
                si = pltpu.roll(xi, shift, 0)
                xr, xi = xr + lr * sr - li * si, xi + lr * si + li * sr
            pr = tab_ref[6, :, tc:tc + 128]
            pi = tab_ref[7, :, tc:tc + 128]
            c_r = carry_ref[:, cre:cre + 128]
            c_i = carry_ref[:, cim:cim + 128]
            er, ei = xr + pr * c_r - pi * c_i, xi + pr * c_i + pi * c_r
            back = 7 if reverse else 1
            in_r = jnp.where(rowid == edge, c_r, pltpu.roll(er, back, 0))
            in_i = jnp.where(rowid == edge, c_i, pltpu.roll(ei, back, 0))
            nr = jnp.sum(jnp.where(rowid == take, er, 0.0), axis=0, keepdims=True)
            ni = jnp.sum(jnp.where(rowid == take, ei, 0.0), axis=0, keepdims=True)
            carry_ref[:, cre:cre + 128] = jnp.broadcast_to(nr, (8, 128))
            carry_ref[:, cim:cim + 128] = jnp.broadcast_to(ni, (8, 128))
            if cseg_ref is not None:
                cseg_ref[0:8, cre:cre + 128] = in_r
                cseg_ref[0:8, cim:cim + 128] = in_i
            start += [in_r, in_i]

        def pass2(it, carry):
            i, r0 = rows(it)
            out = []
            for n, (cre, cim, tc) in enumerate(pairs):
                lr = lam_ref[0, :, tc:tc + 128]
                li = sgn * lam_ref[1, :, tc:tc + 128]
                dr, di = carry[2 * n], carry[2 * n + 1]
                dr, di = lr * dr - li * di, lr * di + li * dr
                xr = s_ref[pl.ds(r0, 8), cre:cre + 128] + dr
                xi = s_ref[pl.ds(r0, 8), cim:cim + 128] + di
                s_ref[pl.ds(r0, 8), cre:cre + 128] = xr
                s_ref[pl.ds(r0, 8), cim:cim + 128] = xi
                if extra is not None:
                    extra(i, cre, cim, tc, xr, xi)
                out += [dr, di]
            return tuple(out)

        lax.fori_loop(0, seg, pass2, tuple(start))


def _to_segments(src_refs, dst_ref, seg, first=None):
    sub = lax.broadcasted_iota(jnp.int32, (8, 1), 0)
    for i in range(seg):
        for c, src in enumerate(src_refs):
            v = src[pl.ds(i, 8, stride=seg), :]
            if first is not None:
                v = jnp.where(first & (sub * seg + i < PAD_ROWS), 0.0, v)
            dst_ref[8 * i:8 * i + 8, c * 128:(c + 1) * 128] = v


def _from_segments(src_ref, dst_ref, seg, fn=None, zero_head=None):
    m = seg // 8
    for j in range(seg):
        for c in range(src_ref.shape[0]):
            v = src_ref[c, pl.ds(64 * (j % m) + j // m, 8, stride=8), :]
            if zero_head is not None and j < PAD_ROWS // 8:
                v = jnp.where(zero_head, 0.0, v)
            dst_ref[8 * j:8 * j + 8, c * 128:(c + 1) * 128] = v if fn is None else fn(v)


def _gelu(x):
    c = math.sqrt(2.0 / math.pi)
    t = jnp.tanh(c * (x + 0.044715 * x * x * x))
    return 0.5 * x * (1.0 + t)


def _gelu_grad(x):
    c = math.sqrt(2.0 / math.pi)
    t = jnp.tanh(c * (x + 0.044715 * x * x * x))
    return 0.5 * (1.0 + t) + 0.5 * x * (1.0 - t * t) * c * (1.0 + 3.0 * 0.044715 * x * x)


def _s5_tile(lp):
    return _pick(lp, (384, 320, 256, 128, 64))


def _s5_fwd(p, bblk, cblk, lam8, tab_f, dskip, lp, comm=None):
    T = p.shape[0]
    ts = _s5_tile(lp)
    seg = ts // 8
    nblk, per_seq = T // ts, lp // ts
    ucol = 0

    def body(u0, u1, u2, u3, b_ref, c_ref, lam_ref, tab_ref, d_ref, y_ref, gy_ref, cin_ref, s_sc, u_sc, y_sc, carry_sc):
        r = pl.program_id(0)
        first = (r % per_seq) == 0

        @pl.when(first)
        def _():
            carry_sc[...] = jnp.zeros_like(carry_sc)

        cin_ref[0] = carry_sc[...]
        _to_segments((u0, u1, u2, u3), u_sc, seg, first)
        u = u_sc[...]
        ub = u.astype(BF16)
        for blk in range(S5_BLK):
            s_sc[:, blk * 1024:(blk + 1) * 1024] = _dot(ub[:, blk * 128:(blk + 1) * 128], b_ref[blk], 1, 0)
        _seg_scan(s_sc, 0, seg, lam_ref, tab_ref, carry_sc, False)
        for blk in range(S5_BLK):
            sb = s_sc[:, blk * 1024:(blk + 1) * 1024].astype(BF16)
            y_sc[blk] = _dot(sb, c_ref[blk], 1, 0) + d_ref[:, blk * 128:(blk + 1) * 128] * u[:, blk * 128:(blk + 1) * 128]
        _from_segments(y_sc, y_ref, seg)
        gy_ref[...] = _gelu(y_ref[...]).astype(BF16)

    def ublock(c):
        return pl.BlockSpec((ts, 128), lambda r: (r, 4 * ucol + c))

    return _ride_call(
        body, "s5_fwd", (nblk,),
        [ublock(0), ublock(1), ublock(2), ublock(3),
         pl.BlockSpec((S5_BLK, 128, 1024), lambda r: (0, 0, 0)),
         pl.BlockSpec((S5_BLK, 1024, 128), lambda r: (0, 0, 0)),
         pl.BlockSpec((2, 8, 2048), lambda r: (0, 0, 0)),
         pl.BlockSpec((8, 8, 2048), lambda r: (0, 0, 0)),
         pl.BlockSpec((1, S5_WIDTH), lambda r: (0, 0))],
        [pl.BlockSpec((ts, S5_WIDTH), lambda r: (r, 0)),
         pl.BlockSpec((ts, S5_WIDTH), lambda r: (r, 0)),
         pl.BlockSpec((1, 8, S5_COLS), lambda r: (r, 0, 0))],
        [jax.ShapeDtypeStruct((T, S5_WIDTH), F32),
         jax.ShapeDtypeStruct((T, S5_WIDTH), BF16),
         jax.ShapeDtypeStruct((nblk, 8, S5_COLS), F32)],
        [pltpu.VMEM((ts, S5_COLS), F32), pltpu.VMEM((ts, S5_WIDTH), F32),
         pltpu.VMEM((S5_BLK, ts, 128), F32), pltpu.VMEM((8, S5_COLS), F32)],
        (p, p, p, p, bblk.astype(BF16), cblk.astype(BF16), lam8, tab_f, dskip), comm=comm)


def _s5_bwd(p, y_pre, dgy, cin, bblk, cblk, lam8, tab_f, tab_b, dskip, dif, dp, lp, comm=None):
    T = p.shape[0]
    ts = _s5_tile(lp)
    seg = ts // 8
    nblk, per_seq = T // ts, lp // ts
    ucol = 0
    assert P_IF == P_U + S5_WIDTH and P_U % (S5_WIDTH + 128) == 0

    def body(u0, u1, u2, u3, y0, y1, y2, y3, g0, g1, g2, g3, cin_ref, b_ref, bt_ref, c_ref, ct_ref, lam_ref, tf_ref,
             tb_ref, d_ref, dif_ref, dpin_ref, du_ref, dbb_ref, dcb_ref, dlam_ref, dd_ref, s_sc, a_sc, u_sc, dy_sc,
             y_sc, w_sc, carry_sc, carry_b):
        t = pl.program_id(0)
        r = nblk - 1 - t
        first = (r % per_seq) == 0
        last = (r % per_seq) == per_seq - 1

        @pl.when(t == 0)
        def _():
            dbb_ref[...] = jnp.zeros_like(dbb_ref)
            dcb_ref[...] = jnp.zeros_like(dcb_ref)
            dlam_ref[...] = jnp.zeros_like(dlam_ref)
            dd_ref[...] = jnp.zeros_like(dd_ref)

        @pl.when(last)
        def _():
            carry_b[...] = jnp.zeros_like(carry_b)

        carry_sc[...] = cin_ref[0]
        _to_segments((u0, u1, u2, u3), u_sc, seg, first)
        u = u_sc[...]
        ub = u.astype(BF16)
        for blk in range(S5_BLK):
            s_sc[8:8 + ts, blk * 1024:(blk + 1) * 1024] = _dot(ub[:, blk * 128:(blk + 1) * 128], b_ref[blk], 1, 0)
        _seg_scan(s_sc, 8, seg, lam_ref, tf_ref, carry_sc, False, cseg_ref=s_sc)

        _to_segments((g0, g1, g2, g3), dy_sc, seg)
        _to_segments((y0, y1, y2, y3), y_sc, seg)
        dy = dy_sc[...] * _gelu_grad(y_sc[...])
        dy_sc[...] = dy
        dyb = dy.astype(BF16)
        dd_ref[0:1, :] += jnp.sum(dy * u, axis=0, keepdims=True)
        for blk in range(S5_BLK):
            a_sc[:, blk * 1024:(blk + 1) * 1024] = _dot(dyb[:, blk * 128:(blk + 1) * 128], ct_ref[blk], 1, 0)
            sb = s_sc[8:8 + ts, blk * 1024:(blk + 1) * 1024].astype(BF16)
            dcb_ref[blk] += _dot(sb, dyb[:, blk * 128:(blk + 1) * 128], 0, 0)

        def lam_grad(i, cre, cim, tc, a_r, a_i):
            r0 = pl.multiple_of(i * 8, 8)
            pr = s_sc[pl.ds(r0, 8), cre:cre + 128]
            pi = s_sc[pl.ds(r0, 8), cim:cim + 128]
            dlam_ref[0, :, tc:tc + 128] += a_r * pr + a_i * pi
            dlam_ref[1, :, tc:tc + 128] += a_i * pr - a_r * pi

        _seg_scan(a_sc, 0, seg, lam_ref, tb_ref, carry_b, True, extra=lam_grad)

        for blk in range(S5_BLK):
            ab = a_sc[:, blk * 1024:(blk + 1) * 1024].astype(BF16)
            w_sc[blk] = _dot(ab, bt_ref[blk], 1, 0) + d_ref[:, blk * 128:(blk + 1) * 128] * dy_sc[:, blk * 128:(blk + 1) * 128]
            dbb_ref[blk] += _dot(u_sc[:, blk * 128:(blk + 1) * 128].astype(BF16), ab, 0, 0)
        _from_segments(w_sc, y_sc, seg, zero_head=first)
        du_ref[:, 0:S5_WIDTH] = y_sc[...].astype(BF16)
        du_ref[:, S5_WIDTH:S5_WIDTH + 128] = dif_ref[...]

    const3 = lambda t: (0, 0, 0)
    rev = lambda t: (nblk - 1 - t, 0)

    def lanes(c0):
        return [pl.BlockSpec((ts, 128), lambda t, cc=c0 + c: (nblk - 1 - t, cc)) for c in range(4)]

    return _ride_call(
        body, "s5_bwd", (nblk,),
        lanes(4 * ucol) + lanes(0) + lanes(0) + [
            pl.BlockSpec((1, 8, S5_COLS), lambda t: (nblk - 1 - t, 0, 0)),
            pl.BlockSpec((S5_BLK, 128, 1024), const3),
            pl.BlockSpec((S5_BLK, 1024, 128), const3),
            pl.BlockSpec((S5_BLK, 1024, 128), const3),
            pl.BlockSpec((S5_BLK, 128, 1024), const3),
            pl.BlockSpec((2, 8, 2048), const3),
            pl.BlockSpec((8, 8, 2048), const3),
            pl.BlockSpec((8, 8, 2048), const3),
            pl.BlockSpec((1, S5_WIDTH), lambda t: (0, 0)),
            pl.BlockSpec((ts, 128), rev),
            pl.BlockSpec(memory_space=pl.ANY)],
        [pl.BlockSpec((ts, S5_WIDTH + 128), lambda t: (nblk - 1 - t, P_U // (S5_WIDTH + 128))),
         pl.BlockSpec((S5_BLK, 128, 1024), const3),
         pl.BlockSpec((S5_BLK, 1024, 128), const3),
         pl.BlockSpec((2, 8, 2048), const3),
         pl.BlockSpec((8, S5_WIDTH), lambda t: (0, 0))],
        [jax.ShapeDtypeStruct((T, PW), BF16),
         jax.ShapeDtypeStruct((S5_BLK, 128, 1024), F32),
         jax.ShapeDtypeStruct((S5_BLK, 1024, 128), F32),
         jax.ShapeDtypeStruct((2, 8, 2048), F32),
         jax.ShapeDtypeStruct((8, S5_WIDTH), F32)],
        [pltpu.VMEM((ts + 8, S5_COLS), F32), pltpu.VMEM((ts, S5_COLS), F32),
         pltpu.VMEM((ts, S5_WIDTH), F32), pltpu.VMEM((ts, S5_WIDTH), F32),
         pltpu.VMEM((ts, S5_WIDTH), F32), pltpu.VMEM((S5_BLK, ts, 128), F32),
         pltpu.VMEM((8, S5_COLS), F32), pltpu.VMEM((8, S5_COLS), F32)],
        (p, p, p, p, y_pre, y_pre, y_pre, y_pre, dgy, dgy, dgy, dgy, cin,
         bblk.astype(BF16), jnp.swapaxes(bblk, 1, 2).astype(BF16),
         cblk.astype(BF16), jnp.swapaxes(cblk, 1, 2).astype(BF16), lam8, tab_f, tab_b, dskip, dif, dp),
        comm=comm, aliases={22: 0})


CONV_HALO = 16


def _row_in_seq(i, tr, lp):
    rowid = lax.broadcasted_iota(jnp.int32, (tr, 1), 0)
    return (i * tr + rowid) % lp


def _conv_fwd(p, w, b, lp):
    T = p.shape[0]
    tr = _pick(T, (384, 320, 256, 128, 64))
    c = P_QK // 1024
    hl = CONV_HALO

    def body(x_ref, xp_ref, w_ref, b_ref, o_ref):
        i = pl.program_id(0)
        pos = _row_in_seq(i, tr, lp)
        x = jnp.where(pos < PAD_ROWS, 0.0, x_ref[...].astype(F32))
        pos_p = (i * tr - hl + lax.broadcasted_iota(jnp.int32, (hl, 1), 0)) % lp
        xp = jnp.where((pos_p < PAD_ROWS) | (i == 0), 0.0, xp_ref[...].astype(F32))
        xx = jnp.concatenate([xp, x], axis=0)
        acc = b_ref[...] + w_ref[3:4, :] * x
        for s in (1, 2, 3):
            acc = acc + w_ref[3 - s:4 - s, :] * pltpu.roll(xx, s, 0)[hl:hl + tr]
        o_ref[...] = acc * _sigmoid(acc)

    return pl.pallas_call(
        body, name="conv_fwd", grid=(T // tr,),
        in_specs=[pl.BlockSpec((tr, 1024), lambda i: (i, c)),
                  pl.BlockSpec((hl, 1024), lambda i: (jnp.maximum(i * (tr // hl) - 1, 0), c)),
                  pl.BlockSpec((4, 1024), lambda i: (0, 0)),
                  pl.BlockSpec((1, 1024), lambda i: (0, 0))],
        out_specs=pl.BlockSpec((tr, 1024), lambda i: (i, 0)),
        out_shape=jax.ShapeDtypeStruct((T, 1024), F32),
        compiler_params=_cparams(("arbitrary",)),
    )(p, p, w, b)


def _conv_bwd(p, dqk, w, b, dp, lp):
    T = p.shape[0]
    tr = _pick(T, (384, 320, 256, 128, 64))
    c = P_QK // 1024
    nb = T // tr
    hl = CONV_HALO

    def body(x_ref, xp_ref, xn_ref, g_ref, gn_ref, w_ref, b_ref, dpin_ref, dx_ref, dw_ref, db_ref):
        i = pl.program_id(0)

        @pl.when(i == 0)
        def _():
            dw_ref[...] = jnp.zeros_like(dw_ref)
            db_ref[...] = jnp.zeros_like(db_ref)

        def seqpos(off, n):
            return (i * tr + off + lax.broadcasted_iota(jnp.int32, (n, 1), 0)) % lp

        x = jnp.where(seqpos(0, tr) < PAD_ROWS, 0.0, x_ref[...].astype(F32))
        xp = jnp.where((seqpos(-hl, hl) < PAD_ROWS) | (i == 0), 0.0, xp_ref[...].astype(F32))
        xn = jnp.where((seqpos(tr, hl) < PAD_ROWS) | (i == nb - 1), 0.0, xn_ref[...].astype(F32))
        xx = jnp.concatenate([xp, x, xn], axis=0)
        gg = jnp.concatenate([g_ref[...], gn_ref[...]], axis=0).astype(F32)
        n2 = tr + hl
        acc = b_ref[...] + w_ref[3:4, :] * xx[hl:hl + n2]
        for s in (1, 2, 3):
            acc = acc + w_ref[3 - s:4 - s, :] * pltpu.roll(xx, s, 0)[hl:hl + n2]
        sg = _sigmoid(acc)
        dpre = gg * (sg * (1.0 + acc * (1.0 - sg)))
        valid = jnp.concatenate(
            [seqpos(0, tr) >= PAD_ROWS, (seqpos(tr, hl) >= PAD_ROWS) & (i < nb - 1)], axis=0)
        dpre = jnp.where(valid, dpre, 0.0)
        d0 = dpre[0:tr]
        dx = w_ref[3:4, :] * d0
        for s in (1, 2, 3):
            dx = dx + w_ref[3 - s:4 - s, :] * pltpu.roll(dpre, n2 - s, 0)[0:tr]
        dx_ref[...] = jnp.where(seqpos(0, tr) < PAD_ROWS, 0.0, dx).astype(BF16)
        db_ref[0:1, :] += jnp.sum(d0, axis=0, keepdims=True)
        dw_ref[3:4, :] += jnp.sum(d0 * x, axis=0, keepdims=True)
        for s in (1, 2, 3):
            xs = pltpu.roll(xx, s, 0)[hl:hl + tr]
            dw_ref[3 - s:4 - s, :] += jnp.sum(d0 * xs, axis=0, keepdims=True)

    t8 = tr // hl
    return pl.pallas_call(
        body, name="conv_bwd", grid=(nb,),
        in_specs=[pl.BlockSpec((tr, 1024), lambda i: (i, c)),
                  pl.BlockSpec((hl, 1024), lambda i: (jnp.maximum(i * t8 - 1, 0), c)),
                  pl.BlockSpec((hl, 1024), lambda i: (jnp.minimum((i + 1) * t8, nb * t8 - 1), c)),
                  pl.BlockSpec((tr, 1024), lambda i: (i, 0)),
                  pl.BlockSpec((hl, 1024), lambda i: (jnp.minimum((i + 1) * t8, nb * t8 - 1), 0)),
                  pl.BlockSpec((4, 1024), lambda i: (0, 0)),
                  pl.BlockSpec((1, 1024), lambda i: (0, 0)),
                  pl.BlockSpec(memory_space=pl.ANY)],
        out_specs=(pl.BlockSpec((tr, 1024), lambda i: (i, c)),
                   pl.BlockSpec((8, 1024), lambda i: (0, 0)),
                   pl.BlockSpec((8, 1024), lambda i: (0, 0))),
        out_shape=(jax.ShapeDtypeStruct((T, PW), BF16),
                   jax.ShapeDtypeStruct((8, 1024), F32),
                   jax.ShapeDtypeStruct((8, 1024), F32)),
        input_output_aliases={7: 0},
        compiler_params=_cparams(("arbitrary",)),
    )(p, p, p, dqk, dqk, w, b, dp)


def _split3(x):
    hi = x.astype(BF16)
    r1 = x - hi.astype(F32)
    mid = r1.astype(BF16)
    lo = (r1 - mid.astype(F32)).astype(BF16)
    return hi, mid, lo


def _tri_sum(x, upper):
    r = lax.broadcasted_iota(jnp.int32, (CHUNK, CHUNK), 0)
    c = lax.broadcasted_iota(jnp.int32, (CHUNK, CHUNK), 1)
    tri = jnp.where((r <= c) if upper else (r >= c), 1.0, 0.0).astype(BF16)
    hi, mid, lo = _split3(x)
    return _dot(tri, hi, 1, 0) + _dot(tri, mid, 1, 0) + _dot(tri, lo, 1, 0)


def _lane_col(x, lane):
    l = lax.broadcasted_iota(jnp.int32, x.shape, 1)
    return jnp.sum(jnp.where(l == lane, x, 0.0), axis=1, keepdims=True)


def _to_row(col):
    r = lax.broadcasted_iota(jnp.int32, (CHUNK, CHUNK), 0)
    c = lax.broadcasted_iota(jnp.int32, (CHUNK, CHUNK), 1)
    return jnp.sum(jnp.where(r == c, col, 0.0), axis=0, keepdims=True)


def _to_col(row):
    r = lax.broadcasted_iota(jnp.int32, (CHUNK, CHUNK), 0)
    c = lax.broadcasted_iota(jnp.int32, (CHUNK, CHUNK), 1)
    return jnp.sum(jnp.where(r == c, row, 0.0), axis=1, keepdims=True)


def _log_sigmoid(x):
    return jnp.minimum(x, 0.0) - jnp.log(1.0 + jnp.exp(-jnp.abs(x)))


def _mlstm_gates(ifv, padmask):
    lf = jnp.where(padmask, 0.0, _log_sigmoid(ifv))
    b_all = _tri_sum(lf, False)
    li = jnp.where(padmask, -jnp.inf, ifv)
    return li, b_all


def _mlstm_head_fwd(q, k, v, li_col, b_col, c_st, n_st, m_st):
    r = lax.broadcasted_iota(jnp.int32, (CHUNK, CHUNK), 0)
    c = lax.broadcasted_iota(jnp.int32, (CHUNK, CHUNK), 1)
    rowid = lax.broadcasted_iota(jnp.int32, (CHUNK, 1), 0)
    b_row = _to_row(b_col)
    li_row = _to_row(li_col)
    dmat = jnp.where(r >= c, b_col - b_row + li_row, -jnp.inf)
    m_inter = b_col + m_st
    m_row = jnp.maximum(m_inter, jnp.max(dmat, axis=1, keepdims=True))
    w_intra = jnp.exp(dmat - m_row)
    w_inter = jnp.exp(m_inter - m_row)
    qb, kb, vb = q.astype(BF16), k.astype(BF16), v.astype(BF16)
    qk = _dot(qb, kb, 1, 1)
    s = qk * w_intra
    cb = c_st.astype(BF16)
    qc = _dot(qb, cb, 1, 0)
    qn = jnp.sum(q * n_st, axis=1, keepdims=True)
    num = _dot(s.astype(BF16), vb, 1, 0) + w_inter * qc
    den = jnp.sum(s, axis=1, keepdims=True) + w_inter * qn
    floor = jnp.exp(-m_row)
    rinv = 1.0 / jnp.maximum(jnp.abs(den), floor)
    h = num * rinv
    b_last = jnp.sum(jnp.where(rowid == CHUNK - 1, b_col, 0.0), axis=0, keepdims=True)
    g_col = b_last - b_col + li_col
    m_new = jnp.maximum(b_last + m_st, jnp.max(g_col, axis=0, keepdims=True))
    w_k = jnp.exp(g_col - m_new)
    decay = jnp.exp(b_last + m_st - m_new)
    kw = w_k * k
    c_new = decay * c_st + _dot(kw.astype(BF16), vb, 0, 0)
    n_new = decay * n_st + jnp.sum(kw, axis=0, keepdims=True)
    return dict(h=h, c_new=c_new, n_new=n_new, m_new=m_new, w_intra=w_intra, w_inter=w_inter, s=s,
                qc=qc, qn=qn, den=den, floor=floor, rinv=rinv, w_k=w_k, decay=decay, kw=kw,
                qb=qb, kb=kb, vb=vb, cb=cb)


def _mlstm_fwd(p, puif, qk, lp):
    T = p.shape[0]
    nch = lp // CHUNK
    B = T // lp
    scale = M_DK ** -0.5

    def body(q_ref, k_ref, v_ref, if_ref, h_ref, cst_ref, nm_ref, *state):
        c_scs, nm_scs = state[:B * M_HEADS], state[B * M_HEADS:]
        ci = pl.program_id(0)

        @pl.when(ci == 0)
        def _():
            for ref in state:
                ref[...] = jnp.zeros_like(ref)

        rowid = lax.broadcasted_iota(jnp.int32, (CHUNK, 1), 0)
        padmask = (ci == 0) & (rowid < PAD_ROWS)
        for bb in range(B):
            li_all, b_all = _mlstm_gates(if_ref[bb], padmask)
            for hd in range(M_HEADS):
                c_sc, nm_sc = c_scs[bb * M_HEADS + hd], nm_scs[bb * M_HEADS + hd]
                c_st, n_st, m_row = c_sc[...], nm_sc[0:1, :], nm_sc[1:2, :]
                cst_ref[bb, 0, hd * M_DK:(hd + 1) * M_DK, :] = c_st
                nm_ref[bb, 0, hd:hd + 1, :] = n_st
                nm_ref[bb, 0, M_HEADS + hd:M_HEADS + hd + 1, :] = m_row
                q = q_ref[bb, :, hd * M_DK:(hd + 1) * M_DK]
                k = k_ref[bb, :, hd * M_DK:(hd + 1) * M_DK] * scale
                v = v_ref[bb, :, hd * M_DV:(hd + 1) * M_DV]
                o = _mlstm_head_fwd(q, k, v, _lane_col(li_all, hd), _lane_col(b_all, M_HEADS + hd),
                                    c_st, n_st, nm_sc[1:2, 0:1])
                h_ref[bb, :, hd * M_DV:(hd + 1) * M_DV] = o["h"].astype(BF16)
                c_sc[...] = o["c_new"]
                nm_sc[0:1, :] = o["n_new"]
                nm_sc[1:2, :] = jnp.broadcast_to(o["m_new"], (1, 128))

    qk3, p3, uif3 = qk.reshape(B, lp, 1024), p.reshape(B, lp, p.shape[1]), puif.reshape(B, lp, puif.shape[1])
    h, cst, nm = pl.pallas_call(
        body, name="mlstm_fwd", grid=(nch,),
        in_specs=[pl.BlockSpec((B, CHUNK, 512), lambda ci: (0, ci, 0)),
                  pl.BlockSpec((B, CHUNK, 512), lambda ci: (0, ci, 1)),
                  pl.BlockSpec((B, CHUNK, 1024), lambda ci: (0, ci, P_V // 1024)),
                  pl.BlockSpec((B, CHUNK, 128), lambda ci: (0, ci, S5_WIDTH // 128))],
        out_specs=(pl.BlockSpec((B, CHUNK, 1024), lambda ci: (0, ci, 0)),
                   pl.BlockSpec((B, 1, M_HEADS * M_DK, M_DV), lambda ci: (0, ci, 0, 0)),
                   pl.BlockSpec((B, 1, 8, 128), lambda ci: (0, ci, 0, 0))),
        out_shape=(jax.ShapeDtypeStruct((B, lp, 1024), BF16),
                   jax.ShapeDtypeStruct((B, nch, M_HEADS * M_DK, M_DV), F32),
                   jax.ShapeDtypeStruct((B, nch, 8, 128), F32)),
        scratch_shapes=([pltpu.VMEM((M_DK, M_DV), F32)] * (B * M_HEADS)
                        + [pltpu.VMEM((8, 128), F32)] * (B * M_HEADS)),
        compiler_params=_cparams(("arbitrary",)),
    )(qk3, qk3, p3, uif3)
    return h.reshape(T, 1024), cst, nm


def _mlstm_bwd(p, puif, qk, cst, nm, dh, dp, lp, comm=None):
    T = p.shape[0]
    nch = lp // CHUNK
    B = T // lp
    scale = M_DK ** -0.5

    def body(q_ref, k_ref, v_ref, if_ref, cst_ref, nm_ref, dh_ref, dpin_ref, dqk_ref, dv_ref, dif_ref, *state):
        t = pl.program_id(0)
        ci = nch - 1 - t

        @pl.when(t == 0)
        def _():
            for ref in state:
                ref[...] = jnp.zeros_like(ref)

        for bb in range(B):
            one_sequence(bb, ci, q_ref, k_ref, v_ref, if_ref, cst_ref, nm_ref, dh_ref, dqk_ref, dv_ref, dif_ref,
                         state[bb * M_HEADS:(bb + 1) * M_HEADS],
                         state[(B + bb) * M_HEADS:(B + bb + 1) * M_HEADS])

    def one_sequence(bb, ci, q_ref, k_ref, v_ref, if_ref, cst_ref, nm_ref, dh_ref, dqk_ref, dv_ref, dif_ref,
                     dc_scs, dn_scs):
        rowid = lax.broadcasted_iota(jnp.int32, (CHUNK, 1), 0)
        lane = lax.broadcasted_iota(jnp.int32, (CHUNK, 128), 1)
        padmask = (ci == 0) & (rowid < PAD_ROWS)
        ifv = if_ref[bb]
        li_all, b_all = _mlstm_gates(ifv, padmask)
        db_all = jnp.zeros((CHUNK, 128), F32)
        dli_all = jnp.zeros((CHUNK, 128), F32)
        for hd in range(M_HEADS):
            q = q_ref[bb, :, hd * M_DK:(hd + 1) * M_DK]
            k = k_ref[bb, :, hd * M_DK:(hd + 1) * M_DK] * scale
            v = v_ref[bb, :, hd * M_DV:(hd + 1) * M_DV]
            c_st = cst_ref[bb, 0, hd * M_DK:(hd + 1) * M_DK, :]
            n_st = nm_ref[bb, 0, hd:hd + 1, :]
            m_st = nm_ref[bb, 0, M_HEADS + hd:M_HEADS + hd + 1, 0:1]
            o = _mlstm_head_fwd(q, k, v, _lane_col(li_all, hd), _lane_col(b_all, M_HEADS + hd), c_st, n_st, m_st)
            dc_sc, dn_sc = dc_scs[hd], dn_scs[hd]
            dc_new = dc_sc[...]
            dn_new = dn_sc[0:1, :]
            dcb = dc_new.astype(BF16)
            dhh = dh_ref[bb, :, hd * M_DV:(hd + 1) * M_DV].astype(F32)
            dnum = dhh * o["rinv"]
            dhh_h = jnp.sum(dhh * o["h"], axis=1, keepdims=True)
            sgn = jnp.where(o["den"] >= 0.0, 1.0, -1.0)
            dden = jnp.where(jnp.abs(o["den"]) > o["floor"], -dhh_h * o["rinv"] * sgn, 0.0)
            dnb = dnum.astype(BF16)
            ds = _dot(dnb, o["vb"], 1, 1) + dden
            sb = o["s"].astype(BF16)
            kwb = o["kw"].astype(BF16)
            dv = _dot(sb, dnb, 0, 0) + _dot(kwb, dcb, 1, 0)
            dqk_m = (ds * o["w_intra"]).astype(BF16)
            wdn = o["w_inter"] * dnum
            wdd = o["w_inter"] * dden
            dq = _dot(dqk_m, o["kb"], 1, 0) + _dot(wdn.astype(BF16), o["cb"], 1, 1) + wdd * n_st
            vdc = _dot(o["vb"], dcb, 1, 1)
            dk = _dot(dqk_m, o["qb"], 0, 0) + o["w_k"] * (vdc + dn_new)
            dd = ds * o["s"]
            dd_col = _to_col(jnp.sum(dd, axis=0, keepdims=True))
            dmi = o["w_inter"] * (jnp.sum(dnum * o["qc"], axis=1, keepdims=True) + dden * o["qn"])
            dg = o["w_k"] * (jnp.sum(k * vdc, axis=1, keepdims=True) + jnp.sum(k * dn_new, axis=1, keepdims=True))
            d_blast = (o["decay"] * (jnp.sum(jnp.sum(dc_new * c_st, axis=1, keepdims=True), axis=0, keepdims=True)
                                     + jnp.sum(dn_new * n_st, axis=1, keepdims=True))
                       + jnp.sum(dg, axis=0, keepdims=True))
            db_col = jnp.sum(dd, axis=1, keepdims=True) - dd_col + dmi - dg
            db_col = db_col + jnp.where(rowid == CHUNK - 1, d_blast, 0.0)
            dli_col = dd_col + dg
            db_all = db_all + jnp.where(lane == M_HEADS + hd, db_col, 0.0)
            dli_all = dli_all + jnp.where(lane == hd, dli_col, 0.0)
            dc_sc[...] = o["decay"] * dc_new + _dot(o["qb"], wdn.astype(BF16), 0, 0)
            dn_sc[0:1, :] = o["decay"] * dn_new + jnp.sum(q * wdd, axis=0, keepdims=True)
            dqk_ref[bb, :, hd * M_DK:(hd + 1) * M_DK] = dq.astype(BF16)
            dqk_ref[bb, :, 512 + hd * M_DK:512 + (hd + 1) * M_DK] = (dk * scale).astype(BF16)
            dv_ref[bb, :, hd * M_DV:(hd + 1) * M_DV] = dv.astype(BF16)
        dlf = _tri_sum(db_all, True)
        dif = dli_all + dlf * _sigmoid(-ifv)
        dif_ref[bb] = jnp.where(padmask | (lane >= 2 * M_HEADS), 0.0, dif).astype(BF16)

    def rev(cb):
        return lambda t: (0, nch - 1 - t, cb)

    rev4 = lambda t: (0, nch - 1 - t, 0, 0)
    qk3, p3, dh3 = qk.reshape(B, lp, 1024), p.reshape(B, lp, p.shape[1]), dh.reshape(B, lp, 1024)
    uif3 = puif.reshape(B, lp, puif.shape[1])
    (dqk, dv, dif), comm_res = _ride_call(
        body, "mlstm_bwd", (nch,),
        [pl.BlockSpec((B, CHUNK, 512), rev(0)),
         pl.BlockSpec((B, CHUNK, 512), rev(1)),
         pl.BlockSpec((B, CHUNK, 1024), rev(P_V // 1024)),
         pl.BlockSpec((B, CHUNK, 128), rev(S5_WIDTH // 128)),
         pl.BlockSpec((B, 1, M_HEADS * M_DK, M_DV), rev4),
         pl.BlockSpec((B, 1, 8, 128), rev4),
         pl.BlockSpec((B, CHUNK, 1024), rev(0)),
         pl.BlockSpec(memory_space=pl.ANY)],
        [pl.BlockSpec((B, CHUNK, 1024), rev(0)),
         pl.BlockSpec((B, CHUNK, 1024), rev(P_V // 1024)),
         pl.BlockSpec((B, CHUNK, 128), rev(0))],
        [jax.ShapeDtypeStruct((B, lp, 1024), BF16),
         jax.ShapeDtypeStruct((B, lp, PW), BF16),
         jax.ShapeDtypeStruct((B, lp, 128), BF16)],
        [pltpu.VMEM((M_DK, M_DV), F32)] * (B * M_HEADS) + [pltpu.VMEM((8, 128), F32)] * (B * M_HEADS),
        (qk3, qk3, p3, uif3, cst, nm, dh3, dp.reshape(B, lp, PW)), comm=comm, aliases={7: 1})
    return dqk.reshape(T, 1024), dv.reshape(T, PW), dif.reshape(T, 128), comm_res


def _headnorm_fwd(hm, p, g):
    T = hm.shape[0]
    tr = _pick(T, (384, 320, 256, 128, 64))

    def body(h_ref, o_ref, g_ref, a_ref):
        for hd in range(M_HEADS):
            sl = slice(hd * M_DV, (hd + 1) * M_DV)
            hn = _ln_rows(h_ref[:, sl].astype(F32), g_ref[:, sl], 0.0)
            a_ref[:, sl] = (_sigmoid(o_ref[:, sl].astype(F32)) * hn).astype(BF16)

    return pl.pallas_call(
        body, name="headnorm_fwd", grid=(T // tr,),
        in_specs=[pl.BlockSpec((tr, 1024), lambda i: (i, 0)),
                  pl.BlockSpec((tr, 1024), lambda i: (i, P_O // 1024)),
                  pl.BlockSpec((1, 1024), lambda i: (0, 0))],
        out_specs=pl.BlockSpec((tr, 1024), lambda i: (i, 0)),
        out_shape=jax.ShapeDtypeStruct((T, 1024), BF16),
        compiler_params=_cparams(("arbitrary",)),
    )(hm, p, g)


def _headnorm_bwd(da, hm, p, g, dp):
    T = hm.shape[0]
    tr = _pick(T, (384, 320, 256, 128, 64))

    def body(da_ref, h_ref, o_ref, g_ref, dpin_ref, dh_ref, do_ref, dg_ref):
        @pl.when(pl.program_id(0) == 0)
        def _():
            dg_ref[...] = jnp.zeros_like(dg_ref)

        for hd in range(M_HEADS):
            sl = slice(hd * M_DV, (hd + 1) * M_DV)
            gg = g_ref[:, sl]
            so = _sigmoid(o_ref[:, sl].astype(F32))
            da = da_ref[:, sl].astype(F32)
            dhn = da * so
            dv, xhat = _ln_bwd_rows(dhn, h_ref[:, sl].astype(F32), gg)
            dh_ref[:, sl] = dv.astype(BF16)
            do_ref[:, sl] = (da * (xhat * gg) * so * (1.0 - so)).astype(BF16)
            dg_ref[0:1, sl] += jnp.sum(dhn * xhat, axis=0, keepdims=True)

    row = pl.BlockSpec((tr, 1024), lambda i: (i, 0))
    return pl.pallas_call(
        body, name="headnorm_bwd", grid=(T // tr,),
        in_specs=[row, row, pl.BlockSpec((tr, 1024), lambda i: (i, P_O // 1024)),
                  pl.BlockSpec((1, 1024), lambda i: (0, 0)), pl.BlockSpec(memory_space=pl.ANY)],
        out_specs=(row, pl.BlockSpec((tr, 1024), lambda i: (i, P_O // 1024)),
                   pl.BlockSpec((8, 1024), lambda i: (0, 0))),
        out_shape=(jax.ShapeDtypeStruct((T, 1024), BF16), jax.ShapeDtypeStruct((T, PW), BF16),
                   jax.ShapeDtypeStruct((8, 1024), F32)),
        input_output_aliases={4: 1},
        compiler_params=_cparams(("arbitrary",)),
    )(da, hm, p, g, dp)


def _mix_fwd(z, ym, p):
    T = ym.shape[0]
    tr = _pick(T, (384, 320, 256, 128, 64))

    def body(z1_ref, z2_ref, ym_ref, gs_ref, gm_ref, o_ref):
        ys = z1_ref[...].astype(F32) * _sigmoid(z2_ref[...].astype(F32))
        o_ref[...] = (_sigmoid(gs_ref[...].astype(F32)) * ys
                      + _sigmoid(gm_ref[...].astype(F32)) * ym_ref[...].astype(F32)).astype(BF16)

    def col(cb):
        return pl.BlockSpec((tr, 1024), lambda i: (i, cb))

    return pl.pallas_call(
        body, name="mix_fwd", grid=(T // tr,),
        in_specs=[col(0), col(1), col(0), col(P_GS // 1024), col(P_GM // 1024)],
        out_specs=col(0),
        out_shape=jax.ShapeDtypeStruct((T, 1024), BF16),
        compiler_params=_cparams(("arbitrary",)),
    )(z, z, ym, p, p)


def _mix_bwd(dmix, z, ym, p):
    T = ym.shape[0]
    tr = _pick(T, (384, 320, 256, 128, 64))

    def body(d_ref, z1_ref, z2_ref, ym_ref, gs_ref, gm_ref, dz_ref, dym_ref, dp_ref):
        d = d_ref[...].astype(F32)
        z1 = z1_ref[...].astype(F32)
        s2 = _sigmoid(z2_ref[...].astype(F32))
        ss = _sigmoid(gs_ref[...].astype(F32))
        sm = _sigmoid(gm_ref[...].astype(F32))
        ys = z1 * s2
        dys = d * ss
        dp_ref[:, 0:1024] = (d * ys * ss * (1.0 - ss)).astype(BF16)
        dp_ref[:, 1024:2048] = (d * ym_ref[...].astype(F32) * sm * (1.0 - sm)).astype(BF16)
        dym_ref[...] = (d * sm).astype(BF16)
        dz_ref[:, 0:1024] = (dys * s2).astype(BF16)
        dz_ref[:, 1024:2048] = (dys * z1 * s2 * (1.0 - s2)).astype(BF16)

    def col(cb):
        return pl.BlockSpec((tr, 1024), lambda i: (i, cb))

    o = jax.ShapeDtypeStruct((T, 1024), BF16)
    return pl.pallas_call(
        body, name="mix_bwd", grid=(T // tr,),
        in_specs=[col(0), col(0), col(1), col(0), col(P_GS // 1024), col(P_GM // 1024)],
        out_specs=(pl.BlockSpec((tr, 2048), lambda i: (i, 0)), col(0),
                   pl.BlockSpec((tr, 2048), lambda i: (i, P_GS // 2048))),
        out_shape=(jax.ShapeDtypeStruct((T, 2048), BF16), o, jax.ShapeDtypeStruct((T, PW), BF16)),
        compiler_params=_cparams(("arbitrary",)),
    )(dmix, z, z, ym, p, p)


def _adamw_math(w, g, m, v):
    m2 = ADAM_B1 * m + (1.0 - ADAM_B1) * g
    v2 = ADAM_B2 * v + (1.0 - ADAM_B2) * jnp.square(g)
    m_hat = m2 / (1.0 - ADAM_B1 ** ADAM_STEP)
    v_hat = v2 / (1.0 - ADAM_B2 ** ADAM_STEP)
    delta = -ADAM_LR * (m_hat / (jnp.sqrt(v_hat) + ADAM_EPS) + ADAM_WD * w)
    return delta, m2, v2


def _adamw_big(recv, w, m, v, name):
    R, C = w.shape
    tr, tc = _shard_tile(R, C)
    ns = recv.shape[0]

    def body(r_ref, w_ref, m_ref, v_ref, g_ref, d_ref, m2_ref, v2_ref):
        g = r_ref[0].astype(F32)
        for s in range(1, ns):
            g = g + r_ref[s].astype(F32)
        d, m2, v2 = _adamw_math(w_ref[...], g, m_ref[...], v_ref[...])
        g_ref[...] = g
        d_ref[...] = d
        m2_ref[...] = m2
        v2_ref[...] = v2

    blk = pl.BlockSpec((tr, tc), lambda i, j: (i, j))
    o = jax.ShapeDtypeStruct((R, C), F32)
    return pl.pallas_call(
        body, name=name, grid=(R // tr, C // tc),
        in_specs=[pl.BlockSpec((ns, tr, tc), lambda i, j: (0, i, j)), blk, blk, blk],
        out_specs=(blk,) * 4, out_shape=(o,) * 4,
        compiler_params=_cparams(("arbitrary", "arbitrary")),
    )(recv, w, m, v)


def _adamw_small(gs, ws, ms, vs):
    n = len(ws)
    shapes = [tuple(w.shape) for w in ws]

    def flat2(a):
        return a.reshape(1, -1) if a.ndim == 1 else a.reshape(-1, a.shape[-1])

    def body(*refs):
        g, w, m, v = refs[:n], refs[n:2 * n], refs[2 * n:3 * n], refs[3 * n:4 * n]
        d, m2, v2 = refs[4 * n:5 * n], refs[5 * n:6 * n], refs[6 * n:]
        for i in range(n):
            di, mi, vi = _adamw_math(w[i][...], g[i][...], m[i][...], v[i][...])
            d[i][...] = di
            m2[i][...] = mi
            v2[i][...] = vi

    args = [flat2(a) for a in list(gs) + list(ws) + list(ms) + list(vs)]
    vm = pl.BlockSpec(memory_space=pltpu.VMEM)
    outs = pl.pallas_call(
        body, name="adamw_small",
        in_specs=[vm] * (4 * n), out_specs=tuple([vm] * (3 * n)),
        out_shape=tuple(jax.ShapeDtypeStruct(a.shape, F32) for a in args[:n] * 3),
        compiler_params=pltpu.CompilerParams(vmem_limit_bytes=VMEM_LIMIT),
    )(*args)
    outs = [o.reshape(s) for o, s in zip(outs, shapes * 3)]
    return outs[:n], outs[n:2 * n], outs[2 * n:]


def _pack(arrs):
    parts = []
    for a in arrs:
        f = a.reshape(-1).astype(F32)
        n = -(-f.shape[0] // 1024) * 1024
        parts.append(jnp.pad(f, (0, n - f.shape[0])))
    return jnp.concatenate(parts).reshape(-1, 128)


def _unpack(pack, shapes):
    flat = pack.reshape(-1)
    out, off = [], 0
    for shp in shapes:
        n = math.prod(shp)
        out.append(flat[off:off + n].reshape(shp))
        off += -(-n // 1024) * 1024
    return out


def _cols_from_shards(g):
    return jnp.transpose(g, (1, 0, 2)).reshape(g.shape[1], -1)


def _cols_to_shards(w):
    R = w.shape[0]
    return jnp.transpose(w.reshape(R, N_DEV, -1), (1, 0, 2))


def _nat_to_aligned(w, axis):
    def sl(start, size):
        return lax.slice_in_dim(w, start, start + size, axis=axis)

    pads = [(0, 0)] * w.ndim
    pads[axis] = (0, PW - P_IF - 8)
    return jnp.concatenate([sl(N_V, 1024), sl(N_O, 1024), sl(N_GS, 1024), sl(N_GM, 1024), sl(N_Q, 1024),
                            sl(N_U, 512), jnp.pad(sl(N_I, 8), pads)], axis=axis)


def _aligned_to_nat(w, axis):
    def sl(start, size):
        return lax.slice_in_dim(w, start, start + size, axis=axis)

    return jnp.concatenate([sl(P_U, 512), sl(P_QK, 1024), sl(P_V, 1024), sl(P_O, 1024), sl(P_IF, 8),
                            sl(P_GS, 1024), sl(P_GM, 1024)], axis=axis)


def kernel(x, meta_tokens, ln0_g, ln0_b, w_in, b_in, qk_conv_w, qk_conv_b, s5_lambda_re, s5_lambda_im, s5_log_dt, s5_b_re, s5_b_im, s5_c_re, s5_c_im, s5_d, s5_w_glu, m_norm_g, m_w_out, w_o, ln1_g, ln1_b, w_up, b_up, w_down, ln2_g, ln2_b, loss_target, m_meta_tokens, m_ln0_g, m_ln0_b, m_w_in, m_b_in, m_qk_conv_w, m_qk_conv_b, m_s5_lambda_re, m_s5_lambda_im, m_s5_log_dt, m_s5_b_re, m_s5_b_im, m_s5_c_re, m_s5_c_im, m_s5_d, m_s5_w_glu, m_m_norm_g, m_m_w_out, m_w_o, m_ln1_g, m_ln1_b, m_w_up, m_b_up, m_w_down, m_ln2_g, m_ln2_b, v_meta_tokens, v_ln0_g, v_ln0_b, v_w_in, v_b_in, v_qk_conv_w, v_qk_conv_b, v_s5_lambda_re, v_s5_lambda_im, v_s5_log_dt, v_s5_b_re, v_s5_b_im, v_s5_c_re, v_s5_c_im, v_s5_d, v_s5_w_glu, v_m_norm_g, v_m_w_out, v_w_o, v_ln1_g, v_ln1_b, v_w_up, v_b_up, v_w_down, v_ln2_g, v_ln2_b):
    B, S, D = x.shape
    lp = S + CHUNK
    me = _my_id()

    ln0g, ln0b = ln0_g.reshape(1, D), ln0_b.reshape(1, D)
    (h0, h0b), first = _ln0_fwd(x, ln0g, ln0b, comm=(_Gather, [w_in[0].T.astype(BF16), meta_tokens, qk_conv_w[0]]))
    win_t = _nat_to_aligned(first[0].reshape(IN_NAT, D), 0)
    meta_f = _cols_from_shards(first[1])
    convw_f = _cols_from_shards(first[2])
    later = [a.astype(BF16) for a in (s5_w_glu[0], m_w_out[0], w_o[0])]
    latest = [a.astype(BF16) for a in (w_up[0], w_down[0])]
    b_in_al = _nat_to_aligned(b_in, 1)

    s5_args = (s5_lambda_re[0], s5_lambda_im[0], s5_log_dt[0], s5_b_re[0], s5_b_im[0], s5_c_re[0], s5_c_im[0])
    (ar, ai, bblk, cblk), s5_vjp = jax.vjp(_s5_prep, *s5_args)
    seg = _s5_tile(lp) // 8
    mu_r, mu_i = ar, ai
    for _ in range(seg - 1):
        mu_r, mu_i = mu_r * ar - mu_i * ai, mu_r * ai + mu_i * ar
    tab_f, tab_b = _scan_tables(mu_r, mu_i)
    lam8 = jnp.stack([jnp.broadcast_to(ar, (8, 2048)), jnp.broadcast_to(ai, (8, 2048))])

    h0, h0b = _ln0_head(meta_f, ln0g, ln0b, h0, h0b, lp)
    p, *gathered = _mm(h0b, win_t[:P_U], "nt", "mm_in", bias=b_in_al[:, :P_U], out_dtype=BF16,
                       comm=(_Gather, later))
    puif = _mm(h0b, win_t[P_U:], "nt", "mm_in_uif", bias=b_in_al[:, P_U:])
    wglu_f = _cols_from_shards(gathered[0])
    wmo_f = gathered[1].reshape(1024, 1024)
    wo_f = gathered[2].reshape(1024, 1024)
    (y_pre, gy, cin), gathered2 = _s5_fwd(puif, bblk, cblk, lam8, tab_f, s5_d, lp, comm=(_Gather, latest))
    wup_f = _cols_from_shards(gathered2[0])
    wdown_f = gathered2[1].reshape(D_FF, 1024)
    z = _mm(gy, wglu_f, "nn", "mm_glu", out_dtype=BF16)
    qk = _conv_fwd(p, convw_f, qk_conv_b, lp)
    hm, cst, nm = _mlstm_fwd(p, puif, qk, lp)
    a_m = _headnorm_fwd(hm, p, m_norm_g)
    ym = _mm(a_m, wmo_f, "nn", "mm_mout", out_dtype=BF16)
    mix = _mix_fwd(z, ym, p)
    r1 = _mm(mix, wo_f, "nn", "mm_o")
    pre1, h1, h1b = _ln_res_fwd(h0, r1, ln1_g, ln1_b, "ln1_fwd")
    act = _mm(h1b, wup_f, "nn", "mm_up", bias=b_up, relu2_out=True, out_dtype=BF16)
    ff = _mm(act, wdown_f, "nn", "mm_down")
    loss_acc, dpre2, dpre2b, dg2, db2 = _ln2_loss(h1, ff, ln2_g, ln2_b, loss_target)

    d_up = _mm(dpre2b, wdown_f, "nt", "mm_d_act", sqrt_gate_of=act, out_dtype=BF16)
    g_wdown = _mm(act, dpre2b, "tn", "mm_g_wdown", out_dtype=BF16)
    g_wup, cs_up = _mm(h1b, d_up, "tn", "mm_g_wup", colsum="b", out_dtype=BF16)
    dh1 = _mm(d_up, wup_f, "nt", "mm_d_h1", add=dpre2, add_scale=ALPHA)
    dpre1, dpre1b, dg1, db1 = _ln_bwd(dh1, pre1, ln1_g, "ln1_bwd")
    g_wo = _mm(mix, dpre1b, "tn", "mm_g_wo", out_dtype=BF16)
    dmix = _mm(dpre1b, wo_f, "nt", "mm_d_mix", out_dtype=BF16)
    dz, dym, dp = _mix_bwd(dmix, z, ym, p)
    g_wmo = _mm(a_m, dym, "tn", "mm_g_wmo", out_dtype=BF16)
    da = _mm(dym, wmo_f, "nt", "mm_d_a", out_dtype=BF16)
    dhm, dp, dg_norm = _headnorm_bwd(da, hm, p, m_norm_g, dp)
    early = [g_wdown.reshape(N_DEV, 512, 1024), _cols_to_shards(g_wup), g_wo.reshape(N_DEV, 128, 1024),
             g_wmo.reshape(N_DEV, 128, 1024)]
    dqk, dp, dif, recv_early = _mlstm_bwd(p, puif, qk, cst, nm, dhm, dp, lp, comm=(_AllToAll, early))
    dp, dconv_w, dconv_b = _conv_bwd(p, dqk, convw_f, qk_conv_b, dp, lp)
    g_wglu = _mm(gy, dz, "tn", "mm_g_wglu", out_dtype=BF16)
    dgy = _mm(dz, wglu_f, "nt", "mm_d_gy")
    (dp, dbblk, dcblk, dlam, dd), recv_glu = _s5_bwd(puif, y_pre, dgy, cin, bblk, cblk, lam8, tab_f, tab_b, s5_d, dif, dp,
                                                     lp, comm=(_AllToAll, [_cols_to_shards(g_wglu)]))
    g_win_t, cs_in = _mm(dp, h0b, "tn", "mm_g_win", colsum="a")
    g_win8 = _aligned_to_nat(g_win_t, 0).astype(BF16).reshape(N_DEV, IN_NAT // N_DEV, D)
    dh0, recv_win = _mm(dp, win_t, "nn", "mm_d_h0", add=dpre1, add_scale=ALPHA, comm=(_AllToAll, [g_win8]))
    grad_x, dmeta, dg0, db0 = _ln0_bwd(dh0, x, meta_f, ln0g)

    dlam2 = jnp.sum(dlam, axis=1)
    s5_grads = s5_vjp((dlam2[0:1], dlam2[1:2], dbblk, dcblk))
    small_local = [
        loss_acc[0:1, 0:1], dg0[0:1], db0[0:1], _aligned_to_nat(cs_in[0:1], 1), dconv_b[0:1],
        s5_grads[0], s5_grads[1], s5_grads[2], s5_grads[3], s5_grads[4], s5_grads[5], s5_grads[6],
        dd[0:1], dg_norm[0:1], dg1[0:1], db1[0:1], cs_up[0:1], dg2[0:1], db2[0:1],
        dmeta, dconv_w[0:4]]
    small_shapes = [(), (D,), (D,), (1, IN_NAT), (1, 1024),
                    (1, 32, 64), (1, 32, 64), (1, 32), (1, 32, 64, 16), (1, 32, 64, 16), (1, 32, 16, 64), (1, 32, 16, 64),
                    (1, 512), (1, 1024), (1, 1024), (1, 1024), (1, D_FF), (1, 1024), (1, 1024),
                    (N_META, D), (4, 1024)]
    red = _unpack(_allreduce_small(_pack(small_local)), small_shapes)
    loss = red[0]
    g_meta = lax.dynamic_slice_in_dim(red[19], me * 128, 128, axis=1)
    g_convw = lax.dynamic_slice_in_dim(red[20], me * 128, 128, axis=1)[None]
    small_g = red[1:19] + [g_meta, g_convw]
    small_w = [ln0_g, ln0_b, b_in, qk_conv_b, s5_lambda_re, s5_lambda_im, s5_log_dt, s5_b_re, s5_b_im,
               s5_c_re, s5_c_im, s5_d, m_norm_g, ln1_g, ln1_b, b_up, ln2_g, ln2_b, meta_tokens, qk_conv_w]
    small_m = [m_ln0_g, m_ln0_b, m_b_in, m_qk_conv_b, m_s5_lambda_re, m_s5_lambda_im, m_s5_log_dt, m_s5_b_re,
               m_s5_b_im, m_s5_c_re, m_s5_c_im, m_s5_d, m_m_norm_g, m_ln1_g, m_ln1_b, m_b_up, m_ln2_g, m_ln2_b,
               m_meta_tokens, m_qk_conv_w]
    small_v = [v_ln0_g, v_ln0_b, v_b_in, v_qk_conv_b, v_s5_lambda_re, v_s5_lambda_im, v_s5_log_dt, v_s5_b_re,
               v_s5_b_im, v_s5_c_re, v_s5_c_im, v_s5_d, v_m_norm_g, v_ln1_g, v_ln1_b, v_b_up, v_ln2_g, v_ln2_b,
               v_meta_tokens, v_qk_conv_w]
    shapes_w = [tuple(w.shape) for w in small_w]
    small_g = [g.reshape(s) for g, s in zip(small_g, shapes_w)]
    sd, sm2, sv2 = _adamw_small(small_g, small_w, small_m, small_v)

    names = ["w_in", "s5_w_glu", "m_w_out", "w_o", "w_up", "w_down"]
    recv = [recv_win, recv_glu[0], recv_early[3], recv_early[2], recv_early[1], recv_early[0]]
    big_m = [m_w_in[0].T, m_s5_w_glu[0], m_m_w_out[0], m_w_o[0], m_w_up[0], m_w_down[0]]
    big_v = [v_w_in[0].T, v_s5_w_glu[0], v_m_w_out[0], v_w_o[0], v_w_up[0], v_w_down[0]]
    big_w = [w_in[0].T, s5_w_glu[0], m_w_out[0], w_o[0], w_up[0], w_down[0]]
    big_out = [_adamw_big(r, w, m, v, "adamw_" + nm_) for r, w, m, v, nm_ in zip(recv, big_w, big_m, big_v, names)]
    big_out[0] = [o.T for o in big_out[0]]

    order = ["meta_tokens", "ln0_g", "ln0_b", "w_in", "b_in", "qk_conv_w", "qk_conv_b", "s5_lambda_re", "s5_lambda_im",
             "s5_log_dt", "s5_b_re", "s5_b_im", "s5_c_re", "s5_c_im", "s5_d", "s5_w_glu", "m_norm_g", "m_w_out", "w_o",
             "ln1_g", "ln1_b", "w_up", "b_up", "w_down", "ln2_g", "ln2_b"]
    small_names = ["ln0_g", "ln0_b", "b_in", "qk_conv_b", "s5_lambda_re", "s5_lambda_im", "s5_log_dt", "s5_b_re",
                   "s5_b_im", "s5_c_re", "s5_c_im", "s5_d", "m_norm_g", "ln1_g", "ln1_b", "b_up", "ln2_g", "ln2_b",
                   "meta_tokens", "qk_conv_w"]
    res = {}
    for i, n in enumerate(small_names):
        res[n] = (small_g[i], sd[i], sm2[i], sv2[i])
    for i, n in enumerate(names):
        res[n] = tuple(o[None] for o in big_out[i])
    outs = [loss, grad_x]
    for kind in range(4):
        outs += [res[n][kind] for n in order]
    return tuple(outs)
```

```python
import functools
import math

import jax
import jax.numpy as jnp
from jax import lax
from jax.experimental import pallas as pl
from jax.experimental.pallas import tpu as pltpu

F32 = jnp.float32
BF16 = jnp.bfloat16

D_MODEL = 1024
N_META = 16
CHUNK = 128
PAD_ROWS = CHUNK - N_META
S5_WIDTH = 512
S5_GROUP = 16
S5_GROUPS = 32
S5_STATE = 64
S5_COLS = 2 * S5_GROUPS * S5_STATE
S5_BLK = 4
M_HEADS = 4
M_DK = 128
M_DV = 256
D_FF = 4096
N_DEV = 8
ALPHA = 2.0 ** 0.25
LN_EPS = 1e-5
IN_NAT = 5640
P_V, P_O, P_GS, P_GM, P_QK, P_U, P_IF, PW = 0, 1024, 2048, 3072, 4096, 5120, 5632, 5760
N_U, N_Q, N_K, N_V, N_O, N_I, N_GS, N_GM = 0, 512, 1024, 1536, 2560, 3584, 3592, 4616

ADAM_LR, ADAM_B1, ADAM_B2, ADAM_EPS, ADAM_WD, ADAM_STEP = 0.001, 0.9, 0.999, 1e-08, 0.01, 10

VMEM_LIMIT = 56 * 1024 * 1024
MXU_WIDTH = 256
MESH = pl.DeviceIdType.MESH


def _pick(n, cands):
    for c in cands:
        if n % c == 0:
            return c
    raise ValueError(f"no tile for {n} among {cands}")


def _cparams(sem):
    return pltpu.CompilerParams(dimension_semantics=sem, vmem_limit_bytes=VMEM_LIMIT)


def _dot(a, b, ca, cb):
    return lax.dot_general(a, b, (((ca,), (cb,)), ((), ())), preferred_element_type=F32)


def _sigmoid(x):
    return 1.0 / (1.0 + jnp.exp(-x))


def _peer(k):
    x, y, c = lax.axis_index("x"), lax.axis_index("y"), lax.axis_index("c")
    px = 1 - x if k & 4 else x
    py = 1 - y if k & 2 else y
    pc = 1 - c if k & 1 else c
    return (px, py, pc), 4 * px + 2 * py + pc


def _my_id():
    return 4 * lax.axis_index("x") + 2 * lax.axis_index("y") + lax.axis_index("c")


def _hbm_call(body, name, arrs, out_shape, n_remote):
    n = len(arrs)
    return pl.pallas_call(
        body, name=name,
        out_shape=tuple(out_shape),
        in_specs=[pl.BlockSpec(memory_space=pl.ANY)] * n,
        out_specs=tuple([pl.BlockSpec(memory_space=pl.ANY)] * len(out_shape)),
        scratch_shapes=[pltpu.SemaphoreType.DMA((n, n_remote)),
                        pltpu.SemaphoreType.DMA((n, n_remote)),
                        pltpu.SemaphoreType.DMA((n,))],
    )(*arrs)


class _Gather:
    def __init__(self, ins, outs, send_sems, recv_sems, local_sems):
        self.ins, self.outs, self.n = ins, outs, len(ins)
        self.send_sems, self.recv_sems, self.local_sems = send_sems, recv_sems, local_sems
        x, y, c = lax.axis_index("x"), lax.axis_index("y"), lax.axis_index("c")
        self.c = c
        self.me, self.sibling = (x, y, c), (x, y, 1 - c)
        self.chips = [(1 - x, y), (x, 1 - y), (1 - x, 1 - y)]

    def slot(self, a, dev):
        return self.outs[a].at[4 * dev[0] + 2 * dev[1] + dev[2]]

    def copy(self, a, k, block, to, own=False):
        return pltpu.make_async_remote_copy(
            src_ref=self.ins[a] if own else self.slot(a, block), dst_ref=self.slot(a, block),
            send_sem=self.send_sems.at[a, k], recv_sem=self.recv_sems.at[a, k],
            device_id=to, device_id_type=MESH)

    def first_sends(self, a):
        return [self.copy(a, 0, self.me, self.sibling, own=True)] + [
            self.copy(a, 1 + j, self.me, (*chip, self.c), own=True) for j, chip in enumerate(self.chips)]

    def start(self):
        for a in range(self.n):
            pltpu.make_async_copy(self.ins[a], self.slot(a, self.me), self.local_sems.at[a]).start()
            for cp in self.first_sends(a):
                cp.start()

    def finish(self):
        forwards = []
        for j, chip in enumerate(self.chips):
            for a in range(self.n):
                self.copy(a, 1 + j, (*chip, self.c), self.me).wait_recv()
                fwd = self.copy(a, 4 + j, (*chip, self.c), self.sibling)
                fwd.start()
                forwards.append(fwd)
        for a in range(self.n):
            self.copy(a, 0, self.sibling, self.me).wait_recv()
            for j, chip in enumerate(self.chips):
                self.copy(a, 4 + j, (*chip, 1 - self.c), self.me).wait_recv()
        for a in range(self.n):
            for cp in self.first_sends(a):
                cp.wait_send()
            pltpu.make_async_copy(self.ins[a], self.slot(a, self.me), self.local_sems.at[a]).wait()
        for cp in forwards:
            cp.wait_send()

    @staticmethod
    def out_shapes(arrs):
        return [jax.ShapeDtypeStruct((N_DEV,) + tuple(a.shape), a.dtype) for a in arrs]


class _AllToAll:
    def __init__(self, ins, outs, send_sems, recv_sems, local_sems):
        self.ins, self.outs, self.n = ins, outs, len(ins)
        self.send_sems, self.recv_sems, self.local_sems = send_sems, recv_sems, local_sems
        self.me = _my_id()

    def copy(self, a, k, landing):
        peer, pid = _peer(k)
        return pltpu.make_async_remote_copy(
            src_ref=self.ins[a].at[pid], dst_ref=self.outs[a].at[pid if landing else self.me],
            send_sem=self.send_sems.at[a, k - 1], recv_sem=self.recv_sems.at[a, k - 1],
            device_id=peer, device_id_type=MESH)

    def local(self, a):
        return pltpu.make_async_copy(self.ins[a].at[self.me], self.outs[a].at[self.me], self.local_sems.at[a])

    def start(self):
        for a in range(self.n):
            self.local(a).start()
            for k in range(1, N_DEV):
                self.copy(a, k, False).start()

    def finish(self):
        for a in range(self.n):
            for k in range(1, N_DEV):
                self.copy(a, k, True).wait_recv()
        for a in range(self.n):
            for k in range(1, N_DEV):
                self.copy(a, k, False).wait_send()
            self.local(a).wait()

    @staticmethod
    def out_shapes(arrs):
        return [jax.ShapeDtypeStruct(tuple(a.shape), a.dtype) for a in arrs]


def _comm_scratch(n):
    return [pltpu.SemaphoreType.DMA((n, N_DEV - 1)), pltpu.SemaphoreType.DMA((n, N_DEV - 1)),
            pltpu.SemaphoreType.DMA((n,))]


def _ride_call(body, name, grid, in_specs, out_specs, out_shape, scratch, args, comm=None, aliases=None):
    n_in, n_out = len(in_specs), len(out_specs)
    in_specs, out_specs, out_shape = list(in_specs), list(out_specs), list(out_shape)
    scratch, args = list(scratch), list(args)
    kernel_fn = body
    if comm is not None:
        cls, arrs = comm
        n = len(arrs)

        def kernel_fn(*refs):
            ins, cin = refs[:n_in], refs[n_in:n_in + n]
            outs = refs[n_in + n:n_in + n + n_out]
            cout = refs[n_in + n + n_out:n_in + 2 * n + n_out]
            own_scratch, sems = refs[n_in + 2 * n + n_out:-3], refs[-3:]
            exchange = cls(cin, cout, *sems)
            ids = [pl.program_id(d) for d in range(len(grid))]
            first = functools.reduce(lambda a, b: a & b, [i == 0 for i in ids])
            last = functools.reduce(lambda a, b: a & b, [i == g - 1 for i, g in zip(ids, grid)])

            @pl.when(first)
            def _():
                exchange.start()

            body(*ins, *outs, *own_scratch)

            @pl.when(last)
            def _():
                exchange.finish()

        in_specs += [pl.BlockSpec(memory_space=pl.ANY)] * n
        args += list(arrs)
        out_specs += [pl.BlockSpec(memory_space=pl.ANY)] * n
        out_shape += cls.out_shapes(arrs)
        scratch += _comm_scratch(n)
    res = pl.pallas_call(
        kernel_fn, name=name, grid=grid,
        in_specs=in_specs, out_specs=tuple(out_specs), out_shape=tuple(out_shape),
        scratch_shapes=scratch, input_output_aliases=aliases or {},
        compiler_params=_cparams(("arbitrary",) * len(grid)),
    )(*args)
    return list(res[:n_out]), list(res[n_out:])


def _shard_tile(R, C):
    if R % 128 == 0:
        return 128, C
    return R, _pick(C, (256, 128))


def _allreduce_small(pack):
    rows = pack.shape[0]

    def gather(*refs):
        g = _Gather(refs[:1], refs[1:2], *refs[2:])
        g.start()
        g.finish()

    slots = _hbm_call(gather, "allreduce_gather", [pack], _Gather.out_shapes([pack]), N_DEV - 1)[0]

    def body(s_ref, o_ref):
        acc = s_ref[0]
        for s in range(1, N_DEV):
            acc = acc + s_ref[s]
        o_ref[...] = acc

    return pl.pallas_call(
        body, name="allreduce_sum", grid=(1,),
        in_specs=[pl.BlockSpec((N_DEV, rows, 128), lambda i: (0, 0, 0))],
        out_specs=pl.BlockSpec((rows, 128), lambda i: (0, 0)),
        out_shape=jax.ShapeDtypeStruct((rows, 128), F32),
        compiler_params=_cparams(("arbitrary",)),
    )(slots)


def _mm_tiles(M, N, K, mode, out_bytes, n_extra_f32, a_bytes, b_bytes):
    budget = 40 * 1024 * 1024
    tms = [t for t in (1664, 1408, 1152, 1024, 640, 512, 256, 128) if M % t == 0]
    tns = [t for t in (1152, 1024, 640, 512, 384, 256, 128) if N % t == 0]
    if mode == "tn":
        tks = [t for t in (1664, 1408, 640, 512, 256, 128) if K % t == 0]
    else:
        tks = [K] if K <= 1152 else [t for t in (1152, 1024, 640, 512) if K % t == 0]
    best = None
    for tm in tms:
        for tn in tns:
            for tk in tks:
                nk = K // tk
                use = 2 * (tm * tk * a_bytes + tk * tn * b_bytes) + 2 * tm * tn * out_bytes
                use += 2 * n_extra_f32 * tm * tn * 4 + tm * tn * 4 * (2 if nk > 1 else 1)
                if use > budget:
                    continue
                score = (tm * tn * tk, tm * tn)
                if best is None or score > best[0]:
                    best = (score, (tm, tn, tk))
    assert best is not None, (M, N, K, mode)
    return best[1]


def _mm(a, b, mode, name, *, bias=None, add=None, add_scale=1.0, sqrt_gate_of=None,
        relu2_out=False, colsum=None, out_dtype=F32, comm=None):
    if mode == "nn":
        (M, K), (K2, N) = a.shape, b.shape
    elif mode == "nt":
        (M, K), (N, K2) = a.shape, b.shape
    else:
        (K, M), (K2, N) = a.shape, b.shape
    assert K == K2, (a.shape, b.shape, mode)
    has_bias, has_add, has_gate = bias is not None, add is not None, sqrt_gate_of is not None
    tm, tn, tk = _mm_tiles(M, N, K, mode, jnp.dtype(out_dtype).itemsize, int(has_add) + int(has_gate),
                           a.dtype.itemsize, b.dtype.itemsize)
    nk = K // tk
    assert colsum is None or mode == "tn"
    assert colsum != "a" or N == tn
    comm_cls, comm_arrs = comm if comm is not None else (None, [])
    nc = len(comm_arrs)
    grid = (N // tn, M // tm, nk)

    def body(*refs):
        it = iter(refs)
        a_ref, b_ref = next(it), next(it)
        bias_ref = next(it) if has_bias else None
        add_ref = next(it) if has_add else None
        gate_ref = next(it) if has_gate else None
        comm_ins = [next(it) for _ in range(nc)]
        o_ref = next(it)
        cs_ref = next(it) if colsum else None
        comm_outs = [next(it) for _ in range(nc)]
        acc_ref = next(it) if nk > 1 else None
        j, i, k = pl.program_id(0), pl.program_id(1), pl.program_id(2)
        if nc:
            exchange = comm_cls(comm_ins, comm_outs, next(it), next(it), next(it))

            @pl.when((j == 0) & (i == 0) & (k == 0))
            def _():
                exchange.start()

        blocks = [slice(n0, min(n0 + MXU_WIDTH, tn)) for n0 in range(0, tn, MXU_WIDTH)]
        av = a_ref[...].astype(BF16)

        def part(cols):
            if mode == "nn":
                return _dot(av, b_ref[:, cols].astype(BF16), 1, 0)
            if mode == "nt":
                return _dot(av, b_ref[cols, :].astype(BF16), 1, 1)
            return _dot(av, b_ref[:, cols].astype(BF16), 0, 0)

        if colsum == "b":
            @pl.when((i == 0) & (k == 0))
            def _():
                cs_ref[...] = jnp.zeros_like(cs_ref)

            @pl.when(i == 0)
            def _():
                cs_ref[0:1, :] += jnp.sum(b_ref[...].astype(F32), axis=0, keepdims=True)
        if colsum == "a":
            @pl.when(k == 0)
            def _():
                cs_ref[...] = jnp.zeros_like(cs_ref)

            cs_ref[0:1, :] += jnp.sum(a_ref[...].astype(F32), axis=0, keepdims=True)

        def finish(cols, r):
            if has_bias:
                r = r + bias_ref[:, cols]
            if has_add:
                r = r + add_scale * add_ref[:, cols]
            if has_gate:
                r = r * (2.0 * jnp.sqrt(gate_ref[:, cols].astype(F32)))
            if relu2_out:
                r = jnp.square(jnp.maximum(r, 0.0))
            o_ref[:, cols] = r.astype(out_dtype)

        if nk == 1:
            for cols in blocks:
                finish(cols, part(cols))
        else:
            @pl.when(k == 0)
            def _():
                for cols in blocks:
                    acc_ref[:, cols] = part(cols)

            @pl.when((k > 0) & (k < nk - 1))
            def _():
                for cols in blocks:
                    acc_ref[:, cols] += part(cols)

            @pl.when(k == nk - 1)
            def _():
                for cols in blocks:
                    finish(cols, acc_ref[:, cols] + part(cols))

        if nc:
            @pl.when((j == grid[0] - 1) & (i == grid[1] - 1) & (k == nk - 1))
            def _():
                exchange.finish()

    if mode == "nn":
        a_spec = pl.BlockSpec((tm, tk), lambda j, i, k: (i, k))
        b_spec = pl.BlockSpec((tk, tn), lambda j, i, k: (k, j))
    elif mode == "nt":
        a_spec = pl.BlockSpec((tm, tk), lambda j, i, k: (i, k))
        b_spec = pl.BlockSpec((tn, tk), lambda j, i, k: (j, k))
    else:
        a_spec = pl.BlockSpec((tk, tm), lambda j, i, k: (k, i))
        b_spec = pl.BlockSpec((tk, tn), lambda j, i, k: (k, j))
    in_specs, args = [a_spec, b_spec], [a, b]
    if has_bias:
        in_specs.append(pl.BlockSpec((1, tn), lambda j, i, k: (0, j)))
        args.append(bias)
    if has_add:
        in_specs.append(pl.BlockSpec((tm, tn), lambda j, i, k: (i, j)))
        args.append(add)
    if has_gate:
        in_specs.append(pl.BlockSpec((tm, tn), lambda j, i, k: (i, j)))
        args.append(sqrt_gate_of)
    out_shape = [jax.ShapeDtypeStruct((M, N), out_dtype)]
    out_specs = [pl.BlockSpec((tm, tn), lambda j, i, k: (i, j))]
    if colsum == "b":
        out_shape.append(jax.ShapeDtypeStruct((8, N), F32))
        out_specs.append(pl.BlockSpec((8, tn), lambda j, i, k: (0, j)))
    if colsum == "a":
        out_shape.append(jax.ShapeDtypeStruct((8, M), F32))
        out_specs.append(pl.BlockSpec((8, tm), lambda j, i, k: (0, i)))
    scratch = [pltpu.VMEM((tm, tn), F32)] if nk > 1 else []
    if nc:
        in_specs += [pl.BlockSpec(memory_space=pl.ANY)] * nc
        args += list(comm_arrs)
        out_specs += [pl.BlockSpec(memory_space=pl.ANY)] * nc
        out_shape += comm_cls.out_shapes(comm_arrs)
        scratch += _comm_scratch(nc)
    res = pl.pallas_call(
        body, name=name, grid=grid,
        in_specs=in_specs, out_specs=tuple(out_specs), out_shape=tuple(out_shape),
        scratch_shapes=scratch,
        compiler_params=_cparams(("arbitrary", "arbitrary", "arbitrary")),
    )(*args)
    return res if len(res) > 1 else res[0]


def _ln_rows(v, g, b):
    mu = jnp.mean(v, axis=-1, keepdims=True)
    xc = v - mu
    var = jnp.mean(xc * xc, axis=-1, keepdims=True)
    return xc * lax.rsqrt(var + LN_EPS) * g + b


def _ln_bwd_rows(dy, v, g):
    mu = jnp.mean(v, axis=-1, keepdims=True)
    xc = v - mu
    var = jnp.mean(xc * xc, axis=-1, keepdims=True)
    rstd = lax.rsqrt(var + LN_EPS)
    xhat = xc * rstd
    dxh = dy * g
    dv = rstd * (dxh - jnp.mean(dxh, axis=-1, keepdims=True)
                 - xhat * jnp.mean(dxh * xhat, axis=-1, keepdims=True))
    return dv, xhat


def _real_tile(S):
    return _pick(S, (512, 256, 128, 64))


def _real_rows(rb, ncols, lp):
    return pl.BlockSpec((pl.Element(rb), pl.Element(ncols)),
                        lambda bb, j: (pl.multiple_of(bb * lp + CHUNK + j * rb, CHUNK), 0))


def _head_rows(ncols, lp):
    return pl.BlockSpec((CHUNK, ncols), lambda bb: (bb * (lp // CHUNK), 0))


def _ln0_fwd(x, g, b, comm=None):
    B, S, D = x.shape
    lp = S + CHUNK
    rb = _real_tile(S)

    def body(x_ref, g_ref, b_ref, h_ref, hb_ref):
        y = _ln_rows(x_ref[0], g_ref[...], b_ref[...])
        h_ref[...] = y
        hb_ref[...] = y.astype(BF16)

    vec = pl.BlockSpec((1, D), lambda bb, j: (0, 0))
    return _ride_call(
        body, "ln0_fwd", (B, S // rb),
        [pl.BlockSpec((1, rb, D), lambda bb, j: (bb, j, 0)), vec, vec],
        [_real_rows(rb, D, lp), _real_rows(rb, D, lp)],
        [jax.ShapeDtypeStruct((B * lp, D), F32), jax.ShapeDtypeStruct((B * lp, D), BF16)],
        [], (x, g, b), comm=comm)


def _ln0_head(meta, g, b, h, hb, lp):
    D = meta.shape[1]
    B = h.shape[0] // lp

    def head(meta_ref, g_ref, b_ref, hin_ref, hbin_ref, h_ref, hb_ref):
        m = _ln_rows(meta_ref[...], g_ref[...], b_ref[...])
        h_ref[0:PAD_ROWS, :] = jnp.zeros((PAD_ROWS, D), F32)
        h_ref[PAD_ROWS:CHUNK, :] = m
        hb_ref[0:PAD_ROWS, :] = jnp.zeros((PAD_ROWS, D), BF16)
        hb_ref[PAD_ROWS:CHUNK, :] = m.astype(BF16)

    vec1 = pl.BlockSpec((1, D), lambda bb: (0, 0))
    anyspec = pl.BlockSpec(memory_space=pl.ANY)
    return pl.pallas_call(
        head, name="ln0_head", grid=(B,),
        in_specs=[pl.BlockSpec((N_META, D), lambda bb: (0, 0)), vec1, vec1, anyspec, anyspec],
        out_specs=(_head_rows(D, lp), _head_rows(D, lp)),
        out_shape=(jax.ShapeDtypeStruct((B * lp, D), F32), jax.ShapeDtypeStruct((B * lp, D), BF16)),
        input_output_aliases={3: 0, 4: 1},
        compiler_params=_cparams(("arbitrary",)),
    )(meta, g, b, h, hb)


def _ln0_bwd(dh0, x, meta, g):
    B, S, D = x.shape
    lp = S + CHUNK
    rb = _real_tile(S)

    def body(dh_ref, x_ref, g_ref, dx_ref, dg_ref, db_ref):
        @pl.when((pl.program_id(0) == 0) & (pl.program_id(1) == 0))
        def _():
            dg_ref[...] = jnp.zeros_like(dg_ref)
            db_ref[...] = jnp.zeros_like(db_ref)

        dy = dh_ref[...]
        dv, xhat = _ln_bwd_rows(dy, x_ref[0], g_ref[...])
        dx_ref[0] = dv
        dg_ref[0:1, :] += jnp.sum(dy * xhat, axis=0, keepdims=True)
        db_ref[0:1, :] += jnp.sum(dy, axis=0, keepdims=True)

    const = lambda bb, j: (0, 0)
    xblk = pl.BlockSpec((1, rb, D), lambda bb, j: (bb, j, 0))
    acc_shape = jax.ShapeDtypeStruct((8, D), F32)
    dx, dg, db = pl.pallas_call(
        body, name="ln0_bwd", grid=(B, S // rb),
        in_specs=[_real_rows(rb, D, lp), xblk, pl.BlockSpec((1, D), const)],
        out_specs=(xblk, pl.BlockSpec((8, D), const), pl.BlockSpec((8, D), const)),
        out_shape=(jax.ShapeDtypeStruct((B, S, D), F32), acc_shape, acc_shape),
        compiler_params=_cparams(("arbitrary", "arbitrary")),
    )(dh0, x, g)

    def head(dh_ref, meta_ref, g_ref, dmeta_ref, dg_ref, db_ref):
        @pl.when(pl.program_id(0) == 0)
        def _():
            dmeta_ref[...] = jnp.zeros_like(dmeta_ref)
            dg_ref[...] = jnp.zeros_like(dg_ref)
            db_ref[...] = jnp.zeros_like(db_ref)

        dy = dh_ref[PAD_ROWS:CHUNK, :]
        dv, xhat = _ln_bwd_rows(dy, meta_ref[...], g_ref[...])
        dmeta_ref[...] += dv
        dg_ref[0:1, :] += jnp.sum(dy * xhat, axis=0, keepdims=True)
        db_ref[0:1, :] += jnp.sum(dy, axis=0, keepdims=True)

    c1 = lambda bb: (0, 0)
    dmeta, dgm, dbm = pl.pallas_call(
        head, name="ln0_bwd_head", grid=(B,),
        in_specs=[_head_rows(D, lp), pl.BlockSpec((N_META, D), c1), pl.BlockSpec((1, D), c1)],
        out_specs=(pl.BlockSpec((N_META, D), c1), pl.BlockSpec((8, D), c1), pl.BlockSpec((8, D), c1)),
        out_shape=(jax.ShapeDtypeStruct((N_META, D), F32), acc_shape, acc_shape),
        compiler_params=_cparams(("arbitrary",)),
    )(dh0, meta, g)
    return dx, dmeta, dg + dgm, db + dbm


def _ln_res_fwd(h_prev, r, g, b, name):
    T, D = h_prev.shape
    tr = _pick(T, (384, 320, 256, 128, 64))

    def body(hp_ref, r_ref, g_ref, b_ref, pre_ref, h_ref, hb_ref):
        pre = ALPHA * hp_ref[...] + r_ref[...]
        y = _ln_rows(pre, g_ref[...], b_ref[...])
        pre_ref[...] = pre
        h_ref[...] = y
        hb_ref[...] = y.astype(BF16)

    row = pl.BlockSpec((tr, D), lambda i: (i, 0))
    vec = pl.BlockSpec((1, D), lambda i: (0, 0))
    return pl.pallas_call(
        body, name=name, grid=(T // tr,),
        in_specs=[row, row, vec, vec], out_specs=(row, row, row),
        out_shape=(jax.ShapeDtypeStruct((T, D), F32), jax.ShapeDtypeStruct((T, D), F32),
                   jax.ShapeDtypeStruct((T, D), BF16)),
        compiler_params=_cparams(("arbitrary",)),
    )(h_prev, r, g, b)


def _ln_bwd(dh, pre, g, name):
    T, D = dh.shape
    tr = _pick(T, (384, 320, 256, 128, 64))

    def body(dh_ref, pre_ref, g_ref, dp_ref, dpb_ref, dg_ref, db_ref):
        @pl.when(pl.program_id(0) == 0)
        def _():
            dg_ref[...] = jnp.zeros_like(dg_ref)
            db_ref[...] = jnp.zeros_like(db_ref)

        dy = dh_ref[...]
        dv, xhat = _ln_bwd_rows(dy, pre_ref[...], g_ref[...])
        dp_ref[...] = dv
        dpb_ref[...] = dv.astype(BF16)
        dg_ref[0:1, :] += jnp.sum(dy * xhat, axis=0, keepdims=True)
        db_ref[0:1, :] += jnp.sum(dy, axis=0, keepdims=True)

    row = pl.BlockSpec((tr, D), lambda i: (i, 0))
    vec = pl.BlockSpec((1, D), lambda i: (0, 0))
    acc = pl.BlockSpec((8, D), lambda i: (0, 0))
    return pl.pallas_call(
        body, name=name, grid=(T // tr,),
        in_specs=[row, row, vec], out_specs=(row, row, acc, acc),
        out_shape=(jax.ShapeDtypeStruct((T, D), F32), jax.ShapeDtypeStruct((T, D), BF16),
                   jax.ShapeDtypeStruct((8, D), F32), jax.ShapeDtypeStruct((8, D), F32)),
        compiler_params=_cparams(("arbitrary",)),
    )(dh, pre, g)


def _ln2_loss(h1, ff, g, b, target):
    T, D = h1.shape
    B, S, _ = target.shape
    lp = S + CHUNK
    rb = _real_tile(S)

    def body(h_ref, ff_ref, g_ref, b_ref, t_ref, loss_ref, dp_ref, dpb_ref, dg_ref, db_ref):
        @pl.when((pl.program_id(0) == 0) & (pl.program_id(1) == 0))
        def _():
            loss_ref[...] = jnp.zeros_like(loss_ref)
            dg_ref[...] = jnp.zeros_like(dg_ref)
            db_ref[...] = jnp.zeros_like(db_ref)

        pre = ALPHA * h_ref[...] + ff_ref[...]
        gg = g_ref[...]
        y = _ln_rows(pre, gg, b_ref[...])
        err = y - t_ref[0]
        loss_ref[0:1, 0:1] += 0.5 * jnp.sum(jnp.mean(err * err, axis=-1, keepdims=True), axis=0, keepdims=True)
        dy = err * (1.0 / D)
        dv, xhat = _ln_bwd_rows(dy, pre, gg)
        dp_ref[...] = dv
        dpb_ref[...] = dv.astype(BF16)
        dg_ref[0:1, :] += jnp.sum(dy * xhat, axis=0, keepdims=True)
        db_ref[0:1, :] += jnp.sum(dy, axis=0, keepdims=True)

    row = _real_rows(rb, D, lp)
    const = lambda bb, j: (0, 0)
    loss, dp, dpb, dg, db = pl.pallas_call(
        body, name="ln2_loss", grid=(B, S // rb),
        in_specs=[row, row, pl.BlockSpec((1, D), const), pl.BlockSpec((1, D), const),
                  pl.BlockSpec((1, rb, D), lambda bb, j: (bb, j, 0))],
        out_specs=(pl.BlockSpec((8, 128), const), row, row,
                   pl.BlockSpec((8, D), const), pl.BlockSpec((8, D), const)),
        out_shape=(jax.ShapeDtypeStruct((8, 128), F32),
                   jax.ShapeDtypeStruct((T, D), F32), jax.ShapeDtypeStruct((T, D), BF16),
                   jax.ShapeDtypeStruct((8, D), F32), jax.ShapeDtypeStruct((8, D), F32)),
        compiler_params=_cparams(("arbitrary", "arbitrary")),
    )(h1, ff, g, b, target)

    def head(dpin_ref, dpbin_ref, dp_ref, dpb_ref):
        dp_ref[...] = jnp.zeros((CHUNK, D), F32)
        dpb_ref[...] = jnp.zeros((CHUNK, D), BF16)

    anyspec = pl.BlockSpec(memory_space=pl.ANY)
    dp, dpb = pl.pallas_call(
        head, name="ln2_head", grid=(B,),
        in_specs=[anyspec, anyspec],
        out_specs=(_head_rows(D, lp), _head_rows(D, lp)),
        out_shape=(jax.ShapeDtypeStruct((T, D), F32), jax.ShapeDtypeStruct((T, D), BF16)),
        input_output_aliases={0: 0, 1: 1},
        compiler_params=_cparams(("arbitrary",)),
    )(dp, dpb)
    return loss, dp, dpb, dg, db


def _s5_prep(lam_re, lam_im, log_dt, b_re, b_im, c_re, c_im):
    dt = jnp.exp(log_dt)[:, None]
    mag = jnp.exp(lam_re * dt)
    ar = mag * jnp.cos(lam_im * dt)
    ai = mag * jnp.sin(lam_im * dt)
    nr, ni = ar - 1.0, ai
    den = lam_re * lam_re + lam_im * lam_im
    cr = (nr * lam_re + ni * lam_im) / den
    ci = (ni * lam_re - nr * lam_im) / den
    bbr = cr[..., None] * b_re - ci[..., None] * b_im
    bbi = cr[..., None] * b_im + ci[..., None] * b_re
    eye = jnp.eye(8, dtype=F32)
    bb = jnp.stack([bbr, bbi]).reshape(2, S5_BLK, 8, S5_STATE, S5_GROUP)
    bblk = jnp.einsum("rbgph,gj->bghrjp", bb, eye).reshape(S5_BLK, 128, 1024)
    cc = jnp.stack([c_re, -c_im]).reshape(2, S5_BLK, 8, S5_GROUP, S5_STATE)
    cblk = jnp.einsum("rbghp,gj->brjpgh", cc, eye).reshape(S5_BLK, 1024, 128)
    return ar.reshape(1, 2048), ai.reshape(1, 2048), bblk, cblk


def _scan_tables(ar, ai):
    pr, pi = [ar], [ai]
    for _ in range(7):
        pr, pi = pr + [pr[-1] * ar - pi[-1] * ai], pi + [pr[-1] * ai + pi[-1] * ar]
    pw_r = jnp.concatenate(pr, axis=0)
    pw_i = jnp.concatenate(pi, axis=0)
    rev_r = jnp.concatenate(pr[::-1], axis=0)
    rev_i = jnp.concatenate(pi[::-1], axis=0)
    row = jnp.arange(8)[:, None]

    def tables(sign, reverse):
        rows = []
        for n, sh in ((0, 1), (1, 2), (3, 4)):
            mask = (row < 8 - sh) if reverse else (row >= sh)
            rows.append(jnp.where(mask, pw_r[n][None, :], 0.0))
            rows.append(jnp.where(mask, sign * pw_i[n][None, :], 0.0))
        cr_ = rev_r if reverse else pw_r
        ci_ = rev_i if reverse else pw_i
        rows += [cr_, sign * ci_]
        return jnp.stack(rows)

    return tables(1.0, False), tables(-1.0, True)


def _seg_scan(s_ref, row0, seg, lam_ref, tab_ref, carry_ref, reverse, cseg_ref=None, extra=None):
    sgn = -1.0 if reverse else 1.0
    take, edge = (0, 7) if reverse else (7, 0)
    rowid = lax.broadcasted_iota(jnp.int32, (8, 128), 0)
    all_pairs = [(blk * 1024 + j * 128, blk * 1024 + j * 128 + 512, blk * 512 + j * 128)
                 for blk in range(S5_BLK) for j in range(4)]

    def rows(it):
        i = (seg - 1 - it) if reverse else it
        return i, pl.multiple_of(row0 + i * 8, 8)

    for half in range(2):
        pairs = all_pairs[8 * half:8 * half + 8]

        def pass1(it, carry):
            _, r0 = rows(it)
            out = []
            for n, (cre, cim, tc) in enumerate(pairs):
                lr = lam_ref[0, :, tc:tc + 128]
                li = sgn * lam_ref[1, :, tc:tc + 128]
                pr, pi = carry[2 * n], carry[2 * n + 1]
                xr = lr * pr - li * pi + s_ref[pl.ds(r0, 8), cre:cre + 128]
                xi = lr * pi + li * pr + s_ref[pl.ds(r0, 8), cim:cim + 128]
                s_ref[pl.ds(r0, 8), cre:cre + 128] = xr
                s_ref[pl.ds(r0, 8), cim:cim + 128] = xi
                out += [xr, xi]
            return tuple(out)

        ends = lax.fori_loop(0, seg, pass1, tuple(jnp.zeros((8, 128), F32) for _ in range(16)))

        start = []
        for n, (cre, cim, tc) in enumerate(pairs):
            xr, xi = ends[2 * n], ends[2 * n + 1]
            for lvl, sh in enumerate((1, 2, 4)):
                lr = tab_ref[2 * lvl, :, tc:tc + 128]
                li = tab_ref[2 * lvl + 1, :, tc:tc + 128]
                shift = (8 - sh) if reverse else sh
                sr = pltpu.roll(xr, shift, 0)
                si = pltpu.roll(xi, shift, 0)
                xr, xi = xr + lr * sr - li * si, xi + lr * si + li * sr
            pr = tab_ref[6, :, tc:tc + 128]
            pi = tab_ref[7, :, tc:tc + 128]
            c_r = carry_ref[:, cre:cre + 128]
            c_i = carry_ref[:, cim:cim + 128]
            er, ei = xr + pr * c_r - pi * c_i, xi + pr * c_i + pi * c_r
            back = 7 if reverse else 1
            in_r = jnp.where(rowid == edge, c_r, pltpu.roll(er, back, 0))
            in_i = jnp.where(rowid == edge, c_i, pltpu.roll(ei, back, 0))
            nr = jnp.sum(jnp.where(rowid == take, er, 0.0), axis=0, keepdims=True)
            ni = jnp.sum(jnp.where(rowid == take, ei, 0.0), axis=0, keepdims=True)
            carry_ref[:, cre:cre + 128] = jnp.broadcast_to(nr, (8, 128))
            carry_ref[:, cim:cim + 128] = jnp.broadcast_to(ni, (8, 128))
            if cseg_ref is not None:
                cseg_ref[0:8, cre:cre + 128] = in_r
                cseg_ref[0:8, cim:cim + 128] = in_i
            start += [in_r, in_i]

        def pass2(it, carry):
            i, r0 = rows(it)
            out = []
            for n, (cre, cim, tc) in enumerate(pairs):
                lr = lam_ref[0, :, tc:tc + 128]
                li = sgn * lam_ref[1, :, tc:tc + 128]
                dr, di = carry[2 * n], carry[2 * n + 1]
                dr, di = lr * dr - li * di, lr * di + li * dr
                xr = s_ref[pl.ds(r0, 8), cre:cre + 128] + dr
                xi = s_ref[pl.ds(r0, 8), cim:cim + 128] + di
                s_ref[pl.ds(r0, 8), cre:cre + 128] = xr
                s_ref[pl.ds(r0, 8), cim:cim + 128] = xi
                if extra is not None:
                    extra(i, cre, cim, tc, xr, xi)
                out += [dr, di]
            return tuple(out)

        lax.fori_loop(0, seg, pass2, tuple(start))


def _to_segments(src_refs, dst_ref, seg, first=None):
    sub = lax.broadcasted_iota(jnp.int32, (8, 1), 0)
    for i in range(seg):
        for c, src in enumerate(src_refs):
            v = src[pl.ds(i, 8, stride=seg), :]
            if first is not None:
                v = jnp.where(first & (sub * seg + i < PAD_ROWS), 0.0, v)
            dst_ref[8 * i:8 * i + 8, c * 128:(c + 1) * 128] = v


def _from_segments(src_ref, dst_ref, seg, fn=None, zero_head=None):
    m = seg // 8
    for j in range(seg):
        for c in range(src_ref.shape[0]):
            v = src_ref[c, pl.ds(64 * (j % m) + j // m, 8, stride=8), :]
            if zero_head is not None and j < PAD_ROWS // 8:
                v = jnp.where(zero_head, 0.0, v)
            dst_ref[8 * j:8 * j + 8, c * 128:(c + 1) * 128] = v if fn is None else fn(v)


def _gelu(x):
    c = math.sqrt(2.0 / math.pi)
    t = jnp.tanh(c * (x + 0.044715 * x * x * x))
    return 0.5 * x * (1.0 + t)


def _gelu_grad(x):
    c = math.sqrt(2.0 / math.pi)
    t = jnp.tanh(c * (x + 0.044715 * x * x * x))
    return 0.5 * (1.0 + t) + 0.5 * x * (1.0 - t * t) * c * (1.0 + 3.0 * 0.044715 * x * x)


def _s5_tile(lp):
    return _pick(lp, (384, 320, 256, 128, 64))


def _s5_fwd(p, bblk, cblk, lam8, tab_f, dskip, lp, comm=None):
    T = p.shape[0]
    ts = _s5_tile(lp)
    seg = ts // 8
    nblk, per_seq = T // ts, lp // ts
    ucol = 0

    def body(u0, u1, u2, u3, b_ref, c_ref, lam_ref, tab_ref, d_ref, y_ref, gy_ref, cin_ref, s_sc, u_sc, y_sc, carry_sc):
        r = pl.program_id(0)
        first = (r % per_seq) == 0

        @pl.when(first)
        def _():
            carry_sc[...] = jnp.zeros_like(carry_sc)

        cin_ref[0] = carry_sc[...]
        _to_segments((u0, u1, u2, u3), u_sc, seg, first)
        u = u_sc[...]
        ub = u.astype(BF16)
        for blk in range(S5_BLK):
            s_sc[:, blk * 1024:(blk + 1) * 1024] = _dot(ub[:, blk * 128:(blk + 1) * 128], b_ref[blk], 1, 0)
        _seg_scan(s_sc, 0, seg, lam_ref, tab_ref, carry_sc, False)
        for blk in range(S5_BLK):
            sb = s_sc[:, blk * 1024:(blk + 1) * 1024].astype(BF16)
            y_sc[blk] = _dot(sb, c_ref[blk], 1, 0) + d_ref[:, blk * 128:(blk + 1) * 128] * u[:, blk * 128:(blk + 1) * 128]
        _from_segments(y_sc, y_ref, seg)
        gy_ref[...] = _gelu(y_ref[...]).astype(BF16)

    def ublock(c):
        return pl.BlockSpec((ts, 128), lambda r: (r, 4 * ucol + c))

    return _ride_call(
        body, "s5_fwd", (nblk,),
        [ublock(0), ublock(1), ublock(2), ublock(3),
         pl.BlockSpec((S5_BLK, 128, 1024), lambda r: (0, 0, 0)),
         pl.BlockSpec((S5_BLK, 1024, 128), lambda r: (0, 0, 0)),
         pl.BlockSpec((2, 8, 2048), lambda r: (0, 0, 0)),
         pl.BlockSpec((8, 8, 2048), lambda r: (0, 0, 0)),
         pl.BlockSpec((1, S5_WIDTH), lambda r: (0, 0))],
        [pl.BlockSpec((ts, S5_WIDTH), lambda r: (r, 0)),
         pl.BlockSpec((ts, S5_WIDTH), lambda r: (r, 0)),
         pl.BlockSpec((1, 8, S5_COLS), lambda r: (r, 0, 0))],
        [jax.ShapeDtypeStruct((T, S5_WIDTH), F32),
         jax.ShapeDtypeStruct((T, S5_WIDTH), BF16),
         jax.ShapeDtypeStruct((nblk, 8, S5_COLS), F32)],
        [pltpu.VMEM((ts, S5_COLS), F32), pltpu.VMEM((ts, S5_WIDTH), F32),
         pltpu.VMEM((S5_BLK, ts, 128), F32), pltpu.VMEM((8, S5_COLS), F32)],
        (p, p, p, p, bblk.astype(BF16), cblk.astype(BF16), lam8, tab_f, dskip), comm=comm)


def _s5_bwd(p, y_pre, dgy, cin, bblk, cblk, lam8, tab_f, tab_b, dskip, dif, dp, lp, comm=None):
    T = p.shape[0]
    ts = _s5_tile(lp)
    seg = ts // 8
    nblk, per_seq = T // ts, lp // ts
    ucol = 0
    assert P_IF == P_U + S5_WIDTH and P_U % (S5_WIDTH + 128) == 0

    def body(u0, u1, u2, u3, y0, y1, y2, y3, g0, g1, g2, g3, cin_ref, b_ref, bt_ref, c_ref, ct_ref, lam_ref, tf_ref,
             tb_ref, d_ref, dif_ref, dpin_ref, du_ref, dbb_ref, dcb_ref, dlam_ref, dd_ref, s_sc, a_sc, u_sc, dy_sc,
             y_sc, w_sc, carry_sc, carry_b):
        t = pl.program_id(0)
        r = nblk - 1 - t
        first = (r % per_seq) == 0
        last = (r % per_seq) == per_seq - 1

        @pl.when(t == 0)
        def _():
            dbb_ref[...] = jnp.zeros_like(dbb_ref)
            dcb_ref[...] = jnp.zeros_like(dcb_ref)
            dlam_ref[...] = jnp.zeros_like(dlam_ref)
            dd_ref[...] = jnp.zeros_like(dd_ref)

        @pl.when(last)
        def _():
            carry_b[...] = jnp.zeros_like(carry_b)

        carry_sc[...] = cin_ref[0]
        _to_segments((u0, u1, u2, u3), u_sc, seg, first)
        u = u_sc[...]
        ub = u.astype(BF16)
        for blk in range(S5_BLK):
            s_sc[8:8 + ts, blk * 1024:(blk + 1) * 1024] = _dot(ub[:, blk * 128:(blk + 1) * 128], b_ref[blk], 1, 0)
        _seg_scan(s_sc, 8, seg, lam_ref, tf_ref, carry_sc, False, cseg_ref=s_sc)

        _to_segments((g0, g1, g2, g3), dy_sc, seg)
        _to_segments((y0, y1, y2, y3), y_sc, seg)
        dy = dy_sc[...] * _gelu_grad(y_sc[...])
        dy_sc[...] = dy
        dyb = dy.astype(BF16)
        dd_ref[0:1, :] += jnp.sum(dy * u, axis=0, keepdims=True)
        for blk in range(S5_BLK):
            a_sc[:, blk * 1024:(blk + 1) * 1024] = _dot(dyb[:, blk * 128:(blk + 1) * 128], ct_ref[blk], 1, 0)
            sb = s_sc[8:8 + ts, blk * 1024:(blk + 1) * 1024].astype(BF16)
            dcb_ref[blk] += _dot(sb, dyb[:, blk * 128:(blk + 1) * 128], 0, 0)

        def lam_grad(i, cre, cim, tc, a_r, a_i):
            r0 = pl.multiple_of(i * 8, 8)
            pr = s_sc[pl.ds(r0, 8), cre:cre + 128]
            pi = s_sc[pl.ds(r0, 8), cim:cim + 128]
            dlam_ref[0, :, tc:tc + 128] += a_r * pr + a_i * pi
            dlam_ref[1, :, tc:tc + 128] += a_i * pr - a_r * pi

        _seg_scan(a_sc, 0, seg, lam_ref, tb_ref, carry_b, True, extra=lam_grad)

        for blk in range(S5_BLK):
            ab = a_sc[:, blk * 1024:(blk + 1) * 1024].astype(BF16)
            w_sc[blk] = _dot(ab, bt_ref[blk], 1, 0) + d_ref[:, blk * 128:(blk + 1) * 128] * dy_sc[:, blk * 128:(blk + 1) * 128]
            dbb_ref[blk] += _dot(u_sc[:, blk * 128:(blk + 1) * 128].astype(BF16), ab, 0, 0)
        _from_segments(w_sc, y_sc, seg, zero_head=first)
        du_ref[:, 0:S5_WIDTH] = y_sc[...].astype(BF16)
        du_ref[:, S5_WIDTH:S5_WIDTH + 128] = dif_ref[...]

    const3 = lambda t: (0, 0, 0)
    rev = lambda t: (nblk - 1 - t, 0)

    def lanes(c0):
        return [pl.BlockSpec((ts, 128), lambda t, cc=c0 + c: (nblk - 1 - t, cc)) for c in range(4)]

    return _ride_call(
        body, "s5_bwd", (nblk,),
        lanes(4 * ucol) + lanes(0) + lanes(0) + [
            pl.BlockSpec((1, 8, S5_COLS), lambda t: (nblk - 1 - t, 0, 0)),
            pl.BlockSpec((S5_BLK, 128, 1024), const3),
            pl.BlockSpec((S5_BLK, 1024, 128), const3),
            pl.BlockSpec((S5_BLK, 1024, 128), const3),
            pl.BlockSpec((S5_BLK, 128, 1024), const3),
            pl.BlockSpec((2, 8, 2048), const3),
            pl.BlockSpec((8, 8, 2048), const3),
            pl.BlockSpec((8, 8, 2048), const3),
            pl.BlockSpec((1, S5_WIDTH), lambda t: (0, 0)),
            pl.BlockSpec((ts, 128), rev),
            pl.BlockSpec(memory_space=pl.ANY)],
        [pl.BlockSpec((ts, S5_WIDTH + 128), lambda t: (nblk - 1 - t, P_U // (S5_WIDTH + 128))),
         pl.BlockSpec((S5_BLK, 128, 1024), const3),
         pl.BlockSpec((S5_BLK, 1024, 128), const3),
         pl.BlockSpec((2, 8, 2048), const3),
         pl.BlockSpec((8, S5_WIDTH), lambda t: (0, 0))],
        [jax.ShapeDtypeStruct((T, PW), BF16),
         jax.ShapeDtypeStruct((S5_BLK, 128, 1024), F32),
         jax.ShapeDtypeStruct((S5_BLK, 1024, 128), F32),
         jax.ShapeDtypeStruct((2, 8, 2048), F32),
         jax.ShapeDtypeStruct((8, S5_WIDTH), F32)],
        [pltpu.VMEM((ts + 8, S5_COLS), F32), pltpu.VMEM((ts, S5_COLS), F32),
         pltpu.VMEM((ts, S5_WIDTH), F32), pltpu.VMEM((ts, S5_WIDTH), F32),
         pltpu.VMEM((ts, S5_WIDTH), F32), pltpu.VMEM((S5_BLK, ts, 128), F32),
         pltpu.VMEM((8, S5_COLS), F32), pltpu.VMEM((8, S5_COLS), F32)],
        (p, p, p, p, y_pre, y_pre, y_pre, y_pre, dgy, dgy, dgy, dgy, cin,
         bblk.astype(BF16), jnp.swapaxes(bblk, 1, 2).astype(BF16),
         cblk.astype(BF16), jnp.swapaxes(cblk, 1, 2).astype(BF16), lam8, tab_f, tab_b, dskip, dif, dp),
        comm=comm, aliases={22: 0})


CONV_HALO = 16


def _row_in_seq(i, tr, lp):
    rowid = lax.broadcasted_iota(jnp.int32, (tr, 1), 0)
    return (i * tr + rowid) % lp


def _conv_fwd(p, w, b, lp):
    T = p.shape[0]
    tr = _pick(T, (384, 320, 256, 128, 64))
    c = P_QK // 1024
    hl = CONV_HALO

    def body(x_ref, xp_ref, w_ref, b_ref, o_ref):
        i = pl.program_id(0)
        pos = _row_in_seq(i, tr, lp)
        x = jnp.where(pos < PAD_ROWS, 0.0, x_ref[...].astype(F32))
        pos_p = (i * tr - hl + lax.broadcasted_iota(jnp.int32, (hl, 1), 0)) % lp
        xp = jnp.where((pos_p < PAD_ROWS) | (i == 0), 0.0, xp_ref[...].astype(F32))
        xx = jnp.concatenate([xp, x], axis=0)
        acc = b_ref[...] + w_ref[3:4, :] * x
        for s in (1, 2, 3):
            acc = acc + w_ref[3 - s:4 - s, :] * pltpu.roll(xx, s, 0)[hl:hl + tr]
        o_ref[...] = acc * _sigmoid(acc)

    return pl.pallas_call(
        body, name="conv_fwd", grid=(T // tr,),
        in_specs=[pl.BlockSpec((tr, 1024), lambda i: (i, c)),
                  pl.BlockSpec((hl, 1024), lambda i: (jnp.maximum(i * (tr // hl) - 1, 0), c)),
                  pl.BlockSpec((4, 1024), lambda i: (0, 0)),
                  pl.BlockSpec((1, 1024), lambda i: (0, 0))],
        out_specs=pl.BlockSpec((tr, 1024), lambda i: (i, 0)),
        out_shape=jax.ShapeDtypeStruct((T, 1024), F32),
        compiler_params=_cparams(("arbitrary",)),
    )(p, p, w, b)


def _conv_bwd(p, dqk, w, b, dp, lp):
    T = p.shape[0]
    tr = _pick(T, (384, 320, 256, 128, 64))
    c = P_QK // 1024
    nb = T // tr
    hl = CONV_HALO

    def body(x_ref, xp_ref, xn_ref, g_ref, gn_ref, w_ref, b_ref, dpin_ref, dx_ref, dw_ref, db_ref, x_sc, g_sc, part):
        i = pl.program_id(0)

        @pl.when(i == 0)
        def _():
            dw_ref[...] = jnp.zeros_like(dw_ref)
            db_ref[...] = jnp.zeros_like(db_ref)

        def seqpos(off, n):
            return (i * tr + off + lax.broadcasted_iota(jnp.int32, (n, 1), 0)) % lp

        x_sc[0:hl, :] = jnp.where((seqpos(-hl, hl) < PAD_ROWS) | (i == 0), 0.0, xp_ref[...].astype(F32))
        x_sc[hl:hl + tr, :] = jnp.where(seqpos(0, tr) < PAD_ROWS, 0.0, x_ref[...].astype(F32))
        x_sc[hl + tr:, :] = jnp.where((seqpos(tr, hl) < PAD_ROWS) | (i == nb - 1), 0.0, xn_ref[...].astype(F32))
        g_sc[0:tr, :] = g_ref[...].astype(F32)
        g_sc[tr:, :] = gn_ref[...].astype(F32)
        part[...] = jnp.zeros_like(part)
        taps = [w_ref[t:t + 1, :] for t in range(4)]
        bias = b_ref[...]

        ngroups = tr // hl

        def group(g, prev):
            r0 = pl.multiple_of(g * hl, hl)
            win = x_sc[pl.ds(r0, 2 * hl), :]
            sh = [win] + [pltpu.roll(win, s, 0) for s in (1, 2, 3)]
            acc = bias
            for s in range(4):
                acc = acc + taps[3 - s] * sh[s][hl:2 * hl]
            sg = _sigmoid(acc)
            rows = r0 + lax.broadcasted_iota(jnp.int32, (hl, 1), 0)
            pos = (i * tr + rows) % lp
            valid = (pos >= PAD_ROWS) & ((g < ngroups) | (i < nb - 1))
            cur = jnp.where(valid, g_sc[pl.ds(r0, hl), :] * (sg * (1.0 + acc * (1.0 - sg))), 0.0)
            own = jnp.where(g < ngroups, cur, 0.0)
            part[4] += own
            for s in range(4):
                part[3 - s] += own * sh[s][hl:2 * hl]

            @pl.when(g > 0)
            def _():
                both = jnp.concatenate([prev, cur], axis=0)
                dx = taps[3] * prev
                for s in (1, 2, 3):
                    dx = dx + taps[3 - s] * pltpu.roll(both, 2 * hl - s, 0)[0:hl]
                q0 = pl.multiple_of(r0 - hl, hl)
                pos_prev = (i * tr + q0 + lax.broadcasted_iota(jnp.int32, (hl, 1), 0)) % lp
                dx_ref[pl.ds(q0, hl), :] = jnp.where(pos_prev < PAD_ROWS, 0.0, dx).astype(BF16)

            return cur

        lax.fori_loop(0, ngroups + 1, group, jnp.zeros((hl, 1024), F32))
        db_ref[0:1, :] += jnp.sum(part[4], axis=0, keepdims=True)
        for t in range(4):
            dw_ref[t:t + 1, :] += jnp.sum(part[t], axis=0, keepdims=True)

    t8 = tr // hl
    return pl.pallas_call(
        body, name="conv_bwd", grid=(nb,),
        in_specs=[pl.BlockSpec((tr, 1024), lambda i: (i, c)),
                  pl.BlockSpec((hl, 1024), lambda i: (jnp.maximum(i * t8 - 1, 0), c)),
                  pl.BlockSpec((hl, 1024), lambda i: (jnp.minimum((i + 1) * t8, nb * t8 - 1), c)),
                  pl.BlockSpec((tr, 1024), lambda i: (i, 0)),
                  pl.BlockSpec((hl, 1024), lambda i: (jnp.minimum((i + 1) * t8, nb * t8 - 1), 0)),
                  pl.BlockSpec((4, 1024), lambda i: (0, 0)),
                  pl.BlockSpec((1, 1024), lambda i: (0, 0)),
                  pl.BlockSpec(memory_space=pl.ANY)],
        out_specs=(pl.BlockSpec((tr, 1024), lambda i: (i, c)),
                   pl.BlockSpec((8, 1024), lambda i: (0, 0)),
                   pl.BlockSpec((8, 1024), lambda i: (0, 0))),
        out_shape=(jax.ShapeDtypeStruct((T, PW), BF16),
                   jax.ShapeDtypeStruct((8, 1024), F32),
                   jax.ShapeDtypeStruct((8, 1024), F32)),
        input_output_aliases={7: 0},
        scratch_shapes=[pltpu.VMEM((tr + 2 * hl, 1024), F32), pltpu.VMEM((tr + hl, 1024), F32),
                        pltpu.VMEM((5, hl, 1024), F32)],
        compiler_params=_cparams(("arbitrary",)),
    )(p, p, p, dqk, dqk, w, b, dp)


def _split3(x):
    hi = x.astype(BF16)
    r1 = x - hi.astype(F32)
    mid = r1.astype(BF16)
    lo = (r1 - mid.astype(F32)).astype(BF16)
    return hi, mid, lo


def _tri_sum(x, upper):
    r = lax.broadcasted_iota(jnp.int32, (CHUNK, CHUNK), 0)
    c = lax.broadcasted_iota(jnp.int32, (CHUNK, CHUNK), 1)
    tri = jnp.where((r <= c) if upper else (r >= c), 1.0, 0.0).astype(BF16)
    hi, mid, lo = _split3(x)
    return _dot(tri, hi, 1, 0) + _dot(tri, mid, 1, 0) + _dot(tri, lo, 1, 0)


def _lane_col(x, lane):
    l = lax.broadcasted_iota(jnp.int32, x.shape, 1)
    return jnp.sum(jnp.where(l == lane, x, 0.0), axis=1, keepdims=True)


def _to_row(col):
    r = lax.broadcasted_iota(jnp.int32, (CHUNK, CHUNK), 0)
    c = lax.broadcasted_iota(jnp.int32, (CHUNK, CHUNK), 1)
    return jnp.sum(jnp.where(r == c, col, 0.0), axis=0, keepdims=True)


def _to_col(row):
    r = lax.broadcasted_iota(jnp.int32, (CHUNK, CHUNK), 0)
    c = lax.broadcasted_iota(jnp.int32, (CHUNK, CHUNK), 1)
    return jnp.sum(jnp.where(r == c, row, 0.0), axis=1, keepdims=True)


def _log_sigmoid(x):
    return jnp.minimum(x, 0.0) - jnp.log(1.0 + jnp.exp(-jnp.abs(x)))


def _mlstm_gates(ifv, padmask):
    lf = jnp.where(padmask, 0.0, _log_sigmoid(ifv))
    b_all = _tri_sum(lf, False)
    li = jnp.where(padmask, -jnp.inf, ifv)
    return li, b_all


def _mlstm_head_fwd(q, k, v, li_col, b_col, c_st, n_st, m_st):
    r = lax.broadcasted_iota(jnp.int32, (CHUNK, CHUNK), 0)
    c = lax.broadcasted_iota(jnp.int32, (CHUNK, CHUNK), 1)
    rowid = lax.broadcasted_iota(jnp.int32, (CHUNK, 1), 0)
    b_row = _to_row(b_col)
    li_row = _to_row(li_col)
    dmat = jnp.where(r >= c, b_col - b_row + li_row, -jnp.inf)
    m_inter = b_col + m_st
    m_row = jnp.maximum(m_inter, jnp.max(dmat, axis=1, keepdims=True))
    w_intra = jnp.exp(dmat - m_row)
    w_inter = jnp.exp(m_inter - m_row)
    qb, kb, vb = q.astype(BF16), k.astype(BF16), v.astype(BF16)
    qk = _dot(qb, kb, 1, 1)
    s = qk * w_intra
    cb = c_st.astype(BF16)
    qc = _dot(qb, cb, 1, 0)
    qn = jnp.sum(q * n_st, axis=1, keepdims=True)
    num = _dot(s.astype(BF16), vb, 1, 0) + w_inter * qc
    den = jnp.sum(s, axis=1, keepdims=True) + w_inter * qn
    floor = jnp.exp(-m_row)
    rinv = 1.0 / jnp.maximum(jnp.abs(den), floor)
    h = num * rinv
    b_last = jnp.sum(jnp.where(rowid == CHUNK - 1, b_col, 0.0), axis=0, keepdims=True)
    g_col = b_last - b_col + li_col
    m_new = jnp.maximum(b_last + m_st, jnp.max(g_col, axis=0, keepdims=True))
    w_k = jnp.exp(g_col - m_new)
    decay = jnp.exp(b_last + m_st - m_new)
    kw = w_k * k
    c_new = decay * c_st + _dot(kw.astype(BF16), vb, 0, 0)
    n_new = decay * n_st + jnp.sum(kw, axis=0, keepdims=True)
    return dict(h=h, c_new=c_new, n_new=n_new, m_new=m_new, w_intra=w_intra, w_inter=w_inter, s=s,
                qc=qc, qn=qn, den=den, floor=floor, rinv=rinv, w_k=w_k, decay=decay, kw=kw,
                qb=qb, kb=kb, vb=vb, cb=cb)


def _mlstm_fwd(p, puif, qk, lp):
    T = p.shape[0]
    nch = lp // CHUNK
    B = T // lp
    scale = M_DK ** -0.5

    def body(q_ref, k_ref, v_ref, if_ref, h_ref, cst_ref, nm_ref, *state):
        c_scs, nm_scs = state[:B * M_HEADS], state[B * M_HEADS:]
        ci = pl.program_id(0)

        @pl.when(ci == 0)
        def _():
            for ref in state:
                ref[...] = jnp.zeros_like(ref)

        rowid = lax.broadcasted_iota(jnp.int32, (CHUNK, 1), 0)
        padmask = (ci == 0) & (rowid < PAD_ROWS)
        for bb in range(B):
            li_all, b_all = _mlstm_gates(if_ref[bb], padmask)
            for hd in range(M_HEADS):
                c_sc, nm_sc = c_scs[bb * M_HEADS + hd], nm_scs[bb * M_HEADS + hd]
                c_st, n_st, m_row = c_sc[...], nm_sc[0:1, :], nm_sc[1:2, :]
                cst_ref[bb, 0, hd * M_DK:(hd + 1) * M_DK, :] = c_st
                nm_ref[bb, 0, hd:hd + 1, :] = n_st
                nm_ref[bb, 0, M_HEADS + hd:M_HEADS + hd + 1, :] = m_row
                q = q_ref[bb, :, hd * M_DK:(hd + 1) * M_DK]
                k = k_ref[bb, :, hd * M_DK:(hd + 1) * M_DK] * scale
                v = v_ref[bb, :, hd * M_DV:(hd + 1) * M_DV]
                o = _mlstm_head_fwd(q, k, v, _lane_col(li_all, hd), _lane_col(b_all, M_HEADS + hd),
                                    c_st, n_st, nm_sc[1:2, 0:1])
                h_ref[bb, :, hd * M_DV:(hd + 1) * M_DV] = o["h"].astype(BF16)
                c_sc[...] = o["c_new"]
                nm_sc[0:1, :] = o["n_new"]
                nm_sc[1:2, :] = jnp.broadcast_to(o["m_new"], (1, 128))

    qk3, p3, uif3 = qk.reshape(B, lp, 1024), p.reshape(B, lp, p.shape[1]), puif.reshape(B, lp, puif.shape[1])
    h, cst, nm = pl.pallas_call(
        body, name="mlstm_fwd", grid=(nch,),
        in_specs=[pl.BlockSpec((B, CHUNK, 512), lambda ci: (0, ci, 0)),
                  pl.BlockSpec((B, CHUNK, 512), lambda ci: (0, ci, 1)),
                  pl.BlockSpec((B, CHUNK, 1024), lambda ci: (0, ci, P_V // 1024)),
                  pl.BlockSpec((B, CHUNK, 128), lambda ci: (0, ci, S5_WIDTH // 128))],
        out_specs=(pl.BlockSpec((B, CHUNK, 1024), lambda ci: (0, ci, 0)),
                   pl.BlockSpec((B, 1, M_HEADS * M_DK, M_DV), lambda ci: (0, ci, 0, 0)),
                   pl.BlockSpec((B, 1, 8, 128), lambda ci: (0, ci, 0, 0))),
        out_shape=(jax.ShapeDtypeStruct((B, lp, 1024), BF16),
                   jax.ShapeDtypeStruct((B, nch, M_HEADS * M_DK, M_DV), F32),
                   jax.ShapeDtypeStruct((B, nch, 8, 128), F32)),
        scratch_shapes=([pltpu.VMEM((M_DK, M_DV), F32)] * (B * M_HEADS)
                        + [pltpu.VMEM((8, 128), F32)] * (B * M_HEADS)),
        compiler_params=_cparams(("arbitrary",)),
    )(qk3, qk3, p3, uif3)
    return h.reshape(T, 1024), cst, nm


def _mlstm_bwd(p, puif, qk, cst, nm, dh, dp, lp, comm=None):
    T = p.shape[0]
    nch = lp // CHUNK
    B = T // lp
    scale = M_DK ** -0.5

    def body(q_ref, k_ref, v_ref, if_ref, cst_ref, nm_ref, dh_ref, dpin_ref, dqk_ref, dv_ref, dif_ref, *state):
        t = pl.program_id(0)
        ci = nch - 1 - t

        @pl.when(t == 0)
        def _():
            for ref in state:
                ref[...] = jnp.zeros_like(ref)

        for bb in range(B):
            one_sequence(bb, ci, q_ref, k_ref, v_ref, if_ref, cst_ref, nm_ref, dh_ref, dqk_ref, dv_ref, dif_ref,
                         state[bb * M_HEADS:(bb + 1) * M_HEADS],
                         state[(B + bb) * M_HEADS:(B + bb + 1) * M_HEADS])

    def one_sequence(bb, ci, q_ref, k_ref, v_ref, if_ref, cst_ref, nm_ref, dh_ref, dqk_ref, dv_ref, dif_ref,
                     dc_scs, dn_scs):
        rowid = lax.broadcasted_iota(jnp.int32, (CHUNK, 1), 0)
        lane = lax.broadcasted_iota(jnp.int32, (CHUNK, 128), 1)
        padmask = (ci == 0) & (rowid < PAD_ROWS)
        ifv = if_ref[bb]
        li_all, b_all = _mlstm_gates(ifv, padmask)
        db_all = jnp.zeros((CHUNK, 128), F32)
        dli_all = jnp.zeros((CHUNK, 128), F32)
        for hd in range(M_HEADS):
            q = q_ref[bb, :, hd * M_DK:(hd + 1) * M_DK]
            k = k_ref[bb, :, hd * M_DK:(hd + 1) * M_DK] * scale
            v = v_ref[bb, :, hd * M_DV:(hd + 1) * M_DV]
            c_st = cst_ref[bb, 0, hd * M_DK:(hd + 1) * M_DK, :]
            n_st = nm_ref[bb, 0, hd:hd + 1, :]
            m_st = nm_ref[bb, 0, M_HEADS + hd:M_HEADS + hd + 1, 0:1]
            o = _mlstm_head_fwd(q, k, v, _lane_col(li_all, hd), _lane_col(b_all, M_HEADS + hd), c_st, n_st, m_st)
            dc_sc, dn_sc = dc_scs[hd], dn_scs[hd]
            dc_new = dc_sc[...]
            dn_new = dn_sc[0:1, :]
            dcb = dc_new.astype(BF16)
            dhh = dh_ref[bb, :, hd * M_DV:(hd + 1) * M_DV].astype(F32)
            dnum = dhh * o["rinv"]
            dhh_h = jnp.sum(dhh * o["h"], axis=1, keepdims=True)
            sgn = jnp.where(o["den"] >= 0.0, 1.0, -1.0)
            dden = jnp.where(jnp.abs(o["den"]) > o["floor"], -dhh_h * o["rinv"] * sgn, 0.0)
            dnb = dnum.astype(BF16)
            ds = _dot(dnb, o["vb"], 1, 1) + dden
            sb = o["s"].astype(BF16)
            kwb = o["kw"].astype(BF16)
            dv = _dot(sb, dnb, 0, 0) + _dot(kwb, dcb, 1, 0)
            dqk_m = (ds * o["w_intra"]).astype(BF16)
            wdn = o["w_inter"] * dnum
            wdd = o["w_inter"] * dden
            dq = _dot(dqk_m, o["kb"], 1, 0) + _dot(wdn.astype(BF16), o["cb"], 1, 1) + wdd * n_st
            vdc = _dot(o["vb"], dcb, 1, 1)
            dk = _dot(dqk_m, o["qb"], 0, 0) + o["w_k"] * (vdc + dn_new)
            dd = ds * o["s"]
            dd_col = _to_col(jnp.sum(dd, axis=0, keepdims=True))
            dmi = o["w_inter"] * (jnp.sum(dnum * o["qc"], axis=1, keepdims=True) + dden * o["qn"])
            dg = o["w_k"] * (jnp.sum(k * vdc, axis=1, keepdims=True) + jnp.sum(k * dn_new, axis=1, keepdims=True))
            d_blast = (o["decay"] * (jnp.sum(jnp.sum(dc_new * c_st, axis=1, keepdims=True), axis=0, keepdims=True)
                                     + jnp.sum(dn_new * n_st, axis=1, keepdims=True))
                       + jnp.sum(dg, axis=0, keepdims=True))
            db_col = jnp.sum(dd, axis=1, keepdims=True) - dd_col + dmi - dg
            db_col = db_col + jnp.where(rowid == CHUNK - 1, d_blast, 0.0)
            dli_col = dd_col + dg
            db_all = db_all + jnp.where(lane == M_HEADS + hd, db_col, 0.0)
            dli_all = dli_all + jnp.where(lane == hd, dli_col, 0.0)
            dc_sc[...] = o["decay"] * dc_new + _dot(o["qb"], wdn.astype(BF16), 0, 0)
            dn_sc[0:1, :] = o["decay"] * dn_new + jnp.sum(q * wdd, axis=0, keepdims=True)
            dqk_ref[bb, :, hd * M_DK:(hd + 1) * M_DK] = dq.astype(BF16)
            dqk_ref[bb, :, 512 + hd * M_DK:512 + (hd + 1) * M_DK] = (dk * scale).astype(BF16)
            dv_ref[bb, :, hd * M_DV:(hd + 1) * M_DV] = dv.astype(BF16)
        dlf = _tri_sum(db_all, True)
        dif = dli_all + dlf * _sigmoid(-ifv)
        dif_ref[bb] = jnp.where(padmask | (lane >= 2 * M_HEADS), 0.0, dif).astype(BF16)

    def rev(cb):
        return lambda t: (0, nch - 1 - t, cb)

    rev4 = lambda t: (0, nch - 1 - t, 0, 0)
    qk3, p3, dh3 = qk.reshape(B, lp, 1024), p.reshape(B, lp, p.shape[1]), dh.reshape(B, lp, 1024)
    uif3 = puif.reshape(B, lp, puif.shape[1])
    (dqk, dv, dif), comm_res = _ride_call(
        body, "mlstm_bwd", (nch,),
        [pl.BlockSpec((B, CHUNK, 512), rev(0)),
         pl.BlockSpec((B, CHUNK, 512), rev(1)),
         pl.BlockSpec((B, CHUNK, 1024), rev(P_V // 1024)),
         pl.BlockSpec((B, CHUNK, 128), rev(S5_WIDTH // 128)),
         pl.BlockSpec((B, 1, M_HEADS * M_DK, M_DV), rev4),
         pl.BlockSpec((B, 1, 8, 128), rev4),
         pl.BlockSpec((B, CHUNK, 1024), rev(0)),
         pl.BlockSpec(memory_space=pl.ANY)],
        [pl.BlockSpec((B, CHUNK, 1024), rev(0)),
         pl.BlockSpec((B, CHUNK, 1024), rev(P_V // 1024)),
         pl.BlockSpec((B, CHUNK, 128), rev(0))],
        [jax.ShapeDtypeStruct((B, lp, 1024), BF16),
         jax.ShapeDtypeStruct((B, lp, PW), BF16),
         jax.ShapeDtypeStruct((B, lp, 128), BF16)],
        [pltpu.VMEM((M_DK, M_DV), F32)] * (B * M_HEADS) + [pltpu.VMEM((8, 128), F32)] * (B * M_HEADS),
        (qk3, qk3, p3, uif3, cst, nm, dh3, dp.reshape(B, lp, PW)), comm=comm, aliases={7: 1})
    return dqk.reshape(T, 1024), dv.reshape(T, PW), dif.reshape(T, 128), comm_res


def _headnorm_fwd(hm, p, g):
    T = hm.shape[0]
    tr = _pick(T, (384, 320, 256, 128, 64))

    def body(h_ref, o_ref, g_ref, a_ref):
        for hd in range(M_HEADS):
            sl = slice(hd * M_DV, (hd + 1) * M_DV)
            hn = _ln_rows(h_ref[:, sl].astype(F32), g_ref[:, sl], 0.0)
            a_ref[:, sl] = (_sigmoid(o_ref[:, sl].astype(F32)) * hn).astype(BF16)

    return pl.pallas_call(
        body, name="headnorm_fwd", grid=(T // tr,),
        in_specs=[pl.BlockSpec((tr, 1024), lambda i: (i, 0)),
                  pl.BlockSpec((tr, 1024), lambda i: (i, P_O // 1024)),
                  pl.BlockSpec((1, 1024), lambda i: (0, 0))],
        out_specs=pl.BlockSpec((tr, 1024), lambda i: (i, 0)),
        out_shape=jax.ShapeDtypeStruct((T, 1024), BF16),
        compiler_params=_cparams(("arbitrary",)),
    )(hm, p, g)


def _headnorm_bwd(da, hm, p, g, dp):
    T = hm.shape[0]
    tr = _pick(T, (384, 320, 256, 128, 64))

    def body(da_ref, h_ref, o_ref, g_ref, dpin_ref, dh_ref, do_ref, dg_ref):
        @pl.when(pl.program_id(0) == 0)
        def _():
            dg_ref[...] = jnp.zeros_like(dg_ref)

        for hd in range(M_HEADS):
            sl = slice(hd * M_DV, (hd + 1) * M_DV)
            gg = g_ref[:, sl]
            so = _sigmoid(o_ref[:, sl].astype(F32))
            da = da_ref[:, sl].astype(F32)
            dhn = da * so
            dv, xhat = _ln_bwd_rows(dhn, h_ref[:, sl].astype(F32), gg)
            dh_ref[:, sl] = dv.astype(BF16)
            do_ref[:, sl] = (da * (xhat * gg) * so * (1.0 - so)).astype(BF16)
            dg_ref[0:1, sl] += jnp.sum(dhn * xhat, axis=0, keepdims=True)

    row = pl.BlockSpec((tr, 1024), lambda i: (i, 0))
    return pl.pallas_call(
        body, name="headnorm_bwd", grid=(T // tr,),
        in_specs=[row, row, pl.BlockSpec((tr, 1024), lambda i: (i, P_O // 1024)),
                  pl.BlockSpec((1, 1024), lambda i: (0, 0)), pl.BlockSpec(memory_space=pl.ANY)],
        out_specs=(row, pl.BlockSpec((tr, 1024), lambda i: (i, P_O // 1024)),
                   pl.BlockSpec((8, 1024), lambda i: (0, 0))),
        out_shape=(jax.ShapeDtypeStruct((T, 1024), BF16), jax.ShapeDtypeStruct((T, PW), BF16),
                   jax.ShapeDtypeStruct((8, 1024), F32)),
        input_output_aliases={4: 1},
        compiler_params=_cparams(("arbitrary",)),
    )(da, hm, p, g, dp)


def _mix_fwd(z, ym, p):
    T = ym.shape[0]
    tr = _pick(T, (384, 320, 256, 128, 64))

    def body(z1_ref, z2_ref, ym_ref, gs_ref, gm_ref, o_ref):
        ys = z1_ref[...].astype(F32) * _sigmoid(z2_ref[...].astype(F32))
        o_ref[...] = (_sigmoid(gs_ref[...].astype(F32)) * ys
                      + _sigmoid(gm_ref[...].astype(F32)) * ym_ref[...].astype(F32)).astype(BF16)

    def col(cb):
        return pl.BlockSpec((tr, 1024), lambda i: (i, cb))

    return pl.pallas_call(
        body, name="mix_fwd", grid=(T // tr,),
        in_specs=[col(0), col(1), col(0), col(P_GS // 1024), col(P_GM // 1024)],
        out_specs=col(0),
        out_shape=jax.ShapeDtypeStruct((T, 1024), BF16),
        compiler_params=_cparams(("arbitrary",)),
    )(z, z, ym, p, p)


def _mix_bwd(dmix, z, ym, p):
    T = ym.shape[0]
    tr = _pick(T, (384, 320, 256, 128, 64))

    def body(d_ref, z1_ref, z2_ref, ym_ref, gs_ref, gm_ref, dz_ref, dym_ref, dp_ref):
        d = d_ref[...].astype(F32)
        z1 = z1_ref[...].astype(F32)
        s2 = _sigmoid(z2_ref[...].astype(F32))
        ss = _sigmoid(gs_ref[...].astype(F32))
        sm = _sigmoid(gm_ref[...].astype(F32))
        ys = z1 * s2
        dys = d * ss
        dp_ref[:, 0:1024] = (d * ys * ss * (1.0 - ss)).astype(BF16)
        dp_ref[:, 1024:2048] = (d * ym_ref[...].astype(F32) * sm * (1.0 - sm)).astype(BF16)
        dym_ref[...] = (d * sm).astype(BF16)
        dz_ref[:, 0:1024] = (dys * s2).astype(BF16)
        dz_ref[:, 1024:2048] = (dys * z1 * s2 * (1.0 - s2)).astype(BF16)

    def col(cb):
        return pl.BlockSpec((tr, 1024), lambda i: (i, cb))

    o = jax.ShapeDtypeStruct((T, 1024), BF16)
    return pl.pallas_call(
        body, name="mix_bwd", grid=(T // tr,),
        in_specs=[col(0), col(0), col(1), col(0), col(P_GS // 1024), col(P_GM // 1024)],
        out_specs=(pl.BlockSpec((tr, 2048), lambda i: (i, 0)), col(0),
                   pl.BlockSpec((tr, 2048), lambda i: (i, P_GS // 2048))),
        out_shape=(jax.ShapeDtypeStruct((T, 2048), BF16), o, jax.ShapeDtypeStruct((T, PW), BF16)),
        compiler_params=_cparams(("arbitrary",)),
    )(dmix, z, z, ym, p, p)


def _adamw_math(w, g, m, v):
    m2 = ADAM_B1 * m + (1.0 - ADAM_B1) * g
    v2 = ADAM_B2 * v + (1.0 - ADAM_B2) * jnp.square(g)
    m_hat = m2 / (1.0 - ADAM_B1 ** ADAM_STEP)
    v_hat = v2 / (1.0 - ADAM_B2 ** ADAM_STEP)
    delta = -ADAM_LR * (m_hat / (jnp.sqrt(v_hat) + ADAM_EPS) + ADAM_WD * w)
    return delta, m2, v2


def _adamw_big(recv, w, m, v, name):
    R, C = w.shape
    tr, tc = _shard_tile(R, C)
    ns = recv.shape[0]

    def body(r_ref, w_ref, m_ref, v_ref, g_ref, d_ref, m2_ref, v2_ref):
        g = r_ref[0].astype(F32)
        for s in range(1, ns):
            g = g + r_ref[s].astype(F32)
        d, m2, v2 = _adamw_math(w_ref[...], g, m_ref[...], v_ref[...])
        g_ref[...] = g
        d_ref[...] = d
        m2_ref[...] = m2
        v2_ref[...] = v2

    blk = pl.BlockSpec((tr, tc), lambda i, j: (i, j))
    o = jax.ShapeDtypeStruct((R, C), F32)
    return pl.pallas_call(
        body, name=name, grid=(R // tr, C // tc),
        in_specs=[pl.BlockSpec((ns, tr, tc), lambda i, j: (0, i, j)), blk, blk, blk],
        out_specs=(blk,) * 4, out_shape=(o,) * 4,
        compiler_params=_cparams(("arbitrary", "arbitrary")),
    )(recv, w, m, v)


def _adamw_small(gs, ws, ms, vs):
    n = len(ws)
    shapes = [tuple(w.shape) for w in ws]

    def flat2(a):
        return a.reshape(1, -1) if a.ndim == 1 else a.reshape(-1, a.shape[-1])

    def body(*refs):
        g, w, m, v = refs[:n], refs[n:2 * n], refs[2 * n:3 * n], refs[3 * n:4 * n]
        d, m2, v2 = refs[4 * n:5 * n], refs[5 * n:6 * n], refs[6 * n:]
        for i in range(n):
            di, mi, vi = _adamw_math(w[i][...], g[i][...], m[i][...], v[i][...])
            d[i][...] = di
            m2[i][...] = mi
            v2[i][...] = vi

    args = [flat2(a) for a in list(gs) + list(ws) + list(ms) + list(vs)]
    vm = pl.BlockSpec(memory_space=pltpu.VMEM)
    outs = pl.pallas_call(
        body, name="adamw_small",
        in_specs=[vm] * (4 * n), out_specs=tuple([vm] * (3 * n)),
        out_shape=tuple(jax.ShapeDtypeStruct(a.shape, F32) for a in args[:n] * 3),
        compiler_params=pltpu.CompilerParams(vmem_limit_bytes=VMEM_LIMIT),
    )(*args)
    outs = [o.reshape(s) for o, s in zip(outs, shapes * 3)]
    return outs[:n], outs[n:2 * n], outs[2 * n:]


def _pack(arrs):
    parts = []
    for a in arrs:
        f = a.reshape(-1).astype(F32)
        n = -(-f.shape[0] // 1024) * 1024
        parts.append(jnp.pad(f, (0, n - f.shape[0])))
    return jnp.concatenate(parts).reshape(-1, 128)


def _unpack(pack, shapes):
    flat = pack.reshape(-1)
    out, off = [], 0
    for shp in shapes:
        n = math.prod(shp)
        out.append(flat[off:off + n].reshape(shp))
        off += -(-n // 1024) * 1024
    return out


def _cols_from_shards(g):
    return jnp.transpose(g, (1, 0, 2)).reshape(g.shape[1], -1)


def _cols_to_shards(w):
    R = w.shape[0]
    return jnp.transpose(w.reshape(R, N_DEV, -1), (1, 0, 2))


def _nat_to_aligned(w, axis):
    def sl(start, size):
        return lax.slice_in_dim(w, start, start + size, axis=axis)

    pads = [(0, 0)] * w.ndim
    pads[axis] = (0, PW - P_IF - 8)
    return jnp.concatenate([sl(N_V, 1024), sl(N_O, 1024), sl(N_GS, 1024), sl(N_GM, 1024), sl(N_Q, 1024),
                            sl(N_U, 512), jnp.pad(sl(N_I, 8), pads)], axis=axis)


def _aligned_to_nat(w, axis):
    def sl(start, size):
        return lax.slice_in_dim(w, start, start + size, axis=axis)

    return jnp.concatenate([sl(P_U, 512), sl(P_QK, 1024), sl(P_V, 1024), sl(P_O, 1024), sl(P_IF, 8),
                            sl(P_GS, 1024), sl(P_GM, 1024)], axis=axis)


def kernel(x, meta_tokens, ln0_g, ln0_b, w_in, b_in, qk_conv_w, qk_conv_b, s5_lambda_re, s5_lambda_im, s5_log_dt, s5_b_re, s5_b_im, s5_c_re, s5_c_im, s5_d, s5_w_glu, m_norm_g, m_w_out, w_o, ln1_g, ln1_b, w_up, b_up, w_down, ln2_g, ln2_b, loss_target, m_meta_tokens, m_ln0_g, m_ln0_b, m_w_in, m_b_in, m_qk_conv_w, m_qk_conv_b, m_s5_lambda_re, m_s5_lambda_im, m_s5_log_dt, m_s5_b_re, m_s5_b_im, m_s5_c_re, m_s5_c_im, m_s5_d, m_s5_w_glu, m_m_norm_g, m_m_w_out, m_w_o, m_ln1_g, m_ln1_b, m_w_up, m_b_up, m_w_down, m_ln2_g, m_ln2_b, v_meta_tokens, v_ln0_g, v_ln0_b, v_w_in, v_b_in, v_qk_conv_w, v_qk_conv_b, v_s5_lambda_re, v_s5_lambda_im, v_s5_log_dt, v_s5_b_re, v_s5_b_im, v_s5_c_re, v_s5_c_im, v_s5_d, v_s5_w_glu, v_m_norm_g, v_m_w_out, v_w_o, v_ln1_g, v_ln1_b, v_w_up, v_b_up, v_w_down, v_ln2_g, v_ln2_b):
    B, S, D = x.shape
    lp = S + CHUNK
    me = _my_id()

    ln0g, ln0b = ln0_g.reshape(1, D), ln0_b.reshape(1, D)
    (h0, h0b), first = _ln0_fwd(x, ln0g, ln0b, comm=(_Gather, [w_in[0].T.astype(BF16), meta_tokens, qk_conv_w[0]]))
    win_t = _nat_to_aligned(first[0].reshape(IN_NAT, D), 0)
    meta_f = _cols_from_shards(first[1])
    convw_f = _cols_from_shards(first[2])
    later = [a.astype(BF16) for a in (s5_w_glu[0], m_w_out[0], w_o[0])]
    latest = [a.astype(BF16) for a in (w_up[0], w_down[0])]
    b_in_al = _nat_to_aligned(b_in, 1)

    s5_args = (s5_lambda_re[0], s5_lambda_im[0], s5_log_dt[0], s5_b_re[0], s5_b_im[0], s5_c_re[0], s5_c_im[0])
    (ar, ai, bblk, cblk), s5_vjp = jax.vjp(_s5_prep, *s5_args)
    seg = _s5_tile(lp) // 8
    mu_r, mu_i = ar, ai
    for _ in range(seg - 1):
        mu_r, mu_i = mu_r * ar - mu_i * ai, mu_r * ai + mu_i * ar
    tab_f, tab_b = _scan_tables(mu_r, mu_i)
    lam8 = jnp.stack([jnp.broadcast_to(ar, (8, 2048)), jnp.broadcast_to(ai, (8, 2048))])

    h0, h0b = _ln0_head(meta_f, ln0g, ln0b, h0, h0b, lp)
    p, *gathered = _mm(h0b, win_t[:P_U], "nt", "mm_in", bias=b_in_al[:, :P_U], out_dtype=BF16,
                       comm=(_Gather, later))
    puif = _mm(h0b, win_t[P_U:], "nt", "mm_in_uif", bias=b_in_al[:, P_U:])
    wglu_f = _cols_from_shards(gathered[0])
    wmo_f = gathered[1].reshape(1024, 1024)
    wo_f = gathered[2].reshape(1024, 1024)
    (y_pre, gy, cin), gathered2 = _s5_fwd(puif, bblk, cblk, lam8, tab_f, s5_d, lp, comm=(_Gather, latest))
    wup_f = _cols_from_shards(gathered2[0])
    wdown_f = gathered2[1].reshape(D_FF, 1024)
    z = _mm(gy, wglu_f, "nn", "mm_glu", out_dtype=BF16)
    qk = _conv_fwd(p, convw_f, qk_conv_b, lp)
    hm, cst, nm = _mlstm_fwd(p, puif, qk, lp)
    a_m = _headnorm_fwd(hm, p, m_norm_g)
    ym = _mm(a_m, wmo_f, "nn", "mm_mout", out_dtype=BF16)
    mix = _mix_fwd(z, ym, p)
    r1 = _mm(mix, wo_f, "nn", "mm_o")
    pre1, h1, h1b = _ln_res_fwd(h0, r1, ln1_g, ln1_b, "ln1_fwd")
    act = _mm(h1b, wup_f, "nn", "mm_up", bias=b_up, relu2_out=True, out_dtype=BF16)
    ff = _mm(act, wdown_f, "nn", "mm_down")
    loss_acc, dpre2, dpre2b, dg2, db2 = _ln2_loss(h1, ff, ln2_g, ln2_b, loss_target)

    d_up = _mm(dpre2b, wdown_f, "nt", "mm_d_act", sqrt_gate_of=act, out_dtype=BF16)
    g_wdown = _mm(act, dpre2b, "tn", "mm_g_wdown", out_dtype=BF16)
    g_wup, cs_up = _mm(h1b, d_up, "tn", "mm_g_wup", colsum="b", out_dtype=BF16)
    dh1 = _mm(d_up, wup_f, "nt", "mm_d_h1", add=dpre2, add_scale=ALPHA)
    dpre1, dpre1b, dg1, db1 = _ln_bwd(dh1, pre1, ln1_g, "ln1_bwd")
    g_wo = _mm(mix, dpre1b, "tn", "mm_g_wo", out_dtype=BF16)
    dmix = _mm(dpre1b, wo_f, "nt", "mm_d_mix", out_dtype=BF16)
    dz, dym, dp = _mix_bwd(dmix, z, ym, p)
    g_wmo = _mm(a_m, dym, "tn", "mm_g_wmo", out_dtype=BF16)
    da = _mm(dym, wmo_f, "nt", "mm_d_a", out_dtype=BF16)
    dhm, dp, dg_norm = _headnorm_bwd(da, hm, p, m_norm_g, dp)
    early = [g_wdown.reshape(N_DEV, 512, 1024), _cols_to_shards(g_wup), g_wo.reshape(N_DEV, 128, 1024),
             g_wmo.reshape(N_DEV, 128, 1024)]
    dqk, dp, dif, recv_early = _mlstm_bwd(p, puif, qk, cst, nm, dhm, dp, lp, comm=(_AllToAll, early))
    dp, dconv_w, dconv_b = _conv_bwd(p, dqk, convw_f, qk_conv_b, dp, lp)
    g_wglu = _mm(gy, dz, "tn", "mm_g_wglu", out_dtype=BF16)
    dgy = _mm(dz, wglu_f, "nt", "mm_d_gy")
    (dp, dbblk, dcblk, dlam, dd), recv_glu = _s5_bwd(puif, y_pre, dgy, cin, bblk, cblk, lam8, tab_f, tab_b, s5_d, dif, dp,
                                                     lp, comm=(_AllToAll, [_cols_to_shards(g_wglu)]))
    g_win_t, cs_in = _mm(dp, h0b, "tn", "mm_g_win", colsum="a")
    g_win8 = _aligned_to_nat(g_win_t, 0).astype(BF16).reshape(N_DEV, IN_NAT // N_DEV, D)
    dh0, recv_win = _mm(dp, win_t, "nn", "mm_d_h0", add=dpre1, add_scale=ALPHA, comm=(_AllToAll, [g_win8]))
    grad_x, dmeta, dg0, db0 = _ln0_bwd(dh0, x, meta_f, ln0g)

    dlam2 = jnp.sum(dlam, axis=1)
    s5_grads = s5_vjp((dlam2[0:1], dlam2[1:2], dbblk, dcblk))
    small_local = [
        loss_acc[0:1, 0:1], dg0[0:1], db0[0:1], _aligned_to_nat(cs_in[0:1], 1), dconv_b[0:1],
        s5_grads[0], s5_grads[1], s5_grads[2], s5_grads[3], s5_grads[4], s5_grads[5], s5_grads[6],
        dd[0:1], dg_norm[0:1], dg1[0:1], db1[0:1], cs_up[0:1], dg2[0:1], db2[0:1],
        dmeta, dconv_w[0:4]]
    small_shapes = [(), (D,), (D,), (1, IN_NAT), (1, 1024),
                    (1, 32, 64), (1, 32, 64), (1, 32), (1, 32, 64, 16), (1, 32, 64, 16), (1, 32, 16, 64), (1, 32, 16, 64),
                    (1, 512), (1, 1024), (1, 1024), (1, 1024), (1, D_FF), (1, 1024), (1, 1024),
                    (N_META, D), (4, 1024)]
    red = _unpack(_allreduce_small(_pack(small_local)), small_shapes)
    loss = red[0]
    g_meta = lax.dynamic_slice_in_dim(red[19], me * 128, 128, axis=1)
    g_convw = lax.dynamic_slice_in_dim(red[20], me * 128, 128, axis=1)[None]
    small_g = red[1:19] + [g_meta, g_convw]
    small_w = [ln0_g, ln0_b, b_in, qk_conv_b, s5_lambda_re, s5_lambda_im, s5_log_dt, s5_b_re, s5_b_im,
               s5_c_re, s5_c_im, s5_d, m_norm_g, ln1_g, ln1_b, b_up, ln2_g, ln2_b, meta_tokens, qk_conv_w]
    small_m = [m_ln0_g, m_ln0_b, m_b_in, m_qk_conv_b, m_s5_lambda_re, m_s5_lambda_im, m_s5_log_dt, m_s5_b_re,
               m_s5_b_im, m_s5_c_re, m_s5_c_im, m_s5_d, m_m_norm_g, m_ln1_g, m_ln1_b, m_b_up, m_ln2_g, m_ln2_b,
               m_meta_tokens, m_qk_conv_w]
    small_v = [v_ln0_g, v_ln0_b, v_b_in, v_qk_conv_b, v_s5_lambda_re, v_s5_lambda_im, v_s5_log_dt, v_s5_b_re,
               v_s5_b_im, v_s5_c_re, v_s5_c_im, v_s5_d, v_m_norm_g, v_ln1_g, v_ln1_b, v_b_up, v_ln2_g, v_ln2_b,
               v_meta_tokens, v_qk_conv_w]
    shapes_w = [tuple(w.shape) for w in small_w]
    small_g = [g.reshape(s) for g, s in zip(small_g, shapes_w)]
    sd, sm2, sv2 = _adamw_small(small_g, small_w, small_m, small_v)

    names = ["w_in", "s5_w_glu", "m_w_out", "w_o", "w_up", "w_down"]
    recv = [recv_win, recv_glu[0], recv_early[3], recv_early[2], recv_early[1], recv_early[0]]
    big_m = [m_w_in[0].T, m_s5_w_glu[0], m_m_w_out[0], m_w_o[0], m_w_up[0], m_w_down[0]]
    big_v = [v_w_in[0].T, v_s5_w_glu[0], v_m_w_out[0], v_w_o[0], v_w_up[0], v_w_down[0]]
    big_w = [w_in[0].T, s5_w_glu[0], m_w_out[0], w_o[0], w_up[0], w_down[0]]
    big_out = [_adamw_big(r, w, m, v, "adamw_" + nm_) for r, w, m, v, nm_ in zip(recv, big_w, big_m, big_v, names)]
    big_out[0] = [o.T for o in big_out[0]]

    order = ["meta_tokens", "ln0_g", "ln0_b", "w_in", "b_in", "qk_conv_w", "qk_conv_b", "s5_lambda_re", "s5_lambda_im",
             "s5_log_dt", "s5_b_re", "s5_b_im", "s5_c_re", "s5_c_im", "s5_d", "s5_w_glu", "m_norm_g", "m_w_out", "w_o",
             "ln1_g", "ln1_b", "w_up", "b_up", "w_down", "ln2_g", "ln2_b"]
    small_names = ["ln0_g", "ln0_b", "b_in", "qk_conv_b", "s5_lambda_re", "s5_lambda_im", "s5_log_dt", "s5_b_re",
                   "s5_b_im", "s5_c_re", "s5_c_im", "s5_d", "m_norm_g", "ln1_g", "ln1_b", "b_up", "ln2_g", "ln2_b",
                   "meta_tokens", "qk_conv_w"]
    res = {}
    for i, n in enumerate(small_names):
        res[n] = (small_g[i], sd[i], sm2[i], sv2[i])
    for i, n in enumerate(names):
        res[n] = tuple(o[None] for o in big_out[i])
    outs = [loss, grad_x]
    for kind in range(4):
        outs += [res[n][kind] for n in order]
    return tuple(outs)
```

```python
import functools
import math

import jax
import jax.numpy as jnp
from jax import lax
from jax.experimental import pallas as pl
from jax.experimental.pallas import tpu as pltpu

F32 = jnp.float32
BF16 = jnp.bfloat16

D_MODEL = 1024
N_META = 16
CHUNK = 128
PAD_ROWS = CHUNK - N_META
S5_WIDTH = 512
S5_GROUP = 16
S5_GROUPS = 32
S5_STATE = 64
S5_COLS = 2 * S5_GROUPS * S5_STATE
S5_BLK = 4
M_HEADS = 4
M_DK = 128
M_DV = 256
D_FF = 4096
N_DEV = 8
ALPHA = 2.0 ** 0.25
LN_EPS = 1e-5
IN_NAT = 5640
P_V, P_O, P_GS, P_GM, P_QK, P_U, P_IF, PW = 0, 1024, 2048, 3072, 4096, 5120, 5632, 5760
N_U, N_Q, N_K, N_V, N_O, N_I, N_GS, N_GM = 0, 512, 1024, 1536, 2560, 3584, 3592, 4616

ADAM_LR, ADAM_B1, ADAM_B2, ADAM_EPS, ADAM_WD, ADAM_STEP = 0.001, 0.9, 0.999, 1e-08, 0.01, 10

VMEM_LIMIT = 56 * 1024 * 1024
MXU_WIDTH = 256
MESH = pl.DeviceIdType.MESH


def _pick(n, cands):
    for c in cands:
        if n % c == 0:
            return c
    raise ValueError(f"no tile for {n} among {cands}")


def _cparams(sem):
    return pltpu.CompilerParams(dimension_semantics=sem, vmem_limit_bytes=VMEM_LIMIT)


def _dot(a, b, ca, cb):
    return lax.dot_general(a, b, (((ca,), (cb,)), ((), ())), preferred_element_type=F32)


def _sigmoid(x):
    return 1.0 / (1.0 + jnp.exp(-x))


def _peer(k):
    x, y, c = lax.axis_index("x"), lax.axis_index("y"), lax.axis_index("c")
    px = 1 - x if k & 4 else x
    py = 1 - y if k & 2 else y
    pc = 1 - c if k & 1 else c
    return (px, py, pc), 4 * px + 2 * py + pc


def _my_id():
    return 4 * lax.axis_index("x") + 2 * lax.axis_index("y") + lax.axis_index("c")


def _hbm_call(body, name, arrs, out_shape, n_remote):
    n = len(arrs)
    return pl.pallas_call(
        body, name=name,
        out_shape=tuple(out_shape),
        in_specs=[pl.BlockSpec(memory_space=pl.ANY)] * n,
        out_specs=tuple([pl.BlockSpec(memory_space=pl.ANY)] * len(out_shape)),
        scratch_shapes=[pltpu.SemaphoreType.DMA((n, n_remote)),
                        pltpu.SemaphoreType.DMA((n, n_remote)),
                        pltpu.SemaphoreType.DMA((n,))],
    )(*arrs)


class _Gather:
    def __init__(self, ins, outs, send_sems, recv_sems, local_sems):
        self.ins, self.outs, self.n = ins, outs, len(ins)
        self.send_sems, self.recv_sems, self.local_sems = send_sems, recv_sems, local_sems
        x, y, c = lax.axis_index("x"), lax.axis_index("y"), lax.axis_index("c")
        self.c = c
        self.me, self.sibling = (x, y, c), (x, y, 1 - c)
        self.chips = [(1 - x, y), (x, 1 - y), (1 - x, 1 - y)]

    def slot(self, a, dev):
        return self.outs[a].at[4 * dev[0] + 2 * dev[1] + dev[2]]

    def copy(self, a, k, block, to, own=False):
        return pltpu.make_async_remote_copy(
            src_ref=self.ins[a] if own else self.slot(a, block), dst_ref=self.slot(a, block),
            send_sem=self.send_sems.at[a, k], recv_sem=self.recv_sems.at[a, k],
            device_id=to, device_id_type=MESH)

    def first_sends(self, a):
        return [self.copy(a, 0, self.me, self.sibling, own=True)] + [
            self.copy(a, 1 + j, self.me, (*chip, self.c), own=True) for j, chip in enumerate(self.chips)]

    def start(self):
        for a in range(self.n):
            pltpu.make_async_copy(self.ins[a], self.slot(a, self.me), self.local_sems.at[a]).start()
            for cp in self.first_sends(a):
                cp.start()

    def finish(self):
        forwards = []
        for j, chip in enumerate(self.chips):
            for a in range(self.n):
                self.copy(a, 1 + j, (*chip, self.c), self.me).wait_recv()
                fwd = self.copy(a, 4 + j, (*chip, self.c), self.sibling)
                fwd.start()
                forwards.append(fwd)
        for a in range(self.n):
            self.copy(a, 0, self.sibling, self.me).wait_recv()
            for j, chip in enumerate(self.chips):
                self.copy(a, 4 + j, (*chip, 1 - self.c), self.me).wait_recv()
        for a in range(self.n):
            for cp in self.first_sends(a):
                cp.wait_send()
            pltpu.make_async_copy(self.ins[a], self.slot(a, self.me), self.local_sems.at[a]).wait()
        for cp in forwards:
            cp.wait_send()

    @staticmethod
    def out_shapes(arrs):
        return [jax.ShapeDtypeStruct((N_DEV,) + tuple(a.shape), a.dtype) for a in arrs]


class _AllToAll:
    def __init__(self, ins, outs, send_sems, recv_sems, local_sems):
        self.ins, self.outs, self.n = ins, outs, len(ins)
        self.send_sems, self.recv_sems, self.local_sems = send_sems, recv_sems, local_sems
        self.me = _my_id()

    def copy(self, a, k, landing):
        peer, pid = _peer(k)
        return pltpu.make_async_remote_copy(
            src_ref=self.ins[a].at[pid], dst_ref=self.outs[a].at[pid if landing else self.me],
            send_sem=self.send_sems.at[a, k - 1], recv_sem=self.recv_sems.at[a, k - 1],
            device_id=peer, device_id_type=MESH)

    def local(self, a):
        return pltpu.make_async_copy(self.ins[a].at[self.me], self.outs[a].at[self.me], self.local_sems.at[a])

    def start(self):
        for a in range(self.n):
            self.local(a).start()
            for k in range(1, N_DEV):
                self.copy(a, k, False).start()

    def finish(self):
        for a in range(self.n):
            for k in range(1, N_DEV):
                self.copy(a, k, True).wait_recv()
        for a in range(self.n):
            for k in range(1, N_DEV):
                self.copy(a, k, False).wait_send()
            self.local(a).wait()

    @staticmethod
    def out_shapes(arrs):
        return [jax.ShapeDtypeStruct(tuple(a.shape), a.dtype) for a in arrs]


def _comm_scratch(n):
    return [pltpu.SemaphoreType.DMA((n, N_DEV - 1)), pltpu.SemaphoreType.DMA((n, N_DEV - 1)),
            pltpu.SemaphoreType.DMA((n,))]


def _ride_call(body, name, grid, in_specs, out_specs, out_shape, scratch, args, comm=None, aliases=None):
    n_in, n_out = len(in_specs), len(out_specs)
    in_specs, out_specs, out_shape = list(in_specs), list(out_specs), list(out_shape)
    scratch, args = list(scratch), list(args)
    kernel_fn = body
    if comm is not None:
        cls, arrs = comm
        n = len(arrs)

        def kernel_fn(*refs):
            ins, cin = refs[:n_in], refs[n_in:n_in + n]
            outs = refs[n_in + n:n_in + n + n_out]
            cout = refs[n_in + n + n_out:n_in + 2 * n + n_out]
            own_scratch, sems = refs[n_in + 2 * n + n_out:-3], refs[-3:]
            exchange = cls(cin, cout, *sems)
            ids = [pl.program_id(d) for d in range(len(grid))]
            first = functools.reduce(lambda a, b: a & b, [i == 0 for i in ids])
            last = functools.reduce(lambda a, b: a & b, [i == g - 1 for i, g in zip(ids, grid)])

            @pl.when(first)
            def _():
                exchange.start()

            body(*ins, *outs, *own_scratch)

            @pl.when(last)
            def _():
                exchange.finish()

        in_specs += [pl.BlockSpec(memory_space=pl.ANY)] * n
        args += list(arrs)
        out_specs += [pl.BlockSpec(memory_space=pl.ANY)] * n
        out_shape += cls.out_shapes(arrs)
        scratch += _comm_scratch(n)
    res = pl.pallas_call(
        kernel_fn, name=name, grid=grid,
        in_specs=in_specs, out_specs=tuple(out_specs), out_shape=tuple(out_shape),
        scratch_shapes=scratch, input_output_aliases=aliases or {},
        compiler_params=_cparams(("arbitrary",) * len(grid)),
    )(*args)
    return list(res[:n_out]), list(res[n_out:])


def _shard_tile(R, C):
    if R % 128 == 0:
        return 128, C
    return R, _pick(C, (256, 128))


def _allreduce_small(pack):
    rows = pack.shape[0]

    def gather(*refs):
        g = _Gather(refs[:1], refs[1:2], *refs[2:])
        g.start()
        g.finish()

    slots = _hbm_call(gather, "allreduce_gather", [pack], _Gather.out_shapes([pack]), N_DEV - 1)[0]

    def body(s_ref, o_ref):
        acc = s_ref[0]
        for s in range(1, N_DEV):
            acc = acc + s_ref[s]
        o_ref[...] = acc

    return pl.pallas_call(
        body, name="allreduce_sum", grid=(1,),
        in_specs=[pl.BlockSpec((N_DEV, rows, 128), lambda i: (0, 0, 0))],
        out_specs=pl.BlockSpec((rows, 128), lambda i: (0, 0)),
        out_shape=jax.ShapeDtypeStruct((rows, 128), F32),
        compiler_params=_cparams(("arbitrary",)),
    )(slots)


def _mm_tiles(M, N, K, mode, out_bytes, n_extra_f32, a_bytes, b_bytes):
    budget = 40 * 1024 * 1024
    tms = [t for t in (1664, 1408, 1152, 1024, 640, 512, 256, 128) if M % t == 0]
    tns = [t for t in (1152, 1024, 640, 512, 384, 256, 128) if N % t == 0]
    if mode == "tn":
        tks = [t for t in (1664, 1408, 640, 512, 256, 128) if K % t == 0]
    else:
        tks = [K] if K <= 1152 else [t for t in (1152, 1024, 640, 512) if K % t == 0]
    best = None
    for tm in tms:
        for tn in tns:
            for tk in tks:
                nk = K // tk
                use = 2 * (tm * tk * a_bytes + tk * tn * b_bytes) + 2 * tm * tn * out_bytes
                use += 2 * n_extra_f32 * tm * tn * 4 + tm * tn * 4 * (2 if nk > 1 else 1)
                if use > budget:
                    continue
                score = (tm * tn * tk, tm * tn)
                if best is None or score > best[0]:
                    best = (score, (tm, tn, tk))
    assert best is not None, (M, N, K, mode)
    return best[1]


def _mm(a, b, mode, name, *, bias=None, add=None, add_scale=1.0, sqrt_gate_of=None,
        relu2_out=False, colsum=None, out_dtype=F32, comm=None, b_rows=None):
    if mode == "nn":
        (M, K), (K2, N) = a.shape, b.shape
    elif mode == "nt":
        (M, K), (N, K2) = a.shape, b.shape
        if b_rows is not None:
            N = b_rows[1]
    else:
        (K, M), (K2, N) = a.shape, b.shape
    assert K == K2, (a.shape, b.shape, mode)
    assert b_rows is None or mode == "nt"
    has_bias, has_add, has_gate = bias is not None, add is not None, sqrt_gate_of is not None
    tm, tn, tk = _mm_tiles(M, N, K, mode, jnp.dtype(out_dtype).itemsize, int(has_add) + int(has_gate),
                           a.dtype.itemsize, b.dtype.itemsize)
    nk = K // tk
    assert colsum is None or mode == "tn"
    assert colsum != "a" or N == tn
    comm_cls, comm_arrs = comm if comm is not None else (None, [])
    nc = len(comm_arrs)
    grid = (N // tn, M // tm, nk)

    def body(*refs):
        it = iter(refs)
        a_ref, b_ref = next(it), next(it)
        bias_ref = next(it) if has_bias else None
        add_ref = next(it) if has_add else None
        gate_ref = next(it) if has_gate else None
        comm_ins = [next(it) for _ in range(nc)]
        o_ref = next(it)
        cs_ref = next(it) if colsum else None
        comm_outs = [next(it) for _ in range(nc)]
        acc_ref = next(it) if nk > 1 else None
        j, i, k = pl.program_id(0), pl.program_id(1), pl.program_id(2)
        if nc:
            exchange = comm_cls(comm_ins, comm_outs, next(it), next(it), next(it))

            @pl.when((j == 0) & (i == 0) & (k == 0))
            def _():
                exchange.start()

        blocks = [slice(n0, min(n0 + MXU_WIDTH, tn)) for n0 in range(0, tn, MXU_WIDTH)]
        av = a_ref[...].astype(BF16)

        def part(cols):
            if mode == "nn":
                return _dot(av, b_ref[:, cols].astype(BF16), 1, 0)
            if mode == "nt":
                return _dot(av, b_ref[cols, :].astype(BF16), 1, 1)
            return _dot(av, b_ref[:, cols].astype(BF16), 0, 0)

        if colsum == "b":
            @pl.when((i == 0) & (k == 0))
            def _():
                cs_ref[...] = jnp.zeros_like(cs_ref)

            @pl.when(i == 0)
            def _():
                cs_ref[0:1, :] += jnp.sum(b_ref[...].astype(F32), axis=0, keepdims=True)
        if colsum == "a":
            @pl.when(k == 0)
            def _():
                cs_ref[...] = jnp.zeros_like(cs_ref)

            cs_ref[0:1, :] += jnp.sum(a_ref[...].astype(F32), axis=0, keepdims=True)

        def finish(cols, r):
            if has_bias:
                r = r + bias_ref[:, cols]
            if has_add:
                r = r + add_scale * add_ref[:, cols]
            if has_gate:
                r = r * (2.0 * jnp.sqrt(gate_ref[:, cols].astype(F32)))
            if relu2_out:
                r = jnp.square(jnp.maximum(r, 0.0))
            o_ref[:, cols] = r.astype(out_dtype)

        if nk == 1:
            for cols in blocks:
                finish(cols, part(cols))
        else:
            @pl.when(k == 0)
            def _():
                for cols in blocks:
                    acc_ref[:, cols] = part(cols)

            @pl.when((k > 0) & (k < nk - 1))
            def _():
                for cols in blocks:
                    acc_ref[:, cols] += part(cols)

            @pl.when(k == nk - 1)
            def _():
                for cols in blocks:
                    finish(cols, acc_ref[:, cols] + part(cols))

        if nc:
            @pl.when((j == grid[0] - 1) & (i == grid[1] - 1) & (k == nk - 1))
            def _():
                exchange.finish()

    if mode == "nn":
        a_spec = pl.BlockSpec((tm, tk), lambda j, i, k: (i, k))
        b_spec = pl.BlockSpec((tk, tn), lambda j, i, k: (k, j))
    elif mode == "nt":
        a_spec = pl.BlockSpec((tm, tk), lambda j, i, k: (i, k))
        b_off = 0 if b_rows is None else b_rows[0] // tn
        assert b_rows is None or b_rows[0] % tn == 0
        b_spec = pl.BlockSpec((tn, tk), lambda j, i, k: (j + b_off, k))
    else:
        a_spec = pl.BlockSpec((tk, tm), lambda j, i, k: (k, i))
        b_spec = pl.BlockSpec((tk, tn), lambda j, i, k: (k, j))
    in_specs, args = [a_spec, b_spec], [a, b]
    if has_bias:
        in_specs.append(pl.BlockSpec((1, tn), lambda j, i, k: (0, j)))
        args.append(bias)
    if has_add:
        in_specs.append(pl.BlockSpec((tm, tn), lambda j, i, k: (i, j)))
        args.append(add)
    if has_gate:
        in_specs.append(pl.BlockSpec((tm, tn), lambda j, i, k: (i, j)))
        args.append(sqrt_gate_of)
    out_shape = [jax.ShapeDtypeStruct((M, N), out_dtype)]
    out_specs = [pl.BlockSpec((tm, tn), lambda j, i, k: (i, j))]
    if colsum == "b":
        out_shape.append(jax.ShapeDtypeStruct((8, N), F32))
        out_specs.append(pl.BlockSpec((8, tn), lambda j, i, k: (0, j)))
    if colsum == "a":
        out_shape.append(jax.ShapeDtypeStruct((8, M), F32))
        out_specs.append(pl.BlockSpec((8, tm), lambda j, i, k: (0, i)))
    scratch = [pltpu.VMEM((tm, tn), F32)] if nk > 1 else []
    if nc:
        in_specs += [pl.BlockSpec(memory_space=pl.ANY)] * nc
        args += list(comm_arrs)
        out_specs += [pl.BlockSpec(memory_space=pl.ANY)] * nc
        out_shape += comm_cls.out_shapes(comm_arrs)
        scratch += _comm_scratch(nc)
    res = pl.pallas_call(
        body, name=name, grid=grid,
        in_specs=in_specs, out_specs=tuple(out_specs), out_shape=tuple(out_shape),
        scratch_shapes=scratch,
        compiler_params=_cparams(("arbitrary", "arbitrary", "arbitrary")),
    )(*args)
    return res if len(res) > 1 else res[0]


def _ln_rows(v, g, b):
    mu = jnp.mean(v, axis=-1, keepdims=True)
    xc = v - mu
    var = jnp.mean(xc * xc, axis=-1, keepdims=True)
    return xc * lax.rsqrt(var + LN_EPS) * g + b


def _ln_bwd_rows(dy, v, g):
    mu = jnp.mean(v, axis=-1, keepdims=True)
    xc = v - mu
    var = jnp.mean(xc * xc, axis=-1, keepdims=True)
    rstd = lax.rsqrt(var + LN_EPS)
    xhat = xc * rstd
    dxh = dy * g
    dv = rstd * (dxh - jnp.mean(dxh, axis=-1, keepdims=True)
                 - xhat * jnp.mean(dxh * xhat, axis=-1, keepdims=True))
    return dv, xhat


def _real_tile(S):
    return _pick(S, (512, 256, 128, 64))


def _real_rows(rb, ncols, lp):
    return pl.BlockSpec((pl.Element(rb), pl.Element(ncols)),
                        lambda bb, j: (pl.multiple_of(bb * lp + CHUNK + j * rb, CHUNK), 0))


def _head_rows(ncols, lp):
    return pl.BlockSpec((CHUNK, ncols), lambda bb: (bb * (lp // CHUNK), 0))


def _ln0_fwd(x, g, b, comm=None):
    B, S, D = x.shape
    lp = S + CHUNK
    rb = _real_tile(S)

    def body(x_ref, g_ref, b_ref, h_ref, hb_ref):
        y = _ln_rows(x_ref[0], g_ref[...], b_ref[...])
        h_ref[...] = y
        hb_ref[...] = y.astype(BF16)

    vec = pl.BlockSpec((1, D), lambda bb, j: (0, 0))
    return _ride_call(
        body, "ln0_fwd", (B, S // rb),
        [pl.BlockSpec((1, rb, D), lambda bb, j: (bb, j, 0)), vec, vec],
        [_real_rows(rb, D, lp), _real_rows(rb, D, lp)],
        [jax.ShapeDtypeStruct((B * lp, D), F32), jax.ShapeDtypeStruct((B * lp, D), BF16)],
        [], (x, g, b), comm=comm)


def _ln0_head(meta, g, b, h, hb, lp):
    D = meta.shape[1]
    B = h.shape[0] // lp

    def head(meta_ref, g_ref, b_ref, hin_ref, hbin_ref, h_ref, hb_ref):
        m = _ln_rows(meta_ref[...], g_ref[...], b_ref[...])
        h_ref[0:PAD_ROWS, :] = jnp.zeros((PAD_ROWS, D), F32)
        h_ref[PAD_ROWS:CHUNK, :] = m
        hb_ref[0:PAD_ROWS, :] = jnp.zeros((PAD_ROWS, D), BF16)
        hb_ref[PAD_ROWS:CHUNK, :] = m.astype(BF16)

    vec1 = pl.BlockSpec((1, D), lambda bb: (0, 0))
    anyspec = pl.BlockSpec(memory_space=pl.ANY)
    return pl.pallas_call(
        head, name="ln0_head", grid=(B,),
        in_specs=[pl.BlockSpec((N_META, D), lambda bb: (0, 0)), vec1, vec1, anyspec, anyspec],
        out_specs=(_head_rows(D, lp), _head_rows(D, lp)),
        out_shape=(jax.ShapeDtypeStruct((B * lp, D), F32), jax.ShapeDtypeStruct((B * lp, D), BF16)),
        input_output_aliases={3: 0, 4: 1},
        compiler_params=_cparams(("arbitrary",)),
    )(meta, g, b, h, hb)


def _ln0_bwd(dh0, x, meta, g):
    B, S, D = x.shape
    lp = S + CHUNK
    rb = _real_tile(S)

    def body(dh_ref, x_ref, g_ref, dx_ref, dg_ref, db_ref):
        @pl.when((pl.program_id(0) == 0) & (pl.program_id(1) == 0))
        def _():
            dg_ref[...] = jnp.zeros_like(dg_ref)
            db_ref[...] = jnp.zeros_like(db_ref)

        dy = dh_ref[...]
        dv, xhat = _ln_bwd_rows(dy, x_ref[0], g_ref[...])
        dx_ref[0] = dv
        dg_ref[0:1, :] += jnp.sum(dy * xhat, axis=0, keepdims=True)
        db_ref[0:1, :] += jnp.sum(dy, axis=0, keepdims=True)

    const = lambda bb, j: (0, 0)
    xblk = pl.BlockSpec((1, rb, D), lambda bb, j: (bb, j, 0))
    acc_shape = jax.ShapeDtypeStruct((8, D), F32)
    dx, dg, db = pl.pallas_call(
        body, name="ln0_bwd", grid=(B, S // rb),
        in_specs=[_real_rows(rb, D, lp), xblk, pl.BlockSpec((1, D), const)],
        out_specs=(xblk, pl.BlockSpec((8, D), const), pl.BlockSpec((8, D), const)),
        out_shape=(jax.ShapeDtypeStruct((B, S, D), F32), acc_shape, acc_shape),
        compiler_params=_cparams(("arbitrary", "arbitrary")),
    )(dh0, x, g)

    def head(dh_ref, meta_ref, g_ref, dmeta_ref, dg_ref, db_ref):
        @pl.when(pl.program_id(0) == 0)
        def _():
            dmeta_ref[...] = jnp.zeros_like(dmeta_ref)
            dg_ref[...] = jnp.zeros_like(dg_ref)
            db_ref[...] = jnp.zeros_like(db_ref)

        dy = dh_ref[PAD_ROWS:CHUNK, :]
        dv, xhat = _ln_bwd_rows(dy, meta_ref[...], g_ref[...])
        dmeta_ref[...] += dv
        dg_ref[0:1, :] += jnp.sum(dy * xhat, axis=0, keepdims=True)
        db_ref[0:1, :] += jnp.sum(dy, axis=0, keepdims=True)

    c1 = lambda bb: (0, 0)
    dmeta, dgm, dbm = pl.pallas_call(
        head, name="ln0_bwd_head", grid=(B,),
        in_specs=[_head_rows(D, lp), pl.BlockSpec((N_META, D), c1), pl.BlockSpec((1, D), c1)],
        out_specs=(pl.BlockSpec((N_META, D), c1), pl.BlockSpec((8, D), c1), pl.BlockSpec((8, D), c1)),
        out_shape=(jax.ShapeDtypeStruct((N_META, D), F32), acc_shape, acc_shape),
        compiler_params=_cparams(("arbitrary",)),
    )(dh0, meta, g)
    return dx, dmeta, dg + dgm, db + dbm


def _ln_res_fwd(h_prev, r, g, b, name):
    T, D = h_prev.shape
    tr = _pick(T, (384, 320, 256, 128, 64))

    def body(hp_ref, r_ref, g_ref, b_ref, pre_ref, h_ref, hb_ref):
        pre = ALPHA * hp_ref[...] + r_ref[...]
        y = _ln_rows(pre, g_ref[...], b_ref[...])
        pre_ref[...] = pre
        h_ref[...] = y
        hb_ref[...] = y.astype(BF16)

    row = pl.BlockSpec((tr, D), lambda i: (i, 0))
    vec = pl.BlockSpec((1, D), lambda i: (0, 0))
    return pl.pallas_call(
        body, name=name, grid=(T // tr,),
        in_specs=[row, row, vec, vec], out_specs=(row, row, row),
        out_shape=(jax.ShapeDtypeStruct((T, D), F32), jax.ShapeDtypeStruct((T, D), F32),
                   jax.ShapeDtypeStruct((T, D), BF16)),
        compiler_params=_cparams(("arbitrary",)),
    )(h_prev, r, g, b)


def _ln_bwd(dh, pre, g, name):
    T, D = dh.shape
    tr = _pick(T, (384, 320, 256, 128, 64))

    def body(dh_ref, pre_ref, g_ref, dp_ref, dpb_ref, dg_ref, db_ref):
        @pl.when(pl.program_id(0) == 0)
        def _():
            dg_ref[...] = jnp.zeros_like(dg_ref)
            db_ref[...] = jnp.zeros_like(db_ref)

        dy = dh_ref[...]
        dv, xhat = _ln_bwd_rows(dy, pre_ref[...], g_ref[...])
        dp_ref[...] = dv
        dpb_ref[...] = dv.astype(BF16)
        dg_ref[0:1, :] += jnp.sum(dy * xhat, axis=0, keepdims=True)
        db_ref[0:1, :] += jnp.sum(dy, axis=0, keepdims=True)

    row = pl.BlockSpec((tr, D), lambda i: (i, 0))
    vec = pl.BlockSpec((1, D), lambda i: (0, 0))
    acc = pl.BlockSpec((8, D), lambda i: (0, 0))
    return pl.pallas_call(
        body, name=name, grid=(T // tr,),
        in_specs=[row, row, vec], out_specs=(row, row, acc, acc),
        out_shape=(jax.ShapeDtypeStruct((T, D), F32), jax.ShapeDtypeStruct((T, D), BF16),
                   jax.ShapeDtypeStruct((8, D), F32), jax.ShapeDtypeStruct((8, D), F32)),
        compiler_params=_cparams(("arbitrary",)),
    )(dh, pre, g)


def _ln2_loss(h1, ff, g, b, target):
    T, D = h1.shape
    B, S, _ = target.shape
    lp = S + CHUNK
    rb = _real_tile(S)

    def body(h_ref, ff_ref, g_ref, b_ref, t_ref, loss_ref, dp_ref, dpb_ref, dg_ref, db_ref):
        @pl.when((pl.program_id(0) == 0) & (pl.program_id(1) == 0))
        def _():
            loss_ref[...] = jnp.zeros_like(loss_ref)
            dg_ref[...] = jnp.zeros_like(dg_ref)
            db_ref[...] = jnp.zeros_like(db_ref)

        pre = ALPHA * h_ref[...] + ff_ref[...]
        gg = g_ref[...]
        y = _ln_rows(pre, gg, b_ref[...])
        err = y - t_ref[0]
        loss_ref[0:1, 0:1] += 0.5 * jnp.sum(jnp.mean(err * err, axis=-1, keepdims=True), axis=0, keepdims=True)
        dy = err * (1.0 / D)
        dv, xhat = _ln_bwd_rows(dy, pre, gg)
        dp_ref[...] = dv
        dpb_ref[...] = dv.astype(BF16)
        dg_ref[0:1, :] += jnp.sum(dy * xhat, axis=0, keepdims=True)
        db_ref[0:1, :] += jnp.sum(dy, axis=0, keepdims=True)

    row = _real_rows(rb, D, lp)
    const = lambda bb, j: (0, 0)
    loss, dp, dpb, dg, db = pl.pallas_call(
        body, name="ln2_loss", grid=(B, S // rb),
        in_specs=[row, row, pl.BlockSpec((1, D), const), pl.BlockSpec((1, D), const),
                  pl.BlockSpec((1, rb, D), lambda bb, j: (bb, j, 0))],
        out_specs=(pl.BlockSpec((8, 128), const), row, row,
                   pl.BlockSpec((8, D), const), pl.BlockSpec((8, D), const)),
        out_shape=(jax.ShapeDtypeStruct((8, 128), F32),
                   jax.ShapeDtypeStruct((T, D), F32), jax.ShapeDtypeStruct((T, D), BF16),
                   jax.ShapeDtypeStruct((8, D), F32), jax.ShapeDtypeStruct((8, D), F32)),
        compiler_params=_cparams(("arbitrary", "arbitrary")),
    )(h1, ff, g, b, target)

    def head(dpin_ref, dpbin_ref, dp_ref, dpb_ref):
        dp_ref[...] = jnp.zeros((CHUNK, D), F32)
        dpb_ref[...] = jnp.zeros((CHUNK, D), BF16)

    anyspec = pl.BlockSpec(memory_space=pl.ANY)
    dp, dpb = pl.pallas_call(
        head, name="ln2_head", grid=(B,),
        in_specs=[anyspec, anyspec],
        out_specs=(_head_rows(D, lp), _head_rows(D, lp)),
        out_shape=(jax.ShapeDtypeStruct((T, D), F32), jax.ShapeDtypeStruct((T, D), BF16)),
        input_output_aliases={0: 0, 1: 1},
        compiler_params=_cparams(("arbitrary",)),
    )(dp, dpb)
    return loss, dp, dpb, dg, db


def _s5_prep(lam_re, lam_im, log_dt, b_re, b_im, c_re, c_im):
    dt = jnp.exp(log_dt)[:, None]
    mag = jnp.exp(lam_re * dt)
    ar = mag * jnp.cos(lam_im * dt)
    ai = mag * jnp.sin(lam_im * dt)
    nr, ni = ar - 1.0, ai
    den = lam_re * lam_re + lam_im * lam_im
    cr = (nr * lam_re + ni * lam_im) / den
    ci = (ni * lam_re - nr * lam_im) / den
    bbr = cr[..., None] * b_re - ci[..., None] * b_im
    bbi = cr[..., None] * b_im + ci[..., None] * b_re
    eye = jnp.eye(8, dtype=F32)
    bb = jnp.stack([bbr, bbi]).reshape(2, S5_BLK, 8, S5_STATE, S5_GROUP)
    bblk = jnp.einsum("rbgph,gj->bghrjp", bb, eye).reshape(S5_BLK, 128, 1024)
    cc = jnp.stack([c_re, -c_im]).reshape(2, S5_BLK, 8, S5_GROUP, S5_STATE)
    cblk = jnp.einsum("rbghp,gj->brjpgh", cc, eye).reshape(S5_BLK, 1024, 128)
    return ar.reshape(1, 2048), ai.reshape(1, 2048), bblk, cblk


def _scan_tables(ar, ai):
    pr, pi = [ar], [ai]
    for _ in range(7):
        pr, pi = pr + [pr[-1] * ar - pi[-1] * ai], pi + [pr[-1] * ai + pi[-1] * ar]
    pw_r = jnp.concatenate(pr, axis=0)
    pw_i = jnp.concatenate(pi, axis=0)
    rev_r = jnp.concatenate(pr[::-1], axis=0)
    rev_i = jnp.concatenate(pi[::-1], axis=0)
    row = jnp.arange(8)[:, None]

    def tables(sign, reverse):
        rows = []
        for n, sh in ((0, 1), (1, 2), (3, 4)):
            mask = (row < 8 - sh) if reverse else (row >= sh)
            rows.append(jnp.where(mask, pw_r[n][None, :], 0.0))
            rows.append(jnp.where(mask, sign * pw_i[n][None, :], 0.0))
        cr_ = rev_r if reverse else pw_r
        ci_ = rev_i if reverse else pw_i
        rows += [cr_, sign * ci_]
        return jnp.stack(rows)

    return tables(1.0, False), tables(-1.0, True)


def _seg_scan(s_ref, row0, seg, lam_ref, tab_ref, carry_ref, reverse, cseg_ref=None, extra=None):
    sgn = -1.0 if reverse else 1.0
    take, edge = (0, 7) if reverse else (7, 0)
    rowid = lax.broadcasted_iota(jnp.int32, (8, 128), 0)
    all_pairs = [(blk * 1024 + j * 128, blk * 1024 + j * 128 + 512, blk * 512 + j * 128)
                 for blk in range(S5_BLK) for j in range(4)]

    def rows(it):
        i = (seg - 1 - it) if reverse else it
        return i, pl.multiple_of(row0 + i * 8, 8)

    for half in range(2):
        pairs = all_pairs[8 * half:8 * half + 8]

        def pass1(it, carry):
            _, r0 = rows(it)
            out = []
            for n, (cre, cim, tc) in enumerate(pairs):
                lr = lam_ref[0, :, tc:tc + 128]
                li = sgn * lam_ref[1, :, tc:tc + 128]
                pr, pi = carry[2 * n], carry[2 * n + 1]
                xr = lr * pr - li * pi + s_ref[pl.ds(r0, 8), cre:cre + 128]
                xi = lr * pi + li * pr + s_ref[pl.ds(r0, 8), cim:cim + 128]
                s_ref[pl.ds(r0, 8), cre:cre + 128] = xr
                s_ref[pl.ds(r0, 8), cim:cim + 128] = xi
                out += [xr, xi]
            return tuple(out)

        ends = lax.fori_loop(0, seg, pass1, tuple(jnp.zeros((8, 128), F32) for _ in range(16)))

        start = []
        for n, (cre, cim, tc) in enumerate(pairs):
            xr, xi = ends[2 * n], ends[2 * n + 1]
            for lvl, sh in enumerate((1, 2, 4)):
                lr = tab_ref[2 * lvl, :, tc:tc + 128]
                li = tab_ref[2 * lvl + 1, :, tc:tc + 128]
                shift = (8 - sh) if reverse else sh
                sr = pltpu.roll(xr, shift, 0)
                si = pltpu.roll(xi, shift, 0)
                xr, xi = xr + lr * sr - li * si, xi + lr * si + li * sr
            pr = tab_ref[6, :, tc:tc + 128]
            pi = tab_ref[7, :, tc:tc + 128]
            c_r = carry_ref[:, cre:cre + 128]
            c_i = carry_ref[:, cim:cim + 128]
            er, ei = xr + pr * c_r - pi * c_i, xi + pr * c_i + pi * c_r
            back = 7 if reverse else 1
            in_r = jnp.where(rowid == edge, c_r, pltpu.roll(er, back, 0))
            in_i = jnp.where(rowid == edge, c_i, pltpu.roll(ei, back, 0))
            nr = jnp.sum(jnp.where(rowid == take, er, 0.0), axis=0, keepdims=True)
            ni = jnp.sum(jnp.where(rowid == take, ei, 0.0), axis=0, keepdims=True)
            carry_ref[:, cre:cre + 128] = jnp.broadcast_to(nr, (8, 128))
            carry_ref[:, cim:cim + 128] = jnp.broadcast_to(ni, (8, 128))
            if cseg_ref is not None:
                cseg_ref[0:8, cre:cre + 128] = in_r
                cseg_ref[0:8, cim:cim + 128] = in_i
            start += [in_r, in_i]

        def pass2(it, carry):
            i, r0 = rows(it)
            out = []
            for n, (cre, cim, tc) in enumerate(pairs):
                lr = lam_ref[0, :, tc:tc + 128]
                li = sgn * lam_ref[1, :, tc:tc + 128]
                dr, di = carry[2 * n], carry[2 * n + 1]
                dr, di = lr * dr - li * di, lr * di + li * dr
                xr = s_ref[pl.ds(r0, 8), cre:cre + 128] + dr
                xi = s_ref[pl.ds(r0, 8), cim:cim + 128] + di
                s_ref[pl.ds(r0, 8), cre:cre + 128] = xr
                s_ref[pl.ds(r0, 8), cim:cim + 128] = xi
                if extra is not None:
                    extra(i, cre, cim, tc, xr, xi)
                out += [dr, di]
            return tuple(out)

        lax.fori_loop(0, seg, pass2, tuple(start))


def _to_segments(src_refs, dst_ref, seg, first=None):
    sub = lax.broadcasted_iota(jnp.int32, (8, 1), 0)
    for i in range(seg):
        for c, src in enumerate(src_refs):
            v = src[pl.ds(i, 8, stride=seg), :]
            if first is not None:
                v = jnp.where(first & (sub * seg + i < PAD_ROWS), 0.0, v)
            dst_ref[8 * i:8 * i + 8, c * 128:(c + 1) * 128] = v


def _from_segments(src_ref, dst_ref, seg, fn=None, zero_head=None):
    m = seg // 8
    for j in range(seg):
        for c in range(src_ref.shape[0]):
            v = src_ref[c, pl.ds(64 * (j % m) + j // m, 8, stride=8), :]
            if zero_head is not None and j < PAD_ROWS // 8:
                v = jnp.where(zero_head, 0.0, v)
            dst_ref[8 * j:8 * j + 8, c * 128:(c + 1) * 128] = v if fn is None else fn(v)


def _gelu(x):
    c = math.sqrt(2.0 / math.pi)
    t = jnp.tanh(c * (x + 0.044715 * x * x * x))
    return 0.5 * x * (1.0 + t)


def _gelu_grad(x):
    c = math.sqrt(2.0 / math.pi)
    t = jnp.tanh(c * (x + 0.044715 * x * x * x))
    return 0.5 * (1.0 + t) + 0.5 * x * (1.0 - t * t) * c * (1.0 + 3.0 * 0.044715 * x * x)


def _s5_tile(lp):
    return _pick(lp, (384, 320, 256, 128, 64))


def _s5_fwd(p, bblk, cblk, lam8, tab_f, dskip, lp, comm=None):
    T = p.shape[0]
    ts = _s5_tile(lp)
    seg = ts // 8
    nblk, per_seq = T // ts, lp // ts
    ucol = 0

    def body(u0, u1, u2, u3, b_ref, c_ref, lam_ref, tab_ref, d_ref, y_ref, gy_ref, cin_ref, s_sc, u_sc, y_sc, carry_sc):
        r = pl.program_id(0)
        first = (r % per_seq) == 0

        @pl.when(first)
        def _():
            carry_sc[...] = jnp.zeros_like(carry_sc)

        cin_ref[0] = carry_sc[...]
        _to_segments((u0, u1, u2, u3), u_sc, seg, first)
        u = u_sc[...]
        ub = u.astype(BF16)
        for blk in range(S5_BLK):
            s_sc[:, blk * 1024:(blk + 1) * 1024] = _dot(ub[:, blk * 128:(blk + 1) * 128], b_ref[blk], 1, 0)
        _seg_scan(s_sc, 0, seg, lam_ref, tab_ref, carry_sc, False)
        for blk in range(S5_BLK):
            sb = s_sc[:, blk * 1024:(blk + 1) * 1024].astype(BF16)
            y_sc[blk] = _dot(sb, c_ref[blk], 1, 0) + d_ref[:, blk * 128:(blk + 1) * 128] * u[:, blk * 128:(blk + 1) * 128]
        _from_segments(y_sc, y_ref, seg)
        gy_ref[...] = _gelu(y_ref[...]).astype(BF16)

    def ublock(c):
        return pl.BlockSpec((ts, 128), lambda r: (r, 4 * ucol + c))

    return _ride_call(
        body, "s5_fwd", (nblk,),
        [ublock(0), ublock(1), ublock(2), ublock(3),
         pl.BlockSpec((S5_BLK, 128, 1024), lambda r: (0, 0, 0)),
         pl.BlockSpec((S5_BLK, 1024, 128), lambda r: (0, 0, 0)),
         pl.BlockSpec((2, 8, 2048), lambda r: (0, 0, 0)),
         pl.BlockSpec((8, 8, 2048), lambda r: (0, 0, 0)),
         pl.BlockSpec((1, S5_WIDTH), lambda r: (0, 0))],
        [pl.BlockSpec((ts, S5_WIDTH), lambda r: (r, 0)),
         pl.BlockSpec((ts, S5_WIDTH), lambda r: (r, 0)),
         pl.BlockSpec((1, 8, S5_COLS), lambda r: (r, 0, 0))],
        [jax.ShapeDtypeStruct((T, S5_WIDTH), F32),
         jax.ShapeDtypeStruct((T, S5_WIDTH), BF16),
         jax.ShapeDtypeStruct((nblk, 8, S5_COLS), F32)],
        [pltpu.VMEM((ts, S5_COLS), F32), pltpu.VMEM((ts, S5_WIDTH), F32),
         pltpu.VMEM((S5_BLK, ts, 128), F32), pltpu.VMEM((8, S5_COLS), F32)],
        (p, p, p, p, bblk.astype(BF16), cblk.astype(BF16), lam8, tab_f, dskip), comm=comm)


def _s5_bwd(p, y_pre, dgy, cin, bblk, cblk, lam8, tab_f, tab_b, dskip, dif, dp, lp, comm=None):
    T = p.shape[0]
    ts = _s5_tile(lp)
    seg = ts // 8
    nblk, per_seq = T // ts, lp // ts
    ucol = 0
    assert P_IF == P_U + S5_WIDTH and P_U % (S5_WIDTH + 128) == 0

    def body(u0, u1, u2, u3, y0, y1, y2, y3, g0, g1, g2, g3, cin_ref, b_ref, bt_ref, c_ref, ct_ref, lam_ref, tf_ref,
             tb_ref, d_ref, dif_ref, dpin_ref, du_ref, dbb_ref, dcb_ref, dlam_ref, dd_ref, s_sc, a_sc, u_sc, dy_sc,
             y_sc, w_sc, carry_sc, carry_b):
        t = pl.program_id(0)
        r = nblk - 1 - t
        first = (r % per_seq) == 0
        last = (r % per_seq) == per_seq - 1

        @pl.when(t == 0)
        def _():
            dbb_ref[...] = jnp.zeros_like(dbb_ref)
            dcb_ref[...] = jnp.zeros_like(dcb_ref)
            dlam_ref[...] = jnp.zeros_like(dlam_ref)
            dd_ref[...] = jnp.zeros_like(dd_ref)

        @pl.when(last)
        def _():
            carry_b[...] = jnp.zeros_like(carry_b)

        carry_sc[...] = cin_ref[0]
        _to_segments((u0, u1, u2, u3), u_sc, seg, first)
        u = u_sc[...]
        ub = u.astype(BF16)
        for blk in range(S5_BLK):
            s_sc[8:8 + ts, blk * 1024:(blk + 1) * 1024] = _dot(ub[:, blk * 128:(blk + 1) * 128], b_ref[blk], 1, 0)
        _seg_scan(s_sc, 8, seg, lam_ref, tf_ref, carry_sc, False, cseg_ref=s_sc)

        _to_segments((g0, g1, g2, g3), dy_sc, seg)
        _to_segments((y0, y1, y2, y3), y_sc, seg)
        dy = dy_sc[...] * _gelu_grad(y_sc[...])
        dy_sc[...] = dy
        dyb = dy.astype(BF16)
        dd_ref[0:1, :] += jnp.sum(dy * u, axis=0, keepdims=True)
        for blk in range(S5_BLK):
            a_sc[:, blk * 1024:(blk + 1) * 1024] = _dot(dyb[:, blk * 128:(blk + 1) * 128], ct_ref[blk], 1, 0)
            sb = s_sc[8:8 + ts, blk * 1024:(blk + 1) * 1024].astype(BF16)
            dcb_ref[blk] += _dot(sb, dyb[:, blk * 128:(blk + 1) * 128], 0, 0)

        def lam_grad(i, cre, cim, tc, a_r, a_i):
            r0 = pl.multiple_of(i * 8, 8)
            pr = s_sc[pl.ds(r0, 8), cre:cre + 128]
            pi = s_sc[pl.ds(r0, 8), cim:cim + 128]
            dlam_ref[0, :, tc:tc + 128] += a_r * pr + a_i * pi
            dlam_ref[1, :, tc:tc + 128] += a_i * pr - a_r * pi

        _seg_scan(a_sc, 0, seg, lam_ref, tb_ref, carry_b, True, extra=lam_grad)

        for blk in range(S5_BLK):
            ab = a_sc[:, blk * 1024:(blk + 1) * 1024].astype(BF16)
            w_sc[blk] = _dot(ab, bt_ref[blk], 1, 0) + d_ref[:, blk * 128:(blk + 1) * 128] * dy_sc[:, blk * 128:(blk + 1) * 128]
            dbb_ref[blk] += _dot(u_sc[:, blk * 128:(blk + 1) * 128].astype(BF16), ab, 0, 0)
        _from_segments(w_sc, y_sc, seg, zero_head=first)
        du_ref[:, 0:S5_WIDTH] = y_sc[...].astype(BF16)
        du_ref[:, S5_WIDTH:S5_WIDTH + 128] = dif_ref[...]

    const3 = lambda t: (0, 0, 0)
    rev = lambda t: (nblk - 1 - t, 0)

    def lanes(c0):
        return [pl.BlockSpec((ts, 128), lambda t, cc=c0 + c: (nblk - 1 - t, cc)) for c in range(4)]

    return _ride_call(
        body, "s5_bwd", (nblk,),
        lanes(4 * ucol) + lanes(0) + lanes(0) + [
            pl.BlockSpec((1, 8, S5_COLS), lambda t: (nblk - 1 - t, 0, 0)),
            pl.BlockSpec((S5_BLK, 128, 1024), const3),
            pl.BlockSpec((S5_BLK, 1024, 128), const3),
            pl.BlockSpec((S5_BLK, 1024, 128), const3),
            pl.BlockSpec((S5_BLK, 128, 1024), const3),
            pl.BlockSpec((2, 8, 2048), const3),
            pl.BlockSpec((8, 8, 2048), const3),
            pl.BlockSpec((8, 8, 2048), const3),
            pl.BlockSpec((1, S5_WIDTH), lambda t: (0, 0)),
            pl.BlockSpec((ts, 128), rev),
            pl.BlockSpec(memory_space=pl.ANY)],
        [pl.BlockSpec((ts, S5_WIDTH + 128), lambda t: (nblk - 1 - t, P_U // (S5_WIDTH + 128))),
         pl.BlockSpec((S5_BLK, 128, 1024), const3),
         pl.BlockSpec((S5_BLK, 1024, 128), const3),
         pl.BlockSpec((2, 8, 2048), const3),
         pl.BlockSpec((8, S5_WIDTH), lambda t: (0, 0))],
        [jax.ShapeDtypeStruct((T, PW), BF16),
         jax.ShapeDtypeStruct((S5_BLK, 128, 1024), F32),
         jax.ShapeDtypeStruct((S5_BLK, 1024, 128), F32),
         jax.ShapeDtypeStruct((2, 8, 2048), F32),
         jax.ShapeDtypeStruct((8, S5_WIDTH), F32)],
        [pltpu.VMEM((ts + 8, S5_COLS), F32), pltpu.VMEM((ts, S5_COLS), F32),
         pltpu.VMEM((ts, S5_WIDTH), F32), pltpu.VMEM((ts, S5_WIDTH), F32),
         pltpu.VMEM((ts, S5_WIDTH), F32), pltpu.VMEM((S5_BLK, ts, 128), F32),
         pltpu.VMEM((8, S5_COLS), F32), pltpu.VMEM((8, S5_COLS), F32)],
        (p, p, p, p, y_pre, y_pre, y_pre, y_pre, dgy, dgy, dgy, dgy, cin,
         bblk.astype(BF16), jnp.swapaxes(bblk, 1, 2).astype(BF16),
         cblk.astype(BF16), jnp.swapaxes(cblk, 1, 2).astype(BF16), lam8, tab_f, tab_b, dskip, dif, dp),
        comm=comm, aliases={22: 0})


CONV_HALO = 16


def _row_in_seq(i, tr, lp):
    rowid = lax.broadcasted_iota(jnp.int32, (tr, 1), 0)
    return (i * tr + rowid) % lp


def _conv_fwd(p, w, b, lp):
    T = p.shape[0]
    tr = _pick(T, (384, 320, 256, 128, 64))
    c = P_QK // 1024
    hl = CONV_HALO

    def body(x_ref, xp_ref, w_ref, b_ref, o_ref):
        i = pl.program_id(0)
        pos = _row_in_seq(i, tr, lp)
        x = jnp.where(pos < PAD_ROWS, 0.0, x_ref[...].astype(F32))
        pos_p = (i * tr - hl + lax.broadcasted_iota(jnp.int32, (hl, 1), 0)) % lp
        xp = jnp.where((pos_p < PAD_ROWS) | (i == 0), 0.0, xp_ref[...].astype(F32))
        xx = jnp.concatenate([xp, x], axis=0)
        acc = b_ref[...] + w_ref[3:4, :] * x
        for s in (1, 2, 3):
            acc = acc + w_ref[3 - s:4 - s, :] * pltpu.roll(xx, s, 0)[hl:hl + tr]
        o_ref[...] = acc * _sigmoid(acc)

    return pl.pallas_call(
        body, name="conv_fwd", grid=(T // tr,),
        in_specs=[pl.BlockSpec((tr, 1024), lambda i: (i, c)),
                  pl.BlockSpec((hl, 1024), lambda i: (jnp.maximum(i * (tr // hl) - 1, 0), c)),
                  pl.BlockSpec((4, 1024), lambda i: (0, 0)),
                  pl.BlockSpec((1, 1024), lambda i: (0, 0))],
        out_specs=pl.BlockSpec((tr, 1024), lambda i: (i, 0)),
        out_shape=jax.ShapeDtypeStruct((T, 1024), F32),
        compiler_params=_cparams(("arbitrary",)),
    )(p, p, w, b)


def _conv_bwd(p, dqk, w, b, dp, lp):
    T = p.shape[0]
    tr = _pick(T, (384, 320, 256, 128, 64))
    c = P_QK // 1024
    nb = T // tr
    hl = CONV_HALO

    def body(x_ref, xp_ref, xn_ref, g_ref, gn_ref, w_ref, b_ref, dpin_ref, dx_ref, dw_ref, db_ref, x_sc, g_sc, part):
        i = pl.program_id(0)

        @pl.when(i == 0)
        def _():
            dw_ref[...] = jnp.zeros_like(dw_ref)
            db_ref[...] = jnp.zeros_like(db_ref)

        def seqpos(off, n):
            return (i * tr + off + lax.broadcasted_iota(jnp.int32, (n, 1), 0)) % lp

        x_sc[0:hl, :] = jnp.where((seqpos(-hl, hl) < PAD_ROWS) | (i == 0), 0.0, xp_ref[...].astype(F32))
        x_sc[hl:hl + tr, :] = jnp.where(seqpos(0, tr) < PAD_ROWS, 0.0, x_ref[...].astype(F32))
        x_sc[hl + tr:, :] = jnp.where((seqpos(tr, hl) < PAD_ROWS) | (i == nb - 1), 0.0, xn_ref[...].astype(F32))
        g_sc[0:tr, :] = g_ref[...].astype(F32)
        g_sc[tr:, :] = gn_ref[...].astype(F32)
        part[...] = jnp.zeros_like(part)
        taps = [w_ref[t:t + 1, :] for t in range(4)]
        bias = b_ref[...]

        ngroups = tr // hl

        def group(g, prev):
            r0 = pl.multiple_of(g * hl, hl)
            win = x_sc[pl.ds(r0, 2 * hl), :]
            sh = [win] + [pltpu.roll(win, s, 0) for s in (1, 2, 3)]
            acc = bias
            for s in range(4):
                acc = acc + taps[3 - s] * sh[s][hl:2 * hl]
            sg = _sigmoid(acc)
            rows = r0 + lax.broadcasted_iota(jnp.int32, (hl, 1), 0)
            pos = (i * tr + rows) % lp
            valid = (pos >= PAD_ROWS) & ((g < ngroups) | (i < nb - 1))
            cur = jnp.where(valid, g_sc[pl.ds(r0, hl), :] * (sg * (1.0 + acc * (1.0 - sg))), 0.0)
            own = jnp.where(g < ngroups, cur, 0.0)
            part[4] += own
            for s in range(4):
                part[3 - s] += own * sh[s][hl:2 * hl]

            @pl.when(g > 0)
            def _():
                both = jnp.concatenate([prev, cur], axis=0)
                dx = taps[3] * prev
                for s in (1, 2, 3):
                    dx = dx + taps[3 - s] * pltpu.roll(both, 2 * hl - s, 0)[0:hl]
                q0 = pl.multiple_of(r0 - hl, hl)
                pos_prev = (i * tr + q0 + lax.broadcasted_iota(jnp.int32, (hl, 1), 0)) % lp
                dx_ref[pl.ds(q0, hl), :] = jnp.where(pos_prev < PAD_ROWS, 0.0, dx).astype(BF16)

            return cur

        lax.fori_loop(0, ngroups + 1, group, jnp.zeros((hl, 1024), F32))
        db_ref[0:1, :] += jnp.sum(part[4], axis=0, keepdims=True)
        for t in range(4):
            dw_ref[t:t + 1, :] += jnp.sum(part[t], axis=0, keepdims=True)

    t8 = tr // hl
    return pl.pallas_call(
        body, name="conv_bwd", grid=(nb,),
        in_specs=[pl.BlockSpec((tr, 1024), lambda i: (i, c)),
                  pl.BlockSpec((hl, 1024), lambda i: (jnp.maximum(i * t8 - 1, 0), c)),
                  pl.BlockSpec((hl, 1024), lambda i: (jnp.minimum((i + 1) * t8, nb * t8 - 1), c)),
                  pl.BlockSpec((tr, 1024), lambda i: (i, 0)),
                  pl.BlockSpec((hl, 1024), lambda i: (jnp.minimum((i + 1) * t8, nb * t8 - 1), 0)),
                  pl.BlockSpec((4, 1024), lambda i: (0, 0)),
                  pl.BlockSpec((1, 1024), lambda i: (0, 0)),
                  pl.BlockSpec(memory_space=pl.ANY)],
        out_specs=(pl.BlockSpec((tr, 1024), lambda i: (i, c)),
                   pl.BlockSpec((8, 1024), lambda i: (0, 0)),
                   pl.BlockSpec((8, 1024), lambda i: (0, 0))),
        out_shape=(jax.ShapeDtypeStruct((T, PW), BF16),
                   jax.ShapeDtypeStruct((8, 1024), F32),
                   jax.ShapeDtypeStruct((8, 1024), F32)),
        input_output_aliases={7: 0},
        scratch_shapes=[pltpu.VMEM((tr + 2 * hl, 1024), F32), pltpu.VMEM((tr + hl, 1024), F32),
                        pltpu.VMEM((5, hl, 1024), F32)],
        compiler_params=_cparams(("arbitrary",)),
    )(p, p, p, dqk, dqk, w, b, dp)


def _split3(x):
    hi = x.astype(BF16)
    r1 = x - hi.astype(F32)
    mid = r1.astype(BF16)
    lo = (r1 - mid.astype(F32)).astype(BF16)
    return hi, mid, lo


def _tri_sum(x, upper):
    r = lax.broadcasted_iota(jnp.int32, (CHUNK, CHUNK), 0)
    c = lax.broadcasted_iota(jnp.int32, (CHUNK, CHUNK), 1)
    tri = jnp.where((r <= c) if upper else (r >= c), 1.0, 0.0).astype(BF16)
    hi, mid, lo = _split3(x)
    return _dot(tri, hi, 1, 0) + _dot(tri, mid, 1, 0) + _dot(tri, lo, 1, 0)


def _lane_col(x, lane):
    l = lax.broadcasted_iota(jnp.int32, x.shape, 1)
    return jnp.sum(jnp.where(l == lane, x, 0.0), axis=1, keepdims=True)


def _to_row(col):
    r = lax.broadcasted_iota(jnp.int32, (CHUNK, CHUNK), 0)
    c = lax.broadcasted_iota(jnp.int32, (CHUNK, CHUNK), 1)
    return jnp.sum(jnp.where(r == c, col, 0.0), axis=0, keepdims=True)


def _to_col(row):
    r = lax.broadcasted_iota(jnp.int32, (CHUNK, CHUNK), 0)
    c = lax.broadcasted_iota(jnp.int32, (CHUNK, CHUNK), 1)
    return jnp.sum(jnp.where(r == c, row, 0.0), axis=1, keepdims=True)


def _log_sigmoid(x):
    return jnp.minimum(x, 0.0) - jnp.log(1.0 + jnp.exp(-jnp.abs(x)))


def _mlstm_gates(ifv, padmask):
    lf = jnp.where(padmask, 0.0, _log_sigmoid(ifv))
    b_all = _tri_sum(lf, False)
    li = jnp.where(padmask, -jnp.inf, ifv)
    return li, b_all


def _mlstm_head_fwd(q, k, v, li_col, b_col, c_st, n_st, m_st):
    r = lax.broadcasted_iota(jnp.int32, (CHUNK, CHUNK), 0)
    c = lax.broadcasted_iota(jnp.int32, (CHUNK, CHUNK), 1)
    rowid = lax.broadcasted_iota(jnp.int32, (CHUNK, 1), 0)
    b_row = _to_row(b_col)
    li_row = _to_row(li_col)
    dmat = jnp.where(r >= c, b_col - b_row + li_row, -jnp.inf)
    m_inter = b_col + m_st
    m_row = jnp.maximum(m_inter, jnp.max(dmat, axis=1, keepdims=True))
    w_intra = jnp.exp(dmat - m_row)
    w_inter = jnp.exp(m_inter - m_row)
    qb, kb, vb = q.astype(BF16), k.astype(BF16), v.astype(BF16)
    qk = _dot(qb, kb, 1, 1)
    s = qk * w_intra
    cb = c_st.astype(BF16)
    qc = _dot(qb, cb, 1, 0)
    qn = jnp.sum(q * n_st, axis=1, keepdims=True)
    num = _dot(s.astype(BF16), vb, 1, 0) + w_inter * qc
    den = jnp.sum(s, axis=1, keepdims=True) + w_inter * qn
    floor = jnp.exp(-m_row)
    rinv = 1.0 / jnp.maximum(jnp.abs(den), floor)
    h = num * rinv
    b_last = jnp.sum(jnp.where(rowid == CHUNK - 1, b_col, 0.0), axis=0, keepdims=True)
    g_col = b_last - b_col + li_col
    m_new = jnp.maximum(b_last + m_st, jnp.max(g_col, axis=0, keepdims=True))
    w_k = jnp.exp(g_col - m_new)
    decay = jnp.exp(b_last + m_st - m_new)
    kw = w_k * k
    c_new = decay * c_st + _dot(kw.astype(BF16), vb, 0, 0)
    n_new = decay * n_st + jnp.sum(kw, axis=0, keepdims=True)
    return dict(h=h, c_new=c_new, n_new=n_new, m_new=m_new, w_intra=w_intra, w_inter=w_inter, s=s,
                qc=qc, qn=qn, den=den, floor=floor, rinv=rinv, w_k=w_k, decay=decay, kw=kw,
                qb=qb, kb=kb, vb=vb, cb=cb)


def _mlstm_fwd(p, puif, qk, lp):
    T = p.shape[0]
    nch = lp // CHUNK
    B = T // lp
    scale = M_DK ** -0.5

    def body(q_ref, k_ref, v_ref, if_ref, h_ref, cst_ref, nm_ref, *state):
        c_scs, nm_scs = state[:B * M_HEADS], state[B * M_HEADS:]
        ci = pl.program_id(0)

        @pl.when(ci == 0)
        def _():
            for ref in state:
                ref[...] = jnp.zeros_like(ref)

        rowid = lax.broadcasted_iota(jnp.int32, (CHUNK, 1), 0)
        padmask = (ci == 0) & (rowid < PAD_ROWS)
        for bb in range(B):
            li_all, b_all = _mlstm_gates(if_ref[bb], padmask)
            for hd in range(M_HEADS):
                c_sc, nm_sc = c_scs[bb * M_HEADS + hd], nm_scs[bb * M_HEADS + hd]
                c_st, n_st, m_row = c_sc[...], nm_sc[0:1, :], nm_sc[1:2, :]
                cst_ref[bb, 0, hd * M_DK:(hd + 1) * M_DK, :] = c_st
                nm_ref[bb, 0, hd:hd + 1, :] = n_st
                nm_ref[bb, 0, M_HEADS + hd:M_HEADS + hd + 1, :] = m_row
                q = q_ref[bb, :, hd * M_DK:(hd + 1) * M_DK]
                k = k_ref[bb, :, hd * M_DK:(hd + 1) * M_DK] * scale
                v = v_ref[bb, :, hd * M_DV:(hd + 1) * M_DV]
                o = _mlstm_head_fwd(q, k, v, _lane_col(li_all, hd), _lane_col(b_all, M_HEADS + hd),
                                    c_st, n_st, nm_sc[1:2, 0:1])
                h_ref[bb, :, hd * M_DV:(hd + 1) * M_DV] = o["h"].astype(BF16)
                c_sc[...] = o["c_new"]
                nm_sc[0:1, :] = o["n_new"]
                nm_sc[1:2, :] = jnp.broadcast_to(o["m_new"], (1, 128))

    qk3, p3, uif3 = qk.reshape(B, lp, 1024), p.reshape(B, lp, p.shape[1]), puif.reshape(B, lp, puif.shape[1])
    h, cst, nm = pl.pallas_call(
        body, name="mlstm_fwd", grid=(nch,),
        in_specs=[pl.BlockSpec((B, CHUNK, 512), lambda ci: (0, ci, 0)),
                  pl.BlockSpec((B, CHUNK, 512), lambda ci: (0, ci, 1)),
                  pl.BlockSpec((B, CHUNK, 1024), lambda ci: (0, ci, P_V // 1024)),
                  pl.BlockSpec((B, CHUNK, 128), lambda ci: (0, ci, S5_WIDTH // 128))],
        out_specs=(pl.BlockSpec((B, CHUNK, 1024), lambda ci: (0, ci, 0)),
                   pl.BlockSpec((B, 1, M_HEADS * M_DK, M_DV), lambda ci: (0, ci, 0, 0)),
                   pl.BlockSpec((B, 1, 8, 128), lambda ci: (0, ci, 0, 0))),
        out_shape=(jax.ShapeDtypeStruct((B, lp, 1024), BF16),
                   jax.ShapeDtypeStruct((B, nch, M_HEADS * M_DK, M_DV), F32),
                   jax.ShapeDtypeStruct((B, nch, 8, 128), F32)),
        scratch_shapes=([pltpu.VMEM((M_DK, M_DV), F32)] * (B * M_HEADS)
                        + [pltpu.VMEM((8, 128), F32)] * (B * M_HEADS)),
        compiler_params=_cparams(("arbitrary",)),
    )(qk3, qk3, p3, uif3)
    return h.reshape(T, 1024), cst, nm


def _mlstm_bwd(p, puif, qk, cst, nm, dh, dp, lp, comm=None):
    T = p.shape[0]
    nch = lp // CHUNK
    B = T // lp
    scale = M_DK ** -0.5

    def body(q_ref, k_ref, v_ref, if_ref, cst_ref, nm_ref, dh_ref, dpin_ref, dqk_ref, dv_ref, dif_ref, *state):
        t = pl.program_id(0)
        ci = nch - 1 - t

        @pl.when(t == 0)
        def _():
            for ref in state:
                ref[...] = jnp.zeros_like(ref)

        for bb in range(B):
            one_sequence(bb, ci, q_ref, k_ref, v_ref, if_ref, cst_ref, nm_ref, dh_ref, dqk_ref, dv_ref, dif_ref,
                         state[bb * M_HEADS:(bb + 1) * M_HEADS],
                         state[(B + bb) * M_HEADS:(B + bb + 1) * M_HEADS])

    def one_sequence(bb, ci, q_ref, k_ref, v_ref, if_ref, cst_ref, nm_ref, dh_ref, dqk_ref, dv_ref, dif_ref,
                     dc_scs, dn_scs):
        rowid = lax.broadcasted_iota(jnp.int32, (CHUNK, 1), 0)
        lane = lax.broadcasted_iota(jnp.int32, (CHUNK, 128), 1)
        padmask = (ci == 0) & (rowid < PAD_ROWS)
        ifv = if_ref[bb]
        li_all, b_all = _mlstm_gates(ifv, padmask)
        db_all = jnp.zeros((CHUNK, 128), F32)
        dli_all = jnp.zeros((CHUNK, 128), F32)
        for hd in range(M_HEADS):
            q = q_ref[bb, :, hd * M_DK:(hd + 1) * M_DK]
            k = k_ref[bb, :, hd * M_DK:(hd + 1) * M_DK] * scale
            v = v_ref[bb, :, hd * M_DV:(hd + 1) * M_DV]
            c_st = cst_ref[bb, 0, hd * M_DK:(hd + 1) * M_DK, :]
            n_st = nm_ref[bb, 0, hd:hd + 1, :]
            m_st = nm_ref[bb, 0, M_HEADS + hd:M_HEADS + hd + 1, 0:1]
            o = _mlstm_head_fwd(q, k, v, _lane_col(li_all, hd), _lane_col(b_all, M_HEADS + hd), c_st, n_st, m_st)
            dc_sc, dn_sc = dc_scs[hd], dn_scs[hd]
            dc_new = dc_sc[...]
            dn_new = dn_sc[0:1, :]
            dcb = dc_new.astype(BF16)
            dhh = dh_ref[bb, :, hd * M_DV:(hd + 1) * M_DV].astype(F32)
            dnum = dhh * o["rinv"]
            dhh_h = jnp.sum(dhh * o["h"], axis=1, keepdims=True)
            sgn = jnp.where(o["den"] >= 0.0, 1.0, -1.0)
            dden = jnp.where(jnp.abs(o["den"]) > o["floor"], -dhh_h * o["rinv"] * sgn, 0.0)
            dnb = dnum.astype(BF16)
            ds = _dot(dnb, o["vb"], 1, 1) + dden
            sb = o["s"].astype(BF16)
            kwb = o["kw"].astype(BF16)
            dv = _dot(sb, dnb, 0, 0) + _dot(kwb, dcb, 1, 0)
            dqk_m = (ds * o["w_intra"]).astype(BF16)
            wdn = o["w_inter"] * dnum
            wdd = o["w_inter"] * dden
            dq = _dot(dqk_m, o["kb"], 1, 0) + _dot(wdn.astype(BF16), o["cb"], 1, 1) + wdd * n_st
            vdc = _dot(o["vb"], dcb, 1, 1)
            dk = _dot(dqk_m, o["qb"], 0, 0) + o["w_k"] * (vdc + dn_new)
            dd = ds * o["s"]
            dd_col = _to_col(jnp.sum(dd, axis=0, keepdims=True))
            dmi = o["w_inter"] * (jnp.sum(dnum * o["qc"], axis=1, keepdims=True) + dden * o["qn"])
            dg = o["w_k"] * (jnp.sum(k * vdc, axis=1, keepdims=True) + jnp.sum(k * dn_new, axis=1, keepdims=True))
            d_blast = (o["decay"] * (jnp.sum(jnp.sum(dc_new * c_st, axis=1, keepdims=True), axis=0, keepdims=True)
                                     + jnp.sum(dn_new * n_st, axis=1, keepdims=True))
                       + jnp.sum(dg, axis=0, keepdims=True))
            db_col = jnp.sum(dd, axis=1, keepdims=True) - dd_col + dmi - dg
            db_col = db_col + jnp.where(rowid == CHUNK - 1, d_blast, 0.0)
            dli_col = dd_col + dg
            db_all = db_all + jnp.where(lane == M_HEADS + hd, db_col, 0.0)
            dli_all = dli_all + jnp.where(lane == hd, dli_col, 0.0)
            dc_sc[...] = o["decay"] * dc_new + _dot(o["qb"], wdn.astype(BF16), 0, 0)
            dn_sc[0:1, :] = o["decay"] * dn_new + jnp.sum(q * wdd, axis=0, keepdims=True)
            dqk_ref[bb, :, hd * M_DK:(hd + 1) * M_DK] = dq.astype(BF16)
            dqk_ref[bb, :, 512 + hd * M_DK:512 + (hd + 1) * M_DK] = (dk * scale).astype(BF16)
            dv_ref[bb, :, hd * M_DV:(hd + 1) * M_DV] = dv.astype(BF16)
        dlf = _tri_sum(db_all, True)
        dif = dli_all + dlf * _sigmoid(-ifv)
        dif_ref[bb] = jnp.where(padmask | (lane >= 2 * M_HEADS), 0.0, dif).astype(BF16)

    def rev(cb):
        return lambda t: (0, nch - 1 - t, cb)

    rev4 = lambda t: (0, nch - 1 - t, 0, 0)
    qk3, p3, dh3 = qk.reshape(B, lp, 1024), p.reshape(B, lp, p.shape[1]), dh.reshape(B, lp, 1024)
    uif3 = puif.reshape(B, lp, puif.shape[1])
    (dqk, dv, dif), comm_res = _ride_call(
        body, "mlstm_bwd", (nch,),
        [pl.BlockSpec((B, CHUNK, 512), rev(0)),
         pl.BlockSpec((B, CHUNK, 512), rev(1)),
         pl.BlockSpec((B, CHUNK, 1024), rev(P_V // 1024)),
         pl.BlockSpec((B, CHUNK, 128), rev(S5_WIDTH // 128)),
         pl.BlockSpec((B, 1, M_HEADS * M_DK, M_DV), rev4),
         pl.BlockSpec((B, 1, 8, 128), rev4),
         pl.BlockSpec((B, CHUNK, 1024), rev(0)),
         pl.BlockSpec(memory_space=pl.ANY)],
        [pl.BlockSpec((B, CHUNK, 1024), rev(0)),
         pl.BlockSpec((B, CHUNK, 1024), rev(P_V // 1024)),
         pl.BlockSpec((B, CHUNK, 128), rev(0))],
        [jax.ShapeDtypeStruct((B, lp, 1024), BF16),
         jax.ShapeDtypeStruct((B, lp, PW), BF16),
         jax.ShapeDtypeStruct((B, lp, 128), BF16)],
        [pltpu.VMEM((M_DK, M_DV), F32)] * (B * M_HEADS) + [pltpu.VMEM((8, 128), F32)] * (B * M_HEADS),
        (qk3, qk3, p3, uif3, cst, nm, dh3, dp.reshape(B, lp, PW)), comm=comm, aliases={7: 1})
    return dqk.reshape(T, 1024), dv.reshape(T, PW), dif.reshape(T, 128), comm_res


def _headnorm_fwd(hm, p, g):
    T = hm.shape[0]
    tr = _pick(T, (384, 320, 256, 128, 64))

    def body(h_ref, o_ref, g_ref, a_ref):
        for hd in range(M_HEADS):
            sl = slice(hd * M_DV, (hd + 1) * M_DV)
            hn = _ln_rows(h_ref[:, sl].astype(F32), g_ref[:, sl], 0.0)
            a_ref[:, sl] = (_sigmoid(o_ref[:, sl].astype(F32)) * hn).astype(BF16)

    return pl.pallas_call(
        body, name="headnorm_fwd", grid=(T // tr,),
        in_specs=[pl.BlockSpec((tr, 1024), lambda i: (i, 0)),
                  pl.BlockSpec((tr, 1024), lambda i: (i, P_O // 1024)),
                  pl.BlockSpec((1, 1024), lambda i: (0, 0))],
        out_specs=pl.BlockSpec((tr, 1024), lambda i: (i, 0)),
        out_shape=jax.ShapeDtypeStruct((T, 1024), BF16),
        compiler_params=_cparams(("arbitrary",)),
    )(hm, p, g)


def _headnorm_bwd(da, hm, p, g, dp):
    T = hm.shape[0]
    tr = _pick(T, (384, 320, 256, 128, 64))

    def body(da_ref, h_ref, o_ref, g_ref, dpin_ref, dh_ref, do_ref, dg_ref):
        @pl.when(pl.program_id(0) == 0)
        def _():
            dg_ref[...] = jnp.zeros_like(dg_ref)

        for hd in range(M_HEADS):
            sl = slice(hd * M_DV, (hd + 1) * M_DV)
            gg = g_ref[:, sl]
            so = _sigmoid(o_ref[:, sl].astype(F32))
            da = da_ref[:, sl].astype(F32)
            dhn = da * so
            dv, xhat = _ln_bwd_rows(dhn, h_ref[:, sl].astype(F32), gg)
            dh_ref[:, sl] = dv.astype(BF16)
            do_ref[:, sl] = (da * (xhat * gg) * so * (1.0 - so)).astype(BF16)
            dg_ref[0:1, sl] += jnp.sum(dhn * xhat, axis=0, keepdims=True)

    row = pl.BlockSpec((tr, 1024), lambda i: (i, 0))
    return pl.pallas_call(
        body, name="headnorm_bwd", grid=(T // tr,),
        in_specs=[row, row, pl.BlockSpec((tr, 1024), lambda i: (i, P_O // 1024)),
                  pl.BlockSpec((1, 1024), lambda i: (0, 0)), pl.BlockSpec(memory_space=pl.ANY)],
        out_specs=(row, pl.BlockSpec((tr, 1024), lambda i: (i, P_O // 1024)),
                   pl.BlockSpec((8, 1024), lambda i: (0, 0))),
        out_shape=(jax.ShapeDtypeStruct((T, 1024), BF16), jax.ShapeDtypeStruct((T, PW), BF16),
                   jax.ShapeDtypeStruct((8, 1024), F32)),
        input_output_aliases={4: 1},
        compiler_params=_cparams(("arbitrary",)),
    )(da, hm, p, g, dp)


def _mix_fwd(z, ym, p):
    T = ym.shape[0]
    tr = _pick(T, (384, 320, 256, 128, 64))

    def body(z1_ref, z2_ref, ym_ref, gs_ref, gm_ref, o_ref):
        ys = z1_ref[...].astype(F32) * _sigmoid(z2_ref[...].astype(F32))
        o_ref[...] = (_sigmoid(gs_ref[...].astype(F32)) * ys
                      + _sigmoid(gm_ref[...].astype(F32)) * ym_ref[...].astype(F32)).astype(BF16)

    def col(cb):
        return pl.BlockSpec((tr, 1024), lambda i: (i, cb))

    return pl.pallas_call(
        body, name="mix_fwd", grid=(T // tr,),
        in_specs=[col(0), col(1), col(0), col(P_GS // 1024), col(P_GM // 1024)],
        out_specs=col(0),
        out_shape=jax.ShapeDtypeStruct((T, 1024), BF16),
        compiler_params=_cparams(("arbitrary",)),
    )(z, z, ym, p, p)


def _mix_bwd(dmix, z, ym, p):
    T = ym.shape[0]
    tr = _pick(T, (384, 320, 256, 128, 64))

    def body(d_ref, z1_ref, z2_ref, ym_ref, gs_ref, gm_ref, dz_ref, dym_ref, dp_ref):
        d = d_ref[...].astype(F32)
        z1 = z1_ref[...].astype(F32)
        s2 = _sigmoid(z2_ref[...].astype(F32))
        ss = _sigmoid(gs_ref[...].astype(F32))
        sm = _sigmoid(gm_ref[...].astype(F32))
        ys = z1 * s2
        dys = d * ss
        dp_ref[:, 0:1024] = (d * ys * ss * (1.0 - ss)).astype(BF16)
        dp_ref[:, 1024:2048] = (d * ym_ref[...].astype(F32) * sm * (1.0 - sm)).astype(BF16)
        dym_ref[...] = (d * sm).astype(BF16)
        dz_ref[:, 0:1024] = (dys * s2).astype(BF16)
        dz_ref[:, 1024:2048] = (dys * z1 * s2 * (1.0 - s2)).astype(BF16)

    def col(cb):
        return pl.BlockSpec((tr, 1024), lambda i: (i, cb))

    o = jax.ShapeDtypeStruct((T, 1024), BF16)
    return pl.pallas_call(
        body, name="mix_bwd", grid=(T // tr,),
        in_specs=[col(0), col(0), col(1), col(0), col(P_GS // 1024), col(P_GM // 1024)],
        out_specs=(pl.BlockSpec((tr, 2048), lambda i: (i, 0)), col(0),
                   pl.BlockSpec((tr, 2048), lambda i: (i, P_GS // 2048))),
        out_shape=(jax.ShapeDtypeStruct((T, 2048), BF16), o, jax.ShapeDtypeStruct((T, PW), BF16)),
        compiler_params=_cparams(("arbitrary",)),
    )(dmix, z, z, ym, p, p)


def _adamw_math(w, g, m, v):
    m2 = ADAM_B1 * m + (1.0 - ADAM_B1) * g
    v2 = ADAM_B2 * v + (1.0 - ADAM_B2) * jnp.square(g)
    m_hat = m2 / (1.0 - ADAM_B1 ** ADAM_STEP)
    v_hat = v2 / (1.0 - ADAM_B2 ** ADAM_STEP)
    delta = -ADAM_LR * (m_hat / (jnp.sqrt(v_hat) + ADAM_EPS) + ADAM_WD * w)
    return delta, m2, v2


def _adamw_big(recv, w, m, v, name):
    R, C = w.shape
    tr, tc = _shard_tile(R, C)
    ns = recv.shape[0]

    def body(r_ref, w_ref, m_ref, v_ref, g_ref, d_ref, m2_ref, v2_ref):
        g = r_ref[0].astype(F32)
        for s in range(1, ns):
            g = g + r_ref[s].astype(F32)
        d, m2, v2 = _adamw_math(w_ref[...], g, m_ref[...], v_ref[...])
        g_ref[...] = g
        d_ref[...] = d
        m2_ref[...] = m2
        v2_ref[...] = v2

    blk = pl.BlockSpec((tr, tc), lambda i, j: (i, j))
    o = jax.ShapeDtypeStruct((R, C), F32)
    return pl.pallas_call(
        body, name=name, grid=(R // tr, C // tc),
        in_specs=[pl.BlockSpec((ns, tr, tc), lambda i, j: (0, i, j)), blk, blk, blk],
        out_specs=(blk,) * 4, out_shape=(o,) * 4,
        compiler_params=_cparams(("arbitrary", "arbitrary")),
    )(recv, w, m, v)


def _adamw_small(gs, ws, ms, vs):
    n = len(ws)
    shapes = [tuple(w.shape) for w in ws]

    def flat2(a):
        return a.reshape(1, -1) if a.ndim == 1 else a.reshape(-1, a.shape[-1])

    def body(*refs):
        g, w, m, v = refs[:n], refs[n:2 * n], refs[2 * n:3 * n], refs[3 * n:4 * n]
        d, m2, v2 = refs[4 * n:5 * n], refs[5 * n:6 * n], refs[6 * n:]
        for i in range(n):
            di, mi, vi = _adamw_math(w[i][...], g[i][...], m[i][...], v[i][...])
            d[i][...] = di
            m2[i][...] = mi
            v2[i][...] = vi

    args = [flat2(a) for a in list(gs) + list(ws) + list(ms) + list(vs)]
    vm = pl.BlockSpec(memory_space=pltpu.VMEM)
    outs = pl.pallas_call(
        body, name="adamw_small",
        in_specs=[vm] * (4 * n), out_specs=tuple([vm] * (3 * n)),
        out_shape=tuple(jax.ShapeDtypeStruct(a.shape, F32) for a in args[:n] * 3),
        compiler_params=pltpu.CompilerParams(vmem_limit_bytes=VMEM_LIMIT),
    )(*args)
    outs = [o.reshape(s) for o, s in zip(outs, shapes * 3)]
    return outs[:n], outs[n:2 * n], outs[2 * n:]


def _pack(arrs):
    parts = []
    for a in arrs:
        f = a.reshape(-1).astype(F32)
        n = -(-f.shape[0] // 1024) * 1024
        parts.append(jnp.pad(f, (0, n - f.shape[0])))
    return jnp.concatenate(parts).reshape(-1, 128)


def _unpack(pack, shapes):
    flat = pack.reshape(-1)
    out, off = [], 0
    for shp in shapes:
        n = math.prod(shp)
        out.append(flat[off:off + n].reshape(shp))
        off += -(-n // 1024) * 1024
    return out


def _cols_from_shards(g):
    return jnp.transpose(g, (1, 0, 2)).reshape(g.shape[1], -1)


def _cols_to_shards(w):
    R = w.shape[0]
    return jnp.transpose(w.reshape(R, N_DEV, -1), (1, 0, 2))


def _nat_to_aligned(w, axis):
    def sl(start, size):
        return lax.slice_in_dim(w, start, start + size, axis=axis)

    pads = [(0, 0)] * w.ndim
    pads[axis] = (0, PW - P_IF - 8)
    return jnp.concatenate([sl(N_V, 1024), sl(N_O, 1024), sl(N_GS, 1024), sl(N_GM, 1024), sl(N_Q, 1024),
                            sl(N_U, 512), jnp.pad(sl(N_I, 8), pads)], axis=axis)


def _aligned_to_nat(w, axis):
    def sl(start, size):
        return lax.slice_in_dim(w, start, start + size, axis=axis)

    return jnp.concatenate([sl(P_U, 512), sl(P_QK, 1024), sl(P_V, 1024), sl(P_O, 1024), sl(P_IF, 8),
                            sl(P_GS, 1024), sl(P_GM, 1024)], axis=axis)


def kernel(x, meta_tokens, ln0_g, ln0_b, w_in, b_in, qk_conv_w, qk_conv_b, s5_lambda_re, s5_lambda_im, s5_log_dt, s5_b_re, s5_b_im, s5_c_re, s5_c_im, s5_d, s5_w_glu, m_norm_g, m_w_out, w_o, ln1_g, ln1_b, w_up, b_up, w_down, ln2_g, ln2_b, loss_target, m_meta_tokens, m_ln0_g, m_ln0_b, m_w_in, m_b_in, m_qk_conv_w, m_qk_conv_b, m_s5_lambda_re, m_s5_lambda_im, m_s5_log_dt, m_s5_b_re, m_s5_b_im, m_s5_c_re, m_s5_c_im, m_s5_d, m_s5_w_glu, m_m_norm_g, m_m_w_out, m_w_o, m_ln1_g, m_ln1_b, m_w_up, m_b_up, m_w_down, m_ln2_g, m_ln2_b, v_meta_tokens, v_ln0_g, v_ln0_b, v_w_in, v_b_in, v_qk_conv_w, v_qk_conv_b, v_s5_lambda_re, v_s5_lambda_im, v_s5_log_dt, v_s5_b_re, v_s5_b_im, v_s5_c_re, v_s5_c_im, v_s5_d, v_s5_w_glu, v_m_norm_g, v_m_w_out, v_w_o, v_ln1_g, v_ln1_b, v_w_up, v_b_up, v_w_down, v_ln2_g, v_ln2_b):
    B, S, D = x.shape
    lp = S + CHUNK
    me = _my_id()

    ln0g, ln0b = ln0_g.reshape(1, D), ln0_b.reshape(1, D)
    (h0, h0b), first = _ln0_fwd(x, ln0g, ln0b, comm=(_Gather, [w_in[0].T.astype(BF16), meta_tokens, qk_conv_w[0]]))
    win_t = _nat_to_aligned(first[0].reshape(IN_NAT, D), 0)
    meta_f = _cols_from_shards(first[1])
    convw_f = _cols_from_shards(first[2])
    later = [a.astype(BF16) for a in (s5_w_glu[0], m_w_out[0], w_o[0])]
    latest = [a.astype(BF16) for a in (w_up[0], w_down[0])]
    b_in_al = _nat_to_aligned(b_in, 1)

    s5_args = (s5_lambda_re[0], s5_lambda_im[0], s5_log_dt[0], s5_b_re[0], s5_b_im[0], s5_c_re[0], s5_c_im[0])
    (ar, ai, bblk, cblk), s5_vjp = jax.vjp(_s5_prep, *s5_args)
    seg = _s5_tile(lp) // 8
    mu_r, mu_i = ar, ai
    for _ in range(seg - 1):
        mu_r, mu_i = mu_r * ar - mu_i * ai, mu_r * ai + mu_i * ar
    tab_f, tab_b = _scan_tables(mu_r, mu_i)
    lam8 = jnp.stack([jnp.broadcast_to(ar, (8, 2048)), jnp.broadcast_to(ai, (8, 2048))])

    h0, h0b = _ln0_head(meta_f, ln0g, ln0b, h0, h0b, lp)
    p, *gathered = _mm(h0b, win_t, "nt", "mm_in", bias=b_in_al[:, :P_U], out_dtype=BF16, b_rows=(0, P_U),
                       comm=(_Gather, later))
    puif = _mm(h0b, win_t, "nt", "mm_in_uif", bias=b_in_al[:, P_U:], b_rows=(P_U, PW - P_U))
    wglu_f = _cols_from_shards(gathered[0])
    wmo_f = gathered[1].reshape(1024, 1024)
    wo_f = gathered[2].reshape(1024, 1024)
    (y_pre, gy, cin), gathered2 = _s5_fwd(puif, bblk, cblk, lam8, tab_f, s5_d, lp, comm=(_Gather, latest))
    wup_f = _cols_from_shards(gathered2[0])
    wdown_f = gathered2[1].reshape(D_FF, 1024)
    z = _mm(gy, wglu_f, "nn", "mm_glu", out_dtype=BF16)
    qk = _conv_fwd(p, convw_f, qk_conv_b, lp)
    hm, cst, nm = _mlstm_fwd(p, puif, qk, lp)
    a_m = _headnorm_fwd(hm, p, m_norm_g)
    ym = _mm(a_m, wmo_f, "nn", "mm_mout", out_dtype=BF16)
    mix = _mix_fwd(z, ym, p)
    r1 = _mm(mix, wo_f, "nn", "mm_o")
    pre1, h1, h1b = _ln_res_fwd(h0, r1, ln1_g, ln1_b, "ln1_fwd")
    act = _mm(h1b, wup_f, "nn", "mm_up", bias=b_up, relu2_out=True, out_dtype=BF16)
    ff = _mm(act, wdown_f, "nn", "mm_down")
    loss_acc, dpre2, dpre2b, dg2, db2 = _ln2_loss(h1, ff, ln2_g, ln2_b, loss_target)

    d_up = _mm(dpre2b, wdown_f, "nt", "mm_d_act", sqrt_gate_of=act, out_dtype=BF16)
    g_wdown = _mm(act, dpre2b, "tn", "mm_g_wdown", out_dtype=BF16)
    g_wup, cs_up = _mm(h1b, d_up, "tn", "mm_g_wup", colsum="b", out_dtype=BF16)
    dh1 = _mm(d_up, wup_f, "nt", "mm_d_h1", add=dpre2, add_scale=ALPHA)
    dpre1, dpre1b, dg1, db1 = _ln_bwd(dh1, pre1, ln1_g, "ln1_bwd")
    g_wo = _mm(mix, dpre1b, "tn", "mm_g_wo", out_dtype=BF16)
    dmix = _mm(dpre1b, wo_f, "nt", "mm_d_mix", out_dtype=BF16)
    dz, dym, dp = _mix_bwd(dmix, z, ym, p)
    g_wmo = _mm(a_m, dym, "tn", "mm_g_wmo", out_dtype=BF16)
    da = _mm(dym, wmo_f, "nt", "mm_d_a", out_dtype=BF16)
    dhm, dp, dg_norm = _headnorm_bwd(da, hm, p, m_norm_g, dp)
    early = [g_wdown.reshape(N_DEV, 512, 1024), _cols_to_shards(g_wup), g_wo.reshape(N_DEV, 128, 1024),
             g_wmo.reshape(N_DEV, 128, 1024)]
    dqk, dp, dif, recv_early = _mlstm_bwd(p, puif, qk, cst, nm, dhm, dp, lp, comm=(_AllToAll, early))
    dp, dconv_w, dconv_b = _conv_bwd(p, dqk, convw_f, qk_conv_b, dp, lp)
    g_wglu = _mm(gy, dz, "tn", "mm_g_wglu", out_dtype=BF16)
    dgy = _mm(dz, wglu_f, "nt", "mm_d_gy")
    (dp, dbblk, dcblk, dlam, dd), recv_glu = _s5_bwd(puif, y_pre, dgy, cin, bblk, cblk, lam8, tab_f, tab_b, s5_d, dif, dp,
                                                     lp, comm=(_AllToAll, [_cols_to_shards(g_wglu)]))
    g_win_t, cs_in = _mm(dp, h0b, "tn", "mm_g_win", colsum="a")
    g_win8 = _aligned_to_nat(g_win_t, 0).astype(BF16).reshape(N_DEV, IN_NAT // N_DEV, D)
    dh0, recv_win = _mm(dp, win_t, "nn", "mm_d_h0", add=dpre1, add_scale=ALPHA, comm=(_AllToAll, [g_win8]))
    grad_x, dmeta, dg0, db0 = _ln0_bwd(dh0, x, meta_f, ln0g)

    dlam2 = jnp.sum(dlam, axis=1)
    s5_grads = s5_vjp((dlam2[0:1], dlam2[1:2], dbblk, dcblk))
    small_local = [
        loss_acc[0:1, 0:1], dg0[0:1], db0[0:1], _aligned_to_nat(cs_in[0:1], 1), dconv_b[0:1],
        s5_grads[0], s5_grads[1], s5_grads[2], s5_grads[3], s5_grads[4], s5_grads[5], s5_grads[6],
        dd[0:1], dg_norm[0:1], dg1[0:1], db1[0:1], cs_up[0:1], dg2[0:1], db2[0:1],
        dmeta, dconv_w[0:4]]
    small_shapes = [(), (D,), (D,), (1, IN_NAT), (1, 1024),
                    (1, 32, 64), (1, 32, 64), (1, 32), (1, 32, 64, 16), (1, 32, 64, 16), (1, 32, 16, 64), (1, 32, 16, 64),
                    (1, 512), (1, 1024), (1, 1024), (1, 1024), (1, D_FF), (1, 1024), (1, 1024),
                    (N_META, D), (4, 1024)]
    red = _unpack(_allreduce_small(_pack(small_local)), small_shapes)
    loss = red[0]
    g_meta = lax.dynamic_slice_in_dim(red[19], me * 128, 128, axis=1)
    g_convw = lax.dynamic_slice_in_dim(red[20], me * 128, 128, axis=1)[None]
    small_g = red[1:19] + [g_meta, g_convw]
    small_w = [ln0_g, ln0_b, b_in, qk_conv_b, s5_lambda_re, s5_lambda_im, s5_log_dt, s5_b_re, s5_b_im,
               s5_c_re, s5_c_im, s5_d, m_norm_g, ln1_g, ln1_b, b_up, ln2_g, ln2_b, meta_tokens, qk_conv_w]
    small_m = [m_ln0_g, m_ln0_b, m_b_in, m_qk_conv_b, m_s5_lambda_re, m_s5_lambda_im, m_s5_log_dt, m_s5_b_re,
               m_s5_b_im, m_s5_c_re, m_s5_c_im, m_s5_d, m_m_norm_g, m_ln1_g, m_ln1_b, m_b_up, m_ln2_g, m_ln2_b,
               m_meta_tokens, m_qk_conv_w]
    small_v = [v_ln0_g, v_ln0_b, v_b_in, v_qk_conv_b, v_s5_lambda_re, v_s5_lambda_im, v_s5_log_dt, v_s5_b_re,
               v_s5_b_im, v_s5_c_re, v_s5_c_im, v_s5_d, v_m_norm_g, v_ln1_g, v_ln1_b, v_b_up, v_ln2_g, v_ln2_b,
               v_meta_tokens, v_qk_conv_w]
    shapes_w = [tuple(w.shape) for w in small_w]
    small_g = [g.reshape(s) for g, s in zip(small_g, shapes_w)]
    sd, sm2, sv2 = _adamw_small(small_g, small_w, small_m, small_v)

    names = ["w_in", "s5_w_glu", "m_w_out", "w_o", "w_up", "w_down"]
    recv = [recv_win, recv_glu[0], recv_early[3], recv_early[2], recv_early[1], recv_early[0]]
    big_m = [m_w_in[0].T, m_s5_w_glu[0], m_m_w_out[0], m_w_o[0], m_w_up[0], m_w_down[0]]
    big_v = [v_w_in[0].T, v_s5_w_glu[0], v_m_w_out[0], v_w_o[0], v_w_up[0], v_w_down[0]]
    big_w = [w_in[0].T, s5_w_glu[0], m_w_out[0], w_o[0], w_up[0], w_down[0]]
    big_out = [_adamw_big(r, w, m, v, "adamw_" + nm_) for r, w, m, v, nm_ in zip(recv, big_w, big_m, big_v, names)]
    big_out[0] = [o.T for o in big_out[0]]

    order = ["meta_tokens", "ln0_g", "ln0_b", "w_in", "b_in", "qk_conv_w", "qk_conv_b", "s5_lambda_re", "s5_lambda_im",
             "s5_log_dt", "s5_b_re", "s5_b_im", "s5_c_re", "s5_c_im", "s5_d", "s5_w_glu", "m_norm_g", "m_w_out", "w_o",
             "ln1_g", "ln1_b", "w_up", "b_up", "w_down", "ln2_g", "ln2_b"]
    small_names = ["ln0_g", "ln0_b", "b_in", "qk_conv_b", "s5_lambda_re", "s5_lambda_im", "s5_log_dt", "s5_b_re",
                   "s5_b_im", "s5_c_re", "s5_c_im", "s5_d", "m_norm_g", "ln1_g", "ln1_b", "b_up", "ln2_g", "ln2_b",
                   "meta_tokens", "qk_conv_w"]
    res = {}
    for i, n in enumerate(small_names):
        res[n] = (small_g[i], sd[i], sm2[i], sv2[i])
    for i, n in enumerate(names):
        res[n] = tuple(o[None] for o in big_out[i])
    outs = [loss, grad_x]
    for kind in range(4):
        outs += [res[n][kind] for n in order]
    return tuple(outs)
```

```python
import functools
import math

import jax
import jax.numpy as jnp
from jax import lax
from jax.experimental import pallas as pl
from jax.experimental.pallas import tpu as pltpu

F32 = jnp.float32
BF16 = jnp.bfloat16

D_MODEL = 1024
N_META = 16
CHUNK = 128
PAD_ROWS = CHUNK - N_META
S5_WIDTH = 512
S5_GROUP = 16
S5_GROUPS = 32
S5_STATE = 64
S5_COLS = 2 * S5_GROUPS * S5_STATE
S5_BLK = 4
M_HEADS = 4
M_DK = 128
M_DV = 256
D_FF = 4096
N_DEV = 8
ALPHA = 2.0 ** 0.25
LN_EPS = 1e-5
IN_NAT = 5640
P_V, P_O, P_GS, P_GM, P_QK, P_U, P_IF, PW = 0, 1024, 2048, 3072, 4096, 5120, 5632, 5760
N_U, N_Q, N_K, N_V, N_O, N_I, N_GS, N_GM = 0, 512, 1024, 1536, 2560, 3584, 3592, 4616

ADAM_LR, ADAM_B1, ADAM_B2, ADAM_EPS, ADAM_WD, ADAM_STEP = 0.001, 0.9, 0.999, 1e-08, 0.01, 10

VMEM_LIMIT = 56 * 1024 * 1024
MXU_WIDTH = 256
MESH = pl.DeviceIdType.MESH


def _pick(n, cands):
    for c in cands:
        if n % c == 0:
            return c
    raise ValueError(f"no tile for {n} among {cands}")


def _cparams(sem):
    return pltpu.CompilerParams(dimension_semantics=sem, vmem_limit_bytes=VMEM_LIMIT)


def _dot(a, b, ca, cb):
    return lax.dot_general(a, b, (((ca,), (cb,)), ((), ())), preferred_element_type=F32)


def _sigmoid(x):
    return 1.0 / (1.0 + jnp.exp(-x))


def _peer(k):
    x, y, c = lax.axis_index("x"), lax.axis_index("y"), lax.axis_index("c")
    px = 1 - x if k & 4 else x
    py = 1 - y if k & 2 else y
    pc = 1 - c if k & 1 else c
    return (px, py, pc), 4 * px + 2 * py + pc


def _my_id():
    return 4 * lax.axis_index("x") + 2 * lax.axis_index("y") + lax.axis_index("c")


def _hbm_call(body, name, arrs, out_shape, n_remote):
    n = len(arrs)
    return pl.pallas_call(
        body, name=name,
        out_shape=tuple(out_shape),
        in_specs=[pl.BlockSpec(memory_space=pl.ANY)] * n,
        out_specs=tuple([pl.BlockSpec(memory_space=pl.ANY)] * len(out_shape)),
        scratch_shapes=[pltpu.SemaphoreType.DMA((n, n_remote)),
                        pltpu.SemaphoreType.DMA((n, n_remote)),
                        pltpu.SemaphoreType.DMA((n,))],
    )(*arrs)


class _Gather:
    def __init__(self, ins, outs, send_sems, recv_sems, local_sems):
        self.ins, self.outs, self.n = ins, outs, len(ins)
        self.send_sems, self.recv_sems, self.local_sems = send_sems, recv_sems, local_sems
        x, y, c = lax.axis_index("x"), lax.axis_index("y"), lax.axis_index("c")
        self.c = c
        self.me, self.sibling = (x, y, c), (x, y, 1 - c)
        self.chips = [(1 - x, y), (x, 1 - y), (1 - x, 1 - y)]

    def slot(self, a, dev):
        return self.outs[a].at[4 * dev[0] + 2 * dev[1] + dev[2]]

    def copy(self, a, k, block, to, own=False):
        return pltpu.make_async_remote_copy(
            src_ref=self.ins[a] if own else self.slot(a, block), dst_ref=self.slot(a, block),
            send_sem=self.send_sems.at[a, k], recv_sem=self.recv_sems.at[a, k],
            device_id=to, device_id_type=MESH)

    def first_sends(self, a):
        return [self.copy(a, 0, self.me, self.sibling, own=True)] + [
            self.copy(a, 1 + j, self.me, (*chip, self.c), own=True) for j, chip in enumerate(self.chips)]

    def start(self):
        for a in range(self.n):
            pltpu.make_async_copy(self.ins[a], self.slot(a, self.me), self.local_sems.at[a]).start()
            for cp in self.first_sends(a):
                cp.start()

    def finish(self):
        forwards = []
        for j, chip in enumerate(self.chips):
            for a in range(self.n):
                self.copy(a, 1 + j, (*chip, self.c), self.me).wait_recv()
                fwd = self.copy(a, 4 + j, (*chip, self.c), self.sibling)
                fwd.start()
                forwards.append(fwd)
        for a in range(self.n):
            self.copy(a, 0, self.sibling, self.me).wait_recv()
            for j, chip in enumerate(self.chips):
                self.copy(a, 4 + j, (*chip, 1 - self.c), self.me).wait_recv()
        for a in range(self.n):
            for cp in self.first_sends(a):
                cp.wait_send()
            pltpu.make_async_copy(self.ins[a], self.slot(a, self.me), self.local_sems.at[a]).wait()
        for cp in forwards:
            cp.wait_send()

    @staticmethod
    def out_shapes(arrs):
        return [jax.ShapeDtypeStruct((N_DEV,) + tuple(a.shape), a.dtype) for a in arrs]


class _AllToAll:
    def __init__(self, ins, outs, send_sems, recv_sems, local_sems):
        self.ins, self.outs, self.n = ins, outs, len(ins)
        self.send_sems, self.recv_sems, self.local_sems = send_sems, recv_sems, local_sems
        self.me = _my_id()

    def copy(self, a, k, landing):
        peer, pid = _peer(k)
        return pltpu.make_async_remote_copy(
            src_ref=self.ins[a].at[pid], dst_ref=self.outs[a].at[pid if landing else self.me],
            send_sem=self.send_sems.at[a, k - 1], recv_sem=self.recv_sems.at[a, k - 1],
            device_id=peer, device_id_type=MESH)

    def local(self, a):
        return pltpu.make_async_copy(self.ins[a].at[self.me], self.outs[a].at[self.me], self.local_sems.at[a])

    def start(self):
        for a in range(self.n):
            self.local(a).start()
            for k in range(1, N_DEV):
                self.copy(a, k, False).start()

    def finish(self):
        for a in range(self.n):
            for k in range(1, N_DEV):
                self.copy(a, k, True).wait_recv()
        for a in range(self.n):
            for k in range(1, N_DEV):
                self.copy(a, k, False).wait_send()
            self.local(a).wait()

    @staticmethod
    def out_shapes(arrs):
        return [jax.ShapeDtypeStruct(tuple(a.shape), a.dtype) for a in arrs]


def _comm_scratch(n):
    return [pltpu.SemaphoreType.DMA((n, N_DEV - 1)), pltpu.SemaphoreType.DMA((n, N_DEV - 1)),
            pltpu.SemaphoreType.DMA((n,))]


def _ride_call(body, name, grid, in_specs, out_specs, out_shape, scratch, args, comm=None, aliases=None):
    n_in, n_out = len(in_specs), len(out_specs)
    in_specs, out_specs, out_shape = list(in_specs), list(out_specs), list(out_shape)
    scratch, args = list(scratch), list(args)
    kernel_fn = body
    if comm is not None:
        cls, arrs = comm
        n = len(arrs)

        def kernel_fn(*refs):
            ins, cin = refs[:n_in], refs[n_in:n_in + n]
            outs = refs[n_in + n:n_in + n + n_out]
            cout = refs[n_in + n + n_out:n_in + 2 * n + n_out]
            own_scratch, sems = refs[n_in + 2 * n + n_out:-3], refs[-3:]
            exchange = cls(cin, cout, *sems)
            ids = [pl.program_id(d) for d in range(len(grid))]
            first = functools.reduce(lambda a, b: a & b, [i == 0 for i in ids])
            last = functools.reduce(lambda a, b: a & b, [i == g - 1 for i, g in zip(ids, grid)])

            @pl.when(first)
            def _():
                exchange.start()

            body(*ins, *outs, *own_scratch)

            @pl.when(last)
            def _():
                exchange.finish()

        in_specs += [pl.BlockSpec(memory_space=pl.ANY)] * n
        args += list(arrs)
        out_specs += [pl.BlockSpec(memory_space=pl.ANY)] * n
        out_shape += cls.out_shapes(arrs)
        scratch += _comm_scratch(n)
    res = pl.pallas_call(
        kernel_fn, name=name, grid=grid,
        in_specs=in_specs, out_specs=tuple(out_specs), out_shape=tuple(out_shape),
        scratch_shapes=scratch, input_output_aliases=aliases or {},
        compiler_params=_cparams(("arbitrary",) * len(grid)),
    )(*args)
    return list(res[:n_out]), list(res[n_out:])


def _shard_tile(R, C):
    if R % 128 == 0:
        return 128, C
    return R, _pick(C, (256, 128))


def _allreduce_small(pack):
    rows = pack.shape[0]

    def gather(*refs):
        g = _Gather(refs[:1], refs[1:2], *refs[2:])
        g.start()
        g.finish()

    slots = _hbm_call(gather, "allreduce_gather", [pack], _Gather.out_shapes([pack]), N_DEV - 1)[0]

    def body(s_ref, o_ref):
        acc = s_ref[0]
        for s in range(1, N_DEV):
            acc = acc + s_ref[s]
        o_ref[...] = acc

    return pl.pallas_call(
        body, name="allreduce_sum", grid=(1,),
        in_specs=[pl.BlockSpec((N_DEV, rows, 128), lambda i: (0, 0, 0))],
        out_specs=pl.BlockSpec((rows, 128), lambda i: (0, 0)),
        out_shape=jax.ShapeDtypeStruct((rows, 128), F32),
        compiler_params=_cparams(("arbitrary",)),
    )(slots)


def _mm_tiles(M, N, K, mode, out_bytes, n_extra_f32, a_bytes, b_bytes):
    budget = 40 * 1024 * 1024
    tms = [t for t in (1664, 1408, 1152, 1024, 640, 512, 256, 128) if M % t == 0]
    tns = [t for t in (1152, 1024, 640, 512, 384, 256, 128) if N % t == 0]
    if mode == "tn":
        tks = [t for t in (1664, 1408, 640, 512, 256, 128) if K % t == 0]
    else:
        tks = [K] if K <= 1152 else [t for t in (1152, 1024, 640, 512) if K % t == 0]
    best = None
    for tm in tms:
        for tn in tns:
            for tk in tks:
                nk = K // tk
                use = 2 * (tm * tk * a_bytes + tk * tn * b_bytes) + 2 * tm * tn * out_bytes
                use += 2 * n_extra_f32 * tm * tn * 4 + tm * tn * 4 * (2 if nk > 1 else 1)
                if use > budget:
                    continue
                score = (tm * tn * tk, tm * tn)
                if best is None or score > best[0]:
                    best = (score, (tm, tn, tk))
    assert best is not None, (M, N, K, mode)
    return best[1]


def _mm(a, b, mode, name, *, bias=None, add=None, add_scale=1.0, sqrt_gate_of=None,
        relu2_out=False, colsum=None, out_dtype=F32, comm=None, b_rows=None):
    if mode == "nn":
        (M, K), (K2, N) = a.shape, b.shape
    elif mode == "nt":
        (M, K), (N, K2) = a.shape, b.shape
        if b_rows is not None:
            N = b_rows[1]
    else:
        (K, M), (K2, N) = a.shape, b.shape
    assert K == K2, (a.shape, b.shape, mode)
    assert b_rows is None or mode == "nt"
    has_bias, has_add, has_gate = bias is not None, add is not None, sqrt_gate_of is not None
    tm, tn, tk = _mm_tiles(M, N, K, mode, jnp.dtype(out_dtype).itemsize, int(has_add) + int(has_gate),
                           a.dtype.itemsize, b.dtype.itemsize)
    nk = K // tk
    assert colsum is None or mode == "tn"
    assert colsum != "a" or N == tn
    comm_cls, comm_arrs = comm if comm is not None else (None, [])
    nc = len(comm_arrs)
    grid = (N // tn, M // tm, nk)

    def body(*refs):
        it = iter(refs)
        a_ref, b_ref = next(it), next(it)
        bias_ref = next(it) if has_bias else None
        add_ref = next(it) if has_add else None
        gate_ref = next(it) if has_gate else None
        comm_ins = [next(it) for _ in range(nc)]
        o_ref = next(it)
        cs_ref = next(it) if colsum else None
        comm_outs = [next(it) for _ in range(nc)]
        acc_ref = next(it) if nk > 1 else None
        j, i, k = pl.program_id(0), pl.program_id(1), pl.program_id(2)
        if nc:
            exchange = comm_cls(comm_ins, comm_outs, next(it), next(it), next(it))

            @pl.when((j == 0) & (i == 0) & (k == 0))
            def _():
                exchange.start()

        blocks = [slice(n0, min(n0 + MXU_WIDTH, tn)) for n0 in range(0, tn, MXU_WIDTH)]
        av = a_ref[...].astype(BF16)

        def part(cols):
            if mode == "nn":
                return _dot(av, b_ref[:, cols].astype(BF16), 1, 0)
            if mode == "nt":
                return _dot(av, b_ref[cols, :].astype(BF16), 1, 1)
            return _dot(av, b_ref[:, cols].astype(BF16), 0, 0)

        if colsum == "b":
            @pl.when((i == 0) & (k == 0))
            def _():
                cs_ref[...] = jnp.zeros_like(cs_ref)

            @pl.when(i == 0)
            def _():
                cs_ref[0:1, :] += jnp.sum(b_ref[...].astype(F32), axis=0, keepdims=True)
        if colsum == "a":
            @pl.when(k == 0)
            def _():
                cs_ref[...] = jnp.zeros_like(cs_ref)

            cs_ref[0:1, :] += jnp.sum(a_ref[...].astype(F32), axis=0, keepdims=True)

        def finish(cols, r):
            if has_bias:
                r = r + bias_ref[:, cols]
            if has_add:
                r = r + add_scale * add_ref[:, cols]
            if has_gate:
                r = r * (2.0 * jnp.sqrt(gate_ref[:, cols].astype(F32)))
            if relu2_out:
                r = jnp.square(jnp.maximum(r, 0.0))
            o_ref[:, cols] = r.astype(out_dtype)

        if nk == 1:
            for cols in blocks:
                finish(cols, part(cols))
        else:
            @pl.when(k == 0)
            def _():
                for cols in blocks:
                    acc_ref[:, cols] = part(cols)

            @pl.when((k > 0) & (k < nk - 1))
            def _():
                for cols in blocks:
                    acc_ref[:, cols] += part(cols)

            @pl.when(k == nk - 1)
            def _():
                for cols in blocks:
                    finish(cols, acc_ref[:, cols] + part(cols))

        if nc:
            @pl.when((j == grid[0] - 1) & (i == grid[1] - 1) & (k == nk - 1))
            def _():
                exchange.finish()

    if mode == "nn":
        a_spec = pl.BlockSpec((tm, tk), lambda j, i, k: (i, k))
        b_spec = pl.BlockSpec((tk, tn), lambda j, i, k: (k, j))
    elif mode == "nt":
        a_spec = pl.BlockSpec((tm, tk), lambda j, i, k: (i, k))
        b_off = 0 if b_rows is None else b_rows[0] // tn
        assert b_rows is None or b_rows[0] % tn == 0
        b_spec = pl.BlockSpec((tn, tk), lambda j, i, k: (j + b_off, k))
    else:
        a_spec = pl.BlockSpec((tk, tm), lambda j, i, k: (k, i))
        b_spec = pl.BlockSpec((tk, tn), lambda j, i, k: (k, j))
    in_specs, args = [a_spec, b_spec], [a, b]
    if has_bias:
        in_specs.append(pl.BlockSpec((1, tn), lambda j, i, k: (0, j)))
        args.append(bias)
    if has_add:
        in_specs.append(pl.BlockSpec((tm, tn), lambda j, i, k: (i, j)))
        args.append(add)
    if has_gate:
        in_specs.append(pl.BlockSpec((tm, tn), lambda j, i, k: (i, j)))
        args.append(sqrt_gate_of)
    out_shape = [jax.ShapeDtypeStruct((M, N), out_dtype)]
    out_specs = [pl.BlockSpec((tm, tn), lambda j, i, k: (i, j))]
    if colsum == "b":
        out_shape.append(jax.ShapeDtypeStruct((8, N), F32))
        out_specs.append(pl.BlockSpec((8, tn), lambda j, i, k: (0, j)))
    if colsum == "a":
        out_shape.append(jax.ShapeDtypeStruct((8, M), F32))
        out_specs.append(pl.BlockSpec((8, tm), lambda j, i, k: (0, i)))
    scratch = [pltpu.VMEM((tm, tn), F32)] if nk > 1 else []
    if nc:
        in_specs += [pl.BlockSpec(memory_space=pl.ANY)] * nc
        args += list(comm_arrs)
        out_specs += [pl.BlockSpec(memory_space=pl.ANY)] * nc
        out_shape += comm_cls.out_shapes(comm_arrs)
        scratch += _comm_scratch(nc)
    res = pl.pallas_call(
        body, name=name, grid=grid,
        in_specs=in_specs, out_specs=tuple(out_specs), out_shape=tuple(out_shape),
        scratch_shapes=scratch,
        compiler_params=_cparams(("arbitrary", "arbitrary", "arbitrary")),
    )(*args)
    return res if len(res) > 1 else res[0]


def _ln_rows(v, g, b):
    mu = jnp.mean(v, axis=-1, keepdims=True)
    xc = v - mu
    var = jnp.mean(xc * xc, axis=-1, keepdims=True)
    return xc * lax.rsqrt(var + LN_EPS) * g + b


def _ln_bwd_rows(dy, v, g):
    mu = jnp.mean(v, axis=-1, keepdims=True)
    xc = v - mu
    var = jnp.mean(xc * xc, axis=-1, keepdims=True)
    rstd = lax.rsqrt(var + LN_EPS)
    xhat = xc * rstd
    dxh = dy * g
    dv = rstd * (dxh - jnp.mean(dxh, axis=-1, keepdims=True)
                 - xhat * jnp.mean(dxh * xhat, axis=-1, keepdims=True))
    return dv, xhat


def _real_tile(S):
    return _pick(S, (512, 256, 128, 64))


def _real_rows(rb, ncols, lp):
    return pl.BlockSpec((pl.Element(rb), pl.Element(ncols)),
                        lambda bb, j: (pl.multiple_of(bb * lp + CHUNK + j * rb, CHUNK), 0))


def _head_rows(ncols, lp):
    return pl.BlockSpec((CHUNK, ncols), lambda bb: (bb * (lp // CHUNK), 0))


def _ln0_fwd(x, g, b, comm=None):
    B, S, D = x.shape
    lp = S + CHUNK
    rb = _real_tile(S)

    def body(x_ref, g_ref, b_ref, h_ref, hb_ref):
        y = _ln_rows(x_ref[0], g_ref[...], b_ref[...])
        h_ref[...] = y
        hb_ref[...] = y.astype(BF16)

    vec = pl.BlockSpec((1, D), lambda bb, j: (0, 0))
    return _ride_call(
        body, "ln0_fwd", (B, S // rb),
        [pl.BlockSpec((1, rb, D), lambda bb, j: (bb, j, 0)), vec, vec],
        [_real_rows(rb, D, lp), _real_rows(rb, D, lp)],
        [jax.ShapeDtypeStruct((B * lp, D), F32), jax.ShapeDtypeStruct((B * lp, D), BF16)],
        [], (x, g, b), comm=comm)


def _ln0_head(meta, g, b, h, hb, lp):
    D = meta.shape[1]
    B = h.shape[0] // lp

    def head(meta_ref, g_ref, b_ref, hin_ref, hbin_ref, h_ref, hb_ref):
        m = _ln_rows(meta_ref[...], g_ref[...], b_ref[...])
        h_ref[0:PAD_ROWS, :] = jnp.zeros((PAD_ROWS, D), F32)
        h_ref[PAD_ROWS:CHUNK, :] = m
        hb_ref[0:PAD_ROWS, :] = jnp.zeros((PAD_ROWS, D), BF16)
        hb_ref[PAD_ROWS:CHUNK, :] = m.astype(BF16)

    vec1 = pl.BlockSpec((1, D), lambda bb: (0, 0))
    anyspec = pl.BlockSpec(memory_space=pl.ANY)
    return pl.pallas_call(
        head, name="ln0_head", grid=(B,),
        in_specs=[pl.BlockSpec((N_META, D), lambda bb: (0, 0)), vec1, vec1, anyspec, anyspec],
        out_specs=(_head_rows(D, lp), _head_rows(D, lp)),
        out_shape=(jax.ShapeDtypeStruct((B * lp, D), F32), jax.ShapeDtypeStruct((B * lp, D), BF16)),
        input_output_aliases={3: 0, 4: 1},
        compiler_params=_cparams(("arbitrary",)),
    )(meta, g, b, h, hb)


def _ln0_bwd(dh0, x, meta, g):
    B, S, D = x.shape
    lp = S + CHUNK
    rb = _real_tile(S)

    def body(dh_ref, x_ref, g_ref, dx_ref, dg_ref, db_ref):
        @pl.when((pl.program_id(0) == 0) & (pl.program_id(1) == 0))
        def _():
            dg_ref[...] = jnp.zeros_like(dg_ref)
            db_ref[...] = jnp.zeros_like(db_ref)

        dy = dh_ref[...]
        dv, xhat = _ln_bwd_rows(dy, x_ref[0], g_ref[...])
        dx_ref[0] = dv
        dg_ref[0:1, :] += jnp.sum(dy * xhat, axis=0, keepdims=True)
        db_ref[0:1, :] += jnp.sum(dy, axis=0, keepdims=True)

    const = lambda bb, j: (0, 0)
    xblk = pl.BlockSpec((1, rb, D), lambda bb, j: (bb, j, 0))
    acc_shape = jax.ShapeDtypeStruct((8, D), F32)
    dx, dg, db = pl.pallas_call(
        body, name="ln0_bwd", grid=(B, S // rb),
        in_specs=[_real_rows(rb, D, lp), xblk, pl.BlockSpec((1, D), const)],
        out_specs=(xblk, pl.BlockSpec((8, D), const), pl.BlockSpec((8, D), const)),
        out_shape=(jax.ShapeDtypeStruct((B, S, D), F32), acc_shape, acc_shape),
        compiler_params=_cparams(("arbitrary", "arbitrary")),
    )(dh0, x, g)

    def head(dh_ref, meta_ref, g_ref, dmeta_ref, dg_ref, db_ref):
        @pl.when(pl.program_id(0) == 0)
        def _():
            dmeta_ref[...] = jnp.zeros_like(dmeta_ref)
            dg_ref[...] = jnp.zeros_like(dg_ref)
            db_ref[...] = jnp.zeros_like(db_ref)

        dy = dh_ref[PAD_ROWS:CHUNK, :]
        dv, xhat = _ln_bwd_rows(dy, meta_ref[...], g_ref[...])
        dmeta_ref[...] += dv
        dg_ref[0:1, :] += jnp.sum(dy * xhat, axis=0, keepdims=True)
        db_ref[0:1, :] += jnp.sum(dy, axis=0, keepdims=True)

    c1 = lambda bb: (0, 0)
    dmeta, dgm, dbm = pl.pallas_call(
        head, name="ln0_bwd_head", grid=(B,),
        in_specs=[_head_rows(D, lp), pl.BlockSpec((N_META, D), c1), pl.BlockSpec((1, D), c1)],
        out_specs=(pl.BlockSpec((N_META, D), c1), pl.BlockSpec((8, D), c1), pl.BlockSpec((8, D), c1)),
        out_shape=(jax.ShapeDtypeStruct((N_META, D), F32), acc_shape, acc_shape),
        compiler_params=_cparams(("arbitrary",)),
    )(dh0, meta, g)
    return dx, dmeta, dg + dgm, db + dbm


def _ln_res_fwd(h_prev, r, g, b, name):
    T, D = h_prev.shape
    tr = _pick(T, (384, 320, 256, 128, 64))

    def body(hp_ref, r_ref, g_ref, b_ref, pre_ref, h_ref, hb_ref):
        pre = ALPHA * hp_ref[...] + r_ref[...]
        y = _ln_rows(pre, g_ref[...], b_ref[...])
        pre_ref[...] = pre
        h_ref[...] = y
        hb_ref[...] = y.astype(BF16)

    row = pl.BlockSpec((tr, D), lambda i: (i, 0))
    vec = pl.BlockSpec((1, D), lambda i: (0, 0))
    return pl.pallas_call(
        body, name=name, grid=(T // tr,),
        in_specs=[row, row, vec, vec], out_specs=(row, row, row),
        out_shape=(jax.ShapeDtypeStruct((T, D), F32), jax.ShapeDtypeStruct((T, D), F32),
                   jax.ShapeDtypeStruct((T, D), BF16)),
        compiler_params=_cparams(("arbitrary",)),
    )(h_prev, r, g, b)


def _ln_bwd(dh, pre, g, name):
    T, D = dh.shape
    tr = _pick(T, (384, 320, 256, 128, 64))

    def body(dh_ref, pre_ref, g_ref, dp_ref, dpb_ref, dg_ref, db_ref):
        @pl.when(pl.program_id(0) == 0)
        def _():
            dg_ref[...] = jnp.zeros_like(dg_ref)
            db_ref[...] = jnp.zeros_like(db_ref)

        dy = dh_ref[...]
        dv, xhat = _ln_bwd_rows(dy, pre_ref[...], g_ref[...])
        dp_ref[...] = dv
        dpb_ref[...] = dv.astype(BF16)
        dg_ref[0:1, :] += jnp.sum(dy * xhat, axis=0, keepdims=True)
        db_ref[0:1, :] += jnp.sum(dy, axis=0, keepdims=True)

    row = pl.BlockSpec((tr, D), lambda i: (i, 0))
    vec = pl.BlockSpec((1, D), lambda i: (0, 0))
    acc = pl.BlockSpec((8, D), lambda i: (0, 0))
    return pl.pallas_call(
        body, name=name, grid=(T // tr,),
        in_specs=[row, row, vec], out_specs=(row, row, acc, acc),
        out_shape=(jax.ShapeDtypeStruct((T, D), F32), jax.ShapeDtypeStruct((T, D), BF16),
                   jax.ShapeDtypeStruct((8, D), F32), jax.ShapeDtypeStruct((8, D), F32)),
        compiler_params=_cparams(("arbitrary",)),
    )(dh, pre, g)


def _ln2_loss(h1, ff, g, b, target):
    T, D = h1.shape
    B, S, _ = target.shape
    lp = S + CHUNK
    rb = _real_tile(S)

    def body(h_ref, ff_ref, g_ref, b_ref, t_ref, loss_ref, dp_ref, dpb_ref, dg_ref, db_ref):
        @pl.when((pl.program_id(0) == 0) & (pl.program_id(1) == 0))
        def _():
            loss_ref[...] = jnp.zeros_like(loss_ref)
            dg_ref[...] = jnp.zeros_like(dg_ref)
            db_ref[...] = jnp.zeros_like(db_ref)

        pre = ALPHA * h_ref[...] + ff_ref[...]
        gg = g_ref[...]
        y = _ln_rows(pre, gg, b_ref[...])
        err = y - t_ref[0]
        loss_ref[0:1, 0:1] += 0.5 * jnp.sum(jnp.mean(err * err, axis=-1, keepdims=True), axis=0, keepdims=True)
        dy = err * (1.0 / D)
        dv, xhat = _ln_bwd_rows(dy, pre, gg)
        dp_ref[...] = dv
        dpb_ref[...] = dv.astype(BF16)
        dg_ref[0:1, :] += jnp.sum(dy * xhat, axis=0, keepdims=True)
        db_ref[0:1, :] += jnp.sum(dy, axis=0, keepdims=True)

    row = _real_rows(rb, D, lp)
    const = lambda bb, j: (0, 0)
    loss, dp, dpb, dg, db = pl.pallas_call(
        body, name="ln2_loss", grid=(B, S // rb),
        in_specs=[row, row, pl.BlockSpec((1, D), const), pl.BlockSpec((1, D), const),
                  pl.BlockSpec((1, rb, D), lambda bb, j: (bb, j, 0))],
        out_specs=(pl.BlockSpec((8, 128), const), row, row,
                   pl.BlockSpec((8, D), const), pl.BlockSpec((8, D), const)),
        out_shape=(jax.ShapeDtypeStruct((8, 128), F32),
                   jax.ShapeDtypeStruct((T, D), F32), jax.ShapeDtypeStruct((T, D), BF16),
                   jax.ShapeDtypeStruct((8, D), F32), jax.ShapeDtypeStruct((8, D), F32)),
        compiler_params=_cparams(("arbitrary", "arbitrary")),
    )(h1, ff, g, b, target)

    def head(dpin_ref, dpbin_ref, dp_ref, dpb_ref):
        dp_ref[...] = jnp.zeros((CHUNK, D), F32)
        dpb_ref[...] = jnp.zeros((CHUNK, D), BF16)

    anyspec = pl.BlockSpec(memory_space=pl.ANY)
    dp, dpb = pl.pallas_call(
        head, name="ln2_head", grid=(B,),
        in_specs=[anyspec, anyspec],
        out_specs=(_head_rows(D, lp), _head_rows(D, lp)),
        out_shape=(jax.ShapeDtypeStruct((T, D), F32), jax.ShapeDtypeStruct((T, D), BF16)),
        input_output_aliases={0: 0, 1: 1},
        compiler_params=_cparams(("arbitrary",)),
    )(dp, dpb)
    return loss, dp, dpb, dg, db


def _s5_prep(lam_re, lam_im, log_dt, b_re, b_im, c_re, c_im):
    dt = jnp.exp(log_dt)[:, None]
    mag = jnp.exp(lam_re * dt)
    ar = mag * jnp.cos(lam_im * dt)
    ai = mag * jnp.sin(lam_im * dt)
    nr, ni = ar - 1.0, ai
    den = lam_re * lam_re + lam_im * lam_im
    cr = (nr * lam_re + ni * lam_im) / den
    ci = (ni * lam_re - nr * lam_im) / den
    bbr = cr[..., None] * b_re - ci[..., None] * b_im
    bbi = cr[..., None] * b_im + ci[..., None] * b_re
    eye = jnp.eye(8, dtype=F32)
    bb = jnp.stack([bbr, bbi]).reshape(2, S5_BLK, 8, S5_STATE, S5_GROUP)
    bblk = jnp.einsum("rbgph,gj->bghrjp", bb, eye).reshape(S5_BLK, 128, 1024)
    cc = jnp.stack([c_re, -c_im]).reshape(2, S5_BLK, 8, S5_GROUP, S5_STATE)
    cblk = jnp.einsum("rbghp,gj->brjpgh", cc, eye).reshape(S5_BLK, 1024, 128)
    return ar.reshape(1, 2048), ai.reshape(1, 2048), bblk, cblk


def _scan_tables(ar, ai):
    pr, pi = [ar], [ai]
    for _ in range(7):
        pr, pi = pr + [pr[-1] * ar - pi[-1] * ai], pi + [pr[-1] * ai + pi[-1] * ar]
    pw_r = jnp.concatenate(pr, axis=0)
    pw_i = jnp.concatenate(pi, axis=0)
    rev_r = jnp.concatenate(pr[::-1], axis=0)
    rev_i = jnp.concatenate(pi[::-1], axis=0)
    row = jnp.arange(8)[:, None]

    def tables(sign, reverse):
        rows = []
        for n, sh in ((0, 1), (1, 2), (3, 4)):
            mask = (row < 8 - sh) if reverse else (row >= sh)
            rows.append(jnp.where(mask, pw_r[n][None, :], 0.0))
            rows.append(jnp.where(mask, sign * pw_i[n][None, :], 0.0))
        cr_ = rev_r if reverse else pw_r
        ci_ = rev_i if reverse else pw_i
        rows += [cr_, sign * ci_]
        return jnp.stack(rows)

    return tables(1.0, False), tables(-1.0, True)


def _seg_scan(s_ref, row0, seg, lam_ref, tab_ref, carry_ref, reverse, cseg_ref=None, extra=None):
    sgn = -1.0 if reverse else 1.0
    take, edge = (0, 7) if reverse else (7, 0)
    rowid = lax.broadcasted_iota(jnp.int32, (8, 128), 0)
    all_pairs = [(blk * 1024 + j * 128, blk * 1024 + j * 128 + 512, blk * 512 + j * 128)
                 for blk in range(S5_BLK) for j in range(4)]

    def rows(it):
        i = (seg - 1 - it) if reverse else it
        return i, pl.multiple_of(row0 + i * 8, 8)

    for half in range(2):
        pairs = all_pairs[8 * half:8 * half + 8]

        def pass1(it, carry):
            _, r0 = rows(it)
            out = []
            for n, (cre, cim, tc) in enumerate(pairs):
                lr = lam_ref[0, :, tc:tc + 128]
                li = sgn * lam_ref[1, :, tc:tc + 128]
                pr, pi = carry[2 * n], carry[2 * n + 1]
                xr = lr * pr - li * pi + s_ref[pl.ds(r0, 8), cre:cre + 128]
                xi = lr * pi + li * pr + s_ref[pl.ds(r0, 8), cim:cim + 128]
                s_ref[pl.ds(r0, 8), cre:cre + 128] = xr
                s_ref[pl.ds(r0, 8), cim:cim + 128] = xi
                out += [xr, xi]
            return tuple(out)

        ends = lax.fori_loop(0, seg, pass1, tuple(jnp.zeros((8, 128), F32) for _ in range(16)))

        start = []
        for n, (cre, cim, tc) in enumerate(pairs):
            xr, xi = ends[2 * n], ends[2 * n + 1]
            for lvl, sh in enumerate((1, 2, 4)):
                lr = tab_ref[2 * lvl, :, tc:tc + 128]
                li = tab_ref[2 * lvl + 1, :, tc:tc + 128]
                shift = (8 - sh) if reverse else sh
                sr = pltpu.roll(xr, shift, 0)
                si = pltpu.roll(xi, shift, 0)
                xr, xi = xr + lr * sr - li * si, xi + lr * si + li * sr
            pr = tab_ref[6, :, tc:tc + 128]
            pi = tab_ref[7, :, tc:tc + 128]
            c_r = carry_ref[:, cre:cre + 128]
            c_i = carry_ref[:, cim:cim + 128]
            er, ei = xr + pr * c_r - pi * c_i, xi + pr * c_i + pi * c_r
            back = 7 if reverse else 1
            in_r = jnp.where(rowid == edge, c_r, pltpu.roll(er, back, 0))
            in_i = jnp.where(rowid == edge, c_i, pltpu.roll(ei, back, 0))
            nr = jnp.sum(jnp.where(rowid == take, er, 0.0), axis=0, keepdims=True)
            ni = jnp.sum(jnp.where(rowid == take, ei, 0.0), axis=0, keepdims=True)
            carry_ref[:, cre:cre + 128] = jnp.broadcast_to(nr, (8, 128))
            carry_ref[:, cim:cim + 128] = jnp.broadcast_to(ni, (8, 128))
            if cseg_ref is not None:
                cseg_ref[0:8, cre:cre + 128] = in_r
                cseg_ref[0:8, cim:cim + 128] = in_i
            start += [in_r, in_i]

        def pass2(it, carry):
            i, r0 = rows(it)
            out = []
            for n, (cre, cim, tc) in enumerate(pairs):
                lr = lam_ref[0, :, tc:tc + 128]
                li = sgn * lam_ref[1, :, tc:tc + 128]
                dr, di = carry[2 * n], carry[2 * n + 1]
                dr, di = lr * dr - li * di, lr * di + li * dr
                xr = s_ref[pl.ds(r0, 8), cre:cre + 128] + dr
                xi = s_ref[pl.ds(r0, 8), cim:cim + 128] + di
                s_ref[pl.ds(r0, 8), cre:cre + 128] = xr
                s_ref[pl.ds(r0, 8), cim:cim + 128] = xi
                if extra is not None:
                    extra(i, cre, cim, tc, xr, xi)
                out += [dr, di]
            return tuple(out)

        lax.fori_loop(0, seg, pass2, tuple(start))


def _to_segments(src_refs, dst_ref, seg, first=None):
    sub = lax.broadcasted_iota(jnp.int32, (8, 1), 0)
    for i in range(seg):
        for c, src in enumerate(src_refs):
            v = src[pl.ds(i, 8, stride=seg), :]
            if first is not None:
                v = jnp.where(first & (sub * seg + i < PAD_ROWS), 0.0, v)
            dst_ref[8 * i:8 * i + 8, c * 128:(c + 1) * 128] = v


def _from_segments(src_ref, dst_ref, seg, fn=None, zero_head=None):
    m = seg // 8
    for j in range(seg):
        for c in range(src_ref.shape[0]):
            v = src_ref[c, pl.ds(64 * (j % m) + j // m, 8, stride=8), :]
            if zero_head is not None and j < PAD_ROWS // 8:
                v = jnp.where(zero_head, 0.0, v)
            dst_ref[8 * j:8 * j + 8, c * 128:(c + 1) * 128] = v if fn is None else fn(v)


def _gelu(x):
    c = math.sqrt(2.0 / math.pi)
    t = jnp.tanh(c * (x + 0.044715 * x * x * x))
    return 0.5 * x * (1.0 + t)


def _gelu_grad(x):
    c = math.sqrt(2.0 / math.pi)
    t = jnp.tanh(c * (x + 0.044715 * x * x * x))
    return 0.5 * (1.0 + t) + 0.5 * x * (1.0 - t * t) * c * (1.0 + 3.0 * 0.044715 * x * x)


def _s5_tile(lp):
    return _pick(lp, (384, 320, 256, 128, 64))


def _s5_fwd(p, bblk, cblk, lam8, tab_f, dskip, lp, comm=None):
    T = p.shape[0]
    ts = _s5_tile(lp)
    seg = ts // 8
    nblk, per_seq = T // ts, lp // ts
    ucol = 0

    def body(u0, u1, u2, u3, b_ref, c_ref, lam_ref, tab_ref, d_ref, y_ref, gy_ref, cin_ref, s_sc, u_sc, y_sc, carry_sc):
        r = pl.program_id(0)
        first = (r % per_seq) == 0

        @pl.when(first)
        def _():
            carry_sc[...] = jnp.zeros_like(carry_sc)

        cin_ref[0] = carry_sc[...]
        _to_segments((u0, u1, u2, u3), u_sc, seg, first)
        u = u_sc[...]
        ub = u.astype(BF16)
        for blk in range(S5_BLK):
            s_sc[:, blk * 1024:(blk + 1) * 1024] = _dot(ub[:, blk * 128:(blk + 1) * 128], b_ref[blk], 1, 0)
        _seg_scan(s_sc, 0, seg, lam_ref, tab_ref, carry_sc, False)
        for blk in range(S5_BLK):
            sb = s_sc[:, blk * 1024:(blk + 1) * 1024].astype(BF16)
            y_sc[blk] = _dot(sb, c_ref[blk], 1, 0) + d_ref[:, blk * 128:(blk + 1) * 128] * u[:, blk * 128:(blk + 1) * 128]
        _from_segments(y_sc, y_ref, seg)
        gy_ref[...] = _gelu(y_ref[...]).astype(BF16)

    def ublock(c):
        return pl.BlockSpec((ts, 128), lambda r: (r, 4 * ucol + c))

    return _ride_call(
        body, "s5_fwd", (nblk,),
        [ublock(0), ublock(1), ublock(2), ublock(3),
         pl.BlockSpec((S5_BLK, 128, 1024), lambda r: (0, 0, 0)),
         pl.BlockSpec((S5_BLK, 1024, 128), lambda r: (0, 0, 0)),
         pl.BlockSpec((2, 8, 2048), lambda r: (0, 0, 0)),
         pl.BlockSpec((8, 8, 2048), lambda r: (0, 0, 0)),
         pl.BlockSpec((1, S5_WIDTH), lambda r: (0, 0))],
        [pl.BlockSpec((ts, S5_WIDTH), lambda r: (r, 0)),
         pl.BlockSpec((ts, S5_WIDTH), lambda r: (r, 0)),
         pl.BlockSpec((1, 8, S5_COLS), lambda r: (r, 0, 0))],
        [jax.ShapeDtypeStruct((T, S5_WIDTH), F32),
         jax.ShapeDtypeStruct((T, S5_WIDTH), BF16),
         jax.ShapeDtypeStruct((nblk, 8, S5_COLS), F32)],
        [pltpu.VMEM((ts, S5_COLS), F32), pltpu.VMEM((ts, S5_WIDTH), F32),
         pltpu.VMEM((S5_BLK, ts, 128), F32), pltpu.VMEM((8, S5_COLS), F32)],
        (p, p, p, p, bblk.astype(BF16), cblk.astype(BF16), lam8, tab_f, dskip), comm=comm)


def _s5_bwd(p, y_pre, dgy, cin, bblk, cblk, lam8, tab_f, tab_b, dskip, dif, dp, lp, comm=None):
    T = p.shape[0]
    ts = _s5_tile(lp)
    seg = ts // 8
    nblk, per_seq = T // ts, lp // ts
    ucol = 0
    assert P_IF == P_U + S5_WIDTH and P_U % (S5_WIDTH + 128) == 0

    def body(u0, u1, u2, u3, y0, y1, y2, y3, g0, g1, g2, g3, cin_ref, b_ref, bt_ref, c_ref, ct_ref, lam_ref, tf_ref,
             tb_ref, d_ref, dif_ref, dpin_ref, du_ref, dbb_ref, dcb_ref, dlam_ref, dd_ref, s_sc, a_sc, u_sc, dy_sc,
             y_sc, w_sc, carry_sc, carry_b):
        t = pl.program_id(0)
        r = nblk - 1 - t
        first = (r % per_seq) == 0
        last = (r % per_seq) == per_seq - 1

        @pl.when(t == 0)
        def _():
            dbb_ref[...] = jnp.zeros_like(dbb_ref)
            dcb_ref[...] = jnp.zeros_like(dcb_ref)
            dlam_ref[...] = jnp.zeros_like(dlam_ref)
            dd_ref[...] = jnp.zeros_like(dd_ref)

        @pl.when(last)
        def _():
            carry_b[...] = jnp.zeros_like(carry_b)

        carry_sc[...] = cin_ref[0]
        _to_segments((u0, u1, u2, u3), u_sc, seg, first)
        u = u_sc[...]
        ub = u.astype(BF16)
        for blk in range(S5_BLK):
            s_sc[8:8 + ts, blk * 1024:(blk + 1) * 1024] = _dot(ub[:, blk * 128:(blk + 1) * 128], b_ref[blk], 1, 0)
        _seg_scan(s_sc, 8, seg, lam_ref, tf_ref, carry_sc, False, cseg_ref=s_sc)

        _to_segments((g0, g1, g2, g3), dy_sc, seg)
        _to_segments((y0, y1, y2, y3), y_sc, seg)
        dy = dy_sc[...] * _gelu_grad(y_sc[...])
        dy_sc[...] = dy
        dyb = dy.astype(BF16)
        dd_ref[0:1, :] += jnp.sum(dy * u, axis=0, keepdims=True)
        for blk in range(S5_BLK):
            a_sc[:, blk * 1024:(blk + 1) * 1024] = _dot(dyb[:, blk * 128:(blk + 1) * 128], ct_ref[blk], 1, 0)
            sb = s_sc[8:8 + ts, blk * 1024:(blk + 1) * 1024].astype(BF16)
            dcb_ref[blk] += _dot(sb, dyb[:, blk * 128:(blk + 1) * 128], 0, 0)

        def lam_grad(i, cre, cim, tc, a_r, a_i):
            r0 = pl.multiple_of(i * 8, 8)
            pr = s_sc[pl.ds(r0, 8), cre:cre + 128]
            pi = s_sc[pl.ds(r0, 8), cim:cim + 128]
            dlam_ref[0, :, tc:tc + 128] += a_r * pr + a_i * pi
            dlam_ref[1, :, tc:tc + 128] += a_i * pr - a_r * pi

        _seg_scan(a_sc, 0, seg, lam_ref, tb_ref, carry_b, True, extra=lam_grad)

        for blk in range(S5_BLK):
            ab = a_sc[:, blk * 1024:(blk + 1) * 1024].astype(BF16)
            w_sc[blk] = _dot(ab, bt_ref[blk], 1, 0) + d_ref[:, blk * 128:(blk + 1) * 128] * dy_sc[:, blk * 128:(blk + 1) * 128]
            dbb_ref[blk] += _dot(u_sc[:, blk * 128:(blk + 1) * 128].astype(BF16), ab, 0, 0)
        _from_segments(w_sc, y_sc, seg, zero_head=first)
        du_ref[:, 0:S5_WIDTH] = y_sc[...].astype(BF16)
        du_ref[:, S5_WIDTH:S5_WIDTH + 128] = dif_ref[...]

    const3 = lambda t: (0, 0, 0)
    rev = lambda t: (nblk - 1 - t, 0)

    def lanes(c0):
        return [pl.BlockSpec((ts, 128), lambda t, cc=c0 + c: (nblk - 1 - t, cc)) for c in range(4)]

    return _ride_call(
        body, "s5_bwd", (nblk,),
        lanes(4 * ucol) + lanes(0) + lanes(0) + [
            pl.BlockSpec((1, 8, S5_COLS), lambda t: (nblk - 1 - t, 0, 0)),
            pl.BlockSpec((S5_BLK, 128, 1024), const3),
            pl.BlockSpec((S5_BLK, 1024, 128), const3),
            pl.BlockSpec((S5_BLK, 1024, 128), const3),
            pl.BlockSpec((S5_BLK, 128, 1024), const3),
            pl.BlockSpec((2, 8, 2048), const3),
            pl.BlockSpec((8, 8, 2048), const3),
            pl.BlockSpec((8, 8, 2048), const3),
            pl.BlockSpec((1, S5_WIDTH), lambda t: (0, 0)),
            pl.BlockSpec((ts, 128), rev),
            pl.BlockSpec(memory_space=pl.ANY)],
        [pl.BlockSpec((ts, S5_WIDTH + 128), lambda t: (nblk - 1 - t, P_U // (S5_WIDTH + 128))),
         pl.BlockSpec((S5_BLK, 128, 1024), const3),
         pl.BlockSpec((S5_BLK, 1024, 128), const3),
         pl.BlockSpec((2, 8, 2048), const3),
         pl.BlockSpec((8, S5_WIDTH), lambda t: (0, 0))],
        [jax.ShapeDtypeStruct((T, PW), BF16),
         jax.ShapeDtypeStruct((S5_BLK, 128, 1024), F32),
         jax.ShapeDtypeStruct((S5_BLK, 1024, 128), F32),
         jax.ShapeDtypeStruct((2, 8, 2048), F32),
         jax.ShapeDtypeStruct((8, S5_WIDTH), F32)],
        [pltpu.VMEM((ts + 8, S5_COLS), F32), pltpu.VMEM((ts, S5_COLS), F32),
         pltpu.VMEM((ts, S5_WIDTH), F32), pltpu.VMEM((ts, S5_WIDTH), F32),
         pltpu.VMEM((ts, S5_WIDTH), F32), pltpu.VMEM((S5_BLK, ts, 128), F32),
         pltpu.VMEM((8, S5_COLS), F32), pltpu.VMEM((8, S5_COLS), F32)],
        (p, p, p, p, y_pre, y_pre, y_pre, y_pre, dgy, dgy, dgy, dgy, cin,
         bblk.astype(BF16), jnp.swapaxes(bblk, 1, 2).astype(BF16),
         cblk.astype(BF16), jnp.swapaxes(cblk, 1, 2).astype(BF16), lam8, tab_f, tab_b, dskip, dif, dp),
        comm=comm, aliases={22: 0})


CONV_HALO = 16


def _row_in_seq(i, tr, lp):
    rowid = lax.broadcasted_iota(jnp.int32, (tr, 1), 0)
    return (i * tr + rowid) % lp


def _conv_fwd(p, w, b, lp):
    T = p.shape[0]
    tr = _pick(T, (384, 320, 256, 128, 64))
    c = P_QK // 1024
    hl = CONV_HALO

    def body(x_ref, xp_ref, w_ref, b_ref, o_ref, x_sc):
        i = pl.program_id(0)
        pos = _row_in_seq(i, tr, lp)
        pos_p = (i * tr - hl + lax.broadcasted_iota(jnp.int32, (hl, 1), 0)) % lp
        x_sc[0:hl, :] = jnp.where((pos_p < PAD_ROWS) | (i == 0), 0.0, xp_ref[...].astype(F32))
        x_sc[hl:, :] = jnp.where(pos < PAD_ROWS, 0.0, x_ref[...].astype(F32))
        taps = [w_ref[t:t + 1, :] for t in range(4)]
        bias = b_ref[...]

        def group(g, carry):
            r0 = pl.multiple_of(g * hl, hl)
            win = x_sc[pl.ds(r0, 2 * hl), :]
            acc = bias + taps[3] * win[hl:2 * hl]
            for s in (1, 2, 3):
                acc = acc + taps[3 - s] * pltpu.roll(win, s, 0)[hl:2 * hl]
            o_ref[pl.ds(r0, hl), :] = acc * _sigmoid(acc)
            return carry

        lax.fori_loop(0, tr // hl, group, 0)

    return pl.pallas_call(
        body, name="conv_fwd", grid=(T // tr,),
        in_specs=[pl.BlockSpec((tr, 1024), lambda i: (i, c)),
                  pl.BlockSpec((hl, 1024), lambda i: (jnp.maximum(i * (tr // hl) - 1, 0), c)),
                  pl.BlockSpec((4, 1024), lambda i: (0, 0)),
                  pl.BlockSpec((1, 1024), lambda i: (0, 0))],
        out_specs=pl.BlockSpec((tr, 1024), lambda i: (i, 0)),
        out_shape=jax.ShapeDtypeStruct((T, 1024), F32),
        scratch_shapes=[pltpu.VMEM((tr + hl, 1024), F32)],
        compiler_params=_cparams(("arbitrary",)),
    )(p, p, w, b)


def _conv_bwd(p, dqk, w, b, dp, lp):
    T = p.shape[0]
    tr = _pick(T, (384, 320, 256, 128, 64))
    c = P_QK // 1024
    nb = T // tr
    hl = CONV_HALO

    def body(x_ref, xp_ref, xn_ref, g_ref, gn_ref, w_ref, b_ref, dpin_ref, dx_ref, dw_ref, db_ref, x_sc, g_sc, part):
        i = pl.program_id(0)

        @pl.when(i == 0)
        def _():
            dw_ref[...] = jnp.zeros_like(dw_ref)
            db_ref[...] = jnp.zeros_like(db_ref)

        def seqpos(off, n):
            return (i * tr + off + lax.broadcasted_iota(jnp.int32, (n, 1), 0)) % lp

        x_sc[0:hl, :] = jnp.where((seqpos(-hl, hl) < PAD_ROWS) | (i == 0), 0.0, xp_ref[...].astype(F32))
        x_sc[hl:hl + tr, :] = jnp.where(seqpos(0, tr) < PAD_ROWS, 0.0, x_ref[...].astype(F32))
        x_sc[hl + tr:, :] = jnp.where((seqpos(tr, hl) < PAD_ROWS) | (i == nb - 1), 0.0, xn_ref[...].astype(F32))
        g_sc[0:tr, :] = g_ref[...].astype(F32)
        g_sc[tr:, :] = gn_ref[...].astype(F32)
        part[...] = jnp.zeros_like(part)
        taps = [w_ref[t:t + 1, :] for t in range(4)]
        bias = b_ref[...]

        ngroups = tr // hl

        def group(g, prev):
            r0 = pl.multiple_of(g * hl, hl)
            win = x_sc[pl.ds(r0, 2 * hl), :]
            sh = [win] + [pltpu.roll(win, s, 0) for s in (1, 2, 3)]
            acc = bias
            for s in range(4):
                acc = acc + taps[3 - s] * sh[s][hl:2 * hl]
            sg = _sigmoid(acc)
            rows = r0 + lax.broadcasted_iota(jnp.int32, (hl, 1), 0)
            pos = (i * tr + rows) % lp
            valid = (pos >= PAD_ROWS) & ((g < ngroups) | (i < nb - 1))
            cur = jnp.where(valid, g_sc[pl.ds(r0, hl), :] * (sg * (1.0 + acc * (1.0 - sg))), 0.0)
            own = jnp.where(g < ngroups, cur, 0.0)
            part[4] += own
            for s in range(4):
                part[3 - s] += own * sh[s][hl:2 * hl]

            @pl.when(g > 0)
            def _():
                both = jnp.concatenate([prev, cur], axis=0)
                dx = taps[3] * prev
                for s in (1, 2, 3):
                    dx = dx + taps[3 - s] * pltpu.roll(both, 2 * hl - s, 0)[0:hl]
                q0 = pl.multiple_of(r0 - hl, hl)
                pos_prev = (i * tr + q0 + lax.broadcasted_iota(jnp.int32, (hl, 1), 0)) % lp
                dx_ref[pl.ds(q0, hl), :] = jnp.where(pos_prev < PAD_ROWS, 0.0, dx).astype(BF16)

            return cur

        lax.fori_loop(0, ngroups + 1, group, jnp.zeros((hl, 1024), F32))
        db_ref[0:1, :] += jnp.sum(part[4], axis=0, keepdims=True)
        for t in range(4):
            dw_ref[t:t + 1, :] += jnp.sum(part[t], axis=0, keepdims=True)

    t8 = tr // hl
    return pl.pallas_call(
        body, name="conv_bwd", grid=(nb,),
        in_specs=[pl.BlockSpec((tr, 1024), lambda i: (i, c)),
                  pl.BlockSpec((hl, 1024), lambda i: (jnp.maximum(i * t8 - 1, 0), c)),
                  pl.BlockSpec((hl, 1024), lambda i: (jnp.minimum((i + 1) * t8, nb * t8 - 1), c)),
                  pl.BlockSpec((tr, 1024), lambda i: (i, 0)),
                  pl.BlockSpec((hl, 1024), lambda i: (jnp.minimum((i + 1) * t8, nb * t8 - 1), 0)),
                  pl.BlockSpec((4, 1024), lambda i: (0, 0)),
                  pl.BlockSpec((1, 1024), lambda i: (0, 0)),
                  pl.BlockSpec(memory_space=pl.ANY)],
        out_specs=(pl.BlockSpec((tr, 1024), lambda i: (i, c)),
                   pl.BlockSpec((8, 1024), lambda i: (0, 0)),
                   pl.BlockSpec((8, 1024), lambda i: (0, 0))),
        out_shape=(jax.ShapeDtypeStruct((T, PW), BF16),
                   jax.ShapeDtypeStruct((8, 1024), F32),
                   jax.ShapeDtypeStruct((8, 1024), F32)),
        input_output_aliases={7: 0},
        scratch_shapes=[pltpu.VMEM((tr + 2 * hl, 1024), F32), pltpu.VMEM((tr + hl, 1024), F32),
                        pltpu.VMEM((5, hl, 1024), F32)],
        compiler_params=_cparams(("arbitrary",)),
    )(p, p, p, dqk, dqk, w, b, dp)


def _split3(x):
    hi = x.astype(BF16)
    r1 = x - hi.astype(F32)
    mid = r1.astype(BF16)
    lo = (r1 - mid.astype(F32)).astype(BF16)
    return hi, mid, lo


def _tri_sum(x, upper):
    r = lax.broadcasted_iota(jnp.int32, (CHUNK, CHUNK), 0)
    c = lax.broadcasted_iota(jnp.int32, (CHUNK, CHUNK), 1)
    tri = jnp.where((r <= c) if upper else (r >= c), 1.0, 0.0).astype(BF16)
    hi, mid, lo = _split3(x)
    return _dot(tri, hi, 1, 0) + _dot(tri, mid, 1, 0) + _dot(tri, lo, 1, 0)


def _lane_col(x, lane):
    l = lax.broadcasted_iota(jnp.int32, x.shape, 1)
    return jnp.sum(jnp.where(l == lane, x, 0.0), axis=1, keepdims=True)


def _to_row(col):
    r = lax.broadcasted_iota(jnp.int32, (CHUNK, CHUNK), 0)
    c = lax.broadcasted_iota(jnp.int32, (CHUNK, CHUNK), 1)
    return jnp.sum(jnp.where(r == c, col, 0.0), axis=0, keepdims=True)


def _to_col(row):
    r = lax.broadcasted_iota(jnp.int32, (CHUNK, CHUNK), 0)
    c = lax.broadcasted_iota(jnp.int32, (CHUNK, CHUNK), 1)
    return jnp.sum(jnp.where(r == c, row, 0.0), axis=1, keepdims=True)


def _log_sigmoid(x):
    return jnp.minimum(x, 0.0) - jnp.log(1.0 + jnp.exp(-jnp.abs(x)))


def _mlstm_gates(ifv, padmask):
    lf = jnp.where(padmask, 0.0, _log_sigmoid(ifv))
    b_all = _tri_sum(lf, False)
    li = jnp.where(padmask, -jnp.inf, ifv)
    return li, b_all


def _mlstm_head_fwd(q, k, v, li_col, b_col, c_st, n_st, m_st):
    r = lax.broadcasted_iota(jnp.int32, (CHUNK, CHUNK), 0)
    c = lax.broadcasted_iota(jnp.int32, (CHUNK, CHUNK), 1)
    rowid = lax.broadcasted_iota(jnp.int32, (CHUNK, 1), 0)
    b_row = _to_row(b_col)
    li_row = _to_row(li_col)
    dmat = jnp.where(r >= c, b_col - b_row + li_row, -jnp.inf)
    m_inter = b_col + m_st
    m_row = jnp.maximum(m_inter, jnp.max(dmat, axis=1, keepdims=True))
    w_intra = jnp.exp(dmat - m_row)
    w_inter = jnp.exp(m_inter - m_row)
    qb, kb, vb = q.astype(BF16), k.astype(BF16), v.astype(BF16)
    qk = _dot(qb, kb, 1, 1)
    s = qk * w_intra
    cb = c_st.astype(BF16)
    qc = _dot(qb, cb, 1, 0)
    qn = jnp.sum(q * n_st, axis=1, keepdims=True)
    num = _dot(s.astype(BF16), vb, 1, 0) + w_inter * qc
    den = jnp.sum(s, axis=1, keepdims=True) + w_inter * qn
    floor = jnp.exp(-m_row)
    rinv = 1.0 / jnp.maximum(jnp.abs(den), floor)
    h = num * rinv
    b_last = jnp.sum(jnp.where(rowid == CHUNK - 1, b_col, 0.0), axis=0, keepdims=True)
    g_col = b_last - b_col + li_col
    m_new = jnp.maximum(b_last + m_st, jnp.max(g_col, axis=0, keepdims=True))
    w_k = jnp.exp(g_col - m_new)
    decay = jnp.exp(b_last + m_st - m_new)
    kw = w_k * k
    c_new = decay * c_st + _dot(kw.astype(BF16), vb, 0, 0)
    n_new = decay * n_st + jnp.sum(kw, axis=0, keepdims=True)
    return dict(h=h, c_new=c_new, n_new=n_new, m_new=m_new, w_intra=w_intra, w_inter=w_inter, s=s,
                qc=qc, qn=qn, den=den, floor=floor, rinv=rinv, w_k=w_k, decay=decay, kw=kw,
                qb=qb, kb=kb, vb=vb, cb=cb)


def _mlstm_fwd(p, puif, qk, lp):
    T = p.shape[0]
    nch = lp // CHUNK
    B = T // lp
    scale = M_DK ** -0.5

    def body(q_ref, k_ref, v_ref, if_ref, h_ref, cst_ref, nm_ref, *state):
        c_scs, nm_scs = state[:B * M_HEADS], state[B * M_HEADS:]
        ci = pl.program_id(0)

        @pl.when(ci == 0)
        def _():
            for ref in state:
                ref[...] = jnp.zeros_like(ref)

        rowid = lax.broadcasted_iota(jnp.int32, (CHUNK, 1), 0)
        padmask = (ci == 0) & (rowid < PAD_ROWS)
        for bb in range(B):
            li_all, b_all = _mlstm_gates(if_ref[bb], padmask)
            for hd in range(M_HEADS):
                c_sc, nm_sc = c_scs[bb * M_HEADS + hd], nm_scs[bb * M_HEADS + hd]
                c_st, n_st, m_row = c_sc[...], nm_sc[0:1, :], nm_sc[1:2, :]
                cst_ref[bb, 0, hd * M_DK:(hd + 1) * M_DK, :] = c_st
                nm_ref[bb, 0, hd:hd + 1, :] = n_st
                nm_ref[bb, 0, M_HEADS + hd:M_HEADS + hd + 1, :] = m_row
                q = q_ref[bb, :, hd * M_DK:(hd + 1) * M_DK]
                k = k_ref[bb, :, hd * M_DK:(hd + 1) * M_DK] * scale
                v = v_ref[bb, :, hd * M_DV:(hd + 1) * M_DV]
                o = _mlstm_head_fwd(q, k, v, _lane_col(li_all, hd), _lane_col(b_all, M_HEADS + hd),
                                    c_st, n_st, nm_sc[1:2, 0:1])
                h_ref[bb, :, hd * M_DV:(hd + 1) * M_DV] = o["h"].astype(BF16)
                c_sc[...] = o["c_new"]
                nm_sc[0:1, :] = o["n_new"]
                nm_sc[1:2, :] = jnp.broadcast_to(o["m_new"], (1, 128))

    qk3, p3, uif3 = qk.reshape(B, lp, 1024), p.reshape(B, lp, p.shape[1]), puif.reshape(B, lp, puif.shape[1])
    h, cst, nm = pl.pallas_call(
        body, name="mlstm_fwd", grid=(nch,),
        in_specs=[pl.BlockSpec((B, CHUNK, 512), lambda ci: (0, ci, 0)),
                  pl.BlockSpec((B, CHUNK, 512), lambda ci: (0, ci, 1)),
                  pl.BlockSpec((B, CHUNK, 1024), lambda ci: (0, ci, P_V // 1024)),
                  pl.BlockSpec((B, CHUNK, 128), lambda ci: (0, ci, S5_WIDTH // 128))],
        out_specs=(pl.BlockSpec((B, CHUNK, 1024), lambda ci: (0, ci, 0)),
                   pl.BlockSpec((B, 1, M_HEADS * M_DK, M_DV), lambda ci: (0, ci, 0, 0)),
                   pl.BlockSpec((B, 1, 8, 128), lambda ci: (0, ci, 0, 0))),
        out_shape=(jax.ShapeDtypeStruct((B, lp, 1024), BF16),
                   jax.ShapeDtypeStruct((B, nch, M_HEADS * M_DK, M_DV), F32),
                   jax.ShapeDtypeStruct((B, nch, 8, 128), F32)),
        scratch_shapes=([pltpu.VMEM((M_DK, M_DV), F32)] * (B * M_HEADS)
                        + [pltpu.VMEM((8, 128), F32)] * (B * M_HEADS)),
        compiler_params=_cparams(("arbitrary",)),
    )(qk3, qk3, p3, uif3)
    return h.reshape(T, 1024), cst, nm


def _mlstm_bwd(p, puif, qk, cst, nm, dh, dp, lp, comm=None):
    T = p.shape[0]
    nch = lp // CHUNK
    B = T // lp
    scale = M_DK ** -0.5

    def body(q_ref, k_ref, v_ref, if_ref, cst_ref, nm_ref, dh_ref, dpin_ref, dqk_ref, dv_ref, dif_ref, *state):
        t = pl.program_id(0)
        ci = nch - 1 - t

        @pl.when(t == 0)
        def _():
            for ref in state:
                ref[...] = jnp.zeros_like(ref)

        for bb in range(B):
            one_sequence(bb, ci, q_ref, k_ref, v_ref, if_ref, cst_ref, nm_ref, dh_ref, dqk_ref, dv_ref, dif_ref,
                         state[bb * M_HEADS:(bb + 1) * M_HEADS],
                         state[(B + bb) * M_HEADS:(B + bb + 1) * M_HEADS])

    def one_sequence(bb, ci, q_ref, k_ref, v_ref, if_ref, cst_ref, nm_ref, dh_ref, dqk_ref, dv_ref, dif_ref,
                     dc_scs, dn_scs):
        rowid = lax.broadcasted_iota(jnp.int32, (CHUNK, 1), 0)
        lane = lax.broadcasted_iota(jnp.int32, (CHUNK, 128), 1)
        padmask = (ci == 0) & (rowid < PAD_ROWS)
        ifv = if_ref[bb]
        li_all, b_all = _mlstm_gates(ifv, padmask)
        db_all = jnp.zeros((CHUNK, 128), F32)
        dli_all = jnp.zeros((CHUNK, 128), F32)
        for hd in range(M_HEADS):
            q = q_ref[bb, :, hd * M_DK:(hd + 1) * M_DK]
            k = k_ref[bb, :, hd * M_DK:(hd + 1) * M_DK] * scale
            v = v_ref[bb, :, hd * M_DV:(hd + 1) * M_DV]
            c_st = cst_ref[bb, 0, hd * M_DK:(hd + 1) * M_DK, :]
            n_st = nm_ref[bb, 0, hd:hd + 1, :]
            m_st = nm_ref[bb, 0, M_HEADS + hd:M_HEADS + hd + 1, 0:1]
            o = _mlstm_head_fwd(q, k, v, _lane_col(li_all, hd), _lane_col(b_all, M_HEADS + hd), c_st, n_st, m_st)
            dc_sc, dn_sc = dc_scs[hd], dn_scs[hd]
            dc_new = dc_sc[...]
            dn_new = dn_sc[0:1, :]
            dcb = dc_new.astype(BF16)
            dhh = dh_ref[bb, :, hd * M_DV:(hd + 1) * M_DV].astype(F32)
            dnum = dhh * o["rinv"]
            dhh_h = jnp.sum(dhh * o["h"], axis=1, keepdims=True)
            sgn = jnp.where(o["den"] >= 0.0, 1.0, -1.0)
            dden = jnp.where(jnp.abs(o["den"]) > o["floor"], -dhh_h * o["rinv"] * sgn, 0.0)
            dnb = dnum.astype(BF16)
            ds = _dot(dnb, o["vb"], 1, 1) + dden
            sb = o["s"].astype(BF16)
            kwb = o["kw"].astype(BF16)
            dv = _dot(sb, dnb, 0, 0) + _dot(kwb, dcb, 1, 0)
            dqk_m = (ds * o["w_intra"]).astype(BF16)
            wdn = o["w_inter"] * dnum
            wdd = o["w_inter"] * dden
            dq = _dot(dqk_m, o["kb"], 1, 0) + _dot(wdn.astype(BF16), o["cb"], 1, 1) + wdd * n_st
            vdc = _dot(o["vb"], dcb, 1, 1)
            dk = _dot(dqk_m, o["qb"], 0, 0) + o["w_k"] * (vdc + dn_new)
            dd = ds * o["s"]
            dd_col = _to_col(jnp.sum(dd, axis=0, keepdims=True))
            dmi = o["w_inter"] * (jnp.sum(dnum * o["qc"], axis=1, keepdims=True) + dden * o["qn"])
            dg = o["w_k"] * (jnp.sum(k * vdc, axis=1, keepdims=True) + jnp.sum(k * dn_new, axis=1, keepdims=True))
            d_blast = (o["decay"] * (jnp.sum(jnp.sum(dc_new * c_st, axis=1, keepdims=True), axis=0, keepdims=True)
                                     + jnp.sum(dn_new * n_st, axis=1, keepdims=True))
                       + jnp.sum(dg, axis=0, keepdims=True))
            db_col = jnp.sum(dd, axis=1, keepdims=True) - dd_col + dmi - dg
            db_col = db_col + jnp.where(rowid == CHUNK - 1, d_blast, 0.0)
            dli_col = dd_col + dg
            db_all = db_all + jnp.where(lane == M_HEADS + hd, db_col, 0.0)
            dli_all = dli_all + jnp.where(lane == hd, dli_col, 0.0)
            dc_sc[...] = o["decay"] * dc_new + _dot(o["qb"], wdn.astype(BF16), 0, 0)
            dn_sc[0:1, :] = o["decay"] * dn_new + jnp.sum(q * wdd, axis=0, keepdims=True)
            dqk_ref[bb, :, hd * M_DK:(hd + 1) * M_DK] = dq.astype(BF16)
            dqk_ref[bb, :, 512 + hd * M_DK:512 + (hd + 1) * M_DK] = (dk * scale).astype(BF16)
            dv_ref[bb, :, hd * M_DV:(hd + 1) * M_DV] = dv.astype(BF16)
        dlf = _tri_sum(db_all, True)
        dif = dli_all + dlf * _sigmoid(-ifv)
        dif_ref[bb] = jnp.where(padmask | (lane >= 2 * M_HEADS), 0.0, dif).astype(BF16)

    def rev(cb):
        return lambda t: (0, nch - 1 - t, cb)

    rev4 = lambda t: (0, nch - 1 - t, 0, 0)
    qk3, p3, dh3 = qk.reshape(B, lp, 1024), p.reshape(B, lp, p.shape[1]), dh.reshape(B, lp, 1024)
    uif3 = puif.reshape(B, lp, puif.shape[1])
    (dqk, dv, dif), comm_res = _ride_call(
        body, "mlstm_bwd", (nch,),
        [pl.BlockSpec((B, CHUNK, 512), rev(0)),
         pl.BlockSpec((B, CHUNK, 512), rev(1)),
         pl.BlockSpec((B, CHUNK, 1024), rev(P_V // 1024)),
         pl.BlockSpec((B, CHUNK, 128), rev(S5_WIDTH // 128)),
         pl.BlockSpec((B, 1, M_HEADS * M_DK, M_DV), rev4),
         pl.BlockSpec((B, 1, 8, 128), rev4),
         pl.BlockSpec((B, CHUNK, 1024), rev(0)),
         pl.BlockSpec(memory_space=pl.ANY)],
        [pl.BlockSpec((B, CHUNK, 1024), rev(0)),
         pl.BlockSpec((B, CHUNK, 1024), rev(P_V // 1024)),
         pl.BlockSpec((B, CHUNK, 128), rev(0))],
        [jax.ShapeDtypeStruct((B, lp, 1024), BF16),
         jax.ShapeDtypeStruct((B, lp, PW), BF16),
         jax.ShapeDtypeStruct((B, lp, 128), BF16)],
        [pltpu.VMEM((M_DK, M_DV), F32)] * (B * M_HEADS) + [pltpu.VMEM((8, 128), F32)] * (B * M_HEADS),
        (qk3, qk3, p3, uif3, cst, nm, dh3, dp.reshape(B, lp, PW)), comm=comm, aliases={7: 1})
    return dqk.reshape(T, 1024), dv.reshape(T, PW), dif.reshape(T, 128), comm_res


def _headnorm_fwd(hm, p, g):
    T = hm.shape[0]
    tr = _pick(T, (384, 320, 256, 128, 64))

    def body(h_ref, o_ref, g_ref, a_ref):
        for hd in range(M_HEADS):
            sl = slice(hd * M_DV, (hd + 1) * M_DV)
            hn = _ln_rows(h_ref[:, sl].astype(F32), g_ref[:, sl], 0.0)
            a_ref[:, sl] = (_sigmoid(o_ref[:, sl].astype(F32)) * hn).astype(BF16)

    return pl.pallas_call(
        body, name="headnorm_fwd", grid=(T // tr,),
        in_specs=[pl.BlockSpec((tr, 1024), lambda i: (i, 0)),
                  pl.BlockSpec((tr, 1024), lambda i: (i, P_O // 1024)),
                  pl.BlockSpec((1, 1024), lambda i: (0, 0))],
        out_specs=pl.BlockSpec((tr, 1024), lambda i: (i, 0)),
        out_shape=jax.ShapeDtypeStruct((T, 1024), BF16),
        compiler_params=_cparams(("arbitrary",)),
    )(hm, p, g)


def _headnorm_bwd(da, hm, p, g, dp):
    T = hm.shape[0]
    tr = _pick(T, (384, 320, 256, 128, 64))

    def body(da_ref, h_ref, o_ref, g_ref, dpin_ref, dh_ref, do_ref, dg_ref):
        @pl.when(pl.program_id(0) == 0)
        def _():
            dg_ref[...] = jnp.zeros_like(dg_ref)

        for hd in range(M_HEADS):
            sl = slice(hd * M_DV, (hd + 1) * M_DV)
            gg = g_ref[:, sl]
            so = _sigmoid(o_ref[:, sl].astype(F32))
            da = da_ref[:, sl].astype(F32)
            dhn = da * so
            dv, xhat = _ln_bwd_rows(dhn, h_ref[:, sl].astype(F32), gg)
            dh_ref[:, sl] = dv.astype(BF16)
            do_ref[:, sl] = (da * (xhat * gg) * so * (1.0 - so)).astype(BF16)
            dg_ref[0:1, sl] += jnp.sum(dhn * xhat, axis=0, keepdims=True)

    row = pl.BlockSpec((tr, 1024), lambda i: (i, 0))
    return pl.pallas_call(
        body, name="headnorm_bwd", grid=(T // tr,),
        in_specs=[row, row, pl.BlockSpec((tr, 1024), lambda i: (i, P_O // 1024)),
                  pl.BlockSpec((1, 1024), lambda i: (0, 0)), pl.BlockSpec(memory_space=pl.ANY)],
        out_specs=(row, pl.BlockSpec((tr, 1024), lambda i: (i, P_O // 1024)),
                   pl.BlockSpec((8, 1024), lambda i: (0, 0))),
        out_shape=(jax.ShapeDtypeStruct((T, 1024), BF16), jax.ShapeDtypeStruct((T, PW), BF16),
                   jax.ShapeDtypeStruct((8, 1024), F32)),
        input_output_aliases={4: 1},
        compiler_params=_cparams(("arbitrary",)),
    )(da, hm, p, g, dp)


def _mix_fwd(z, ym, p):
    T = ym.shape[0]
    tr = _pick(T, (384, 320, 256, 128, 64))

    def body(z1_ref, z2_ref, ym_ref, gs_ref, gm_ref, o_ref):
        ys = z1_ref[...].astype(F32) * _sigmoid(z2_ref[...].astype(F32))
        o_ref[...] = (_sigmoid(gs_ref[...].astype(F32)) * ys
                      + _sigmoid(gm_ref[...].astype(F32)) * ym_ref[...].astype(F32)).astype(BF16)

    def col(cb):
        return pl.BlockSpec((tr, 1024), lambda i: (i, cb))

    return pl.pallas_call(
        body, name="mix_fwd", grid=(T // tr,),
        in_specs=[col(0), col(1), col(0), col(P_GS // 1024), col(P_GM // 1024)],
        out_specs=col(0),
        out_shape=jax.ShapeDtypeStruct((T, 1024), BF16),
        compiler_params=_cparams(("arbitrary",)),
    )(z, z, ym, p, p)


def _mix_bwd(dmix, z, ym, p):
    T = ym.shape[0]
    tr = _pick(T, (384, 320, 256, 128, 64))

    def body(d_ref, z1_ref, z2_ref, ym_ref, gs_ref, gm_ref, dz_ref, dym_ref, dp_ref):
        d = d_ref[...].astype(F32)
        z1 = z1_ref[...].astype(F32)
        s2 = _sigmoid(z2_ref[...].astype(F32))
        ss = _sigmoid(gs_ref[...].astype(F32))
        sm = _sigmoid(gm_ref[...].astype(F32))
        ys = z1 * s2
        dys = d * ss
        dp_ref[:, 0:1024] = (d * ys * ss * (1.0 - ss)).astype(BF16)
        dp_ref[:, 1024:2048] = (d * ym_ref[...].astype(F32) * sm * (1.0 - sm)).astype(BF16)
        dym_ref[...] = (d * sm).astype(BF16)
        dz_ref[:, 0:1024] = (dys * s2).astype(BF16)
        dz_ref[:, 1024:2048] = (dys * z1 * s2 * (1.0 - s2)).astype(BF16)

    def col(cb):
        return pl.BlockSpec((tr, 1024), lambda i: (i, cb))

    o = jax.ShapeDtypeStruct((T, 1024), BF16)
    return pl.pallas_call(
        body, name="mix_bwd", grid=(T // tr,),
        in_specs=[col(0), col(0), col(1), col(0), col(P_GS // 1024), col(P_GM // 1024)],
        out_specs=(pl.BlockSpec((tr, 2048), lambda i: (i, 0)), col(0),
                   pl.BlockSpec((tr, 2048), lambda i: (i, P_GS // 2048))),
        out_shape=(jax.ShapeDtypeStruct((T, 2048), BF16), o, jax.ShapeDtypeStruct((T, PW), BF16)),
        compiler_params=_cparams(("arbitrary",)),
    )(dmix, z, z, ym, p, p)


def _adamw_math(w, g, m, v):
    m2 = ADAM_B1 * m + (1.0 - ADAM_B1) * g
    v2 = ADAM_B2 * v + (1.0 - ADAM_B2) * jnp.square(g)
    m_hat = m2 / (1.0 - ADAM_B1 ** ADAM_STEP)
    v_hat = v2 / (1.0 - ADAM_B2 ** ADAM_STEP)
    delta = -ADAM_LR * (m_hat / (jnp.sqrt(v_hat) + ADAM_EPS) + ADAM_WD * w)
    return delta, m2, v2


def _adamw_big(recv, w, m, v, name):
    R, C = w.shape
    tr, tc = _shard_tile(R, C)
    ns = recv.shape[0]

    def body(r_ref, w_ref, m_ref, v_ref, g_ref, d_ref, m2_ref, v2_ref):
        g = r_ref[0].astype(F32)
        for s in range(1, ns):
            g = g + r_ref[s].astype(F32)
        d, m2, v2 = _adamw_math(w_ref[...], g, m_ref[...], v_ref[...])
        g_ref[...] = g
        d_ref[...] = d
        m2_ref[...] = m2
        v2_ref[...] = v2

    blk = pl.BlockSpec((tr, tc), lambda i, j: (i, j))
    o = jax.ShapeDtypeStruct((R, C), F32)
    return pl.pallas_call(
        body, name=name, grid=(R // tr, C // tc),
        in_specs=[pl.BlockSpec((ns, tr, tc), lambda i, j: (0, i, j)), blk, blk, blk],
        out_specs=(blk,) * 4, out_shape=(o,) * 4,
        compiler_params=_cparams(("arbitrary", "arbitrary")),
    )(recv, w, m, v)


def _adamw_small(gs, ws, ms, vs):
    n = len(ws)
    shapes = [tuple(w.shape) for w in ws]

    def flat2(a):
        return a.reshape(1, -1) if a.ndim == 1 else a.reshape(-1, a.shape[-1])

    def body(*refs):
        g, w, m, v = refs[:n], refs[n:2 * n], refs[2 * n:3 * n], refs[3 * n:4 * n]
        d, m2, v2 = refs[4 * n:5 * n], refs[5 * n:6 * n], refs[6 * n:]
        for i in range(n):
            di, mi, vi = _adamw_math(w[i][...], g[i][...], m[i][...], v[i][...])
            d[i][...] = di
            m2[i][...] = mi
            v2[i][...] = vi

    args = [flat2(a) for a in list(gs) + list(ws) + list(ms) + list(vs)]
    vm = pl.BlockSpec(memory_space=pltpu.VMEM)
    outs = pl.pallas_call(
        body, name="adamw_small",
        in_specs=[vm] * (4 * n), out_specs=tuple([vm] * (3 * n)),
        out_shape=tuple(jax.ShapeDtypeStruct(a.shape, F32) for a in args[:n] * 3),
        compiler_params=pltpu.CompilerParams(vmem_limit_bytes=VMEM_LIMIT),
    )(*args)
    outs = [o.reshape(s) for o, s in zip(outs, shapes * 3)]
    return outs[:n], outs[n:2 * n], outs[2 * n:]


def _pack(arrs):
    parts = []
    for a in arrs:
        f = a.reshape(-1).astype(F32)
        n = -(-f.shape[0] // 1024) * 1024
        parts.append(jnp.pad(f, (0, n - f.shape[0])))
    return jnp.concatenate(parts).reshape(-1, 128)


def _unpack(pack, shapes):
    flat = pack.reshape(-1)
    out, off = [], 0
    for shp in shapes:
        n = math.prod(shp)
        out.append(flat[off:off + n].reshape(shp))
        off += -(-n // 1024) * 1024
    return out


def _cols_from_shards(g):
    return jnp.transpose(g, (1, 0, 2)).reshape(g.shape[1], -1)


def _cols_to_shards(w):
    R = w.shape[0]
    return jnp.transpose(w.reshape(R, N_DEV, -1), (1, 0, 2))


def _nat_to_aligned(w, axis):
    def sl(start, size):
        return lax.slice_in_dim(w, start, start + size, axis=axis)

    pads = [(0, 0)] * w.ndim
    pads[axis] = (0, PW - P_IF - 8)
    return jnp.concatenate([sl(N_V, 1024), sl(N_O, 1024), sl(N_GS, 1024), sl(N_GM, 1024), sl(N_Q, 1024),
                            sl(N_U, 512), jnp.pad(sl(N_I, 8), pads)], axis=axis)


def _aligned_to_nat(w, axis):
    def sl(start, size):
        return lax.slice_in_dim(w, start, start + size, axis=axis)

    return jnp.concatenate([sl(P_U, 512), sl(P_QK, 1024), sl(P_V, 1024), sl(P_O, 1024), sl(P_IF, 8),
                            sl(P_GS, 1024), sl(P_GM, 1024)], axis=axis)


def kernel(x, meta_tokens, ln0_g, ln0_b, w_in, b_in, qk_conv_w, qk_conv_b, s5_lambda_re, s5_lambda_im, s5_log_dt, s5_b_re, s5_b_im, s5_c_re, s5_c_im, s5_d, s5_w_glu, m_norm_g, m_w_out, w_o, ln1_g, ln1_b, w_up, b_up, w_down, ln2_g, ln2_b, loss_target, m_meta_tokens, m_ln0_g, m_ln0_b, m_w_in, m_b_in, m_qk_conv_w, m_qk_conv_b, m_s5_lambda_re, m_s5_lambda_im, m_s5_log_dt, m_s5_b_re, m_s5_b_im, m_s5_c_re, m_s5_c_im, m_s5_d, m_s5_w_glu, m_m_norm_g, m_m_w_out, m_w_o, m_ln1_g, m_ln1_b, m_w_up, m_b_up, m_w_down, m_ln2_g, m_ln2_b, v_meta_tokens, v_ln0_g, v_ln0_b, v_w_in, v_b_in, v_qk_conv_w, v_qk_conv_b, v_s5_lambda_re, v_s5_lambda_im, v_s5_log_dt, v_s5_b_re, v_s5_b_im, v_s5_c_re, v_s5_c_im, v_s5_d, v_s5_w_glu, v_m_norm_g, v_m_w_out, v_w_o, v_ln1_g, v_ln1_b, v_w_up, v_b_up, v_w_down, v_ln2_g, v_ln2_b):
    B, S, D = x.shape
    lp = S + CHUNK
    me = _my_id()

    ln0g, ln0b = ln0_g.reshape(1, D), ln0_b.reshape(1, D)
    (h0, h0b), first = _ln0_fwd(x, ln0g, ln0b, comm=(_Gather, [w_in[0].T.astype(BF16), meta_tokens, qk_conv_w[0]]))
    win_t = _nat_to_aligned(first[0].reshape(IN_NAT, D), 0)
    meta_f = _cols_from_shards(first[1])
    convw_f = _cols_from_shards(first[2])
    later = [a.astype(BF16) for a in (s5_w_glu[0], m_w_out[0], w_o[0])]
    latest = [a.astype(BF16) for a in (w_up[0], w_down[0])]
    b_in_al = _nat_to_aligned(b_in, 1)

    s5_args = (s5_lambda_re[0], s5_lambda_im[0], s5_log_dt[0], s5_b_re[0], s5_b_im[0], s5_c_re[0], s5_c_im[0])
    (ar, ai, bblk, cblk), s5_vjp = jax.vjp(_s5_prep, *s5_args)
    seg = _s5_tile(lp) // 8
    mu_r, mu_i = ar, ai
    for _ in range(seg - 1):
        mu_r, mu_i = mu_r * ar - mu_i * ai, mu_r * ai + mu_i * ar
    tab_f, tab_b = _scan_tables(mu_r, mu_i)
    lam8 = jnp.stack([jnp.broadcast_to(ar, (8, 2048)), jnp.broadcast_to(ai, (8, 2048))])

    h0, h0b = _ln0_head(meta_f, ln0g, ln0b, h0, h0b, lp)
    p, *gathered = _mm(h0b, win_t, "nt", "mm_in", bias=b_in_al[:, :P_U], out_dtype=BF16, b_rows=(0, P_U),
                       comm=(_Gather, later))
    puif = _mm(h0b, win_t, "nt", "mm_in_uif", bias=b_in_al[:, P_U:], b_rows=(P_U, PW - P_U))
    wglu_f = _cols_from_shards(gathered[0])
    wmo_f = gathered[1].reshape(1024, 1024)
    wo_f = gathered[2].reshape(1024, 1024)
    (y_pre, gy, cin), gathered2 = _s5_fwd(puif, bblk, cblk, lam8, tab_f, s5_d, lp, comm=(_Gather, latest))
    wup_f = _cols_from_shards(gathered2[0])
    wdown_f = gathered2[1].reshape(D_FF, 1024)
    z = _mm(gy, wglu_f, "nn", "mm_glu", out_dtype=BF16)
    qk = _conv_fwd(p, convw_f, qk_conv_b, lp)
    hm, cst, nm = _mlstm_fwd(p, puif, qk, lp)
    a_m = _headnorm_fwd(hm, p, m_norm_g)
    ym = _mm(a_m, wmo_f, "nn", "mm_mout", out_dtype=BF16)
    mix = _mix_fwd(z, ym, p)
    r1 = _mm(mix, wo_f, "nn", "mm_o")
    pre1, h1, h1b = _ln_res_fwd(h0, r1, ln1_g, ln1_b, "ln1_fwd")
    act = _mm(h1b, wup_f, "nn", "mm_up", bias=b_up, relu2_out=True, out_dtype=BF16)
    ff = _mm(act, wdown_f, "nn", "mm_down")
    loss_acc, dpre2, dpre2b, dg2, db2 = _ln2_loss(h1, ff, ln2_g, ln2_b, loss_target)

    d_up = _mm(dpre2b, wdown_f, "nt", "mm_d_act", sqrt_gate_of=act, out_dtype=BF16)
    g_wdown = _mm(act, dpre2b, "tn", "mm_g_wdown", out_dtype=BF16)
    g_wup, cs_up = _mm(h1b, d_up, "tn", "mm_g_wup", colsum="b", out_dtype=BF16)
    dh1 = _mm(d_up, wup_f, "nt", "mm_d_h1", add=dpre2, add_scale=ALPHA)
    dpre1, dpre1b, dg1, db1 = _ln_bwd(dh1, pre1, ln1_g, "ln1_bwd")
    g_wo = _mm(mix, dpre1b, "tn", "mm_g_wo", out_dtype=BF16)
    dmix = _mm(dpre1b, wo_f, "nt", "mm_d_mix", out_dtype=BF16)
    dz, dym, dp = _mix_bwd(dmix, z, ym, p)
    g_wmo = _mm(a_m, dym, "tn", "mm_g_wmo", out_dtype=BF16)
    da = _mm(dym, wmo_f, "nt", "mm_d_a", out_dtype=BF16)
    dhm, dp, dg_norm = _headnorm_bwd(da, hm, p, m_norm_g, dp)
    early = [g_wdown.reshape(N_DEV, 512, 1024), _cols_to_shards(g_wup), g_wo.reshape(N_DEV, 128, 1024),
             g_wmo.reshape(N_DEV, 128, 1024)]
    dqk, dp, dif, recv_early = _mlstm_bwd(p, puif, qk, cst, nm, dhm, dp, lp, comm=(_AllToAll, early))
    dp, dconv_w, dconv_b = _conv_bwd(p, dqk, convw_f, qk_conv_b, dp, lp)
    g_wglu = _mm(gy, dz, "tn", "mm_g_wglu", out_dtype=BF16)
    dgy = _mm(dz, wglu_f, "nt", "mm_d_gy")
    (dp, dbblk, dcblk, dlam, dd), recv_glu = _s5_bwd(puif, y_pre, dgy, cin, bblk, cblk, lam8, tab_f, tab_b, s5_d, dif, dp,
                                                     lp, comm=(_AllToAll, [_cols_to_shards(g_wglu)]))
    g_win_t, cs_in = _mm(dp, h0b, "tn", "mm_g_win", colsum="a")
    g_win8 = _aligned_to_nat(g_win_t, 0).astype(BF16).reshape(N_DEV, IN_NAT // N_DEV, D)
    dh0, recv_win = _mm(dp, win_t, "nn", "mm_d_h0", add=dpre1, add_scale=ALPHA, comm=(_AllToAll, [g_win8]))
    grad_x, dmeta, dg0, db0 = _ln0_bwd(dh0, x, meta_f, ln0g)

    dlam2 = jnp.sum(dlam, axis=1)
    s5_grads = s5_vjp((dlam2[0:1], dlam2[1:2], dbblk, dcblk))
    small_local = [
        loss_acc[0:1, 0:1], dg0[0:1], db0[0:1], _aligned_to_nat(cs_in[0:1], 1), dconv_b[0:1],
        s5_grads[0], s5_grads[1], s5_grads[2], s5_grads[3], s5_grads[4], s5_grads[5], s5_grads[6],
        dd[0:1], dg_norm[0:1], dg1[0:1], db1[0:1], cs_up[0:1], dg2[0:1], db2[0:1],
        dmeta, dconv_w[0:4]]
    small_shapes = [(), (D,), (D,), (1, IN_NAT), (1, 1024),
                    (1, 32, 64), (1, 32, 64), (1, 32), (1, 32, 64, 16), (1, 32, 64, 16), (1, 32, 16, 64), (1, 32, 16, 64),
                    (1, 512), (1, 1024), (1, 1024), (1, 1024), (1, D_FF), (1, 1024), (1, 1024),
                    (N_META, D), (4, 1024)]
    red = _unpack(_allreduce_small(_pack(small_local)), small_shapes)
    loss = red[0]
    g_meta = lax.dynamic_slice_in_dim(red[19], me * 128, 128, axis=1)
    g_convw = lax.dynamic_slice_in_dim(red[20], me * 128, 128, axis=1)[None]
    small_g = red[1:19] + [g_meta, g_convw]
    small_w = [ln0_g, ln0_b, b_in, qk_conv_b, s5_lambda_re, s5_lambda_im, s5_log_dt, s5_b_re, s5_b_im,
               s5_c_re, s5_c_im, s5_d, m_norm_g, ln1_g, ln1_b, b_up, ln2_g, ln2_b, meta_tokens, qk_conv_w]
    small_m = [m_ln0_g, m_ln0_b, m_b_in, m_qk_conv_b, m_s5_lambda_re, m_s5_lambda_im, m_s5_log_dt, m_s5_b_re,
               m_s5_b_im, m_s5_c_re, m_s5_c_im, m_s5_d, m_m_norm_g, m_ln1_g, m_ln1_b, m_b_up, m_ln2_g, m_ln2_b,
               m_meta_tokens, m_qk_conv_w]
    small_v = [v_ln0_g, v_ln0_b, v_b_in, v_qk_conv_b, v_s5_lambda_re, v_s5_lambda_im, v_s5_log_dt, v_s5_b_re,
               v_s5_b_im, v_s5_c_re, v_s5_c_im, v_s5_d, v_m_norm_g, v_ln1_g, v_ln1_b, v_b_up, v_ln2_g, v_ln2_b,
               v_meta_tokens, v_qk_conv_w]
    shapes_w = [tuple(w.shape) for w in small_w]
    small_g = [g.reshape(s) for g, s in zip(small_g, shapes_w)]
    sd, sm2, sv2 = _adamw_small(small_g, small_w, small_m, small_v)

    names = ["w_in", "s5_w_glu", "m_w_out", "w_o", "w_up", "w_down"]
    recv = [recv_win, recv_glu[0], recv_early[3], recv_early[2], recv_early[1], recv_early[0]]
    big_m = [m_w_in[0].T, m_s5_w_glu[0], m_m_w_out[0], m_w_o[0], m_w_up[0], m_w_down[0]]
    big_v = [v_w_in[0].T, v_s5_w_glu[0], v_m_w_out[0], v_w_o[0], v_w_up[0], v_w_down[0]]
    big_w = [w_in[0].T, s5_w_glu[0], m_w_out[0], w_o[0], w_up[0], w_down[0]]
    big_out = [_adamw_big(r, w, m, v, "adamw_" + nm_) for r, w, m, v, nm_ in zip(recv, big_w, big_m, big_v, names)]
    big_out[0] = [o.T for o in big_out[0]]

    order = ["meta_tokens", "ln0_g", "ln0_b", "w_in", "b_in", "qk_conv_w", "qk_conv_b", "s5_lambda_re", "s5_lambda_im",
             "s5_log_dt", "s5_b_re", "s5_b_im", "s5_c_re", "s5_c_im", "s5_d", "s5_w_glu", "m_norm_g", "m_w_out", "w_o",
             "ln1_g", "ln1_b", "w_up", "b_up", "w_down", "ln2_g", "ln2_b"]
    small_names = ["ln0_g", "ln0_b", "b_in", "qk_conv_b", "s5_lambda_re", "s5_lambda_im", "s5_log_dt", "s5_b_re",
                   "s5_b_im", "s5_c_re", "s5_c_im", "s5_d", "m_norm_g", "ln1_g", "ln1_b", "b_up", "ln2_g", "ln2_b",
                   "meta_tokens", "qk_conv_w"]
    res = {}
    for i, n in enumerate(small_names):
        res[n] = (small_g[i], sd[i], sm2[i], sv2[i])
    for i, n in enumerate(names):
        res[n] = tuple(o[None] for o in big_out[i])
    outs = [loss, grad_x]
    for kind in range(4):
        outs += [res[n][kind] for n in order]
    return tuple(outs)
```

```python
import functools
import math

import jax
import jax.numpy as jnp
from jax import lax
from jax.experimental import pallas as pl
from jax.experimental.pallas import tpu as pltpu

F32 = jnp.float32
BF16 = jnp.bfloat16

D_MODEL = 1024
N_META = 16
CHUNK = 128
PAD_ROWS = CHUNK - N_META
S5_WIDTH = 512
S5_GROUP = 16
S5_GROUPS = 32
S5_STATE = 64
S5_COLS = 2 * S5_GROUPS * S5_STATE
S5_BLK = 4
M_HEADS = 4
M_DK = 128
M_DV = 256
D_FF = 4096
N_DEV = 8
ALPHA = 2.0 ** 0.25
LN_EPS = 1e-5
IN_NAT = 5640
P_V, P_O, P_GS, P_GM, P_QK, P_U, P_IF, PW = 0, 1024, 2048, 3072, 4096, 5120, 5632, 5760
N_U, N_Q, N_K, N_V, N_O, N_I, N_GS, N_GM = 0, 512, 1024, 1536, 2560, 3584, 3592, 4616

ADAM_LR, ADAM_B1, ADAM_B2, ADAM_EPS, ADAM_WD, ADAM_STEP = 0.001, 0.9, 0.999, 1e-08, 0.01, 10

VMEM_LIMIT = 56 * 1024 * 1024
MXU_WIDTH = 256
MESH = pl.DeviceIdType.MESH


def _pick(n, cands):
    for c in cands:
        if n % c == 0:
            return c
    raise ValueError(f"no tile for {n} among {cands}")


def _cparams(sem):
    return pltpu.CompilerParams(dimension_semantics=sem, vmem_limit_bytes=VMEM_LIMIT)


def _dot(a, b, ca, cb):
    return lax.dot_general(a, b, (((ca,), (cb,)), ((), ())), preferred_element_type=F32)


def _sigmoid(x):
    return 1.0 / (1.0 + jnp.exp(-x))


def _peer(k):
    x, y, c = lax.axis_index("x"), lax.axis_index("y"), lax.axis_index("c")
    px = 1 - x if k & 4 else x
    py = 1 - y if k & 2 else y
    pc = 1 - c if k & 1 else c
    return (px, py, pc), 4 * px + 2 * py + pc


def _my_id():
    return 4 * lax.axis_index("x") + 2 * lax.axis_index("y") + lax.axis_index("c")


def _hbm_call(body, name, arrs, out_shape, n_remote):
    n = len(arrs)
    return pl.pallas_call(
        body, name=name,
        out_shape=tuple(out_shape),
        in_specs=[pl.BlockSpec(memory_space=pl.ANY)] * n,
        out_specs=tuple([pl.BlockSpec(memory_space=pl.ANY)] * len(out_shape)),
        scratch_shapes=[pltpu.SemaphoreType.DMA((n, n_remote)),
                        pltpu.SemaphoreType.DMA((n, n_remote)),
                        pltpu.SemaphoreType.DMA((n,))],
    )(*arrs)


class _Gather:
    def __init__(self, ins, outs, send_sems, recv_sems, local_sems):
        self.ins, self.outs, self.n = ins, outs, len(ins)
        self.send_sems, self.recv_sems, self.local_sems = send_sems, recv_sems, local_sems
        x, y, c = lax.axis_index("x"), lax.axis_index("y"), lax.axis_index("c")
        self.c = c
        self.me, self.sibling = (x, y, c), (x, y, 1 - c)
        self.chips = [(1 - x, y), (x, 1 - y), (1 - x, 1 - y)]

    def slot(self, a, dev):
        return self.outs[a].at[4 * dev[0] + 2 * dev[1] + dev[2]]

    def copy(self, a, k, block, to, own=False):
        return pltpu.make_async_remote_copy(
            src_ref=self.ins[a] if own else self.slot(a, block), dst_ref=self.slot(a, block),
            send_sem=self.send_sems.at[a, k], recv_sem=self.recv_sems.at[a, k],
            device_id=to, device_id_type=MESH)

    def first_sends(self, a):
        return [self.copy(a, 0, self.me, self.sibling, own=True)] + [
            self.copy(a, 1 + j, self.me, (*chip, self.c), own=True) for j, chip in enumerate(self.chips)]

    def start(self):
        for a in range(self.n):
            pltpu.make_async_copy(self.ins[a], self.slot(a, self.me), self.local_sems.at[a]).start()
            for cp in self.first_sends(a):
                cp.start()

    def finish(self):
        forwards = []
        for j, chip in enumerate(self.chips):
            for a in range(self.n):
                self.copy(a, 1 + j, (*chip, self.c), self.me).wait_recv()
                fwd = self.copy(a, 4 + j, (*chip, self.c), self.sibling)
                fwd.start()
                forwards.append(fwd)
        for a in range(self.n):
            self.copy(a, 0, self.sibling, self.me).wait_recv()
            for j, chip in enumerate(self.chips):
                self.copy(a, 4 + j, (*chip, 1 - self.c), self.me).wait_recv()
        for a in range(self.n):
            for cp in self.first_sends(a):
                cp.wait_send()
            pltpu.make_async_copy(self.ins[a], self.slot(a, self.me), self.local_sems.at[a]).wait()
        for cp in forwards:
            cp.wait_send()

    @staticmethod
    def out_shapes(arrs):
        return [jax.ShapeDtypeStruct((N_DEV,) + tuple(a.shape), a.dtype) for a in arrs]


class _AllToAll:
    def __init__(self, ins, outs, send_sems, recv_sems, local_sems):
        self.ins, self.outs, self.n = ins, outs, len(ins)
        self.send_sems, self.recv_sems, self.local_sems = send_sems, recv_sems, local_sems
        self.me = _my_id()

    def copy(self, a, k, landing):
        peer, pid = _peer(k)
        return pltpu.make_async_remote_copy(
            src_ref=self.ins[a].at[pid], dst_ref=self.outs[a].at[pid if landing else self.me],
            send_sem=self.send_sems.at[a, k - 1], recv_sem=self.recv_sems.at[a, k - 1],
            device_id=peer, device_id_type=MESH)

    def local(self, a):
        return pltpu.make_async_copy(self.ins[a].at[self.me], self.outs[a].at[self.me], self.local_sems.at[a])

    def start(self):
        for a in range(self.n):
            self.local(a).start()
            for k in range(1, N_DEV):
                self.copy(a, k, False).start()

    def finish(self):
        for a in range(self.n):
            for k in range(1, N_DEV):
                self.copy(a, k, True).wait_recv()
        for a in range(self.n):
            for k in range(1, N_DEV):
                self.copy(a, k, False).wait_send()
            self.local(a).wait()

    @staticmethod
    def out_shapes(arrs):
        return [jax.ShapeDtypeStruct(tuple(a.shape), a.dtype) for a in arrs]


def _comm_scratch(n):
    return [pltpu.SemaphoreType.DMA((n, N_DEV - 1)), pltpu.SemaphoreType.DMA((n, N_DEV - 1)),
            pltpu.SemaphoreType.DMA((n,))]


def _ride_call(body, name, grid, in_specs, out_specs, out_shape, scratch, args, comm=None, aliases=None):
    n_in, n_out = len(in_specs), len(out_specs)
    in_specs, out_specs, out_shape = list(in_specs), list(out_specs), list(out_shape)
    scratch, args = list(scratch), list(args)
    kernel_fn = body
    if comm is not None:
        cls, arrs = comm
        n = len(arrs)

        def kernel_fn(*refs):
            ins, cin = refs[:n_in], refs[n_in:n_in + n]
            outs = refs[n_in + n:n_in + n + n_out]
            cout = refs[n_in + n + n_out:n_in + 2 * n + n_out]
            own_scratch, sems = refs[n_in + 2 * n + n_out:-3], refs[-3:]
            exchange = cls(cin, cout, *sems)
            ids = [pl.program_id(d) for d in range(len(grid))]
            first = functools.reduce(lambda a, b: a & b, [i == 0 for i in ids])
            last = functools.reduce(lambda a, b: a & b, [i == g - 1 for i, g in zip(ids, grid)])

            @pl.when(first)
            def _():
                exchange.start()

            body(*ins, *outs, *own_scratch)

            @pl.when(last)
            def _():
                exchange.finish()

        in_specs += [pl.BlockSpec(memory_space=pl.ANY)] * n
        args += list(arrs)
        out_specs += [pl.BlockSpec(memory_space=pl.ANY)] * n
        out_shape += cls.out_shapes(arrs)
        scratch += _comm_scratch(n)
    res = pl.pallas_call(
        kernel_fn, name=name, grid=grid,
        in_specs=in_specs, out_specs=tuple(out_specs), out_shape=tuple(out_shape),
        scratch_shapes=scratch, input_output_aliases=aliases or {},
        compiler_params=_cparams(("arbitrary",) * len(grid)),
    )(*args)
    return list(res[:n_out]), list(res[n_out:])


def _shard_tile(R, C):
    if R % 128 == 0:
        return 128, C
    return R, _pick(C, (256, 128))


def _allreduce_small(pack):
    rows = pack.shape[0]

    def gather(*refs):
        g = _Gather(refs[:1], refs[1:2], *refs[2:])
        g.start()
        g.finish()

    slots = _hbm_call(gather, "allreduce_gather", [pack], _Gather.out_shapes([pack]), N_DEV - 1)[0]

    def body(s_ref, o_ref):
        acc = s_ref[0]
        for s in range(1, N_DEV):
            acc = acc + s_ref[s]
        o_ref[...] = acc

    return pl.pallas_call(
        body, name="allreduce_sum", grid=(1,),
        in_specs=[pl.BlockSpec((N_DEV, rows, 128), lambda i: (0, 0, 0))],
        out_specs=pl.BlockSpec((rows, 128), lambda i: (0, 0)),
        out_shape=jax.ShapeDtypeStruct((rows, 128), F32),
        compiler_params=_cparams(("arbitrary",)),
    )(slots)


def _mm_tiles(M, N, K, mode, out_bytes, n_extra_f32, a_bytes, b_bytes):
    budget = 48 * 1024 * 1024
    tms = [t for t in (1664, 1408, 1152, 1024, 640, 512, 256, 128) if M % t == 0]
    tns = [t for t in (1152, 1024, 640, 512, 384, 256, 128) if N % t == 0]
    if mode == "tn":
        tks = [t for t in (1664, 1408, 640, 512, 256, 128) if K % t == 0]
    else:
        tks = [K] if K <= 1152 else [t for t in (1152, 1024, 640, 512) if K % t == 0]
    best = None
    for tm in tms:
        for tn in tns:
            for tk in tks:
                nk = K // tk
                use = 2 * (tm * tk * a_bytes + tk * tn * b_bytes) + 2 * tm * tn * out_bytes
                use += 2 * n_extra_f32 * tm * tn * 4 + tm * tn * 4 * (2 if nk > 1 else 1)
                if use > budget:
                    continue
                score = (tm * tn * tk, tm * tn)
                if best is None or score > best[0]:
                    best = (score, (tm, tn, tk))
    assert best is not None, (M, N, K, mode)
    return best[1]


def _mm(a, b, mode, name, *, bias=None, add=None, add_scale=1.0, sqrt_gate_of=None,
        relu2_out=False, colsum=None, out_dtype=F32, comm=None, b_rows=None):
    if mode == "nn":
        (M, K), (K2, N) = a.shape, b.shape
    elif mode == "nt":
        (M, K), (N, K2) = a.shape, b.shape
        if b_rows is not None:
            N = b_rows[1]
    else:
        (K, M), (K2, N) = a.shape, b.shape
    assert K == K2, (a.shape, b.shape, mode)
    assert b_rows is None or mode == "nt"
    has_bias, has_add, has_gate = bias is not None, add is not None, sqrt_gate_of is not None
    tm, tn, tk = _mm_tiles(M, N, K, mode, jnp.dtype(out_dtype).itemsize, int(has_add) + int(has_gate),
                           a.dtype.itemsize, b.dtype.itemsize)
    nk = K // tk
    assert colsum is None or mode == "tn"
    assert colsum != "a" or N == tn
    comm_cls, comm_arrs = comm if comm is not None else (None, [])
    nc = len(comm_arrs)
    grid = (N // tn, M // tm, nk)

    def body(*refs):
        it = iter(refs)
        a_ref, b_ref = next(it), next(it)
        bias_ref = next(it) if has_bias else None
        add_ref = next(it) if has_add else None
        gate_ref = next(it) if has_gate else None
        comm_ins = [next(it) for _ in range(nc)]
        o_ref = next(it)
        cs_ref = next(it) if colsum else None
        comm_outs = [next(it) for _ in range(nc)]
        acc_ref = next(it) if nk > 1 else None
        j, i, k = pl.program_id(0), pl.program_id(1), pl.program_id(2)
        if nc:
            exchange = comm_cls(comm_ins, comm_outs, next(it), next(it), next(it))

            @pl.when((j == 0) & (i == 0) & (k == 0))
            def _():
                exchange.start()

        blocks = [slice(n0, min(n0 + MXU_WIDTH, tn)) for n0 in range(0, tn, MXU_WIDTH)]
        av = a_ref[...].astype(BF16)

        def part(cols):
            if mode == "nn":
                return _dot(av, b_ref[:, cols].astype(BF16), 1, 0)
            if mode == "nt":
                return _dot(av, b_ref[cols, :].astype(BF16), 1, 1)
            return _dot(av, b_ref[:, cols].astype(BF16), 0, 0)

        if colsum == "b":
            @pl.when((i == 0) & (k == 0))
            def _():
                cs_ref[...] = jnp.zeros_like(cs_ref)

            @pl.when(i == 0)
            def _():
                cs_ref[0:1, :] += jnp.sum(b_ref[...].astype(F32), axis=0, keepdims=True)
        if colsum == "a":
            @pl.when(k == 0)
            def _():
                cs_ref[...] = jnp.zeros_like(cs_ref)

            cs_ref[0:1, :] += jnp.sum(a_ref[...].astype(F32), axis=0, keepdims=True)

        def finish(cols, r):
            if has_bias:
                r = r + bias_ref[:, cols]
            if has_add:
                r = r + add_scale * add_ref[:, cols]
            if has_gate:
                r = r * (2.0 * jnp.sqrt(gate_ref[:, cols].astype(F32)))
            if relu2_out:
                r = jnp.square(jnp.maximum(r, 0.0))
            o_ref[:, cols] = r.astype(out_dtype)

        if nk == 1:
            for cols in blocks:
                finish(cols, part(cols))
        else:
            @pl.when(k == 0)
            def _():
                for cols in blocks:
                    acc_ref[:, cols] = part(cols)

            @pl.when((k > 0) & (k < nk - 1))
            def _():
                for cols in blocks:
                    acc_ref[:, cols] += part(cols)

            @pl.when(k == nk - 1)
            def _():
                for cols in blocks:
                    finish(cols, acc_ref[:, cols] + part(cols))

        if nc:
            @pl.when((j == grid[0] - 1) & (i == grid[1] - 1) & (k == nk - 1))
            def _():
                exchange.finish()

    if mode == "nn":
        a_spec = pl.BlockSpec((tm, tk), lambda j, i, k: (i, k))
        b_spec = pl.BlockSpec((tk, tn), lambda j, i, k: (k, j))
    elif mode == "nt":
        a_spec = pl.BlockSpec((tm, tk), lambda j, i, k: (i, k))
        b_off = 0 if b_rows is None else b_rows[0] // tn
        assert b_rows is None or b_rows[0] % tn == 0
        b_spec = pl.BlockSpec((tn, tk), lambda j, i, k: (j + b_off, k))
    else:
        a_spec = pl.BlockSpec((tk, tm), lambda j, i, k: (k, i))
        b_spec = pl.BlockSpec((tk, tn), lambda j, i, k: (k, j))
    in_specs, args = [a_spec, b_spec], [a, b]
    if has_bias:
        in_specs.append(pl.BlockSpec((1, tn), lambda j, i, k: (0, j)))
        args.append(bias)
    if has_add:
        in_specs.append(pl.BlockSpec((tm, tn), lambda j, i, k: (i, j)))
        args.append(add)
    if has_gate:
        in_specs.append(pl.BlockSpec((tm, tn), lambda j, i, k: (i, j)))
        args.append(sqrt_gate_of)
    out_shape = [jax.ShapeDtypeStruct((M, N), out_dtype)]
    out_specs = [pl.BlockSpec((tm, tn), lambda j, i, k: (i, j))]
    if colsum == "b":
        out_shape.append(jax.ShapeDtypeStruct((8, N), F32))
        out_specs.append(pl.BlockSpec((8, tn), lambda j, i, k: (0, j)))
    if colsum == "a":
        out_shape.append(jax.ShapeDtypeStruct((8, M), F32))
        out_specs.append(pl.BlockSpec((8, tm), lambda j, i, k: (0, i)))
    scratch = [pltpu.VMEM((tm, tn), F32)] if nk > 1 else []
    if nc:
        in_specs += [pl.BlockSpec(memory_space=pl.ANY)] * nc
        args += list(comm_arrs)
        out_specs += [pl.BlockSpec(memory_space=pl.ANY)] * nc
        out_shape += comm_cls.out_shapes(comm_arrs)
        scratch += _comm_scratch(nc)
    res = pl.pallas_call(
        body, name=name, grid=grid,
        in_specs=in_specs, out_specs=tuple(out_specs), out_shape=tuple(out_shape),
        scratch_shapes=scratch,
        compiler_params=_cparams(("arbitrary", "arbitrary", "arbitrary")),
    )(*args)
    return res if len(res) > 1 else res[0]


def _ln_rows(v, g, b):
    mu = jnp.mean(v, axis=-1, keepdims=True)
    xc = v - mu
    var = jnp.mean(xc * xc, axis=-1, keepdims=True)
    return xc * lax.rsqrt(var + LN_EPS) * g + b


def _ln_bwd_rows(dy, v, g):
    mu = jnp.mean(v, axis=-1, keepdims=True)
    xc = v - mu
    var = jnp.mean(xc * xc, axis=-1, keepdims=True)
    rstd = lax.rsqrt(var + LN_EPS)
    xhat = xc * rstd
    dxh = dy * g
    dv = rstd * (dxh - jnp.mean(dxh, axis=-1, keepdims=True)
                 - xhat * jnp.mean(dxh * xhat, axis=-1, keepdims=True))
    return dv, xhat


def _real_tile(S):
    return _pick(S, (512, 256, 128, 64))


def _real_rows(rb, ncols, lp):
    return pl.BlockSpec((pl.Element(rb), pl.Element(ncols)),
                        lambda bb, j: (pl.multiple_of(bb * lp + CHUNK + j * rb, CHUNK), 0))


def _head_rows(ncols, lp):
    return pl.BlockSpec((CHUNK, ncols), lambda bb: (bb * (lp // CHUNK), 0))


def _ln0_fwd(x, g, b, comm=None):
    B, S, D = x.shape
    lp = S + CHUNK
    rb = _real_tile(S)

    def body(x_ref, g_ref, b_ref, h_ref, hb_ref):
        y = _ln_rows(x_ref[0], g_ref[...], b_ref[...])
        h_ref[...] = y
        hb_ref[...] = y.astype(BF16)

    vec = pl.BlockSpec((1, D), lambda bb, j: (0, 0))
    return _ride_call(
        body, "ln0_fwd", (B, S // rb),
        [pl.BlockSpec((1, rb, D), lambda bb, j: (bb, j, 0)), vec, vec],
        [_real_rows(rb, D, lp), _real_rows(rb, D, lp)],
        [jax.ShapeDtypeStruct((B * lp, D), F32), jax.ShapeDtypeStruct((B * lp, D), BF16)],
        [], (x, g, b), comm=comm)


def _ln0_head(meta, g, b, h, hb, lp):
    D = meta.shape[1]
    B = h.shape[0] // lp

    def head(meta_ref, g_ref, b_ref, hin_ref, hbin_ref, h_ref, hb_ref):
        m = _ln_rows(meta_ref[...], g_ref[...], b_ref[...])
        h_ref[0:PAD_ROWS, :] = jnp.zeros((PAD_ROWS, D), F32)
        h_ref[PAD_ROWS:CHUNK, :] = m
        hb_ref[0:PAD_ROWS, :] = jnp.zeros((PAD_ROWS, D), BF16)
        hb_ref[PAD_ROWS:CHUNK, :] = m.astype(BF16)

    vec1 = pl.BlockSpec((1, D), lambda bb: (0, 0))
    anyspec = pl.BlockSpec(memory_space=pl.ANY)
    return pl.pallas_call(
        head, name="ln0_head", grid=(B,),
        in_specs=[pl.BlockSpec((N_META, D), lambda bb: (0, 0)), vec1, vec1, anyspec, anyspec],
        out_specs=(_head_rows(D, lp), _head_rows(D, lp)),
        out_shape=(jax.ShapeDtypeStruct((B * lp, D), F32), jax.ShapeDtypeStruct((B * lp, D), BF16)),
        input_output_aliases={3: 0, 4: 1},
        compiler_params=_cparams(("arbitrary",)),
    )(meta, g, b, h, hb)


def _ln0_bwd(dh0, x, meta, g):
    B, S, D = x.shape
    lp = S + CHUNK
    rb = _real_tile(S)

    def body(dh_ref, x_ref, g_ref, dx_ref, dg_ref, db_ref):
        @pl.when((pl.program_id(0) == 0) & (pl.program_id(1) == 0))
        def _():
            dg_ref[...] = jnp.zeros_like(dg_ref)
            db_ref[...] = jnp.zeros_like(db_ref)

        dy = dh_ref[...]
        dv, xhat = _ln_bwd_rows(dy, x_ref[0], g_ref[...])
        dx_ref[0] = dv
        dg_ref[0:1, :] += jnp.sum(dy * xhat, axis=0, keepdims=True)
        db_ref[0:1, :] += jnp.sum(dy, axis=0, keepdims=True)

    const = lambda bb, j: (0, 0)
    xblk = pl.BlockSpec((1, rb, D), lambda bb, j: (bb, j, 0))
    acc_shape = jax.ShapeDtypeStruct((8, D), F32)
    dx, dg, db = pl.pallas_call(
        body, name="ln0_bwd", grid=(B, S // rb),
        in_specs=[_real_rows(rb, D, lp), xblk, pl.BlockSpec((1, D), const)],
        out_specs=(xblk, pl.BlockSpec((8, D), const), pl.BlockSpec((8, D), const)),
        out_shape=(jax.ShapeDtypeStruct((B, S, D), F32), acc_shape, acc_shape),
        compiler_params=_cparams(("arbitrary", "arbitrary")),
    )(dh0, x, g)

    def head(dh_ref, meta_ref, g_ref, dmeta_ref, dg_ref, db_ref):
        @pl.when(pl.program_id(0) == 0)
        def _():
            dmeta_ref[...] = jnp.zeros_like(dmeta_ref)
            dg_ref[...] = jnp.zeros_like(dg_ref)
            db_ref[...] = jnp.zeros_like(db_ref)

        dy = dh_ref[PAD_ROWS:CHUNK, :]
        dv, xhat = _ln_bwd_rows(dy, meta_ref[...], g_ref[...])
        dmeta_ref[...] += dv
        dg_ref[0:1, :] += jnp.sum(dy * xhat, axis=0, keepdims=True)
        db_ref[0:1, :] += jnp.sum(dy, axis=0, keepdims=True)

    c1 = lambda bb: (0, 0)
    dmeta, dgm, dbm = pl.pallas_call(
        head, name="ln0_bwd_head", grid=(B,),
        in_specs=[_head_rows(D, lp), pl.BlockSpec((N_META, D), c1), pl.BlockSpec((1, D), c1)],
        out_specs=(pl.BlockSpec((N_META, D), c1), pl.BlockSpec((8, D), c1), pl.BlockSpec((8, D), c1)),
        out_shape=(jax.ShapeDtypeStruct((N_META, D), F32), acc_shape, acc_shape),
        compiler_params=_cparams(("arbitrary",)),
    )(dh0, meta, g)
    return dx, dmeta, dg + dgm, db + dbm


def _ln_res_fwd(h_prev, r, g, b, name):
    T, D = h_prev.shape
    tr = _pick(T, (384, 320, 256, 128, 64))

    def body(hp_ref, r_ref, g_ref, b_ref, pre_ref, h_ref, hb_ref):
        pre = ALPHA * hp_ref[...] + r_ref[...]
        y = _ln_rows(pre, g_ref[...], b_ref[...])
        pre_ref[...] = pre
        h_ref[...] = y
        hb_ref[...] = y.astype(BF16)

    row = pl.BlockSpec((tr, D), lambda i: (i, 0))
    vec = pl.BlockSpec((1, D), lambda i: (0, 0))
    return pl.pallas_call(
        body, name=name, grid=(T // tr,),
        in_specs=[row, row, vec, vec], out_specs=(row, row, row),
        out_shape=(jax.ShapeDtypeStruct((T, D), F32), jax.ShapeDtypeStruct((T, D), F32),
                   jax.ShapeDtypeStruct((T, D), BF16)),
        compiler_params=_cparams(("arbitrary",)),
    )(h_prev, r, g, b)


def _ln_bwd(dh, pre, g, name):
    T, D = dh.shape
    tr = _pick(T, (384, 320, 256, 128, 64))

    def body(dh_ref, pre_ref, g_ref, dp_ref, dpb_ref, dg_ref, db_ref):
        @pl.when(pl.program_id(0) == 0)
        def _():
            dg_ref[...] = jnp.zeros_like(dg_ref)
            db_ref[...] = jnp.zeros_like(db_ref)

        dy = dh_ref[...]
        dv, xhat = _ln_bwd_rows(dy, pre_ref[...], g_ref[...])
        dp_ref[...] = dv
        dpb_ref[...] = dv.astype(BF16)
        dg_ref[0:1, :] += jnp.sum(dy * xhat, axis=0, keepdims=True)
        db_ref[0:1, :] += jnp.sum(dy, axis=0, keepdims=True)

    row = pl.BlockSpec((tr, D), lambda i: (i, 0))
    vec = pl.BlockSpec((1, D), lambda i: (0, 0))
    acc = pl.BlockSpec((8, D), lambda i: (0, 0))
    return pl.pallas_call(
        body, name=name, grid=(T // tr,),
        in_specs=[row, row, vec], out_specs=(row, row, acc, acc),
        out_shape=(jax.ShapeDtypeStruct((T, D), F32), jax.ShapeDtypeStruct((T, D), BF16),
                   jax.ShapeDtypeStruct((8, D), F32), jax.ShapeDtypeStruct((8, D), F32)),
        compiler_params=_cparams(("arbitrary",)),
    )(dh, pre, g)


def _ln2_loss(h1, ff, g, b, target):
    T, D = h1.shape
    B, S, _ = target.shape
    lp = S + CHUNK
    rb = _real_tile(S)

    def body(h_ref, ff_ref, g_ref, b_ref, t_ref, loss_ref, dp_ref, dpb_ref, dg_ref, db_ref):
        @pl.when((pl.program_id(0) == 0) & (pl.program_id(1) == 0))
        def _():
            loss_ref[...] = jnp.zeros_like(loss_ref)
            dg_ref[...] = jnp.zeros_like(dg_ref)
            db_ref[...] = jnp.zeros_like(db_ref)

        pre = ALPHA * h_ref[...] + ff_ref[...]
        gg = g_ref[...]
        y = _ln_rows(pre, gg, b_ref[...])
        err = y - t_ref[0]
        loss_ref[0:1, 0:1] += 0.5 * jnp.sum(jnp.mean(err * err, axis=-1, keepdims=True), axis=0, keepdims=True)
        dy = err * (1.0 / D)
        dv, xhat = _ln_bwd_rows(dy, pre, gg)
        dp_ref[...] = dv
        dpb_ref[...] = dv.astype(BF16)
        dg_ref[0:1, :] += jnp.sum(dy * xhat, axis=0, keepdims=True)
        db_ref[0:1, :] += jnp.sum(dy, axis=0, keepdims=True)

    row = _real_rows(rb, D, lp)
    const = lambda bb, j: (0, 0)
    loss, dp, dpb, dg, db = pl.pallas_call(
        body, name="ln2_loss", grid=(B, S // rb),
        in_specs=[row, row, pl.BlockSpec((1, D), const), pl.BlockSpec((1, D), const),
                  pl.BlockSpec((1, rb, D), lambda bb, j: (bb, j, 0))],
        out_specs=(pl.BlockSpec((8, 128), const), row, row,
                   pl.BlockSpec((8, D), const), pl.BlockSpec((8, D), const)),
        out_shape=(jax.ShapeDtypeStruct((8, 128), F32),
                   jax.ShapeDtypeStruct((T, D), F32), jax.ShapeDtypeStruct((T, D), BF16),
                   jax.ShapeDtypeStruct((8, D), F32), jax.ShapeDtypeStruct((8, D), F32)),
        compiler_params=_cparams(("arbitrary", "arbitrary")),
    )(h1, ff, g, b, target)

    def head(dpin_ref, dpbin_ref, dp_ref, dpb_ref):
        dp_ref[...] = jnp.zeros((CHUNK, D), F32)
        dpb_ref[...] = jnp.zeros((CHUNK, D), BF16)

    anyspec = pl.BlockSpec(memory_space=pl.ANY)
    dp, dpb = pl.pallas_call(
        head, name="ln2_head", grid=(B,),
        in_specs=[anyspec, anyspec],
        out_specs=(_head_rows(D, lp), _head_rows(D, lp)),
        out_shape=(jax.ShapeDtypeStruct((T, D), F32), jax.ShapeDtypeStruct((T, D), BF16)),
        input_output_aliases={0: 0, 1: 1},
        compiler_params=_cparams(("arbitrary",)),
    )(dp, dpb)
    return loss, dp, dpb, dg, db


def _s5_prep(lam_re, lam_im, log_dt, b_re, b_im, c_re, c_im):
    dt = jnp.exp(log_dt)[:, None]
    mag = jnp.exp(lam_re * dt)
    ar = mag * jnp.cos(lam_im * dt)
    ai = mag * jnp.sin(lam_im * dt)
    nr, ni = ar - 1.0, ai
    den = lam_re * lam_re + lam_im * lam_im
    cr = (nr * lam_re + ni * lam_im) / den
    ci = (ni * lam_re - nr * lam_im) / den
    bbr = cr[..., None] * b_re - ci[..., None] * b_im
    bbi = cr[..., None] * b_im + ci[..., None] * b_re
    eye = jnp.eye(8, dtype=F32)
    bb = jnp.stack([bbr, bbi]).reshape(2, S5_BLK, 8, S5_STATE, S5_GROUP)
    bblk = jnp.einsum("rbgph,gj->bghrjp", bb, eye).reshape(S5_BLK, 128, 1024)
    cc = jnp.stack([c_re, -c_im]).reshape(2, S5_BLK, 8, S5_GROUP, S5_STATE)
    cblk = jnp.einsum("rbghp,gj->brjpgh", cc, eye).reshape(S5_BLK, 1024, 128)
    return ar.reshape(1, 2048), ai.reshape(1, 2048), bblk, cblk


def _scan_tables(ar, ai):
    pr, pi = [ar], [ai]
    for _ in range(7):
        pr, pi = pr + [pr[-1] * ar - pi[-1] * ai], pi + [pr[-1] * ai + pi[-1] * ar]
    pw_r = jnp.concatenate(pr, axis=0)
    pw_i = jnp.concatenate(pi, axis=0)
    rev_r = jnp.concatenate(pr[::-1], axis=0)
    rev_i = jnp.concatenate(pi[::-1], axis=0)
    row = jnp.arange(8)[:, None]

    def tables(sign, reverse):
        rows = []
        for n, sh in ((0, 1), (1, 2), (3, 4)):
            mask = (row < 8 - sh) if reverse else (row >= sh)
            rows.append(jnp.where(mask, pw_r[n][None, :], 0.0))
            rows.append(jnp.where(mask, sign * pw_i[n][None, :], 0.0))
        cr_ = rev_r if reverse else pw_r
        ci_ = rev_i if reverse else pw_i
        rows += [cr_, sign * ci_]
        return jnp.stack(rows)

    return tables(1.0, False), tables(-1.0, True)


def _seg_scan(s_ref, row0, seg, lam_ref, tab_ref, carry_ref, reverse, cseg_ref=None, extra=None):
    sgn = -1.0 if reverse else 1.0
    take, edge = (0, 7) if reverse else (7, 0)
    rowid = lax.broadcasted_iota(jnp.int32, (8, 128), 0)
    all_pairs = [(blk * 1024 + j * 128, blk * 1024 + j * 128 + 512, blk * 512 + j * 128)
                 for blk in range(S5_BLK) for j in range(4)]

    def rows(it):
        i = (seg - 1 - it) if reverse else it
        return i, pl.multiple_of(row0 + i * 8, 8)

    for half in range(2):
        pairs = all_pairs[8 * half:8 * half + 8]

        def pass1(it, carry):
            _, r0 = rows(it)
            out = []
            for n, (cre, cim, tc) in enumerate(pairs):
                lr = lam_ref[0, :, tc:tc + 128]
                li = sgn * lam_ref[1, :, tc:tc + 128]
                pr, pi = carry[2 * n], carry[2 * n + 1]
                xr = lr * pr - li * pi + s_ref[pl.ds(r0, 8), cre:cre + 128]
                xi = lr * pi + li * pr + s_ref[pl.ds(r0, 8), cim:cim + 128]
                s_ref[pl.ds(r0, 8), cre:cre + 128] = xr
                s_ref[pl.ds(r0, 8), cim:cim + 128] = xi
                out += [xr, xi]
            return tuple(out)

        ends = lax.fori_loop(0, seg, pass1, tuple(jnp.zeros((8, 128), F32) for _ in range(16)))

        start = []
        for n, (cre, cim, tc) in enumerate(pairs):
            xr, xi = ends[2 * n], ends[2 * n + 1]
            for lvl, sh in enumerate((1, 2, 4)):
                lr = tab_ref[2 * lvl, :, tc:tc + 128]
                li = tab_ref[2 * lvl + 1, :, tc:tc + 128]
                shift = (8 - sh) if reverse else sh
                sr = pltpu.roll(xr, shift, 0)
                si = pltpu.roll(xi, shift, 0)
                xr, xi = xr + lr * sr - li * si, xi + lr * si + li * sr
            pr = tab_ref[6, :, tc:tc + 128]
            pi = tab_ref[7, :, tc:tc + 128]
            c_r = carry_ref[:, cre:cre + 128]
            c_i = carry_ref[:, cim:cim + 128]
            er, ei = xr + pr * c_r - pi * c_i, xi + pr * c_i + pi * c_r
            back = 7 if reverse else 1
            in_r = jnp.where(rowid == edge, c_r, pltpu.roll(er, back, 0))
            in_i = jnp.where(rowid == edge, c_i, pltpu.roll(ei, back, 0))
            nr = jnp.sum(jnp.where(rowid == take, er, 0.0), axis=0, keepdims=True)
            ni = jnp.sum(jnp.where(rowid == take, ei, 0.0), axis=0, keepdims=True)
            carry_ref[:, cre:cre + 128] = jnp.broadcast_to(nr, (8, 128))
            carry_ref[:, cim:cim + 128] = jnp.broadcast_to(ni, (8, 128))
            if cseg_ref is not None:
                cseg_ref[0:8, cre:cre + 128] = in_r
                cseg_ref[0:8, cim:cim + 128] = in_i
            start += [in_r, in_i]

        def pass2(it, carry):
            i, r0 = rows(it)
            out = []
            for n, (cre, cim, tc) in enumerate(pairs):
                lr = lam_ref[0, :, tc:tc + 128]
                li = sgn * lam_ref[1, :, tc:tc + 128]
                dr, di = carry[2 * n], carry[2 * n + 1]
                dr, di = lr * dr - li * di, lr * di + li * dr
                xr = s_ref[pl.ds(r0, 8), cre:cre + 128] + dr
                xi = s_ref[pl.ds(r0, 8), cim:cim + 128] + di
                s_ref[pl.ds(r0, 8), cre:cre + 128] = xr
                s_ref[pl.ds(r0, 8), cim:cim + 128] = xi
                if extra is not None:
                    extra(i, cre, cim, tc, xr, xi)
                out += [dr, di]
            return tuple(out)

        lax.fori_loop(0, seg, pass2, tuple(start))


def _to_segments(src_refs, dst_ref, seg, first=None):
    sub = lax.broadcasted_iota(jnp.int32, (8, 1), 0)
    for i in range(seg):
        for c, src in enumerate(src_refs):
            v = src[pl.ds(i, 8, stride=seg), :]
            if first is not None:
                v = jnp.where(first & (sub * seg + i < PAD_ROWS), 0.0, v)
            dst_ref[8 * i:8 * i + 8, c * 128:(c + 1) * 128] = v


def _from_segments(src_ref, dst_ref, seg, fn=None, zero_head=None):
    m = seg // 8
    for j in range(seg):
        for c in range(src_ref.shape[0]):
            v = src_ref[c, pl.ds(64 * (j % m) + j // m, 8, stride=8), :]
            if zero_head is not None and j < PAD_ROWS // 8:
                v = jnp.where(zero_head, 0.0, v)
            dst_ref[8 * j:8 * j + 8, c * 128:(c + 1) * 128] = v if fn is None else fn(v)


def _gelu(x):
    c = math.sqrt(2.0 / math.pi)
    t = jnp.tanh(c * (x + 0.044715 * x * x * x))
    return 0.5 * x * (1.0 + t)


def _gelu_grad(x):
    c = math.sqrt(2.0 / math.pi)
    t = jnp.tanh(c * (x + 0.044715 * x * x * x))
    return 0.5 * (1.0 + t) + 0.5 * x * (1.0 - t * t) * c * (1.0 + 3.0 * 0.044715 * x * x)


def _s5_tile(lp):
    return _pick(lp, (384, 320, 256, 128, 64))


def _s5_fwd(p, bblk, cblk, lam8, tab_f, dskip, lp, comm=None):
    T = p.shape[0]
    ts = _s5_tile(lp)
    seg = ts // 8
    nblk, per_seq = T // ts, lp // ts
    ucol = 0

    def body(u0, u1, u2, u3, b_ref, c_ref, lam_ref, tab_ref, d_ref, y_ref, gy_ref, cin_ref, s_sc, u_sc, y_sc, carry_sc):
        r = pl.program_id(0)
        first = (r % per_seq) == 0

        @pl.when(first)
        def _():
            carry_sc[...] = jnp.zeros_like(carry_sc)

        cin_ref[0] = carry_sc[...]
        _to_segments((u0, u1, u2, u3), u_sc, seg, first)
        u = u_sc[...]
        ub = u.astype(BF16)
        for blk in range(S5_BLK):
            s_sc[:, blk * 1024:(blk + 1) * 1024] = _dot(ub[:, blk * 128:(blk + 1) * 128], b_ref[blk], 1, 0)
        _seg_scan(s_sc, 0, seg, lam_ref, tab_ref, carry_sc, False)
        for blk in range(S5_BLK):
            sb = s_sc[:, blk * 1024:(blk + 1) * 1024].astype(BF16)
            y_sc[blk] = _dot(sb, c_ref[blk], 1, 0) + d_ref[:, blk * 128:(blk + 1) * 128] * u[:, blk * 128:(blk + 1) * 128]
        _from_segments(y_sc, y_ref, seg)
        gy_ref[...] = _gelu(y_ref[...]).astype(BF16)

    def ublock(c):
        return pl.BlockSpec((ts, 128), lambda r: (r, 4 * ucol + c))

    return _ride_call(
        body, "s5_fwd", (nblk,),
        [ublock(0), ublock(1), ublock(2), ublock(3),
         pl.BlockSpec((S5_BLK, 128, 1024), lambda r: (0, 0, 0)),
         pl.BlockSpec((S5_BLK, 1024, 128), lambda r: (0, 0, 0)),
         pl.BlockSpec((2, 8, 2048), lambda r: (0, 0, 0)),
         pl.BlockSpec((8, 8, 2048), lambda r: (0, 0, 0)),
         pl.BlockSpec((1, S5_WIDTH), lambda r: (0, 0))],
        [pl.BlockSpec((ts, S5_WIDTH), lambda r: (r, 0)),
         pl.BlockSpec((ts, S5_WIDTH), lambda r: (r, 0)),
         pl.BlockSpec((1, 8, S5_COLS), lambda r: (r, 0, 0))],
        [jax.ShapeDtypeStruct((T, S5_WIDTH), F32),
         jax.ShapeDtypeStruct((T, S5_WIDTH), BF16),
         jax.ShapeDtypeStruct((nblk, 8, S5_COLS), F32)],
        [pltpu.VMEM((ts, S5_COLS), F32), pltpu.VMEM((ts, S5_WIDTH), F32),
         pltpu.VMEM((S5_BLK, ts, 128), F32), pltpu.VMEM((8, S5_COLS), F32)],
        (p, p, p, p, bblk.astype(BF16), cblk.astype(BF16), lam8, tab_f, dskip), comm=comm)


def _s5_bwd(p, y_pre, dgy, cin, bblk, cblk, lam8, tab_f, tab_b, dskip, dif, dp, lp, comm=None):
    T = p.shape[0]
    ts = _s5_tile(lp)
    seg = ts // 8
    nblk, per_seq = T // ts, lp // ts
    ucol = 0
    assert P_IF == P_U + S5_WIDTH and P_U % (S5_WIDTH + 128) == 0

    def body(u0, u1, u2, u3, y0, y1, y2, y3, g0, g1, g2, g3, cin_ref, b_ref, bt_ref, c_ref, ct_ref, lam_ref, tf_ref,
             tb_ref, d_ref, dif_ref, dpin_ref, du_ref, dbb_ref, dcb_ref, dlam_ref, dd_ref, s_sc, a_sc, u_sc, dy_sc,
             y_sc, w_sc, carry_sc, carry_b):
        t = pl.program_id(0)
        r = nblk - 1 - t
        first = (r % per_seq) == 0
        last = (r % per_seq) == per_seq - 1

        @pl.when(t == 0)
        def _():
            dbb_ref[...] = jnp.zeros_like(dbb_ref)
            dcb_ref[...] = jnp.zeros_like(dcb_ref)
            dlam_ref[...] = jnp.zeros_like(dlam_ref)
            dd_ref[...] = jnp.zeros_like(dd_ref)

        @pl.when(last)
        def _():
            carry_b[...] = jnp.zeros_like(carry_b)

        carry_sc[...] = cin_ref[0]
        _to_segments((u0, u1, u2, u3), u_sc, seg, first)
        u = u_sc[...]
        ub = u.astype(BF16)
        for blk in range(S5_BLK):
            s_sc[8:8 + ts, blk * 1024:(blk + 1) * 1024] = _dot(ub[:, blk * 128:(blk + 1) * 128], b_ref[blk], 1, 0)
        _seg_scan(s_sc, 8, seg, lam_ref, tf_ref, carry_sc, False, cseg_ref=s_sc)

        _to_segments((g0, g1, g2, g3), dy_sc, seg)
        _to_segments((y0, y1, y2, y3), y_sc, seg)
        dy = dy_sc[...] * _gelu_grad(y_sc[...])
        dy_sc[...] = dy
        dyb = dy.astype(BF16)
        dd_ref[0:1, :] += jnp.sum(dy * u, axis=0, keepdims=True)
        for blk in range(S5_BLK):
            a_sc[:, blk * 1024:(blk + 1) * 1024] = _dot(dyb[:, blk * 128:(blk + 1) * 128], ct_ref[blk], 1, 0)
            sb = s_sc[8:8 + ts, blk * 1024:(blk + 1) * 1024].astype(BF16)
            dcb_ref[blk] += _dot(sb, dyb[:, blk * 128:(blk + 1) * 128], 0, 0)

        def lam_grad(i, cre, cim, tc, a_r, a_i):
            r0 = pl.multiple_of(i * 8, 8)
            pr = s_sc[pl.ds(r0, 8), cre:cre + 128]
            pi = s_sc[pl.ds(r0, 8), cim:cim + 128]
            dlam_ref[0, :, tc:tc + 128] += a_r * pr + a_i * pi
            dlam_ref[1, :, tc:tc + 128] += a_i * pr - a_r * pi

        _seg_scan(a_sc, 0, seg, lam_ref, tb_ref, carry_b, True, extra=lam_grad)

        for blk in range(S5_BLK):
            ab = a_sc[:, blk * 1024:(blk + 1) * 1024].astype(BF16)
            w_sc[blk] = _dot(ab, bt_ref[blk], 1, 0) + d_ref[:, blk * 128:(blk + 1) * 128] * dy_sc[:, blk * 128:(blk + 1) * 128]
            dbb_ref[blk] += _dot(u_sc[:, blk * 128:(blk + 1) * 128].astype(BF16), ab, 0, 0)
        _from_segments(w_sc, y_sc, seg, zero_head=first)
        du_ref[:, 0:S5_WIDTH] = y_sc[...].astype(BF16)
        du_ref[:, S5_WIDTH:S5_WIDTH + 128] = dif_ref[...]

    const3 = lambda t: (0, 0, 0)
    rev = lambda t: (nblk - 1 - t, 0)

    def lanes(c0):
        return [pl.BlockSpec((ts, 128), lambda t, cc=c0 + c: (nblk - 1 - t, cc)) for c in range(4)]

    return _ride_call(
        body, "s5_bwd", (nblk,),
        lanes(4 * ucol) + lanes(0) + lanes(0) + [
            pl.BlockSpec((1, 8, S5_COLS), lambda t: (nblk - 1 - t, 0, 0)),
            pl.BlockSpec((S5_BLK, 128, 1024), const3),
            pl.BlockSpec((S5_BLK, 1024, 128), const3),
            pl.BlockSpec((S5_BLK, 1024, 128), const3),
            pl.BlockSpec((S5_BLK, 128, 1024), const3),
            pl.BlockSpec((2, 8, 2048), const3),
            pl.BlockSpec((8, 8, 2048), const3),
            pl.BlockSpec((8, 8, 2048), const3),
            pl.BlockSpec((1, S5_WIDTH), lambda t: (0, 0)),
            pl.BlockSpec((ts, 128), rev),
            pl.BlockSpec(memory_space=pl.ANY)],
        [pl.BlockSpec((ts, S5_WIDTH + 128), lambda t: (nblk - 1 - t, P_U // (S5_WIDTH + 128))),
         pl.BlockSpec((S5_BLK, 128, 1024), const3),
         pl.BlockSpec((S5_BLK, 1024, 128), const3),
         pl.BlockSpec((2, 8, 2048), const3),
         pl.BlockSpec((8, S5_WIDTH), lambda t: (0, 0))],
        [jax.ShapeDtypeStruct((T, PW), BF16),
         jax.ShapeDtypeStruct((S5_BLK, 128, 1024), F32),
         jax.ShapeDtypeStruct((S5_BLK, 1024, 128), F32),
         jax.ShapeDtypeStruct((2, 8, 2048), F32),
         jax.ShapeDtypeStruct((8, S5_WIDTH), F32)],
        [pltpu.VMEM((ts + 8, S5_COLS), F32), pltpu.VMEM((ts, S5_COLS), F32),
         pltpu.VMEM((ts, S5_WIDTH), F32), pltpu.VMEM((ts, S5_WIDTH), F32),
         pltpu.VMEM((ts, S5_WIDTH), F32), pltpu.VMEM((S5_BLK, ts, 128), F32),
         pltpu.VMEM((8, S5_COLS), F32), pltpu.VMEM((8, S5_COLS), F32)],
        (p, p, p, p, y_pre, y_pre, y_pre, y_pre, dgy, dgy, dgy, dgy, cin,
         bblk.astype(BF16), jnp.swapaxes(bblk, 1, 2).astype(BF16),
         cblk.astype(BF16), jnp.swapaxes(cblk, 1, 2).astype(BF16), lam8, tab_f, tab_b, dskip, dif, dp),
        comm=comm, aliases={22: 0})


CONV_HALO = 16


def _row_in_seq(i, tr, lp):
    rowid = lax.broadcasted_iota(jnp.int32, (tr, 1), 0)
    return (i * tr + rowid) % lp


def _conv_fwd(p, w, b, lp):
    T = p.shape[0]
    tr = _pick(T, (384, 320, 256, 128, 64))
    c = P_QK // 1024
    hl = CONV_HALO

    def body(x_ref, xp_ref, w_ref, b_ref, o_ref):
        i = pl.program_id(0)
        pos = _row_in_seq(i, tr, lp)
        x = jnp.where(pos < PAD_ROWS, 0.0, x_ref[...].astype(F32))
        pos_p = (i * tr - hl + lax.broadcasted_iota(jnp.int32, (hl, 1), 0)) % lp
        xp = jnp.where((pos_p < PAD_ROWS) | (i == 0), 0.0, xp_ref[...].astype(F32))
        xx = jnp.concatenate([xp, x], axis=0)
        acc = b_ref[...] + w_ref[3:4, :] * x
        for s in (1, 2, 3):
            acc = acc + w_ref[3 - s:4 - s, :] * pltpu.roll(xx, s, 0)[hl:hl + tr]
        o_ref[...] = acc * _sigmoid(acc)

    return pl.pallas_call(
        body, name="conv_fwd", grid=(T // tr,),
        in_specs=[pl.BlockSpec((tr, 1024), lambda i: (i, c)),
                  pl.BlockSpec((hl, 1024), lambda i: (jnp.maximum(i * (tr // hl) - 1, 0), c)),
                  pl.BlockSpec((4, 1024), lambda i: (0, 0)),
                  pl.BlockSpec((1, 1024), lambda i: (0, 0))],
        out_specs=pl.BlockSpec((tr, 1024), lambda i: (i, 0)),
        out_shape=jax.ShapeDtypeStruct((T, 1024), F32),
        compiler_params=_cparams(("arbitrary",)),
    )(p, p, w, b)


def _conv_bwd(p, dqk, w, b, dp, lp):
    T = p.shape[0]
    tr = _pick(T, (384, 320, 256, 128, 64))
    c = P_QK // 1024
    nb = T // tr
    hl = CONV_HALO

    def body(x_ref, xp_ref, xn_ref, g_ref, gn_ref, w_ref, b_ref, dpin_ref, dx_ref, dw_ref, db_ref, x_sc, g_sc, part):
        i = pl.program_id(0)

        @pl.when(i == 0)
        def _():
            dw_ref[...] = jnp.zeros_like(dw_ref)
            db_ref[...] = jnp.zeros_like(db_ref)

        def seqpos(off, n):
            return (i * tr + off + lax.broadcasted_iota(jnp.int32, (n, 1), 0)) % lp

        x_sc[0:hl, :] = jnp.where((seqpos(-hl, hl) < PAD_ROWS) | (i == 0), 0.0, xp_ref[...].astype(F32))
        x_sc[hl:hl + tr, :] = jnp.where(seqpos(0, tr) < PAD_ROWS, 0.0, x_ref[...].astype(F32))
        x_sc[hl + tr:, :] = jnp.where((seqpos(tr, hl) < PAD_ROWS) | (i == nb - 1), 0.0, xn_ref[...].astype(F32))
        g_sc[0:tr, :] = g_ref[...].astype(F32)
        g_sc[tr:, :] = gn_ref[...].astype(F32)
        part[...] = jnp.zeros_like(part)
        taps = [w_ref[t:t + 1, :] for t in range(4)]
        bias = b_ref[...]

        ngroups = tr // hl

        def group(g, prev):
            r0 = pl.multiple_of(g * hl, hl)
            win = x_sc[pl.ds(r0, 2 * hl), :]
            sh = [win] + [pltpu.roll(win, s, 0) for s in (1, 2, 3)]
            acc = bias
            for s in range(4):
                acc = acc + taps[3 - s] * sh[s][hl:2 * hl]
            sg = _sigmoid(acc)
            rows = r0 + lax.broadcasted_iota(jnp.int32, (hl, 1), 0)
            pos = (i * tr + rows) % lp
            valid = (pos >= PAD_ROWS) & ((g < ngroups) | (i < nb - 1))
            cur = jnp.where(valid, g_sc[pl.ds(r0, hl), :] * (sg * (1.0 + acc * (1.0 - sg))), 0.0)
            own = jnp.where(g < ngroups, cur, 0.0)
            part[4] += own
            for s in range(4):
                part[3 - s] += own * sh[s][hl:2 * hl]

            @pl.when(g > 0)
            def _():
                both = jnp.concatenate([prev, cur], axis=0)
                dx = taps[3] * prev
                for s in (1, 2, 3):
                    dx = dx + taps[3 - s] * pltpu.roll(both, 2 * hl - s, 0)[0:hl]
                q0 = pl.multiple_of(r0 - hl, hl)
                pos_prev = (i * tr + q0 + lax.broadcasted_iota(jnp.int32, (hl, 1), 0)) % lp
                dx_ref[pl.ds(q0, hl), :] = jnp.where(pos_prev < PAD_ROWS, 0.0, dx).astype(BF16)

            return cur

        lax.fori_loop(0, ngroups + 1, group, jnp.zeros((hl, 1024), F32))
        db_ref[0:1, :] += jnp.sum(part[4], axis=0, keepdims=True)
        for t in range(4):
            dw_ref[t:t + 1, :] += jnp.sum(part[t], axis=0, keepdims=True)

    t8 = tr // hl
    return pl.pallas_call(
        body, name="conv_bwd", grid=(nb,),
        in_specs=[pl.BlockSpec((tr, 1024), lambda i: (i, c)),
                  pl.BlockSpec((hl, 1024), lambda i: (jnp.maximum(i * t8 - 1, 0), c)),
                  pl.BlockSpec((hl, 1024), lambda i: (jnp.minimum((i + 1) * t8, nb * t8 - 1), c)),
                  pl.BlockSpec((tr, 1024), lambda i: (i, 0)),
                  pl.BlockSpec((hl, 1024), lambda i: (jnp.minimum((i + 1) * t8, nb * t8 - 1), 0)),
                  pl.BlockSpec((4, 1024), lambda i: (0, 0)),
                  pl.BlockSpec((1, 1024), lambda i: (0, 0)),
                  pl.BlockSpec(memory_space=pl.ANY)],
        out_specs=(pl.BlockSpec((tr, 1024), lambda i: (i, c)),
                   pl.BlockSpec((8, 1024), lambda i: (0, 0)),
                   pl.BlockSpec((8, 1024), lambda i: (0, 0))),
        out_shape=(jax.ShapeDtypeStruct((T, PW), BF16),
                   jax.ShapeDtypeStruct((8, 1024), F32),
                   jax.ShapeDtypeStruct((8, 1024), F32)),
        input_output_aliases={7: 0},
        scratch_shapes=[pltpu.VMEM((tr + 2 * hl, 1024), F32), pltpu.VMEM((tr + hl, 1024), F32),
                        pltpu.VMEM((5, hl, 1024), F32)],
        compiler_params=_cparams(("arbitrary",)),
    )(p, p, p, dqk, dqk, w, b, dp)


def _split3(x):
    hi = x.astype(BF16)
    r1 = x - hi.astype(F32)
    mid = r1.astype(BF16)
    lo = (r1 - mid.astype(F32)).astype(BF16)
    return hi, mid, lo


def _tri_sum(x, upper):
    r = lax.broadcasted_iota(jnp.int32, (CHUNK, CHUNK), 0)
    c = lax.broadcasted_iota(jnp.int32, (CHUNK, CHUNK), 1)
    tri = jnp.where((r <= c) if upper else (r >= c), 1.0, 0.0).astype(BF16)
    hi, mid, lo = _split3(x)
    return _dot(tri, hi, 1, 0) + _dot(tri, mid, 1, 0) + _dot(tri, lo, 1, 0)


def _lane_col(x, lane):
    l = lax.broadcasted_iota(jnp.int32, x.shape, 1)
    return jnp.sum(jnp.where(l == lane, x, 0.0), axis=1, keepdims=True)


def _to_row(col):
    r = lax.broadcasted_iota(jnp.int32, (CHUNK, CHUNK), 0)
    c = lax.broadcasted_iota(jnp.int32, (CHUNK, CHUNK), 1)
    return jnp.sum(jnp.where(r == c, col, 0.0), axis=0, keepdims=True)


def _to_col(row):
    r = lax.broadcasted_iota(jnp.int32, (CHUNK, CHUNK), 0)
    c = lax.broadcasted_iota(jnp.int32, (CHUNK, CHUNK), 1)
    return jnp.sum(jnp.where(r == c, row, 0.0), axis=1, keepdims=True)


def _log_sigmoid(x):
    return jnp.minimum(x, 0.0) - jnp.log(1.0 + jnp.exp(-jnp.abs(x)))


def _mlstm_gates(ifv, padmask):
    lf = jnp.where(padmask, 0.0, _log_sigmoid(ifv))
    b_all = _tri_sum(lf, False)
    li = jnp.where(padmask, -jnp.inf, ifv)
    return li, b_all


def _mlstm_head_fwd(q, k, v, li_col, b_col, c_st, n_st, m_st):
    r = lax.broadcasted_iota(jnp.int32, (CHUNK, CHUNK), 0)
    c = lax.broadcasted_iota(jnp.int32, (CHUNK, CHUNK), 1)
    rowid = lax.broadcasted_iota(jnp.int32, (CHUNK, 1), 0)
    b_row = _to_row(b_col)
    li_row = _to_row(li_col)
    dmat = jnp.where(r >= c, b_col - b_row + li_row, -jnp.inf)
    m_inter = b_col + m_st
    m_row = jnp.maximum(m_inter, jnp.max(dmat, axis=1, keepdims=True))
    w_intra = jnp.exp(dmat - m_row)
    w_inter = jnp.exp(m_inter - m_row)
    qb, kb, vb = q.astype(BF16), k.astype(BF16), v.astype(BF16)
    qk = _dot(qb, kb, 1, 1)
    s = qk * w_intra
    cb = c_st.astype(BF16)
    qc = _dot(qb, cb, 1, 0)
    qn = jnp.sum(q * n_st, axis=1, keepdims=True)
    num = _dot(s.astype(BF16), vb, 1, 0) + w_inter * qc
    den = jnp.sum(s, axis=1, keepdims=True) + w_inter * qn
    floor = jnp.exp(-m_row)
    rinv = 1.0 / jnp.maximum(jnp.abs(den), floor)
    h = num * rinv
    b_last = jnp.sum(jnp.where(rowid == CHUNK - 1, b_col, 0.0), axis=0, keepdims=True)
    g_col = b_last - b_col + li_col
    m_new = jnp.maximum(b_last + m_st, jnp.max(g_col, axis=0, keepdims=True))
    w_k = jnp.exp(g_col - m_new)
    decay = jnp.exp(b_last + m_st - m_new)
    kw = w_k * k
    c_new = decay * c_st + _dot(kw.astype(BF16), vb, 0, 0)
    n_new = decay * n_st + jnp.sum(kw, axis=0, keepdims=True)
    return dict(h=h, c_new=c_new, n_new=n_new, m_new=m_new, w_intra=w_intra, w_inter=w_inter, s=s,
                qc=qc, qn=qn, den=den, floor=floor, rinv=rinv, w_k=w_k, decay=decay, kw=kw,
                qb=qb, kb=kb, vb=vb, cb=cb)


def _mlstm_fwd(p, puif, qk, lp):
    T = p.shape[0]
    nch = lp // CHUNK
    B = T // lp
    scale = M_DK ** -0.5

    def body(q_ref, k_ref, v_ref, if_ref, h_ref, cst_ref, nm_ref, *state):
        c_scs, nm_scs = state[:B * M_HEADS], state[B * M_HEADS:]
        ci = pl.program_id(0)

        @pl.when(ci == 0)
        def _():
            for ref in state:
                ref[...] = jnp.zeros_like(ref)

        rowid = lax.broadcasted_iota(jnp.int32, (CHUNK, 1), 0)
        padmask = (ci == 0) & (rowid < PAD_ROWS)
        for bb in range(B):
            li_all, b_all = _mlstm_gates(if_ref[bb], padmask)
            for hd in range(M_HEADS):
                c_sc, nm_sc = c_scs[bb * M_HEADS + hd], nm_scs[bb * M_HEADS + hd]
                c_st, n_st, m_row = c_sc[...], nm_sc[0:1, :], nm_sc[1:2, :]
                cst_ref[bb, 0, hd * M_DK:(hd + 1) * M_DK, :] = c_st
                nm_ref[bb, 0, hd:hd + 1, :] = n_st
                nm_ref[bb, 0, M_HEADS + hd:M_HEADS + hd + 1, :] = m_row
                q = q_ref[bb, :, hd * M_DK:(hd + 1) * M_DK]
                k = k_ref[bb, :, hd * M_DK:(hd + 1) * M_DK] * scale
                v = v_ref[bb, :, hd * M_DV:(hd + 1) * M_DV]
                o = _mlstm_head_fwd(q, k, v, _lane_col(li_all, hd), _lane_col(b_all, M_HEADS + hd),
                                    c_st, n_st, nm_sc[1:2, 0:1])
                h_ref[bb, :, hd * M_DV:(hd + 1) * M_DV] = o["h"].astype(BF16)
                c_sc[...] = o["c_new"]
                nm_sc[0:1, :] = o["n_new"]
                nm_sc[1:2, :] = jnp.broadcast_to(o["m_new"], (1, 128))

    qk3, p3, uif3 = qk.reshape(B, lp, 1024), p.reshape(B, lp, p.shape[1]), puif.reshape(B, lp, puif.shape[1])
    h, cst, nm = pl.pallas_call(
        body, name="mlstm_fwd", grid=(nch,),
        in_specs=[pl.BlockSpec((B, CHUNK, 512), lambda ci: (0, ci, 0)),
                  pl.BlockSpec((B, CHUNK, 512), lambda ci: (0, ci, 1)),
                  pl.BlockSpec((B, CHUNK, 1024), lambda ci: (0, ci, P_V // 1024)),
                  pl.BlockSpec((B, CHUNK, 128), lambda ci: (0, ci, S5_WIDTH // 128))],
        out_specs=(pl.BlockSpec((B, CHUNK, 1024), lambda ci: (0, ci, 0)),
                   pl.BlockSpec((B, 1, M_HEADS * M_DK, M_DV), lambda ci: (0, ci, 0, 0)),
                   pl.BlockSpec((B, 1, 8, 128), lambda ci: (0, ci, 0, 0))),
        out_shape=(jax.ShapeDtypeStruct((B, lp, 1024), BF16),
                   jax.ShapeDtypeStruct((B, nch, M_HEADS * M_DK, M_DV), F32),
                   jax.ShapeDtypeStruct((B, nch, 8, 128), F32)),
        scratch_shapes=([pltpu.VMEM((M_DK, M_DV), F32)] * (B * M_HEADS)
                        + [pltpu.VMEM((8, 128), F32)] * (B * M_HEADS)),
        compiler_params=_cparams(("arbitrary",)),
    )(qk3, qk3, p3, uif3)
    return h.reshape(T, 1024), cst, nm


def _mlstm_bwd(p, puif, qk, cst, nm, dh, dp, lp, comm=None):
    T = p.shape[0]
    nch = lp // CHUNK
    B = T // lp
    scale = M_DK ** -0.5

    def body(q_ref, k_ref, v_ref, if_ref, cst_ref, nm_ref, dh_ref, dpin_ref, dqk_ref, dv_ref, dif_ref, *state):
        t = pl.program_id(0)
        ci = nch - 1 - t

        @pl.when(t == 0)
        def _():
            for ref in state:
                ref[...] = jnp.zeros_like(ref)

        for bb in range(B):
            one_sequence(bb, ci, q_ref, k_ref, v_ref, if_ref, cst_ref, nm_ref, dh_ref, dqk_ref, dv_ref, dif_ref,
                         state[bb * M_HEADS:(bb + 1) * M_HEADS],
                         state[(B + bb) * M_HEADS:(B + bb + 1) * M_HEADS])

    def one_sequence(bb, ci, q_ref, k_ref, v_ref, if_ref, cst_ref, nm_ref, dh_ref, dqk_ref, dv_ref, dif_ref,
                     dc_scs, dn_scs):
        rowid = lax.broadcasted_iota(jnp.int32, (CHUNK, 1), 0)
        lane = lax.broadcasted_iota(jnp.int32, (CHUNK, 128), 1)
        padmask = (ci == 0) & (rowid < PAD_ROWS)
        ifv = if_ref[bb]
        li_all, b_all = _mlstm_gates(ifv, padmask)
        db_all = jnp.zeros((CHUNK, 128), F32)
        dli_all = jnp.zeros((CHUNK, 128), F32)
        for hd in range(M_HEADS):
            q = q_ref[bb, :, hd * M_DK:(hd + 1) * M_DK]
            k = k_ref[bb, :, hd * M_DK:(hd + 1) * M_DK] * scale
            v = v_ref[bb, :, hd * M_DV:(hd + 1) * M_DV]
            c_st = cst_ref[bb, 0, hd * M_DK:(hd + 1) * M_DK, :]
            n_st = nm_ref[bb, 0, hd:hd + 1, :]
            m_st = nm_ref[bb, 0, M_HEADS + hd:M_HEADS + hd + 1, 0:1]
            o = _mlstm_head_fwd(q, k, v, _lane_col(li_all, hd), _lane_col(b_all, M_HEADS + hd), c_st, n_st, m_st)
            dc_sc, dn_sc = dc_scs[hd], dn_scs[hd]
            dc_new = dc_sc[...]
            dn_new = dn_sc[0:1, :]
            dcb = dc_new.astype(BF16)
            dhh = dh_ref[bb, :, hd * M_DV:(hd + 1) * M_DV].astype(F32)
            dnum = dhh * o["rinv"]
            dhh_h = jnp.sum(dhh * o["h"], axis=1, keepdims=True)
            sgn = jnp.where(o["den"] >= 0.0, 1.0, -1.0)
            dden = jnp.where(jnp.abs(o["den"]) > o["floor"], -dhh_h * o["rinv"] * sgn, 0.0)
            dnb = dnum.astype(BF16)
            ds = _dot(dnb, o["vb"], 1, 1) + dden
            sb = o["s"].astype(BF16)
            kwb = o["kw"].astype(BF16)
            dv = _dot(sb, dnb, 0, 0) + _dot(kwb, dcb, 1, 0)
            dqk_m = (ds * o["w_intra"]).astype(BF16)
            wdn = o["w_inter"] * dnum
            wdd = o["w_inter"] * dden
            dq = _dot(dqk_m, o["kb"], 1, 0) + _dot(wdn.astype(BF16), o["cb"], 1, 1) + wdd * n_st
            vdc = _dot(o["vb"], dcb, 1, 1)
            dk = _dot(dqk_m, o["qb"], 0, 0) + o["w_k"] * (vdc + dn_new)
            dd = ds * o["s"]
            dd_col = _to_col(jnp.sum(dd, axis=0, keepdims=True))
            dmi = o["w_inter"] * (jnp.sum(dnum * o["qc"], axis=1, keepdims=True) + dden * o["qn"])
            dg = o["w_k"] * (jnp.sum(k * vdc, axis=1, keepdims=True) + jnp.sum(k * dn_new, axis=1, keepdims=True))
            d_blast = (o["decay"] * (jnp.sum(jnp.sum(dc_new * c_st, axis=1, keepdims=True), axis=0, keepdims=True)
                                     + jnp.sum(dn_new * n_st, axis=1, keepdims=True))
                       + jnp.sum(dg, axis=0, keepdims=True))
            db_col = jnp.sum(dd, axis=1, keepdims=True) - dd_col + dmi - dg
            db_col = db_col + jnp.where(rowid == CHUNK - 1, d_blast, 0.0)
            dli_col = dd_col + dg
            db_all = db_all + jnp.where(lane == M_HEADS + hd, db_col, 0.0)
            dli_all = dli_all + jnp.where(lane == hd, dli_col, 0.0)
            dc_sc[...] = o["decay"] * dc_new + _dot(o["qb"], wdn.astype(BF16), 0, 0)
            dn_sc[0:1, :] = o["decay"] * dn_new + jnp.sum(q * wdd, axis=0, keepdims=True)
            dqk_ref[bb, :, hd * M_DK:(hd + 1) * M_DK] = dq.astype(BF16)
            dqk_ref[bb, :, 512 + hd * M_DK:512 + (hd + 1) * M_DK] = (dk * scale).astype(BF16)
            dv_ref[bb, :, hd * M_DV:(hd + 1) * M_DV] = dv.astype(BF16)
        dlf = _tri_sum(db_all, True)
        dif = dli_all + dlf * _sigmoid(-ifv)
        dif_ref[bb] = jnp.where(padmask | (lane >= 2 * M_HEADS), 0.0, dif).astype(BF16)

    def rev(cb):
        return lambda t: (0, nch - 1 - t, cb)

    rev4 = lambda t: (0, nch - 1 - t, 0, 0)
    qk3, p3, dh3 = qk.reshape(B, lp, 1024), p.reshape(B, lp, p.shape[1]), dh.reshape(B, lp, 1024)
    uif3 = puif.reshape(B, lp, puif.shape[1])
    (dqk, dv, dif), comm_res = _ride_call(
        body, "mlstm_bwd", (nch,),
        [pl.BlockSpec((B, CHUNK, 512), rev(0)),
         pl.BlockSpec((B, CHUNK, 512), rev(1)),
         pl.BlockSpec((B, CHUNK, 1024), rev(P_V // 1024)),
         pl.BlockSpec((B, CHUNK, 128), rev(S5_WIDTH // 128)),
         pl.BlockSpec((B, 1, M_HEADS * M_DK, M_DV), rev4),
         pl.BlockSpec((B, 1, 8, 128), rev4),
         pl.BlockSpec((B, CHUNK, 1024), rev(0)),
         pl.BlockSpec(memory_space=pl.ANY)],
        [pl.BlockSpec((B, CHUNK, 1024), rev(0)),
         pl.BlockSpec((B, CHUNK, 1024), rev(P_V // 1024)),
         pl.BlockSpec((B, CHUNK, 128), rev(0))],
        [jax.ShapeDtypeStruct((B, lp, 1024), BF16),
         jax.ShapeDtypeStruct((B, lp, PW), BF16),
         jax.ShapeDtypeStruct((B, lp, 128), BF16)],
        [pltpu.VMEM((M_DK, M_DV), F32)] * (B * M_HEADS) + [pltpu.VMEM((8, 128), F32)] * (B * M_HEADS),
        (qk3, qk3, p3, uif3, cst, nm, dh3, dp.reshape(B, lp, PW)), comm=comm, aliases={7: 1})
    return dqk.reshape(T, 1024), dv.reshape(T, PW), dif.reshape(T, 128), comm_res


def _headnorm_fwd(hm, p, g):
    T = hm.shape[0]
    tr = _pick(T, (384, 320, 256, 128, 64))

    def body(h_ref, o_ref, g_ref, a_ref):
        for hd in range(M_HEADS):
            sl = slice(hd * M_DV, (hd + 1) * M_DV)
            hn = _ln_rows(h_ref[:, sl].astype(F32), g_ref[:, sl], 0.0)
            a_ref[:, sl] = (_sigmoid(o_ref[:, sl].astype(F32)) * hn).astype(BF16)

    return pl.pallas_call(
        body, name="headnorm_fwd", grid=(T // tr,),
        in_specs=[pl.BlockSpec((tr, 1024), lambda i: (i, 0)),
                  pl.BlockSpec((tr, 1024), lambda i: (i, P_O // 1024)),
                  pl.BlockSpec((1, 1024), lambda i: (0, 0))],
        out_specs=pl.BlockSpec((tr, 1024), lambda i: (i, 0)),
        out_shape=jax.ShapeDtypeStruct((T, 1024), BF16),
        compiler_params=_cparams(("arbitrary",)),
    )(hm, p, g)


def _headnorm_bwd(da, hm, p, g, dp):
    T = hm.shape[0]
    tr = _pick(T, (384, 320, 256, 128, 64))

    def body(da_ref, h_ref, o_ref, g_ref, dpin_ref, dh_ref, do_ref, dg_ref):
        @pl.when(pl.program_id(0) == 0)
        def _():
            dg_ref[...] = jnp.zeros_like(dg_ref)

        for hd in range(M_HEADS):
            sl = slice(hd * M_DV, (hd + 1) * M_DV)
            gg = g_ref[:, sl]
            so = _sigmoid(o_ref[:, sl].astype(F32))
            da = da_ref[:, sl].astype(F32)
            dhn = da * so
            dv, xhat = _ln_bwd_rows(dhn, h_ref[:, sl].astype(F32), gg)
            dh_ref[:, sl] = dv.astype(BF16)
            do_ref[:, sl] = (da * (xhat * gg) * so * (1.0 - so)).astype(BF16)
            dg_ref[0:1, sl] += jnp.sum(dhn * xhat, axis=0, keepdims=True)

    row = pl.BlockSpec((tr, 1024), lambda i: (i, 0))
    return pl.pallas_call(
        body, name="headnorm_bwd", grid=(T // tr,),
        in_specs=[row, row, pl.BlockSpec((tr, 1024), lambda i: (i, P_O // 1024)),
                  pl.BlockSpec((1, 1024), lambda i: (0, 0)), pl.BlockSpec(memory_space=pl.ANY)],
        out_specs=(row, pl.BlockSpec((tr, 1024), lambda i: (i, P_O // 1024)),
                   pl.BlockSpec((8, 1024), lambda i: (0, 0))),
        out_shape=(jax.ShapeDtypeStruct((T, 1024), BF16), jax.ShapeDtypeStruct((T, PW), BF16),
                   jax.ShapeDtypeStruct((8, 1024), F32)),
        input_output_aliases={4: 1},
        compiler_params=_cparams(("arbitrary",)),
    )(da, hm, p, g, dp)


def _mix_fwd(z, ym, p):
    T = ym.shape[0]
    tr = _pick(T, (384, 320, 256, 128, 64))

    def body(z1_ref, z2_ref, ym_ref, gs_ref, gm_ref, o_ref):
        ys = z1_ref[...].astype(F32) * _sigmoid(z2_ref[...].astype(F32))
        o_ref[...] = (_sigmoid(gs_ref[...].astype(F32)) * ys
                      + _sigmoid(gm_ref[...].astype(F32)) * ym_ref[...].astype(F32)).astype(BF16)

    def col(cb):
        return pl.BlockSpec((tr, 1024), lambda i: (i, cb))

    return pl.pallas_call(
        body, name="mix_fwd", grid=(T // tr,),
        in_specs=[col(0), col(1), col(0), col(P_GS // 1024), col(P_GM // 1024)],
        out_specs=col(0),
        out_shape=jax.ShapeDtypeStruct((T, 1024), BF16),
        compiler_params=_cparams(("arbitrary",)),
    )(z, z, ym, p, p)


def _mix_bwd(dmix, z, ym, p):
    T = ym.shape[0]
    tr = _pick(T, (384, 320, 256, 128, 64))

    def body(d_ref, z1_ref, z2_ref, ym_ref, gs_ref, gm_ref, dz_ref, dym_ref, dp_ref):
        d = d_ref[...].astype(F32)
        z1 = z1_ref[...].astype(F32)
        s2 = _sigmoid(z2_ref[...].astype(F32))
        ss = _sigmoid(gs_ref[...].astype(F32))
        sm = _sigmoid(gm_ref[...].astype(F32))
        ys = z1 * s2
        dys = d * ss
        dp_ref[:, 0:1024] = (d * ys * ss * (1.0 - ss)).astype(BF16)
        dp_ref[:, 1024:2048] = (d * ym_ref[...].astype(F32) * sm * (1.0 - sm)).astype(BF16)
        dym_ref[...] = (d * sm).astype(BF16)
        dz_ref[:, 0:1024] = (dys * s2).astype(BF16)
        dz_ref[:, 1024:2048] = (dys * z1 * s2 * (1.0 - s2)).astype(BF16)

    def col(cb):
        return pl.BlockSpec((tr, 1024), lambda i: (i, cb))

    o = jax.ShapeDtypeStruct((T, 1024), BF16)
    return pl.pallas_call(
        body, name="mix_bwd", grid=(T // tr,),
        in_specs=[col(0), col(0), col(1), col(0), col(P_GS // 1024), col(P_GM // 1024)],
        out_specs=(pl.BlockSpec((tr, 2048), lambda i: (i, 0)), col(0),
                   pl.BlockSpec((tr, 2048), lambda i: (i, P_GS // 2048))),
        out_shape=(jax.ShapeDtypeStruct((T, 2048), BF16), o, jax.ShapeDtypeStruct((T, PW), BF16)),
        compiler_params=_cparams(("arbitrary",)),
    )(dmix, z, z, ym, p, p)


def _adamw_math(w, g, m, v):
    m2 = ADAM_B1 * m + (1.0 - ADAM_B1) * g
    v2 = ADAM_B2 * v + (1.0 - ADAM_B2) * jnp.square(g)
    m_hat = m2 / (1.0 - ADAM_B1 ** ADAM_STEP)
    v_hat = v2 / (1.0 - ADAM_B2 ** ADAM_STEP)
    delta = -ADAM_LR * (m_hat / (jnp.sqrt(v_hat) + ADAM_EPS) + ADAM_WD * w)
    return delta, m2, v2


def _adamw_big(recv, w, m, v, name):
    R, C = w.shape
    tr, tc = _shard_tile(R, C)
    ns = recv.shape[0]

    def body(r_ref, w_ref, m_ref, v_ref, g_ref, d_ref, m2_ref, v2_ref):
        g = r_ref[0].astype(F32)
        for s in range(1, ns):
            g = g + r_ref[s].astype(F32)
        d, m2, v2 = _adamw_math(w_ref[...], g, m_ref[...], v_ref[...])
        g_ref[...] = g
        d_ref[...] = d
        m2_ref[...] = m2
        v2_ref[...] = v2

    blk = pl.BlockSpec((tr, tc), lambda i, j: (i, j))
    o = jax.ShapeDtypeStruct((R, C), F32)
    return pl.pallas_call(
        body, name=name, grid=(R // tr, C // tc),
        in_specs=[pl.BlockSpec((ns, tr, tc), lambda i, j: (0, i, j)), blk, blk, blk],
        out_specs=(blk,) * 4, out_shape=(o,) * 4,
        compiler_params=_cparams(("arbitrary", "arbitrary")),
    )(recv, w, m, v)


def _adamw_small(gs, ws, ms, vs):
    n = len(ws)
    shapes = [tuple(w.shape) for w in ws]

    def flat2(a):
        return a.reshape(1, -1) if a.ndim == 1 else a.reshape(-1, a.shape[-1])

    def body(*refs):
        g, w, m, v = refs[:n], refs[n:2 * n], refs[2 * n:3 * n], refs[3 * n:4 * n]
        d, m2, v2 = refs[4 * n:5 * n], refs[5 * n:6 * n], refs[6 * n:]
        for i in range(n):
            di, mi, vi = _adamw_math(w[i][...], g[i][...], m[i][...], v[i][...])
            d[i][...] = di
            m2[i][...] = mi
            v2[i][...] = vi

    args = [flat2(a) for a in list(gs) + list(ws) + list(ms) + list(vs)]
    vm = pl.BlockSpec(memory_space=pltpu.VMEM)
    outs = pl.pallas_call(
        body, name="adamw_small",
        in_specs=[vm] * (4 * n), out_specs=tuple([vm] * (3 * n)),
        out_shape=tuple(jax.ShapeDtypeStruct(a.shape, F32) for a in args[:n] * 3),
        compiler_params=pltpu.CompilerParams(vmem_limit_bytes=VMEM_LIMIT),
    )(*args)
    outs = [o.reshape(s) for o, s in zip(outs, shapes * 3)]
    return outs[:n], outs[n:2 * n], outs[2 * n:]


def _pack(arrs):
    parts = []
    for a in arrs:
        f = a.reshape(-1).astype(F32)
        n = -(-f.shape[0] // 1024) * 1024
        parts.append(jnp.pad(f, (0, n - f.shape[0])))
    return jnp.concatenate(parts).reshape(-1, 128)


def _unpack(pack, shapes):
    flat = pack.reshape(-1)
    out, off = [], 0
    for shp in shapes:
        n = math.prod(shp)
        out.append(flat[off:off + n].reshape(shp))
        off += -(-n // 1024) * 1024
    return out


def _cols_from_shards(g):
    return jnp.transpose(g, (1, 0, 2)).reshape(g.shape[1], -1)


def _cols_to_shards(w):
    R = w.shape[0]
    return jnp.transpose(w.reshape(R, N_DEV, -1), (1, 0, 2))


def _nat_to_aligned(w, axis):
    def sl(start, size):
        return lax.slice_in_dim(w, start, start + size, axis=axis)

    pads = [(0, 0)] * w.ndim
    pads[axis] = (0, PW - P_IF - 8)
    return jnp.concatenate([sl(N_V, 1024), sl(N_O, 1024), sl(N_GS, 1024), sl(N_GM, 1024), sl(N_Q, 1024),
                            sl(N_U, 512), jnp.pad(sl(N_I, 8), pads)], axis=axis)


def _aligned_to_nat(w, axis):
    def sl(start, size):
        return lax.slice_in_dim(w, start, start + size, axis=axis)

    return jnp.concatenate([sl(P_U, 512), sl(P_QK, 1024), sl(P_V, 1024), sl(P_O, 1024), sl(P_IF, 8),
                            sl(P_GS, 1024), sl(P_GM, 1024)], axis=axis)


def kernel(x, meta_tokens, ln0_g, ln0_b, w_in, b_in, qk_conv_w, qk_conv_b, s5_lambda_re, s5_lambda_im, s5_log_dt, s5_b_re, s5_b_im, s5_c_re, s5_c_im, s5_d, s5_w_glu, m_norm_g, m_w_out, w_o, ln1_g, ln1_b, w_up, b_up, w_down, ln2_g, ln2_b, loss_target, m_meta_tokens, m_ln0_g, m_ln0_b, m_w_in, m_b_in, m_qk_conv_w, m_qk_conv_b, m_s5_lambda_re, m_s5_lambda_im, m_s5_log_dt, m_s5_b_re, m_s5_b_im, m_s5_c_re, m_s5_c_im, m_s5_d, m_s5_w_glu, m_m_norm_g, m_m_w_out, m_w_o, m_ln1_g, m_ln1_b, m_w_up, m_b_up, m_w_down, m_ln2_g, m_ln2_b, v_meta_tokens, v_ln0_g, v_ln0_b, v_w_in, v_b_in, v_qk_conv_w, v_qk_conv_b, v_s5_lambda_re, v_s5_lambda_im, v_s5_log_dt, v_s5_b_re, v_s5_b_im, v_s5_c_re, v_s5_c_im, v_s5_d, v_s5_w_glu, v_m_norm_g, v_m_w_out, v_w_o, v_ln1_g, v_ln1_b, v_w_up, v_b_up, v_w_down, v_ln2_g, v_ln2_b):
    B, S, D = x.shape
    lp = S + CHUNK
    me = _my_id()

    ln0g, ln0b = ln0_g.reshape(1, D), ln0_b.reshape(1, D)
    (h0, h0b), first = _ln0_fwd(x, ln0g, ln0b, comm=(_Gather, [w_in[0].T.astype(BF16), meta_tokens, qk_conv_w[0]]))
    win_t = _nat_to_aligned(first[0].reshape(IN_NAT, D), 0)
    meta_f = _cols_from_shards(first[1])
    convw_f = _cols_from_shards(first[2])
    later = [a.astype(BF16) for a in (s5_w_glu[0], m_w_out[0], w_o[0])]
    latest = [a.astype(BF16) for a in (w_up[0], w_down[0])]
    b_in_al = _nat_to_aligned(b_in, 1)

    s5_args = (s5_lambda_re[0], s5_lambda_im[0], s5_log_dt[0], s5_b_re[0], s5_b_im[0], s5_c_re[0], s5_c_im[0])
    (ar, ai, bblk, cblk), s5_vjp = jax.vjp(_s5_prep, *s5_args)
    seg = _s5_tile(lp) // 8
    mu_r, mu_i = ar, ai
    for _ in range(seg - 1):
        mu_r, mu_i = mu_r * ar - mu_i * ai, mu_r * ai + mu_i * ar
    tab_f, tab_b = _scan_tables(mu_r, mu_i)
    lam8 = jnp.stack([jnp.broadcast_to(ar, (8, 2048)), jnp.broadcast_to(ai, (8, 2048))])

    h0, h0b = _ln0_head(meta_f, ln0g, ln0b, h0, h0b, lp)
    p, *gathered = _mm(h0b, win_t, "nt", "mm_in", bias=b_in_al[:, :P_U], out_dtype=BF16, b_rows=(0, P_U),
                       comm=(_Gather, later))
    puif = _mm(h0b, win_t, "nt", "mm_in_uif", bias=b_in_al[:, P_U:], b_rows=(P_U, PW - P_U))
    wglu_f = _cols_from_shards(gathered[0])
    wmo_f = gathered[1].reshape(1024, 1024)
    wo_f = gathered[2].reshape(1024, 1024)
    (y_pre, gy, cin), gathered2 = _s5_fwd(puif, bblk, cblk, lam8, tab_f, s5_d, lp, comm=(_Gather, latest))
    wup_f = _cols_from_shards(gathered2[0])
    wdown_f = gathered2[1].reshape(D_FF, 1024)
    z = _mm(gy, wglu_f, "nn", "mm_glu", out_dtype=BF16)
    qk = _conv_fwd(p, convw_f, qk_conv_b, lp)
    hm, cst, nm = _mlstm_fwd(p, puif, qk, lp)
    a_m = _headnorm_fwd(hm, p, m_norm_g)
    ym = _mm(a_m, wmo_f, "nn", "mm_mout", out_dtype=BF16)
    mix = _mix_fwd(z, ym, p)
    r1 = _mm(mix, wo_f, "nn", "mm_o")
    pre1, h1, h1b = _ln_res_fwd(h0, r1, ln1_g, ln1_b, "ln1_fwd")
    act = _mm(h1b, wup_f, "nn", "mm_up", bias=b_up, relu2_out=True, out_dtype=BF16)
    ff = _mm(act, wdown_f, "nn", "mm_down")
    loss_acc, dpre2, dpre2b, dg2, db2 = _ln2_loss(h1, ff, ln2_g, ln2_b, loss_target)

    d_up = _mm(dpre2b, wdown_f, "nt", "mm_d_act", sqrt_gate_of=act, out_dtype=BF16)
    g_wdown = _mm(act, dpre2b, "tn", "mm_g_wdown", out_dtype=BF16)
    g_wup, cs_up = _mm(h1b, d_up, "tn", "mm_g_wup", colsum="b", out_dtype=BF16)
    dh1 = _mm(d_up, wup_f, "nt", "mm_d_h1", add=dpre2, add_scale=ALPHA)
    dpre1, dpre1b, dg1, db1 = _ln_bwd(dh1, pre1, ln1_g, "ln1_bwd")
    g_wo = _mm(mix, dpre1b, "tn", "mm_g_wo", out_dtype=BF16)
    dmix = _mm(dpre1b, wo_f, "nt", "mm_d_mix", out_dtype=BF16)
    dz, dym, dp = _mix_bwd(dmix, z, ym, p)
    g_wmo = _mm(a_m, dym, "tn", "mm_g_wmo", out_dtype=BF16)
    da = _mm(dym, wmo_f, "nt", "mm_d_a", out_dtype=BF16)
    dhm, dp, dg_norm = _headnorm_bwd(da, hm, p, m_norm_g, dp)
    early = [g_wdown.reshape(N_DEV, 512, 1024), _cols_to_shards(g_wup), g_wo.reshape(N_DEV, 128, 1024),
             g_wmo.reshape(N_DEV, 128, 1024)]
    dqk, dp, dif, recv_early = _mlstm_bwd(p, puif, qk, cst, nm, dhm, dp, lp, comm=(_AllToAll, early))
    dp, dconv_w, dconv_b = _conv_bwd(p, dqk, convw_f, qk_conv_b, dp, lp)
    g_wglu = _mm(gy, dz, "tn", "mm_g_wglu", out_dtype=BF16)
    dgy = _mm(dz, wglu_f, "nt", "mm_d_gy")
    (dp, dbblk, dcblk, dlam, dd), recv_glu = _s5_bwd(puif, y_pre, dgy, cin, bblk, cblk, lam8, tab_f, tab_b, s5_d, dif, dp,
                                                     lp, comm=(_AllToAll, [_cols_to_shards(g_wglu)]))
    g_win_t, cs_in = _mm(dp, h0b, "tn", "mm_g_win", colsum="a")
    g_win8 = _aligned_to_nat(g_win_t, 0).astype(BF16).reshape(N_DEV, IN_NAT // N_DEV, D)
    dh0, recv_win = _mm(dp, win_t, "nn", "mm_d_h0", add=dpre1, add_scale=ALPHA, comm=(_AllToAll, [g_win8]))
    grad_x, dmeta, dg0, db0 = _ln0_bwd(dh0, x, meta_f, ln0g)

    dlam2 = jnp.sum(dlam, axis=1)
    s5_grads = s5_vjp((dlam2[0:1], dlam2[1:2], dbblk, dcblk))
    small_local = [
        loss_acc[0:1, 0:1], dg0[0:1], db0[0:1], _aligned_to_nat(cs_in[0:1], 1), dconv_b[0:1],
        s5_grads[0], s5_grads[1], s5_grads[2], s5_grads[3], s5_grads[4], s5_grads[5], s5_grads[6],
        dd[0:1], dg_norm[0:1], dg1[0:1], db1[0:1], cs_up[0:1], dg2[0:1], db2[0:1],
        dmeta, dconv_w[0:4]]
    small_shapes = [(), (D,), (D,), (1, IN_NAT), (1, 1024),
                    (1, 32, 64), (1, 32, 64), (1, 32), (1, 32, 64, 16), (1, 32, 64, 16), (1, 32, 16, 64), (1, 32, 16, 64),
                    (1, 512), (1, 1024), (1, 1024), (1, 1024), (1, D_FF), (1, 1024), (1, 1024),
                    (N_META, D), (4, 1024)]
    red = _unpack(_allreduce_small(_pack(small_local)), small_shapes)
    loss = red[0]
    g_meta = lax.dynamic_slice_in_dim(red[19], me * 128, 128, axis=1)
    g_convw = lax.dynamic_slice_in_dim(red[20], me * 128, 128, axis=1)[None]
    small_g = red[1:19] + [g_meta, g_convw]
    small_w = [ln0_g, ln0_b, b_in, qk_conv_b, s5_lambda_re, s5_lambda_im, s5_log_dt, s5_b_re, s5_b_im,
               s5_c_re, s5_c_im, s5_d, m_norm_g, ln1_g, ln1_b, b_up, ln2_g, ln2_b, meta_tokens, qk_conv_w]
    small_m = [m_ln0_g, m_ln0_b, m_b_in, m_qk_conv_b, m_s5_lambda_re, m_s5_lambda_im, m_s5_log_dt, m_s5_b_re,
               m_s5_b_im, m_s5_c_re, m_s5_c_im, m_s5_d, m_m_norm_g, m_ln1_g, m_ln1_b, m_b_up, m_ln2_g, m_ln2_b,
               m_meta_tokens, m_qk_conv_w]
    small_v = [v_ln0_g, v_ln0_b, v_b_in, v_qk_conv_b, v_s5_lambda_re, v_s5_lambda_im, v_s5_log_dt, v_s5_b_re,
               v_s5_b_im, v_s5_c_re, v_s5_c_im, v_s5_d, v_m_norm_g, v_ln1_g, v_ln1_b, v_b_up, v_ln2_g, v_ln2_b,
               v_meta_tokens, v_qk_conv_w]
    shapes_w = [tuple(w.shape) for w in small_w]
    small_g = [g.reshape(s) for g, s in zip(small_g, shapes_w)]
    sd, sm2, sv2 = _adamw_small(small_g, small_w, small_m, small_v)

    names = ["w_in", "s5_w_glu", "m_w_out", "w_o", "w_up", "w_down"]
    recv = [recv_win, recv_glu[0], recv_early[3], recv_early[2], recv_early[1], recv_early[0]]
    big_m = [m_w_in[0].T, m_s5_w_glu[0], m_m_w_out[0], m_w_o[0], m_w_up[0], m_w_down[0]]
    big_v = [v_w_in[0].T, v_s5_w_glu[0], v_m_w_out[0], v_w_o[0], v_w_up[0], v_w_down[0]]
    big_w = [w_in[0].T, s5_w_glu[0], m_w_out[0], w_o[0], w_up[0], w_down[0]]
    big_out = [_adamw_big(r, w, m, v, "adamw_" + nm_) for r, w, m, v, nm_ in zip(recv, big_w, big_m, big_v, names)]
    big_out[0] = [o.T for o in big_out[0]]

    order = ["meta_tokens", "ln0_g", "ln0_b", "w_in", "b_in", "qk_conv_w", "qk_conv_b", "s5_lambda_re", "s5_lambda_im",
             "s5_log_dt", "s5_b_re", "s5_b_im", "s5_c_re", "s5_c_im", "s5_d", "s5_w_glu", "m_norm_g", "m_w_out", "w_o",
             "ln1_g", "ln1_b", "w_up", "b_up", "w_down", "ln2_g", "ln2_b"]
    small_names = ["ln0_g", "ln0_b", "b_in", "qk_conv_b", "s5_lambda_re", "s5_lambda_im", "s5_log_dt", "s5_b_re",
                   "s5_b_im", "s5_c_re", "s5_c_im", "s5_d", "m_norm_g", "ln1_g", "ln1_b", "b_up", "ln2_g", "ln2_b",
                   "meta_tokens", "qk_conv_w"]
    res = {}
    for i, n in enumerate(small_names):
        res[n] = (small_g[i], sd[i], sm2[i], sv2[i])
    for i, n in enumerate(names):
        res[n] = tuple(o[None] for o in big_out[i])
    outs = [loss, grad_x]
    for kind in range(4):
        outs += [res[n][kind] for n in order]
    return tuple(outs)
```
